```python
import jax, jax.numpy as jnp
from jax import lax
import numpy as np

D_MODEL = 1024
BATCH = 8
SEQ = 8192
DEPTH = 4

GRID_W = 64
CTX_LEN = 256
HEAD_DIM = 64
NA_HEADS = 8
NA_WIN_H = 8
NA_WIN_W = 16
GQA_Q_HEADS = 8
GQA_KV_HEADS = 2
GQA_REP = GQA_Q_HEADS // GQA_KV_HEADS
NA_WIDTH = NA_HEADS * HEAD_DIM
GQA_Q_WIDTH = GQA_Q_HEADS * HEAD_DIM
GQA_KV_WIDTH = GQA_KV_HEADS * HEAD_DIM
IN_SIZES = (NA_WIDTH, NA_WIDTH, NA_WIDTH, GQA_Q_WIDTH, GQA_KV_WIDTH, GQA_KV_WIDTH, D_MODEL, D_MODEL)
IN_COLS = sum(IN_SIZES)
IN_SPLITS = tuple(int(v) for v in np.cumsum(IN_SIZES)[:-1])
D_FF = -(-(8 * D_MODEL) // (3 * 256)) * 256
Q_BLOCK = 128
ROPE_THETA = 10000.0
EPS = 1e-6
SCALE = HEAD_DIM ** -0.5

kernel_name = "hybrid_natten_gqa_prefix_dit"


def rms_norm(x, g):
    xf = x.astype(jnp.float32)
    y = xf * lax.rsqrt(jnp.mean(xf * xf, axis=-1, keepdims=True) + EPS)
    return (y * g.astype(jnp.float32)).astype(x.dtype)


def modulate(h, shift, scale):
    return h * (1 + scale) + shift


def heads(t, n_heads):
    b, n, _ = t.shape
    return t.reshape(b, n, n_heads, HEAD_DIM).transpose(0, 2, 1, 3)


def merge_heads(t):
    b, h, n, d = t.shape
    return t.transpose(0, 2, 1, 3).reshape(b, n, h * d)


def axial_rope_tables(n_tokens):
    t = jnp.arange(n_tokens)
    row = (t // GRID_W).astype(jnp.float32)
    col = (t % GRID_W).astype(jnp.float32)
    half = HEAD_DIM // 2
    inv = ROPE_THETA ** (-jnp.arange(0, half, 2, dtype=jnp.float32) / half)
    ang = jnp.concatenate([row[:, None] * inv, col[:, None] * inv], axis=-1)
    return jnp.cos(ang), jnp.sin(ang)


def apply_rope(x, cos, sin):
    xf = x.astype(jnp.float32).reshape(x.shape[:-1] + (HEAD_DIM // 2, 2))
    x0, x1 = xf[..., 0], xf[..., 1]
    out = jnp.stack([x0 * cos - x1 * sin, x0 * sin + x1 * cos], axis=-1)
    return out.reshape(x.shape).astype(x.dtype)


def na_window_indices(rows):
    kh = min(NA_WIN_H, rows)
    r = jnp.arange(rows)
    col = jnp.arange(GRID_W)
    r_start = jnp.clip(r - kh // 2, 0, rows - kh)
    row_idx = r_start[:, None] + jnp.arange(kh)[None, :]
    c_start = jnp.clip(col - NA_WIN_W // 2, 0, GRID_W - NA_WIN_W)
    in_win = (col[None, :] >= c_start[:, None]) & (col[None, :] < c_start[:, None] + NA_WIN_W)
    dr_idx = row_idx - r[:, None] + (NA_WIN_H - 1)
    dc = col[None, :] - col[:, None]
    dc_idx = jnp.clip(dc, -(NA_WIN_W - 1), NA_WIN_W - 1) + (NA_WIN_W - 1)
    return row_idx, in_win, dr_idx, dc_idx


def na_bias(rpb, in_win, dr_idx, dc_idx):
    b = rpb.astype(jnp.float32)[:, dr_idx[:, None, :, None], dc_idx[None, :, None, :]]
    return jnp.where(in_win[None, None, :, None, :], b, -jnp.inf)


def na_latent(q, k, v, k_ctx, v_ctx, bias, row_idx):
    b, h, s, d = q.shape
    rows = s // GRID_W
    kh = row_idx.shape[1]
    qg = q.reshape(b, h, rows, GRID_W, d)
    kg = k.reshape(b, h, rows, GRID_W, d)[:, :, row_idx]
    vg = v.reshape(b, h, rows, GRID_W, d)[:, :, row_idx]
    s_win = jnp.einsum('bhrwd,bhrjud->bhrwju', qg, kg).astype(jnp.float32) * SCALE + bias[None]
    s_win = s_win.reshape(b, h, rows, GRID_W, kh * GRID_W)
    s_ctx = jnp.einsum('bhrwd,bhld->bhrwl', qg, k_ctx).astype(jnp.float32) * SCALE
    p = jax.nn.softmax(jnp.concatenate([s_win, s_ctx], axis=-1), axis=-1).astype(v.dtype)
    p_win = p[..., :kh * GRID_W].reshape(b, h, rows, GRID_W, kh, GRID_W)
    p_ctx = p[..., kh * GRID_W:]
    o = jnp.einsum('bhrwju,bhrjud->bhrwd', p_win, vg) + jnp.einsum('bhrwl,bhld->bhrwd', p_ctx, v_ctx)
    return o.reshape(b, h, s, d)


def softmax_attn(q, k, v):
    s = jnp.einsum('bgrqd,bgkd->bgrqk', q, k).astype(jnp.float32) * SCALE
    p = jax.nn.softmax(s, axis=-1).astype(v.dtype)
    return jnp.einsum('bgrqk,bgkd->bgrqd', p, v)


def gqa_latent(q, k_all, v_all):
    b, g, r, s, d = q.shape
    nb = s // Q_BLOCK
    qb = q.reshape(b, g, r, nb, Q_BLOCK, d).transpose(3, 0, 1, 2, 4, 5)
    o = lax.map(lambda qblk: softmax_attn(qblk, k_all, v_all), qb)
    return o.transpose(1, 2, 3, 0, 4, 5).reshape(b, g, r, s, d)


def branch_merge(ya, yb, ga, gb, w_pa, w_pb, w_o):
    merged = jax.nn.sigmoid(ga) * (ya @ w_pa) + jax.nn.sigmoid(gb) * (yb @ w_pb)
    return merged @ w_o


def swiglu(h, w_in, w_out):
    a, u = jnp.split(h @ w_in, 2, axis=-1)
    return (jax.nn.silu(a) * u) @ w_out


def _fwd_setup_inputs(seed: int = 0) -> dict:
    key = jax.random.key(seed)
    ks = jax.random.split(key, 18)
    f32 = jnp.float32
    nrm = lambda k, shp, sc: jax.random.normal(k, shp, f32) * sc
    return {
        "x": nrm(ks[0], (BATCH, SEQ, D_MODEL), 1.0),
        "c": nrm(ks[1], (BATCH, D_MODEL), 1.0),
        "ctx": nrm(ks[2], (BATCH, CTX_LEN, D_MODEL), 1.0),
        "c_ctx": nrm(ks[3], (D_MODEL,), 1.0),
        "w_mod": nrm(ks[4], (DEPTH, D_MODEL, 6 * D_MODEL), D_MODEL ** -0.5),
        "b_mod": nrm(ks[5], (DEPTH, 6 * D_MODEL), 0.01),
        "norm1": 1.0 + nrm(ks[6], (DEPTH, D_MODEL), 0.02),
        "w_in": nrm(ks[7], (DEPTH, D_MODEL, IN_COLS), D_MODEL ** -0.5),
        "na_rpb": nrm(ks[8], (DEPTH, NA_HEADS, 2 * NA_WIN_H - 1, 2 * NA_WIN_W - 1), 0.1),
        "q_gain": 1.0 + nrm(ks[9], (DEPTH, HEAD_DIM), 0.02),
        "k_gain": 1.0 + nrm(ks[10], (DEPTH, HEAD_DIM), 0.02),
        "w_pa": nrm(ks[11], (DEPTH, NA_WIDTH, D_MODEL), NA_WIDTH ** -0.5),
        "w_pb": nrm(ks[12], (DEPTH, GQA_Q_WIDTH, D_MODEL), GQA_Q_WIDTH ** -0.5),
        "w_o": nrm(ks[13], (DEPTH, D_MODEL, D_MODEL), D_MODEL ** -0.5),
        "norm2": 1.0 + nrm(ks[14], (DEPTH, D_MODEL), 0.02),
        "w_ffn_in": nrm(ks[15], (DEPTH, D_MODEL, 2 * D_FF), D_MODEL ** -0.5),
        "w_ffn_out": nrm(ks[16], (DEPTH, D_FF, D_MODEL), D_FF ** -0.5),
        "final_norm": 1.0 + nrm(ks[17], (D_MODEL,), 0.02),
    }


def _fwd_reference(x, c, ctx, c_ctx, w_mod, b_mod, norm1, w_in, na_rpb, q_gain, k_gain, w_pa, w_pb, w_o,
              norm2, w_ffn_in, w_ffn_out, final_norm):
    b, s, _ = x.shape
    n_ctx = ctx.shape[1]
    rows = s // GRID_W
    cos, sin = axial_rope_tables(s)
    row_idx, in_win, dr_idx, dc_idx = na_window_indices(rows)
    silu_c = jax.nn.silu(c)
    silu_cc = jax.nn.silu(c_ctx)

    for l in range(DEPTH):
        last = l == DEPTH - 1
        mod = silu_c @ w_mod[l] + b_mod[l]
        mod_c = silu_cc @ w_mod[l] + b_mod[l]
        sh1, sc1, g1, sh2, sc2, g2 = jnp.split(mod[:, None, :], 6, axis=-1)
        csh1, csc1, cg1, csh2, csc2, cg2 = jnp.split(mod_c, 6)

        h = modulate(rms_norm(x, norm1[l]), sh1, sc1)
        hc = modulate(rms_norm(ctx, norm1[l]), csh1, csc1)
        na_q, na_k, na_v, gq, gk, gv, ga, gb = jnp.split(h @ w_in[l], IN_SPLITS, axis=-1)
        na_qc, na_kc, na_vc, gqc, gkc, gvc, gac, gbc = jnp.split(hc @ w_in[l], IN_SPLITS, axis=-1)

        qa, ka, va = heads(na_q, NA_HEADS), heads(na_k, NA_HEADS), heads(na_v, NA_HEADS)
        qa_c, ka_c, va_c = heads(na_qc, NA_HEADS), heads(na_kc, NA_HEADS), heads(na_vc, NA_HEADS)
        bias = na_bias(na_rpb[l], in_win, dr_idx, dc_idx)
        ya = merge_heads(na_latent(qa, ka, va, ka_c, va_c, bias, row_idx))

        qb = apply_rope(rms_norm(heads(gq, GQA_Q_HEADS), q_gain[l]), cos, sin)
        kb = apply_rope(rms_norm(heads(gk, GQA_KV_HEADS), k_gain[l]), cos, sin)
        vb = heads(gv, GQA_KV_HEADS)
        qb_c = rms_norm(heads(gqc, GQA_Q_HEADS), q_gain[l])
        kb_c = rms_norm(heads(gkc, GQA_KV_HEADS), k_gain[l])
        vb_c = heads(gvc, GQA_KV_HEADS)
        k_all = jnp.concatenate([kb, kb_c], axis=2)
        v_all = jnp.concatenate([vb, vb_c], axis=2)
        ob = gqa_latent(qb.reshape(b, GQA_KV_HEADS, GQA_REP, s, HEAD_DIM), k_all, v_all)
        yb = merge_heads(ob.reshape(b, GQA_Q_HEADS, s, HEAD_DIM))

        x = x + g1 * branch_merge(ya, yb, ga, gb, w_pa[l], w_pb[l], w_o[l])

        if not last:
            ya_c = merge_heads(softmax_attn(qa_c[:, :, None], ka_c, va_c)[:, :, 0])
            ob_c = softmax_attn(qb_c.reshape(b, GQA_KV_HEADS, GQA_REP, n_ctx, HEAD_DIM), kb_c, vb_c)
            yb_c = merge_heads(ob_c.reshape(b, GQA_Q_HEADS, n_ctx, HEAD_DIM))
            ctx = ctx + cg1 * branch_merge(ya_c, yb_c, gac, gbc, w_pa[l], w_pb[l], w_o[l])

        h2 = modulate(rms_norm(x, norm2[l]), sh2, sc2)
        x = x + g2 * swiglu(h2, w_ffn_in[l], w_ffn_out[l])
        if not last:
            hc2 = modulate(rms_norm(ctx, norm2[l]), csh2, csc2)
            ctx = ctx + cg2 * swiglu(hc2, w_ffn_in[l], w_ffn_out[l])

    return rms_norm(x, final_norm)


import jax as _jax
import jax.numpy as _jnp

TWIN_FORMAT = 'train_step'
FWD_PARAMS = ['x', 'c', 'ctx', 'c_ctx', 'w_mod', 'b_mod', 'norm1', 'w_in', 'na_rpb', 'q_gain', 'k_gain', 'w_pa', 'w_pb', 'w_o', 'norm2', 'w_ffn_in', 'w_ffn_out', 'final_norm']
TWIN_WEIGHTS = ['c_ctx', 'w_mod', 'b_mod', 'norm1', 'w_in', 'na_rpb', 'q_gain', 'k_gain', 'w_pa', 'w_pb', 'w_o', 'norm2', 'w_ffn_in', 'w_ffn_out', 'final_norm']
TWIN_DIFF_INPUT = 'x'
TWIN_INPUTS = ['x', 'c', 'ctx', 'c_ctx', 'w_mod', 'b_mod', 'norm1', 'w_in', 'na_rpb', 'q_gain', 'k_gain', 'w_pa', 'w_pb', 'w_o', 'norm2', 'w_ffn_in', 'w_ffn_out', 'final_norm', 'loss_target', 'm_c_ctx', 'm_w_mod', 'm_b_mod', 'm_norm1', 'm_w_in', 'm_na_rpb', 'm_q_gain', 'm_k_gain', 'm_w_pa', 'm_w_pb', 'm_w_o', 'm_norm2', 'm_w_ffn_in', 'm_w_ffn_out', 'm_final_norm', 'v_c_ctx', 'v_w_mod', 'v_b_mod', 'v_norm1', 'v_w_in', 'v_na_rpb', 'v_q_gain', 'v_k_gain', 'v_w_pa', 'v_w_pb', 'v_w_o', 'v_norm2', 'v_w_ffn_in', 'v_w_ffn_out', 'v_final_norm']
TWIN_OUTPUTS = ['loss', 'grad_x', 'grad_c_ctx', 'grad_w_mod', 'grad_b_mod', 'grad_norm1', 'grad_w_in', 'grad_na_rpb', 'grad_q_gain', 'grad_k_gain', 'grad_w_pa', 'grad_w_pb', 'grad_w_o', 'grad_norm2', 'grad_w_ffn_in', 'grad_w_ffn_out', 'grad_final_norm', 'delta_c_ctx', 'delta_w_mod', 'delta_b_mod', 'delta_norm1', 'delta_w_in', 'delta_na_rpb', 'delta_q_gain', 'delta_k_gain', 'delta_w_pa', 'delta_w_pb', 'delta_w_o', 'delta_norm2', 'delta_w_ffn_in', 'delta_w_ffn_out', 'delta_final_norm', 'new_m_c_ctx', 'new_m_w_mod', 'new_m_b_mod', 'new_m_norm1', 'new_m_w_in', 'new_m_na_rpb', 'new_m_q_gain', 'new_m_k_gain', 'new_m_w_pa', 'new_m_w_pb', 'new_m_w_o', 'new_m_norm2', 'new_m_w_ffn_in', 'new_m_w_ffn_out', 'new_m_final_norm', 'new_v_c_ctx', 'new_v_w_mod', 'new_v_b_mod', 'new_v_norm1', 'new_v_w_in', 'new_v_na_rpb', 'new_v_q_gain', 'new_v_k_gain', 'new_v_w_pa', 'new_v_w_pb', 'new_v_w_o', 'new_v_norm2', 'new_v_w_ffn_in', 'new_v_w_ffn_out', 'new_v_final_norm']
TWIN_LEAF_KINDS = {'loss': 'loss', 'grad_x': 'grad_x', 'grad_c_ctx': 'grad_w', 'grad_w_mod': 'grad_w', 'grad_b_mod': 'grad_w', 'grad_norm1': 'grad_w', 'grad_w_in': 'grad_w', 'grad_na_rpb': 'grad_w', 'grad_q_gain': 'grad_w', 'grad_k_gain': 'grad_w', 'grad_w_pa': 'grad_w', 'grad_w_pb': 'grad_w', 'grad_w_o': 'grad_w', 'grad_norm2': 'grad_w', 'grad_w_ffn_in': 'grad_w', 'grad_w_ffn_out': 'grad_w', 'grad_final_norm': 'grad_w', 'delta_c_ctx': 'delta_w', 'delta_w_mod': 'delta_w', 'delta_b_mod': 'delta_w', 'delta_norm1': 'delta_w', 'delta_w_in': 'delta_w', 'delta_na_rpb': 'delta_w', 'delta_q_gain': 'delta_w', 'delta_k_gain': 'delta_w', 'delta_w_pa': 'delta_w', 'delta_w_pb': 'delta_w', 'delta_w_o': 'delta_w', 'delta_norm2': 'delta_w', 'delta_w_ffn_in': 'delta_w', 'delta_w_ffn_out': 'delta_w', 'delta_final_norm': 'delta_w', 'new_m_c_ctx': 'new_m', 'new_m_w_mod': 'new_m', 'new_m_b_mod': 'new_m', 'new_m_norm1': 'new_m', 'new_m_w_in': 'new_m', 'new_m_na_rpb': 'new_m', 'new_m_q_gain': 'new_m', 'new_m_k_gain': 'new_m', 'new_m_w_pa': 'new_m', 'new_m_w_pb': 'new_m', 'new_m_w_o': 'new_m', 'new_m_norm2': 'new_m', 'new_m_w_ffn_in': 'new_m', 'new_m_w_ffn_out': 'new_m', 'new_m_final_norm': 'new_m', 'new_v_c_ctx': 'new_v', 'new_v_w_mod': 'new_v', 'new_v_b_mod': 'new_v', 'new_v_norm1': 'new_v', 'new_v_w_in': 'new_v', 'new_v_na_rpb': 'new_v', 'new_v_q_gain': 'new_v', 'new_v_k_gain': 'new_v', 'new_v_w_pa': 'new_v', 'new_v_w_pb': 'new_v', 'new_v_w_o': 'new_v', 'new_v_norm2': 'new_v', 'new_v_w_ffn_in': 'new_v', 'new_v_w_ffn_out': 'new_v', 'new_v_final_norm': 'new_v'}


def _forward(args):
    return _fwd_reference(*[args[k] for k in FWD_PARAMS])


def _output_shape():
    def fwd():
        inp = _fwd_setup_inputs(0)
        return _fwd_reference(*[inp[k] for k in FWD_PARAMS])
    out = _jax.eval_shape(fwd)
    return out.shape, out.dtype

N_MICROBATCH = 1
ADAM_LR = 0.001
ADAM_B1 = 0.9
ADAM_B2 = 0.999
ADAM_EPS = 1e-08
ADAM_WD = 0.01
ADAM_STEP = 10
PER_EXAMPLE_BATCH_AXIS = {'x': 0, 'c': 0, 'ctx': 0, 'loss_target': 0}
SHARED_INPUTS = []
_WEIGHT_DTYPES = {'c_ctx': _jnp.float32, 'w_mod': _jnp.float32, 'b_mod': _jnp.float32, 'norm1': _jnp.float32, 'w_in': _jnp.float32, 'na_rpb': _jnp.float32, 'q_gain': _jnp.float32, 'k_gain': _jnp.float32, 'w_pa': _jnp.float32, 'w_pb': _jnp.float32, 'w_o': _jnp.float32, 'norm2': _jnp.float32, 'w_ffn_in': _jnp.float32, 'w_ffn_out': _jnp.float32, 'final_norm': _jnp.float32}
MOMENT_SCALE = {'c_ctx': 1.181959e-01, 'w_mod': 8.704274e-02, 'b_mod': 1.522124e-01, 'norm1': 5.416129e-02, 'w_in': 4.246752e-02, 'na_rpb': 4.560813e-03, 'q_gain': 4.281470e-02, 'k_gain': 4.449013e-02, 'w_pa': 4.753237e-02, 'w_pb': 6.189824e-02, 'w_o': 7.313719e-02, 'norm2': 1.583859e-01, 'w_ffn_in': 7.561179e-02, 'w_ffn_out': 1.240205e-01, 'final_norm': 6.487519e+01}


def _to_microbatches(a, axis):
    t = _jnp.moveaxis(a, axis, 0)
    t = t.reshape((N_MICROBATCH, t.shape[0] // N_MICROBATCH) + t.shape[1:])
    return _jnp.moveaxis(t, 1, axis + 1)


def setup_inputs(seed: int = 0) -> dict:
    inp = _fwd_setup_inputs(seed)
    key = _jax.random.fold_in(_jax.random.key(seed), 7919)
    shape, _ = _output_shape()
    out = dict(inp)
    out["loss_target"] = _jax.random.normal(_jax.random.fold_in(key, 0), shape, _jnp.float32)
    for i, name in enumerate(TWIN_WEIGHTS):
        w = inp[name].astype(_jnp.float32)
        if MOMENT_SCALE is None:
            s = _jnp.sqrt(_jnp.mean(_jnp.square(w)) + 1e-30)
        else:
            s = MOMENT_SCALE[name]
        km, kv = _jax.random.split(_jax.random.fold_in(key, i + 1))
        out[name] = w
        out["m_" + name] = s * _jax.random.normal(km, w.shape, _jnp.float32)
        out["v_" + name] = (s * s) * _jax.random.uniform(kv, w.shape, _jnp.float32, 0.5, 1.5)
    if N_MICROBATCH > 1:
        for name, axis in PER_EXAMPLE_BATCH_AXIS.items():
            out[name] = _to_microbatches(out[name], axis)
    return {'x': out['x'], 'c': out['c'], 'ctx': out['ctx'], 'c_ctx': out['c_ctx'], 'w_mod': out['w_mod'], 'b_mod': out['b_mod'], 'norm1': out['norm1'], 'w_in': out['w_in'], 'na_rpb': out['na_rpb'], 'q_gain': out['q_gain'], 'k_gain': out['k_gain'], 'w_pa': out['w_pa'], 'w_pb': out['w_pb'], 'w_o': out['w_o'], 'norm2': out['norm2'], 'w_ffn_in': out['w_ffn_in'], 'w_ffn_out': out['w_ffn_out'], 'final_norm': out['final_norm'], 'loss_target': out['loss_target'], 'm_c_ctx': out['m_c_ctx'], 'm_w_mod': out['m_w_mod'], 'm_b_mod': out['m_b_mod'], 'm_norm1': out['m_norm1'], 'm_w_in': out['m_w_in'], 'm_na_rpb': out['m_na_rpb'], 'm_q_gain': out['m_q_gain'], 'm_k_gain': out['m_k_gain'], 'm_w_pa': out['m_w_pa'], 'm_w_pb': out['m_w_pb'], 'm_w_o': out['m_w_o'], 'm_norm2': out['m_norm2'], 'm_w_ffn_in': out['m_w_ffn_in'], 'm_w_ffn_out': out['m_w_ffn_out'], 'm_final_norm': out['m_final_norm'], 'v_c_ctx': out['v_c_ctx'], 'v_w_mod': out['v_w_mod'], 'v_b_mod': out['v_b_mod'], 'v_norm1': out['v_norm1'], 'v_w_in': out['v_w_in'], 'v_na_rpb': out['v_na_rpb'], 'v_q_gain': out['v_q_gain'], 'v_k_gain': out['v_k_gain'], 'v_w_pa': out['v_w_pa'], 'v_w_pb': out['v_w_pb'], 'v_w_o': out['v_w_o'], 'v_norm2': out['v_norm2'], 'v_w_ffn_in': out['v_w_ffn_in'], 'v_w_ffn_out': out['v_w_ffn_out'], 'v_final_norm': out['v_final_norm']}


def _loss(weights, diff, rest, loss_target):
    with _jax.named_scope("forward"):
        args = {**rest, TWIN_DIFF_INPUT: diff, **{k: w.astype(_WEIGHT_DTYPES[k]) for k, w in weights.items()}}
        y = _forward(args)
    with _jax.named_scope("loss_head"):
        err = _jnp.square(y.astype(_jnp.float32) - loss_target)
        return 0.5 * _jnp.sum(_jnp.mean(err, axis=-1)) if err.ndim else 0.5 * err


def _adamw(w, g, m, v):
    m = ADAM_B1 * m + (1.0 - ADAM_B1) * g
    v = ADAM_B2 * v + (1.0 - ADAM_B2) * _jnp.square(g)
    m_hat = m / (1.0 - ADAM_B1 ** ADAM_STEP)
    v_hat = v / (1.0 - ADAM_B2 ** ADAM_STEP)
    delta = -ADAM_LR * (m_hat / (_jnp.sqrt(v_hat) + ADAM_EPS) + ADAM_WD * w)
    return delta, m, v


def reference(x, c, ctx, c_ctx, w_mod, b_mod, norm1, w_in, na_rpb, q_gain, k_gain, w_pa, w_pb, w_o, norm2, w_ffn_in, w_ffn_out, final_norm, loss_target, m_c_ctx, m_w_mod, m_b_mod, m_norm1, m_w_in, m_na_rpb, m_q_gain, m_k_gain, m_w_pa, m_w_pb, m_w_o, m_norm2, m_w_ffn_in, m_w_ffn_out, m_final_norm, v_c_ctx, v_w_mod, v_b_mod, v_norm1, v_w_in, v_na_rpb, v_q_gain, v_k_gain, v_w_pa, v_w_pb, v_w_o, v_norm2, v_w_ffn_in, v_w_ffn_out, v_final_norm):
    given = dict(x=x, c=c, ctx=ctx, c_ctx=c_ctx, w_mod=w_mod, b_mod=b_mod, norm1=norm1, w_in=w_in, na_rpb=na_rpb, q_gain=q_gain, k_gain=k_gain, w_pa=w_pa, w_pb=w_pb, w_o=w_o, norm2=norm2, w_ffn_in=w_ffn_in, w_ffn_out=w_ffn_out, final_norm=final_norm, loss_target=loss_target, m_c_ctx=m_c_ctx, m_w_mod=m_w_mod, m_b_mod=m_b_mod, m_norm1=m_norm1, m_w_in=m_w_in, m_na_rpb=m_na_rpb, m_q_gain=m_q_gain, m_k_gain=m_k_gain, m_w_pa=m_w_pa, m_w_pb=m_w_pb, m_w_o=m_w_o, m_norm2=m_norm2, m_w_ffn_in=m_w_ffn_in, m_w_ffn_out=m_w_ffn_out, m_final_norm=m_final_norm, v_c_ctx=v_c_ctx, v_w_mod=v_w_mod, v_b_mod=v_b_mod, v_norm1=v_norm1, v_w_in=v_w_in, v_na_rpb=v_na_rpb, v_q_gain=v_q_gain, v_k_gain=v_k_gain, v_w_pa=v_w_pa, v_w_pb=v_w_pb, v_w_o=v_w_o, v_norm2=v_norm2, v_w_ffn_in=v_w_ffn_in, v_w_ffn_out=v_w_ffn_out, v_final_norm=v_final_norm)
    weights = {n: given[n] for n in TWIN_WEIGHTS}
    shared = {n: given[n] for n in SHARED_INPUTS}
    per_example = {n: given[n] for n in ['x', 'c', 'ctx']}
    grad_fn = _jax.value_and_grad(_loss, argnums=(0, 1))

    def one_microbatch(ex, loss_target):
        ex = dict(ex)
        diff = ex.pop(TWIN_DIFF_INPUT)
        return grad_fn(weights, diff, {**shared, **ex}, loss_target)

    if N_MICROBATCH == 1:
        loss, (grad_w, grad_x) = one_microbatch(per_example, given["loss_target"])
    else:
        def body(carry, xs):
            loss_sum, grad_sum = carry
            l_k, (gw_k, gx_k) = one_microbatch(xs[0], xs[1])
            with _jax.named_scope("update"):
                return (loss_sum + l_k, _jax.tree.map(_jnp.add, grad_sum, gw_k)), gx_k

        init = (_jnp.zeros((), _jnp.float32), _jax.tree.map(_jnp.zeros_like, weights))
        (loss, grad_w), grad_x = _jax.lax.scan(body, init, (per_example, given["loss_target"]))
    with _jax.named_scope("update"):
        delta_w, new_m, new_v = {}, {}, {}
        for n in TWIN_WEIGHTS:
            delta_w[n], new_m[n], new_v[n] = _adamw(weights[n], grad_w[n], given["m_" + n], given["v_" + n])
    return (loss, grad_x, *[grad_w[n] for n in TWIN_WEIGHTS], *[delta_w[n] for n in TWIN_WEIGHTS],
            *[new_m[n] for n in TWIN_WEIGHTS], *[new_v[n] for n in TWIN_WEIGHTS])
```

```python
import functools
import math

import numpy as np
import jax
import jax.numpy as jnp
from jax import lax
from jax.experimental import pallas as pl
from jax.experimental.pallas import tpu as pltpu

F32 = jnp.float32
BF16 = jnp.bfloat16
MESH = pl.DeviceIdType.MESH

HEAD_DIM = 64
NA_HEADS = 8
GQA_Q_HEADS = 8
GQA_KV_HEADS = 2
GRID_W = 64
NA_WIN_H = 8
NA_WIN_W = 16
ROPE_THETA = 10000.0
EPS = 1e-6
SCALE = HEAD_DIM ** -0.5
NEG = -1e30

ADAM_LR = 0.001
ADAM_B1 = 0.9
ADAM_B2 = 0.999
ADAM_EPS = 1e-08
ADAM_WD = 0.01
ADAM_STEP = 10

TM = 256
NA_ROWS = 4
NA_KROWS = 12
LANES = 128
VMEM_BIG = 56 * 1024 * 1024


def _cp(sem, vmem=None):
    return pltpu.CompilerParams(dimension_semantics=sem, vmem_limit_bytes=vmem)


def _dot(a, b):
    return jnp.dot(a, b, preferred_element_type=F32)


def _dot_nt(a, b):
    return lax.dot_general(a, b, (((1,), (1,)), ((), ())), preferred_element_type=F32)


def _dot_tn(a, b):
    return lax.dot_general(a, b, (((0,), (0,)), ((), ())), preferred_element_type=F32)


def _colsum8(v):
    tm, d = v.shape
    return v.reshape(tm // 8, 8, d).sum(axis=0)


def _acc(ref, val, first):
    @pl.when(first)
    def _():
        ref[0] = val

    @pl.when(jnp.logical_not(first))
    def _():
        ref[0] += val


def _rms_r(x):
    return lax.rsqrt(jnp.mean(x * x, axis=-1, keepdims=True) + EPS)


def _norm_mod(x, nw, sh, sc):
    return (x * _rms_r(x) * nw) * (1.0 + sc) + sh


def _norm_mod_bwd(dh, x, nw, sc):
    r = _rms_r(x)
    xn = x * r
    dxn = dh * (nw * (1.0 + sc))
    dx = r * (dxn - xn * jnp.mean(dxn * xn, axis=-1, keepdims=True))
    return dx, dh, dh * (xn * nw), dh * ((1.0 + sc) * xn)


def _sigmoid(x):
    return 1.0 / (1.0 + jnp.exp(-x))


def _group_of(i, n_lat_tiles):
    return jnp.where(i >= n_lat_tiles, 1, 0)


def _lane_half(shape):
    return (lax.broadcasted_iota(jnp.int32, shape, len(shape) - 1) % LANES) // HEAD_DIM


def _lane_lt64(shape):
    return _lane_half(shape) == 0


def _roll64(x):
    return pltpu.roll(x, HEAD_DIM, x.ndim - 1)


def norm_mod_call(x, nw, modv, n_lat_tiles, which):
    T, D = x.shape

    def body(x_ref, nw_ref, mod_ref, h_ref):
        sh = mod_ref[0, 3 * which:3 * which + 1, :]
        sc = mod_ref[0, 3 * which + 1:3 * which + 2, :]
        h_ref[...] = _norm_mod(x_ref[...], nw_ref[...], sh, sc).astype(BF16)

    return pl.pallas_call(
        body, name="norm_mod", grid=(T // TM,),
        in_specs=[pl.BlockSpec((TM, D), lambda i: (i, 0)),
                  pl.BlockSpec((1, D), lambda i: (0, 0)),
                  pl.BlockSpec((1, 6, D), lambda i: (_group_of(i, n_lat_tiles), 0, 0))],
        out_specs=pl.BlockSpec((TM, D), lambda i: (i, 0)),
        out_shape=jax.ShapeDtypeStruct((T, D), BF16),
        compiler_params=_cp(("parallel",)),
    )(x, nw, modv)


def qkv_call(h, w_in):
    T, D = h.shape
    N = w_in.shape[1]
    n_na = 3 * NA_HEADS * HEAD_DIM
    n_g = (GQA_Q_HEADS + 2 * GQA_KV_HEADS) * HEAD_DIM
    n_gate = N - n_na - n_g

    def body(h_ref, w_ref, na_ref, g_ref, gate_ref):
        acc = _dot(h_ref[...], w_ref[...])
        na_ref[...] = acc[:, :n_na].astype(BF16)
        g_ref[...] = acc[:, n_na:n_na + n_g].astype(BF16)
        gate_ref[...] = acc[:, n_na + n_g:].astype(BF16)

    return pl.pallas_call(
        body, name="qkv", grid=(T // TM,),
        in_specs=[pl.BlockSpec((TM, D), lambda i: (i, 0)),
                  pl.BlockSpec((D, N), lambda i: (0, 0))],
        out_specs=[pl.BlockSpec((TM, n_na), lambda i: (i, 0)),
                   pl.BlockSpec((TM, n_g), lambda i: (i, 0)),
                   pl.BlockSpec((TM, n_gate), lambda i: (i, 0))],
        out_shape=[jax.ShapeDtypeStruct((T, n_na), BF16),
                   jax.ShapeDtypeStruct((T, n_g), BF16),
                   jax.ShapeDtypeStruct((T, n_gate), BF16)],
        compiler_params=_cp(("parallel",), VMEM_BIG),
    )(h, w_in)


def merge_call(ya, yb, gates, w_pa, w_pb):
    T, Ka = ya.shape
    D = w_pa.shape[1]

    def body(ya_ref, yb_ref, ga_ref, gb_ref, wa_ref, wb_ref, m_ref, pa_ref, pb_ref):
        pa = _dot(ya_ref[...], wa_ref[...])
        pb = _dot(yb_ref[...], wb_ref[...])
        m = _sigmoid(ga_ref[...].astype(F32)) * pa + _sigmoid(gb_ref[...].astype(F32)) * pb
        m_ref[...] = m.astype(BF16)
        pa_ref[...] = pa.astype(BF16)
        pb_ref[...] = pb.astype(BF16)

    row = lambda w: pl.BlockSpec((TM, w), lambda i: (i, 0))
    return pl.pallas_call(
        body, name="merge", grid=(T // TM,),
        in_specs=[row(Ka), row(Ka),
                  pl.BlockSpec((TM, D), lambda i: (i, 0)), pl.BlockSpec((TM, D), lambda i: (i, 1)),
                  pl.BlockSpec((Ka, D), lambda i: (0, 0)), pl.BlockSpec((Ka, D), lambda i: (0, 0))],
        out_specs=[row(D), row(D), row(D)],
        out_shape=[jax.ShapeDtypeStruct((T, D), BF16)] * 3,
        compiler_params=_cp(("parallel",)),
    )(ya, yb, gates, gates, w_pa, w_pb)


def wo_call(merged, w_o, x, modv, nw2, n_lat_tiles):
    T, D = x.shape

    def body(m_ref, w_ref, x_ref, mod_ref, nw_ref, x1_ref, o_ref, h2_ref):
        o = _dot(m_ref[...], w_ref[...])
        x1 = x_ref[...] + mod_ref[0, 2:3, :] * o
        x1_ref[...] = x1
        o_ref[...] = o.astype(BF16)
        h2_ref[...] = _norm_mod(x1, nw_ref[...], mod_ref[0, 3:4, :], mod_ref[0, 4:5, :]).astype(BF16)

    row = pl.BlockSpec((TM, D), lambda i: (i, 0))
    return pl.pallas_call(
        body, name="wo", grid=(T // TM,),
        in_specs=[row, pl.BlockSpec((D, D), lambda i: (0, 0)), row,
                  pl.BlockSpec((1, 6, D), lambda i: (_group_of(i, n_lat_tiles), 0, 0)),
                  pl.BlockSpec((1, D), lambda i: (0, 0))],
        out_specs=[row, row, row],
        out_shape=[jax.ShapeDtypeStruct((T, D), F32), jax.ShapeDtypeStruct((T, D), BF16),
                   jax.ShapeDtypeStruct((T, D), BF16)],
        compiler_params=_cp(("parallel",)),
    )(merged, w_o, x, modv, nw2)


def ffn_in_call(h2, w_ffn_in, n_col_tiles=2):
    T, D = h2.shape
    F = w_ffn_in.shape[1] // 2
    tn = F // n_col_tiles

    def body(h_ref, wa_ref, wu_ref, act_ref, a_ref, u_ref):
        h = h_ref[...]
        a = _dot(h, wa_ref[...])
        u = _dot(h, wu_ref[...])
        act_ref[...] = (a * _sigmoid(a) * u).astype(BF16)
        a_ref[...] = a.astype(BF16)
        u_ref[...] = u.astype(BF16)

    blk = pl.BlockSpec((TM, tn), lambda j, i: (i, j))
    return pl.pallas_call(
        body, name="ffn_in", grid=(n_col_tiles, T // TM),
        in_specs=[pl.BlockSpec((TM, D), lambda j, i: (i, 0)),
                  pl.BlockSpec((D, tn), lambda j, i: (0, j)),
                  pl.BlockSpec((D, tn), lambda j, i: (0, n_col_tiles + j))],
        out_specs=[blk, blk, blk],
        out_shape=[jax.ShapeDtypeStruct((T, F), BF16)] * 3,
        compiler_params=_cp(("parallel", "parallel"), VMEM_BIG),
    )(h2, w_ffn_in, w_ffn_in)


def ffn_out_call(act, w_ffn_out, x1, modv, n_lat_tiles, next_nw=None, next_modv=None):
    T, D = x1.shape
    F = act.shape[1]
    with_next = next_nw is not None

    def body(*refs):
        if with_next:
            a_ref, w_ref, x_ref, mod_ref, nw_ref, nmod_ref, x2_ref, f_ref, hn_ref = refs
        else:
            a_ref, w_ref, x_ref, mod_ref, x2_ref, f_ref = refs
        f = _dot(a_ref[...], w_ref[...])
        x2 = x_ref[...] + mod_ref[0, 5:6, :] * f
        x2_ref[...] = x2
        f_ref[...] = f.astype(BF16)
        if with_next:
            hn_ref[...] = _norm_mod(x2, nw_ref[...], nmod_ref[0, 0:1, :], nmod_ref[0, 1:2, :]).astype(BF16)

    row = pl.BlockSpec((TM, D), lambda i: (i, 0))
    modspec = pl.BlockSpec((1, 6, D), lambda i: (_group_of(i, n_lat_tiles), 0, 0))
    in_specs = [pl.BlockSpec((TM, F), lambda i: (i, 0)), pl.BlockSpec((F, D), lambda i: (0, 0)), row, modspec]
    args = [act, w_ffn_out, x1, modv]
    out_specs = [row, row]
    out_shape = [jax.ShapeDtypeStruct((T, D), F32), jax.ShapeDtypeStruct((T, D), BF16)]
    if with_next:
        in_specs += [pl.BlockSpec((1, D), lambda i: (0, 0)), modspec]
        args += [next_nw, next_modv]
        out_specs.append(row)
        out_shape.append(jax.ShapeDtypeStruct((T, D), BF16))
    return pl.pallas_call(
        body, name="ffn_out", grid=(T // TM,), in_specs=in_specs, out_specs=out_specs, out_shape=out_shape,
        compiler_params=_cp(("parallel",), VMEM_BIG),
    )(*args)


def loss_call(x, nw, target, n_lat_tiles):
    T, D = x.shape

    def body(x_ref, nw_ref, t_ref, dx_ref, dw_ref, ls_ref):
        i = pl.program_id(0)

        @pl.when(i < n_lat_tiles)
        def _():
            xv = x_ref[...]
            nw_v = nw_ref[...]
            r = _rms_r(xv)
            xn = xv * r
            e = xn * nw_v - t_ref[...]
            dy = e * (1.0 / D)
            dxn = dy * nw_v
            dx_ref[...] = r * (dxn - xn * jnp.mean(dxn * xn, axis=-1, keepdims=True))
            _acc(dw_ref, _colsum8(dy * xn), i == 0)
            _acc(ls_ref, _colsum8(e * e * (0.5 / D)), i == 0)

        @pl.when(i >= n_lat_tiles)
        def _():
            dx_ref[...] = jnp.zeros((TM, D), F32)

    stat = pl.BlockSpec((1, 8, D), lambda i: (0, 0, 0))
    return pl.pallas_call(
        body, name="loss", grid=(T // TM,),
        in_specs=[pl.BlockSpec((TM, D), lambda i: (i, 0)), pl.BlockSpec((1, D), lambda i: (0, 0)),
                  pl.BlockSpec((TM, D), lambda i: (jnp.minimum(i, n_lat_tiles - 1), 0))],
        out_specs=[pl.BlockSpec((TM, D), lambda i: (i, 0)), stat, stat],
        out_shape=[jax.ShapeDtypeStruct((T, D), F32), jax.ShapeDtypeStruct((1, 8, D), F32),
                   jax.ShapeDtypeStruct((1, 8, D), F32)],
        compiler_params=_cp(("arbitrary",)),
    )(x, nw, target)


def dact_call(dx2, modv, fsave, w_ffn_out, a, u, n_lat_tiles):
    T, D = dx2.shape
    F = a.shape[1]

    def body(dx_ref, mod_ref, f_ref, w_ref, a_ref, u_ref, da_ref, du_ref, df_ref, dg_ref):
        i = pl.program_id(0)
        dx = dx_ref[...]
        df = (dx * mod_ref[0, 5:6, :]).astype(BF16)
        df_ref[...] = df
        dact = _dot_nt(df, w_ref[...])
        av = a_ref[...].astype(F32)
        uv = u_ref[...].astype(F32)
        sg = _sigmoid(av)
        da_ref[...] = (dact * uv * (sg * (1.0 + av * (1.0 - sg)))).astype(BF16)
        du_ref[...] = (dact * (av * sg)).astype(BF16)
        _acc(dg_ref, _colsum8(dx * f_ref[...].astype(F32)), (i == 0) | (i == n_lat_tiles))

    row = pl.BlockSpec((TM, D), lambda i: (i, 0))
    wide = pl.BlockSpec((TM, F), lambda i: (i, 0))
    grp = lambda i: (_group_of(i, n_lat_tiles), 0, 0)
    return pl.pallas_call(
        body, name="dact", grid=(T // TM,),
        in_specs=[row, pl.BlockSpec((1, 6, D), grp), row, pl.BlockSpec((F, D), lambda i: (0, 0)), wide, wide],
        out_specs=[wide, wide, row, pl.BlockSpec((1, 8, D), grp)],
        out_shape=[jax.ShapeDtypeStruct((T, F), BF16), jax.ShapeDtypeStruct((T, F), BF16),
                   jax.ShapeDtypeStruct((T, D), BF16), jax.ShapeDtypeStruct((2, 8, D), F32)],
        compiler_params=_cp(("arbitrary",), VMEM_BIG),
    )(dx2, modv, fsave, w_ffn_out, a, u)


def _norm_bwd_tail(dh, x_ref, mod_ref, nw_ref, dres_ref, dx_ref, s_sh, s_sc, s_w, which, first):
    sc = mod_ref[0, 3 * which + 1:3 * which + 2, :]
    dxn, t_sh, t_sc, t_w = _norm_mod_bwd(dh, x_ref[...], nw_ref[...], sc)
    dx_ref[...] = dres_ref[...] + dxn
    _acc(s_sh, _colsum8(t_sh), first)
    _acc(s_sc, _colsum8(t_sc), first)
    _acc(s_w, _colsum8(t_w), first)


def dh2_call(da, du, w_ffn_in, x1, modv, nw2, dx2, n_lat_tiles):
    T, D = x1.shape
    F = da.shape[1]

    def body(da_ref, du_ref, wa_ref, wu_ref, x_ref, mod_ref, nw_ref, dres_ref, dx_ref, s_sh, s_sc, s_w):
        i = pl.program_id(0)
        dh = _dot_nt(da_ref[...], wa_ref[...]) + _dot_nt(du_ref[...], wu_ref[...])
        _norm_bwd_tail(dh, x_ref, mod_ref, nw_ref, dres_ref, dx_ref, s_sh, s_sc, s_w, 1,
                       (i == 0) | (i == n_lat_tiles))

    row = pl.BlockSpec((TM, D), lambda i: (i, 0))
    wide = pl.BlockSpec((TM, F), lambda i: (i, 0))
    grp = lambda i: (_group_of(i, n_lat_tiles), 0, 0)
    stat = pl.BlockSpec((1, 8, D), grp)
    return pl.pallas_call(
        body, name="dh2", grid=(T // TM,),
        in_specs=[wide, wide, pl.BlockSpec((D, F), lambda i: (0, 0)), pl.BlockSpec((D, F), lambda i: (0, 1)),
                  row, pl.BlockSpec((1, 6, D), grp), pl.BlockSpec((1, D), lambda i: (0, 0)), row],
        out_specs=[row, stat, stat, stat],
        out_shape=[jax.ShapeDtypeStruct((T, D), F32)] + [jax.ShapeDtypeStruct((2, 8, D), F32)] * 3,
        compiler_params=_cp(("arbitrary",), VMEM_BIG),
    )(da, du, w_ffn_in, w_ffn_in, x1, modv, nw2, dx2)


def dmerged_call(dx1, modv, osave, w_o, pa, pb, gates, n_lat_tiles):
    T, D = dx1.shape

    def body(dx_ref, mod_ref, o_ref, w_ref, pa_ref, pb_ref, ga_ref, gb_ref,
             dpa_ref, dpb_ref, dgate_ref, do_ref, dg_ref):
        i = pl.program_id(0)
        dx = dx_ref[...]
        do = (dx * mod_ref[0, 2:3, :]).astype(BF16)
        do_ref[...] = do
        dm = _dot_nt(do, w_ref[...])
        sa = _sigmoid(ga_ref[...].astype(F32))
        sb = _sigmoid(gb_ref[...].astype(F32))
        dpa_ref[...] = (dm * sa).astype(BF16)
        dpb_ref[...] = (dm * sb).astype(BF16)
        dgate_ref[:, :D] = (dm * pa_ref[...].astype(F32) * (sa * (1.0 - sa))).astype(BF16)
        dgate_ref[:, D:] = (dm * pb_ref[...].astype(F32) * (sb * (1.0 - sb))).astype(BF16)
        _acc(dg_ref, _colsum8(dx * o_ref[...].astype(F32)), (i == 0) | (i == n_lat_tiles))

    row = pl.BlockSpec((TM, D), lambda i: (i, 0))
    grp = lambda i: (_group_of(i, n_lat_tiles), 0, 0)
    return pl.pallas_call(
        body, name="dmerged", grid=(T // TM,),
        in_specs=[row, pl.BlockSpec((1, 6, D), grp), row, pl.BlockSpec((D, D), lambda i: (0, 0)), row, row,
                  pl.BlockSpec((TM, D), lambda i: (i, 0)), pl.BlockSpec((TM, D), lambda i: (i, 1))],
        out_specs=[row, row, pl.BlockSpec((TM, 2 * D), lambda i: (i, 0)), row, pl.BlockSpec((1, 8, D), grp)],
        out_shape=[jax.ShapeDtypeStruct((T, D), BF16), jax.ShapeDtypeStruct((T, D), BF16),
                   jax.ShapeDtypeStruct((T, 2 * D), BF16), jax.ShapeDtypeStruct((T, D), BF16),
                   jax.ShapeDtypeStruct((2, 8, D), F32)],
        compiler_params=_cp(("arbitrary",)),
    )(dx1, modv, osave, w_o, pa, pb, gates, gates)


def dy_call(dpa, dpb, w_pa, w_pb):
    T, D = dpa.shape
    K = w_pa.shape[0]

    def body(a_ref, b_ref, wa_ref, wb_ref, ya_ref, yb_ref):
        ya_ref[...] = _dot_nt(a_ref[...], wa_ref[...]).astype(BF16)
        yb_ref[...] = _dot_nt(b_ref[...], wb_ref[...]).astype(BF16)

    row = pl.BlockSpec((TM, D), lambda i: (i, 0))
    w = pl.BlockSpec((K, D), lambda i: (0, 0))
    out = pl.BlockSpec((TM, K), lambda i: (i, 0))
    return pl.pallas_call(
        body, name="dy", grid=(T // TM,), in_specs=[row, row, w, w], out_specs=[out, out],
        out_shape=[jax.ShapeDtypeStruct((T, K), BF16)] * 2,
        compiler_params=_cp(("parallel",)),
    )(dpa, dpb, w_pa, w_pb)


def dh_call(dna, dg, dgates, w_in, x, modv, nw1, dx1, n_lat_tiles):
    T, D = x.shape
    n1, n2, n3 = dna.shape[1], dg.shape[1], dgates.shape[1]
    N = n1 + n2 + n3

    def body(a_ref, b_ref, c_ref, w_ref, x_ref, mod_ref, nw_ref, dres_ref, dx_ref, s_sh, s_sc, s_w):
        i = pl.program_id(0)
        dh = (_dot_nt(a_ref[...], w_ref[:, :n1]) + _dot_nt(b_ref[...], w_ref[:, n1:n1 + n2])
              + _dot_nt(c_ref[...], w_ref[:, n1 + n2:]))
        _norm_bwd_tail(dh, x_ref, mod_ref, nw_ref, dres_ref, dx_ref, s_sh, s_sc, s_w, 0,
                       (i == 0) | (i == n_lat_tiles))

    row = pl.BlockSpec((TM, D), lambda i: (i, 0))
    grp = lambda i: (_group_of(i, n_lat_tiles), 0, 0)
    stat = pl.BlockSpec((1, 8, D), grp)
    return pl.pallas_call(
        body, name="dh", grid=(T // TM,),
        in_specs=[pl.BlockSpec((TM, n1), lambda i: (i, 0)), pl.BlockSpec((TM, n2), lambda i: (i, 0)),
                  pl.BlockSpec((TM, n3), lambda i: (i, 0)), pl.BlockSpec((D, N), lambda i: (0, 0)),
                  row, pl.BlockSpec((1, 6, D), grp), pl.BlockSpec((1, D), lambda i: (0, 0)), row],
        out_specs=[row, stat, stat, stat],
        out_shape=[jax.ShapeDtypeStruct((T, D), F32)] + [jax.ShapeDtypeStruct((2, 8, D), F32)] * 3,
        compiler_params=_cp(("arbitrary",), VMEM_BIG),
    )(dna, dg, dgates, w_in, x, modv, nw1, dx1)


def _pick_tile(n, cap, mult):
    best = None
    for d in range(mult, min(n, cap) + 1, mult):
        if n % d == 0:
            best = d
    assert best is not None, (n, cap, mult)
    return best


def wgrad_call(a, b, name):
    T, K = a.shape
    N = b.shape[1]
    tt = _pick_tile(T, 1536, 128)
    tk = _pick_tile(K, 512, 128)
    tn = N if N <= 2304 else _pick_tile(N, 2304, 128)
    nt = T // tt

    def body(a_ref, b_ref, o_ref, acc_ref):
        t = pl.program_id(2)
        p = _dot_tn(a_ref[...], b_ref[...])

        @pl.when(t == 0)
        def _():
            acc_ref[...] = p

        @pl.when(t > 0)
        def _():
            acc_ref[...] += p

        @pl.when(t == nt - 1)
        def _():
            o_ref[...] = acc_ref[...].astype(BF16)

    return pl.pallas_call(
        body, name=name, grid=(K // tk, N // tn, nt),
        in_specs=[pl.BlockSpec((tt, tk), lambda i, j, t: (t, i)), pl.BlockSpec((tt, tn), lambda i, j, t: (t, j))],
        out_specs=pl.BlockSpec((tk, tn), lambda i, j, t: (i, j)),
        out_shape=jax.ShapeDtypeStruct((K, N), BF16),
        scratch_shapes=[pltpu.VMEM((tk, tn), F32)],
        compiler_params=_cp(("parallel", "parallel", "arbitrary"), VMEM_BIG),
    )(a, b)


def _head_mean(v):
    n = v.shape[-1]
    r = lax.broadcasted_iota(jnp.int32, (n, n), 0) // HEAD_DIM
    c = lax.broadcasted_iota(jnp.int32, (n, n), 1) // HEAD_DIM
    bd = jnp.where(r == c, 1.0 / HEAD_DIM, 0.0).astype(F32)
    return jnp.dot(v, bd, precision=lax.Precision.HIGHEST, preferred_element_type=F32)


def _swap_pairs(x):
    n = x.shape[-1]
    even = lax.broadcasted_iota(jnp.int32, x.shape, x.ndim - 1) % 2 == 0
    return jnp.where(even, pltpu.roll(x, n - 1, x.ndim - 1), pltpu.roll(x, 1, x.ndim - 1))


def _tile_lanes(t, n):
    return jnp.tile(t, (1, n // t.shape[-1]))


def gqa_prep_call(gqkv, qgain, kgain, cos_t, sin_t):
    T = gqkv.shape[0]
    nq = GQA_Q_HEADS * HEAD_DIM
    nk = GQA_KV_HEADS * HEAD_DIM
    rep = GQA_Q_HEADS // GQA_KV_HEADS

    def body(g_ref, qg_ref, kg_ref, c_ref, s_ref, qp_ref, kp_ref):
        cos_v, sin_v = c_ref[...], s_ref[...]
        xq = g_ref[:, :nq].astype(F32)
        yq = xq * lax.rsqrt(_head_mean(xq * xq) + EPS) * qg_ref[...]
        yq = (yq * _tile_lanes(cos_v, nq) + _swap_pairs(yq) * _tile_lanes(sin_v, nq)) * SCALE
        lo = _lane_lt64((TM, LANES))
        for p in range(GQA_Q_HEADS // 2):
            g = (2 * p) // rep
            chunk = yq[:, LANES * p:LANES * (p + 1)]
            rolled = _roll64(chunk)
            if g == 0:
                first, second = jnp.where(lo, chunk, 0.0), jnp.where(lo, rolled, 0.0)
            else:
                first, second = jnp.where(lo, 0.0, rolled), jnp.where(lo, 0.0, chunk)
            qp_ref[:, 2 * LANES * p:2 * LANES * p + LANES] = first.astype(BF16)
            qp_ref[:, 2 * LANES * p + LANES:2 * LANES * (p + 1)] = second.astype(BF16)
        xk = g_ref[:, nq:nq + nk].astype(F32)
        yk = xk * lax.rsqrt(_head_mean(xk * xk) + EPS) * kg_ref[...]
        kp_ref[...] = (yk * cos_v + _swap_pairs(yk) * sin_v).astype(BF16)

    return pl.pallas_call(
        body, name="gqa_prep", grid=(T // TM,),
        in_specs=[pl.BlockSpec((TM, gqkv.shape[1]), lambda i: (i, 0)),
                  pl.BlockSpec((1, nq), lambda i: (0, 0)), pl.BlockSpec((1, nk), lambda i: (0, 0)),
                  pl.BlockSpec((TM, nk), lambda i: (i, 0)), pl.BlockSpec((TM, nk), lambda i: (i, 0))],
        out_specs=[pl.BlockSpec((TM, GQA_Q_HEADS * LANES), lambda i: (i, 0)), pl.BlockSpec((TM, nk), lambda i: (i, 0))],
        out_shape=[jax.ShapeDtypeStruct((T, GQA_Q_HEADS * LANES), BF16), jax.ShapeDtypeStruct((T, nk), BF16)],
        compiler_params=_cp(("parallel",)),
    )(gqkv, qgain, kgain, cos_t, sin_t)


def gqa_prep_bwd_call(gqkv, qgain, kgain, cos_t, sin_t, dqp, dkp, dv):
    T = gqkv.shape[0]
    nq = GQA_Q_HEADS * HEAD_DIM
    nk = GQA_KV_HEADS * HEAD_DIM
    rep = GQA_Q_HEADS // GQA_KV_HEADS

    def norm_rope_bwd(x, gain, dy, cos_v, sin_v):
        dyn = dy * cos_v - _swap_pairs(dy) * sin_v
        r = lax.rsqrt(_head_mean(x * x) + EPS)
        xn = x * r
        dxn = dyn * gain
        dx = r * (dxn - xn * _head_mean(dxn * xn))
        return dx, dyn * xn

    def body(g_ref, qg_ref, kg_ref, c_ref, s_ref, dqp_ref, dkp_ref, dv_ref, dg_ref, sq_ref, sk_ref):
        i = pl.program_id(0)
        cos_v, sin_v = c_ref[...], s_ref[...]
        lo = _lane_lt64((TM, LANES))
        chunks = []
        for p in range(GQA_Q_HEADS // 2):
            g = (2 * p) // rep
            da = dqp_ref[:, 2 * LANES * p:2 * LANES * p + LANES]
            db = dqp_ref[:, 2 * LANES * p + LANES:2 * LANES * (p + 1)]
            if g == 0:
                chunks.append(jnp.where(lo, da, _roll64(db)))
            else:
                chunks.append(jnp.where(lo, _roll64(da), db))
        dyq = jnp.concatenate(chunks, axis=1) * SCALE
        dxq, gq_term = norm_rope_bwd(g_ref[:, :nq].astype(F32), qg_ref[...], dyq,
                                     _tile_lanes(cos_v, nq), _tile_lanes(sin_v, nq))
        dxk, gk_term = norm_rope_bwd(g_ref[:, nq:nq + nk].astype(F32), kg_ref[...], dkp_ref[...], cos_v, sin_v)
        dg_ref[:, :nq] = dxq.astype(BF16)
        dg_ref[:, nq:nq + nk] = dxk.astype(BF16)
        dg_ref[:, nq + nk:] = dv_ref[...].astype(BF16)
        _acc(sq_ref, _colsum8(gq_term), i == 0)
        _acc(sk_ref, _colsum8(gk_term), i == 0)

    W = gqkv.shape[1]
    return pl.pallas_call(
        body, name="gqa_prep_bwd", grid=(T // TM,),
        in_specs=[pl.BlockSpec((TM, W), lambda i: (i, 0)),
                  pl.BlockSpec((1, nq), lambda i: (0, 0)), pl.BlockSpec((1, nk), lambda i: (0, 0)),
                  pl.BlockSpec((TM, nk), lambda i: (i, 0)), pl.BlockSpec((TM, nk), lambda i: (i, 0)),
                  pl.BlockSpec((TM, GQA_Q_HEADS * LANES), lambda i: (i, 0)),
                  pl.BlockSpec((TM, nk), lambda i: (i, 0)), pl.BlockSpec((TM, nk), lambda i: (i, 0))],
        out_specs=[pl.BlockSpec((TM, W), lambda i: (i, 0)),
                   pl.BlockSpec((1, 8, nq), lambda i: (0, 0, 0)), pl.BlockSpec((1, 8, nk), lambda i: (0, 0, 0))],
        out_shape=[jax.ShapeDtypeStruct((T, W), BF16), jax.ShapeDtypeStruct((1, 8, nq), F32),
                   jax.ShapeDtypeStruct((1, 8, nk), F32)],
        compiler_params=_cp(("arbitrary",)),
    )(gqkv, qgain, kgain, cos_t, sin_t, dqp, dkp, dv)


GQ_TQ = 256
GQ_TK = 256


def _gqa_place(chunk, half, g):
    gv = jnp.broadcast_to(g, chunk.shape)
    placed = jnp.where(gv == half, chunk, _roll64(chunk))
    return jnp.where(_lane_half(chunk.shape) == gv, placed, jnp.zeros_like(placed))


def gqa_fwd_call(qp, kp, gqkv, n_lat):
    T = qp.shape[0]
    rep = GQA_Q_HEADS // GQA_KV_HEADS
    nq = T // GQ_TQ
    nk = T // GQ_TK
    n_lat_q = n_lat // GQ_TQ
    n_lat_k = n_lat // GQ_TK
    v_blk = (GQA_Q_HEADS + GQA_KV_HEADS) * HEAD_DIM // LANES
    R = rep * GQ_TQ

    def body(q_ref, k_ref, v_ref, y_ref, lse_ref, m_sc, l_sc, acc_sc):
        g = pl.program_id(0)
        qb = pl.program_id(1)
        qs = jnp.concatenate([q_ref[:, LANES * i:LANES * (i + 1)] for i in range(rep)], axis=0)
        m_sc[...] = jnp.full((R, 1), NEG, F32)
        l_sc[...] = jnp.zeros((R, 1), F32)
        acc_sc[...] = jnp.zeros((R, LANES), F32)
        start = jnp.where(qb >= n_lat_q, n_lat_k, 0)

        def step(j, carry):
            off = pl.multiple_of(j * GQ_TK, GQ_TK)
            kc = k_ref[pl.ds(off, GQ_TK), :]
            vc = v_ref[pl.ds(off, GQ_TK), :]
            s = _dot_nt(qs, kc)
            m_old = m_sc[...]
            m_new = jnp.maximum(m_old, jnp.max(s, axis=-1, keepdims=True))
            alpha = jnp.exp(m_old - m_new)
            p = jnp.exp(s - m_new)
            l_sc[...] = alpha * l_sc[...] + jnp.sum(p, axis=-1, keepdims=True)
            acc_sc[...] = alpha * acc_sc[...] + _dot(p.astype(BF16), vc)
            m_sc[...] = m_new
            return carry

        lax.fori_loop(start, nk, step, 0)
        l = l_sc[...]
        o = acc_sc[...] / l
        lse = m_sc[...] + jnp.log(l)
        lo = _lane_lt64((GQ_TQ, LANES))
        g0 = jnp.broadcast_to(g, (GQ_TQ, LANES)) == 0
        for c2 in range(rep // 2):
            oa = o[(2 * c2) * GQ_TQ:(2 * c2 + 1) * GQ_TQ]
            ob = o[(2 * c2 + 1) * GQ_TQ:(2 * c2 + 2) * GQ_TQ]
            left = jnp.where(g0, oa, _roll64(oa))
            right = jnp.where(g0, _roll64(ob), ob)
            y_ref[:, LANES * c2:LANES * (c2 + 1)] = jnp.where(lo, left, right).astype(BF16)
        for i in range(rep):
            lse_ref[i] = jnp.broadcast_to(lse[i * GQ_TQ:(i + 1) * GQ_TQ], (GQ_TQ, LANES))

    return pl.pallas_call(
        body, name="gqa_fwd", grid=(GQA_KV_HEADS, nq),
        in_specs=[pl.BlockSpec((GQ_TQ, rep * LANES), lambda g, i: (i, g)),
                  pl.BlockSpec((T, LANES), lambda g, i: (0, 0)),
                  pl.BlockSpec((T, LANES), lambda g, i: (0, v_blk))],
        out_specs=[pl.BlockSpec((GQ_TQ, rep * HEAD_DIM), lambda g, i: (i, g)),
                   pl.BlockSpec((rep, GQ_TQ, LANES), lambda g, i: (g, i, 0))],
        out_shape=[jax.ShapeDtypeStruct((T, GQA_Q_HEADS * HEAD_DIM), BF16),
                   jax.ShapeDtypeStruct((GQA_Q_HEADS, T, LANES), F32)],
        scratch_shapes=[pltpu.VMEM((R, 1), F32), pltpu.VMEM((R, 1), F32), pltpu.VMEM((R, LANES), F32)],
        compiler_params=_cp(("parallel", "parallel"), VMEM_BIG),
    )(qp, kp, gqkv)


def gqa_bwd_call(qp, kp, gqkv, yb, dyb, lse, n_lat):
    T = qp.shape[0]
    rep = GQA_Q_HEADS // GQA_KV_HEADS
    nq = T // GQ_TQ
    nk = T // GQ_TK
    n_lat_q = n_lat // GQ_TQ
    n_lat_k = n_lat // GQ_TK
    v_blk = (GQA_Q_HEADS + GQA_KV_HEADS) * HEAD_DIM // LANES
    R = rep * GQ_TQ

    def body(q_ref, k_ref, v_ref, y_ref, dy_ref, lse_ref, dq_ref, dk_ref, dv_ref, dq_sc):
        g = pl.program_id(0)
        qb = pl.program_id(1)

        @pl.when((g == 0) & (qb == 0))
        def _():
            dk_ref[...] = jnp.zeros((T, LANES), F32)
            dv_ref[...] = jnp.zeros((T, LANES), F32)

        qs = jnp.concatenate([q_ref[:, LANES * i:LANES * (i + 1)] for i in range(rep)], axis=0)
        dos, deltas, lses = [], [], []
        for i in range(rep):
            c2, half = i // 2, i % 2
            dch = _gqa_place(dy_ref[:, LANES * c2:LANES * (c2 + 1)].astype(F32), half, g)
            ych = _gqa_place(y_ref[:, LANES * c2:LANES * (c2 + 1)].astype(F32), half, g)
            dos.append(dch.astype(BF16))
            deltas.append(jnp.sum(dch * ych, axis=-1, keepdims=True))
            lses.append(lse_ref[i][:, 0:1])
        do_s = jnp.concatenate(dos, axis=0)
        delta = jnp.concatenate(deltas, axis=0)
        lse_v = jnp.concatenate(lses, axis=0)
        dq_sc[...] = jnp.zeros((R, LANES), F32)
        start = jnp.where(qb >= n_lat_q, n_lat_k, 0)

        def step(j, carry):
            off = pl.multiple_of(j * GQ_TK, GQ_TK)
            kc = k_ref[pl.ds(off, GQ_TK), :]
            vc = v_ref[pl.ds(off, GQ_TK), :]
            p = jnp.exp(_dot_nt(qs, kc) - lse_v)
            dp = _dot_nt(do_s, vc)
            ds = (p * (dp - delta)).astype(BF16)
            dv_ref[pl.ds(off, GQ_TK), :] += _dot_tn(p.astype(BF16), do_s)
            dk_ref[pl.ds(off, GQ_TK), :] += _dot_tn(ds, qs)
            dq_sc[...] += _dot(ds, kc)
            return carry

        lax.fori_loop(start, nk, step, 0)
        for i in range(rep):
            dq_ref[:, LANES * i:LANES * (i + 1)] = dq_sc[i * GQ_TQ:(i + 1) * GQ_TQ, :]

    whole = pl.BlockSpec((T, LANES), lambda g, i: (0, 0))
    return pl.pallas_call(
        body, name="gqa_bwd", grid=(GQA_KV_HEADS, nq),
        in_specs=[pl.BlockSpec((GQ_TQ, rep * LANES), lambda g, i: (i, g)),
                  whole,
                  pl.BlockSpec((T, LANES), lambda g, i: (0, v_blk)),
                  pl.BlockSpec((GQ_TQ, rep * HEAD_DIM), lambda g, i: (i, g)),
                  pl.BlockSpec((GQ_TQ, rep * HEAD_DIM), lambda g, i: (i, g)),
                  pl.BlockSpec((rep, GQ_TQ, LANES), lambda g, i: (g, i, 0))],
        out_specs=[pl.BlockSpec((GQ_TQ, rep * LANES), lambda g, i: (i, g)), whole, whole],
        out_shape=[jax.ShapeDtypeStruct((T, GQA_Q_HEADS * LANES), F32),
                   jax.ShapeDtypeStruct((T, LANES), F32), jax.ShapeDtypeStruct((T, LANES), F32)],
        scratch_shapes=[pltpu.VMEM((R, LANES), F32)],
        compiler_params=_cp(("arbitrary", "arbitrary"), VMEM_BIG),
    )(qp, kp, gqkv, yb, dyb, lse)


def _na_tables(rows):
    kh = min(NA_WIN_H, rows)
    assert kh == NA_WIN_H and rows >= NA_KROWS and rows % NA_ROWS == 0
    a = np.zeros((3, NA_ROWS, NA_KROWS, 2 * NA_WIN_H - 1), np.float32)
    for v, r0 in enumerate((0, NA_ROWS, rows - NA_ROWS)):
        ks = min(max(r0 - NA_WIN_H // 2, 0), rows - NA_KROWS)
        for rq in range(NA_ROWS):
            r = r0 + rq
            rs = min(max(r - kh // 2, 0), rows - kh)
            for rk in range(NA_KROWS):
                kr = ks + rk
                if rs <= kr < rs + kh:
                    a[v, rq, rk, kr - r + NA_WIN_H - 1] = 1.0
    c = np.zeros((GRID_W, GRID_W, 2 * NA_WIN_W - 1), np.float32)
    for wq in range(GRID_W):
        cs = min(max(wq - NA_WIN_W // 2, 0), GRID_W - NA_WIN_W)
        for wk in range(cs, cs + NA_WIN_W):
            c[wq, wk, min(max(wk - wq, -(NA_WIN_W - 1)), NA_WIN_W - 1) + NA_WIN_W - 1] = 1.0
    return a, c


def na_bias_table(rpb, rows):
    a, c = _na_tables(rows)
    hp = lax.Precision.HIGHEST
    t1 = jnp.einsum("hde,wue->hdwu", rpb, jnp.asarray(c), precision=hp)
    b = jnp.einsum("hdwu,vqkd->hvqwku", t1, jnp.asarray(a), precision=hp)
    ok = np.einsum("vqk,wu->vqwku", a.sum(-1), c.sum(-1)) > 0
    b = jnp.where(jnp.asarray(ok)[None], b, NEG)
    return b.reshape(rpb.shape[0], 3, NA_ROWS * GRID_W, NA_KROWS * GRID_W)


def na_bias_grad(dbias, rows):
    a, c = _na_tables(rows)
    hp = lax.Precision.HIGHEST
    d = dbias.reshape(dbias.shape[0], 3, NA_ROWS, GRID_W, NA_KROWS, GRID_W)
    t1 = jnp.einsum("hvqwku,vqkd->hdwu", d, jnp.asarray(a), precision=hp)
    return jnp.einsum("hdwu,wue->hde", t1, jnp.asarray(c), precision=hp)


def _na_variant(blk, n_lat_blk):
    return jnp.where(blk == 0, 0, jnp.where(blk >= n_lat_blk - 1, 2, 1))


def _na_kstart(blk, rows):
    r0 = blk * NA_ROWS
    ks = jnp.clip(r0 - NA_WIN_H // 2, 0, rows - NA_KROWS)
    return pl.multiple_of(ks * GRID_W, GRID_W)


def na_fwd_call(naqkv, bias, n_lat):
    T = naqkv.shape[0]
    nb = T // TM
    n_lat_blk = n_lat // TM
    rows = n_lat // GRID_W
    n_ctx = T - n_lat
    KW = NA_KROWS * GRID_W
    npair = NA_HEADS // 2

    def body(q_ref, k_ref, v_ref, b_ref, y_ref, lse_ref):
        blk = pl.program_id(1)
        half = _lane_half((TM, LANES))
        lo = half == 0
        q = q_ref[...].astype(F32) * SCALE
        kctx = k_ref[pl.ds(n_lat, n_ctx), :]
        vctx = v_ref[pl.ds(n_lat, n_ctx), :]

        @pl.when(blk < n_lat_blk)
        def _():
            off = _na_kstart(blk, rows)
            kwin = k_ref[pl.ds(off, KW), :]
            vwin = v_ref[pl.ds(off, KW), :]
            outs = []
            for hh in range(2):
                qm = jnp.where(half == hh, q, 0.0).astype(BF16)
                sw = _dot_nt(qm, kwin) + b_ref[hh, 0]
                sc = _dot_nt(qm, kctx)
                m = jnp.maximum(jnp.max(sw, axis=-1, keepdims=True), jnp.max(sc, axis=-1, keepdims=True))
                pw = jnp.exp(sw - m)
                pc = jnp.exp(sc - m)
                l = jnp.sum(pw, axis=-1, keepdims=True) + jnp.sum(pc, axis=-1, keepdims=True)
                outs.append((_dot(pw.astype(BF16), vwin) + _dot(pc.astype(BF16), vctx)) / l)
                lse_ref[hh] = jnp.broadcast_to(m + jnp.log(l), (TM, LANES))
            y_ref[...] = jnp.where(lo, outs[0], outs[1]).astype(BF16)

        @pl.when(blk >= n_lat_blk)
        def _():
            outs = []
            for hh in range(2):
                qm = jnp.where(half == hh, q, 0.0).astype(BF16)
                sc = _dot_nt(qm, kctx)
                m = jnp.max(sc, axis=-1, keepdims=True)
                pc = jnp.exp(sc - m)
                l = jnp.sum(pc, axis=-1, keepdims=True)
                outs.append(_dot(pc.astype(BF16), vctx) / l)
                lse_ref[hh] = jnp.broadcast_to(m + jnp.log(l), (TM, LANES))
            y_ref[...] = jnp.where(lo, outs[0], outs[1]).astype(BF16)

    return pl.pallas_call(
        body, name="na_fwd", grid=(npair, nb),
        in_specs=[pl.BlockSpec((TM, LANES), lambda p, b: (b, p)),
                  pl.BlockSpec((T, LANES), lambda p, b: (0, npair + p)),
                  pl.BlockSpec((T, LANES), lambda p, b: (0, 2 * npair + p)),
                  pl.BlockSpec((2, 1, TM, KW), lambda p, b: (p, _na_variant(b, n_lat_blk), 0, 0))],
        out_specs=[pl.BlockSpec((TM, LANES), lambda p, b: (b, p)),
                   pl.BlockSpec((2, TM, LANES), lambda p, b: (p, b, 0))],
        out_shape=[jax.ShapeDtypeStruct((T, NA_HEADS * HEAD_DIM), BF16),
                   jax.ShapeDtypeStruct((NA_HEADS, T, LANES), F32)],
        compiler_params=_cp(("parallel", "parallel"), VMEM_BIG),
    )(naqkv, naqkv, naqkv, bias)


def na_bwd_call(naqkv, bias, ya, dya, lse, n_lat):
    T = naqkv.shape[0]
    nb = T // TM
    n_lat_blk = n_lat // TM
    rows = n_lat // GRID_W
    n_ctx = T - n_lat
    KW = NA_KROWS * GRID_W
    npair = NA_HEADS // 2

    def body(q_ref, k_ref, v_ref, b_ref, y_ref, dy_ref, lse_ref, dq_ref, dk_ref, dv_ref, db_ref, dk_sc, dv_sc):
        blk = pl.program_id(1)
        half = _lane_half((TM, LANES))
        lo = half == 0
        q = q_ref[...].astype(F32) * SCALE
        kctx = k_ref[pl.ds(n_lat, n_ctx), :]
        vctx = v_ref[pl.ds(n_lat, n_ctx), :]
        dyv = dy_ref[...].astype(F32)
        yv = y_ref[...].astype(F32)

        @pl.when(blk == 0)
        def _():
            dk_sc[...] = jnp.zeros((T, LANES), F32)
            dv_sc[...] = jnp.zeros((T, LANES), F32)

        @pl.when(blk < n_lat_blk)
        def _():
            off = _na_kstart(blk, rows)
            kwin = k_ref[pl.ds(off, KW), :]
            vwin = v_ref[pl.ds(off, KW), :]
            first = (blk == 0) | (blk == 1) | (blk == n_lat_blk - 1)
            dqs = []
            for hh in range(2):
                sel = half == hh
                qm = jnp.where(sel, q, 0.0).astype(BF16)
                dom = jnp.where(sel, dyv, 0.0)
                delta = jnp.sum(dom * yv, axis=-1, keepdims=True)
                dom = dom.astype(BF16)
                lse_v = lse_ref[hh][:, 0:1]
                pw = jnp.exp(_dot_nt(qm, kwin) + b_ref[hh, 0] - lse_v)
                pc = jnp.exp(_dot_nt(qm, kctx) - lse_v)
                dsw = pw * (_dot_nt(dom, vwin) - delta)
                dsc = pc * (_dot_nt(dom, vctx) - delta)

                @pl.when(first)
                def _():
                    db_ref[hh, 0] = dsw

                @pl.when(jnp.logical_not(first))
                def _():
                    db_ref[hh, 0] += dsw

                dsw = dsw.astype(BF16)
                dsc = dsc.astype(BF16)
                dv_sc[pl.ds(off, KW), :] += _dot_tn(pw.astype(BF16), dom)
                dv_sc[pl.ds(n_lat, n_ctx), :] += _dot_tn(pc.astype(BF16), dom)
                dk_sc[pl.ds(off, KW), :] += _dot_tn(dsw, qm)
                dk_sc[pl.ds(n_lat, n_ctx), :] += _dot_tn(dsc, qm)
                dqs.append((_dot(dsw, kwin) + _dot(dsc, kctx)) * SCALE)
            dq_ref[...] = jnp.where(lo, dqs[0], dqs[1]).astype(BF16)

        @pl.when(blk >= n_lat_blk)
        def _():
            dqs = []
            for hh in range(2):
                sel = half == hh
                qm = jnp.where(sel, q, 0.0).astype(BF16)
                dom = jnp.where(sel, dyv, 0.0)
                delta = jnp.sum(dom * yv, axis=-1, keepdims=True)
                dom = dom.astype(BF16)
                pc = jnp.exp(_dot_nt(qm, kctx) - lse_ref[hh][:, 0:1])
                dsc = (pc * (_dot_nt(dom, vctx) - delta)).astype(BF16)
                dv_sc[pl.ds(n_lat, n_ctx), :] += _dot_tn(pc.astype(BF16), dom)
                dk_sc[pl.ds(n_lat, n_ctx), :] += _dot_tn(dsc, qm)
                dqs.append(_dot(dsc, kctx) * SCALE)
            dq_ref[...] = jnp.where(lo, dqs[0], dqs[1]).astype(BF16)

        @pl.when(blk == nb - 1)
        def _():
            dk_ref[...] = dk_sc[...].astype(BF16)
            dv_ref[...] = dv_sc[...].astype(BF16)

    blkspec = pl.BlockSpec((TM, LANES), lambda p, b: (b, p))
    col = pl.BlockSpec((T, LANES), lambda p, b: (0, p))
    bspec = pl.BlockSpec((2, 1, TM, KW), lambda p, b: (p, _na_variant(b, n_lat_blk), 0, 0))
    dq, dk, dv, db = pl.pallas_call(
        body, name="na_bwd", grid=(npair, nb),
        in_specs=[blkspec,
                  pl.BlockSpec((T, LANES), lambda p, b: (0, npair + p)),
                  pl.BlockSpec((T, LANES), lambda p, b: (0, 2 * npair + p)),
                  bspec, blkspec, blkspec,
                  pl.BlockSpec((2, TM, LANES), lambda p, b: (p, b, 0))],
        out_specs=[blkspec, col, col, bspec],
        out_shape=[jax.ShapeDtypeStruct((T, NA_HEADS * HEAD_DIM), BF16)] * 3
        + [jax.ShapeDtypeStruct((NA_HEADS, 3, TM, KW), F32)],
        scratch_shapes=[pltpu.VMEM((T, LANES), F32), pltpu.VMEM((T, LANES), F32)],
        compiler_params=_cp(("arbitrary", "arbitrary"), VMEM_BIG),
    )(naqkv, naqkv, naqkv, bias, ya, dya, lse)
    return jnp.concatenate([dq, dk, dv], axis=1), db


def rope_tables(n_lat, n_ctx):
    t = jnp.arange(n_lat)
    row = (t // GRID_W).astype(F32)
    col = (t % GRID_W).astype(F32)
    half = HEAD_DIM // 2
    inv = ROPE_THETA ** (-jnp.arange(0, half, 2, dtype=F32) / half)
    ang = jnp.concatenate([row[:, None] * inv, col[:, None] * inv], axis=-1)
    cos = jnp.repeat(jnp.cos(ang), 2, axis=-1)
    sin = jnp.repeat(jnp.sin(ang), 2, axis=-1) * jnp.tile(jnp.array([-1.0, 1.0], F32), half)
    cos = jnp.concatenate([cos, jnp.ones((n_ctx, HEAD_DIM), F32)], axis=0)
    sin = jnp.concatenate([sin, jnp.zeros((n_ctx, HEAD_DIM), F32)], axis=0)
    return jnp.tile(cos, (1, 2)), jnp.tile(sin, (1, 2))


def _row(v):
    return v.reshape(1, -1)


def device_fwd_bwd(xs, target, modv, w, small, n_lat):
    T, D = xs.shape
    n_ctx = T - n_lat
    L = modv.shape[0]
    nlt = n_lat // TM
    rows = n_lat // GRID_W
    cos_t, sin_t = rope_tables(n_lat, n_ctx)
    qg = [jnp.tile(_row(small["q_gain"][l]), (1, GQA_Q_HEADS)) for l in range(L)]
    kg = [jnp.tile(_row(small["k_gain"][l]), (1, GQA_KV_HEADS)) for l in range(L)]
    bias = [na_bias_table(small["na_rpb"][l], rows) for l in range(L)]

    saved = []
    x = xs
    h = norm_mod_call(x, _row(small["norm1"][0]), modv[0], nlt, 0)
    for l in range(L):
        naqkv, gqkv, gates = qkv_call(h, w["w_in"][l])
        ya, lse_a = na_fwd_call(naqkv, bias[l], n_lat)
        qp, kp = gqa_prep_call(gqkv, qg[l], kg[l], cos_t, sin_t)
        yb, lse_b = gqa_fwd_call(qp, kp, gqkv, n_lat)
        merged, pa, pb = merge_call(ya, yb, gates, w["w_pa"][l], w["w_pb"][l])
        x1, osave, h2 = wo_call(merged, w["w_o"][l], x, modv[l], _row(small["norm2"][l]), nlt)
        act, a, u = ffn_in_call(h2, w["w_ffn_in"][l])
        if l + 1 < L:
            x2, fsave, hn = ffn_out_call(act, w["w_ffn_out"][l], x1, modv[l], nlt,
                                         _row(small["norm1"][l + 1]), modv[l + 1])
        else:
            x2, fsave = ffn_out_call(act, w["w_ffn_out"][l], x1, modv[l], nlt)
            hn = None
        saved.append(dict(x=x, h=h, naqkv=naqkv, gqkv=gqkv, gates=gates, ya=ya, lse_a=lse_a, qp=qp, kp=kp,
                          yb=yb, lse_b=lse_b, merged=merged, pa=pa, pb=pb, x1=x1, osave=osave, h2=h2,
                          act=act, a=a, u=u, fsave=fsave))
        x, h = x2, hn

    dx, d_final, loss_part = loss_call(x, _row(small["final_norm"]), target, nlt)

    gw = {k: [None] * L for k in ("w_in", "w_pa", "w_pb", "w_o", "w_ffn_in", "w_ffn_out")}
    gs = {k: [None] * L for k in ("norm1", "norm2", "na_rpb", "q_gain", "k_gain")}
    dmod = [None] * L
    fold = lambda s: s.sum(axis=1)
    for l in reversed(range(L)):
        s = saved[l]
        da, du, df, s_g2 = dact_call(dx, modv[l], s["fsave"], w["w_ffn_out"][l], s["a"], s["u"], nlt)
        gw["w_ffn_out"][l] = wgrad_call(s["act"], df, "wgrad_ffn_out")
        dx1, s_sh2, s_sc2, s_n2 = dh2_call(da, du, w["w_ffn_in"][l], s["x1"], modv[l], _row(small["norm2"][l]), dx, nlt)
        gw["w_ffn_in"][l] = jnp.concatenate(
            [wgrad_call(s["h2"], da, "wgrad_ffn_in_a"), wgrad_call(s["h2"], du, "wgrad_ffn_in_u")], axis=1)
        dpa, dpb, dgates, do, s_g1 = dmerged_call(dx1, modv[l], s["osave"], w["w_o"][l], s["pa"], s["pb"], s["gates"], nlt)
        gw["w_o"][l] = wgrad_call(s["merged"], do, "wgrad_o")
        dya, dyb = dy_call(dpa, dpb, w["w_pa"][l], w["w_pb"][l])
        gw["w_pa"][l] = wgrad_call(s["ya"], dpa, "wgrad_pa")
        gw["w_pb"][l] = wgrad_call(s["yb"], dpb, "wgrad_pb")
        dna, dbias = na_bwd_call(s["naqkv"], bias[l], s["ya"], dya, s["lse_a"], n_lat)
        gs["na_rpb"][l] = na_bias_grad(dbias, rows)
        dqp, dkp, dv = gqa_bwd_call(s["qp"], s["kp"], s["gqkv"], s["yb"], dyb, s["lse_b"], n_lat)
        dg, s_qg, s_kg = gqa_prep_bwd_call(s["gqkv"], qg[l], kg[l], cos_t, sin_t, dqp, dkp, dv)
        gs["q_gain"][l] = s_qg.reshape(8 * GQA_Q_HEADS, HEAD_DIM).sum(axis=0)
        gs["k_gain"][l] = s_kg.reshape(8 * GQA_KV_HEADS, HEAD_DIM).sum(axis=0)
        dx, s_sh1, s_sc1, s_n1 = dh_call(dna, dg, dgates, w["w_in"][l], s["x"], modv[l], _row(small["norm1"][l]), dx1, nlt)
        gw["w_in"][l] = jnp.concatenate(
            [wgrad_call(s["h"], dna, "wgrad_in_na"), wgrad_call(s["h"], dg, "wgrad_in_g"),
             wgrad_call(s["h"], dgates, "wgrad_in_gate")], axis=1)
        gs["norm1"][l] = fold(s_n1).sum(axis=0)
        gs["norm2"][l] = fold(s_n2).sum(axis=0)
        dmod[l] = jnp.stack([fold(s_sh1), fold(s_sc1), fold(s_g1), fold(s_sh2), fold(s_sc2), fold(s_g2)], axis=1)

    gw = {k: jnp.stack(v) for k, v in gw.items()}
    gs = {k: jnp.stack(v) for k, v in gs.items()}
    gs["final_norm"] = d_final.sum(axis=(0, 1))
    return loss_part, dx, gw, jnp.stack(dmod), gs


N_DEV = 8
N_CHIP = 4
ANY = pl.BlockSpec(memory_space=pl.ANY)


def _place():
    x, y, c = lax.axis_index("x"), lax.axis_index("y"), lax.axis_index("c")
    chips = [(1 - x, y), (x, 1 - y), (1 - x, 1 - y)]
    return x, y, c, chips


def small_all_gather(v, name):
    m_per, n = v.shape
    assert m_per % 8 == 0

    def body(x_ref, out_ref, send_sems, recv_sems, local_sem):
        x, y, c, chips = _place()
        me, sibling = (x, y, c), (x, y, 1 - c)

        def rows(px, py, pc):
            return out_ref.at[pl.ds(pl.multiple_of((4 * px + 2 * py + pc) * m_per, 8), m_per), :]

        def copy(k, block, to, src=None):
            return pltpu.make_async_remote_copy(
                src_ref=rows(*block) if src is None else src, dst_ref=rows(*block),
                send_sem=send_sems.at[k], recv_sem=recv_sems.at[k], device_id=to, device_id_type=MESH)

        mine = pltpu.make_async_copy(x_ref, rows(*me), local_sem)
        mine.start()
        first = [copy(0, me, sibling, src=x_ref)]
        first += [copy(1 + j, me, (*chip, c), src=x_ref) for j, chip in enumerate(chips)]
        for cp in first:
            cp.start()
        passed = [copy(4 + j, (*chip, c), sibling) for j, chip in enumerate(chips)]
        for j, chip in enumerate(chips):
            copy(1 + j, (*chip, c), me).wait_recv()
            passed[j].start()
        copy(0, sibling, me).wait_recv()
        for j, chip in enumerate(chips):
            copy(4 + j, (*chip, 1 - c), me).wait_recv()
        for cp in first + passed:
            cp.wait_send()
        mine.wait()

    out = pl.pallas_call(
        body, name=name,
        out_shape=jax.ShapeDtypeStruct((N_DEV * m_per, n), v.dtype),
        in_specs=[pl.BlockSpec(memory_space=pltpu.VMEM)],
        out_specs=pl.BlockSpec(memory_space=pltpu.VMEM),
        scratch_shapes=[pltpu.SemaphoreType.DMA((7,)), pltpu.SemaphoreType.DMA((7,)), pltpu.SemaphoreType.DMA],
    )(v)
    return out.reshape(N_DEV, m_per, n)


def weights_all_gather(packed):
    _, rh, ncol = packed.shape

    def body(in_ref, out_ref, send_sems, recv_sems, local_sem):
        x, y, c, chips = _place()
        sibling = (x, y, 1 - c)

        def blk(px, py, half):
            return out_ref.at[2 * px + py, half]

        def copy(k, block, to, src=None):
            return pltpu.make_async_remote_copy(
                src_ref=blk(*block) if src is None else src, dst_ref=blk(*block),
                send_sem=send_sems.at[k], recv_sem=recv_sems.at[k], device_id=to, device_id_type=MESH)

        mine = pltpu.make_async_copy(in_ref, out_ref.at[2 * x + y], local_sem)
        mine.start()
        first = [copy(j, (x, y, c), (*chip, c), src=in_ref.at[c]) for j, chip in enumerate(chips)]
        for cp in first:
            cp.start()
        passed = [copy(3 + j, (*chip, c), sibling) for j, chip in enumerate(chips)]
        for j, chip in enumerate(chips):
            copy(j, (*chip, c), (x, y, c)).wait_recv()
            passed[j].start()
        for j, chip in enumerate(chips):
            copy(3 + j, (*chip, 1 - c), (x, y, c)).wait_recv()
        for cp in first + passed:
            cp.wait_send()
        mine.wait()

    return pl.pallas_call(
        body, name="weights_all_gather",
        out_shape=jax.ShapeDtypeStruct((N_CHIP, 2, rh, ncol), packed.dtype),
        in_specs=[ANY], out_specs=ANY,
        scratch_shapes=[pltpu.SemaphoreType.DMA((6,)), pltpu.SemaphoreType.DMA((6,)), pltpu.SemaphoreType.DMA],
    )(packed)


def grads_swap_halves(gp):
    _, _, rh, ncol = gp.shape

    def body(gp_ref, r1_ref, send_sem, recv_sem):
        x, y, c, _ = _place()
        cp = pltpu.make_async_remote_copy(src_ref=gp_ref.at[1 - c], dst_ref=r1_ref, send_sem=send_sem,
                                          recv_sem=recv_sem, device_id=(x, y, 1 - c), device_id_type=MESH)
        cp.start()
        cp.wait()

    return pl.pallas_call(
        body, name="grads_swap_halves", out_shape=jax.ShapeDtypeStruct((N_CHIP, rh, ncol), gp.dtype),
        in_specs=[ANY], out_specs=ANY,
        scratch_shapes=[pltpu.SemaphoreType.DMA, pltpu.SemaphoreType.DMA],
    )(gp)


def grads_to_owner(p1):
    _, rh, ncol = p1.shape

    def body(p1_ref, r2_ref, send_sems, recv_sems):
        x, y, c, chips = _place()
        cps = [pltpu.make_async_remote_copy(src_ref=p1_ref.at[2 * px + py], dst_ref=r2_ref.at[j],
                                            send_sem=send_sems.at[j], recv_sem=recv_sems.at[j],
                                            device_id=(px, py, c), device_id_type=MESH)
               for j, (px, py) in enumerate(chips)]
        for cp in cps:
            cp.start()
        for cp in cps:
            cp.wait()

    return pl.pallas_call(
        body, name="grads_to_owner", out_shape=jax.ShapeDtypeStruct((3, rh, ncol), p1.dtype),
        in_specs=[ANY], out_specs=ANY,
        scratch_shapes=[pltpu.SemaphoreType.DMA((3,)), pltpu.SemaphoreType.DMA((3,))],
    )(p1)


def grads_join_halves(gh):
    rh, ncol = gh.shape

    def body(gh_ref, out_ref, send_sem, recv_sem, local_sem):
        x, y, c, _ = _place()
        mine = pltpu.make_async_copy(gh_ref, out_ref.at[c], local_sem)
        cp = pltpu.make_async_remote_copy(src_ref=gh_ref, dst_ref=out_ref.at[c], send_sem=send_sem,
                                          recv_sem=recv_sem, device_id=(x, y, 1 - c), device_id_type=MESH)
        mine.start()
        cp.start()
        cp.wait()
        mine.wait()

    return pl.pallas_call(
        body, name="grads_join_halves", out_shape=jax.ShapeDtypeStruct((2, rh, ncol), gh.dtype),
        in_specs=[ANY], out_specs=ANY,
        scratch_shapes=[pltpu.SemaphoreType.DMA, pltpu.SemaphoreType.DMA, pltpu.SemaphoreType.DMA],
    )(gh)


PACK_C = 1024


def add_halves_call(gp, r1, c):
    _, n, rh, ncol = gp.shape
    tr = _pick_tile(rh, 512, 16)

    def body(c_ref, a_ref, b_ref, o_ref):
        o_ref[0] = (a_ref[0, 0].astype(F32) + b_ref[0].astype(F32)).astype(BF16)

    return pl.pallas_call(
        body, name="add_halves",
        grid_spec=pltpu.PrefetchScalarGridSpec(
            num_scalar_prefetch=1, grid=(n, rh // tr),
            in_specs=[pl.BlockSpec((1, 1, tr, ncol), lambda k, i, c_ref: (c_ref[0], k, i, 0)),
                      pl.BlockSpec((1, tr, ncol), lambda k, i, c_ref: (k, i, 0))],
            out_specs=pl.BlockSpec((1, tr, ncol), lambda k, i, c_ref: (k, i, 0))),
        out_shape=jax.ShapeDtypeStruct((n, rh, ncol), BF16),
        compiler_params=_cp(("parallel", "parallel")),
    )(jnp.reshape(c, (1,)).astype(jnp.int32), gp, r1)


def sum_owner_call(p1, r2, k):
    _, rh, ncol = p1.shape
    tr = _pick_tile(rh, 512, 16)

    def body(k_ref, a_ref, b0_ref, b1_ref, b2_ref, o_ref):
        o_ref[...] = (((a_ref[0].astype(F32) + b0_ref[0].astype(F32)) + b1_ref[0].astype(F32))
                      + b2_ref[0].astype(F32))

    other = lambda j: pl.BlockSpec((1, tr, ncol), lambda i, k_ref: (j, i, 0))
    return pl.pallas_call(
        body, name="sum_owner",
        grid_spec=pltpu.PrefetchScalarGridSpec(
            num_scalar_prefetch=1, grid=(rh // tr,),
            in_specs=[pl.BlockSpec((1, tr, ncol), lambda i, k_ref: (k_ref[0], i, 0)), other(0), other(1), other(2)],
            out_specs=pl.BlockSpec((tr, ncol), lambda i, k_ref: (i, 0))),
        out_shape=jax.ShapeDtypeStruct((rh, ncol), F32),
        compiler_params=_cp(("parallel",)),
    )(jnp.reshape(k, (1,)).astype(jnp.int32), p1, r2, r2, r2)


def _silu(x):
    return x * _sigmoid(x)


def mod_matmul_call(craw, w_mod):
    L, D, N = w_mod.shape

    def body(c_ref, w_ref, o_ref):
        o_ref[0] = _dot(_silu(c_ref[...]).astype(BF16), w_ref[0].astype(BF16))

    return pl.pallas_call(
        body, name="mod_matmul", grid=(L,),
        in_specs=[pl.BlockSpec((16, D), lambda l: (0, 0)), pl.BlockSpec((1, D, N), lambda l: (l, 0, 0))],
        out_specs=pl.BlockSpec((1, 16, N), lambda l: (l, 0, 0)),
        out_shape=jax.ShapeDtypeStruct((L, 16, N), F32),
        compiler_params=_cp(("parallel",), VMEM_BIG),
    )(craw, w_mod)


def mod_bwd_call(craw, dmod16, w_mod):
    L, D, N = w_mod.shape

    def body(c_ref, d_ref, w_ref, gw_ref, ds_ref):
        l = pl.program_id(0)
        d = d_ref[0].astype(BF16)
        gw_ref[0] = _dot_tn(_silu(c_ref[...]).astype(BF16), d)
        part = _dot_nt(d, w_ref[0].astype(BF16))

        @pl.when(l == 0)
        def _():
            ds_ref[...] = part

        @pl.when(l > 0)
        def _():
            ds_ref[...] += part

    return pl.pallas_call(
        body, name="mod_bwd", grid=(L,),
        in_specs=[pl.BlockSpec((16, D), lambda l: (0, 0)), pl.BlockSpec((1, 16, N), lambda l: (l, 0, 0)),
                  pl.BlockSpec((1, D, N), lambda l: (l, 0, 0))],
        out_specs=[pl.BlockSpec((1, D, N), lambda l: (l, 0, 0)), pl.BlockSpec((16, D), lambda l: (0, 0))],
        out_shape=[jax.ShapeDtypeStruct((L, D, N), F32), jax.ShapeDtypeStruct((16, D), F32)],
        compiler_params=_cp(("arbitrary",), VMEM_BIG),
    )(craw, dmod16, w_mod)


def _adamw(w, g, m, v):
    m2 = ADAM_B1 * m + (1.0 - ADAM_B1) * g
    v2 = ADAM_B2 * v + (1.0 - ADAM_B2) * (g * g)
    m_hat = m2 / (1.0 - ADAM_B1 ** ADAM_STEP)
    v_hat = v2 / (1.0 - ADAM_B2 ** ADAM_STEP)
    delta = -ADAM_LR * (m_hat / (jnp.sqrt(v_hat) + ADAM_EPS) + ADAM_WD * w)
    return delta, m2, v2


def adamw_call(w, g, m, v):
    shape = w.shape
    ncol = shape[-1]
    r = math.prod(shape[:-1])
    tr = _pick_tile(r, 512, 8)
    as2d = lambda t: t.reshape(r, ncol)

    def body(w_ref, g_ref, m_ref, v_ref, d_ref, m2_ref, v2_ref):
        d_ref[...], m2_ref[...], v2_ref[...] = _adamw(w_ref[...], g_ref[...], m_ref[...], v_ref[...])

    spec = pl.BlockSpec((tr, ncol), lambda i: (i, 0))
    outs = pl.pallas_call(
        body, name="adamw", grid=(r // tr,), in_specs=[spec] * 4, out_specs=[spec] * 3,
        out_shape=[jax.ShapeDtypeStruct((r, ncol), F32)] * 3,
        compiler_params=_cp(("parallel",)),
    )(as2d(w), as2d(g), as2d(m), as2d(v))
    return [o.reshape(shape) for o in outs]


def adamw_small_call(parts, w, m, v, silu_bwd=False):
    P, R, C = parts.shape

    def body(p_ref, w_ref, m_ref, v_ref, g_ref, d_ref, m2_ref, v2_ref):
        g = p_ref[0]
        for i in range(1, P):
            g = g + p_ref[i]
        wv = w_ref[...]
        if silu_bwd:
            sg = _sigmoid(wv)
            g = g * (sg * (1.0 + wv * (1.0 - sg)))
        g_ref[...] = g
        d_ref[...], m2_ref[...], v2_ref[...] = _adamw(wv, g, m_ref[...], v_ref[...])

    return pl.pallas_call(
        body, name="adamw_small", out_shape=[jax.ShapeDtypeStruct((R, C), F32)] * 4,
    )(parts, w, m, v)


def sum_parts_call(parts):
    P, R, C = parts.shape

    def body(p_ref, o_ref):
        g = p_ref[0]
        for i in range(1, P):
            g = g + p_ref[i]
        o_ref[...] = g

    return pl.pallas_call(body, name="sum_parts", out_shape=jax.ShapeDtypeStruct((R, C), F32))(parts)


BIG = (("w_in", "col"), ("w_pa", "col"), ("w_pb", "col"), ("w_o", "row"), ("w_ffn_in", "col"), ("w_ffn_out", "row"))
SMALL = ("norm1", "norm2", "final_norm", "na_rpb", "q_gain", "k_gain")


def _pack_shards(shards):
    parts = [shards[n].reshape(2, -1, PACK_C) for n, _ in BIG]
    return jnp.concatenate(parts, axis=1)


def _unpack_shards(packed, shapes):
    out, r0 = {}, 0
    lead = packed.shape[:-3]
    for n, _ in BIG:
        L, K, N = shapes[n]
        rw = (L // 2) * K * N // PACK_C
        out[n] = packed[..., r0:r0 + rw, :].reshape(lead + (L, K, N))
        r0 += rw
    return out


def _whole_from_chips(g, kind):
    _, L, K, N = g.shape
    if kind == "col":
        return g.transpose(1, 2, 0, 3).reshape(L, K, N_CHIP * N)
    return g.transpose(1, 0, 2, 3).reshape(L, N_CHIP * K, N)


def _chips_from_whole(g, kind):
    L, K, N = g.shape
    if kind == "col":
        return g.reshape(L, K, N_CHIP, N // N_CHIP).transpose(2, 0, 1, 3)
    return g.reshape(L, N_CHIP, K // N_CHIP, N).transpose(1, 0, 2, 3)


def _pad_rows(v, rows):
    return jnp.pad(v.reshape(-1), (0, rows * PACK_C - v.size)).reshape(rows, PACK_C)


def kernel(x, c, ctx, c_ctx, w_mod, b_mod, norm1, w_in, na_rpb, q_gain, k_gain, w_pa, w_pb, w_o, norm2, w_ffn_in, w_ffn_out, final_norm, loss_target, m_c_ctx, m_w_mod, m_b_mod, m_norm1, m_w_in, m_na_rpb, m_q_gain, m_k_gain, m_w_pa, m_w_pb, m_w_o, m_norm2, m_w_ffn_in, m_w_ffn_out, m_final_norm, v_c_ctx, v_w_mod, v_b_mod, v_norm1, v_w_in, v_na_rpb, v_q_gain, v_k_gain, v_w_pa, v_w_pb, v_w_o, v_norm2, v_w_ffn_in, v_w_ffn_out, v_final_norm):
    W = dict(c_ctx=c_ctx, w_mod=w_mod, b_mod=b_mod, norm1=norm1, w_in=w_in, na_rpb=na_rpb, q_gain=q_gain, k_gain=k_gain,
             w_pa=w_pa, w_pb=w_pb, w_o=w_o, norm2=norm2, w_ffn_in=w_ffn_in, w_ffn_out=w_ffn_out, final_norm=final_norm)
    M = dict(c_ctx=m_c_ctx, w_mod=m_w_mod, b_mod=m_b_mod, norm1=m_norm1, w_in=m_w_in, na_rpb=m_na_rpb, q_gain=m_q_gain,
             k_gain=m_k_gain, w_pa=m_w_pa, w_pb=m_w_pb, w_o=m_w_o, norm2=m_norm2, w_ffn_in=m_w_ffn_in,
             w_ffn_out=m_w_ffn_out, final_norm=m_final_norm)
    V = dict(c_ctx=v_c_ctx, w_mod=v_w_mod, b_mod=v_b_mod, norm1=v_norm1, w_in=v_w_in, na_rpb=v_na_rpb, q_gain=v_q_gain,
             k_gain=v_k_gain, w_pa=v_w_pa, w_pb=v_w_pb, w_o=v_w_o, norm2=v_norm2, w_ffn_in=v_w_ffn_in,
             w_ffn_out=v_w_ffn_out, final_norm=v_final_norm)
    order = ["c_ctx", "w_mod", "b_mod", "norm1", "w_in", "na_rpb", "q_gain", "k_gain", "w_pa", "w_pb", "w_o", "norm2",
             "w_ffn_in", "w_ffn_out", "final_norm"]
    L, D = norm1.shape
    n_lat = x.shape[1]
    ax, ay, ac = lax.axis_index("x"), lax.axis_index("y"), lax.axis_index("c")
    chip = 2 * ax + ay
    dev = 2 * chip + ac

    shard_shapes = {n: W[n].shape for n, _ in BIG}
    packed = _pack_shards({n: W[n].astype(BF16) for n, _ in BIG})
    gathered = _unpack_shards(weights_all_gather(packed), shard_shapes)
    wfull = {n: _whole_from_chips(gathered[n], kind) for n, kind in BIG}

    c_all = small_all_gather(jnp.pad(c, ((0, 7), (0, 0))), "gather_c")[:, 0, :]
    craw = jnp.pad(c_all, ((0, 8), (0, 0))) + jnp.pad(c_ctx[None, :], ((8, 7), (0, 0)))
    ncol_mod = w_mod.shape[2]
    mod_loc = mod_matmul_call(craw, w_mod)
    mod_all = small_all_gather(mod_loc.reshape(L * 16, ncol_mod), "gather_mod")
    mod_all = mod_all[0::2].reshape(N_CHIP, L, 16, ncol_mod).transpose(1, 2, 0, 3).reshape(L, 16, 6 * D)
    mod_all = mod_all + b_mod[:, None, :]
    mod_mine = lax.dynamic_index_in_dim(mod_all, dev, axis=1, keepdims=False)
    modv = jnp.stack([mod_mine, mod_all[:, 8]], axis=1).reshape(L, 2, 6, D)

    xs = jnp.concatenate([x[0], ctx[0]], axis=0)
    small = dict(norm1=norm1, norm2=norm2, na_rpb=na_rpb, q_gain=q_gain, k_gain=k_gain, final_norm=final_norm)
    loss_part, dx, gw, dmod, gs = device_fwd_bwd(xs, loss_target[0], modv, wfull, small, n_lat)
    loss = lax.psum(jnp.sum(loss_part), ("x", "y", "c"))
    grad_x = dx[:n_lat][None]

    gp = jnp.concatenate(
        [_chips_from_whole(gw[n], kind).reshape(N_CHIP, 2, -1, PACK_C).transpose(1, 0, 2, 3) for n, kind in BIG], axis=2)
    r1 = grads_swap_halves(gp)
    p1 = add_halves_call(gp, r1, ac)
    r2 = grads_to_owner(p1)
    gh = sum_owner_call(p1, r2, chip)
    gbig = _unpack_shards(grads_join_halves(gh), shard_shapes)

    n_mod_rows = L * 6 * D // PACK_C
    small_flat = jnp.concatenate([gs[n].reshape(-1) for n in SMALL])
    n_small_rows = -(-small_flat.size // (8 * PACK_C)) * 8
    g1 = jnp.concatenate([dmod[:, 0].reshape(n_mod_rows, PACK_C), dmod[:, 1].reshape(n_mod_rows, PACK_C),
                          _pad_rows(small_flat, n_small_rows)], axis=0)
    g1 = small_all_gather(g1, "gather_small_grads")
    dm_lat = g1[:, :n_mod_rows].reshape(N_DEV, L, 6 * D)
    dm_ctx_parts = g1[:, n_mod_rows:2 * n_mod_rows]
    dm_ctx = sum_parts_call(dm_ctx_parts).reshape(L, 1, 6 * D)
    dmod16 = (jnp.pad(dm_lat.transpose(1, 0, 2), ((0, 0), (0, 8), (0, 0)))
              + jnp.pad(dm_ctx, ((0, 0), (8, 7), (0, 0))))
    dmod16_loc = lax.dynamic_slice_in_dim(dmod16, chip * ncol_mod, ncol_mod, axis=2)
    g_wmod, dsilu_part = mod_bwd_call(craw, dmod16_loc, w_mod)
    ds_all = small_all_gather(dsilu_part[8:16], "gather_dsilu")
    ds_parts = ds_all[0::2, 0:1, :]
    ds_parts = jnp.pad(ds_parts, ((0, 0), (0, 7), (0, 0)))

    out_g, out_d, out_m, out_v = {}, {}, {}, {}
    for n, _ in BIG:
        out_g[n] = gbig[n]
        out_d[n], out_m[n], out_v[n] = adamw_call(W[n], gbig[n], M[n], V[n])
    out_g["w_mod"] = g_wmod
    out_d["w_mod"], out_m["w_mod"], out_v["w_mod"] = adamw_call(w_mod, g_wmod, m_w_mod, v_w_mod)

    pack_small = lambda t: _pad_rows(jnp.concatenate([t[n].reshape(-1) for n in SMALL]), n_small_rows)
    res = adamw_small_call(g1[:, 2 * n_mod_rows:], pack_small(W), pack_small(M), pack_small(V))
    off = 0
    for n in SMALL:
        sz = W[n].size
        for dst, r in zip((out_g, out_d, out_m, out_v), res):
            dst[n] = r.reshape(-1)[off:off + sz].reshape(W[n].shape)
        off += sz
    bparts = jnp.concatenate([g1[:, :n_mod_rows], dm_ctx_parts], axis=0)
    as_rows = lambda t: t.reshape(n_mod_rows, PACK_C)
    res = adamw_small_call(bparts, as_rows(b_mod), as_rows(m_b_mod), as_rows(v_b_mod))
    for dst, r in zip((out_g, out_d, out_m, out_v), res):
        dst["b_mod"] = r.reshape(b_mod.shape)
    row8 = lambda t: jnp.pad(t[None, :], ((0, 7), (0, 0)))
    res = adamw_small_call(ds_parts, row8(c_ctx), row8(m_c_ctx), row8(v_c_ctx), silu_bwd=True)
    for dst, r in zip((out_g, out_d, out_m, out_v), res):
        dst["c_ctx"] = r[0]

    return (loss, grad_x, *[out_g[n] for n in order], *[out_d[n] for n in order],
            *[out_m[n] for n in order], *[out_v[n] for n in order])
```

```python
import functools
import math

import numpy as np
import jax
import jax.numpy as jnp
from jax import lax
from jax.experimental import pallas as pl
from jax.experimental.pallas import tpu as pltpu

F32 = jnp.float32
BF16 = jnp.bfloat16
MESH = pl.DeviceIdType.MESH

HEAD_DIM = 64
NA_HEADS = 8
GQA_Q_HEADS = 8
GQA_KV_HEADS = 2
GRID_W = 64
NA_WIN_H = 8
NA_WIN_W = 16
ROPE_THETA = 10000.0
EPS = 1e-6
SCALE = HEAD_DIM ** -0.5
NEG = -1e30

ADAM_LR = 0.001
ADAM_B1 = 0.9
ADAM_B2 = 0.999
ADAM_EPS = 1e-08
ADAM_WD = 0.01
ADAM_STEP = 10

TM = 256
NA_ROWS = 4
NA_KROWS = 12
LANES = 128
VMEM_BIG = 56 * 1024 * 1024


def _cp(sem, vmem=None):
    return pltpu.CompilerParams(dimension_semantics=sem, vmem_limit_bytes=vmem)


def _dot(a, b):
    return jnp.dot(a, b, preferred_element_type=F32)


def _dot_nt(a, b):
    return lax.dot_general(a, b, (((1,), (1,)), ((), ())), preferred_element_type=F32)


def _dot_tn(a, b):
    return lax.dot_general(a, b, (((0,), (0,)), ((), ())), preferred_element_type=F32)


def _colsum8(v):
    tm, d = v.shape
    return v.reshape(tm // 8, 8, d).sum(axis=0)


def _acc(ref, val, first):
    @pl.when(first)
    def _():
        ref[0] = val

    @pl.when(jnp.logical_not(first))
    def _():
        ref[0] += val


def _rms_r(x):
    return lax.rsqrt(jnp.mean(x * x, axis=-1, keepdims=True) + EPS)


def _norm_mod(x, nw, sh, sc):
    return (x * _rms_r(x) * nw) * (1.0 + sc) + sh


def _norm_mod_bwd(dh, x, nw, sc):
    r = _rms_r(x)
    xn = x * r
    dxn = dh * (nw * (1.0 + sc))
    dx = r * (dxn - xn * jnp.mean(dxn * xn, axis=-1, keepdims=True))
    return dx, dh, dh * (xn * nw), dh * ((1.0 + sc) * xn)


def _sigmoid(x):
    return 1.0 / (1.0 + jnp.exp(-x))


def _group_of(i, n_lat_tiles):
    return jnp.where(i >= n_lat_tiles, 1, 0)


def _lane_half(shape):
    return (lax.broadcasted_iota(jnp.int32, shape, len(shape) - 1) % LANES) // HEAD_DIM


def _lane_lt64(shape):
    return _lane_half(shape) == 0


def _roll64(x):
    return pltpu.roll(x, HEAD_DIM, x.ndim - 1)


def norm_mod_call(x, nw, modv, n_lat_tiles, which):
    T, D = x.shape

    def body(x_ref, nw_ref, mod_ref, h_ref):
        sh = mod_ref[0, 3 * which:3 * which + 1, :]
        sc = mod_ref[0, 3 * which + 1:3 * which + 2, :]
        h_ref[...] = _norm_mod(x_ref[...], nw_ref[...], sh, sc).astype(BF16)

    return pl.pallas_call(
        body, name="norm_mod", grid=(T // TM,),
        in_specs=[pl.BlockSpec((TM, D), lambda i: (i, 0)),
                  pl.BlockSpec((1, D), lambda i: (0, 0)),
                  pl.BlockSpec((1, 6, D), lambda i: (_group_of(i, n_lat_tiles), 0, 0))],
        out_specs=pl.BlockSpec((TM, D), lambda i: (i, 0)),
        out_shape=jax.ShapeDtypeStruct((T, D), BF16),
        compiler_params=_cp(("parallel",)),
    )(x, nw, modv)


def qkv_call(h, w_in):
    T, D = h.shape
    N = w_in.shape[1]
    n_na = 3 * NA_HEADS * HEAD_DIM
    n_g = (GQA_Q_HEADS + 2 * GQA_KV_HEADS) * HEAD_DIM
    n_gate = N - n_na - n_g

    def body(h_ref, w_ref, na_ref, g_ref, gate_ref):
        acc = _dot(h_ref[...], w_ref[...])
        na_ref[...] = acc[:, :n_na].astype(BF16)
        g_ref[...] = acc[:, n_na:n_na + n_g].astype(BF16)
        gate_ref[...] = acc[:, n_na + n_g:].astype(BF16)

    return pl.pallas_call(
        body, name="qkv", grid=(T // TM,),
        in_specs=[pl.BlockSpec((TM, D), lambda i: (i, 0)),
                  pl.BlockSpec((D, N), lambda i: (0, 0))],
        out_specs=[pl.BlockSpec((TM, n_na), lambda i: (i, 0)),
                   pl.BlockSpec((TM, n_g), lambda i: (i, 0)),
                   pl.BlockSpec((TM, n_gate), lambda i: (i, 0))],
        out_shape=[jax.ShapeDtypeStruct((T, n_na), BF16),
                   jax.ShapeDtypeStruct((T, n_g), BF16),
                   jax.ShapeDtypeStruct((T, n_gate), BF16)],
        compiler_params=_cp(("parallel",), VMEM_BIG),
    )(h, w_in)


def merge_call(ya, yb, gates, w_pa, w_pb):
    T, Ka = ya.shape
    D = w_pa.shape[1]

    def body(ya_ref, yb_ref, ga_ref, gb_ref, wa_ref, wb_ref, m_ref, pa_ref, pb_ref):
        pa = _dot(ya_ref[...], wa_ref[...])
        pb = _dot(yb_ref[...], wb_ref[...])
        m = _sigmoid(ga_ref[...].astype(F32)) * pa + _sigmoid(gb_ref[...].astype(F32)) * pb
        m_ref[...] = m.astype(BF16)
        pa_ref[...] = pa.astype(BF16)
        pb_ref[...] = pb.astype(BF16)

    row = lambda w: pl.BlockSpec((TM, w), lambda i: (i, 0))
    return pl.pallas_call(
        body, name="merge", grid=(T // TM,),
        in_specs=[row(Ka), row(Ka),
                  pl.BlockSpec((TM, D), lambda i: (i, 0)), pl.BlockSpec((TM, D), lambda i: (i, 1)),
                  pl.BlockSpec((Ka, D), lambda i: (0, 0)), pl.BlockSpec((Ka, D), lambda i: (0, 0))],
        out_specs=[row(D), row(D), row(D)],
        out_shape=[jax.ShapeDtypeStruct((T, D), BF16)] * 3,
        compiler_params=_cp(("parallel",)),
    )(ya, yb, gates, gates, w_pa, w_pb)


def wo_call(merged, w_o, x, modv, nw2, n_lat_tiles):
    T, D = x.shape

    def body(m_ref, w_ref, x_ref, mod_ref, nw_ref, x1_ref, o_ref, h2_ref):
        o = _dot(m_ref[...], w_ref[...])
        x1 = x_ref[...] + mod_ref[0, 2:3, :] * o
        x1_ref[...] = x1
        o_ref[...] = o.astype(BF16)
        h2_ref[...] = _norm_mod(x1, nw_ref[...], mod_ref[0, 3:4, :], mod_ref[0, 4:5, :]).astype(BF16)

    row = pl.BlockSpec((TM, D), lambda i: (i, 0))
    return pl.pallas_call(
        body, name="wo", grid=(T // TM,),
        in_specs=[row, pl.BlockSpec((D, D), lambda i: (0, 0)), row,
                  pl.BlockSpec((1, 6, D), lambda i: (_group_of(i, n_lat_tiles), 0, 0)),
                  pl.BlockSpec((1, D), lambda i: (0, 0))],
        out_specs=[row, row, row],
        out_shape=[jax.ShapeDtypeStruct((T, D), F32), jax.ShapeDtypeStruct((T, D), BF16),
                   jax.ShapeDtypeStruct((T, D), BF16)],
        compiler_params=_cp(("parallel",)),
    )(merged, w_o, x, modv, nw2)


def ffn_in_call(h2, w_ffn_in, n_col_tiles=2):
    T, D = h2.shape
    F = w_ffn_in.shape[1] // 2
    tn = F // n_col_tiles

    def body(h_ref, wa_ref, wu_ref, act_ref, a_ref, u_ref):
        h = h_ref[...]
        a = _dot(h, wa_ref[...])
        u = _dot(h, wu_ref[...])
        act_ref[...] = (a * _sigmoid(a) * u).astype(BF16)
        a_ref[...] = a.astype(BF16)
        u_ref[...] = u.astype(BF16)

    blk = pl.BlockSpec((TM, tn), lambda j, i: (i, j))
    return pl.pallas_call(
        body, name="ffn_in", grid=(n_col_tiles, T // TM),
        in_specs=[pl.BlockSpec((TM, D), lambda j, i: (i, 0)),
                  pl.BlockSpec((D, tn), lambda j, i: (0, j)),
                  pl.BlockSpec((D, tn), lambda j, i: (0, n_col_tiles + j))],
        out_specs=[blk, blk, blk],
        out_shape=[jax.ShapeDtypeStruct((T, F), BF16)] * 3,
        compiler_params=_cp(("parallel", "parallel"), VMEM_BIG),
    )(h2, w_ffn_in, w_ffn_in)


def ffn_out_call(act, w_ffn_out, x1, modv, n_lat_tiles, next_nw=None, next_modv=None):
    T, D = x1.shape
    F = act.shape[1]
    with_next = next_nw is not None

    def body(*refs):
        if with_next:
            a_ref, w_ref, x_ref, mod_ref, nw_ref, nmod_ref, x2_ref, f_ref, hn_ref = refs
        else:
            a_ref, w_ref, x_ref, mod_ref, x2_ref, f_ref = refs
        f = _dot(a_ref[...], w_ref[...])
        x2 = x_ref[...] + mod_ref[0, 5:6, :] * f
        x2_ref[...] = x2
        f_ref[...] = f.astype(BF16)
        if with_next:
            hn_ref[...] = _norm_mod(x2, nw_ref[...], nmod_ref[0, 0:1, :], nmod_ref[0, 1:2, :]).astype(BF16)

    row = pl.BlockSpec((TM, D), lambda i: (i, 0))
    modspec = pl.BlockSpec((1, 6, D), lambda i: (_group_of(i, n_lat_tiles), 0, 0))
    in_specs = [pl.BlockSpec((TM, F), lambda i: (i, 0)), pl.BlockSpec((F, D), lambda i: (0, 0)), row, modspec]
    args = [act, w_ffn_out, x1, modv]
    out_specs = [row, row]
    out_shape = [jax.ShapeDtypeStruct((T, D), F32), jax.ShapeDtypeStruct((T, D), BF16)]
    if with_next:
        in_specs += [pl.BlockSpec((1, D), lambda i: (0, 0)), modspec]
        args += [next_nw, next_modv]
        out_specs.append(row)
        out_shape.append(jax.ShapeDtypeStruct((T, D), BF16))
    return pl.pallas_call(
        body, name="ffn_out", grid=(T // TM,), in_specs=in_specs, out_specs=out_specs, out_shape=out_shape,
        compiler_params=_cp(("parallel",), VMEM_BIG),
    )(*args)


def loss_call(x, nw, target, n_lat_tiles):
    T, D = x.shape

    def body(x_ref, nw_ref, t_ref, dx_ref, dw_ref, ls_ref):
        i = pl.program_id(0)

        @pl.when(i < n_lat_tiles)
        def _():
            xv = x_ref[...]
            nw_v = nw_ref[...]
            r = _rms_r(xv)
            xn = xv * r
            e = xn * nw_v - t_ref[...]
            dy = e * (1.0 / D)
            dxn = dy * nw_v
            dx_ref[...] = r * (dxn - xn * jnp.mean(dxn * xn, axis=-1, keepdims=True))
            _acc(dw_ref, _colsum8(dy * xn), i == 0)
            _acc(ls_ref, _colsum8(e * e * (0.5 / D)), i == 0)

        @pl.when(i >= n_lat_tiles)
        def _():
            dx_ref[...] = jnp.zeros((TM, D), F32)

    stat = pl.BlockSpec((1, 8, D), lambda i: (0, 0, 0))
    return pl.pallas_call(
        body, name="loss", grid=(T // TM,),
        in_specs=[pl.BlockSpec((TM, D), lambda i: (i, 0)), pl.BlockSpec((1, D), lambda i: (0, 0)),
                  pl.BlockSpec((TM, D), lambda i: (jnp.minimum(i, n_lat_tiles - 1), 0))],
        out_specs=[pl.BlockSpec((TM, D), lambda i: (i, 0)), stat, stat],
        out_shape=[jax.ShapeDtypeStruct((T, D), F32), jax.ShapeDtypeStruct((1, 8, D), F32),
                   jax.ShapeDtypeStruct((1, 8, D), F32)],
        compiler_params=_cp(("arbitrary",)),
    )(x, nw, target)


def dact_call(dx2, modv, fsave, w_ffn_out, a, u, n_lat_tiles):
    T, D = dx2.shape
    F = a.shape[1]

    def body(dx_ref, mod_ref, f_ref, w_ref, a_ref, u_ref, da_ref, du_ref, df_ref, dg_ref):
        i = pl.program_id(0)
        dx = dx_ref[...]
        df = (dx * mod_ref[0, 5:6, :]).astype(BF16)
        df_ref[...] = df
        dact = _dot_nt(df, w_ref[...])
        av = a_ref[...].astype(F32)
        uv = u_ref[...].astype(F32)
        sg = _sigmoid(av)
        da_ref[...] = (dact * uv * (sg * (1.0 + av * (1.0 - sg)))).astype(BF16)
        du_ref[...] = (dact * (av * sg)).astype(BF16)
        _acc(dg_ref, _colsum8(dx * f_ref[...].astype(F32)), (i == 0) | (i == n_lat_tiles))

    row = pl.BlockSpec((TM, D), lambda i: (i, 0))
    wide = pl.BlockSpec((TM, F), lambda i: (i, 0))
    grp = lambda i: (_group_of(i, n_lat_tiles), 0, 0)
    return pl.pallas_call(
        body, name="dact", grid=(T // TM,),
        in_specs=[row, pl.BlockSpec((1, 6, D), grp), row, pl.BlockSpec((F, D), lambda i: (0, 0)), wide, wide],
        out_specs=[wide, wide, row, pl.BlockSpec((1, 8, D), grp)],
        out_shape=[jax.ShapeDtypeStruct((T, F), BF16), jax.ShapeDtypeStruct((T, F), BF16),
                   jax.ShapeDtypeStruct((T, D), BF16), jax.ShapeDtypeStruct((2, 8, D), F32)],
        compiler_params=_cp(("arbitrary",), VMEM_BIG),
    )(dx2, modv, fsave, w_ffn_out, a, u)


def _norm_bwd_tail(dh, x_ref, mod_ref, nw_ref, dres_ref, dx_ref, s_sh, s_sc, s_w, which, first):
    sc = mod_ref[0, 3 * which + 1:3 * which + 2, :]
    dxn, t_sh, t_sc, t_w = _norm_mod_bwd(dh, x_ref[...], nw_ref[...], sc)
    dx_ref[...] = dres_ref[...] + dxn
    _acc(s_sh, _colsum8(t_sh), first)
    _acc(s_sc, _colsum8(t_sc), first)
    _acc(s_w, _colsum8(t_w), first)


def dh2_call(da, du, w_ffn_in, x1, modv, nw2, dx2, n_lat_tiles):
    T, D = x1.shape
    F = da.shape[1]

    def body(da_ref, du_ref, wa_ref, wu_ref, x_ref, mod_ref, nw_ref, dres_ref, dx_ref, s_sh, s_sc, s_w):
        i = pl.program_id(0)
        dh = _dot_nt(da_ref[...], wa_ref[...]) + _dot_nt(du_ref[...], wu_ref[...])
        _norm_bwd_tail(dh, x_ref, mod_ref, nw_ref, dres_ref, dx_ref, s_sh, s_sc, s_w, 1,
                       (i == 0) | (i == n_lat_tiles))

    row = pl.BlockSpec((TM, D), lambda i: (i, 0))
    wide = pl.BlockSpec((TM, F), lambda i: (i, 0))
    grp = lambda i: (_group_of(i, n_lat_tiles), 0, 0)
    stat = pl.BlockSpec((1, 8, D), grp)
    return pl.pallas_call(
        body, name="dh2", grid=(T // TM,),
        in_specs=[wide, wide, pl.BlockSpec((D, F), lambda i: (0, 0)), pl.BlockSpec((D, F), lambda i: (0, 1)),
                  row, pl.BlockSpec((1, 6, D), grp), pl.BlockSpec((1, D), lambda i: (0, 0)), row],
        out_specs=[row, stat, stat, stat],
        out_shape=[jax.ShapeDtypeStruct((T, D), F32)] + [jax.ShapeDtypeStruct((2, 8, D), F32)] * 3,
        compiler_params=_cp(("arbitrary",), VMEM_BIG),
    )(da, du, w_ffn_in, w_ffn_in, x1, modv, nw2, dx2)


def dmerged_call(dx1, modv, osave, w_o, pa, pb, gates, n_lat_tiles):
    T, D = dx1.shape

    def body(dx_ref, mod_ref, o_ref, w_ref, pa_ref, pb_ref, ga_ref, gb_ref,
             dpa_ref, dpb_ref, dgate_ref, do_ref, dg_ref):
        i = pl.program_id(0)
        dx = dx_ref[...]
        do = (dx * mod_ref[0, 2:3, :]).astype(BF16)
        do_ref[...] = do
        dm = _dot_nt(do, w_ref[...])
        sa = _sigmoid(ga_ref[...].astype(F32))
        sb = _sigmoid(gb_ref[...].astype(F32))
        dpa_ref[...] = (dm * sa).astype(BF16)
        dpb_ref[...] = (dm * sb).astype(BF16)
        dgate_ref[:, :D] = (dm * pa_ref[...].astype(F32) * (sa * (1.0 - sa))).astype(BF16)
        dgate_ref[:, D:] = (dm * pb_ref[...].astype(F32) * (sb * (1.0 - sb))).astype(BF16)
        _acc(dg_ref, _colsum8(dx * o_ref[...].astype(F32)), (i == 0) | (i == n_lat_tiles))

    row = pl.BlockSpec((TM, D), lambda i: (i, 0))
    grp = lambda i: (_group_of(i, n_lat_tiles), 0, 0)
    return pl.pallas_call(
        body, name="dmerged", grid=(T // TM,),
        in_specs=[row, pl.BlockSpec((1, 6, D), grp), row, pl.BlockSpec((D, D), lambda i: (0, 0)), row, row,
                  pl.BlockSpec((TM, D), lambda i: (i, 0)), pl.BlockSpec((TM, D), lambda i: (i, 1))],
        out_specs=[row, row, pl.BlockSpec((TM, 2 * D), lambda i: (i, 0)), row, pl.BlockSpec((1, 8, D), grp)],
        out_shape=[jax.ShapeDtypeStruct((T, D), BF16), jax.ShapeDtypeStruct((T, D), BF16),
                   jax.ShapeDtypeStruct((T, 2 * D), BF16), jax.ShapeDtypeStruct((T, D), BF16),
                   jax.ShapeDtypeStruct((2, 8, D), F32)],
        compiler_params=_cp(("arbitrary",)),
    )(dx1, modv, osave, w_o, pa, pb, gates, gates)


def dy_call(dpa, dpb, w_pa, w_pb):
    T, D = dpa.shape
    K = w_pa.shape[0]

    def body(a_ref, b_ref, wa_ref, wb_ref, ya_ref, yb_ref):
        ya_ref[...] = _dot_nt(a_ref[...], wa_ref[...]).astype(BF16)
        yb_ref[...] = _dot_nt(b_ref[...], wb_ref[...]).astype(BF16)

    row = pl.BlockSpec((TM, D), lambda i: (i, 0))
    w = pl.BlockSpec((K, D), lambda i: (0, 0))
    out = pl.BlockSpec((TM, K), lambda i: (i, 0))
    return pl.pallas_call(
        body, name="dy", grid=(T // TM,), in_specs=[row, row, w, w], out_specs=[out, out],
        out_shape=[jax.ShapeDtypeStruct((T, K), BF16)] * 2,
        compiler_params=_cp(("parallel",)),
    )(dpa, dpb, w_pa, w_pb)


def dh_call(dna, dg, dgates, w_in, x, modv, nw1, dx1, n_lat_tiles):
    T, D = x.shape
    n1, n2, n3 = dna.shape[1], dg.shape[1], dgates.shape[1]
    N = n1 + n2 + n3

    def body(a_ref, b_ref, c_ref, w_ref, x_ref, mod_ref, nw_ref, dres_ref, dx_ref, s_sh, s_sc, s_w):
        i = pl.program_id(0)
        dh = (_dot_nt(a_ref[...], w_ref[:, :n1]) + _dot_nt(b_ref[...], w_ref[:, n1:n1 + n2])
              + _dot_nt(c_ref[...], w_ref[:, n1 + n2:]))
        _norm_bwd_tail(dh, x_ref, mod_ref, nw_ref, dres_ref, dx_ref, s_sh, s_sc, s_w, 0,
                       (i == 0) | (i == n_lat_tiles))

    row = pl.BlockSpec((TM, D), lambda i: (i, 0))
    grp = lambda i: (_group_of(i, n_lat_tiles), 0, 0)
    stat = pl.BlockSpec((1, 8, D), grp)
    return pl.pallas_call(
        body, name="dh", grid=(T // TM,),
        in_specs=[pl.BlockSpec((TM, n1), lambda i: (i, 0)), pl.BlockSpec((TM, n2), lambda i: (i, 0)),
                  pl.BlockSpec((TM, n3), lambda i: (i, 0)), pl.BlockSpec((D, N), lambda i: (0, 0)),
                  row, pl.BlockSpec((1, 6, D), grp), pl.BlockSpec((1, D), lambda i: (0, 0)), row],
        out_specs=[row, stat, stat, stat],
        out_shape=[jax.ShapeDtypeStruct((T, D), F32)] + [jax.ShapeDtypeStruct((2, 8, D), F32)] * 3,
        compiler_params=_cp(("arbitrary",), VMEM_BIG),
    )(dna, dg, dgates, w_in, x, modv, nw1, dx1)


def _pick_tile(n, cap, mult):
    best = None
    for d in range(mult, min(n, cap) + 1, mult):
        if n % d == 0:
            best = d
    assert best is not None, (n, cap, mult)
    return best


def wgrad_call(a, b, name):
    T, K = a.shape
    N = b.shape[1]
    tt = _pick_tile(T, 1536, 128)
    tk = _pick_tile(K, 512, 128)
    tn = N if N <= 2304 else _pick_tile(N, 2304, 128)
    nt = T // tt

    def body(a_ref, b_ref, o_ref, acc_ref):
        t = pl.program_id(2)
        p = _dot_tn(a_ref[...], b_ref[...])

        @pl.when(t == 0)
        def _():
            acc_ref[...] = p

        @pl.when(t > 0)
        def _():
            acc_ref[...] += p

        @pl.when(t == nt - 1)
        def _():
            o_ref[...] = acc_ref[...].astype(BF16)

    return pl.pallas_call(
        body, name=name, grid=(K // tk, N // tn, nt),
        in_specs=[pl.BlockSpec((tt, tk), lambda i, j, t: (t, i)), pl.BlockSpec((tt, tn), lambda i, j, t: (t, j))],
        out_specs=pl.BlockSpec((tk, tn), lambda i, j, t: (i, j)),
        out_shape=jax.ShapeDtypeStruct((K, N), BF16),
        scratch_shapes=[pltpu.VMEM((tk, tn), F32)],
        compiler_params=_cp(("parallel", "parallel", "arbitrary"), VMEM_BIG),
    )(a, b)


def _head_mean(v):
    n = v.shape[-1]
    r = lax.broadcasted_iota(jnp.int32, (n, n), 0) // HEAD_DIM
    c = lax.broadcasted_iota(jnp.int32, (n, n), 1) // HEAD_DIM
    bd = jnp.where(r == c, 1.0 / HEAD_DIM, 0.0).astype(F32)
    return jnp.dot(v, bd, precision=lax.Precision.HIGHEST, preferred_element_type=F32)


def _swap_pairs(x):
    n = x.shape[-1]
    even = lax.broadcasted_iota(jnp.int32, x.shape, x.ndim - 1) % 2 == 0
    return jnp.where(even, pltpu.roll(x, n - 1, x.ndim - 1), pltpu.roll(x, 1, x.ndim - 1))


def _tile_lanes(t, n):
    return jnp.tile(t, (1, n // t.shape[-1]))


def gqa_prep_call(gqkv, qgain, kgain, cos_t, sin_t):
    T = gqkv.shape[0]
    nq = GQA_Q_HEADS * HEAD_DIM
    nk = GQA_KV_HEADS * HEAD_DIM
    rep = GQA_Q_HEADS // GQA_KV_HEADS

    def body(g_ref, qg_ref, kg_ref, c_ref, s_ref, qp_ref, kp_ref):
        cos_v, sin_v = c_ref[...], s_ref[...]
        xq = g_ref[:, :nq].astype(F32)
        yq = xq * lax.rsqrt(_head_mean(xq * xq) + EPS) * qg_ref[...]
        yq = (yq * _tile_lanes(cos_v, nq) + _swap_pairs(yq) * _tile_lanes(sin_v, nq)) * SCALE
        lo = _lane_lt64((TM, LANES))
        for p in range(GQA_Q_HEADS // 2):
            g = (2 * p) // rep
            chunk = yq[:, LANES * p:LANES * (p + 1)]
            rolled = _roll64(chunk)
            if g == 0:
                first, second = jnp.where(lo, chunk, 0.0), jnp.where(lo, rolled, 0.0)
            else:
                first, second = jnp.where(lo, 0.0, rolled), jnp.where(lo, 0.0, chunk)
            qp_ref[:, 2 * LANES * p:2 * LANES * p + LANES] = first.astype(BF16)
            qp_ref[:, 2 * LANES * p + LANES:2 * LANES * (p + 1)] = second.astype(BF16)
        xk = g_ref[:, nq:nq + nk].astype(F32)
        yk = xk * lax.rsqrt(_head_mean(xk * xk) + EPS) * kg_ref[...]
        kp_ref[...] = (yk * cos_v + _swap_pairs(yk) * sin_v).astype(BF16)

    return pl.pallas_call(
        body, name="gqa_prep", grid=(T // TM,),
        in_specs=[pl.BlockSpec((TM, gqkv.shape[1]), lambda i: (i, 0)),
                  pl.BlockSpec((1, nq), lambda i: (0, 0)), pl.BlockSpec((1, nk), lambda i: (0, 0)),
                  pl.BlockSpec((TM, nk), lambda i: (i, 0)), pl.BlockSpec((TM, nk), lambda i: (i, 0))],
        out_specs=[pl.BlockSpec((TM, GQA_Q_HEADS * LANES), lambda i: (i, 0)), pl.BlockSpec((TM, nk), lambda i: (i, 0))],
        out_shape=[jax.ShapeDtypeStruct((T, GQA_Q_HEADS * LANES), BF16), jax.ShapeDtypeStruct((T, nk), BF16)],
        compiler_params=_cp(("parallel",)),
    )(gqkv, qgain, kgain, cos_t, sin_t)


def gqa_prep_bwd_call(gqkv, qgain, kgain, cos_t, sin_t, dqp, dkp, dv):
    T = gqkv.shape[0]
    nq = GQA_Q_HEADS * HEAD_DIM
    nk = GQA_KV_HEADS * HEAD_DIM
    rep = GQA_Q_HEADS // GQA_KV_HEADS

    def norm_rope_bwd(x, gain, dy, cos_v, sin_v):
        dyn = dy * cos_v - _swap_pairs(dy) * sin_v
        r = lax.rsqrt(_head_mean(x * x) + EPS)
        xn = x * r
        dxn = dyn * gain
        dx = r * (dxn - xn * _head_mean(dxn * xn))
        return dx, dyn * xn

    def body(g_ref, qg_ref, kg_ref, c_ref, s_ref, dqp_ref, dkp_ref, dv_ref, dg_ref, sq_ref, sk_ref):
        i = pl.program_id(0)
        cos_v, sin_v = c_ref[...], s_ref[...]
        lo = _lane_lt64((TM, LANES))
        chunks = []
        for p in range(GQA_Q_HEADS // 2):
            g = (2 * p) // rep
            da = dqp_ref[:, 2 * LANES * p:2 * LANES * p + LANES]
            db = dqp_ref[:, 2 * LANES * p + LANES:2 * LANES * (p + 1)]
            if g == 0:
                chunks.append(jnp.where(lo, da, _roll64(db)))
            else:
                chunks.append(jnp.where(lo, _roll64(da), db))
        dyq = jnp.concatenate(chunks, axis=1) * SCALE
        dxq, gq_term = norm_rope_bwd(g_ref[:, :nq].astype(F32), qg_ref[...], dyq,
                                     _tile_lanes(cos_v, nq), _tile_lanes(sin_v, nq))
        dxk, gk_term = norm_rope_bwd(g_ref[:, nq:nq + nk].astype(F32), kg_ref[...], dkp_ref[...], cos_v, sin_v)
        dg_ref[:, :nq] = dxq.astype(BF16)
        dg_ref[:, nq:nq + nk] = dxk.astype(BF16)
        dg_ref[:, nq + nk:] = dv_ref[...].astype(BF16)
        _acc(sq_ref, _colsum8(gq_term), i == 0)
        _acc(sk_ref, _colsum8(gk_term), i == 0)

    W = gqkv.shape[1]
    return pl.pallas_call(
        body, name="gqa_prep_bwd", grid=(T // TM,),
        in_specs=[pl.BlockSpec((TM, W), lambda i: (i, 0)),
                  pl.BlockSpec((1, nq), lambda i: (0, 0)), pl.BlockSpec((1, nk), lambda i: (0, 0)),
                  pl.BlockSpec((TM, nk), lambda i: (i, 0)), pl.BlockSpec((TM, nk), lambda i: (i, 0)),
                  pl.BlockSpec((TM, GQA_Q_HEADS * LANES), lambda i: (i, 0)),
                  pl.BlockSpec((TM, nk), lambda i: (i, 0)), pl.BlockSpec((TM, nk), lambda i: (i, 0))],
        out_specs=[pl.BlockSpec((TM, W), lambda i: (i, 0)),
                   pl.BlockSpec((1, 8, nq), lambda i: (0, 0, 0)), pl.BlockSpec((1, 8, nk), lambda i: (0, 0, 0))],
        out_shape=[jax.ShapeDtypeStruct((T, W), BF16), jax.ShapeDtypeStruct((1, 8, nq), F32),
                   jax.ShapeDtypeStruct((1, 8, nk), F32)],
        compiler_params=_cp(("arbitrary",)),
    )(gqkv, qgain, kgain, cos_t, sin_t, dqp, dkp, dv)


GQ_TQ = 256
GQ_TK = 256


def _chunk_offset(j):
    return j * GQ_TK if isinstance(j, int) else pl.multiple_of(j * GQ_TK, GQ_TK)


def _sweep_keys(chunk, is_latent_query, n_lat_k, nk):
    assert n_lat_k % 2 == 0

    @pl.when(is_latent_query)
    def _():
        def two(jj, carry):
            chunk(2 * jj)
            chunk(2 * jj + 1)
            return carry

        lax.fori_loop(0, n_lat_k // 2, two, 0)

    for j in range(n_lat_k, nk):
        chunk(j)


def _gqa_place(chunk, half, g):
    gv = jnp.broadcast_to(g, chunk.shape)
    placed = jnp.where(gv == half, chunk, _roll64(chunk))
    return jnp.where(_lane_half(chunk.shape) == gv, placed, jnp.zeros_like(placed))


def gqa_fwd_call(qp, kp, gqkv, n_lat):
    T = qp.shape[0]
    rep = GQA_Q_HEADS // GQA_KV_HEADS
    nq = T // GQ_TQ
    nk = T // GQ_TK
    n_lat_q = n_lat // GQ_TQ
    n_lat_k = n_lat // GQ_TK
    v_blk = (GQA_Q_HEADS + GQA_KV_HEADS) * HEAD_DIM // LANES
    R = rep * GQ_TQ

    def body(q_ref, k_ref, v_ref, y_ref, lse_ref, m_sc, acc_sc):
        g = pl.program_id(0)
        qb = pl.program_id(1)
        qs = jnp.concatenate([q_ref[:, LANES * i:LANES * (i + 1)] for i in range(rep)], axis=0)
        is_lat = qb < n_lat_q
        nsub = GQ_TK // LANES
        m_sc[...] = jnp.full((R, LANES), NEG, F32)

        def chunk_max(j):
            s = _dot_nt(qs, k_ref[pl.ds(_chunk_offset(j), GQ_TK), :])
            mm = s[:, :LANES]
            for t in range(1, nsub):
                mm = jnp.maximum(mm, s[:, LANES * t:LANES * (t + 1)])
            m_sc[...] = jnp.maximum(m_sc[...], mm)

        _sweep_keys(chunk_max, is_lat, n_lat_k, nk)
        m_sc[...] = jnp.broadcast_to(jnp.max(m_sc[...], axis=-1, keepdims=True), (R, LANES))
        acc_sc[...] = jnp.zeros((R, LANES), F32)
        own_v = _lane_half((GQ_TK, LANES)) == jnp.broadcast_to(g, (GQ_TK, LANES))

        def chunk_acc(j):
            off = _chunk_offset(j)
            vc = v_ref[pl.ds(off, GQ_TK), :]
            v1 = jnp.where(own_v, vc, jnp.ones_like(vc))
            s = _dot_nt(qs, k_ref[pl.ds(off, GQ_TK), :])
            mb = m_sc[...]
            p = jnp.concatenate([jnp.exp(s[:, LANES * t:LANES * (t + 1)] - mb).astype(BF16) for t in range(nsub)],
                                axis=1)
            acc_sc[...] += _dot(p, v1)

        _sweep_keys(chunk_acc, is_lat, n_lat_k, nk)
        acc = acc_sc[...]
        own = _lane_half((R, LANES)) == jnp.broadcast_to(g, (R, LANES))
        l = jnp.where(own, _roll64(acc), acc)
        o = acc / l
        lse = m_sc[...] + jnp.log(l)
        lo = _lane_lt64((GQ_TQ, LANES))
        g0 = jnp.broadcast_to(g, (GQ_TQ, LANES)) == 0
        for c2 in range(rep // 2):
            oa = o[(2 * c2) * GQ_TQ:(2 * c2 + 1) * GQ_TQ]
            ob = o[(2 * c2 + 1) * GQ_TQ:(2 * c2 + 2) * GQ_TQ]
            left = jnp.where(g0, oa, _roll64(oa))
            right = jnp.where(g0, _roll64(ob), ob)
            y_ref[:, LANES * c2:LANES * (c2 + 1)] = jnp.where(lo, left, right).astype(BF16)
        for i in range(rep):
            lse_ref[i] = lse[i * GQ_TQ:(i + 1) * GQ_TQ]

    return pl.pallas_call(
        body, name="gqa_fwd", grid=(GQA_KV_HEADS, nq),
        in_specs=[pl.BlockSpec((GQ_TQ, rep * LANES), lambda g, i: (i, g)),
                  pl.BlockSpec((T, LANES), lambda g, i: (0, 0)),
                  pl.BlockSpec((T, LANES), lambda g, i: (0, v_blk))],
        out_specs=[pl.BlockSpec((GQ_TQ, rep * HEAD_DIM), lambda g, i: (i, g)),
                   pl.BlockSpec((rep, GQ_TQ, LANES), lambda g, i: (g, i, 0))],
        out_shape=[jax.ShapeDtypeStruct((T, GQA_Q_HEADS * HEAD_DIM), BF16),
                   jax.ShapeDtypeStruct((GQA_Q_HEADS, T, LANES), F32)],
        scratch_shapes=[pltpu.VMEM((R, LANES), F32), pltpu.VMEM((R, LANES), F32)],
        compiler_params=_cp(("parallel", "parallel"), VMEM_BIG),
    )(qp, kp, gqkv)


def gqa_bwd_call(qp, kp, gqkv, yb, dyb, lse, n_lat):
    T = qp.shape[0]
    rep = GQA_Q_HEADS // GQA_KV_HEADS
    nq = T // GQ_TQ
    nk = T // GQ_TK
    n_lat_q = n_lat // GQ_TQ
    n_lat_k = n_lat // GQ_TK
    v_blk = (GQA_Q_HEADS + GQA_KV_HEADS) * HEAD_DIM // LANES
    R = rep * GQ_TQ

    def body(q_ref, k_ref, v_ref, y_ref, dy_ref, lse_ref, dq_ref, dk_ref, dv_ref, dq_sc):
        g = pl.program_id(0)
        qb = pl.program_id(1)

        @pl.when((g == 0) & (qb == 0))
        def _():
            dk_ref[...] = jnp.zeros((T, LANES), F32)
            dv_ref[...] = jnp.zeros((T, LANES), F32)

        qs = jnp.concatenate([q_ref[:, LANES * i:LANES * (i + 1)] for i in range(rep)], axis=0)
        dos, deltas, lses = [], [], []
        for i in range(rep):
            c2, half = i // 2, i % 2
            dch = _gqa_place(dy_ref[:, LANES * c2:LANES * (c2 + 1)].astype(F32), half, g)
            ych = _gqa_place(y_ref[:, LANES * c2:LANES * (c2 + 1)].astype(F32), half, g)
            dos.append(dch.astype(BF16))
            deltas.append(jnp.sum(dch * ych, axis=-1, keepdims=True))
            lses.append(lse_ref[i])
        do_s = jnp.concatenate(dos, axis=0)
        delta = jnp.broadcast_to(jnp.concatenate(deltas, axis=0), (R, LANES))
        lse_v = jnp.concatenate(lses, axis=0)
        dq_sc[...] = jnp.zeros((R, LANES), F32)
        nsub = GQ_TK // LANES

        def chunk(j):
            off = _chunk_offset(j)
            kc = k_ref[pl.ds(off, GQ_TK), :]
            vc = v_ref[pl.ds(off, GQ_TK), :]
            s = _dot_nt(qs, kc)
            dp = _dot_nt(do_s, vc)
            ps, dss = [], []
            for t in range(nsub):
                pt = jnp.exp(s[:, LANES * t:LANES * (t + 1)] - lse_v)
                ps.append(pt.astype(BF16))
                dss.append((pt * (dp[:, LANES * t:LANES * (t + 1)] - delta)).astype(BF16))
            p = jnp.concatenate(ps, axis=1)
            ds = jnp.concatenate(dss, axis=1)
            dv_ref[pl.ds(off, GQ_TK), :] += _dot_tn(p, do_s)
            dk_ref[pl.ds(off, GQ_TK), :] += _dot_tn(ds, qs)
            dq_sc[...] += _dot(ds, kc)

        _sweep_keys(chunk, qb < n_lat_q, n_lat_k, nk)
        for i in range(rep):
            dq_ref[:, LANES * i:LANES * (i + 1)] = dq_sc[i * GQ_TQ:(i + 1) * GQ_TQ, :]

    whole = pl.BlockSpec((T, LANES), lambda g, i: (0, 0))
    return pl.pallas_call(
        body, name="gqa_bwd", grid=(GQA_KV_HEADS, nq),
        in_specs=[pl.BlockSpec((GQ_TQ, rep * LANES), lambda g, i: (i, g)),
                  whole,
                  pl.BlockSpec((T, LANES), lambda g, i: (0, v_blk)),
                  pl.BlockSpec((GQ_TQ, rep * HEAD_DIM), lambda g, i: (i, g)),
                  pl.BlockSpec((GQ_TQ, rep * HEAD_DIM), lambda g, i: (i, g)),
                  pl.BlockSpec((rep, GQ_TQ, LANES), lambda g, i: (g, i, 0))],
        out_specs=[pl.BlockSpec((GQ_TQ, rep * LANES), lambda g, i: (i, g)), whole, whole],
        out_shape=[jax.ShapeDtypeStruct((T, GQA_Q_HEADS * LANES), F32),
                   jax.ShapeDtypeStruct((T, LANES), F32), jax.ShapeDtypeStruct((T, LANES), F32)],
        scratch_shapes=[pltpu.VMEM((R, LANES), F32)],
        compiler_params=_cp(("arbitrary", "arbitrary"), VMEM_BIG),
    )(qp, kp, gqkv, yb, dyb, lse)


def _na_tables(rows):
    kh = min(NA_WIN_H, rows)
    assert kh == NA_WIN_H and rows >= NA_KROWS and rows % NA_ROWS == 0
    a = np.zeros((3, NA_ROWS, NA_KROWS, 2 * NA_WIN_H - 1), np.float32)
    for v, r0 in enumerate((0, NA_ROWS, rows - NA_ROWS)):
        ks = min(max(r0 - NA_WIN_H // 2, 0), rows - NA_KROWS)
        for rq in range(NA_ROWS):
            r = r0 + rq
            rs = min(max(r - kh // 2, 0), rows - kh)
            for rk in range(NA_KROWS):
                kr = ks + rk
                if rs <= kr < rs + kh:
                    a[v, rq, rk, kr - r + NA_WIN_H - 1] = 1.0
    c = np.zeros((GRID_W, GRID_W, 2 * NA_WIN_W - 1), np.float32)
    for wq in range(GRID_W):
        cs = min(max(wq - NA_WIN_W // 2, 0), GRID_W - NA_WIN_W)
        for wk in range(cs, cs + NA_WIN_W):
            c[wq, wk, min(max(wk - wq, -(NA_WIN_W - 1)), NA_WIN_W - 1) + NA_WIN_W - 1] = 1.0
    return a, c


def na_bias_table(rpb, rows):
    a, c = _na_tables(rows)
    hp = lax.Precision.HIGHEST
    t1 = jnp.einsum("hde,wue->hdwu", rpb, jnp.asarray(c), precision=hp)
    b = jnp.einsum("hdwu,vqkd->hvqwku", t1, jnp.asarray(a), precision=hp)
    ok = np.einsum("vqk,wu->vqwku", a.sum(-1), c.sum(-1)) > 0
    b = jnp.where(jnp.asarray(ok)[None], b, NEG)
    return b.reshape(rpb.shape[0], 3, NA_ROWS * GRID_W, NA_KROWS * GRID_W)


def na_bias_grad(dbias, rows):
    a, c = _na_tables(rows)
    hp = lax.Precision.HIGHEST
    d = dbias.reshape(dbias.shape[0], 3, NA_ROWS, GRID_W, NA_KROWS, GRID_W)
    t1 = jnp.einsum("hvqwku,vqkd->hdwu", d, jnp.asarray(a), precision=hp)
    return jnp.einsum("hdwu,wue->hde", t1, jnp.asarray(c), precision=hp)


def _na_variant(blk, n_lat_blk):
    return jnp.where(blk == 0, 0, jnp.where(blk >= n_lat_blk - 1, 2, 1))


def _na_kstart(blk, rows):
    r0 = blk * NA_ROWS
    ks = jnp.clip(r0 - NA_WIN_H // 2, 0, rows - NA_KROWS)
    return pl.multiple_of(ks * GRID_W, GRID_W)


def na_fwd_call(naqkv, bias, n_lat):
    T = naqkv.shape[0]
    nb = T // TM
    n_lat_blk = n_lat // TM
    rows = n_lat // GRID_W
    n_ctx = T - n_lat
    KW = NA_KROWS * GRID_W
    npair = NA_HEADS // 2

    def body(q_ref, k_ref, v_ref, b_ref, y_ref, lse_ref):
        blk = pl.program_id(1)
        half = _lane_half((TM, LANES))
        lo = half == 0
        q = q_ref[...].astype(F32) * SCALE
        kctx = k_ref[pl.ds(n_lat, n_ctx), :]
        vctx = v_ref[pl.ds(n_lat, n_ctx), :]

        @pl.when(blk < n_lat_blk)
        def _():
            off = _na_kstart(blk, rows)
            kwin = k_ref[pl.ds(off, KW), :]
            vwin = v_ref[pl.ds(off, KW), :]
            outs = []
            for hh in range(2):
                qm = jnp.where(half == hh, q, 0.0).astype(BF16)
                sw = _dot_nt(qm, kwin) + b_ref[hh, 0]
                sc = _dot_nt(qm, kctx)
                m = jnp.maximum(jnp.max(sw, axis=-1, keepdims=True), jnp.max(sc, axis=-1, keepdims=True))
                pw = jnp.exp(sw - m)
                pc = jnp.exp(sc - m)
                l = jnp.sum(pw, axis=-1, keepdims=True) + jnp.sum(pc, axis=-1, keepdims=True)
                outs.append((_dot(pw.astype(BF16), vwin) + _dot(pc.astype(BF16), vctx)) / l)
                lse_ref[hh] = jnp.broadcast_to(m + jnp.log(l), (TM, LANES))
            y_ref[...] = jnp.where(lo, outs[0], outs[1]).astype(BF16)

        @pl.when(blk >= n_lat_blk)
        def _():
            outs = []
            for hh in range(2):
                qm = jnp.where(half == hh, q, 0.0).astype(BF16)
                sc = _dot_nt(qm, kctx)
                m = jnp.max(sc, axis=-1, keepdims=True)
                pc = jnp.exp(sc - m)
                l = jnp.sum(pc, axis=-1, keepdims=True)
                outs.append(_dot(pc.astype(BF16), vctx) / l)
                lse_ref[hh] = jnp.broadcast_to(m + jnp.log(l), (TM, LANES))
            y_ref[...] = jnp.where(lo, outs[0], outs[1]).astype(BF16)

    return pl.pallas_call(
        body, name="na_fwd", grid=(npair, nb),
        in_specs=[pl.BlockSpec((TM, LANES), lambda p, b: (b, p)),
                  pl.BlockSpec((T, LANES), lambda p, b: (0, npair + p)),
                  pl.BlockSpec((T, LANES), lambda p, b: (0, 2 * npair + p)),
                  pl.BlockSpec((2, 1, TM, KW), lambda p, b: (p, _na_variant(b, n_lat_blk), 0, 0))],
        out_specs=[pl.BlockSpec((TM, LANES), lambda p, b: (b, p)),
                   pl.BlockSpec((2, TM, LANES), lambda p, b: (p, b, 0))],
        out_shape=[jax.ShapeDtypeStruct((T, NA_HEADS * HEAD_DIM), BF16),
                   jax.ShapeDtypeStruct((NA_HEADS, T, LANES), F32)],
        compiler_params=_cp(("parallel", "parallel"), VMEM_BIG),
    )(naqkv, naqkv, naqkv, bias)


def na_bwd_call(naqkv, bias, ya, dya, lse, n_lat):
    T = naqkv.shape[0]
    nb = T // TM
    n_lat_blk = n_lat // TM
    rows = n_lat // GRID_W
    n_ctx = T - n_lat
    KW = NA_KROWS * GRID_W
    npair = NA_HEADS // 2

    def body(q_ref, k_ref, v_ref, b_ref, y_ref, dy_ref, lse_ref, dq_ref, dk_ref, dv_ref, db_ref, dk_sc, dv_sc):
        blk = pl.program_id(1)
        half = _lane_half((TM, LANES))
        lo = half == 0
        q = q_ref[...].astype(F32) * SCALE
        kctx = k_ref[pl.ds(n_lat, n_ctx), :]
        vctx = v_ref[pl.ds(n_lat, n_ctx), :]
        dyv = dy_ref[...].astype(F32)
        yv = y_ref[...].astype(F32)

        @pl.when(blk == 0)
        def _():
            dk_sc[...] = jnp.zeros((T, LANES), F32)
            dv_sc[...] = jnp.zeros((T, LANES), F32)

        @pl.when(blk < n_lat_blk)
        def _():
            off = _na_kstart(blk, rows)
            kwin = k_ref[pl.ds(off, KW), :]
            vwin = v_ref[pl.ds(off, KW), :]
            first = (blk == 0) | (blk == 1) | (blk == n_lat_blk - 1)
            dqs = []
            for hh in range(2):
                sel = half == hh
                qm = jnp.where(sel, q, 0.0).astype(BF16)
                dom = jnp.where(sel, dyv, 0.0)
                delta = jnp.sum(dom * yv, axis=-1, keepdims=True)
                dom = dom.astype(BF16)
                lse_v = lse_ref[hh][:, 0:1]
                pw = jnp.exp(_dot_nt(qm, kwin) + b_ref[hh, 0] - lse_v)
                pc = jnp.exp(_dot_nt(qm, kctx) - lse_v)
                dsw = pw * (_dot_nt(dom, vwin) - delta)
                dsc = pc * (_dot_nt(dom, vctx) - delta)

                @pl.when(first)
                def _():
                    db_ref[hh, 0] = dsw

                @pl.when(jnp.logical_not(first))
                def _():
                    db_ref[hh, 0] += dsw

                dsw = dsw.astype(BF16)
                dsc = dsc.astype(BF16)
                dv_sc[pl.ds(off, KW), :] += _dot_tn(pw.astype(BF16), dom)
                dv_sc[pl.ds(n_lat, n_ctx), :] += _dot_tn(pc.astype(BF16), dom)
                dk_sc[pl.ds(off, KW), :] += _dot_tn(dsw, qm)
                dk_sc[pl.ds(n_lat, n_ctx), :] += _dot_tn(dsc, qm)
                dqs.append((_dot(dsw, kwin) + _dot(dsc, kctx)) * SCALE)
            dq_ref[...] = jnp.where(lo, dqs[0], dqs[1]).astype(BF16)

        @pl.when(blk >= n_lat_blk)
        def _():
            dqs = []
            for hh in range(2):
                sel = half == hh
                qm = jnp.where(sel, q, 0.0).astype(BF16)
                dom = jnp.where(sel, dyv, 0.0)
                delta = jnp.sum(dom * yv, axis=-1, keepdims=True)
                dom = dom.astype(BF16)
                pc = jnp.exp(_dot_nt(qm, kctx) - lse_ref[hh][:, 0:1])
                dsc = (pc * (_dot_nt(dom, vctx) - delta)).astype(BF16)
                dv_sc[pl.ds(n_lat, n_ctx), :] += _dot_tn(pc.astype(BF16), dom)
                dk_sc[pl.ds(n_lat, n_ctx), :] += _dot_tn(dsc, qm)
                dqs.append(_dot(dsc, kctx) * SCALE)
            dq_ref[...] = jnp.where(lo, dqs[0], dqs[1]).astype(BF16)

        @pl.when(blk == nb - 1)
        def _():
            dk_ref[...] = dk_sc[...].astype(BF16)
            dv_ref[...] = dv_sc[...].astype(BF16)

    blkspec = pl.BlockSpec((TM, LANES), lambda p, b: (b, p))
    col = pl.BlockSpec((T, LANES), lambda p, b: (0, p))
    bspec = pl.BlockSpec((2, 1, TM, KW), lambda p, b: (p, _na_variant(b, n_lat_blk), 0, 0))
    dq, dk, dv, db = pl.pallas_call(
        body, name="na_bwd", grid=(npair, nb),
        in_specs=[blkspec,
                  pl.BlockSpec((T, LANES), lambda p, b: (0, npair + p)),
                  pl.BlockSpec((T, LANES), lambda p, b: (0, 2 * npair + p)),
                  bspec, blkspec, blkspec,
                  pl.BlockSpec((2, TM, LANES), lambda p, b: (p, b, 0))],
        out_specs=[blkspec, col, col, bspec],
        out_shape=[jax.ShapeDtypeStruct((T, NA_HEADS * HEAD_DIM), BF16)] * 3
        + [jax.ShapeDtypeStruct((NA_HEADS, 3, TM, KW), F32)],
        scratch_shapes=[pltpu.VMEM((T, LANES), F32), pltpu.VMEM((T, LANES), F32)],
        compiler_params=_cp(("arbitrary", "arbitrary"), VMEM_BIG),
    )(naqkv, naqkv, naqkv, bias, ya, dya, lse)
    return jnp.concatenate([dq, dk, dv], axis=1), db


def rope_tables(n_lat, n_ctx):
    t = jnp.arange(n_lat)
    row = (t // GRID_W).astype(F32)
    col = (t % GRID_W).astype(F32)
    half = HEAD_DIM // 2
    inv = ROPE_THETA ** (-jnp.arange(0, half, 2, dtype=F32) / half)
    ang = jnp.concatenate([row[:, None] * inv, col[:, None] * inv], axis=-1)
    cos = jnp.repeat(jnp.cos(ang), 2, axis=-1)
    sin = jnp.repeat(jnp.sin(ang), 2, axis=-1) * jnp.tile(jnp.array([-1.0, 1.0], F32), half)
    cos = jnp.concatenate([cos, jnp.ones((n_ctx, HEAD_DIM), F32)], axis=0)
    sin = jnp.concatenate([sin, jnp.zeros((n_ctx, HEAD_DIM), F32)], axis=0)
    return jnp.tile(cos, (1, 2)), jnp.tile(sin, (1, 2))


def _row(v):
    return v.reshape(1, -1)


def device_fwd_bwd(xs, target, modv, w, small, n_lat):
    T, D = xs.shape
    n_ctx = T - n_lat
    L = modv.shape[0]
    nlt = n_lat // TM
    rows = n_lat // GRID_W
    cos_t, sin_t = rope_tables(n_lat, n_ctx)
    qg = [jnp.tile(_row(small["q_gain"][l]), (1, GQA_Q_HEADS)) for l in range(L)]
    kg = [jnp.tile(_row(small["k_gain"][l]), (1, GQA_KV_HEADS)) for l in range(L)]
    bias = [na_bias_table(small["na_rpb"][l], rows) for l in range(L)]

    saved = []
    x = xs
    h = norm_mod_call(x, _row(small["norm1"][0]), modv[0], nlt, 0)
    for l in range(L):
        naqkv, gqkv, gates = qkv_call(h, w["w_in"][l])
        ya, lse_a = na_fwd_call(naqkv, bias[l], n_lat)
        qp, kp = gqa_prep_call(gqkv, qg[l], kg[l], cos_t, sin_t)
        yb, lse_b = gqa_fwd_call(qp, kp, gqkv, n_lat)
        merged, pa, pb = merge_call(ya, yb, gates, w["w_pa"][l], w["w_pb"][l])
        x1, osave, h2 = wo_call(merged, w["w_o"][l], x, modv[l], _row(small["norm2"][l]), nlt)
        act, a, u = ffn_in_call(h2, w["w_ffn_in"][l])
        if l + 1 < L:
            x2, fsave, hn = ffn_out_call(act, w["w_ffn_out"][l], x1, modv[l], nlt,
                                         _row(small["norm1"][l + 1]), modv[l + 1])
        else:
            x2, fsave = ffn_out_call(act, w["w_ffn_out"][l], x1, modv[l], nlt)
            hn = None
        saved.append(dict(x=x, h=h, naqkv=naqkv, gqkv=gqkv, gates=gates, ya=ya, lse_a=lse_a, qp=qp, kp=kp,
                          yb=yb, lse_b=lse_b, merged=merged, pa=pa, pb=pb, x1=x1, osave=osave, h2=h2,
                          act=act, a=a, u=u, fsave=fsave))
        x, h = x2, hn

    dx, d_final, loss_part = loss_call(x, _row(small["final_norm"]), target, nlt)

    gw = {k: [None] * L for k in ("w_in", "w_pa", "w_pb", "w_o", "w_ffn_in", "w_ffn_out")}
    gs = {k: [None] * L for k in ("norm1", "norm2", "na_rpb", "q_gain", "k_gain")}
    dmod = [None] * L
    fold = lambda s: s.sum(axis=1)
    for l in reversed(range(L)):
        s = saved[l]
        da, du, df, s_g2 = dact_call(dx, modv[l], s["fsave"], w["w_ffn_out"][l], s["a"], s["u"], nlt)
        gw["w_ffn_out"][l] = wgrad_call(s["act"], df, "wgrad_ffn_out")
        dx1, s_sh2, s_sc2, s_n2 = dh2_call(da, du, w["w_ffn_in"][l], s["x1"], modv[l], _row(small["norm2"][l]), dx, nlt)
        gw["w_ffn_in"][l] = jnp.concatenate(
            [wgrad_call(s["h2"], da, "wgrad_ffn_in_a"), wgrad_call(s["h2"], du, "wgrad_ffn_in_u")], axis=1)
        dpa, dpb, dgates, do, s_g1 = dmerged_call(dx1, modv[l], s["osave"], w["w_o"][l], s["pa"], s["pb"], s["gates"], nlt)
        gw["w_o"][l] = wgrad_call(s["merged"], do, "wgrad_o")
        dya, dyb = dy_call(dpa, dpb, w["w_pa"][l], w["w_pb"][l])
        gw["w_pa"][l] = wgrad_call(s["ya"], dpa, "wgrad_pa")
        gw["w_pb"][l] = wgrad_call(s["yb"], dpb, "wgrad_pb")
        dna, dbias = na_bwd_call(s["naqkv"], bias[l], s["ya"], dya, s["lse_a"], n_lat)
        gs["na_rpb"][l] = na_bias_grad(dbias, rows)
        dqp, dkp, dv = gqa_bwd_call(s["qp"], s["kp"], s["gqkv"], s["yb"], dyb, s["lse_b"], n_lat)
        dg, s_qg, s_kg = gqa_prep_bwd_call(s["gqkv"], qg[l], kg[l], cos_t, sin_t, dqp, dkp, dv)
        gs["q_gain"][l] = s_qg.reshape(8 * GQA_Q_HEADS, HEAD_DIM).sum(axis=0)
        gs["k_gain"][l] = s_kg.reshape(8 * GQA_KV_HEADS, HEAD_DIM).sum(axis=0)
        dx, s_sh1, s_sc1, s_n1 = dh_call(dna, dg, dgates, w["w_in"][l], s["x"], modv[l], _row(small["norm1"][l]), dx1, nlt)
        gw["w_in"][l] = jnp.concatenate(
            [wgrad_call(s["h"], dna, "wgrad_in_na"), wgrad_call(s["h"], dg, "wgrad_in_g"),
             wgrad_call(s["h"], dgates, "wgrad_in_gate")], axis=1)
        gs["norm1"][l] = fold(s_n1).sum(axis=0)
        gs["norm2"][l] = fold(s_n2).sum(axis=0)
        dmod[l] = jnp.stack([fold(s_sh1), fold(s_sc1), fold(s_g1), fold(s_sh2), fold(s_sc2), fold(s_g2)], axis=1)

    gw = {k: jnp.stack(v) for k, v in gw.items()}
    gs = {k: jnp.stack(v) for k, v in gs.items()}
    gs["final_norm"] = d_final.sum(axis=(0, 1))
    return loss_part, dx, gw, jnp.stack(dmod), gs


N_DEV = 8
N_CHIP = 4
ANY = pl.BlockSpec(memory_space=pl.ANY)


def _place():
    x, y, c = lax.axis_index("x"), lax.axis_index("y"), lax.axis_index("c")
    chips = [(1 - x, y), (x, 1 - y), (1 - x, 1 - y)]
    return x, y, c, chips


def small_all_gather(v, name):
    m_per, n = v.shape
    assert m_per % 8 == 0

    def body(x_ref, out_ref, send_sems, recv_sems, local_sem):
        x, y, c, chips = _place()
        me, sibling = (x, y, c), (x, y, 1 - c)

        def rows(px, py, pc):
            return out_ref.at[pl.ds(pl.multiple_of((4 * px + 2 * py + pc) * m_per, 8), m_per), :]

        def copy(k, block, to, src=None):
            return pltpu.make_async_remote_copy(
                src_ref=rows(*block) if src is None else src, dst_ref=rows(*block),
                send_sem=send_sems.at[k], recv_sem=recv_sems.at[k], device_id=to, device_id_type=MESH)

        mine = pltpu.make_async_copy(x_ref, rows(*me), local_sem)
        mine.start()
        first = [copy(0, me, sibling, src=x_ref)]
        first += [copy(1 + j, me, (*chip, c), src=x_ref) for j, chip in enumerate(chips)]
        for cp in first:
            cp.start()
        passed = [copy(4 + j, (*chip, c), sibling) for j, chip in enumerate(chips)]
        for j, chip in enumerate(chips):
            copy(1 + j, (*chip, c), me).wait_recv()
            passed[j].start()
        copy(0, sibling, me).wait_recv()
        for j, chip in enumerate(chips):
            copy(4 + j, (*chip, 1 - c), me).wait_recv()
        for cp in first + passed:
            cp.wait_send()
        mine.wait()

    out = pl.pallas_call(
        body, name=name,
        out_shape=jax.ShapeDtypeStruct((N_DEV * m_per, n), v.dtype),
        in_specs=[pl.BlockSpec(memory_space=pltpu.VMEM)],
        out_specs=pl.BlockSpec(memory_space=pltpu.VMEM),
        scratch_shapes=[pltpu.SemaphoreType.DMA((7,)), pltpu.SemaphoreType.DMA((7,)), pltpu.SemaphoreType.DMA],
    )(v)
    return out.reshape(N_DEV, m_per, n)


def weights_all_gather(packed):
    _, rh, ncol = packed.shape

    def body(in_ref, out_ref, send_sems, recv_sems):
        x, y, c, chips = _place()
        sibling = (x, y, 1 - c)

        def blk(px, py, half):
            return out_ref.at[2 * px + py, half]

        def copy(k, block, to, src=None):
            return pltpu.make_async_remote_copy(
                src_ref=blk(*block) if src is None else src, dst_ref=blk(*block),
                send_sem=send_sems.at[k], recv_sem=recv_sems.at[k], device_id=to, device_id_type=MESH)

        first = [copy(j, (x, y, c), (*chip, c), src=in_ref.at[c]) for j, chip in enumerate(chips)]
        for cp in first:
            cp.start()
        passed = [copy(3 + j, (*chip, c), sibling) for j, chip in enumerate(chips)]
        for j, chip in enumerate(chips):
            copy(j, (*chip, c), (x, y, c)).wait_recv()
            passed[j].start()
        for j, chip in enumerate(chips):
            copy(3 + j, (*chip, 1 - c), (x, y, c)).wait_recv()
        for cp in first + passed:
            cp.wait_send()

    out = pl.pallas_call(
        body, name="weights_all_gather",
        out_shape=jax.ShapeDtypeStruct((N_CHIP, 2, rh, ncol), packed.dtype),
        in_specs=[ANY], out_specs=ANY,
        scratch_shapes=[pltpu.SemaphoreType.DMA((6,)), pltpu.SemaphoreType.DMA((6,))],
    )(packed)
    chip = 2 * lax.axis_index("x") + lax.axis_index("y")
    return lax.dynamic_update_slice(out, packed[None], (chip, 0, 0, 0))


def grads_swap_halves(gp):
    _, _, rh, ncol = gp.shape

    def body(gp_ref, r1_ref, send_sem, recv_sem):
        x, y, c, _ = _place()
        cp = pltpu.make_async_remote_copy(src_ref=gp_ref.at[1 - c], dst_ref=r1_ref, send_sem=send_sem,
                                          recv_sem=recv_sem, device_id=(x, y, 1 - c), device_id_type=MESH)
        cp.start()
        cp.wait()

    return pl.pallas_call(
        body, name="grads_swap_halves", out_shape=jax.ShapeDtypeStruct((N_CHIP, rh, ncol), gp.dtype),
        in_specs=[ANY], out_specs=ANY,
        scratch_shapes=[pltpu.SemaphoreType.DMA, pltpu.SemaphoreType.DMA],
    )(gp)


def grads_to_owner(p1):
    _, rh, ncol = p1.shape

    def body(p1_ref, r2_ref, send_sems, recv_sems):
        x, y, c, chips = _place()
        cps = [pltpu.make_async_remote_copy(src_ref=p1_ref.at[2 * px + py], dst_ref=r2_ref.at[j],
                                            send_sem=send_sems.at[j], recv_sem=recv_sems.at[j],
                                            device_id=(px, py, c), device_id_type=MESH)
               for j, (px, py) in enumerate(chips)]
        for cp in cps:
            cp.start()
        for cp in cps:
            cp.wait()

    return pl.pallas_call(
        body, name="grads_to_owner", out_shape=jax.ShapeDtypeStruct((3, rh, ncol), p1.dtype),
        in_specs=[ANY], out_specs=ANY,
        scratch_shapes=[pltpu.SemaphoreType.DMA((3,)), pltpu.SemaphoreType.DMA((3,))],
    )(p1)


def grads_join_halves(gh):
    rh, ncol = gh.shape

    def body(gh_ref, out_ref, send_sem, recv_sem):
        x, y, c, _ = _place()
        cp = pltpu.make_async_remote_copy(src_ref=gh_ref, dst_ref=out_ref, send_sem=send_sem,
                                          recv_sem=recv_sem, device_id=(x, y, 1 - c), device_id_type=MESH)
        cp.start()
        cp.wait()

    other = pl.pallas_call(
        body, name="grads_join_halves", out_shape=jax.ShapeDtypeStruct((rh, ncol), gh.dtype),
        in_specs=[ANY], out_specs=ANY,
        scratch_shapes=[pltpu.SemaphoreType.DMA, pltpu.SemaphoreType.DMA],
    )(gh)
    c0 = lax.axis_index("c") == 0
    return jnp.stack([jnp.where(c0, gh, other), jnp.where(c0, other, gh)])


PACK_C = 1024


def add_halves_call(gp, r1, c):
    _, n, rh, ncol = gp.shape
    tr = _pick_tile(rh, 512, 16)

    def body(c_ref, a_ref, b_ref, o_ref):
        o_ref[0] = (a_ref[0, 0].astype(F32) + b_ref[0].astype(F32)).astype(BF16)

    return pl.pallas_call(
        body, name="add_halves",
        grid_spec=pltpu.PrefetchScalarGridSpec(
            num_scalar_prefetch=1, grid=(n, rh // tr),
            in_specs=[pl.BlockSpec((1, 1, tr, ncol), lambda k, i, c_ref: (c_ref[0], k, i, 0)),
                      pl.BlockSpec((1, tr, ncol), lambda k, i, c_ref: (k, i, 0))],
            out_specs=pl.BlockSpec((1, tr, ncol), lambda k, i, c_ref: (k, i, 0))),
        out_shape=jax.ShapeDtypeStruct((n, rh, ncol), BF16),
        compiler_params=_cp(("parallel", "parallel")),
    )(jnp.reshape(c, (1,)).astype(jnp.int32), gp, r1)


def sum_owner_call(p1, r2, k):
    _, rh, ncol = p1.shape
    tr = _pick_tile(rh, 512, 16)

    def body(k_ref, a_ref, b0_ref, b1_ref, b2_ref, o_ref):
        o_ref[...] = (((a_ref[0].astype(F32) + b0_ref[0].astype(F32)) + b1_ref[0].astype(F32))
                      + b2_ref[0].astype(F32))

    other = lambda j: pl.BlockSpec((1, tr, ncol), lambda i, k_ref: (j, i, 0))
    return pl.pallas_call(
        body, name="sum_owner",
        grid_spec=pltpu.PrefetchScalarGridSpec(
            num_scalar_prefetch=1, grid=(rh // tr,),
            in_specs=[pl.BlockSpec((1, tr, ncol), lambda i, k_ref: (k_ref[0], i, 0)), other(0), other(1), other(2)],
            out_specs=pl.BlockSpec((tr, ncol), lambda i, k_ref: (i, 0))),
        out_shape=jax.ShapeDtypeStruct((rh, ncol), F32),
        compiler_params=_cp(("parallel",)),
    )(jnp.reshape(k, (1,)).astype(jnp.int32), p1, r2, r2, r2)


def _silu(x):
    return x * _sigmoid(x)


def mod_matmul_call(craw, w_mod):
    L, D, N = w_mod.shape

    def body(c_ref, w_ref, o_ref):
        o_ref[0] = _dot(_silu(c_ref[...]).astype(BF16), w_ref[0].astype(BF16))

    return pl.pallas_call(
        body, name="mod_matmul", grid=(L,),
        in_specs=[pl.BlockSpec((16, D), lambda l: (0, 0)), pl.BlockSpec((1, D, N), lambda l: (l, 0, 0))],
        out_specs=pl.BlockSpec((1, 16, N), lambda l: (l, 0, 0)),
        out_shape=jax.ShapeDtypeStruct((L, 16, N), F32),
        compiler_params=_cp(("parallel",), VMEM_BIG),
    )(craw, w_mod)


def mod_bwd_call(craw, dmod16, w_mod):
    L, D, N = w_mod.shape

    def body(c_ref, d_ref, w_ref, gw_ref, ds_ref):
        l = pl.program_id(0)
        d = d_ref[0].astype(BF16)
        gw_ref[0] = _dot_tn(_silu(c_ref[...]).astype(BF16), d)
        part = _dot_nt(d, w_ref[0].astype(BF16))

        @pl.when(l == 0)
        def _():
            ds_ref[...] = part

        @pl.when(l > 0)
        def _():
            ds_ref[...] += part

    return pl.pallas_call(
        body, name="mod_bwd", grid=(L,),
        in_specs=[pl.BlockSpec((16, D), lambda l: (0, 0)), pl.BlockSpec((1, 16, N), lambda l: (l, 0, 0)),
                  pl.BlockSpec((1, D, N), lambda l: (l, 0, 0))],
        out_specs=[pl.BlockSpec((1, D, N), lambda l: (l, 0, 0)), pl.BlockSpec((16, D), lambda l: (0, 0))],
        out_shape=[jax.ShapeDtypeStruct((L, D, N), F32), jax.ShapeDtypeStruct((16, D), F32)],
        compiler_params=_cp(("arbitrary",), VMEM_BIG),
    )(craw, dmod16, w_mod)


def _adamw(w, g, m, v):
    m2 = ADAM_B1 * m + (1.0 - ADAM_B1) * g
    v2 = ADAM_B2 * v + (1.0 - ADAM_B2) * (g * g)
    m_hat = m2 / (1.0 - ADAM_B1 ** ADAM_STEP)
    v_hat = v2 / (1.0 - ADAM_B2 ** ADAM_STEP)
    delta = -ADAM_LR * (m_hat / (jnp.sqrt(v_hat) + ADAM_EPS) + ADAM_WD * w)
    return delta, m2, v2


def adamw_call(w, g, m, v):
    shape = w.shape
    ncol = shape[-1]
    r = math.prod(shape[:-1])
    tr = _pick_tile(r, 512, 8)
    as2d = lambda t: t.reshape(r, ncol)

    def body(w_ref, g_ref, m_ref, v_ref, d_ref, m2_ref, v2_ref):
        d_ref[...], m2_ref[...], v2_ref[...] = _adamw(w_ref[...], g_ref[...], m_ref[...], v_ref[...])

    spec = pl.BlockSpec((tr, ncol), lambda i: (i, 0))
    outs = pl.pallas_call(
        body, name="adamw", grid=(r // tr,), in_specs=[spec] * 4, out_specs=[spec] * 3,
        out_shape=[jax.ShapeDtypeStruct((r, ncol), F32)] * 3,
        compiler_params=_cp(("parallel",)),
    )(as2d(w), as2d(g), as2d(m), as2d(v))
    return [o.reshape(shape) for o in outs]


def adamw_small_call(parts, w, m, v, silu_bwd=False):
    P, R, C = parts.shape

    def body(p_ref, w_ref, m_ref, v_ref, g_ref, d_ref, m2_ref, v2_ref):
        g = p_ref[0]
        for i in range(1, P):
            g = g + p_ref[i]
        wv = w_ref[...]
        if silu_bwd:
            sg = _sigmoid(wv)
            g = g * (sg * (1.0 + wv * (1.0 - sg)))
        g_ref[...] = g
        d_ref[...], m2_ref[...], v2_ref[...] = _adamw(wv, g, m_ref[...], v_ref[...])

    return pl.pallas_call(
        body, name="adamw_small", out_shape=[jax.ShapeDtypeStruct((R, C), F32)] * 4,
    )(parts, w, m, v)


def sum_parts_call(parts):
    P, R, C = parts.shape

    def body(p_ref, o_ref):
        g = p_ref[0]
        for i in range(1, P):
            g = g + p_ref[i]
        o_ref[...] = g

    return pl.pallas_call(body, name="sum_parts", out_shape=jax.ShapeDtypeStruct((R, C), F32))(parts)


BIG = (("w_in", "col"), ("w_pa", "col"), ("w_pb", "col"), ("w_o", "row"), ("w_ffn_in", "col"), ("w_ffn_out", "row"))
SMALL = ("norm1", "norm2", "final_norm", "na_rpb", "q_gain", "k_gain")


def _pack_shards(shards):
    parts = [shards[n].reshape(2, -1, PACK_C) for n, _ in BIG]
    return jnp.concatenate(parts, axis=1)


def _unpack_shards(packed, shapes):
    out, r0 = {}, 0
    lead = packed.shape[:-3]
    for n, _ in BIG:
        L, K, N = shapes[n]
        rw = (L // 2) * K * N // PACK_C
        out[n] = packed[..., r0:r0 + rw, :].reshape(lead + (L, K, N))
        r0 += rw
    return out


def _whole_from_chips(g, kind):
    _, L, K, N = g.shape
    if kind == "col":
        return g.transpose(1, 2, 0, 3).reshape(L, K, N_CHIP * N)
    return g.transpose(1, 0, 2, 3).reshape(L, N_CHIP * K, N)


def _chips_from_whole(g, kind):
    L, K, N = g.shape
    if kind == "col":
        return g.reshape(L, K, N_CHIP, N // N_CHIP).transpose(2, 0, 1, 3)
    return g.reshape(L, N_CHIP, K // N_CHIP, N).transpose(1, 0, 2, 3)


def _pad_rows(v, rows):
    return jnp.pad(v.reshape(-1), (0, rows * PACK_C - v.size)).reshape(rows, PACK_C)


def kernel(x, c, ctx, c_ctx, w_mod, b_mod, norm1, w_in, na_rpb, q_gain, k_gain, w_pa, w_pb, w_o, norm2, w_ffn_in, w_ffn_out, final_norm, loss_target, m_c_ctx, m_w_mod, m_b_mod, m_norm1, m_w_in, m_na_rpb, m_q_gain, m_k_gain, m_w_pa, m_w_pb, m_w_o, m_norm2, m_w_ffn_in, m_w_ffn_out, m_final_norm, v_c_ctx, v_w_mod, v_b_mod, v_norm1, v_w_in, v_na_rpb, v_q_gain, v_k_gain, v_w_pa, v_w_pb, v_w_o, v_norm2, v_w_ffn_in, v_w_ffn_out, v_final_norm):
    W = dict(c_ctx=c_ctx, w_mod=w_mod, b_mod=b_mod, norm1=norm1, w_in=w_in, na_rpb=na_rpb, q_gain=q_gain, k_gain=k_gain,
             w_pa=w_pa, w_pb=w_pb, w_o=w_o, norm2=norm2, w_ffn_in=w_ffn_in, w_ffn_out=w_ffn_out, final_norm=final_norm)
    M = dict(c_ctx=m_c_ctx, w_mod=m_w_mod, b_mod=m_b_mod, norm1=m_norm1, w_in=m_w_in, na_rpb=m_na_rpb, q_gain=m_q_gain,
             k_gain=m_k_gain, w_pa=m_w_pa, w_pb=m_w_pb, w_o=m_w_o, norm2=m_norm2, w_ffn_in=m_w_ffn_in,
             w_ffn_out=m_w_ffn_out, final_norm=m_final_norm)
    V = dict(c_ctx=v_c_ctx, w_mod=v_w_mod, b_mod=v_b_mod, norm1=v_norm1, w_in=v_w_in, na_rpb=v_na_rpb, q_gain=v_q_gain,
             k_gain=v_k_gain, w_pa=v_w_pa, w_pb=v_w_pb, w_o=v_w_o, norm2=v_norm2, w_ffn_in=v_w_ffn_in,
             w_ffn_out=v_w_ffn_out, final_norm=v_final_norm)
    order = ["c_ctx", "w_mod", "b_mod", "norm1", "w_in", "na_rpb", "q_gain", "k_gain", "w_pa", "w_pb", "w_o", "norm2",
             "w_ffn_in", "w_ffn_out", "final_norm"]
    L, D = norm1.shape
    n_lat = x.shape[1]
    ax, ay, ac = lax.axis_index("x"), lax.axis_index("y"), lax.axis_index("c")
    chip = 2 * ax + ay
    dev = 2 * chip + ac

    shard_shapes = {n: W[n].shape for n, _ in BIG}
    packed = _pack_shards({n: W[n].astype(BF16) for n, _ in BIG})
    gathered = _unpack_shards(weights_all_gather(packed), shard_shapes)
    wfull = {n: _whole_from_chips(gathered[n], kind) for n, kind in BIG}

    c_all = small_all_gather(jnp.pad(c, ((0, 7), (0, 0))), "gather_c")[:, 0, :]
    craw = jnp.pad(c_all, ((0, 8), (0, 0))) + jnp.pad(c_ctx[None, :], ((8, 7), (0, 0)))
    ncol_mod = w_mod.shape[2]
    mod_loc = mod_matmul_call(craw, w_mod)
    mod_all = small_all_gather(mod_loc.reshape(L * 16, ncol_mod), "gather_mod")
    mod_all = mod_all[0::2].reshape(N_CHIP, L, 16, ncol_mod).transpose(1, 2, 0, 3).reshape(L, 16, 6 * D)
    mod_all = mod_all + b_mod[:, None, :]
    mod_mine = lax.dynamic_index_in_dim(mod_all, dev, axis=1, keepdims=False)
    modv = jnp.stack([mod_mine, mod_all[:, 8]], axis=1).reshape(L, 2, 6, D)

    xs = jnp.concatenate([x[0], ctx[0]], axis=0)
    small = dict(norm1=norm1, norm2=norm2, na_rpb=na_rpb, q_gain=q_gain, k_gain=k_gain, final_norm=final_norm)
    loss_part, dx, gw, dmod, gs = device_fwd_bwd(xs, loss_target[0], modv, wfull, small, n_lat)
    loss = lax.psum(jnp.sum(loss_part), ("x", "y", "c"))
    grad_x = dx[:n_lat][None]

    gp = jnp.concatenate(
        [_chips_from_whole(gw[n], kind).reshape(N_CHIP, 2, -1, PACK_C).transpose(1, 0, 2, 3) for n, kind in BIG], axis=2)
    r1 = grads_swap_halves(gp)
    p1 = add_halves_call(gp, r1, ac)
    r2 = grads_to_owner(p1)
    gh = sum_owner_call(p1, r2, chip)
    gbig = _unpack_shards(grads_join_halves(gh), shard_shapes)

    n_mod_rows = L * 6 * D // PACK_C
    small_flat = jnp.concatenate([gs[n].reshape(-1) for n in SMALL])
    n_small_rows = -(-small_flat.size // (8 * PACK_C)) * 8
    g1 = jnp.concatenate([dmod[:, 0].reshape(n_mod_rows, PACK_C), dmod[:, 1].reshape(n_mod_rows, PACK_C),
                          _pad_rows(small_flat, n_small_rows)], axis=0)
    g1 = small_all_gather(g1, "gather_small_grads")
    dm_lat = g1[:, :n_mod_rows].reshape(N_DEV, L, 6 * D)
    dm_ctx_parts = g1[:, n_mod_rows:2 * n_mod_rows]
    dm_ctx = sum_parts_call(dm_ctx_parts).reshape(L, 1, 6 * D)
    dmod16 = (jnp.pad(dm_lat.transpose(1, 0, 2), ((0, 0), (0, 8), (0, 0)))
              + jnp.pad(dm_ctx, ((0, 0), (8, 7), (0, 0))))
    dmod16_loc = lax.dynamic_slice_in_dim(dmod16, chip * ncol_mod, ncol_mod, axis=2)
    g_wmod, dsilu_part = mod_bwd_call(craw, dmod16_loc, w_mod)
    ds_all = small_all_gather(dsilu_part[8:16], "gather_dsilu")
    ds_parts = ds_all[0::2, 0:1, :]
    ds_parts = jnp.pad(ds_parts, ((0, 0), (0, 7), (0, 0)))

    out_g, out_d, out_m, out_v = {}, {}, {}, {}
    for n, _ in BIG:
        out_g[n] = gbig[n]
        out_d[n], out_m[n], out_v[n] = adamw_call(W[n], gbig[n], M[n], V[n])
    out_g["w_mod"] = g_wmod
    out_d["w_mod"], out_m["w_mod"], out_v["w_mod"] = adamw_call(w_mod, g_wmod, m_w_mod, v_w_mod)

    pack_small = lambda t: _pad_rows(jnp.concatenate([t[n].reshape(-1) for n in SMALL]), n_small_rows)
    res = adamw_small_call(g1[:, 2 * n_mod_rows:], pack_small(W), pack_small(M), pack_small(V))
    off = 0
    for n in SMALL:
        sz = W[n].size
        for dst, r in zip((out_g, out_d, out_m, out_v), res):
            dst[n] = r.reshape(-1)[off:off + sz].reshape(W[n].shape)
        off += sz
    bparts = jnp.concatenate([g1[:, :n_mod_rows], dm_ctx_parts], axis=0)
    as_rows = lambda t: t.reshape(n_mod_rows, PACK_C)
    res = adamw_small_call(bparts, as_rows(b_mod), as_rows(m_b_mod), as_rows(v_b_mod))
    for dst, r in zip((out_g, out_d, out_m, out_v), res):
        dst["b_mod"] = r.reshape(b_mod.shape)
    row8 = lambda t: jnp.pad(t[None, :], ((0, 7), (0, 0)))
    res = adamw_small_call(ds_parts, row8(c_ctx), row8(m_c_ctx), row8(v_c_ctx), silu_bwd=True)
    for dst, r in zip((out_g, out_d, out_m, out_v), res):
        dst["c_ctx"] = r[0]

    return (loss, grad_x, *[out_g[n] for n in order], *[out_d[n] for n in order],
            *[out_m[n] for n in order], *[out_v[n] for n in order])
```

```python
import functools
import math

import numpy as np
import jax
import jax.numpy as jnp
from jax import lax
from jax.experimental import pallas as pl
from jax.experimental.pallas import tpu as pltpu

F32 = jnp.float32
BF16 = jnp.bfloat16
MESH = pl.DeviceIdType.MESH

HEAD_DIM = 64
NA_HEADS = 8
GQA_Q_HEADS = 8
GQA_KV_HEADS = 2
GRID_W = 64
NA_WIN_H = 8
NA_WIN_W = 16
ROPE_THETA = 10000.0
EPS = 1e-6
SCALE = HEAD_DIM ** -0.5
NEG = -1e30

ADAM_LR = 0.001
ADAM_B1 = 0.9
ADAM_B2 = 0.999
ADAM_EPS = 1e-08
ADAM_WD = 0.01
ADAM_STEP = 10

TM = 256
NA_ROWS = 4
NA_KROWS = 12
LANES = 128
VMEM_BIG = 56 * 1024 * 1024


def _cp(sem, vmem=None):
    return pltpu.CompilerParams(dimension_semantics=sem, vmem_limit_bytes=vmem)


def _dot(a, b):
    return jnp.dot(a, b, preferred_element_type=F32)


def _dot_nt(a, b):
    return lax.dot_general(a, b, (((1,), (1,)), ((), ())), preferred_element_type=F32)


def _dot_tn(a, b):
    return lax.dot_general(a, b, (((0,), (0,)), ((), ())), preferred_element_type=F32)


def _colsum8(v):
    tm, d = v.shape
    return v.reshape(tm // 8, 8, d).sum(axis=0)


def _acc(ref, val, first):
    @pl.when(first)
    def _():
        ref[0] = val

    @pl.when(jnp.logical_not(first))
    def _():
        ref[0] += val


def _rms_r(x):
    return lax.rsqrt(jnp.mean(x * x, axis=-1, keepdims=True) + EPS)


def _norm_mod(x, nw, sh, sc):
    return (x * _rms_r(x) * nw) * (1.0 + sc) + sh


def _norm_mod_bwd(dh, x, nw, sc):
    r = _rms_r(x)
    xn = x * r
    dxn = dh * (nw * (1.0 + sc))
    dx = r * (dxn - xn * jnp.mean(dxn * xn, axis=-1, keepdims=True))
    return dx, dh, dh * (xn * nw), dh * ((1.0 + sc) * xn)


def _sigmoid(x):
    return 1.0 / (1.0 + jnp.exp(-x))


def _group_of(i, n_lat_tiles):
    return jnp.where(i >= n_lat_tiles, 1, 0)


def _lane_half(shape):
    return (lax.broadcasted_iota(jnp.int32, shape, len(shape) - 1) % LANES) // HEAD_DIM


def _lane_lt64(shape):
    return _lane_half(shape) == 0


def _roll64(x):
    return pltpu.roll(x, HEAD_DIM, x.ndim - 1)


def norm_mod_call(x, nw, modv, n_lat_tiles, which):
    T, D = x.shape

    def body(x_ref, nw_ref, mod_ref, h_ref):
        sh = mod_ref[0, 3 * which:3 * which + 1, :]
        sc = mod_ref[0, 3 * which + 1:3 * which + 2, :]
        h_ref[...] = _norm_mod(x_ref[...], nw_ref[...], sh, sc).astype(BF16)

    return pl.pallas_call(
        body, name="norm_mod", grid=(T // TM,),
        in_specs=[pl.BlockSpec((TM, D), lambda i: (i, 0)),
                  pl.BlockSpec((1, D), lambda i: (0, 0)),
                  pl.BlockSpec((1, 6, D), lambda i: (_group_of(i, n_lat_tiles), 0, 0))],
        out_specs=pl.BlockSpec((TM, D), lambda i: (i, 0)),
        out_shape=jax.ShapeDtypeStruct((T, D), BF16),
        compiler_params=_cp(("parallel",)),
    )(x, nw, modv)


def qkv_call(h, w_in):
    T, D = h.shape
    N = w_in.shape[1]
    n_na = 3 * NA_HEADS * HEAD_DIM
    n_g = (GQA_Q_HEADS + 2 * GQA_KV_HEADS) * HEAD_DIM
    n_gate = N - n_na - n_g

    def body(h_ref, w_ref, na_ref, g_ref, gate_ref):
        acc = _dot(h_ref[...], w_ref[...])
        na_ref[...] = acc[:, :n_na].astype(BF16)
        g_ref[...] = acc[:, n_na:n_na + n_g].astype(BF16)
        gate_ref[...] = acc[:, n_na + n_g:].astype(BF16)

    return pl.pallas_call(
        body, name="qkv", grid=(T // TM,),
        in_specs=[pl.BlockSpec((TM, D), lambda i: (i, 0)),
                  pl.BlockSpec((D, N), lambda i: (0, 0))],
        out_specs=[pl.BlockSpec((TM, n_na), lambda i: (i, 0)),
                   pl.BlockSpec((TM, n_g), lambda i: (i, 0)),
                   pl.BlockSpec((TM, n_gate), lambda i: (i, 0))],
        out_shape=[jax.ShapeDtypeStruct((T, n_na), BF16),
                   jax.ShapeDtypeStruct((T, n_g), BF16),
                   jax.ShapeDtypeStruct((T, n_gate), BF16)],
        compiler_params=_cp(("parallel",), VMEM_BIG),
    )(h, w_in)


def merge_call(ya, yb, gates, w_pa, w_pb):
    T, Ka = ya.shape
    D = w_pa.shape[1]

    def body(ya_ref, yb_ref, ga_ref, gb_ref, wa_ref, wb_ref, m_ref, pa_ref, pb_ref):
        pa = _dot(ya_ref[...], wa_ref[...])
        pb = _dot(yb_ref[...], wb_ref[...])
        m = _sigmoid(ga_ref[...].astype(F32)) * pa + _sigmoid(gb_ref[...].astype(F32)) * pb
        m_ref[...] = m.astype(BF16)
        pa_ref[...] = pa.astype(BF16)
        pb_ref[...] = pb.astype(BF16)

    row = lambda w: pl.BlockSpec((TM, w), lambda i: (i, 0))
    return pl.pallas_call(
        body, name="merge", grid=(T // TM,),
        in_specs=[row(Ka), row(Ka),
                  pl.BlockSpec((TM, D), lambda i: (i, 0)), pl.BlockSpec((TM, D), lambda i: (i, 1)),
                  pl.BlockSpec((Ka, D), lambda i: (0, 0)), pl.BlockSpec((Ka, D), lambda i: (0, 0))],
        out_specs=[row(D), row(D), row(D)],
        out_shape=[jax.ShapeDtypeStruct((T, D), BF16)] * 3,
        compiler_params=_cp(("parallel",)),
    )(ya, yb, gates, gates, w_pa, w_pb)


def wo_call(merged, w_o, x, modv, nw2, n_lat_tiles):
    T, D = x.shape

    def body(m_ref, w_ref, x_ref, mod_ref, nw_ref, x1_ref, o_ref, h2_ref):
        o = _dot(m_ref[...], w_ref[...])
        x1 = x_ref[...] + mod_ref[0, 2:3, :] * o
        x1_ref[...] = x1
        o_ref[...] = o.astype(BF16)
        h2_ref[...] = _norm_mod(x1, nw_ref[...], mod_ref[0, 3:4, :], mod_ref[0, 4:5, :]).astype(BF16)

    row = pl.BlockSpec((TM, D), lambda i: (i, 0))
    return pl.pallas_call(
        body, name="wo", grid=(T // TM,),
        in_specs=[row, pl.BlockSpec((D, D), lambda i: (0, 0)), row,
                  pl.BlockSpec((1, 6, D), lambda i: (_group_of(i, n_lat_tiles), 0, 0)),
                  pl.BlockSpec((1, D), lambda i: (0, 0))],
        out_specs=[row, row, row],
        out_shape=[jax.ShapeDtypeStruct((T, D), F32), jax.ShapeDtypeStruct((T, D), BF16),
                   jax.ShapeDtypeStruct((T, D), BF16)],
        compiler_params=_cp(("parallel",)),
    )(merged, w_o, x, modv, nw2)


def ffn_in_call(h2, w_ffn_in, n_col_tiles=2):
    T, D = h2.shape
    F = w_ffn_in.shape[1] // 2
    tn = F // n_col_tiles

    def body(h_ref, wa_ref, wu_ref, act_ref, a_ref, u_ref):
        h = h_ref[...]
        a = _dot(h, wa_ref[...])
        u = _dot(h, wu_ref[...])
        act_ref[...] = (a * _sigmoid(a) * u).astype(BF16)
        a_ref[...] = a.astype(BF16)
        u_ref[...] = u.astype(BF16)

    blk = pl.BlockSpec((TM, tn), lambda j, i: (i, j))
    return pl.pallas_call(
        body, name="ffn_in", grid=(n_col_tiles, T // TM),
        in_specs=[pl.BlockSpec((TM, D), lambda j, i: (i, 0)),
                  pl.BlockSpec((D, tn), lambda j, i: (0, j)),
                  pl.BlockSpec((D, tn), lambda j, i: (0, n_col_tiles + j))],
        out_specs=[blk, blk, blk],
        out_shape=[jax.ShapeDtypeStruct((T, F), BF16)] * 3,
        compiler_params=_cp(("parallel", "parallel"), VMEM_BIG),
    )(h2, w_ffn_in, w_ffn_in)


def ffn_out_call(act, w_ffn_out, x1, modv, n_lat_tiles, next_nw=None, next_modv=None):
    T, D = x1.shape
    F = act.shape[1]
    with_next = next_nw is not None

    def body(*refs):
        if with_next:
            a_ref, w_ref, x_ref, mod_ref, nw_ref, nmod_ref, x2_ref, f_ref, hn_ref = refs
        else:
            a_ref, w_ref, x_ref, mod_ref, x2_ref, f_ref = refs
        f = _dot(a_ref[...], w_ref[...])
        x2 = x_ref[...] + mod_ref[0, 5:6, :] * f
        x2_ref[...] = x2
        f_ref[...] = f.astype(BF16)
        if with_next:
            hn_ref[...] = _norm_mod(x2, nw_ref[...], nmod_ref[0, 0:1, :], nmod_ref[0, 1:2, :]).astype(BF16)

    row = pl.BlockSpec((TM, D), lambda i: (i, 0))
    modspec = pl.BlockSpec((1, 6, D), lambda i: (_group_of(i, n_lat_tiles), 0, 0))
    in_specs = [pl.BlockSpec((TM, F), lambda i: (i, 0)), pl.BlockSpec((F, D), lambda i: (0, 0)), row, modspec]
    args = [act, w_ffn_out, x1, modv]
    out_specs = [row, row]
    out_shape = [jax.ShapeDtypeStruct((T, D), F32), jax.ShapeDtypeStruct((T, D), BF16)]
    if with_next:
        in_specs += [pl.BlockSpec((1, D), lambda i: (0, 0)), modspec]
        args += [next_nw, next_modv]
        out_specs.append(row)
        out_shape.append(jax.ShapeDtypeStruct((T, D), BF16))
    return pl.pallas_call(
        body, name="ffn_out", grid=(T // TM,), in_specs=in_specs, out_specs=out_specs, out_shape=out_shape,
        compiler_params=_cp(("parallel",), VMEM_BIG),
    )(*args)


def loss_call(x, nw, target, n_lat_tiles):
    T, D = x.shape

    def body(x_ref, nw_ref, t_ref, dx_ref, dw_ref, ls_ref):
        i = pl.program_id(0)

        @pl.when(i < n_lat_tiles)
        def _():
            xv = x_ref[...]
            nw_v = nw_ref[...]
            r = _rms_r(xv)
            xn = xv * r
            e = xn * nw_v - t_ref[...]
            dy = e * (1.0 / D)
            dxn = dy * nw_v
            dx_ref[...] = r * (dxn - xn * jnp.mean(dxn * xn, axis=-1, keepdims=True))
            _acc(dw_ref, _colsum8(dy * xn), i == 0)
            _acc(ls_ref, _colsum8(e * e * (0.5 / D)), i == 0)

        @pl.when(i >= n_lat_tiles)
        def _():
            dx_ref[...] = jnp.zeros((TM, D), F32)

    stat = pl.BlockSpec((1, 8, D), lambda i: (0, 0, 0))
    return pl.pallas_call(
        body, name="loss", grid=(T // TM,),
        in_specs=[pl.BlockSpec((TM, D), lambda i: (i, 0)), pl.BlockSpec((1, D), lambda i: (0, 0)),
                  pl.BlockSpec((TM, D), lambda i: (jnp.minimum(i, n_lat_tiles - 1), 0))],
        out_specs=[pl.BlockSpec((TM, D), lambda i: (i, 0)), stat, stat],
        out_shape=[jax.ShapeDtypeStruct((T, D), F32), jax.ShapeDtypeStruct((1, 8, D), F32),
                   jax.ShapeDtypeStruct((1, 8, D), F32)],
        compiler_params=_cp(("arbitrary",)),
    )(x, nw, target)


def dact_call(dx2, modv, fsave, w_ffn_out, a, u, n_lat_tiles):
    T, D = dx2.shape
    F = a.shape[1]

    def body(dx_ref, mod_ref, f_ref, w_ref, a_ref, u_ref, da_ref, du_ref, df_ref, dg_ref):
        i = pl.program_id(0)
        dx = dx_ref[...]
        df = (dx * mod_ref[0, 5:6, :]).astype(BF16)
        df_ref[...] = df
        dact = _dot_nt(df, w_ref[...])
        av = a_ref[...].astype(F32)
        uv = u_ref[...].astype(F32)
        sg = _sigmoid(av)
        da_ref[...] = (dact * uv * (sg * (1.0 + av * (1.0 - sg)))).astype(BF16)
        du_ref[...] = (dact * (av * sg)).astype(BF16)
        _acc(dg_ref, _colsum8(dx * f_ref[...].astype(F32)), (i == 0) | (i == n_lat_tiles))

    row = pl.BlockSpec((TM, D), lambda i: (i, 0))
    wide = pl.BlockSpec((TM, F), lambda i: (i, 0))
    grp = lambda i: (_group_of(i, n_lat_tiles), 0, 0)
    return pl.pallas_call(
        body, name="dact", grid=(T // TM,),
        in_specs=[row, pl.BlockSpec((1, 6, D), grp), row, pl.BlockSpec((F, D), lambda i: (0, 0)), wide, wide],
        out_specs=[wide, wide, row, pl.BlockSpec((1, 8, D), grp)],
        out_shape=[jax.ShapeDtypeStruct((T, F), BF16), jax.ShapeDtypeStruct((T, F), BF16),
                   jax.ShapeDtypeStruct((T, D), BF16), jax.ShapeDtypeStruct((2, 8, D), F32)],
        compiler_params=_cp(("arbitrary",), VMEM_BIG),
    )(dx2, modv, fsave, w_ffn_out, a, u)


def _norm_bwd_tail(dh, x_ref, mod_ref, nw_ref, dres_ref, dx_ref, s_sh, s_sc, s_w, which, first):
    sc = mod_ref[0, 3 * which + 1:3 * which + 2, :]
    dxn, t_sh, t_sc, t_w = _norm_mod_bwd(dh, x_ref[...], nw_ref[...], sc)
    dx_ref[...] = dres_ref[...] + dxn
    _acc(s_sh, _colsum8(t_sh), first)
    _acc(s_sc, _colsum8(t_sc), first)
    _acc(s_w, _colsum8(t_w), first)


def dh2_call(da, du, w_ffn_in, x1, modv, nw2, dx2, n_lat_tiles):
    T, D = x1.shape
    F = da.shape[1]

    def body(da_ref, du_ref, wa_ref, wu_ref, x_ref, mod_ref, nw_ref, dres_ref, dx_ref, s_sh, s_sc, s_w):
        i = pl.program_id(0)
        dh = _dot_nt(da_ref[...], wa_ref[...]) + _dot_nt(du_ref[...], wu_ref[...])
        _norm_bwd_tail(dh, x_ref, mod_ref, nw_ref, dres_ref, dx_ref, s_sh, s_sc, s_w, 1,
                       (i == 0) | (i == n_lat_tiles))

    row = pl.BlockSpec((TM, D), lambda i: (i, 0))
    wide = pl.BlockSpec((TM, F), lambda i: (i, 0))
    grp = lambda i: (_group_of(i, n_lat_tiles), 0, 0)
    stat = pl.BlockSpec((1, 8, D), grp)
    return pl.pallas_call(
        body, name="dh2", grid=(T // TM,),
        in_specs=[wide, wide, pl.BlockSpec((D, F), lambda i: (0, 0)), pl.BlockSpec((D, F), lambda i: (0, 1)),
                  row, pl.BlockSpec((1, 6, D), grp), pl.BlockSpec((1, D), lambda i: (0, 0)), row],
        out_specs=[row, stat, stat, stat],
        out_shape=[jax.ShapeDtypeStruct((T, D), F32)] + [jax.ShapeDtypeStruct((2, 8, D), F32)] * 3,
        compiler_params=_cp(("arbitrary",), VMEM_BIG),
    )(da, du, w_ffn_in, w_ffn_in, x1, modv, nw2, dx2)


def dmerged_call(dx1, modv, osave, w_o, pa, pb, gates, n_lat_tiles):
    T, D = dx1.shape

    def body(dx_ref, mod_ref, o_ref, w_ref, pa_ref, pb_ref, ga_ref, gb_ref,
             dpa_ref, dpb_ref, dgate_ref, do_ref, dg_ref):
        i = pl.program_id(0)
        dx = dx_ref[...]
        do = (dx * mod_ref[0, 2:3, :]).astype(BF16)
        do_ref[...] = do
        dm = _dot_nt(do, w_ref[...])
        sa = _sigmoid(ga_ref[...].astype(F32))
        sb = _sigmoid(gb_ref[...].astype(F32))
        dpa_ref[...] = (dm * sa).astype(BF16)
        dpb_ref[...] = (dm * sb).astype(BF16)
        dgate_ref[:, :D] = (dm * pa_ref[...].astype(F32) * (sa * (1.0 - sa))).astype(BF16)
        dgate_ref[:, D:] = (dm * pb_ref[...].astype(F32) * (sb * (1.0 - sb))).astype(BF16)
        _acc(dg_ref, _colsum8(dx * o_ref[...].astype(F32)), (i == 0) | (i == n_lat_tiles))

    row = pl.BlockSpec((TM, D), lambda i: (i, 0))
    grp = lambda i: (_group_of(i, n_lat_tiles), 0, 0)
    return pl.pallas_call(
        body, name="dmerged", grid=(T // TM,),
        in_specs=[row, pl.BlockSpec((1, 6, D), grp), row, pl.BlockSpec((D, D), lambda i: (0, 0)), row, row,
                  pl.BlockSpec((TM, D), lambda i: (i, 0)), pl.BlockSpec((TM, D), lambda i: (i, 1))],
        out_specs=[row, row, pl.BlockSpec((TM, 2 * D), lambda i: (i, 0)), row, pl.BlockSpec((1, 8, D), grp)],
        out_shape=[jax.ShapeDtypeStruct((T, D), BF16), jax.ShapeDtypeStruct((T, D), BF16),
                   jax.ShapeDtypeStruct((T, 2 * D), BF16), jax.ShapeDtypeStruct((T, D), BF16),
                   jax.ShapeDtypeStruct((2, 8, D), F32)],
        compiler_params=_cp(("arbitrary",)),
    )(dx1, modv, osave, w_o, pa, pb, gates, gates)


def dy_call(dpa, dpb, w_pa, w_pb):
    T, D = dpa.shape
    K = w_pa.shape[0]

    def body(a_ref, b_ref, wa_ref, wb_ref, ya_ref, yb_ref):
        ya_ref[...] = _dot_nt(a_ref[...], wa_ref[...]).astype(BF16)
        yb_ref[...] = _dot_nt(b_ref[...], wb_ref[...]).astype(BF16)

    row = pl.BlockSpec((TM, D), lambda i: (i, 0))
    w = pl.BlockSpec((K, D), lambda i: (0, 0))
    out = pl.BlockSpec((TM, K), lambda i: (i, 0))
    return pl.pallas_call(
        body, name="dy", grid=(T // TM,), in_specs=[row, row, w, w], out_specs=[out, out],
        out_shape=[jax.ShapeDtypeStruct((T, K), BF16)] * 2,
        compiler_params=_cp(("parallel",)),
    )(dpa, dpb, w_pa, w_pb)


def dh_call(dna, dg, dgates, w_in, x, modv, nw1, dx1, n_lat_tiles):
    T, D = x.shape
    n1, n2, n3 = dna.shape[1], dg.shape[1], dgates.shape[1]
    N = n1 + n2 + n3

    def body(a_ref, b_ref, c_ref, w_ref, x_ref, mod_ref, nw_ref, dres_ref, dx_ref, s_sh, s_sc, s_w):
        i = pl.program_id(0)
        dh = (_dot_nt(a_ref[...], w_ref[:, :n1]) + _dot_nt(b_ref[...], w_ref[:, n1:n1 + n2])
              + _dot_nt(c_ref[...], w_ref[:, n1 + n2:]))
        _norm_bwd_tail(dh, x_ref, mod_ref, nw_ref, dres_ref, dx_ref, s_sh, s_sc, s_w, 0,
                       (i == 0) | (i == n_lat_tiles))

    row = pl.BlockSpec((TM, D), lambda i: (i, 0))
    grp = lambda i: (_group_of(i, n_lat_tiles), 0, 0)
    stat = pl.BlockSpec((1, 8, D), grp)
    return pl.pallas_call(
        body, name="dh", grid=(T // TM,),
        in_specs=[pl.BlockSpec((TM, n1), lambda i: (i, 0)), pl.BlockSpec((TM, n2), lambda i: (i, 0)),
                  pl.BlockSpec((TM, n3), lambda i: (i, 0)), pl.BlockSpec((D, N), lambda i: (0, 0)),
                  row, pl.BlockSpec((1, 6, D), grp), pl.BlockSpec((1, D), lambda i: (0, 0)), row],
        out_specs=[row, stat, stat, stat],
        out_shape=[jax.ShapeDtypeStruct((T, D), F32)] + [jax.ShapeDtypeStruct((2, 8, D), F32)] * 3,
        compiler_params=_cp(("arbitrary",), VMEM_BIG),
    )(dna, dg, dgates, w_in, x, modv, nw1, dx1)


def _pick_tile(n, cap, mult):
    best = None
    for d in range(mult, min(n, cap) + 1, mult):
        if n % d == 0:
            best = d
    assert best is not None, (n, cap, mult)
    return best


def wgrad_call(a, b, name):
    T, K = a.shape
    N = b.shape[1]
    tt = _pick_tile(T, 768, 128)
    tk = K if K <= 1024 else _pick_tile(K, 1408, 128)
    tn = N if N <= 2304 else _pick_tile(N, 2304, 128)
    nt = T // tt

    def body(a_ref, b_ref, o_ref, acc_ref):
        t = pl.program_id(2)
        p = _dot_tn(a_ref[...], b_ref[...])

        @pl.when(t == 0)
        def _():
            acc_ref[...] = p

        @pl.when(t > 0)
        def _():
            acc_ref[...] += p

        @pl.when(t == nt - 1)
        def _():
            o_ref[...] = acc_ref[...].astype(BF16)

    return pl.pallas_call(
        body, name=name, grid=(K // tk, N // tn, nt),
        in_specs=[pl.BlockSpec((tt, tk), lambda i, j, t: (t, i)), pl.BlockSpec((tt, tn), lambda i, j, t: (t, j))],
        out_specs=pl.BlockSpec((tk, tn), lambda i, j, t: (i, j)),
        out_shape=jax.ShapeDtypeStruct((K, N), BF16),
        scratch_shapes=[pltpu.VMEM((tk, tn), F32)],
        compiler_params=_cp(("parallel", "parallel", "arbitrary"), VMEM_BIG),
    )(a, b)


def _head_mean(v):
    n = v.shape[-1]
    r = lax.broadcasted_iota(jnp.int32, (n, n), 0) // HEAD_DIM
    c = lax.broadcasted_iota(jnp.int32, (n, n), 1) // HEAD_DIM
    bd = jnp.where(r == c, 1.0 / HEAD_DIM, 0.0).astype(BF16)
    hi = v.astype(BF16)
    lo = (v - hi.astype(F32)).astype(BF16)
    return _dot(hi, bd) + _dot(lo, bd)


def _swap_pairs(x):
    n = x.shape[-1]
    even = lax.broadcasted_iota(jnp.int32, x.shape, x.ndim - 1) % 2 == 0
    return jnp.where(even, pltpu.roll(x, n - 1, x.ndim - 1), pltpu.roll(x, 1, x.ndim - 1))


def _tile_lanes(t, n):
    return jnp.tile(t, (1, n // t.shape[-1]))


def gqa_prep_call(gqkv, qgain, kgain, cos_t, sin_t):
    T = gqkv.shape[0]
    nq = GQA_Q_HEADS * HEAD_DIM
    nk = GQA_KV_HEADS * HEAD_DIM
    rep = GQA_Q_HEADS // GQA_KV_HEADS

    def body(g_ref, qg_ref, kg_ref, c_ref, s_ref, qp_ref, kp_ref):
        cos_v, sin_v = c_ref[...], s_ref[...]
        xq = g_ref[:, :nq].astype(F32)
        yq = xq * lax.rsqrt(_head_mean(xq * xq) + EPS) * qg_ref[...]
        yq = (yq * _tile_lanes(cos_v, nq) + _swap_pairs(yq) * _tile_lanes(sin_v, nq)) * SCALE
        lo = _lane_lt64((TM, LANES))
        for p in range(GQA_Q_HEADS // 2):
            g = (2 * p) // rep
            chunk = yq[:, LANES * p:LANES * (p + 1)]
            rolled = _roll64(chunk)
            if g == 0:
                first, second = jnp.where(lo, chunk, 0.0), jnp.where(lo, rolled, 0.0)
            else:
                first, second = jnp.where(lo, 0.0, rolled), jnp.where(lo, 0.0, chunk)
            qp_ref[:, 2 * LANES * p:2 * LANES * p + LANES] = first.astype(BF16)
            qp_ref[:, 2 * LANES * p + LANES:2 * LANES * (p + 1)] = second.astype(BF16)
        xk = g_ref[:, nq:nq + nk].astype(F32)
        yk = xk * lax.rsqrt(_head_mean(xk * xk) + EPS) * kg_ref[...]
        kp_ref[...] = (yk * cos_v + _swap_pairs(yk) * sin_v).astype(BF16)

    return pl.pallas_call(
        body, name="gqa_prep", grid=(T // TM,),
        in_specs=[pl.BlockSpec((TM, gqkv.shape[1]), lambda i: (i, 0)),
                  pl.BlockSpec((1, nq), lambda i: (0, 0)), pl.BlockSpec((1, nk), lambda i: (0, 0)),
                  pl.BlockSpec((TM, nk), lambda i: (i, 0)), pl.BlockSpec((TM, nk), lambda i: (i, 0))],
        out_specs=[pl.BlockSpec((TM, GQA_Q_HEADS * LANES), lambda i: (i, 0)), pl.BlockSpec((TM, nk), lambda i: (i, 0))],
        out_shape=[jax.ShapeDtypeStruct((T, GQA_Q_HEADS * LANES), BF16), jax.ShapeDtypeStruct((T, nk), BF16)],
        compiler_params=_cp(("parallel",)),
    )(gqkv, qgain, kgain, cos_t, sin_t)


def gqa_prep_bwd_call(gqkv, qgain, kgain, cos_t, sin_t, dqp, dkp, dv):
    T = gqkv.shape[0]
    nq = GQA_Q_HEADS * HEAD_DIM
    nk = GQA_KV_HEADS * HEAD_DIM
    rep = GQA_Q_HEADS // GQA_KV_HEADS

    def norm_rope_bwd(x, gain, dy, cos_v, sin_v):
        dyn = dy * cos_v - _swap_pairs(dy) * sin_v
        r = lax.rsqrt(_head_mean(x * x) + EPS)
        xn = x * r
        dxn = dyn * gain
        dx = r * (dxn - xn * _head_mean(dxn * xn))
        return dx, dyn * xn

    def body(g_ref, qg_ref, kg_ref, c_ref, s_ref, dqp_ref, dkp_ref, dv_ref, dg_ref, sq_ref, sk_ref):
        i = pl.program_id(0)
        cos_v, sin_v = c_ref[...], s_ref[...]
        lo = _lane_lt64((TM, LANES))
        chunks = []
        for p in range(GQA_Q_HEADS // 2):
            g = (2 * p) // rep
            da = dqp_ref[:, 2 * LANES * p:2 * LANES * p + LANES]
            db = dqp_ref[:, 2 * LANES * p + LANES:2 * LANES * (p + 1)]
            if g == 0:
                chunks.append(jnp.where(lo, da, _roll64(db)))
            else:
                chunks.append(jnp.where(lo, _roll64(da), db))
        dyq = jnp.concatenate(chunks, axis=1) * SCALE
        dxq, gq_term = norm_rope_bwd(g_ref[:, :nq].astype(F32), qg_ref[...], dyq,
                                     _tile_lanes(cos_v, nq), _tile_lanes(sin_v, nq))
        dxk, gk_term = norm_rope_bwd(g_ref[:, nq:nq + nk].astype(F32), kg_ref[...], dkp_ref[...], cos_v, sin_v)
        dg_ref[:, :nq] = dxq.astype(BF16)
        dg_ref[:, nq:nq + nk] = dxk.astype(BF16)
        dg_ref[:, nq + nk:] = dv_ref[...].astype(BF16)
        _acc(sq_ref, _colsum8(gq_term), i == 0)
        _acc(sk_ref, _colsum8(gk_term), i == 0)

    W = gqkv.shape[1]
    return pl.pallas_call(
        body, name="gqa_prep_bwd", grid=(T // TM,),
        in_specs=[pl.BlockSpec((TM, W), lambda i: (i, 0)),
                  pl.BlockSpec((1, nq), lambda i: (0, 0)), pl.BlockSpec((1, nk), lambda i: (0, 0)),
                  pl.BlockSpec((TM, nk), lambda i: (i, 0)), pl.BlockSpec((TM, nk), lambda i: (i, 0)),
                  pl.BlockSpec((TM, GQA_Q_HEADS * LANES), lambda i: (i, 0)),
                  pl.BlockSpec((TM, nk), lambda i: (i, 0)), pl.BlockSpec((TM, nk), lambda i: (i, 0))],
        out_specs=[pl.BlockSpec((TM, W), lambda i: (i, 0)),
                   pl.BlockSpec((1, 8, nq), lambda i: (0, 0, 0)), pl.BlockSpec((1, 8, nk), lambda i: (0, 0, 0))],
        out_shape=[jax.ShapeDtypeStruct((T, W), BF16), jax.ShapeDtypeStruct((1, 8, nq), F32),
                   jax.ShapeDtypeStruct((1, 8, nk), F32)],
        compiler_params=_cp(("arbitrary",)),
    )(gqkv, qgain, kgain, cos_t, sin_t, dqp, dkp, dv)


GQ_TQ = 256
GQ_TK = 256


def _chunk_offset(j):
    return j * GQ_TK if isinstance(j, int) else pl.multiple_of(j * GQ_TK, GQ_TK)


def _sweep_keys(chunk, is_latent_query, n_lat_k, nk):
    assert n_lat_k % 2 == 0

    @pl.when(is_latent_query)
    def _():
        def two(jj, carry):
            chunk(2 * jj)
            chunk(2 * jj + 1)
            return carry

        lax.fori_loop(0, n_lat_k // 2, two, 0)

    for j in range(n_lat_k, nk):
        chunk(j)


def _gqa_place(chunk, half, g):
    gv = jnp.broadcast_to(g, chunk.shape)
    placed = jnp.where(gv == half, chunk, _roll64(chunk))
    return jnp.where(_lane_half(chunk.shape) == gv, placed, jnp.zeros_like(placed))


GQA_SMALL_SCORE = 20.0


def gqa_scores_small(q_gain, k_gain):
    bound = SCALE * HEAD_DIM * jnp.max(jnp.abs(q_gain)) * jnp.max(jnp.abs(k_gain))
    return (bound <= GQA_SMALL_SCORE).astype(jnp.int32).reshape(1)


def gqa_fwd_call(qp, kp, gqkv, n_lat, scores_small):
    T = qp.shape[0]
    rep = GQA_Q_HEADS // GQA_KV_HEADS
    nq = T // GQ_TQ
    nk = T // GQ_TK
    n_lat_q = n_lat // GQ_TQ
    n_lat_k = n_lat // GQ_TK
    v_blk = (GQA_Q_HEADS + GQA_KV_HEADS) * HEAD_DIM // LANES
    R = rep * GQ_TQ

    def body(small_ref, q_ref, k_ref, v_ref, y_ref, lse_ref, m_sc, acc_sc):
        g = pl.program_id(0)
        qb = pl.program_id(1)
        qs = jnp.concatenate([q_ref[:, LANES * i:LANES * (i + 1)] for i in range(rep)], axis=0)
        is_lat = qb < n_lat_q
        nsub = GQ_TK // LANES
        scores_small = small_ref[0] == 1

        @pl.when(scores_small)
        def _():
            m_sc[...] = jnp.zeros((R, LANES), F32)

        @pl.when(jnp.logical_not(scores_small))
        def _():
            m_sc[...] = jnp.full((R, LANES), NEG, F32)

            def chunk_max(j):
                s = _dot_nt(qs, k_ref[pl.ds(_chunk_offset(j), GQ_TK), :])
                mm = s[:, :LANES]
                for t in range(1, nsub):
                    mm = jnp.maximum(mm, s[:, LANES * t:LANES * (t + 1)])
                m_sc[...] = jnp.maximum(m_sc[...], mm)

            _sweep_keys(chunk_max, is_lat, n_lat_k, nk)
            m_sc[...] = jnp.broadcast_to(jnp.max(m_sc[...], axis=-1, keepdims=True), (R, LANES))

        acc_sc[...] = jnp.zeros((R, LANES), F32)
        own_v = _lane_half((GQ_TK, LANES)) == jnp.broadcast_to(g, (GQ_TK, LANES))

        def chunk_acc(j):
            off = _chunk_offset(j)
            vc = v_ref[pl.ds(off, GQ_TK), :]
            v1 = jnp.where(own_v, vc, jnp.ones_like(vc))
            s = _dot_nt(qs, k_ref[pl.ds(off, GQ_TK), :])
            mb = m_sc[...]
            p = jnp.concatenate([jnp.exp(s[:, LANES * t:LANES * (t + 1)] - mb).astype(BF16) for t in range(nsub)],
                                axis=1)
            acc_sc[...] += _dot(p, v1)

        _sweep_keys(chunk_acc, is_lat, n_lat_k, nk)
        acc = acc_sc[...]
        own = _lane_half((R, LANES)) == jnp.broadcast_to(g, (R, LANES))
        l = jnp.where(own, _roll64(acc), acc)
        o = acc / l
        lse = m_sc[...] + jnp.log(l)
        lo = _lane_lt64((GQ_TQ, LANES))
        g0 = jnp.broadcast_to(g, (GQ_TQ, LANES)) == 0
        for c2 in range(rep // 2):
            oa = o[(2 * c2) * GQ_TQ:(2 * c2 + 1) * GQ_TQ]
            ob = o[(2 * c2 + 1) * GQ_TQ:(2 * c2 + 2) * GQ_TQ]
            left = jnp.where(g0, oa, _roll64(oa))
            right = jnp.where(g0, _roll64(ob), ob)
            y_ref[:, LANES * c2:LANES * (c2 + 1)] = jnp.where(lo, left, right).astype(BF16)
        for i in range(rep):
            lse_ref[i] = lse[i * GQ_TQ:(i + 1) * GQ_TQ]

    return pl.pallas_call(
        body, name="gqa_fwd",
        grid_spec=pltpu.PrefetchScalarGridSpec(
            num_scalar_prefetch=1, grid=(GQA_KV_HEADS, nq),
            in_specs=[pl.BlockSpec((GQ_TQ, rep * LANES), lambda g, i, s: (i, g)),
                      pl.BlockSpec((T, LANES), lambda g, i, s: (0, 0)),
                      pl.BlockSpec((T, LANES), lambda g, i, s: (0, v_blk))],
            out_specs=[pl.BlockSpec((GQ_TQ, rep * HEAD_DIM), lambda g, i, s: (i, g)),
                       pl.BlockSpec((rep, GQ_TQ, LANES), lambda g, i, s: (g, i, 0))],
            scratch_shapes=[pltpu.VMEM((R, LANES), F32), pltpu.VMEM((R, LANES), F32)]),
        out_shape=[jax.ShapeDtypeStruct((T, GQA_Q_HEADS * HEAD_DIM), BF16),
                   jax.ShapeDtypeStruct((GQA_Q_HEADS, T, LANES), F32)],
        compiler_params=_cp(("parallel", "parallel"), VMEM_BIG),
    )(scores_small, qp, kp, gqkv)


def gqa_bwd_call(qp, kp, gqkv, yb, dyb, lse, n_lat):
    T = qp.shape[0]
    rep = GQA_Q_HEADS // GQA_KV_HEADS
    nq = T // GQ_TQ
    nk = T // GQ_TK
    n_lat_q = n_lat // GQ_TQ
    n_lat_k = n_lat // GQ_TK
    v_blk = (GQA_Q_HEADS + GQA_KV_HEADS) * HEAD_DIM // LANES
    R = rep * GQ_TQ

    def body(q_ref, k_ref, v_ref, y_ref, dy_ref, lse_ref, dq_ref, dk_ref, dv_ref, dq_sc):
        g = pl.program_id(0)
        qb = pl.program_id(1)

        @pl.when((g == 0) & (qb == 0))
        def _():
            dk_ref[...] = jnp.zeros((T, LANES), F32)
            dv_ref[...] = jnp.zeros((T, LANES), F32)

        qs = jnp.concatenate([q_ref[:, LANES * i:LANES * (i + 1)] for i in range(rep)], axis=0)
        dos, deltas, lses = [], [], []
        for i in range(rep):
            c2, half = i // 2, i % 2
            dch = _gqa_place(dy_ref[:, LANES * c2:LANES * (c2 + 1)].astype(F32), half, g)
            ych = _gqa_place(y_ref[:, LANES * c2:LANES * (c2 + 1)].astype(F32), half, g)
            dos.append(dch.astype(BF16))
            deltas.append(jnp.sum(dch * ych, axis=-1, keepdims=True))
            lses.append(lse_ref[i])
        do_s = jnp.concatenate(dos, axis=0)
        delta = jnp.broadcast_to(jnp.concatenate(deltas, axis=0), (R, LANES))
        lse_v = jnp.concatenate(lses, axis=0)
        dq_sc[...] = jnp.zeros((R, LANES), F32)
        nsub = GQ_TK // LANES

        def chunk(j):
            off = _chunk_offset(j)
            kc = k_ref[pl.ds(off, GQ_TK), :]
            vc = v_ref[pl.ds(off, GQ_TK), :]
            s = _dot_nt(qs, kc)
            dp = _dot_nt(do_s, vc)
            ps, dss = [], []
            for t in range(nsub):
                pt = jnp.exp(s[:, LANES * t:LANES * (t + 1)] - lse_v)
                ps.append(pt.astype(BF16))
                dss.append((pt * (dp[:, LANES * t:LANES * (t + 1)] - delta)).astype(BF16))
            p = jnp.concatenate(ps, axis=1)
            ds = jnp.concatenate(dss, axis=1)
            dv_ref[pl.ds(off, GQ_TK), :] += _dot_tn(p, do_s)
            dk_ref[pl.ds(off, GQ_TK), :] += _dot_tn(ds, qs)
            dq_sc[...] += _dot(ds, kc)

        _sweep_keys(chunk, qb < n_lat_q, n_lat_k, nk)
        for i in range(rep):
            dq_ref[:, LANES * i:LANES * (i + 1)] = dq_sc[i * GQ_TQ:(i + 1) * GQ_TQ, :]

    whole = pl.BlockSpec((T, LANES), lambda g, i: (0, 0))
    return pl.pallas_call(
        body, name="gqa_bwd", grid=(GQA_KV_HEADS, nq),
        in_specs=[pl.BlockSpec((GQ_TQ, rep * LANES), lambda g, i: (i, g)),
                  whole,
                  pl.BlockSpec((T, LANES), lambda g, i: (0, v_blk)),
                  pl.BlockSpec((GQ_TQ, rep * HEAD_DIM), lambda g, i: (i, g)),
                  pl.BlockSpec((GQ_TQ, rep * HEAD_DIM), lambda g, i: (i, g)),
                  pl.BlockSpec((rep, GQ_TQ, LANES), lambda g, i: (g, i, 0))],
        out_specs=[pl.BlockSpec((GQ_TQ, rep * LANES), lambda g, i: (i, g)), whole, whole],
        out_shape=[jax.ShapeDtypeStruct((T, GQA_Q_HEADS * LANES), F32),
                   jax.ShapeDtypeStruct((T, LANES), F32), jax.ShapeDtypeStruct((T, LANES), F32)],
        scratch_shapes=[pltpu.VMEM((R, LANES), F32)],
        compiler_params=_cp(("arbitrary", "arbitrary"), VMEM_BIG),
    )(qp, kp, gqkv, yb, dyb, lse)


def _na_tables(rows):
    kh = min(NA_WIN_H, rows)
    assert kh == NA_WIN_H and rows >= NA_KROWS and rows % NA_ROWS == 0
    a = np.zeros((3, NA_ROWS, NA_KROWS, 2 * NA_WIN_H - 1), np.float32)
    for v, r0 in enumerate((0, NA_ROWS, rows - NA_ROWS)):
        ks = min(max(r0 - NA_WIN_H // 2, 0), rows - NA_KROWS)
        for rq in range(NA_ROWS):
            r = r0 + rq
            rs = min(max(r - kh // 2, 0), rows - kh)
            for rk in range(NA_KROWS):
                kr = ks + rk
                if rs <= kr < rs + kh:
                    a[v, rq, rk, kr - r + NA_WIN_H - 1] = 1.0
    c = np.zeros((GRID_W, GRID_W, 2 * NA_WIN_W - 1), np.float32)
    for wq in range(GRID_W):
        cs = min(max(wq - NA_WIN_W // 2, 0), GRID_W - NA_WIN_W)
        for wk in range(cs, cs + NA_WIN_W):
            c[wq, wk, min(max(wk - wq, -(NA_WIN_W - 1)), NA_WIN_W - 1) + NA_WIN_W - 1] = 1.0
    return a, c


def na_bias_table(rpb, rows):
    a, c = _na_tables(rows)
    hp = lax.Precision.HIGHEST
    t1 = jnp.einsum("hde,wue->hdwu", rpb, jnp.asarray(c), precision=hp)
    b = jnp.einsum("hdwu,vqkd->hvqwku", t1, jnp.asarray(a), precision=hp)
    ok = np.einsum("vqk,wu->vqwku", a.sum(-1), c.sum(-1)) > 0
    b = jnp.where(jnp.asarray(ok)[None], b, NEG)
    return b.reshape(rpb.shape[0], 3, NA_ROWS * GRID_W, NA_KROWS * GRID_W)


def na_bias_grad(dbias, rows):
    a, c = _na_tables(rows)
    hp = lax.Precision.HIGHEST
    d = dbias.reshape(dbias.shape[0], 3, NA_ROWS, GRID_W, NA_KROWS, GRID_W)
    t1 = jnp.einsum("hvqwku,vqkd->hdwu", d, jnp.asarray(a), precision=hp)
    return jnp.einsum("hdwu,wue->hde", t1, jnp.asarray(c), precision=hp)


def _na_variant(blk, n_lat_blk):
    return jnp.where(blk == 0, 0, jnp.where(blk >= n_lat_blk - 1, 2, 1))


def _na_kstart(blk, rows):
    r0 = blk * NA_ROWS
    ks = jnp.clip(r0 - NA_WIN_H // 2, 0, rows - NA_KROWS)
    return pl.multiple_of(ks * GRID_W, GRID_W)


def na_fwd_call(naqkv, bias, n_lat):
    T = naqkv.shape[0]
    nb = T // TM
    n_lat_blk = n_lat // TM
    rows = n_lat // GRID_W
    n_ctx = T - n_lat
    KW = NA_KROWS * GRID_W
    npair = NA_HEADS // 2

    def body(q_ref, k_ref, v_ref, b_ref, y_ref, lse_ref):
        blk = pl.program_id(1)
        half = _lane_half((TM, LANES))
        lo = half == 0
        q = q_ref[...].astype(F32) * SCALE
        kctx = k_ref[pl.ds(n_lat, n_ctx), :]
        vctx = v_ref[pl.ds(n_lat, n_ctx), :]

        @pl.when(blk < n_lat_blk)
        def _():
            off = _na_kstart(blk, rows)
            kwin = k_ref[pl.ds(off, KW), :]
            vwin = v_ref[pl.ds(off, KW), :]
            outs = []
            for hh in range(2):
                qm = jnp.where(half == hh, q, 0.0).astype(BF16)
                sw = _dot_nt(qm, kwin) + b_ref[hh, 0]
                sc = _dot_nt(qm, kctx)
                m = jnp.maximum(jnp.max(sw, axis=-1, keepdims=True), jnp.max(sc, axis=-1, keepdims=True))
                pw = jnp.exp(sw - m)
                pc = jnp.exp(sc - m)
                l = jnp.sum(pw, axis=-1, keepdims=True) + jnp.sum(pc, axis=-1, keepdims=True)
                outs.append((_dot(pw.astype(BF16), vwin) + _dot(pc.astype(BF16), vctx)) / l)
                lse_ref[hh] = jnp.broadcast_to(m + jnp.log(l), (TM, LANES))
            y_ref[...] = jnp.where(lo, outs[0], outs[1]).astype(BF16)

        @pl.when(blk >= n_lat_blk)
        def _():
            outs = []
            for hh in range(2):
                qm = jnp.where(half == hh, q, 0.0).astype(BF16)
                sc = _dot_nt(qm, kctx)
                m = jnp.max(sc, axis=-1, keepdims=True)
                pc = jnp.exp(sc - m)
                l = jnp.sum(pc, axis=-1, keepdims=True)
                outs.append(_dot(pc.astype(BF16), vctx) / l)
                lse_ref[hh] = jnp.broadcast_to(m + jnp.log(l), (TM, LANES))
            y_ref[...] = jnp.where(lo, outs[0], outs[1]).astype(BF16)

    return pl.pallas_call(
        body, name="na_fwd", grid=(npair, nb),
        in_specs=[pl.BlockSpec((TM, LANES), lambda p, b: (b, p)),
                  pl.BlockSpec((T, LANES), lambda p, b: (0, npair + p)),
                  pl.BlockSpec((T, LANES), lambda p, b: (0, 2 * npair + p)),
                  pl.BlockSpec((2, 1, TM, KW), lambda p, b: (p, _na_variant(b, n_lat_blk), 0, 0))],
        out_specs=[pl.BlockSpec((TM, LANES), lambda p, b: (b, p)),
                   pl.BlockSpec((2, TM, LANES), lambda p, b: (p, b, 0))],
        out_shape=[jax.ShapeDtypeStruct((T, NA_HEADS * HEAD_DIM), BF16),
                   jax.ShapeDtypeStruct((NA_HEADS, T, LANES), F32)],
        compiler_params=_cp(("parallel", "parallel"), VMEM_BIG),
    )(naqkv, naqkv, naqkv, bias)


def na_bwd_call(naqkv, bias, ya, dya, lse, n_lat):
    T = naqkv.shape[0]
    nb = T // TM
    n_lat_blk = n_lat // TM
    rows = n_lat // GRID_W
    n_ctx = T - n_lat
    KW = NA_KROWS * GRID_W
    npair = NA_HEADS // 2

    def body(q_ref, k_ref, v_ref, b_ref, y_ref, dy_ref, lse_ref, dq_ref, dk_ref, dv_ref, db_ref, dk_sc, dv_sc):
        blk = pl.program_id(1)
        half = _lane_half((TM, LANES))
        lo = half == 0
        q = q_ref[...].astype(F32) * SCALE
        kctx = k_ref[pl.ds(n_lat, n_ctx), :]
        vctx = v_ref[pl.ds(n_lat, n_ctx), :]
        dyv = dy_ref[...].astype(F32)
        yv = y_ref[...].astype(F32)

        @pl.when(blk == 0)
        def _():
            dk_sc[...] = jnp.zeros((T, LANES), F32)
            dv_sc[...] = jnp.zeros((T, LANES), F32)

        @pl.when(blk < n_lat_blk)
        def _():
            off = _na_kstart(blk, rows)
            kwin = k_ref[pl.ds(off, KW), :]
            vwin = v_ref[pl.ds(off, KW), :]
            first = (blk == 0) | (blk == 1) | (blk == n_lat_blk - 1)
            dqs = []
            for hh in range(2):
                sel = half == hh
                qm = jnp.where(sel, q, 0.0).astype(BF16)
                dom = jnp.where(sel, dyv, 0.0)
                delta = jnp.sum(dom * yv, axis=-1, keepdims=True)
                dom = dom.astype(BF16)
                lse_v = lse_ref[hh][:, 0:1]
                pw = jnp.exp(_dot_nt(qm, kwin) + b_ref[hh, 0] - lse_v)
                pc = jnp.exp(_dot_nt(qm, kctx) - lse_v)
                dsw = pw * (_dot_nt(dom, vwin) - delta)
                dsc = pc * (_dot_nt(dom, vctx) - delta)

                @pl.when(first)
                def _():
                    db_ref[hh, 0] = dsw

                @pl.when(jnp.logical_not(first))
                def _():
                    db_ref[hh, 0] += dsw

                dsw = dsw.astype(BF16)
                dsc = dsc.astype(BF16)
                dv_sc[pl.ds(off, KW), :] += _dot_tn(pw.astype(BF16), dom)
                dv_sc[pl.ds(n_lat, n_ctx), :] += _dot_tn(pc.astype(BF16), dom)
                dk_sc[pl.ds(off, KW), :] += _dot_tn(dsw, qm)
                dk_sc[pl.ds(n_lat, n_ctx), :] += _dot_tn(dsc, qm)
                dqs.append((_dot(dsw, kwin) + _dot(dsc, kctx)) * SCALE)
            dq_ref[...] = jnp.where(lo, dqs[0], dqs[1]).astype(BF16)

        @pl.when(blk >= n_lat_blk)
        def _():
            dqs = []
            for hh in range(2):
                sel = half == hh
                qm = jnp.where(sel, q, 0.0).astype(BF16)
                dom = jnp.where(sel, dyv, 0.0)
                delta = jnp.sum(dom * yv, axis=-1, keepdims=True)
                dom = dom.astype(BF16)
                pc = jnp.exp(_dot_nt(qm, kctx) - lse_ref[hh][:, 0:1])
                dsc = (pc * (_dot_nt(dom, vctx) - delta)).astype(BF16)
                dv_sc[pl.ds(n_lat, n_ctx), :] += _dot_tn(pc.astype(BF16), dom)
                dk_sc[pl.ds(n_lat, n_ctx), :] += _dot_tn(dsc, qm)
                dqs.append(_dot(dsc, kctx) * SCALE)
            dq_ref[...] = jnp.where(lo, dqs[0], dqs[1]).astype(BF16)

        @pl.when(blk == nb - 1)
        def _():
            dk_ref[...] = dk_sc[...].astype(BF16)
            dv_ref[...] = dv_sc[...].astype(BF16)

    blkspec = pl.BlockSpec((TM, LANES), lambda p, b: (b, p))
    col = pl.BlockSpec((T, LANES), lambda p, b: (0, p))
    bspec = pl.BlockSpec((2, 1, TM, KW), lambda p, b: (p, _na_variant(b, n_lat_blk), 0, 0))
    dq, dk, dv, db = pl.pallas_call(
        body, name="na_bwd", grid=(npair, nb),
        in_specs=[blkspec,
                  pl.BlockSpec((T, LANES), lambda p, b: (0, npair + p)),
                  pl.BlockSpec((T, LANES), lambda p, b: (0, 2 * npair + p)),
                  bspec, blkspec, blkspec,
                  pl.BlockSpec((2, TM, LANES), lambda p, b: (p, b, 0))],
        out_specs=[blkspec, col, col, bspec],
        out_shape=[jax.ShapeDtypeStruct((T, NA_HEADS * HEAD_DIM), BF16)] * 3
        + [jax.ShapeDtypeStruct((NA_HEADS, 3, TM, KW), F32)],
        scratch_shapes=[pltpu.VMEM((T, LANES), F32), pltpu.VMEM((T, LANES), F32)],
        compiler_params=_cp(("arbitrary", "arbitrary"), VMEM_BIG),
    )(naqkv, naqkv, naqkv, bias, ya, dya, lse)
    return jnp.concatenate([dq, dk, dv], axis=1), db


def rope_tables(n_lat, n_ctx):
    t = jnp.arange(n_lat)
    row = (t // GRID_W).astype(F32)
    col = (t % GRID_W).astype(F32)
    half = HEAD_DIM // 2
    inv = ROPE_THETA ** (-jnp.arange(0, half, 2, dtype=F32) / half)
    ang = jnp.concatenate([row[:, None] * inv, col[:, None] * inv], axis=-1)
    cos = jnp.repeat(jnp.cos(ang), 2, axis=-1)
    sin = jnp.repeat(jnp.sin(ang), 2, axis=-1) * jnp.tile(jnp.array([-1.0, 1.0], F32), half)
    cos = jnp.concatenate([cos, jnp.ones((n_ctx, HEAD_DIM), F32)], axis=0)
    sin = jnp.concatenate([sin, jnp.zeros((n_ctx, HEAD_DIM), F32)], axis=0)
    return jnp.tile(cos, (1, 2)), jnp.tile(sin, (1, 2))


def _row(v):
    return v.reshape(1, -1)


def device_fwd_bwd(xs, target, modv, w, small, n_lat):
    T, D = xs.shape
    n_ctx = T - n_lat
    L = modv.shape[0]
    nlt = n_lat // TM
    rows = n_lat // GRID_W
    cos_t, sin_t = rope_tables(n_lat, n_ctx)
    qg = [jnp.tile(_row(small["q_gain"][l]), (1, GQA_Q_HEADS)) for l in range(L)]
    kg = [jnp.tile(_row(small["k_gain"][l]), (1, GQA_KV_HEADS)) for l in range(L)]
    bias = [na_bias_table(small["na_rpb"][l], rows) for l in range(L)]

    saved = []
    x = xs
    h = norm_mod_call(x, _row(small["norm1"][0]), modv[0], nlt, 0)
    for l in range(L):
        naqkv, gqkv, gates = qkv_call(h, w["w_in"][l])
        ya, lse_a = na_fwd_call(naqkv, bias[l], n_lat)
        qp, kp = gqa_prep_call(gqkv, qg[l], kg[l], cos_t, sin_t)
        yb, lse_b = gqa_fwd_call(qp, kp, gqkv, n_lat, gqa_scores_small(small["q_gain"][l], small["k_gain"][l]))
        merged, pa, pb = merge_call(ya, yb, gates, w["w_pa"][l], w["w_pb"][l])
        x1, osave, h2 = wo_call(merged, w["w_o"][l], x, modv[l], _row(small["norm2"][l]), nlt)
        act, a, u = ffn_in_call(h2, w["w_ffn_in"][l])
        if l + 1 < L:
            x2, fsave, hn = ffn_out_call(act, w["w_ffn_out"][l], x1, modv[l], nlt,
                                         _row(small["norm1"][l + 1]), modv[l + 1])
        else:
            x2, fsave = ffn_out_call(act, w["w_ffn_out"][l], x1, modv[l], nlt)
            hn = None
        saved.append(dict(x=x, h=h, naqkv=naqkv, gqkv=gqkv, gates=gates, ya=ya, lse_a=lse_a, qp=qp, kp=kp,
                          yb=yb, lse_b=lse_b, merged=merged, pa=pa, pb=pb, x1=x1, osave=osave, h2=h2,
                          act=act, a=a, u=u, fsave=fsave))
        x, h = x2, hn

    dx, d_final, loss_part = loss_call(x, _row(small["final_norm"]), target, nlt)

    gw = {k: [None] * L for k in ("w_in", "w_pa", "w_pb", "w_o", "w_ffn_in", "w_ffn_out")}
    gs = {k: [None] * L for k in ("norm1", "norm2", "na_rpb", "q_gain", "k_gain")}
    dmod = [None] * L
    fold = lambda s: s.sum(axis=1)
    for l in reversed(range(L)):
        s = saved[l]
        da, du, df, s_g2 = dact_call(dx, modv[l], s["fsave"], w["w_ffn_out"][l], s["a"], s["u"], nlt)
        gw["w_ffn_out"][l] = wgrad_call(s["act"], df, "wgrad_ffn_out")
        dx1, s_sh2, s_sc2, s_n2 = dh2_call(da, du, w["w_ffn_in"][l], s["x1"], modv[l], _row(small["norm2"][l]), dx, nlt)
        gw["w_ffn_in"][l] = jnp.concatenate(
            [wgrad_call(s["h2"], da, "wgrad_ffn_in_a"), wgrad_call(s["h2"], du, "wgrad_ffn_in_u")], axis=1)
        dpa, dpb, dgates, do, s_g1 = dmerged_call(dx1, modv[l], s["osave"], w["w_o"][l], s["pa"], s["pb"], s["gates"], nlt)
        gw["w_o"][l] = wgrad_call(s["merged"], do, "wgrad_o")
        dya, dyb = dy_call(dpa, dpb, w["w_pa"][l], w["w_pb"][l])
        gw["w_pa"][l] = wgrad_call(s["ya"], dpa, "wgrad_pa")
        gw["w_pb"][l] = wgrad_call(s["yb"], dpb, "wgrad_pb")
        dna, dbias = na_bwd_call(s["naqkv"], bias[l], s["ya"], dya, s["lse_a"], n_lat)
        gs["na_rpb"][l] = na_bias_grad(dbias, rows)
        dqp, dkp, dv = gqa_bwd_call(s["qp"], s["kp"], s["gqkv"], s["yb"], dyb, s["lse_b"], n_lat)
        dg, s_qg, s_kg = gqa_prep_bwd_call(s["gqkv"], qg[l], kg[l], cos_t, sin_t, dqp, dkp, dv)
        gs["q_gain"][l] = s_qg.reshape(8 * GQA_Q_HEADS, HEAD_DIM).sum(axis=0)
        gs["k_gain"][l] = s_kg.reshape(8 * GQA_KV_HEADS, HEAD_DIM).sum(axis=0)
        dx, s_sh1, s_sc1, s_n1 = dh_call(dna, dg, dgates, w["w_in"][l], s["x"], modv[l], _row(small["norm1"][l]), dx1, nlt)
        gw["w_in"][l] = jnp.concatenate(
            [wgrad_call(s["h"], dna, "wgrad_in_na"), wgrad_call(s["h"], dg, "wgrad_in_g"),
             wgrad_call(s["h"], dgates, "wgrad_in_gate")], axis=1)
        gs["norm1"][l] = fold(s_n1).sum(axis=0)
        gs["norm2"][l] = fold(s_n2).sum(axis=0)
        dmod[l] = jnp.stack([fold(s_sh1), fold(s_sc1), fold(s_g1), fold(s_sh2), fold(s_sc2), fold(s_g2)], axis=1)

    gw = {k: jnp.stack(v) for k, v in gw.items()}
    gs = {k: jnp.stack(v) for k, v in gs.items()}
    gs["final_norm"] = d_final.sum(axis=(0, 1))
    return loss_part, dx, gw, jnp.stack(dmod), gs


N_DEV = 8
N_CHIP = 4
ANY = pl.BlockSpec(memory_space=pl.ANY)


def _place():
    x, y, c = lax.axis_index("x"), lax.axis_index("y"), lax.axis_index("c")
    chips = [(1 - x, y), (x, 1 - y), (1 - x, 1 - y)]
    return x, y, c, chips


def small_all_gather(v, name):
    m_per, n = v.shape
    assert m_per % 8 == 0

    def body(x_ref, out_ref, send_sems, recv_sems, local_sem):
        x, y, c, chips = _place()
        me, sibling = (x, y, c), (x, y, 1 - c)

        def rows(px, py, pc):
            return out_ref.at[pl.ds(pl.multiple_of((4 * px + 2 * py + pc) * m_per, 8), m_per), :]

        def copy(k, block, to, src=None):
            return pltpu.make_async_remote_copy(
                src_ref=rows(*block) if src is None else src, dst_ref=rows(*block),
                send_sem=send_sems.at[k], recv_sem=recv_sems.at[k], device_id=to, device_id_type=MESH)

        mine = pltpu.make_async_copy(x_ref, rows(*me), local_sem)
        mine.start()
        first = [copy(0, me, sibling, src=x_ref)]
        first += [copy(1 + j, me, (*chip, c), src=x_ref) for j, chip in enumerate(chips)]
        for cp in first:
            cp.start()
        passed = [copy(4 + j, (*chip, c), sibling) for j, chip in enumerate(chips)]
        for j, chip in enumerate(chips):
            copy(1 + j, (*chip, c), me).wait_recv()
            passed[j].start()
        copy(0, sibling, me).wait_recv()
        for j, chip in enumerate(chips):
            copy(4 + j, (*chip, 1 - c), me).wait_recv()
        for cp in first + passed:
            cp.wait_send()
        mine.wait()

    out = pl.pallas_call(
        body, name=name,
        out_shape=jax.ShapeDtypeStruct((N_DEV * m_per, n), v.dtype),
        in_specs=[pl.BlockSpec(memory_space=pltpu.VMEM)],
        out_specs=pl.BlockSpec(memory_space=pltpu.VMEM),
        scratch_shapes=[pltpu.SemaphoreType.DMA((7,)), pltpu.SemaphoreType.DMA((7,)), pltpu.SemaphoreType.DMA],
    )(v)
    return out.reshape(N_DEV, m_per, n)


def weights_all_gather(packed):
    _, rh, ncol = packed.shape

    def body(in_ref, out_ref, send_sems, recv_sems):
        x, y, c, chips = _place()
        sibling = (x, y, 1 - c)

        def blk(px, py, half):
            return out_ref.at[2 * px + py, half]

        def copy(k, block, to, src=None):
            return pltpu.make_async_remote_copy(
                src_ref=blk(*block) if src is None else src, dst_ref=blk(*block),
                send_sem=send_sems.at[k], recv_sem=recv_sems.at[k], device_id=to, device_id_type=MESH)

        first = [copy(j, (x, y, c), (*chip, c), src=in_ref.at[c]) for j, chip in enumerate(chips)]
        for cp in first:
            cp.start()
        passed = [copy(3 + j, (*chip, c), sibling) for j, chip in enumerate(chips)]
        for j, chip in enumerate(chips):
            copy(j, (*chip, c), (x, y, c)).wait_recv()
            passed[j].start()
        for j, chip in enumerate(chips):
            copy(3 + j, (*chip, 1 - c), (x, y, c)).wait_recv()
        for cp in first + passed:
            cp.wait_send()

    out = pl.pallas_call(
        body, name="weights_all_gather",
        out_shape=jax.ShapeDtypeStruct((N_CHIP, 2, rh, ncol), packed.dtype),
        in_specs=[ANY], out_specs=ANY,
        scratch_shapes=[pltpu.SemaphoreType.DMA((6,)), pltpu.SemaphoreType.DMA((6,))],
    )(packed)
    chip = 2 * lax.axis_index("x") + lax.axis_index("y")
    return lax.dynamic_update_slice(out, packed[None], (chip, 0, 0, 0))


def grads_swap_halves(gp):
    _, _, rh, ncol = gp.shape

    def body(gp_ref, r1_ref, send_sem, recv_sem):
        x, y, c, _ = _place()
        cp = pltpu.make_async_remote_copy(src_ref=gp_ref.at[1 - c], dst_ref=r1_ref, send_sem=send_sem,
                                          recv_sem=recv_sem, device_id=(x, y, 1 - c), device_id_type=MESH)
        cp.start()
        cp.wait()

    return pl.pallas_call(
        body, name="grads_swap_halves", out_shape=jax.ShapeDtypeStruct((N_CHIP, rh, ncol), gp.dtype),
        in_specs=[ANY], out_specs=ANY,
        scratch_shapes=[pltpu.SemaphoreType.DMA, pltpu.SemaphoreType.DMA],
    )(gp)


def grads_to_owner(p1):
    _, rh, ncol = p1.shape

    def body(p1_ref, r2_ref, send_sems, recv_sems):
        x, y, c, chips = _place()
        cps = [pltpu.make_async_remote_copy(src_ref=p1_ref.at[2 * px + py], dst_ref=r2_ref.at[j],
                                            send_sem=send_sems.at[j], recv_sem=recv_sems.at[j],
                                            device_id=(px, py, c), device_id_type=MESH)
               for j, (px, py) in enumerate(chips)]
        for cp in cps:
            cp.start()
        for cp in cps:
            cp.wait()

    return pl.pallas_call(
        body, name="grads_to_owner", out_shape=jax.ShapeDtypeStruct((3, rh, ncol), p1.dtype),
        in_specs=[ANY], out_specs=ANY,
        scratch_shapes=[pltpu.SemaphoreType.DMA((3,)), pltpu.SemaphoreType.DMA((3,))],
    )(p1)


def grads_join_halves(gh):
    rh, ncol = gh.shape

    def body(gh_ref, out_ref, send_sem, recv_sem):
        x, y, c, _ = _place()
        cp = pltpu.make_async_remote_copy(src_ref=gh_ref, dst_ref=out_ref, send_sem=send_sem,
                                          recv_sem=recv_sem, device_id=(x, y, 1 - c), device_id_type=MESH)
        cp.start()
        cp.wait()

    other = pl.pallas_call(
        body, name="grads_join_halves", out_shape=jax.ShapeDtypeStruct((rh, ncol), gh.dtype),
        in_specs=[ANY], out_specs=ANY,
        scratch_shapes=[pltpu.SemaphoreType.DMA, pltpu.SemaphoreType.DMA],
    )(gh)
    c0 = lax.axis_index("c") == 0
    return jnp.stack([jnp.where(c0, gh, other), jnp.where(c0, other, gh)])


PACK_C = 1024


def add_halves_call(gp, r1, c):
    _, n, rh, ncol = gp.shape
    tr = _pick_tile(rh, 512, 16)

    def body(c_ref, a_ref, b_ref, o_ref):
        o_ref[0] = (a_ref[0, 0].astype(F32) + b_ref[0].astype(F32)).astype(BF16)

    return pl.pallas_call(
        body, name="add_halves",
        grid_spec=pltpu.PrefetchScalarGridSpec(
            num_scalar_prefetch=1, grid=(n, rh // tr),
            in_specs=[pl.BlockSpec((1, 1, tr, ncol), lambda k, i, c_ref: (c_ref[0], k, i, 0)),
                      pl.BlockSpec((1, tr, ncol), lambda k, i, c_ref: (k, i, 0))],
            out_specs=pl.BlockSpec((1, tr, ncol), lambda k, i, c_ref: (k, i, 0))),
        out_shape=jax.ShapeDtypeStruct((n, rh, ncol), BF16),
        compiler_params=_cp(("parallel", "parallel")),
    )(jnp.reshape(c, (1,)).astype(jnp.int32), gp, r1)


def sum_owner_call(p1, r2, k):
    _, rh, ncol = p1.shape
    tr = _pick_tile(rh, 512, 16)

    def body(k_ref, a_ref, b0_ref, b1_ref, b2_ref, o_ref):
        o_ref[...] = (((a_ref[0].astype(F32) + b0_ref[0].astype(F32)) + b1_ref[0].astype(F32))
                      + b2_ref[0].astype(F32))

    other = lambda j: pl.BlockSpec((1, tr, ncol), lambda i, k_ref: (j, i, 0))
    return pl.pallas_call(
        body, name="sum_owner",
        grid_spec=pltpu.PrefetchScalarGridSpec(
            num_scalar_prefetch=1, grid=(rh // tr,),
            in_specs=[pl.BlockSpec((1, tr, ncol), lambda i, k_ref: (k_ref[0], i, 0)), other(0), other(1), other(2)],
            out_specs=pl.BlockSpec((tr, ncol), lambda i, k_ref: (i, 0))),
        out_shape=jax.ShapeDtypeStruct((rh, ncol), F32),
        compiler_params=_cp(("parallel",)),
    )(jnp.reshape(k, (1,)).astype(jnp.int32), p1, r2, r2, r2)


def _silu(x):
    return x * _sigmoid(x)


def mod_matmul_call(craw, w_mod):
    L, D, N = w_mod.shape

    def body(c_ref, w_ref, o_ref):
        o_ref[0] = _dot(_silu(c_ref[...]).astype(BF16), w_ref[0].astype(BF16))

    return pl.pallas_call(
        body, name="mod_matmul", grid=(L,),
        in_specs=[pl.BlockSpec((16, D), lambda l: (0, 0)), pl.BlockSpec((1, D, N), lambda l: (l, 0, 0))],
        out_specs=pl.BlockSpec((1, 16, N), lambda l: (l, 0, 0)),
        out_shape=jax.ShapeDtypeStruct((L, 16, N), F32),
        compiler_params=_cp(("parallel",), VMEM_BIG),
    )(craw, w_mod)


def mod_bwd_call(craw, dmod16, w_mod):
    L, D, N = w_mod.shape

    def body(c_ref, d_ref, w_ref, gw_ref, ds_ref):
        l = pl.program_id(0)
        d = d_ref[0].astype(BF16)
        gw_ref[0] = _dot_tn(_silu(c_ref[...]).astype(BF16), d)
        part = _dot_nt(d, w_ref[0].astype(BF16))

        @pl.when(l == 0)
        def _():
            ds_ref[...] = part

        @pl.when(l > 0)
        def _():
            ds_ref[...] += part

    return pl.pallas_call(
        body, name="mod_bwd", grid=(L,),
        in_specs=[pl.BlockSpec((16, D), lambda l: (0, 0)), pl.BlockSpec((1, 16, N), lambda l: (l, 0, 0)),
                  pl.BlockSpec((1, D, N), lambda l: (l, 0, 0))],
        out_specs=[pl.BlockSpec((1, D, N), lambda l: (l, 0, 0)), pl.BlockSpec((16, D), lambda l: (0, 0))],
        out_shape=[jax.ShapeDtypeStruct((L, D, N), F32), jax.ShapeDtypeStruct((16, D), F32)],
        compiler_params=_cp(("arbitrary",), VMEM_BIG),
    )(craw, dmod16, w_mod)


def _adamw(w, g, m, v):
    m2 = ADAM_B1 * m + (1.0 - ADAM_B1) * g
    v2 = ADAM_B2 * v + (1.0 - ADAM_B2) * (g * g)
    m_hat = m2 / (1.0 - ADAM_B1 ** ADAM_STEP)
    v_hat = v2 / (1.0 - ADAM_B2 ** ADAM_STEP)
    delta = -ADAM_LR * (m_hat / (jnp.sqrt(v_hat) + ADAM_EPS) + ADAM_WD * w)
    return delta, m2, v2


def adamw_call(w, g, m, v):
    shape = w.shape
    ncol = shape[-1]
    r = math.prod(shape[:-1])
    tr = _pick_tile(r, 512, 8)
    as2d = lambda t: t.reshape(r, ncol)

    def body(w_ref, g_ref, m_ref, v_ref, d_ref, m2_ref, v2_ref):
        d_ref[...], m2_ref[...], v2_ref[...] = _adamw(w_ref[...], g_ref[...], m_ref[...], v_ref[...])

    spec = pl.BlockSpec((tr, ncol), lambda i: (i, 0))
    outs = pl.pallas_call(
        body, name="adamw", grid=(r // tr,), in_specs=[spec] * 4, out_specs=[spec] * 3,
        out_shape=[jax.ShapeDtypeStruct((r, ncol), F32)] * 3,
        compiler_params=_cp(("parallel",)),
    )(as2d(w), as2d(g), as2d(m), as2d(v))
    return [o.reshape(shape) for o in outs]


def adamw_small_call(parts, w, m, v, silu_bwd=False):
    P, R, C = parts.shape

    def body(p_ref, w_ref, m_ref, v_ref, g_ref, d_ref, m2_ref, v2_ref):
        g = p_ref[0]
        for i in range(1, P):
            g = g + p_ref[i]
        wv = w_ref[...]
        if silu_bwd:
            sg = _sigmoid(wv)
            g = g * (sg * (1.0 + wv * (1.0 - sg)))
        g_ref[...] = g
        d_ref[...], m2_ref[...], v2_ref[...] = _adamw(wv, g, m_ref[...], v_ref[...])

    return pl.pallas_call(
        body, name="adamw_small", out_shape=[jax.ShapeDtypeStruct((R, C), F32)] * 4,
    )(parts, w, m, v)


def sum_parts_call(parts):
    P, R, C = parts.shape

    def body(p_ref, o_ref):
        g = p_ref[0]
        for i in range(1, P):
            g = g + p_ref[i]
        o_ref[...] = g

    return pl.pallas_call(body, name="sum_parts", out_shape=jax.ShapeDtypeStruct((R, C), F32))(parts)


BIG = (("w_in", "col"), ("w_pa", "col"), ("w_pb", "col"), ("w_o", "row"), ("w_ffn_in", "col"), ("w_ffn_out", "row"))
SMALL = ("norm1", "norm2", "final_norm", "na_rpb", "q_gain", "k_gain")


def _pack_shards(shards):
    parts = [shards[n].reshape(2, -1, PACK_C) for n, _ in BIG]
    return jnp.concatenate(parts, axis=1)


def _unpack_shards(packed, shapes):
    out, r0 = {}, 0
    lead = packed.shape[:-3]
    for n, _ in BIG:
        L, K, N = shapes[n]
        rw = (L // 2) * K * N // PACK_C
        out[n] = packed[..., r0:r0 + rw, :].reshape(lead + (L, K, N))
        r0 += rw
    return out


def _whole_from_chips(g, kind):
    _, L, K, N = g.shape
    if kind == "col":
        return g.transpose(1, 2, 0, 3).reshape(L, K, N_CHIP * N)
    return g.transpose(1, 0, 2, 3).reshape(L, N_CHIP * K, N)


def _chips_from_whole(g, kind):
    L, K, N = g.shape
    if kind == "col":
        return g.reshape(L, K, N_CHIP, N // N_CHIP).transpose(2, 0, 1, 3)
    return g.reshape(L, N_CHIP, K // N_CHIP, N).transpose(1, 0, 2, 3)


def _pad_rows(v, rows):
    return jnp.pad(v.reshape(-1), (0, rows * PACK_C - v.size)).reshape(rows, PACK_C)


def kernel(x, c, ctx, c_ctx, w_mod, b_mod, norm1, w_in, na_rpb, q_gain, k_gain, w_pa, w_pb, w_o, norm2, w_ffn_in, w_ffn_out, final_norm, loss_target, m_c_ctx, m_w_mod, m_b_mod, m_norm1, m_w_in, m_na_rpb, m_q_gain, m_k_gain, m_w_pa, m_w_pb, m_w_o, m_norm2, m_w_ffn_in, m_w_ffn_out, m_final_norm, v_c_ctx, v_w_mod, v_b_mod, v_norm1, v_w_in, v_na_rpb, v_q_gain, v_k_gain, v_w_pa, v_w_pb, v_w_o, v_norm2, v_w_ffn_in, v_w_ffn_out, v_final_norm):
    W = dict(c_ctx=c_ctx, w_mod=w_mod, b_mod=b_mod, norm1=norm1, w_in=w_in, na_rpb=na_rpb, q_gain=q_gain, k_gain=k_gain,
             w_pa=w_pa, w_pb=w_pb, w_o=w_o, norm2=norm2, w_ffn_in=w_ffn_in, w_ffn_out=w_ffn_out, final_norm=final_norm)
    M = dict(c_ctx=m_c_ctx, w_mod=m_w_mod, b_mod=m_b_mod, norm1=m_norm1, w_in=m_w_in, na_rpb=m_na_rpb, q_gain=m_q_gain,
             k_gain=m_k_gain, w_pa=m_w_pa, w_pb=m_w_pb, w_o=m_w_o, norm2=m_norm2, w_ffn_in=m_w_ffn_in,
             w_ffn_out=m_w_ffn_out, final_norm=m_final_norm)
    V = dict(c_ctx=v_c_ctx, w_mod=v_w_mod, b_mod=v_b_mod, norm1=v_norm1, w_in=v_w_in, na_rpb=v_na_rpb, q_gain=v_q_gain,
             k_gain=v_k_gain, w_pa=v_w_pa, w_pb=v_w_pb, w_o=v_w_o, norm2=v_norm2, w_ffn_in=v_w_ffn_in,
             w_ffn_out=v_w_ffn_out, final_norm=v_final_norm)
    order = ["c_ctx", "w_mod", "b_mod", "norm1", "w_in", "na_rpb", "q_gain", "k_gain", "w_pa", "w_pb", "w_o", "norm2",
             "w_ffn_in", "w_ffn_out", "final_norm"]
    L, D = norm1.shape
    n_lat = x.shape[1]
    ax, ay, ac = lax.axis_index("x"), lax.axis_index("y"), lax.axis_index("c")
    chip = 2 * ax + ay
    dev = 2 * chip + ac

    shard_shapes = {n: W[n].shape for n, _ in BIG}
    packed = _pack_shards({n: W[n].astype(BF16) for n, _ in BIG})
    gathered = _unpack_shards(weights_all_gather(packed), shard_shapes)
    wfull = {n: _whole_from_chips(gathered[n], kind) for n, kind in BIG}

    c_all = small_all_gather(jnp.pad(c, ((0, 7), (0, 0))), "gather_c")[:, 0, :]
    craw = jnp.pad(c_all, ((0, 8), (0, 0))) + jnp.pad(c_ctx[None, :], ((8, 7), (0, 0)))
    ncol_mod = w_mod.shape[2]
    mod_loc = mod_matmul_call(craw, w_mod)
    mod_all = small_all_gather(mod_loc.reshape(L * 16, ncol_mod), "gather_mod")
    mod_all = mod_all[0::2].reshape(N_CHIP, L, 16, ncol_mod).transpose(1, 2, 0, 3).reshape(L, 16, 6 * D)
    mod_all = mod_all + b_mod[:, None, :]
    mod_mine = lax.dynamic_index_in_dim(mod_all, dev, axis=1, keepdims=False)
    modv = jnp.stack([mod_mine, mod_all[:, 8]], axis=1).reshape(L, 2, 6, D)

    xs = jnp.concatenate([x[0], ctx[0]], axis=0)
    small = dict(norm1=norm1, norm2=norm2, na_rpb=na_rpb, q_gain=q_gain, k_gain=k_gain, final_norm=final_norm)
    loss_part, dx, gw, dmod, gs = device_fwd_bwd(xs, loss_target[0], modv, wfull, small, n_lat)
    loss = lax.psum(jnp.sum(loss_part), ("x", "y", "c"))
    grad_x = dx[:n_lat][None]

    gp = jnp.concatenate(
        [_chips_from_whole(gw[n], kind).reshape(N_CHIP, 2, -1, PACK_C).transpose(1, 0, 2, 3) for n, kind in BIG], axis=2)
    r1 = grads_swap_halves(gp)
    p1 = add_halves_call(gp, r1, ac)
    r2 = grads_to_owner(p1)
    gh = sum_owner_call(p1, r2, chip)
    gbig = _unpack_shards(grads_join_halves(gh), shard_shapes)

    n_mod_rows = L * 6 * D // PACK_C
    small_flat = jnp.concatenate([gs[n].reshape(-1) for n in SMALL])
    n_small_rows = -(-small_flat.size // (8 * PACK_C)) * 8
    g1 = jnp.concatenate([dmod[:, 0].reshape(n_mod_rows, PACK_C), dmod[:, 1].reshape(n_mod_rows, PACK_C),
                          _pad_rows(small_flat, n_small_rows)], axis=0)
    g1 = small_all_gather(g1, "gather_small_grads")
    dm_lat = g1[:, :n_mod_rows].reshape(N_DEV, L, 6 * D)
    dm_ctx_parts = g1[:, n_mod_rows:2 * n_mod_rows]
    dm_ctx = sum_parts_call(dm_ctx_parts).reshape(L, 1, 6 * D)
    dmod16 = (jnp.pad(dm_lat.transpose(1, 0, 2), ((0, 0), (0, 8), (0, 0)))
              + jnp.pad(dm_ctx, ((0, 0), (8, 7), (0, 0))))
    dmod16_loc = lax.dynamic_slice_in_dim(dmod16, chip * ncol_mod, ncol_mod, axis=2)
    g_wmod, dsilu_part = mod_bwd_call(craw, dmod16_loc, w_mod)
    ds_all = small_all_gather(dsilu_part[8:16], "gather_dsilu")
    ds_parts = ds_all[0::2, 0:1, :]
    ds_parts = jnp.pad(ds_parts, ((0, 0), (0, 7), (0, 0)))

    out_g, out_d, out_m, out_v = {}, {}, {}, {}
    for n, _ in BIG:
        out_g[n] = gbig[n]
        out_d[n], out_m[n], out_v[n] = adamw_call(W[n], gbig[n], M[n], V[n])
    out_g["w_mod"] = g_wmod
    out_d["w_mod"], out_m["w_mod"], out_v["w_mod"] = adamw_call(w_mod, g_wmod, m_w_mod, v_w_mod)

    pack_small = lambda t: _pad_rows(jnp.concatenate([t[n].reshape(-1) for n in SMALL]), n_small_rows)
    res = adamw_small_call(g1[:, 2 * n_mod_rows:], pack_small(W), pack_small(M), pack_small(V))
    off = 0
    for n in SMALL:
        sz = W[n].size
        for dst, r in zip((out_g, out_d, out_m, out_v), res):
            dst[n] = r.reshape(-1)[off:off + sz].reshape(W[n].shape)
        off += sz
    bparts = jnp.concatenate([g1[:, :n_mod_rows], dm_ctx_parts], axis=0)
    as_rows = lambda t: t.reshape(n_mod_rows, PACK_C)
    res = adamw_small_call(bparts, as_rows(b_mod), as_rows(m_b_mod), as_rows(v_b_mod))
    for dst, r in zip((out_g, out_d, out_m, out_v), res):
        dst["b_mod"] = r.reshape(b_mod.shape)
    row8 = lambda t: jnp.pad(t[None, :], ((0, 7), (0, 0)))
    res = adamw_small_call(ds_parts, row8(c_ctx), row8(m_c_ctx), row8(v_c_ctx), silu_bwd=True)
    for dst, r in zip((out_g, out_d, out_m, out_v), res):
        dst["c_ctx"] = r[0]

    return (loss, grad_x, *[out_g[n] for n in order], *[out_d[n] for n in order],
            *[out_m[n] for n in order], *[out_v[n] for n in order])
```

```python
import functools
import math

import numpy as np
import jax
import jax.numpy as jnp
from jax import lax
from jax.experimental import pallas as pl
from jax.experimental.pallas import tpu as pltpu

F32 = jnp.float32
BF16 = jnp.bfloat16
MESH = pl.DeviceIdType.MESH

HEAD_DIM = 64
NA_HEADS = 8
GQA_Q_HEADS = 8
GQA_KV_HEADS = 2
GRID_W = 64
NA_WIN_H = 8
NA_WIN_W = 16
ROPE_THETA = 10000.0
EPS = 1e-6
SCALE = HEAD_DIM ** -0.5
NEG = -1e30

ADAM_LR = 0.001
ADAM_B1 = 0.9
ADAM_B2 = 0.999
ADAM_EPS = 1e-08
ADAM_WD = 0.01
ADAM_STEP = 10

TM = 256
NA_ROWS = 4
NA_KROWS = 12
LANES = 128
VMEM_BIG = 56 * 1024 * 1024


def _cp(sem, vmem=None):
    return pltpu.CompilerParams(dimension_semantics=sem, vmem_limit_bytes=vmem)


def _dot(a, b):
    return jnp.dot(a, b, preferred_element_type=F32)


def _dot_nt(a, b):
    return lax.dot_general(a, b, (((1,), (1,)), ((), ())), preferred_element_type=F32)


def _dot_tn(a, b):
    return lax.dot_general(a, b, (((0,), (0,)), ((), ())), preferred_element_type=F32)


def _colsum8(v):
    tm, d = v.shape
    return v.reshape(tm // 8, 8, d).sum(axis=0)


def _acc(ref, val, first):
    @pl.when(first)
    def _():
        ref[0] = val

    @pl.when(jnp.logical_not(first))
    def _():
        ref[0] += val


def _rms_r(x):
    return lax.rsqrt(jnp.mean(x * x, axis=-1, keepdims=True) + EPS)


def _norm_mod(x, nw, sh, sc):
    return (x * _rms_r(x) * nw) * (1.0 + sc) + sh


def _norm_mod_bwd(dh, x, nw, sc):
    r = _rms_r(x)
    xn = x * r
    dxn = dh * (nw * (1.0 + sc))
    dx = r * (dxn - xn * jnp.mean(dxn * xn, axis=-1, keepdims=True))
    return dx, dh, dh * (xn * nw), dh * ((1.0 + sc) * xn)


def _sigmoid(x):
    return 1.0 / (1.0 + jnp.exp(-x))


def _group_of(i, n_lat_tiles):
    return jnp.where(i >= n_lat_tiles, 1, 0)


def _lane_half(shape):
    return (lax.broadcasted_iota(jnp.int32, shape, len(shape) - 1) % LANES) // HEAD_DIM


def _lane_lt64(shape):
    return _lane_half(shape) == 0


def _roll64(x):
    return pltpu.roll(x, HEAD_DIM, x.ndim - 1)


def norm_mod_call(x, nw, modv, n_lat_tiles, which):
    T, D = x.shape

    def body(x_ref, nw_ref, mod_ref, h_ref):
        sh = mod_ref[0, 3 * which:3 * which + 1, :]
        sc = mod_ref[0, 3 * which + 1:3 * which + 2, :]
        h_ref[...] = _norm_mod(x_ref[...], nw_ref[...], sh, sc).astype(BF16)

    return pl.pallas_call(
        body, name="norm_mod", grid=(T // TM,),
        in_specs=[pl.BlockSpec((TM, D), lambda i: (i, 0)),
                  pl.BlockSpec((1, D), lambda i: (0, 0)),
                  pl.BlockSpec((1, 6, D), lambda i: (_group_of(i, n_lat_tiles), 0, 0))],
        out_specs=pl.BlockSpec((TM, D), lambda i: (i, 0)),
        out_shape=jax.ShapeDtypeStruct((T, D), BF16),
        compiler_params=_cp(("parallel",)),
    )(x, nw, modv)


def qkv_call(h, w_in):
    T, D = h.shape
    N = w_in.shape[1]
    n_na = 3 * NA_HEADS * HEAD_DIM
    n_g = (GQA_Q_HEADS + 2 * GQA_KV_HEADS) * HEAD_DIM
    n_gate = N - n_na - n_g

    def body(h_ref, w_ref, na_ref, g_ref, gate_ref):
        acc = _dot(h_ref[...], w_ref[...])
        na_ref[...] = acc[:, :n_na].astype(BF16)
        g_ref[...] = acc[:, n_na:n_na + n_g].astype(BF16)
        gate_ref[...] = acc[:, n_na + n_g:].astype(BF16)

    return pl.pallas_call(
        body, name="qkv", grid=(T // TM,),
        in_specs=[pl.BlockSpec((TM, D), lambda i: (i, 0)),
                  pl.BlockSpec((D, N), lambda i: (0, 0))],
        out_specs=[pl.BlockSpec((TM, n_na), lambda i: (i, 0)),
                   pl.BlockSpec((TM, n_g), lambda i: (i, 0)),
                   pl.BlockSpec((TM, n_gate), lambda i: (i, 0))],
        out_shape=[jax.ShapeDtypeStruct((T, n_na), BF16),
                   jax.ShapeDtypeStruct((T, n_g), BF16),
                   jax.ShapeDtypeStruct((T, n_gate), BF16)],
        compiler_params=_cp(("parallel",), VMEM_BIG),
    )(h, w_in)


def merge_call(ya, yb, gates, w_pa, w_pb):
    T, Ka = ya.shape
    D = w_pa.shape[1]

    def body(ya_ref, yb_ref, ga_ref, gb_ref, wa_ref, wb_ref, m_ref, pa_ref, pb_ref):
        pa = _dot(ya_ref[...], wa_ref[...])
        pb = _dot(yb_ref[...], wb_ref[...])
        m = _sigmoid(ga_ref[...].astype(F32)) * pa + _sigmoid(gb_ref[...].astype(F32)) * pb
        m_ref[...] = m.astype(BF16)
        pa_ref[...] = pa.astype(BF16)
        pb_ref[...] = pb.astype(BF16)

    row = lambda w: pl.BlockSpec((TM, w), lambda i: (i, 0))
    return pl.pallas_call(
        body, name="merge", grid=(T // TM,),
        in_specs=[row(Ka), row(Ka),
                  pl.BlockSpec((TM, D), lambda i: (i, 0)), pl.BlockSpec((TM, D), lambda i: (i, 1)),
                  pl.BlockSpec((Ka, D), lambda i: (0, 0)), pl.BlockSpec((Ka, D), lambda i: (0, 0))],
        out_specs=[row(D), row(D), row(D)],
        out_shape=[jax.ShapeDtypeStruct((T, D), BF16)] * 3,
        compiler_params=_cp(("parallel",)),
    )(ya, yb, gates, gates, w_pa, w_pb)


def wo_call(merged, w_o, x, modv, nw2, n_lat_tiles):
    T, D = x.shape

    def body(m_ref, w_ref, x_ref, mod_ref, nw_ref, x1_ref, o_ref, h2_ref):
        o = _dot(m_ref[...], w_ref[...])
        x1 = x_ref[...] + mod_ref[0, 2:3, :] * o
        x1_ref[...] = x1
        o_ref[...] = o.astype(BF16)
        h2_ref[...] = _norm_mod(x1, nw_ref[...], mod_ref[0, 3:4, :], mod_ref[0, 4:5, :]).astype(BF16)

    row = pl.BlockSpec((TM, D), lambda i: (i, 0))
    return pl.pallas_call(
        body, name="wo", grid=(T // TM,),
        in_specs=[row, pl.BlockSpec((D, D), lambda i: (0, 0)), row,
                  pl.BlockSpec((1, 6, D), lambda i: (_group_of(i, n_lat_tiles), 0, 0)),
                  pl.BlockSpec((1, D), lambda i: (0, 0))],
        out_specs=[row, row, row],
        out_shape=[jax.ShapeDtypeStruct((T, D), F32), jax.ShapeDtypeStruct((T, D), BF16),
                   jax.ShapeDtypeStruct((T, D), BF16)],
        compiler_params=_cp(("parallel",)),
    )(merged, w_o, x, modv, nw2)


def ffn_in_call(h2, w_ffn_in, n_col_tiles=2):
    T, D = h2.shape
    F = w_ffn_in.shape[1] // 2
    tn = F // n_col_tiles

    def body(h_ref, wa_ref, wu_ref, act_ref, a_ref, u_ref):
        h = h_ref[...]
        a = _dot(h, wa_ref[...])
        u = _dot(h, wu_ref[...])
        act_ref[...] = (a * _sigmoid(a) * u).astype(BF16)
        a_ref[...] = a.astype(BF16)
        u_ref[...] = u.astype(BF16)

    blk = pl.BlockSpec((TM, tn), lambda j, i: (i, j))
    return pl.pallas_call(
        body, name="ffn_in", grid=(n_col_tiles, T // TM),
        in_specs=[pl.BlockSpec((TM, D), lambda j, i: (i, 0)),
                  pl.BlockSpec((D, tn), lambda j, i: (0, j)),
                  pl.BlockSpec((D, tn), lambda j, i: (0, n_col_tiles + j))],
        out_specs=[blk, blk, blk],
        out_shape=[jax.ShapeDtypeStruct((T, F), BF16)] * 3,
        compiler_params=_cp(("parallel", "parallel"), VMEM_BIG),
    )(h2, w_ffn_in, w_ffn_in)


def ffn_out_call(act, w_ffn_out, x1, modv, n_lat_tiles, next_nw=None, next_modv=None):
    T, D = x1.shape
    F = act.shape[1]
    with_next = next_nw is not None

    def body(*refs):
        if with_next:
            a_ref, w_ref, x_ref, mod_ref, nw_ref, nmod_ref, x2_ref, f_ref, hn_ref = refs
        else:
            a_ref, w_ref, x_ref, mod_ref, x2_ref, f_ref = refs
        f = _dot(a_ref[...], w_ref[...])
        x2 = x_ref[...] + mod_ref[0, 5:6, :] * f
        x2_ref[...] = x2
        f_ref[...] = f.astype(BF16)
        if with_next:
            hn_ref[...] = _norm_mod(x2, nw_ref[...], nmod_ref[0, 0:1, :], nmod_ref[0, 1:2, :]).astype(BF16)

    row = pl.BlockSpec((TM, D), lambda i: (i, 0))
    modspec = pl.BlockSpec((1, 6, D), lambda i: (_group_of(i, n_lat_tiles), 0, 0))
    in_specs = [pl.BlockSpec((TM, F), lambda i: (i, 0)), pl.BlockSpec((F, D), lambda i: (0, 0)), row, modspec]
    args = [act, w_ffn_out, x1, modv]
    out_specs = [row, row]
    out_shape = [jax.ShapeDtypeStruct((T, D), F32), jax.ShapeDtypeStruct((T, D), BF16)]
    if with_next:
        in_specs += [pl.BlockSpec((1, D), lambda i: (0, 0)), modspec]
        args += [next_nw, next_modv]
        out_specs.append(row)
        out_shape.append(jax.ShapeDtypeStruct((T, D), BF16))
    return pl.pallas_call(
        body, name="ffn_out", grid=(T // TM,), in_specs=in_specs, out_specs=out_specs, out_shape=out_shape,
        compiler_params=_cp(("parallel",), VMEM_BIG),
    )(*args)


def loss_call(x, nw, target, n_lat_tiles):
    T, D = x.shape

    def body(x_ref, nw_ref, t_ref, dx_ref, dw_ref, ls_ref):
        i = pl.program_id(0)

        @pl.when(i < n_lat_tiles)
        def _():
            xv = x_ref[...]
            nw_v = nw_ref[...]
            r = _rms_r(xv)
            xn = xv * r
            e = xn * nw_v - t_ref[...]
            dy = e * (1.0 / D)
            dxn = dy * nw_v
            dx_ref[...] = r * (dxn - xn * jnp.mean(dxn * xn, axis=-1, keepdims=True))
            _acc(dw_ref, _colsum8(dy * xn), i == 0)
            _acc(ls_ref, _colsum8(e * e * (0.5 / D)), i == 0)

        @pl.when(i >= n_lat_tiles)
        def _():
            dx_ref[...] = jnp.zeros((TM, D), F32)

    stat = pl.BlockSpec((1, 8, D), lambda i: (0, 0, 0))
    return pl.pallas_call(
        body, name="loss", grid=(T // TM,),
        in_specs=[pl.BlockSpec((TM, D), lambda i: (i, 0)), pl.BlockSpec((1, D), lambda i: (0, 0)),
                  pl.BlockSpec((TM, D), lambda i: (jnp.minimum(i, n_lat_tiles - 1), 0))],
        out_specs=[pl.BlockSpec((TM, D), lambda i: (i, 0)), stat, stat],
        out_shape=[jax.ShapeDtypeStruct((T, D), F32), jax.ShapeDtypeStruct((1, 8, D), F32),
                   jax.ShapeDtypeStruct((1, 8, D), F32)],
        compiler_params=_cp(("arbitrary",)),
    )(x, nw, target)


def dact_call(dx2, modv, fsave, w_ffn_out, a, u, n_lat_tiles):
    T, D = dx2.shape
    F = a.shape[1]

    def body(dx_ref, mod_ref, f_ref, w_ref, a_ref, u_ref, da_ref, du_ref, df_ref, dg_ref):
        i = pl.program_id(0)
        dx = dx_ref[...]
        df = (dx * mod_ref[0, 5:6, :]).astype(BF16)
        df_ref[...] = df
        dact = _dot_nt(df, w_ref[...])
        av = a_ref[...].astype(F32)
        uv = u_ref[...].astype(F32)
        sg = _sigmoid(av)
        da_ref[...] = (dact * uv * (sg * (1.0 + av * (1.0 - sg)))).astype(BF16)
        du_ref[...] = (dact * (av * sg)).astype(BF16)
        _acc(dg_ref, _colsum8(dx * f_ref[...].astype(F32)), (i == 0) | (i == n_lat_tiles))

    row = pl.BlockSpec((TM, D), lambda i: (i, 0))
    wide = pl.BlockSpec((TM, F), lambda i: (i, 0))
    grp = lambda i: (_group_of(i, n_lat_tiles), 0, 0)
    return pl.pallas_call(
        body, name="dact", grid=(T // TM,),
        in_specs=[row, pl.BlockSpec((1, 6, D), grp), row, pl.BlockSpec((F, D), lambda i: (0, 0)), wide, wide],
        out_specs=[wide, wide, row, pl.BlockSpec((1, 8, D), grp)],
        out_shape=[jax.ShapeDtypeStruct((T, F), BF16), jax.ShapeDtypeStruct((T, F), BF16),
                   jax.ShapeDtypeStruct((T, D), BF16), jax.ShapeDtypeStruct((2, 8, D), F32)],
        compiler_params=_cp(("arbitrary",), VMEM_BIG),
    )(dx2, modv, fsave, w_ffn_out, a, u)


def _norm_bwd_tail(dh, x_ref, mod_ref, nw_ref, dres_ref, dx_ref, s_sh, s_sc, s_w, which, first):
    sc = mod_ref[0, 3 * which + 1:3 * which + 2, :]
    dxn, t_sh, t_sc, t_w = _norm_mod_bwd(dh, x_ref[...], nw_ref[...], sc)
    dx_ref[...] = dres_ref[...] + dxn
    _acc(s_sh, _colsum8(t_sh), first)
    _acc(s_sc, _colsum8(t_sc), first)
    _acc(s_w, _colsum8(t_w), first)


def dh2_call(da, du, w_ffn_in, x1, modv, nw2, dx2, n_lat_tiles):
    T, D = x1.shape
    F = da.shape[1]

    def body(da_ref, du_ref, wa_ref, wu_ref, x_ref, mod_ref, nw_ref, dres_ref, dx_ref, s_sh, s_sc, s_w):
        i = pl.program_id(0)
        dh = _dot_nt(da_ref[...], wa_ref[...]) + _dot_nt(du_ref[...], wu_ref[...])
        _norm_bwd_tail(dh, x_ref, mod_ref, nw_ref, dres_ref, dx_ref, s_sh, s_sc, s_w, 1,
                       (i == 0) | (i == n_lat_tiles))

    row = pl.BlockSpec((TM, D), lambda i: (i, 0))
    wide = pl.BlockSpec((TM, F), lambda i: (i, 0))
    grp = lambda i: (_group_of(i, n_lat_tiles), 0, 0)
    stat = pl.BlockSpec((1, 8, D), grp)
    return pl.pallas_call(
        body, name="dh2", grid=(T // TM,),
        in_specs=[wide, wide, pl.BlockSpec((D, F), lambda i: (0, 0)), pl.BlockSpec((D, F), lambda i: (0, 1)),
                  row, pl.BlockSpec((1, 6, D), grp), pl.BlockSpec((1, D), lambda i: (0, 0)), row],
        out_specs=[row, stat, stat, stat],
        out_shape=[jax.ShapeDtypeStruct((T, D), F32)] + [jax.ShapeDtypeStruct((2, 8, D), F32)] * 3,
        compiler_params=_cp(("arbitrary",), VMEM_BIG),
    )(da, du, w_ffn_in, w_ffn_in, x1, modv, nw2, dx2)


def dmerged_call(dx1, modv, osave, w_o, pa, pb, gates, n_lat_tiles):
    T, D = dx1.shape

    def body(dx_ref, mod_ref, o_ref, w_ref, pa_ref, pb_ref, ga_ref, gb_ref,
             dpa_ref, dpb_ref, dgate_ref, do_ref, dg_ref):
        i = pl.program_id(0)
        dx = dx_ref[...]
        do = (dx * mod_ref[0, 2:3, :]).astype(BF16)
        do_ref[...] = do
        dm = _dot_nt(do, w_ref[...])
        sa = _sigmoid(ga_ref[...].astype(F32))
        sb = _sigmoid(gb_ref[...].astype(F32))
        dpa_ref[...] = (dm * sa).astype(BF16)
        dpb_ref[...] = (dm * sb).astype(BF16)
        dgate_ref[:, :D] = (dm * pa_ref[...].astype(F32) * (sa * (1.0 - sa))).astype(BF16)
        dgate_ref[:, D:] = (dm * pb_ref[...].astype(F32) * (sb * (1.0 - sb))).astype(BF16)
        _acc(dg_ref, _colsum8(dx * o_ref[...].astype(F32)), (i == 0) | (i == n_lat_tiles))

    row = pl.BlockSpec((TM, D), lambda i: (i, 0))
    grp = lambda i: (_group_of(i, n_lat_tiles), 0, 0)
    return pl.pallas_call(
        body, name="dmerged", grid=(T // TM,),
        in_specs=[row, pl.BlockSpec((1, 6, D), grp), row, pl.BlockSpec((D, D), lambda i: (0, 0)), row, row,
                  pl.BlockSpec((TM, D), lambda i: (i, 0)), pl.BlockSpec((TM, D), lambda i: (i, 1))],
        out_specs=[row, row, pl.BlockSpec((TM, 2 * D), lambda i: (i, 0)), row, pl.BlockSpec((1, 8, D), grp)],
        out_shape=[jax.ShapeDtypeStruct((T, D), BF16), jax.ShapeDtypeStruct((T, D), BF16),
                   jax.ShapeDtypeStruct((T, 2 * D), BF16), jax.ShapeDtypeStruct((T, D), BF16),
                   jax.ShapeDtypeStruct((2, 8, D), F32)],
        compiler_params=_cp(("arbitrary",)),
    )(dx1, modv, osave, w_o, pa, pb, gates, gates)


def dy_call(dpa, dpb, w_pa, w_pb):
    T, D = dpa.shape
    K = w_pa.shape[0]

    def body(a_ref, b_ref, wa_ref, wb_ref, ya_ref, yb_ref):
        ya_ref[...] = _dot_nt(a_ref[...], wa_ref[...]).astype(BF16)
        yb_ref[...] = _dot_nt(b_ref[...], wb_ref[...]).astype(BF16)

    row = pl.BlockSpec((TM, D), lambda i: (i, 0))
    w = pl.BlockSpec((K, D), lambda i: (0, 0))
    out = pl.BlockSpec((TM, K), lambda i: (i, 0))
    return pl.pallas_call(
        body, name="dy", grid=(T // TM,), in_specs=[row, row, w, w], out_specs=[out, out],
        out_shape=[jax.ShapeDtypeStruct((T, K), BF16)] * 2,
        compiler_params=_cp(("parallel",)),
    )(dpa, dpb, w_pa, w_pb)


def dh_call(dna, dg, dgates, w_in, x, modv, nw1, dx1, n_lat_tiles):
    T, D = x.shape
    n1, n2, n3 = dna.shape[1], dg.shape[1], dgates.shape[1]
    N = n1 + n2 + n3

    def body(a_ref, b_ref, c_ref, w_ref, x_ref, mod_ref, nw_ref, dres_ref, dx_ref, s_sh, s_sc, s_w):
        i = pl.program_id(0)
        dh = (_dot_nt(a_ref[...], w_ref[:, :n1]) + _dot_nt(b_ref[...], w_ref[:, n1:n1 + n2])
              + _dot_nt(c_ref[...], w_ref[:, n1 + n2:]))
        _norm_bwd_tail(dh, x_ref, mod_ref, nw_ref, dres_ref, dx_ref, s_sh, s_sc, s_w, 0,
                       (i == 0) | (i == n_lat_tiles))

    row = pl.BlockSpec((TM, D), lambda i: (i, 0))
    grp = lambda i: (_group_of(i, n_lat_tiles), 0, 0)
    stat = pl.BlockSpec((1, 8, D), grp)
    return pl.pallas_call(
        body, name="dh", grid=(T // TM,),
        in_specs=[pl.BlockSpec((TM, n1), lambda i: (i, 0)), pl.BlockSpec((TM, n2), lambda i: (i, 0)),
                  pl.BlockSpec((TM, n3), lambda i: (i, 0)), pl.BlockSpec((D, N), lambda i: (0, 0)),
                  row, pl.BlockSpec((1, 6, D), grp), pl.BlockSpec((1, D), lambda i: (0, 0)), row],
        out_specs=[row, stat, stat, stat],
        out_shape=[jax.ShapeDtypeStruct((T, D), F32)] + [jax.ShapeDtypeStruct((2, 8, D), F32)] * 3,
        compiler_params=_cp(("arbitrary",), VMEM_BIG),
    )(dna, dg, dgates, w_in, x, modv, nw1, dx1)


def _pick_tile(n, cap, mult):
    best = None
    for d in range(mult, min(n, cap) + 1, mult):
        if n % d == 0:
            best = d
    assert best is not None, (n, cap, mult)
    return best


def wgrad_call(a, b, name):
    T, K = a.shape
    N = b.shape[1]
    tt = _pick_tile(T, 768, 128)
    tk = K if K <= 1024 else _pick_tile(K, 1408, 128)
    tn = N if N <= 2304 else _pick_tile(N, 2304, 128)
    nt = T // tt

    def body(a_ref, b_ref, o_ref, acc_ref):
        t = pl.program_id(2)
        p = _dot_tn(a_ref[...], b_ref[...])

        @pl.when(t == 0)
        def _():
            acc_ref[...] = p

        @pl.when(t > 0)
        def _():
            acc_ref[...] += p

        @pl.when(t == nt - 1)
        def _():
            o_ref[...] = acc_ref[...].astype(BF16)

    return pl.pallas_call(
        body, name=name, grid=(K // tk, N // tn, nt),
        in_specs=[pl.BlockSpec((tt, tk), lambda i, j, t: (t, i)), pl.BlockSpec((tt, tn), lambda i, j, t: (t, j))],
        out_specs=pl.BlockSpec((tk, tn), lambda i, j, t: (i, j)),
        out_shape=jax.ShapeDtypeStruct((K, N), BF16),
        scratch_shapes=[pltpu.VMEM((tk, tn), F32)],
        compiler_params=_cp(("parallel", "parallel", "arbitrary"), VMEM_BIG),
    )(a, b)


def _head_mean(v):
    n = v.shape[-1]
    r = lax.broadcasted_iota(jnp.int32, (n, n), 0) // HEAD_DIM
    c = lax.broadcasted_iota(jnp.int32, (n, n), 1) // HEAD_DIM
    bd = jnp.where(r == c, 1.0 / HEAD_DIM, 0.0).astype(BF16)
    hi = v.astype(BF16)
    lo = (v - hi.astype(F32)).astype(BF16)
    return _dot(hi, bd) + _dot(lo, bd)


def _swap_pairs(x):
    n = x.shape[-1]
    even = lax.broadcasted_iota(jnp.int32, x.shape, x.ndim - 1) % 2 == 0
    return jnp.where(even, pltpu.roll(x, n - 1, x.ndim - 1), pltpu.roll(x, 1, x.ndim - 1))


def _tile_lanes(t, n):
    return jnp.tile(t, (1, n // t.shape[-1]))


def gqa_prep_call(gqkv, qgain, kgain, cos_t, sin_t):
    T = gqkv.shape[0]
    nq = GQA_Q_HEADS * HEAD_DIM
    nk = GQA_KV_HEADS * HEAD_DIM
    rep = GQA_Q_HEADS // GQA_KV_HEADS

    def body(g_ref, qg_ref, kg_ref, c_ref, s_ref, qp_ref, kp_ref):
        cos_v, sin_v = c_ref[...], s_ref[...]
        xq = g_ref[:, :nq].astype(F32)
        yq = xq * lax.rsqrt(_head_mean(xq * xq) + EPS) * qg_ref[...]
        yq = (yq * _tile_lanes(cos_v, nq) + _swap_pairs(yq) * _tile_lanes(sin_v, nq)) * SCALE
        lo = _lane_lt64((TM, LANES))
        for p in range(GQA_Q_HEADS // 2):
            g = (2 * p) // rep
            chunk = yq[:, LANES * p:LANES * (p + 1)]
            rolled = _roll64(chunk)
            if g == 0:
                first, second = jnp.where(lo, chunk, 0.0), jnp.where(lo, rolled, 0.0)
            else:
                first, second = jnp.where(lo, 0.0, rolled), jnp.where(lo, 0.0, chunk)
            qp_ref[:, 2 * LANES * p:2 * LANES * p + LANES] = first.astype(BF16)
            qp_ref[:, 2 * LANES * p + LANES:2 * LANES * (p + 1)] = second.astype(BF16)
        xk = g_ref[:, nq:nq + nk].astype(F32)
        yk = xk * lax.rsqrt(_head_mean(xk * xk) + EPS) * kg_ref[...]
        kp_ref[...] = (yk * cos_v + _swap_pairs(yk) * sin_v).astype(BF16)

    return pl.pallas_call(
        body, name="gqa_prep", grid=(T // TM,),
        in_specs=[pl.BlockSpec((TM, gqkv.shape[1]), lambda i: (i, 0)),
                  pl.BlockSpec((1, nq), lambda i: (0, 0)), pl.BlockSpec((1, nk), lambda i: (0, 0)),
                  pl.BlockSpec((TM, nk), lambda i: (i, 0)), pl.BlockSpec((TM, nk), lambda i: (i, 0))],
        out_specs=[pl.BlockSpec((TM, GQA_Q_HEADS * LANES), lambda i: (i, 0)), pl.BlockSpec((TM, nk), lambda i: (i, 0))],
        out_shape=[jax.ShapeDtypeStruct((T, GQA_Q_HEADS * LANES), BF16), jax.ShapeDtypeStruct((T, nk), BF16)],
        compiler_params=_cp(("parallel",)),
    )(gqkv, qgain, kgain, cos_t, sin_t)


def gqa_prep_bwd_call(gqkv, qgain, kgain, cos_t, sin_t, dqp, dkp, dv):
    T = gqkv.shape[0]
    nq = GQA_Q_HEADS * HEAD_DIM
    nk = GQA_KV_HEADS * HEAD_DIM
    rep = GQA_Q_HEADS // GQA_KV_HEADS

    def norm_rope_bwd(x, gain, dy, cos_v, sin_v):
        dyn = dy * cos_v - _swap_pairs(dy) * sin_v
        r = lax.rsqrt(_head_mean(x * x) + EPS)
        xn = x * r
        dxn = dyn * gain
        dx = r * (dxn - xn * _head_mean(dxn * xn))
        return dx, dyn * xn

    def body(g_ref, qg_ref, kg_ref, c_ref, s_ref, dqp_ref, dkp_ref, dv_ref, dg_ref, sq_ref, sk_ref):
        i = pl.program_id(0)
        cos_v, sin_v = c_ref[...], s_ref[...]
        lo = _lane_lt64((TM, LANES))
        chunks = []
        for p in range(GQA_Q_HEADS // 2):
            g = (2 * p) // rep
            da = dqp_ref[:, 2 * LANES * p:2 * LANES * p + LANES]
            db = dqp_ref[:, 2 * LANES * p + LANES:2 * LANES * (p + 1)]
            if g == 0:
                chunks.append(jnp.where(lo, da, _roll64(db)))
            else:
                chunks.append(jnp.where(lo, _roll64(da), db))
        dyq = jnp.concatenate(chunks, axis=1) * SCALE
        dxq, gq_term = norm_rope_bwd(g_ref[:, :nq].astype(F32), qg_ref[...], dyq,
                                     _tile_lanes(cos_v, nq), _tile_lanes(sin_v, nq))
        dxk, gk_term = norm_rope_bwd(g_ref[:, nq:nq + nk].astype(F32), kg_ref[...], dkp_ref[...], cos_v, sin_v)
        dg_ref[:, :nq] = dxq.astype(BF16)
        dg_ref[:, nq:nq + nk] = dxk.astype(BF16)
        dg_ref[:, nq + nk:] = dv_ref[...].astype(BF16)
        _acc(sq_ref, _colsum8(gq_term), i == 0)
        _acc(sk_ref, _colsum8(gk_term), i == 0)

    W = gqkv.shape[1]
    return pl.pallas_call(
        body, name="gqa_prep_bwd", grid=(T // TM,),
        in_specs=[pl.BlockSpec((TM, W), lambda i: (i, 0)),
                  pl.BlockSpec((1, nq), lambda i: (0, 0)), pl.BlockSpec((1, nk), lambda i: (0, 0)),
                  pl.BlockSpec((TM, nk), lambda i: (i, 0)), pl.BlockSpec((TM, nk), lambda i: (i, 0)),
                  pl.BlockSpec((TM, GQA_Q_HEADS * LANES), lambda i: (i, 0)),
                  pl.BlockSpec((TM, nk), lambda i: (i, 0)), pl.BlockSpec((TM, nk), lambda i: (i, 0))],
        out_specs=[pl.BlockSpec((TM, W), lambda i: (i, 0)),
                   pl.BlockSpec((1, 8, nq), lambda i: (0, 0, 0)), pl.BlockSpec((1, 8, nk), lambda i: (0, 0, 0))],
        out_shape=[jax.ShapeDtypeStruct((T, W), BF16), jax.ShapeDtypeStruct((1, 8, nq), F32),
                   jax.ShapeDtypeStruct((1, 8, nk), F32)],
        compiler_params=_cp(("arbitrary",)),
    )(gqkv, qgain, kgain, cos_t, sin_t, dqp, dkp, dv)


GQ_TQ = 256
GQ_TK = 256


def _chunk_offset(j):
    return j * GQ_TK if isinstance(j, int) else pl.multiple_of(j * GQ_TK, GQ_TK)


GQ_CHUNKS_PER_TRIP = 8


def _sweep_keys(chunk, is_latent_query, n_lat_k, nk):
    per_trip = math.gcd(n_lat_k, GQ_CHUNKS_PER_TRIP)

    @pl.when(is_latent_query)
    def _():
        def trip(jj, carry):
            for t in range(per_trip):
                chunk(per_trip * jj + t)
            return carry

        lax.fori_loop(0, n_lat_k // per_trip, trip, 0)

    for j in range(n_lat_k, nk):
        chunk(j)


def _gqa_place(chunk, half, g):
    gv = jnp.broadcast_to(g, chunk.shape)
    placed = jnp.where(gv == half, chunk, _roll64(chunk))
    return jnp.where(_lane_half(chunk.shape) == gv, placed, jnp.zeros_like(placed))


GQA_SMALL_SCORE = 20.0


def gqa_scores_small(q_gain, k_gain):
    bound = SCALE * HEAD_DIM * jnp.max(jnp.abs(q_gain)) * jnp.max(jnp.abs(k_gain))
    return (bound <= GQA_SMALL_SCORE).astype(jnp.int32).reshape(1)


def gqa_fwd_call(qp, kp, gqkv, n_lat, scores_small):
    T = qp.shape[0]
    rep = GQA_Q_HEADS // GQA_KV_HEADS
    nq = T // GQ_TQ
    nk = T // GQ_TK
    n_lat_q = n_lat // GQ_TQ
    n_lat_k = n_lat // GQ_TK
    v_blk = (GQA_Q_HEADS + GQA_KV_HEADS) * HEAD_DIM // LANES
    R = rep * GQ_TQ

    def body(small_ref, q_ref, k_ref, v_ref, y_ref, lse_ref, m_sc, acc_sc):
        g = pl.program_id(0)
        qb = pl.program_id(1)
        qs = jnp.concatenate([q_ref[:, LANES * i:LANES * (i + 1)] for i in range(rep)], axis=0)
        is_lat = qb < n_lat_q
        nsub = GQ_TK // LANES
        scores_small = small_ref[0] == 1

        @pl.when(scores_small)
        def _():
            m_sc[...] = jnp.zeros((R, LANES), F32)

        @pl.when(jnp.logical_not(scores_small))
        def _():
            m_sc[...] = jnp.full((R, LANES), NEG, F32)

            def chunk_max(j):
                s = _dot_nt(qs, k_ref[pl.ds(_chunk_offset(j), GQ_TK), :])
                mm = s[:, :LANES]
                for t in range(1, nsub):
                    mm = jnp.maximum(mm, s[:, LANES * t:LANES * (t + 1)])
                m_sc[...] = jnp.maximum(m_sc[...], mm)

            _sweep_keys(chunk_max, is_lat, n_lat_k, nk)
            m_sc[...] = jnp.broadcast_to(jnp.max(m_sc[...], axis=-1, keepdims=True), (R, LANES))

        acc_sc[...] = jnp.zeros((R, LANES), F32)
        own_v = _lane_half((GQ_TK, LANES)) == jnp.broadcast_to(g, (GQ_TK, LANES))

        def chunk_acc(j):
            off = _chunk_offset(j)
            vc = v_ref[pl.ds(off, GQ_TK), :]
            v1 = jnp.where(own_v, vc, jnp.ones_like(vc))
            s = _dot_nt(qs, k_ref[pl.ds(off, GQ_TK), :])
            mb = m_sc[...]
            p = jnp.concatenate([jnp.exp(s[:, LANES * t:LANES * (t + 1)] - mb).astype(BF16) for t in range(nsub)],
                                axis=1)
            acc_sc[...] += _dot(p, v1)

        _sweep_keys(chunk_acc, is_lat, n_lat_k, nk)
        acc = acc_sc[...]
        own = _lane_half((R, LANES)) == jnp.broadcast_to(g, (R, LANES))
        l = jnp.where(own, _roll64(acc), acc)
        o = acc / l
        lse = m_sc[...] + jnp.log(l)
        lo = _lane_lt64((GQ_TQ, LANES))
        g0 = jnp.broadcast_to(g, (GQ_TQ, LANES)) == 0
        for c2 in range(rep // 2):
            oa = o[(2 * c2) * GQ_TQ:(2 * c2 + 1) * GQ_TQ]
            ob = o[(2 * c2 + 1) * GQ_TQ:(2 * c2 + 2) * GQ_TQ]
            left = jnp.where(g0, oa, _roll64(oa))
            right = jnp.where(g0, _roll64(ob), ob)
            y_ref[:, LANES * c2:LANES * (c2 + 1)] = jnp.where(lo, left, right).astype(BF16)
        for i in range(rep):
            lse_ref[i] = lse[i * GQ_TQ:(i + 1) * GQ_TQ]

    return pl.pallas_call(
        body, name="gqa_fwd",
        grid_spec=pltpu.PrefetchScalarGridSpec(
            num_scalar_prefetch=1, grid=(GQA_KV_HEADS, nq),
            in_specs=[pl.BlockSpec((GQ_TQ, rep * LANES), lambda g, i, s: (i, g)),
                      pl.BlockSpec((T, LANES), lambda g, i, s: (0, 0)),
                      pl.BlockSpec((T, LANES), lambda g, i, s: (0, v_blk))],
            out_specs=[pl.BlockSpec((GQ_TQ, rep * HEAD_DIM), lambda g, i, s: (i, g)),
                       pl.BlockSpec((rep, GQ_TQ, LANES), lambda g, i, s: (g, i, 0))],
            scratch_shapes=[pltpu.VMEM((R, LANES), F32), pltpu.VMEM((R, LANES), F32)]),
        out_shape=[jax.ShapeDtypeStruct((T, GQA_Q_HEADS * HEAD_DIM), BF16),
                   jax.ShapeDtypeStruct((GQA_Q_HEADS, T, LANES), F32)],
        compiler_params=_cp(("parallel", "parallel"), VMEM_BIG),
    )(scores_small, qp, kp, gqkv)


def gqa_bwd_call(qp, kp, gqkv, yb, dyb, lse, n_lat):
    T = qp.shape[0]
    rep = GQA_Q_HEADS // GQA_KV_HEADS
    nq = T // GQ_TQ
    nk = T // GQ_TK
    n_lat_q = n_lat // GQ_TQ
    n_lat_k = n_lat // GQ_TK
    v_blk = (GQA_Q_HEADS + GQA_KV_HEADS) * HEAD_DIM // LANES
    R = rep * GQ_TQ

    def body(q_ref, k_ref, v_ref, y_ref, dy_ref, lse_ref, dq_ref, dk_ref, dv_ref, dq_sc):
        g = pl.program_id(0)
        qb = pl.program_id(1)

        @pl.when((g == 0) & (qb == 0))
        def _():
            dk_ref[...] = jnp.zeros((T, LANES), F32)
            dv_ref[...] = jnp.zeros((T, LANES), F32)

        qs = jnp.concatenate([q_ref[:, LANES * i:LANES * (i + 1)] for i in range(rep)], axis=0)
        dos, deltas, lses = [], [], []
        for i in range(rep):
            c2, half = i // 2, i % 2
            dch = _gqa_place(dy_ref[:, LANES * c2:LANES * (c2 + 1)].astype(F32), half, g)
            ych = _gqa_place(y_ref[:, LANES * c2:LANES * (c2 + 1)].astype(F32), half, g)
            dos.append(dch.astype(BF16))
            deltas.append(jnp.sum(dch * ych, axis=-1, keepdims=True))
            lses.append(lse_ref[i])
        do_s = jnp.concatenate(dos, axis=0)
        delta = jnp.broadcast_to(jnp.concatenate(deltas, axis=0), (R, LANES))
        lse_v = jnp.concatenate(lses, axis=0)
        dq_sc[...] = jnp.zeros((R, LANES), F32)
        nsub = GQ_TK // LANES

        def chunk(j):
            off = _chunk_offset(j)
            kc = k_ref[pl.ds(off, GQ_TK), :]
            vc = v_ref[pl.ds(off, GQ_TK), :]
            s = _dot_nt(qs, kc)
            dp = _dot_nt(do_s, vc)
            ps, dss = [], []
            for t in range(nsub):
                pt = jnp.exp(s[:, LANES * t:LANES * (t + 1)] - lse_v)
                ps.append(pt.astype(BF16))
                dss.append((pt * (dp[:, LANES * t:LANES * (t + 1)] - delta)).astype(BF16))
            p = jnp.concatenate(ps, axis=1)
            ds = jnp.concatenate(dss, axis=1)
            dv_ref[pl.ds(off, GQ_TK), :] += _dot_tn(p, do_s)
            dk_ref[pl.ds(off, GQ_TK), :] += _dot_tn(ds, qs)
            dq_sc[...] += _dot(ds, kc)

        _sweep_keys(chunk, qb < n_lat_q, n_lat_k, nk)
        for i in range(rep):
            dq_ref[:, LANES * i:LANES * (i + 1)] = dq_sc[i * GQ_TQ:(i + 1) * GQ_TQ, :]

    whole = pl.BlockSpec((T, LANES), lambda g, i: (0, 0))
    return pl.pallas_call(
        body, name="gqa_bwd", grid=(GQA_KV_HEADS, nq),
        in_specs=[pl.BlockSpec((GQ_TQ, rep * LANES), lambda g, i: (i, g)),
                  whole,
                  pl.BlockSpec((T, LANES), lambda g, i: (0, v_blk)),
                  pl.BlockSpec((GQ_TQ, rep * HEAD_DIM), lambda g, i: (i, g)),
                  pl.BlockSpec((GQ_TQ, rep * HEAD_DIM), lambda g, i: (i, g)),
                  pl.BlockSpec((rep, GQ_TQ, LANES), lambda g, i: (g, i, 0))],
        out_specs=[pl.BlockSpec((GQ_TQ, rep * LANES), lambda g, i: (i, g)), whole, whole],
        out_shape=[jax.ShapeDtypeStruct((T, GQA_Q_HEADS * LANES), F32),
                   jax.ShapeDtypeStruct((T, LANES), F32), jax.ShapeDtypeStruct((T, LANES), F32)],
        scratch_shapes=[pltpu.VMEM((R, LANES), F32)],
        compiler_params=_cp(("arbitrary", "arbitrary"), VMEM_BIG),
    )(qp, kp, gqkv, yb, dyb, lse)


def _na_tables(rows):
    kh = min(NA_WIN_H, rows)
    assert kh == NA_WIN_H and rows >= NA_KROWS and rows % NA_ROWS == 0
    a = np.zeros((3, NA_ROWS, NA_KROWS, 2 * NA_WIN_H - 1), np.float32)
    for v, r0 in enumerate((0, NA_ROWS, rows - NA_ROWS)):
        ks = min(max(r0 - NA_WIN_H // 2, 0), rows - NA_KROWS)
        for rq in range(NA_ROWS):
            r = r0 + rq
            rs = min(max(r - kh // 2, 0), rows - kh)
            for rk in range(NA_KROWS):
                kr = ks + rk
                if rs <= kr < rs + kh:
                    a[v, rq, rk, kr - r + NA_WIN_H - 1] = 1.0
    c = np.zeros((GRID_W, GRID_W, 2 * NA_WIN_W - 1), np.float32)
    for wq in range(GRID_W):
        cs = min(max(wq - NA_WIN_W // 2, 0), GRID_W - NA_WIN_W)
        for wk in range(cs, cs + NA_WIN_W):
            c[wq, wk, min(max(wk - wq, -(NA_WIN_W - 1)), NA_WIN_W - 1) + NA_WIN_W - 1] = 1.0
    return a, c


def na_bias_table(rpb, rows):
    a, c = _na_tables(rows)
    hp = lax.Precision.HIGHEST
    t1 = jnp.einsum("hde,wue->hdwu", rpb, jnp.asarray(c), precision=hp)
    b = jnp.einsum("hdwu,vqkd->hvqwku", t1, jnp.asarray(a), precision=hp)
    ok = np.einsum("vqk,wu->vqwku", a.sum(-1), c.sum(-1)) > 0
    b = jnp.where(jnp.asarray(ok)[None], b, NEG)
    return b.reshape(rpb.shape[0], 3, NA_ROWS * GRID_W, NA_KROWS * GRID_W)


def na_bias_grad(dbias, rows):
    a, c = _na_tables(rows)
    hp = lax.Precision.HIGHEST
    d = dbias.reshape(dbias.shape[0], 3, NA_ROWS, GRID_W, NA_KROWS, GRID_W)
    t1 = jnp.einsum("hvqwku,vqkd->hdwu", d, jnp.asarray(a), precision=hp)
    return jnp.einsum("hdwu,wue->hde", t1, jnp.asarray(c), precision=hp)


def _na_variant(blk, n_lat_blk):
    return jnp.where(blk == 0, 0, jnp.where(blk >= n_lat_blk - 1, 2, 1))


def _na_kstart(blk, rows):
    r0 = blk * NA_ROWS
    ks = jnp.clip(r0 - NA_WIN_H // 2, 0, rows - NA_KROWS)
    return pl.multiple_of(ks * GRID_W, GRID_W)


def na_fwd_call(naqkv, bias, n_lat):
    T = naqkv.shape[0]
    nb = T // TM
    n_lat_blk = n_lat // TM
    rows = n_lat // GRID_W
    n_ctx = T - n_lat
    KW = NA_KROWS * GRID_W
    npair = NA_HEADS // 2

    def body(q_ref, k_ref, v_ref, b_ref, y_ref, lse_ref):
        blk = pl.program_id(1)
        half = _lane_half((TM, LANES))
        lo = half == 0
        q = q_ref[...].astype(F32) * SCALE
        kctx = k_ref[pl.ds(n_lat, n_ctx), :]
        vctx = v_ref[pl.ds(n_lat, n_ctx), :]

        @pl.when(blk < n_lat_blk)
        def _():
            off = _na_kstart(blk, rows)
            kwin = k_ref[pl.ds(off, KW), :]
            vwin = v_ref[pl.ds(off, KW), :]
            outs = []
            for hh in range(2):
                qm = jnp.where(half == hh, q, 0.0).astype(BF16)
                sw = _dot_nt(qm, kwin) + b_ref[hh, 0]
                sc = _dot_nt(qm, kctx)
                m = jnp.maximum(jnp.max(sw, axis=-1, keepdims=True), jnp.max(sc, axis=-1, keepdims=True))
                pw = jnp.exp(sw - m)
                pc = jnp.exp(sc - m)
                l = jnp.sum(pw, axis=-1, keepdims=True) + jnp.sum(pc, axis=-1, keepdims=True)
                outs.append((_dot(pw.astype(BF16), vwin) + _dot(pc.astype(BF16), vctx)) / l)
                lse_ref[hh] = jnp.broadcast_to(m + jnp.log(l), (TM, LANES))
            y_ref[...] = jnp.where(lo, outs[0], outs[1]).astype(BF16)

        @pl.when(blk >= n_lat_blk)
        def _():
            outs = []
            for hh in range(2):
                qm = jnp.where(half == hh, q, 0.0).astype(BF16)
                sc = _dot_nt(qm, kctx)
                m = jnp.max(sc, axis=-1, keepdims=True)
                pc = jnp.exp(sc - m)
                l = jnp.sum(pc, axis=-1, keepdims=True)
                outs.append(_dot(pc.astype(BF16), vctx) / l)
                lse_ref[hh] = jnp.broadcast_to(m + jnp.log(l), (TM, LANES))
            y_ref[...] = jnp.where(lo, outs[0], outs[1]).astype(BF16)

    return pl.pallas_call(
        body, name="na_fwd", grid=(npair, nb),
        in_specs=[pl.BlockSpec((TM, LANES), lambda p, b: (b, p)),
                  pl.BlockSpec((T, LANES), lambda p, b: (0, npair + p)),
                  pl.BlockSpec((T, LANES), lambda p, b: (0, 2 * npair + p)),
                  pl.BlockSpec((2, 1, TM, KW), lambda p, b: (p, _na_variant(b, n_lat_blk), 0, 0))],
        out_specs=[pl.BlockSpec((TM, LANES), lambda p, b: (b, p)),
                   pl.BlockSpec((2, TM, LANES), lambda p, b: (p, b, 0))],
        out_shape=[jax.ShapeDtypeStruct((T, NA_HEADS * HEAD_DIM), BF16),
                   jax.ShapeDtypeStruct((NA_HEADS, T, LANES), F32)],
        compiler_params=_cp(("parallel", "parallel"), VMEM_BIG),
    )(naqkv, naqkv, naqkv, bias)


def na_bwd_call(naqkv, bias, ya, dya, lse, n_lat):
    T = naqkv.shape[0]
    nb = T // TM
    n_lat_blk = n_lat // TM
    rows = n_lat // GRID_W
    n_ctx = T - n_lat
    KW = NA_KROWS * GRID_W
    npair = NA_HEADS // 2

    def body(q_ref, k_ref, v_ref, b_ref, y_ref, dy_ref, lse_ref, dq_ref, dk_ref, dv_ref, db_ref, dk_sc, dv_sc):
        blk = pl.program_id(1)
        half = _lane_half((TM, LANES))
        lo = half == 0
        q = q_ref[...].astype(F32) * SCALE
        kctx = k_ref[pl.ds(n_lat, n_ctx), :]
        vctx = v_ref[pl.ds(n_lat, n_ctx), :]
        dyv = dy_ref[...].astype(F32)
        yv = y_ref[...].astype(F32)

        @pl.when(blk == 0)
        def _():
            dk_sc[...] = jnp.zeros((T, LANES), F32)
            dv_sc[...] = jnp.zeros((T, LANES), F32)

        @pl.when(blk < n_lat_blk)
        def _():
            off = _na_kstart(blk, rows)
            kwin = k_ref[pl.ds(off, KW), :]
            vwin = v_ref[pl.ds(off, KW), :]
            first = (blk == 0) | (blk == 1) | (blk == n_lat_blk - 1)
            dqs = []
            for hh in range(2):
                sel = half == hh
                qm = jnp.where(sel, q, 0.0).astype(BF16)
                dom = jnp.where(sel, dyv, 0.0)
                delta = jnp.sum(dom * yv, axis=-1, keepdims=True)
                dom = dom.astype(BF16)
                lse_v = lse_ref[hh][:, 0:1]
                pw = jnp.exp(_dot_nt(qm, kwin) + b_ref[hh, 0] - lse_v)
                pc = jnp.exp(_dot_nt(qm, kctx) - lse_v)
                dsw = pw * (_dot_nt(dom, vwin) - delta)
                dsc = pc * (_dot_nt(dom, vctx) - delta)

                @pl.when(first)
                def _():
                    db_ref[hh, 0] = dsw

                @pl.when(jnp.logical_not(first))
                def _():
                    db_ref[hh, 0] += dsw

                dsw = dsw.astype(BF16)
                dsc = dsc.astype(BF16)
                dv_sc[pl.ds(off, KW), :] += _dot_tn(pw.astype(BF16), dom)
                dv_sc[pl.ds(n_lat, n_ctx), :] += _dot_tn(pc.astype(BF16), dom)
                dk_sc[pl.ds(off, KW), :] += _dot_tn(dsw, qm)
                dk_sc[pl.ds(n_lat, n_ctx), :] += _dot_tn(dsc, qm)
                dqs.append((_dot(dsw, kwin) + _dot(dsc, kctx)) * SCALE)
            dq_ref[...] = jnp.where(lo, dqs[0], dqs[1]).astype(BF16)

        @pl.when(blk >= n_lat_blk)
        def _():
            dqs = []
            for hh in range(2):
                sel = half == hh
                qm = jnp.where(sel, q, 0.0).astype(BF16)
                dom = jnp.where(sel, dyv, 0.0)
                delta = jnp.sum(dom * yv, axis=-1, keepdims=True)
                dom = dom.astype(BF16)
                pc = jnp.exp(_dot_nt(qm, kctx) - lse_ref[hh][:, 0:1])
                dsc = (pc * (_dot_nt(dom, vctx) - delta)).astype(BF16)
                dv_sc[pl.ds(n_lat, n_ctx), :] += _dot_tn(pc.astype(BF16), dom)
                dk_sc[pl.ds(n_lat, n_ctx), :] += _dot_tn(dsc, qm)
                dqs.append(_dot(dsc, kctx) * SCALE)
            dq_ref[...] = jnp.where(lo, dqs[0], dqs[1]).astype(BF16)

        @pl.when(blk == nb - 1)
        def _():
            dk_ref[...] = dk_sc[...].astype(BF16)
            dv_ref[...] = dv_sc[...].astype(BF16)

    blkspec = pl.BlockSpec((TM, LANES), lambda p, b: (b, p))
    col = pl.BlockSpec((T, LANES), lambda p, b: (0, p))
    bspec = pl.BlockSpec((2, 1, TM, KW), lambda p, b: (p, _na_variant(b, n_lat_blk), 0, 0))
    dq, dk, dv, db = pl.pallas_call(
        body, name="na_bwd", grid=(npair, nb),
        in_specs=[blkspec,
                  pl.BlockSpec((T, LANES), lambda p, b: (0, npair + p)),
                  pl.BlockSpec((T, LANES), lambda p, b: (0, 2 * npair + p)),
                  bspec, blkspec, blkspec,
                  pl.BlockSpec((2, TM, LANES), lambda p, b: (p, b, 0))],
        out_specs=[blkspec, col, col, bspec],
        out_shape=[jax.ShapeDtypeStruct((T, NA_HEADS * HEAD_DIM), BF16)] * 3
        + [jax.ShapeDtypeStruct((NA_HEADS, 3, TM, KW), F32)],
        scratch_shapes=[pltpu.VMEM((T, LANES), F32), pltpu.VMEM((T, LANES), F32)],
        compiler_params=_cp(("arbitrary", "arbitrary"), VMEM_BIG),
    )(naqkv, naqkv, naqkv, bias, ya, dya, lse)
    return jnp.concatenate([dq, dk, dv], axis=1), db


def rope_tables(n_lat, n_ctx):
    t = jnp.arange(n_lat)
    row = (t // GRID_W).astype(F32)
    col = (t % GRID_W).astype(F32)
    half = HEAD_DIM // 2
    inv = ROPE_THETA ** (-jnp.arange(0, half, 2, dtype=F32) / half)
    ang = jnp.concatenate([row[:, None] * inv, col[:, None] * inv], axis=-1)
    cos = jnp.repeat(jnp.cos(ang), 2, axis=-1)
    sin = jnp.repeat(jnp.sin(ang), 2, axis=-1) * jnp.tile(jnp.array([-1.0, 1.0], F32), half)
    cos = jnp.concatenate([cos, jnp.ones((n_ctx, HEAD_DIM), F32)], axis=0)
    sin = jnp.concatenate([sin, jnp.zeros((n_ctx, HEAD_DIM), F32)], axis=0)
    return jnp.tile(cos, (1, 2)), jnp.tile(sin, (1, 2))


def _row(v):
    return v.reshape(1, -1)


def device_fwd_bwd(xs, target, modv, w, small, n_lat):
    T, D = xs.shape
    n_ctx = T - n_lat
    L = modv.shape[0]
    nlt = n_lat // TM
    rows = n_lat // GRID_W
    cos_t, sin_t = rope_tables(n_lat, n_ctx)
    qg = [jnp.tile(_row(small["q_gain"][l]), (1, GQA_Q_HEADS)) for l in range(L)]
    kg = [jnp.tile(_row(small["k_gain"][l]), (1, GQA_KV_HEADS)) for l in range(L)]
    bias = [na_bias_table(small["na_rpb"][l], rows) for l in range(L)]

    saved = []
    x = xs
    h = norm_mod_call(x, _row(small["norm1"][0]), modv[0], nlt, 0)
    for l in range(L):
        naqkv, gqkv, gates = qkv_call(h, w["w_in"][l])
        ya, lse_a = na_fwd_call(naqkv, bias[l], n_lat)
        qp, kp = gqa_prep_call(gqkv, qg[l], kg[l], cos_t, sin_t)
        yb, lse_b = gqa_fwd_call(qp, kp, gqkv, n_lat, gqa_scores_small(small["q_gain"][l], small["k_gain"][l]))
        merged, pa, pb = merge_call(ya, yb, gates, w["w_pa"][l], w["w_pb"][l])
        x1, osave, h2 = wo_call(merged, w["w_o"][l], x, modv[l], _row(small["norm2"][l]), nlt)
        act, a, u = ffn_in_call(h2, w["w_ffn_in"][l])
        if l + 1 < L:
            x2, fsave, hn = ffn_out_call(act, w["w_ffn_out"][l], x1, modv[l], nlt,
                                         _row(small["norm1"][l + 1]), modv[l + 1])
        else:
            x2, fsave = ffn_out_call(act, w["w_ffn_out"][l], x1, modv[l], nlt)
            hn = None
        saved.append(dict(x=x, h=h, naqkv=naqkv, gqkv=gqkv, gates=gates, ya=ya, lse_a=lse_a, qp=qp, kp=kp,
                          yb=yb, lse_b=lse_b, merged=merged, pa=pa, pb=pb, x1=x1, osave=osave, h2=h2,
                          act=act, a=a, u=u, fsave=fsave))
        x, h = x2, hn

    dx, d_final, loss_part = loss_call(x, _row(small["final_norm"]), target, nlt)

    gw = {k: [None] * L for k in ("w_in", "w_pa", "w_pb", "w_o", "w_ffn_in", "w_ffn_out")}
    gs = {k: [None] * L for k in ("norm1", "norm2", "na_rpb", "q_gain", "k_gain")}
    dmod = [None] * L
    fold = lambda s: s.sum(axis=1)
    for l in reversed(range(L)):
        s = saved[l]
        da, du, df, s_g2 = dact_call(dx, modv[l], s["fsave"], w["w_ffn_out"][l], s["a"], s["u"], nlt)
        gw["w_ffn_out"][l] = wgrad_call(s["act"], df, "wgrad_ffn_out")
        dx1, s_sh2, s_sc2, s_n2 = dh2_call(da, du, w["w_ffn_in"][l], s["x1"], modv[l], _row(small["norm2"][l]), dx, nlt)
        gw["w_ffn_in"][l] = jnp.concatenate(
            [wgrad_call(s["h2"], da, "wgrad_ffn_in_a"), wgrad_call(s["h2"], du, "wgrad_ffn_in_u")], axis=1)
        dpa, dpb, dgates, do, s_g1 = dmerged_call(dx1, modv[l], s["osave"], w["w_o"][l], s["pa"], s["pb"], s["gates"], nlt)
        gw["w_o"][l] = wgrad_call(s["merged"], do, "wgrad_o")
        dya, dyb = dy_call(dpa, dpb, w["w_pa"][l], w["w_pb"][l])
        gw["w_pa"][l] = wgrad_call(s["ya"], dpa, "wgrad_pa")
        gw["w_pb"][l] = wgrad_call(s["yb"], dpb, "wgrad_pb")
        dna, dbias = na_bwd_call(s["naqkv"], bias[l], s["ya"], dya, s["lse_a"], n_lat)
        gs["na_rpb"][l] = na_bias_grad(dbias, rows)
        dqp, dkp, dv = gqa_bwd_call(s["qp"], s["kp"], s["gqkv"], s["yb"], dyb, s["lse_b"], n_lat)
        dg, s_qg, s_kg = gqa_prep_bwd_call(s["gqkv"], qg[l], kg[l], cos_t, sin_t, dqp, dkp, dv)
        gs["q_gain"][l] = s_qg.reshape(8 * GQA_Q_HEADS, HEAD_DIM).sum(axis=0)
        gs["k_gain"][l] = s_kg.reshape(8 * GQA_KV_HEADS, HEAD_DIM).sum(axis=0)
        dx, s_sh1, s_sc1, s_n1 = dh_call(dna, dg, dgates, w["w_in"][l], s["x"], modv[l], _row(small["norm1"][l]), dx1, nlt)
        gw["w_in"][l] = jnp.concatenate(
            [wgrad_call(s["h"], dna, "wgrad_in_na"), wgrad_call(s["h"], dg, "wgrad_in_g"),
             wgrad_call(s["h"], dgates, "wgrad_in_gate")], axis=1)
        gs["norm1"][l] = fold(s_n1).sum(axis=0)
        gs["norm2"][l] = fold(s_n2).sum(axis=0)
        dmod[l] = jnp.stack([fold(s_sh1), fold(s_sc1), fold(s_g1), fold(s_sh2), fold(s_sc2), fold(s_g2)], axis=1)

    gw = {k: jnp.stack(v) for k, v in gw.items()}
    gs = {k: jnp.stack(v) for k, v in gs.items()}
    gs["final_norm"] = d_final.sum(axis=(0, 1))
    return loss_part, dx, gw, jnp.stack(dmod), gs


N_DEV = 8
N_CHIP = 4
ANY = pl.BlockSpec(memory_space=pl.ANY)


def _place():
    x, y, c = lax.axis_index("x"), lax.axis_index("y"), lax.axis_index("c")
    chips = [(1 - x, y), (x, 1 - y), (1 - x, 1 - y)]
    return x, y, c, chips


def small_all_gather(v, name):
    m_per, n = v.shape
    assert m_per % 8 == 0

    def body(x_ref, out_ref, send_sems, recv_sems, local_sem):
        x, y, c, chips = _place()
        me, sibling = (x, y, c), (x, y, 1 - c)

        def rows(px, py, pc):
            return out_ref.at[pl.ds(pl.multiple_of((4 * px + 2 * py + pc) * m_per, 8), m_per), :]

        def copy(k, block, to, src=None):
            return pltpu.make_async_remote_copy(
                src_ref=rows(*block) if src is None else src, dst_ref=rows(*block),
                send_sem=send_sems.at[k], recv_sem=recv_sems.at[k], device_id=to, device_id_type=MESH)

        mine = pltpu.make_async_copy(x_ref, rows(*me), local_sem)
        mine.start()
        first = [copy(0, me, sibling, src=x_ref)]
        first += [copy(1 + j, me, (*chip, c), src=x_ref) for j, chip in enumerate(chips)]
        for cp in first:
            cp.start()
        passed = [copy(4 + j, (*chip, c), sibling) for j, chip in enumerate(chips)]
        for j, chip in enumerate(chips):
            copy(1 + j, (*chip, c), me).wait_recv()
            passed[j].start()
        copy(0, sibling, me).wait_recv()
        for j, chip in enumerate(chips):
            copy(4 + j, (*chip, 1 - c), me).wait_recv()
        for cp in first + passed:
            cp.wait_send()
        mine.wait()

    out = pl.pallas_call(
        body, name=name,
        out_shape=jax.ShapeDtypeStruct((N_DEV * m_per, n), v.dtype),
        in_specs=[pl.BlockSpec(memory_space=pltpu.VMEM)],
        out_specs=pl.BlockSpec(memory_space=pltpu.VMEM),
        scratch_shapes=[pltpu.SemaphoreType.DMA((7,)), pltpu.SemaphoreType.DMA((7,)), pltpu.SemaphoreType.DMA],
    )(v)
    return out.reshape(N_DEV, m_per, n)


def weights_all_gather(packed):
    _, rh, ncol = packed.shape

    def body(in_ref, out_ref, send_sems, recv_sems):
        x, y, c, chips = _place()
        sibling = (x, y, 1 - c)

        def blk(px, py, half):
            return out_ref.at[2 * px + py, half]

        def copy(k, block, to, src=None):
            return pltpu.make_async_remote_copy(
                src_ref=blk(*block) if src is None else src, dst_ref=blk(*block),
                send_sem=send_sems.at[k], recv_sem=recv_sems.at[k], device_id=to, device_id_type=MESH)

        first = [copy(j, (x, y, c), (*chip, c), src=in_ref.at[c]) for j, chip in enumerate(chips)]
        for cp in first:
            cp.start()
        passed = [copy(3 + j, (*chip, c), sibling) for j, chip in enumerate(chips)]
        for j, chip in enumerate(chips):
            copy(j, (*chip, c), (x, y, c)).wait_recv()
            passed[j].start()
        for j, chip in enumerate(chips):
            copy(3 + j, (*chip, 1 - c), (x, y, c)).wait_recv()
        for cp in first + passed:
            cp.wait_send()

    out = pl.pallas_call(
        body, name="weights_all_gather",
        out_shape=jax.ShapeDtypeStruct((N_CHIP, 2, rh, ncol), packed.dtype),
        in_specs=[ANY], out_specs=ANY,
        scratch_shapes=[pltpu.SemaphoreType.DMA((6,)), pltpu.SemaphoreType.DMA((6,))],
    )(packed)
    chip = 2 * lax.axis_index("x") + lax.axis_index("y")
    return lax.dynamic_update_slice(out, packed[None], (chip, 0, 0, 0))


def grads_swap_halves(gp):
    _, _, rh, ncol = gp.shape

    def body(gp_ref, r1_ref, send_sem, recv_sem):
        x, y, c, _ = _place()
        cp = pltpu.make_async_remote_copy(src_ref=gp_ref.at[1 - c], dst_ref=r1_ref, send_sem=send_sem,
                                          recv_sem=recv_sem, device_id=(x, y, 1 - c), device_id_type=MESH)
        cp.start()
        cp.wait()

    return pl.pallas_call(
        body, name="grads_swap_halves", out_shape=jax.ShapeDtypeStruct((N_CHIP, rh, ncol), gp.dtype),
        in_specs=[ANY], out_specs=ANY,
        scratch_shapes=[pltpu.SemaphoreType.DMA, pltpu.SemaphoreType.DMA],
    )(gp)


def grads_to_owner(p1):
    _, rh, ncol = p1.shape

    def body(p1_ref, r2_ref, send_sems, recv_sems):
        x, y, c, chips = _place()
        cps = [pltpu.make_async_remote_copy(src_ref=p1_ref.at[2 * px + py], dst_ref=r2_ref.at[j],
                                            send_sem=send_sems.at[j], recv_sem=recv_sems.at[j],
                                            device_id=(px, py, c), device_id_type=MESH)
               for j, (px, py) in enumerate(chips)]
        for cp in cps:
            cp.start()
        for cp in cps:
            cp.wait()

    return pl.pallas_call(
        body, name="grads_to_owner", out_shape=jax.ShapeDtypeStruct((3, rh, ncol), p1.dtype),
        in_specs=[ANY], out_specs=ANY,
        scratch_shapes=[pltpu.SemaphoreType.DMA((3,)), pltpu.SemaphoreType.DMA((3,))],
    )(p1)


def grads_join_halves(gh):
    rh, ncol = gh.shape

    def body(gh_ref, out_ref, send_sem, recv_sem):
        x, y, c, _ = _place()
        cp = pltpu.make_async_remote_copy(src_ref=gh_ref, dst_ref=out_ref, send_sem=send_sem,
                                          recv_sem=recv_sem, device_id=(x, y, 1 - c), device_id_type=MESH)
        cp.start()
        cp.wait()

    other = pl.pallas_call(
        body, name="grads_join_halves", out_shape=jax.ShapeDtypeStruct((rh, ncol), gh.dtype),
        in_specs=[ANY], out_specs=ANY,
        scratch_shapes=[pltpu.SemaphoreType.DMA, pltpu.SemaphoreType.DMA],
    )(gh)
    c0 = lax.axis_index("c") == 0
    return jnp.stack([jnp.where(c0, gh, other), jnp.where(c0, other, gh)])


PACK_C = 1024


def add_halves_call(gp, r1, c):
    _, n, rh, ncol = gp.shape
    tr = _pick_tile(rh, 512, 16)

    def body(c_ref, a_ref, b_ref, o_ref):
        o_ref[0] = (a_ref[0, 0].astype(F32) + b_ref[0].astype(F32)).astype(BF16)

    return pl.pallas_call(
        body, name="add_halves",
        grid_spec=pltpu.PrefetchScalarGridSpec(
            num_scalar_prefetch=1, grid=(n, rh // tr),
            in_specs=[pl.BlockSpec((1, 1, tr, ncol), lambda k, i, c_ref: (c_ref[0], k, i, 0)),
                      pl.BlockSpec((1, tr, ncol), lambda k, i, c_ref: (k, i, 0))],
            out_specs=pl.BlockSpec((1, tr, ncol), lambda k, i, c_ref: (k, i, 0))),
        out_shape=jax.ShapeDtypeStruct((n, rh, ncol), BF16),
        compiler_params=_cp(("parallel", "parallel")),
    )(jnp.reshape(c, (1,)).astype(jnp.int32), gp, r1)


def sum_owner_call(p1, r2, k):
    _, rh, ncol = p1.shape
    tr = _pick_tile(rh, 512, 16)

    def body(k_ref, a_ref, b0_ref, b1_ref, b2_ref, o_ref):
        o_ref[...] = (((a_ref[0].astype(F32) + b0_ref[0].astype(F32)) + b1_ref[0].astype(F32))
                      + b2_ref[0].astype(F32))

    other = lambda j: pl.BlockSpec((1, tr, ncol), lambda i, k_ref: (j, i, 0))
    return pl.pallas_call(
        body, name="sum_owner",
        grid_spec=pltpu.PrefetchScalarGridSpec(
            num_scalar_prefetch=1, grid=(rh // tr,),
            in_specs=[pl.BlockSpec((1, tr, ncol), lambda i, k_ref: (k_ref[0], i, 0)), other(0), other(1), other(2)],
            out_specs=pl.BlockSpec((tr, ncol), lambda i, k_ref: (i, 0))),
        out_shape=jax.ShapeDtypeStruct((rh, ncol), F32),
        compiler_params=_cp(("parallel",)),
    )(jnp.reshape(k, (1,)).astype(jnp.int32), p1, r2, r2, r2)


def _silu(x):
    return x * _sigmoid(x)


def mod_matmul_call(craw, w_mod):
    L, D, N = w_mod.shape

    def body(c_ref, w_ref, o_ref):
        o_ref[0] = _dot(_silu(c_ref[...]).astype(BF16), w_ref[0].astype(BF16))

    return pl.pallas_call(
        body, name="mod_matmul", grid=(L,),
        in_specs=[pl.BlockSpec((16, D), lambda l: (0, 0)), pl.BlockSpec((1, D, N), lambda l: (l, 0, 0))],
        out_specs=pl.BlockSpec((1, 16, N), lambda l: (l, 0, 0)),
        out_shape=jax.ShapeDtypeStruct((L, 16, N), F32),
        compiler_params=_cp(("parallel",), VMEM_BIG),
    )(craw, w_mod)


def mod_bwd_call(craw, dmod16, w_mod):
    L, D, N = w_mod.shape

    def body(c_ref, d_ref, w_ref, gw_ref, ds_ref):
        l = pl.program_id(0)
        d = d_ref[0].astype(BF16)
        gw_ref[0] = _dot_tn(_silu(c_ref[...]).astype(BF16), d)
        part = _dot_nt(d, w_ref[0].astype(BF16))

        @pl.when(l == 0)
        def _():
            ds_ref[...] = part

        @pl.when(l > 0)
        def _():
            ds_ref[...] += part

    return pl.pallas_call(
        body, name="mod_bwd", grid=(L,),
        in_specs=[pl.BlockSpec((16, D), lambda l: (0, 0)), pl.BlockSpec((1, 16, N), lambda l: (l, 0, 0)),
                  pl.BlockSpec((1, D, N), lambda l: (l, 0, 0))],
        out_specs=[pl.BlockSpec((1, D, N), lambda l: (l, 0, 0)), pl.BlockSpec((16, D), lambda l: (0, 0))],
        out_shape=[jax.ShapeDtypeStruct((L, D, N), F32), jax.ShapeDtypeStruct((16, D), F32)],
        compiler_params=_cp(("arbitrary",), VMEM_BIG),
    )(craw, dmod16, w_mod)


def _adamw(w, g, m, v):
    m2 = ADAM_B1 * m + (1.0 - ADAM_B1) * g
    v2 = ADAM_B2 * v + (1.0 - ADAM_B2) * (g * g)
    m_hat = m2 / (1.0 - ADAM_B1 ** ADAM_STEP)
    v_hat = v2 / (1.0 - ADAM_B2 ** ADAM_STEP)
    delta = -ADAM_LR * (m_hat / (jnp.sqrt(v_hat) + ADAM_EPS) + ADAM_WD * w)
    return delta, m2, v2


def adamw_call(w, g, m, v):
    shape = w.shape
    ncol = shape[-1]
    r = math.prod(shape[:-1])
    tr = _pick_tile(r, 512, 8)
    as2d = lambda t: t.reshape(r, ncol)

    def body(w_ref, g_ref, m_ref, v_ref, d_ref, m2_ref, v2_ref):
        d_ref[...], m2_ref[...], v2_ref[...] = _adamw(w_ref[...], g_ref[...], m_ref[...], v_ref[...])

    spec = pl.BlockSpec((tr, ncol), lambda i: (i, 0))
    outs = pl.pallas_call(
        body, name="adamw", grid=(r // tr,), in_specs=[spec] * 4, out_specs=[spec] * 3,
        out_shape=[jax.ShapeDtypeStruct((r, ncol), F32)] * 3,
        compiler_params=_cp(("parallel",)),
    )(as2d(w), as2d(g), as2d(m), as2d(v))
    return [o.reshape(shape) for o in outs]


def adamw_small_call(parts, w, m, v, silu_bwd=False):
    P, R, C = parts.shape

    def body(p_ref, w_ref, m_ref, v_ref, g_ref, d_ref, m2_ref, v2_ref):
        g = p_ref[0]
        for i in range(1, P):
            g = g + p_ref[i]
        wv = w_ref[...]
        if silu_bwd:
            sg = _sigmoid(wv)
            g = g * (sg * (1.0 + wv * (1.0 - sg)))
        g_ref[...] = g
        d_ref[...], m2_ref[...], v2_ref[...] = _adamw(wv, g, m_ref[...], v_ref[...])

    return pl.pallas_call(
        body, name="adamw_small", out_shape=[jax.ShapeDtypeStruct((R, C), F32)] * 4,
    )(parts, w, m, v)


def sum_parts_call(parts):
    P, R, C = parts.shape

    def body(p_ref, o_ref):
        g = p_ref[0]
        for i in range(1, P):
            g = g + p_ref[i]
        o_ref[...] = g

    return pl.pallas_call(body, name="sum_parts", out_shape=jax.ShapeDtypeStruct((R, C), F32))(parts)


BIG = (("w_in", "col"), ("w_pa", "col"), ("w_pb", "col"), ("w_o", "row"), ("w_ffn_in", "col"), ("w_ffn_out", "row"))
SMALL = ("norm1", "norm2", "final_norm", "na_rpb", "q_gain", "k_gain")


def _pack_shards(shards):
    parts = [shards[n].reshape(2, -1, PACK_C) for n, _ in BIG]
    return jnp.concatenate(parts, axis=1)


def _unpack_shards(packed, shapes):
    out, r0 = {}, 0
    lead = packed.shape[:-3]
    for n, _ in BIG:
        L, K, N = shapes[n]
        rw = (L // 2) * K * N // PACK_C
        out[n] = packed[..., r0:r0 + rw, :].reshape(lead + (L, K, N))
        r0 += rw
    return out


def _whole_from_chips(g, kind):
    _, L, K, N = g.shape
    if kind == "col":
        return g.transpose(1, 2, 0, 3).reshape(L, K, N_CHIP * N)
    return g.transpose(1, 0, 2, 3).reshape(L, N_CHIP * K, N)


def _chips_from_whole(g, kind):
    L, K, N = g.shape
    if kind == "col":
        return g.reshape(L, K, N_CHIP, N // N_CHIP).transpose(2, 0, 1, 3)
    return g.reshape(L, N_CHIP, K // N_CHIP, N).transpose(1, 0, 2, 3)


def _pad_rows(v, rows):
    return jnp.pad(v.reshape(-1), (0, rows * PACK_C - v.size)).reshape(rows, PACK_C)


def kernel(x, c, ctx, c_ctx, w_mod, b_mod, norm1, w_in, na_rpb, q_gain, k_gain, w_pa, w_pb, w_o, norm2, w_ffn_in, w_ffn_out, final_norm, loss_target, m_c_ctx, m_w_mod, m_b_mod, m_norm1, m_w_in, m_na_rpb, m_q_gain, m_k_gain, m_w_pa, m_w_pb, m_w_o, m_norm2, m_w_ffn_in, m_w_ffn_out, m_final_norm, v_c_ctx, v_w_mod, v_b_mod, v_norm1, v_w_in, v_na_rpb, v_q_gain, v_k_gain, v_w_pa, v_w_pb, v_w_o, v_norm2, v_w_ffn_in, v_w_ffn_out, v_final_norm):
    W = dict(c_ctx=c_ctx, w_mod=w_mod, b_mod=b_mod, norm1=norm1, w_in=w_in, na_rpb=na_rpb, q_gain=q_gain, k_gain=k_gain,
             w_pa=w_pa, w_pb=w_pb, w_o=w_o, norm2=norm2, w_ffn_in=w_ffn_in, w_ffn_out=w_ffn_out, final_norm=final_norm)
    M = dict(c_ctx=m_c_ctx, w_mod=m_w_mod, b_mod=m_b_mod, norm1=m_norm1, w_in=m_w_in, na_rpb=m_na_rpb, q_gain=m_q_gain,
             k_gain=m_k_gain, w_pa=m_w_pa, w_pb=m_w_pb, w_o=m_w_o, norm2=m_norm2, w_ffn_in=m_w_ffn_in,
             w_ffn_out=m_w_ffn_out, final_norm=m_final_norm)
    V = dict(c_ctx=v_c_ctx, w_mod=v_w_mod, b_mod=v_b_mod, norm1=v_norm1, w_in=v_w_in, na_rpb=v_na_rpb, q_gain=v_q_gain,
             k_gain=v_k_gain, w_pa=v_w_pa, w_pb=v_w_pb, w_o=v_w_o, norm2=v_norm2, w_ffn_in=v_w_ffn_in,
             w_ffn_out=v_w_ffn_out, final_norm=v_final_norm)
    order = ["c_ctx", "w_mod", "b_mod", "norm1", "w_in", "na_rpb", "q_gain", "k_gain", "w_pa", "w_pb", "w_o", "norm2",
             "w_ffn_in", "w_ffn_out", "final_norm"]
    L, D = norm1.shape
    n_lat = x.shape[1]
    ax, ay, ac = lax.axis_index("x"), lax.axis_index("y"), lax.axis_index("c")
    chip = 2 * ax + ay
    dev = 2 * chip + ac

    shard_shapes = {n: W[n].shape for n, _ in BIG}
    packed = _pack_shards({n: W[n].astype(BF16) for n, _ in BIG})
    gathered = _unpack_shards(weights_all_gather(packed), shard_shapes)
    wfull = {n: _whole_from_chips(gathered[n], kind) for n, kind in BIG}

    c_all = small_all_gather(jnp.pad(c, ((0, 7), (0, 0))), "gather_c")[:, 0, :]
    craw = jnp.pad(c_all, ((0, 8), (0, 0))) + jnp.pad(c_ctx[None, :], ((8, 7), (0, 0)))
    ncol_mod = w_mod.shape[2]
    mod_loc = mod_matmul_call(craw, w_mod)
    mod_all = small_all_gather(mod_loc.reshape(L * 16, ncol_mod), "gather_mod")
    mod_all = mod_all[0::2].reshape(N_CHIP, L, 16, ncol_mod).transpose(1, 2, 0, 3).reshape(L, 16, 6 * D)
    mod_all = mod_all + b_mod[:, None, :]
    mod_mine = lax.dynamic_index_in_dim(mod_all, dev, axis=1, keepdims=False)
    modv = jnp.stack([mod_mine, mod_all[:, 8]], axis=1).reshape(L, 2, 6, D)

    xs = jnp.concatenate([x[0], ctx[0]], axis=0)
    small = dict(norm1=norm1, norm2=norm2, na_rpb=na_rpb, q_gain=q_gain, k_gain=k_gain, final_norm=final_norm)
    loss_part, dx, gw, dmod, gs = device_fwd_bwd(xs, loss_target[0], modv, wfull, small, n_lat)
    loss = lax.psum(jnp.sum(loss_part), ("x", "y", "c"))
    grad_x = dx[:n_lat][None]

    gp = jnp.concatenate(
        [_chips_from_whole(gw[n], kind).reshape(N_CHIP, 2, -1, PACK_C).transpose(1, 0, 2, 3) for n, kind in BIG], axis=2)
    r1 = grads_swap_halves(gp)
    p1 = add_halves_call(gp, r1, ac)
    r2 = grads_to_owner(p1)
    gh = sum_owner_call(p1, r2, chip)
    gbig = _unpack_shards(grads_join_halves(gh), shard_shapes)

    n_mod_rows = L * 6 * D // PACK_C
    small_flat = jnp.concatenate([gs[n].reshape(-1) for n in SMALL])
    n_small_rows = -(-small_flat.size // (8 * PACK_C)) * 8
    g1 = jnp.concatenate([dmod[:, 0].reshape(n_mod_rows, PACK_C), dmod[:, 1].reshape(n_mod_rows, PACK_C),
                          _pad_rows(small_flat, n_small_rows)], axis=0)
    g1 = small_all_gather(g1, "gather_small_grads")
    dm_lat = g1[:, :n_mod_rows].reshape(N_DEV, L, 6 * D)
    dm_ctx_parts = g1[:, n_mod_rows:2 * n_mod_rows]
    dm_ctx = sum_parts_call(dm_ctx_parts).reshape(L, 1, 6 * D)
    dmod16 = (jnp.pad(dm_lat.transpose(1, 0, 2), ((0, 0), (0, 8), (0, 0)))
              + jnp.pad(dm_ctx, ((0, 0), (8, 7), (0, 0))))
    dmod16_loc = lax.dynamic_slice_in_dim(dmod16, chip * ncol_mod, ncol_mod, axis=2)
    g_wmod, dsilu_part = mod_bwd_call(craw, dmod16_loc, w_mod)
    ds_all = small_all_gather(dsilu_part[8:16], "gather_dsilu")
    ds_parts = ds_all[0::2, 0:1, :]
    ds_parts = jnp.pad(ds_parts, ((0, 0), (0, 7), (0, 0)))

    out_g, out_d, out_m, out_v = {}, {}, {}, {}
    for n, _ in BIG:
        out_g[n] = gbig[n]
        out_d[n], out_m[n], out_v[n] = adamw_call(W[n], gbig[n], M[n], V[n])
    out_g["w_mod"] = g_wmod
    out_d["w_mod"], out_m["w_mod"], out_v["w_mod"] = adamw_call(w_mod, g_wmod, m_w_mod, v_w_mod)

    pack_small = lambda t: _pad_rows(jnp.concatenate([t[n].reshape(-1) for n in SMALL]), n_small_rows)
    res = adamw_small_call(g1[:, 2 * n_mod_rows:], pack_small(W), pack_small(M), pack_small(V))
    off = 0
    for n in SMALL:
        sz = W[n].size
        for dst, r in zip((out_g, out_d, out_m, out_v), res):
            dst[n] = r.reshape(-1)[off:off + sz].reshape(W[n].shape)
        off += sz
    bparts = jnp.concatenate([g1[:, :n_mod_rows], dm_ctx_parts], axis=0)
    as_rows = lambda t: t.reshape(n_mod_rows, PACK_C)
    res = adamw_small_call(bparts, as_rows(b_mod), as_rows(m_b_mod), as_rows(v_b_mod))
    for dst, r in zip((out_g, out_d, out_m, out_v), res):
        dst["b_mod"] = r.reshape(b_mod.shape)
    row8 = lambda t: jnp.pad(t[None, :], ((0, 7), (0, 0)))
    res = adamw_small_call(ds_parts, row8(c_ctx), row8(m_c_ctx), row8(v_c_ctx), silu_bwd=True)
    for dst, r in zip((out_g, out_d, out_m, out_v), res):
        dst["c_ctx"] = r[0]

    return (loss, grad_x, *[out_g[n] for n in order], *[out_d[n] for n in order],
            *[out_m[n] for n in order], *[out_v[n] for n in order])
```

```python
import functools
import math

import numpy as np
import jax
import jax.numpy as jnp
from jax import lax
from jax.experimental import pallas as pl
from jax.experimental.pallas import tpu as pltpu

F32 = jnp.float32
BF16 = jnp.bfloat16
MESH = pl.DeviceIdType.MESH

HEAD_DIM = 64
NA_HEADS = 8
GQA_Q_HEADS = 8
GQA_KV_HEADS = 2
GRID_W = 64
NA_WIN_H = 8
NA_WIN_W = 16
ROPE_THETA = 10000.0
EPS = 1e-6
SCALE = HEAD_DIM ** -0.5
NEG = -1e30

ADAM_LR = 0.001
ADAM_B1 = 0.9
ADAM_B2 = 0.999
ADAM_EPS = 1e-08
ADAM_WD = 0.01
ADAM_STEP = 10

TM = 256
NA_ROWS = 4
NA_KROWS = 12
LANES = 128
VMEM_BIG = 56 * 1024 * 1024


def _cp(sem, vmem=None):
    return pltpu.CompilerParams(dimension_semantics=sem, vmem_limit_bytes=vmem)


def _dot(a, b):
    return jnp.dot(a, b, preferred_element_type=F32)


def _dot_nt(a, b):
    return lax.dot_general(a, b, (((1,), (1,)), ((), ())), preferred_element_type=F32)


def _dot_tn(a, b):
    return lax.dot_general(a, b, (((0,), (0,)), ((), ())), preferred_element_type=F32)


def _colsum8(v):
    tm, d = v.shape
    return v.reshape(tm // 8, 8, d).sum(axis=0)


def _acc(ref, val, first):
    @pl.when(first)
    def _():
        ref[0] = val

    @pl.when(jnp.logical_not(first))
    def _():
        ref[0] += val


def _rms_r(x):
    return lax.rsqrt(jnp.mean(x * x, axis=-1, keepdims=True) + EPS)


def _norm_mod(x, nw, sh, sc):
    return (x * _rms_r(x) * nw) * (1.0 + sc) + sh


def _norm_mod_bwd(dh, x, nw, sc):
    r = _rms_r(x)
    xn = x * r
    dxn = dh * (nw * (1.0 + sc))
    dx = r * (dxn - xn * jnp.mean(dxn * xn, axis=-1, keepdims=True))
    return dx, dh, dh * (xn * nw), dh * ((1.0 + sc) * xn)


def _sigmoid(x):
    return 1.0 / (1.0 + jnp.exp(-x))


def _group_of(i, n_lat_tiles):
    return jnp.where(i >= n_lat_tiles, 1, 0)


def _lane_half(shape):
    return (lax.broadcasted_iota(jnp.int32, shape, len(shape) - 1) % LANES) // HEAD_DIM


def _lane_lt64(shape):
    return _lane_half(shape) == 0


def _roll64(x):
    return pltpu.roll(x, HEAD_DIM, x.ndim - 1)


def norm_mod_call(x, nw, modv, n_lat_tiles, which):
    T, D = x.shape

    def body(x_ref, nw_ref, mod_ref, h_ref):
        sh = mod_ref[0, 3 * which:3 * which + 1, :]
        sc = mod_ref[0, 3 * which + 1:3 * which + 2, :]
        h_ref[...] = _norm_mod(x_ref[...], nw_ref[...], sh, sc).astype(BF16)

    return pl.pallas_call(
        body, name="norm_mod", grid=(T // TM,),
        in_specs=[pl.BlockSpec((TM, D), lambda i: (i, 0)),
                  pl.BlockSpec((1, D), lambda i: (0, 0)),
                  pl.BlockSpec((1, 6, D), lambda i: (_group_of(i, n_lat_tiles), 0, 0))],
        out_specs=pl.BlockSpec((TM, D), lambda i: (i, 0)),
        out_shape=jax.ShapeDtypeStruct((T, D), BF16),
        compiler_params=_cp(("parallel",)),
    )(x, nw, modv)


def qkv_call(h, w_in):
    T, D = h.shape
    N = w_in.shape[1]
    n_na = 3 * NA_HEADS * HEAD_DIM
    n_g = (GQA_Q_HEADS + 2 * GQA_KV_HEADS) * HEAD_DIM
    n_gate = N - n_na - n_g

    def body(h_ref, w_ref, na_ref, g_ref, gate_ref):
        acc = _dot(h_ref[...], w_ref[...])
        na_ref[...] = acc[:, :n_na].astype(BF16)
        g_ref[...] = acc[:, n_na:n_na + n_g].astype(BF16)
        gate_ref[...] = acc[:, n_na + n_g:].astype(BF16)

    return pl.pallas_call(
        body, name="qkv", grid=(T // TM,),
        in_specs=[pl.BlockSpec((TM, D), lambda i: (i, 0)),
                  pl.BlockSpec((D, N), lambda i: (0, 0))],
        out_specs=[pl.BlockSpec((TM, n_na), lambda i: (i, 0)),
                   pl.BlockSpec((TM, n_g), lambda i: (i, 0)),
                   pl.BlockSpec((TM, n_gate), lambda i: (i, 0))],
        out_shape=[jax.ShapeDtypeStruct((T, n_na), BF16),
                   jax.ShapeDtypeStruct((T, n_g), BF16),
                   jax.ShapeDtypeStruct((T, n_gate), BF16)],
        compiler_params=_cp(("parallel",), VMEM_BIG),
    )(h, w_in)


def merge_call(ya, yb, gates, w_pa, w_pb):
    T, Ka = ya.shape
    D = w_pa.shape[1]

    def body(ya_ref, yb_ref, ga_ref, gb_ref, wa_ref, wb_ref, m_ref, pa_ref, pb_ref):
        pa = _dot(ya_ref[...], wa_ref[...])
        pb = _dot(yb_ref[...], wb_ref[...])
        m = _sigmoid(ga_ref[...].astype(F32)) * pa + _sigmoid(gb_ref[...].astype(F32)) * pb
        m_ref[...] = m.astype(BF16)
        pa_ref[...] = pa.astype(BF16)
        pb_ref[...] = pb.astype(BF16)

    row = lambda w: pl.BlockSpec((TM, w), lambda i: (i, 0))
    return pl.pallas_call(
        body, name="merge", grid=(T // TM,),
        in_specs=[row(Ka), row(Ka),
                  pl.BlockSpec((TM, D), lambda i: (i, 0)), pl.BlockSpec((TM, D), lambda i: (i, 1)),
                  pl.BlockSpec((Ka, D), lambda i: (0, 0)), pl.BlockSpec((Ka, D), lambda i: (0, 0))],
        out_specs=[row(D), row(D), row(D)],
        out_shape=[jax.ShapeDtypeStruct((T, D), BF16)] * 3,
        compiler_params=_cp(("parallel",)),
    )(ya, yb, gates, gates, w_pa, w_pb)


def wo_call(merged, w_o, x, modv, nw2, n_lat_tiles):
    T, D = x.shape

    def body(m_ref, w_ref, x_ref, mod_ref, nw_ref, x1_ref, o_ref, h2_ref):
        o = _dot(m_ref[...], w_ref[...])
        x1 = x_ref[...] + mod_ref[0, 2:3, :] * o
        x1_ref[...] = x1
        o_ref[...] = o.astype(BF16)
        h2_ref[...] = _norm_mod(x1, nw_ref[...], mod_ref[0, 3:4, :], mod_ref[0, 4:5, :]).astype(BF16)

    row = pl.BlockSpec((TM, D), lambda i: (i, 0))
    return pl.pallas_call(
        body, name="wo", grid=(T // TM,),
        in_specs=[row, pl.BlockSpec((D, D), lambda i: (0, 0)), row,
                  pl.BlockSpec((1, 6, D), lambda i: (_group_of(i, n_lat_tiles), 0, 0)),
                  pl.BlockSpec((1, D), lambda i: (0, 0))],
        out_specs=[row, row, row],
        out_shape=[jax.ShapeDtypeStruct((T, D), F32), jax.ShapeDtypeStruct((T, D), BF16),
                   jax.ShapeDtypeStruct((T, D), BF16)],
        compiler_params=_cp(("parallel",)),
    )(merged, w_o, x, modv, nw2)


def ffn_in_call(h2, w_ffn_in, n_col_tiles=2):
    T, D = h2.shape
    F = w_ffn_in.shape[1] // 2
    tn = F // n_col_tiles

    def body(h_ref, wa_ref, wu_ref, act_ref, a_ref, u_ref):
        h = h_ref[...]
        a = _dot(h, wa_ref[...])
        u = _dot(h, wu_ref[...])
        act_ref[...] = (a * _sigmoid(a) * u).astype(BF16)
        a_ref[...] = a.astype(BF16)
        u_ref[...] = u.astype(BF16)

    blk = pl.BlockSpec((TM, tn), lambda j, i: (i, j))
    return pl.pallas_call(
        body, name="ffn_in", grid=(n_col_tiles, T // TM),
        in_specs=[pl.BlockSpec((TM, D), lambda j, i: (i, 0)),
                  pl.BlockSpec((D, tn), lambda j, i: (0, j)),
                  pl.BlockSpec((D, tn), lambda j, i: (0, n_col_tiles + j))],
        out_specs=[blk, blk, blk],
        out_shape=[jax.ShapeDtypeStruct((T, F), BF16)] * 3,
        compiler_params=_cp(("parallel", "parallel"), VMEM_BIG),
    )(h2, w_ffn_in, w_ffn_in)


def ffn_out_call(act, w_ffn_out, x1, modv, n_lat_tiles, next_nw=None, next_modv=None):
    T, D = x1.shape
    F = act.shape[1]
    with_next = next_nw is not None

    def body(*refs):
        if with_next:
            a_ref, w_ref, x_ref, mod_ref, nw_ref, nmod_ref, x2_ref, f_ref, hn_ref = refs
        else:
            a_ref, w_ref, x_ref, mod_ref, x2_ref, f_ref = refs
        f = _dot(a_ref[...], w_ref[...])
        x2 = x_ref[...] + mod_ref[0, 5:6, :] * f
        x2_ref[...] = x2
        f_ref[...] = f.astype(BF16)
        if with_next:
            hn_ref[...] = _norm_mod(x2, nw_ref[...], nmod_ref[0, 0:1, :], nmod_ref[0, 1:2, :]).astype(BF16)

    row = pl.BlockSpec((TM, D), lambda i: (i, 0))
    modspec = pl.BlockSpec((1, 6, D), lambda i: (_group_of(i, n_lat_tiles), 0, 0))
    in_specs = [pl.BlockSpec((TM, F), lambda i: (i, 0)), pl.BlockSpec((F, D), lambda i: (0, 0)), row, modspec]
    args = [act, w_ffn_out, x1, modv]
    out_specs = [row, row]
    out_shape = [jax.ShapeDtypeStruct((T, D), F32), jax.ShapeDtypeStruct((T, D), BF16)]
    if with_next:
        in_specs += [pl.BlockSpec((1, D), lambda i: (0, 0)), modspec]
        args += [next_nw, next_modv]
        out_specs.append(row)
        out_shape.append(jax.ShapeDtypeStruct((T, D), BF16))
    return pl.pallas_call(
        body, name="ffn_out", grid=(T // TM,), in_specs=in_specs, out_specs=out_specs, out_shape=out_shape,
        compiler_params=_cp(("parallel",), VMEM_BIG),
    )(*args)


def loss_call(x, nw, target, n_lat_tiles):
    T, D = x.shape

    def body(x_ref, nw_ref, t_ref, dx_ref, dw_ref, ls_ref):
        i = pl.program_id(0)

        @pl.when(i < n_lat_tiles)
        def _():
            xv = x_ref[...]
            nw_v = nw_ref[...]
            r = _rms_r(xv)
            xn = xv * r
            e = xn * nw_v - t_ref[...]
            dy = e * (1.0 / D)
            dxn = dy * nw_v
            dx_ref[...] = r * (dxn - xn * jnp.mean(dxn * xn, axis=-1, keepdims=True))
            _acc(dw_ref, _colsum8(dy * xn), i == 0)
            _acc(ls_ref, _colsum8(e * e * (0.5 / D)), i == 0)

        @pl.when(i >= n_lat_tiles)
        def _():
            dx_ref[...] = jnp.zeros((TM, D), F32)

    stat = pl.BlockSpec((1, 8, D), lambda i: (0, 0, 0))
    return pl.pallas_call(
        body, name="loss", grid=(T // TM,),
        in_specs=[pl.BlockSpec((TM, D), lambda i: (i, 0)), pl.BlockSpec((1, D), lambda i: (0, 0)),
                  pl.BlockSpec((TM, D), lambda i: (jnp.minimum(i, n_lat_tiles - 1), 0))],
        out_specs=[pl.BlockSpec((TM, D), lambda i: (i, 0)), stat, stat],
        out_shape=[jax.ShapeDtypeStruct((T, D), F32), jax.ShapeDtypeStruct((1, 8, D), F32),
                   jax.ShapeDtypeStruct((1, 8, D), F32)],
        compiler_params=_cp(("arbitrary",)),
    )(x, nw, target)


def dact_call(dx2, modv, fsave, w_ffn_out, a, u, n_lat_tiles):
    T, D = dx2.shape
    F = a.shape[1]

    def body(dx_ref, mod_ref, f_ref, w_ref, a_ref, u_ref, da_ref, du_ref, df_ref, dg_ref):
        i = pl.program_id(0)
        dx = dx_ref[...]
        df = (dx * mod_ref[0, 5:6, :]).astype(BF16)
        df_ref[...] = df
        dact = _dot_nt(df, w_ref[...])
        av = a_ref[...].astype(F32)
        uv = u_ref[...].astype(F32)
        sg = _sigmoid(av)
        da_ref[...] = (dact * uv * (sg * (1.0 + av * (1.0 - sg)))).astype(BF16)
        du_ref[...] = (dact * (av * sg)).astype(BF16)
        _acc(dg_ref, _colsum8(dx * f_ref[...].astype(F32)), (i == 0) | (i == n_lat_tiles))

    row = pl.BlockSpec((TM, D), lambda i: (i, 0))
    wide = pl.BlockSpec((TM, F), lambda i: (i, 0))
    grp = lambda i: (_group_of(i, n_lat_tiles), 0, 0)
    return pl.pallas_call(
        body, name="dact", grid=(T // TM,),
        in_specs=[row, pl.BlockSpec((1, 6, D), grp), row, pl.BlockSpec((F, D), lambda i: (0, 0)), wide, wide],
        out_specs=[wide, wide, row, pl.BlockSpec((1, 8, D), grp)],
        out_shape=[jax.ShapeDtypeStruct((T, F), BF16), jax.ShapeDtypeStruct((T, F), BF16),
                   jax.ShapeDtypeStruct((T, D), BF16), jax.ShapeDtypeStruct((2, 8, D), F32)],
        compiler_params=_cp(("arbitrary",), VMEM_BIG),
    )(dx2, modv, fsave, w_ffn_out, a, u)


def _norm_bwd_tail(dh, x_ref, mod_ref, nw_ref, dres_ref, dx_ref, s_sh, s_sc, s_w, which, first):
    sc = mod_ref[0, 3 * which + 1:3 * which + 2, :]
    dxn, t_sh, t_sc, t_w = _norm_mod_bwd(dh, x_ref[...], nw_ref[...], sc)
    dx_ref[...] = dres_ref[...] + dxn
    _acc(s_sh, _colsum8(t_sh), first)
    _acc(s_sc, _colsum8(t_sc), first)
    _acc(s_w, _colsum8(t_w), first)


def dh2_call(da, du, w_ffn_in, x1, modv, nw2, dx2, n_lat_tiles):
    T, D = x1.shape
    F = da.shape[1]

    def body(da_ref, du_ref, wa_ref, wu_ref, x_ref, mod_ref, nw_ref, dres_ref, dx_ref, s_sh, s_sc, s_w):
        i = pl.program_id(0)
        dh = _dot_nt(da_ref[...], wa_ref[...]) + _dot_nt(du_ref[...], wu_ref[...])
        _norm_bwd_tail(dh, x_ref, mod_ref, nw_ref, dres_ref, dx_ref, s_sh, s_sc, s_w, 1,
                       (i == 0) | (i == n_lat_tiles))

    row = pl.BlockSpec((TM, D), lambda i: (i, 0))
    wide = pl.BlockSpec((TM, F), lambda i: (i, 0))
    grp = lambda i: (_group_of(i, n_lat_tiles), 0, 0)
    stat = pl.BlockSpec((1, 8, D), grp)
    return pl.pallas_call(
        body, name="dh2", grid=(T // TM,),
        in_specs=[wide, wide, pl.BlockSpec((D, F), lambda i: (0, 0)), pl.BlockSpec((D, F), lambda i: (0, 1)),
                  row, pl.BlockSpec((1, 6, D), grp), pl.BlockSpec((1, D), lambda i: (0, 0)), row],
        out_specs=[row, stat, stat, stat],
        out_shape=[jax.ShapeDtypeStruct((T, D), F32)] + [jax.ShapeDtypeStruct((2, 8, D), F32)] * 3,
        compiler_params=_cp(("arbitrary",), VMEM_BIG),
    )(da, du, w_ffn_in, w_ffn_in, x1, modv, nw2, dx2)


def dmerged_call(dx1, modv, osave, w_o, pa, pb, gates, n_lat_tiles):
    T, D = dx1.shape

    def body(dx_ref, mod_ref, o_ref, w_ref, pa_ref, pb_ref, ga_ref, gb_ref,
             dpa_ref, dpb_ref, dgate_ref, do_ref, dg_ref):
        i = pl.program_id(0)
        dx = dx_ref[...]
        do = (dx * mod_ref[0, 2:3, :]).astype(BF16)
        do_ref[...] = do
        dm = _dot_nt(do, w_ref[...])
        sa = _sigmoid(ga_ref[...].astype(F32))
        sb = _sigmoid(gb_ref[...].astype(F32))
        dpa_ref[...] = (dm * sa).astype(BF16)
        dpb_ref[...] = (dm * sb).astype(BF16)
        dgate_ref[:, :D] = (dm * pa_ref[...].astype(F32) * (sa * (1.0 - sa))).astype(BF16)
        dgate_ref[:, D:] = (dm * pb_ref[...].astype(F32) * (sb * (1.0 - sb))).astype(BF16)
        _acc(dg_ref, _colsum8(dx * o_ref[...].astype(F32)), (i == 0) | (i == n_lat_tiles))

    row = pl.BlockSpec((TM, D), lambda i: (i, 0))
    grp = lambda i: (_group_of(i, n_lat_tiles), 0, 0)
    return pl.pallas_call(
        body, name="dmerged", grid=(T // TM,),
        in_specs=[row, pl.BlockSpec((1, 6, D), grp), row, pl.BlockSpec((D, D), lambda i: (0, 0)), row, row,
                  pl.BlockSpec((TM, D), lambda i: (i, 0)), pl.BlockSpec((TM, D), lambda i: (i, 1))],
        out_specs=[row, row, pl.BlockSpec((TM, 2 * D), lambda i: (i, 0)), row, pl.BlockSpec((1, 8, D), grp)],
        out_shape=[jax.ShapeDtypeStruct((T, D), BF16), jax.ShapeDtypeStruct((T, D), BF16),
                   jax.ShapeDtypeStruct((T, 2 * D), BF16), jax.ShapeDtypeStruct((T, D), BF16),
                   jax.ShapeDtypeStruct((2, 8, D), F32)],
        compiler_params=_cp(("arbitrary",)),
    )(dx1, modv, osave, w_o, pa, pb, gates, gates)


def dy_call(dpa, dpb, w_pa, w_pb):
    T, D = dpa.shape
    K = w_pa.shape[0]

    def body(a_ref, b_ref, wa_ref, wb_ref, ya_ref, yb_ref):
        ya_ref[...] = _dot_nt(a_ref[...], wa_ref[...]).astype(BF16)
        yb_ref[...] = _dot_nt(b_ref[...], wb_ref[...]).astype(BF16)

    row = pl.BlockSpec((TM, D), lambda i: (i, 0))
    w = pl.BlockSpec((K, D), lambda i: (0, 0))
    out = pl.BlockSpec((TM, K), lambda i: (i, 0))
    return pl.pallas_call(
        body, name="dy", grid=(T // TM,), in_specs=[row, row, w, w], out_specs=[out, out],
        out_shape=[jax.ShapeDtypeStruct((T, K), BF16)] * 2,
        compiler_params=_cp(("parallel",)),
    )(dpa, dpb, w_pa, w_pb)


def dh_call(dna, dg, dgates, w_in, x, modv, nw1, dx1, n_lat_tiles):
    T, D = x.shape
    n1, n2, n3 = dna.shape[1], dg.shape[1], dgates.shape[1]
    N = n1 + n2 + n3

    def body(a_ref, b_ref, c_ref, w_ref, x_ref, mod_ref, nw_ref, dres_ref, dx_ref, s_sh, s_sc, s_w):
        i = pl.program_id(0)
        dh = (_dot_nt(a_ref[...], w_ref[:, :n1]) + _dot_nt(b_ref[...], w_ref[:, n1:n1 + n2])
              + _dot_nt(c_ref[...], w_ref[:, n1 + n2:]))
        _norm_bwd_tail(dh, x_ref, mod_ref, nw_ref, dres_ref, dx_ref, s_sh, s_sc, s_w, 0,
                       (i == 0) | (i == n_lat_tiles))

    row = pl.BlockSpec((TM, D), lambda i: (i, 0))
    grp = lambda i: (_group_of(i, n_lat_tiles), 0, 0)
    stat = pl.BlockSpec((1, 8, D), grp)
    return pl.pallas_call(
        body, name="dh", grid=(T // TM,),
        in_specs=[pl.BlockSpec((TM, n1), lambda i: (i, 0)), pl.BlockSpec((TM, n2), lambda i: (i, 0)),
                  pl.BlockSpec((TM, n3), lambda i: (i, 0)), pl.BlockSpec((D, N), lambda i: (0, 0)),
                  row, pl.BlockSpec((1, 6, D), grp), pl.BlockSpec((1, D), lambda i: (0, 0)), row],
        out_specs=[row, stat, stat, stat],
        out_shape=[jax.ShapeDtypeStruct((T, D), F32)] + [jax.ShapeDtypeStruct((2, 8, D), F32)] * 3,
        compiler_params=_cp(("arbitrary",), VMEM_BIG),
    )(dna, dg, dgates, w_in, x, modv, nw1, dx1)


def _pick_tile(n, cap, mult):
    best = None
    for d in range(mult, min(n, cap) + 1, mult):
        if n % d == 0:
            best = d
    assert best is not None, (n, cap, mult)
    return best


def wgrad_call(a, b, name):
    T, K = a.shape
    N = b.shape[1]
    tt = _pick_tile(T, 768, 128)
    tk = K if K <= 1024 else _pick_tile(K, 1408, 128)
    tn = N if N <= 2304 else _pick_tile(N, 2304, 128)
    nt = T // tt

    def body(a_ref, b_ref, o_ref, acc_ref):
        t = pl.program_id(2)
        p = _dot_tn(a_ref[...], b_ref[...])

        @pl.when(t == 0)
        def _():
            acc_ref[...] = p

        @pl.when(t > 0)
        def _():
            acc_ref[...] += p

        @pl.when(t == nt - 1)
        def _():
            o_ref[...] = acc_ref[...].astype(BF16)

    return pl.pallas_call(
        body, name=name, grid=(K // tk, N // tn, nt),
        in_specs=[pl.BlockSpec((tt, tk), lambda i, j, t: (t, i)), pl.BlockSpec((tt, tn), lambda i, j, t: (t, j))],
        out_specs=pl.BlockSpec((tk, tn), lambda i, j, t: (i, j)),
        out_shape=jax.ShapeDtypeStruct((K, N), BF16),
        scratch_shapes=[pltpu.VMEM((tk, tn), F32)],
        compiler_params=_cp(("parallel", "parallel", "arbitrary"), VMEM_BIG),
    )(a, b)


def _head_mean(v):
    n = v.shape[-1]
    r = lax.broadcasted_iota(jnp.int32, (n, n), 0) // HEAD_DIM
    c = lax.broadcasted_iota(jnp.int32, (n, n), 1) // HEAD_DIM
    bd = jnp.where(r == c, 1.0 / HEAD_DIM, 0.0).astype(BF16)
    hi = v.astype(BF16)
    lo = (v - hi.astype(F32)).astype(BF16)
    return _dot(hi, bd) + _dot(lo, bd)


def _swap_pairs(x):
    n = x.shape[-1]
    even = lax.broadcasted_iota(jnp.int32, x.shape, x.ndim - 1) % 2 == 0
    return jnp.where(even, pltpu.roll(x, n - 1, x.ndim - 1), pltpu.roll(x, 1, x.ndim - 1))


def _tile_lanes(t, n):
    return jnp.tile(t, (1, n // t.shape[-1]))


def gqa_prep_call(gqkv, qgain, kgain, cos_t, sin_t):
    T = gqkv.shape[0]
    nq = GQA_Q_HEADS * HEAD_DIM
    nk = GQA_KV_HEADS * HEAD_DIM
    rep = GQA_Q_HEADS // GQA_KV_HEADS

    def body(g_ref, qg_ref, kg_ref, c_ref, s_ref, qp_ref, kp_ref):
        cos_v, sin_v = c_ref[...], s_ref[...]
        xq = g_ref[:, :nq].astype(F32)
        yq = xq * lax.rsqrt(_head_mean(xq * xq) + EPS) * qg_ref[...]
        yq = (yq * _tile_lanes(cos_v, nq) + _swap_pairs(yq) * _tile_lanes(sin_v, nq)) * SCALE
        lo = _lane_lt64((TM, LANES))
        for p in range(GQA_Q_HEADS // 2):
            g = (2 * p) // rep
            chunk = yq[:, LANES * p:LANES * (p + 1)]
            rolled = _roll64(chunk)
            if g == 0:
                first, second = jnp.where(lo, chunk, 0.0), jnp.where(lo, rolled, 0.0)
            else:
                first, second = jnp.where(lo, 0.0, rolled), jnp.where(lo, 0.0, chunk)
            qp_ref[:, 2 * LANES * p:2 * LANES * p + LANES] = first.astype(BF16)
            qp_ref[:, 2 * LANES * p + LANES:2 * LANES * (p + 1)] = second.astype(BF16)
        xk = g_ref[:, nq:nq + nk].astype(F32)
        yk = xk * lax.rsqrt(_head_mean(xk * xk) + EPS) * kg_ref[...]
        kp_ref[...] = (yk * cos_v + _swap_pairs(yk) * sin_v).astype(BF16)

    return pl.pallas_call(
        body, name="gqa_prep", grid=(T // TM,),
        in_specs=[pl.BlockSpec((TM, gqkv.shape[1]), lambda i: (i, 0)),
                  pl.BlockSpec((1, nq), lambda i: (0, 0)), pl.BlockSpec((1, nk), lambda i: (0, 0)),
                  pl.BlockSpec((TM, nk), lambda i: (i, 0)), pl.BlockSpec((TM, nk), lambda i: (i, 0))],
        out_specs=[pl.BlockSpec((TM, GQA_Q_HEADS * LANES), lambda i: (i, 0)), pl.BlockSpec((TM, nk), lambda i: (i, 0))],
        out_shape=[jax.ShapeDtypeStruct((T, GQA_Q_HEADS * LANES), BF16), jax.ShapeDtypeStruct((T, nk), BF16)],
        compiler_params=_cp(("parallel",)),
    )(gqkv, qgain, kgain, cos_t, sin_t)


def gqa_prep_bwd_call(gqkv, qgain, kgain, cos_t, sin_t, dqp, dkp, dv):
    T = gqkv.shape[0]
    nq = GQA_Q_HEADS * HEAD_DIM
    nk = GQA_KV_HEADS * HEAD_DIM
    rep = GQA_Q_HEADS // GQA_KV_HEADS

    def norm_rope_bwd(x, gain, dy, cos_v, sin_v):
        dyn = dy * cos_v - _swap_pairs(dy) * sin_v
        r = lax.rsqrt(_head_mean(x * x) + EPS)
        xn = x * r
        dxn = dyn * gain
        dx = r * (dxn - xn * _head_mean(dxn * xn))
        return dx, dyn * xn

    def body(g_ref, qg_ref, kg_ref, c_ref, s_ref, dqp_ref, dkp_ref, dv_ref, dg_ref, sq_ref, sk_ref):
        i = pl.program_id(0)
        cos_v, sin_v = c_ref[...], s_ref[...]
        lo = _lane_lt64((TM, LANES))
        chunks = []
        for p in range(GQA_Q_HEADS // 2):
            g = (2 * p) // rep
            da = dqp_ref[:, 2 * LANES * p:2 * LANES * p + LANES]
            db = dqp_ref[:, 2 * LANES * p + LANES:2 * LANES * (p + 1)]
            if g == 0:
                chunks.append(jnp.where(lo, da, _roll64(db)))
            else:
                chunks.append(jnp.where(lo, _roll64(da), db))
        dyq = jnp.concatenate(chunks, axis=1) * SCALE
        dxq, gq_term = norm_rope_bwd(g_ref[:, :nq].astype(F32), qg_ref[...], dyq,
                                     _tile_lanes(cos_v, nq), _tile_lanes(sin_v, nq))
        dxk, gk_term = norm_rope_bwd(g_ref[:, nq:nq + nk].astype(F32), kg_ref[...], dkp_ref[...], cos_v, sin_v)
        dg_ref[:, :nq] = dxq.astype(BF16)
        dg_ref[:, nq:nq + nk] = dxk.astype(BF16)
        dg_ref[:, nq + nk:] = dv_ref[...].astype(BF16)
        _acc(sq_ref, _colsum8(gq_term), i == 0)
        _acc(sk_ref, _colsum8(gk_term), i == 0)

    W = gqkv.shape[1]
    return pl.pallas_call(
        body, name="gqa_prep_bwd", grid=(T // TM,),
        in_specs=[pl.BlockSpec((TM, W), lambda i: (i, 0)),
                  pl.BlockSpec((1, nq), lambda i: (0, 0)), pl.BlockSpec((1, nk), lambda i: (0, 0)),
                  pl.BlockSpec((TM, nk), lambda i: (i, 0)), pl.BlockSpec((TM, nk), lambda i: (i, 0)),
                  pl.BlockSpec((TM, GQA_Q_HEADS * LANES), lambda i: (i, 0)),
                  pl.BlockSpec((TM, nk), lambda i: (i, 0)), pl.BlockSpec((TM, nk), lambda i: (i, 0))],
        out_specs=[pl.BlockSpec((TM, W), lambda i: (i, 0)),
                   pl.BlockSpec((1, 8, nq), lambda i: (0, 0, 0)), pl.BlockSpec((1, 8, nk), lambda i: (0, 0, 0))],
        out_shape=[jax.ShapeDtypeStruct((T, W), BF16), jax.ShapeDtypeStruct((1, 8, nq), F32),
                   jax.ShapeDtypeStruct((1, 8, nk), F32)],
        compiler_params=_cp(("arbitrary",)),
    )(gqkv, qgain, kgain, cos_t, sin_t, dqp, dkp, dv)


GQ_TQ = 256
GQ_TK = 256


def _chunk_offset(j):
    return j * GQ_TK if isinstance(j, int) else pl.multiple_of(j * GQ_TK, GQ_TK)


GQ_CHUNKS_PER_TRIP = 8


def _sweep_keys(chunk, is_latent_query, n_lat_k, nk):
    per_trip = math.gcd(n_lat_k, GQ_CHUNKS_PER_TRIP)

    @pl.when(is_latent_query)
    def _():
        def trip(jj, carry):
            for t in range(per_trip):
                chunk(per_trip * jj + t)
            return carry

        lax.fori_loop(0, n_lat_k // per_trip, trip, 0)

    for j in range(n_lat_k, nk):
        chunk(j)


def _gqa_place(chunk, half, g):
    gv = jnp.broadcast_to(g, chunk.shape)
    placed = jnp.where(gv == half, chunk, _roll64(chunk))
    return jnp.where(_lane_half(chunk.shape) == gv, placed, jnp.zeros_like(placed))


GQA_SMALL_SCORE = 20.0


def gqa_scores_small(q_gain, k_gain):
    bound = SCALE * HEAD_DIM * jnp.max(jnp.abs(q_gain)) * jnp.max(jnp.abs(k_gain))
    return (bound <= GQA_SMALL_SCORE).astype(jnp.int32).reshape(1)


def gqa_fwd_call(qp, kp, gqkv, n_lat, scores_small):
    T = qp.shape[0]
    rep = GQA_Q_HEADS // GQA_KV_HEADS
    nq = T // GQ_TQ
    nk = T // GQ_TK
    n_lat_q = n_lat // GQ_TQ
    n_lat_k = n_lat // GQ_TK
    v_blk = (GQA_Q_HEADS + GQA_KV_HEADS) * HEAD_DIM // LANES
    R = rep * GQ_TQ

    def body(small_ref, q_ref, k_ref, v_ref, y_ref, lse_ref, m_sc, acc_sc):
        g = pl.program_id(0)
        qb = pl.program_id(1)
        qs = jnp.concatenate([q_ref[:, LANES * i:LANES * (i + 1)] for i in range(rep)], axis=0)
        is_lat = qb < n_lat_q
        nsub = GQ_TK // LANES
        scores_small = small_ref[0] == 1

        @pl.when(scores_small)
        def _():
            m_sc[...] = jnp.zeros((R, LANES), F32)

        @pl.when(jnp.logical_not(scores_small))
        def _():
            m_sc[...] = jnp.full((R, LANES), NEG, F32)

            def chunk_max(j):
                s = _dot_nt(qs, k_ref[pl.ds(_chunk_offset(j), GQ_TK), :])
                mm = s[:, :LANES]
                for t in range(1, nsub):
                    mm = jnp.maximum(mm, s[:, LANES * t:LANES * (t + 1)])
                m_sc[...] = jnp.maximum(m_sc[...], mm)

            _sweep_keys(chunk_max, is_lat, n_lat_k, nk)
            m_sc[...] = jnp.broadcast_to(jnp.max(m_sc[...], axis=-1, keepdims=True), (R, LANES))

        acc_sc[...] = jnp.zeros((R, LANES), F32)
        own_v = _lane_half((GQ_TK, LANES)) == jnp.broadcast_to(g, (GQ_TK, LANES))

        def chunk_acc(j):
            off = _chunk_offset(j)
            vc = v_ref[pl.ds(off, GQ_TK), :]
            v1 = jnp.where(own_v, vc, jnp.ones_like(vc))
            s = _dot_nt(qs, k_ref[pl.ds(off, GQ_TK), :])
            mb = m_sc[...]
            p = jnp.concatenate([jnp.exp(s[:, LANES * t:LANES * (t + 1)] - mb).astype(BF16) for t in range(nsub)],
                                axis=1)
            acc_sc[...] += _dot(p, v1)

        _sweep_keys(chunk_acc, is_lat, n_lat_k, nk)
        acc = acc_sc[...]
        own = _lane_half((R, LANES)) == jnp.broadcast_to(g, (R, LANES))
        l = jnp.where(own, _roll64(acc), acc)
        o = acc / l
        lse = m_sc[...] + jnp.log(l)
        lo = _lane_lt64((GQ_TQ, LANES))
        g0 = jnp.broadcast_to(g, (GQ_TQ, LANES)) == 0
        for c2 in range(rep // 2):
            oa = o[(2 * c2) * GQ_TQ:(2 * c2 + 1) * GQ_TQ]
            ob = o[(2 * c2 + 1) * GQ_TQ:(2 * c2 + 2) * GQ_TQ]
            left = jnp.where(g0, oa, _roll64(oa))
            right = jnp.where(g0, _roll64(ob), ob)
            y_ref[:, LANES * c2:LANES * (c2 + 1)] = jnp.where(lo, left, right).astype(BF16)
        for i in range(rep):
            lse_ref[i] = lse[i * GQ_TQ:(i + 1) * GQ_TQ]

    return pl.pallas_call(
        body, name="gqa_fwd",
        grid_spec=pltpu.PrefetchScalarGridSpec(
            num_scalar_prefetch=1, grid=(GQA_KV_HEADS, nq),
            in_specs=[pl.BlockSpec((GQ_TQ, rep * LANES), lambda g, i, s: (i, g)),
                      pl.BlockSpec((T, LANES), lambda g, i, s: (0, 0)),
                      pl.BlockSpec((T, LANES), lambda g, i, s: (0, v_blk))],
            out_specs=[pl.BlockSpec((GQ_TQ, rep * HEAD_DIM), lambda g, i, s: (i, g)),
                       pl.BlockSpec((rep, GQ_TQ, LANES), lambda g, i, s: (g, i, 0))],
            scratch_shapes=[pltpu.VMEM((R, LANES), F32), pltpu.VMEM((R, LANES), F32)]),
        out_shape=[jax.ShapeDtypeStruct((T, GQA_Q_HEADS * HEAD_DIM), BF16),
                   jax.ShapeDtypeStruct((GQA_Q_HEADS, T, LANES), F32)],
        compiler_params=_cp(("parallel", "parallel"), VMEM_BIG),
    )(scores_small, qp, kp, gqkv)


def gqa_bwd_call(qp, kp, gqkv, yb, dyb, lse, n_lat):
    T = qp.shape[0]
    rep = GQA_Q_HEADS // GQA_KV_HEADS
    nq = T // GQ_TQ
    nk = T // GQ_TK
    n_lat_q = n_lat // GQ_TQ
    n_lat_k = n_lat // GQ_TK
    v_blk = (GQA_Q_HEADS + GQA_KV_HEADS) * HEAD_DIM // LANES
    R = rep * GQ_TQ

    def body(q_ref, k_ref, v_ref, y_ref, dy_ref, lse_ref, dq_ref, dk_ref, dv_ref, dq_sc):
        g = pl.program_id(0)
        qb = pl.program_id(1)

        @pl.when((g == 0) & (qb == 0))
        def _():
            dk_ref[...] = jnp.zeros((T, LANES), F32)
            dv_ref[...] = jnp.zeros((T, LANES), F32)

        qs = jnp.concatenate([q_ref[:, LANES * i:LANES * (i + 1)] for i in range(rep)], axis=0)
        dos, deltas, lses = [], [], []
        for i in range(rep):
            c2, half = i // 2, i % 2
            dch = _gqa_place(dy_ref[:, LANES * c2:LANES * (c2 + 1)].astype(F32), half, g)
            ych = _gqa_place(y_ref[:, LANES * c2:LANES * (c2 + 1)].astype(F32), half, g)
            dos.append(dch.astype(BF16))
            deltas.append(jnp.sum(dch * ych, axis=-1, keepdims=True))
            lses.append(lse_ref[i])
        do_s = jnp.concatenate(dos, axis=0)
        delta = jnp.broadcast_to(jnp.concatenate(deltas, axis=0), (R, LANES))
        lse_v = jnp.concatenate(lses, axis=0)
        dq_sc[...] = jnp.zeros((R, LANES), F32)
        nsub = GQ_TK // LANES

        def chunk(j):
            off = _chunk_offset(j)
            kc = k_ref[pl.ds(off, GQ_TK), :]
            vc = v_ref[pl.ds(off, GQ_TK), :]
            s = _dot_nt(qs, kc)
            dp = _dot_nt(do_s, vc)
            ps, dss = [], []
            for t in range(nsub):
                pt = jnp.exp(s[:, LANES * t:LANES * (t + 1)] - lse_v)
                ps.append(pt.astype(BF16))
                dss.append((pt * (dp[:, LANES * t:LANES * (t + 1)] - delta)).astype(BF16))
            p = jnp.concatenate(ps, axis=1)
            ds = jnp.concatenate(dss, axis=1)
            dv_ref[pl.ds(off, GQ_TK), :] += _dot_tn(p, do_s)
            dk_ref[pl.ds(off, GQ_TK), :] += _dot_tn(ds, qs)
            dq_sc[...] += _dot(ds, kc)

        _sweep_keys(chunk, qb < n_lat_q, n_lat_k, nk)
        for i in range(rep):
            dq_ref[:, LANES * i:LANES * (i + 1)] = dq_sc[i * GQ_TQ:(i + 1) * GQ_TQ, :]

    whole = pl.BlockSpec((T, LANES), lambda g, i: (0, 0))
    return pl.pallas_call(
        body, name="gqa_bwd", grid=(GQA_KV_HEADS, nq),
        in_specs=[pl.BlockSpec((GQ_TQ, rep * LANES), lambda g, i: (i, g)),
                  whole,
                  pl.BlockSpec((T, LANES), lambda g, i: (0, v_blk)),
                  pl.BlockSpec((GQ_TQ, rep * HEAD_DIM), lambda g, i: (i, g)),
                  pl.BlockSpec((GQ_TQ, rep * HEAD_DIM), lambda g, i: (i, g)),
                  pl.BlockSpec((rep, GQ_TQ, LANES), lambda g, i: (g, i, 0))],
        out_specs=[pl.BlockSpec((GQ_TQ, rep * LANES), lambda g, i: (i, g)), whole, whole],
        out_shape=[jax.ShapeDtypeStruct((T, GQA_Q_HEADS * LANES), F32),
                   jax.ShapeDtypeStruct((T, LANES), F32), jax.ShapeDtypeStruct((T, LANES), F32)],
        scratch_shapes=[pltpu.VMEM((R, LANES), F32)],
        compiler_params=_cp(("arbitrary", "arbitrary"), VMEM_BIG),
    )(qp, kp, gqkv, yb, dyb, lse)


def _na_tables(rows):
    kh = min(NA_WIN_H, rows)
    assert kh == NA_WIN_H and rows >= NA_KROWS and rows % NA_ROWS == 0
    a = np.zeros((3, NA_ROWS, NA_KROWS, 2 * NA_WIN_H - 1), np.float32)
    for v, r0 in enumerate((0, NA_ROWS, rows - NA_ROWS)):
        ks = min(max(r0 - NA_WIN_H // 2, 0), rows - NA_KROWS)
        for rq in range(NA_ROWS):
            r = r0 + rq
            rs = min(max(r - kh // 2, 0), rows - kh)
            for rk in range(NA_KROWS):
                kr = ks + rk
                if rs <= kr < rs + kh:
                    a[v, rq, rk, kr - r + NA_WIN_H - 1] = 1.0
    c = np.zeros((GRID_W, GRID_W, 2 * NA_WIN_W - 1), np.float32)
    for wq in range(GRID_W):
        cs = min(max(wq - NA_WIN_W // 2, 0), GRID_W - NA_WIN_W)
        for wk in range(cs, cs + NA_WIN_W):
            c[wq, wk, min(max(wk - wq, -(NA_WIN_W - 1)), NA_WIN_W - 1) + NA_WIN_W - 1] = 1.0
    return a, c


NA_DR = 2 * NA_WIN_H - 1


def _na_tile_index(rows):
    a, _ = _na_tables(rows)
    idx = np.where(a.sum(-1) > 0, a.argmax(-1), NA_DR).astype(np.int32)
    return jnp.asarray(idx.reshape(-1))


def na_bias_table(rpb, rows):
    _, c = _na_tables(rows)
    H = rpb.shape[0]
    KW = NA_KROWS * GRID_W
    t1 = jnp.einsum("hde,wue->hdwu", rpb, jnp.asarray(c), precision=lax.Precision.HIGHEST)
    t1 = jnp.where(jnp.asarray(c.sum(-1) > 0)[None, None], t1, NEG)
    t1 = jnp.concatenate([t1, jnp.full((H, 1, GRID_W, GRID_W), NEG, F32)], axis=1)
    t1 = jnp.tile(t1, (1, 1, 1, 2))

    def body(idx_ref, t_ref, o_ref):
        v = pl.program_id(1)
        lo = _lane_lt64((GRID_W, LANES))
        for q in range(NA_ROWS):
            for kp in range(NA_KROWS // 2):
                base = (v * NA_ROWS + q) * NA_KROWS + 2 * kp
                tile = jnp.where(lo, t_ref[0, idx_ref[base]], t_ref[0, idx_ref[base + 1]])
                o_ref[0, 0, GRID_W * q:GRID_W * (q + 1), LANES * kp:LANES * (kp + 1)] = tile

    return pl.pallas_call(
        body, name="na_bias_table",
        grid_spec=pltpu.PrefetchScalarGridSpec(
            num_scalar_prefetch=1, grid=(H, 3),
            in_specs=[pl.BlockSpec((1, NA_DR + 1, GRID_W, LANES), lambda h, v, i: (h, 0, 0, 0))],
            out_specs=pl.BlockSpec((1, 1, TM, KW), lambda h, v, i: (h, v, 0, 0))),
        out_shape=jax.ShapeDtypeStruct((H, 3, TM, KW), F32),
        compiler_params=_cp(("parallel", "parallel")),
    )(_na_tile_index(rows), t1)


def na_bias_grad(dbias, rows):
    _, c = _na_tables(rows)
    H = dbias.shape[0]
    KW = NA_KROWS * GRID_W

    def body(idx_ref, d_ref, o_ref):
        v = pl.program_id(1)
        lo = _lane_lt64((GRID_W, LANES))

        @pl.when(v == 0)
        def _():
            o_ref[...] = jnp.zeros((1, NA_DR + 1, GRID_W, LANES), F32)

        for q in range(NA_ROWS):
            for kp in range(NA_KROWS // 2):
                base = (v * NA_ROWS + q) * NA_KROWS + 2 * kp
                tile = d_ref[0, 0, GRID_W * q:GRID_W * (q + 1), LANES * kp:LANES * (kp + 1)]
                o_ref[0, idx_ref[base]] += jnp.where(lo, tile, 0.0)
                o_ref[0, idx_ref[base + 1]] += jnp.where(lo, 0.0, tile)

    dt = pl.pallas_call(
        body, name="na_bias_grad",
        grid_spec=pltpu.PrefetchScalarGridSpec(
            num_scalar_prefetch=1, grid=(H, 3),
            in_specs=[pl.BlockSpec((1, 1, TM, KW), lambda h, v, i: (h, v, 0, 0))],
            out_specs=pl.BlockSpec((1, NA_DR + 1, GRID_W, LANES), lambda h, v, i: (h, 0, 0, 0))),
        out_shape=jax.ShapeDtypeStruct((H, NA_DR + 1, GRID_W, LANES), F32),
        compiler_params=_cp(("parallel", "arbitrary")),
    )(_na_tile_index(rows), dbias)
    dt = dt[:, :NA_DR, :, :GRID_W] + dt[:, :NA_DR, :, GRID_W:]
    return jnp.einsum("hdwu,wue->hde", dt, jnp.asarray(c), precision=lax.Precision.HIGHEST)


def _na_variant(blk, n_lat_blk):
    return jnp.where(blk == 0, 0, jnp.where(blk >= n_lat_blk - 1, 2, 1))


def _na_kstart(blk, rows):
    r0 = blk * NA_ROWS
    ks = jnp.clip(r0 - NA_WIN_H // 2, 0, rows - NA_KROWS)
    return pl.multiple_of(ks * GRID_W, GRID_W)


def na_fwd_call(naqkv, bias, n_lat):
    T = naqkv.shape[0]
    nb = T // TM
    n_lat_blk = n_lat // TM
    rows = n_lat // GRID_W
    n_ctx = T - n_lat
    KW = NA_KROWS * GRID_W
    npair = NA_HEADS // 2

    def body(q_ref, k_ref, v_ref, b_ref, y_ref, lse_ref):
        blk = pl.program_id(1)
        half = _lane_half((TM, LANES))
        lo = half == 0
        q = q_ref[...].astype(F32) * SCALE
        kctx = k_ref[pl.ds(n_lat, n_ctx), :]
        vctx = v_ref[pl.ds(n_lat, n_ctx), :]

        @pl.when(blk < n_lat_blk)
        def _():
            off = _na_kstart(blk, rows)
            kwin = k_ref[pl.ds(off, KW), :]
            vwin = v_ref[pl.ds(off, KW), :]
            outs = []
            for hh in range(2):
                qm = jnp.where(half == hh, q, 0.0).astype(BF16)
                sw = _dot_nt(qm, kwin) + b_ref[hh, 0]
                sc = _dot_nt(qm, kctx)
                m = jnp.maximum(jnp.max(sw, axis=-1, keepdims=True), jnp.max(sc, axis=-1, keepdims=True))
                pw = jnp.exp(sw - m)
                pc = jnp.exp(sc - m)
                l = jnp.sum(pw, axis=-1, keepdims=True) + jnp.sum(pc, axis=-1, keepdims=True)
                outs.append((_dot(pw.astype(BF16), vwin) + _dot(pc.astype(BF16), vctx)) / l)
                lse_ref[hh] = jnp.broadcast_to(m + jnp.log(l), (TM, LANES))
            y_ref[...] = jnp.where(lo, outs[0], outs[1]).astype(BF16)

        @pl.when(blk >= n_lat_blk)
        def _():
            outs = []
            for hh in range(2):
                qm = jnp.where(half == hh, q, 0.0).astype(BF16)
                sc = _dot_nt(qm, kctx)
                m = jnp.max(sc, axis=-1, keepdims=True)
                pc = jnp.exp(sc - m)
                l = jnp.sum(pc, axis=-1, keepdims=True)
                outs.append(_dot(pc.astype(BF16), vctx) / l)
                lse_ref[hh] = jnp.broadcast_to(m + jnp.log(l), (TM, LANES))
            y_ref[...] = jnp.where(lo, outs[0], outs[1]).astype(BF16)

    return pl.pallas_call(
        body, name="na_fwd", grid=(npair, nb),
        in_specs=[pl.BlockSpec((TM, LANES), lambda p, b: (b, p)),
                  pl.BlockSpec((T, LANES), lambda p, b: (0, npair + p)),
                  pl.BlockSpec((T, LANES), lambda p, b: (0, 2 * npair + p)),
                  pl.BlockSpec((2, 1, TM, KW), lambda p, b: (p, _na_variant(b, n_lat_blk), 0, 0))],
        out_specs=[pl.BlockSpec((TM, LANES), lambda p, b: (b, p)),
                   pl.BlockSpec((2, TM, LANES), lambda p, b: (p, b, 0))],
        out_shape=[jax.ShapeDtypeStruct((T, NA_HEADS * HEAD_DIM), BF16),
                   jax.ShapeDtypeStruct((NA_HEADS, T, LANES), F32)],
        compiler_params=_cp(("parallel", "parallel"), VMEM_BIG),
    )(naqkv, naqkv, naqkv, bias)


def na_bwd_call(naqkv, bias, ya, dya, lse, n_lat):
    T = naqkv.shape[0]
    nb = T // TM
    n_lat_blk = n_lat // TM
    rows = n_lat // GRID_W
    n_ctx = T - n_lat
    KW = NA_KROWS * GRID_W
    npair = NA_HEADS // 2

    def body(q_ref, k_ref, v_ref, b_ref, y_ref, dy_ref, lse_ref, dq_ref, dk_ref, dv_ref, db_ref, dk_sc, dv_sc):
        blk = pl.program_id(1)
        half = _lane_half((TM, LANES))
        lo = half == 0
        q = q_ref[...].astype(F32) * SCALE
        kctx = k_ref[pl.ds(n_lat, n_ctx), :]
        vctx = v_ref[pl.ds(n_lat, n_ctx), :]
        dyv = dy_ref[...].astype(F32)
        yv = y_ref[...].astype(F32)

        @pl.when(blk == 0)
        def _():
            dk_sc[...] = jnp.zeros((T, LANES), F32)
            dv_sc[...] = jnp.zeros((T, LANES), F32)

        @pl.when(blk < n_lat_blk)
        def _():
            off = _na_kstart(blk, rows)
            kwin = k_ref[pl.ds(off, KW), :]
            vwin = v_ref[pl.ds(off, KW), :]
            first = (blk == 0) | (blk == 1) | (blk == n_lat_blk - 1)
            dqs = []
            for hh in range(2):
                sel = half == hh
                qm = jnp.where(sel, q, 0.0).astype(BF16)
                dom = jnp.where(sel, dyv, 0.0)
                delta = jnp.sum(dom * yv, axis=-1, keepdims=True)
                dom = dom.astype(BF16)
                lse_v = lse_ref[hh][:, 0:1]
                pw = jnp.exp(_dot_nt(qm, kwin) + b_ref[hh, 0] - lse_v)
                pc = jnp.exp(_dot_nt(qm, kctx) - lse_v)
                dsw = pw * (_dot_nt(dom, vwin) - delta)
                dsc = pc * (_dot_nt(dom, vctx) - delta)

                @pl.when(first)
                def _():
                    db_ref[hh, 0] = dsw

                @pl.when(jnp.logical_not(first))
                def _():
                    db_ref[hh, 0] += dsw

                dsw = dsw.astype(BF16)
                dsc = dsc.astype(BF16)
                dv_sc[pl.ds(off, KW), :] += _dot_tn(pw.astype(BF16), dom)
                dv_sc[pl.ds(n_lat, n_ctx), :] += _dot_tn(pc.astype(BF16), dom)
                dk_sc[pl.ds(off, KW), :] += _dot_tn(dsw, qm)
                dk_sc[pl.ds(n_lat, n_ctx), :] += _dot_tn(dsc, qm)
                dqs.append((_dot(dsw, kwin) + _dot(dsc, kctx)) * SCALE)
            dq_ref[...] = jnp.where(lo, dqs[0], dqs[1]).astype(BF16)

        @pl.when(blk >= n_lat_blk)
        def _():
            dqs = []
            for hh in range(2):
                sel = half == hh
                qm = jnp.where(sel, q, 0.0).astype(BF16)
                dom = jnp.where(sel, dyv, 0.0)
                delta = jnp.sum(dom * yv, axis=-1, keepdims=True)
                dom = dom.astype(BF16)
                pc = jnp.exp(_dot_nt(qm, kctx) - lse_ref[hh][:, 0:1])
                dsc = (pc * (_dot_nt(dom, vctx) - delta)).astype(BF16)
                dv_sc[pl.ds(n_lat, n_ctx), :] += _dot_tn(pc.astype(BF16), dom)
                dk_sc[pl.ds(n_lat, n_ctx), :] += _dot_tn(dsc, qm)
                dqs.append(_dot(dsc, kctx) * SCALE)
            dq_ref[...] = jnp.where(lo, dqs[0], dqs[1]).astype(BF16)

        @pl.when(blk == nb - 1)
        def _():
            dk_ref[...] = dk_sc[...].astype(BF16)
            dv_ref[...] = dv_sc[...].astype(BF16)

    blkspec = pl.BlockSpec((TM, LANES), lambda p, b: (b, p))
    col = pl.BlockSpec((T, LANES), lambda p, b: (0, p))
    bspec = pl.BlockSpec((2, 1, TM, KW), lambda p, b: (p, _na_variant(b, n_lat_blk), 0, 0))
    dq, dk, dv, db = pl.pallas_call(
        body, name="na_bwd", grid=(npair, nb),
        in_specs=[blkspec,
                  pl.BlockSpec((T, LANES), lambda p, b: (0, npair + p)),
                  pl.BlockSpec((T, LANES), lambda p, b: (0, 2 * npair + p)),
                  bspec, blkspec, blkspec,
                  pl.BlockSpec((2, TM, LANES), lambda p, b: (p, b, 0))],
        out_specs=[blkspec, col, col, bspec],
        out_shape=[jax.ShapeDtypeStruct((T, NA_HEADS * HEAD_DIM), BF16)] * 3
        + [jax.ShapeDtypeStruct((NA_HEADS, 3, TM, KW), F32)],
        scratch_shapes=[pltpu.VMEM((T, LANES), F32), pltpu.VMEM((T, LANES), F32)],
        compiler_params=_cp(("arbitrary", "arbitrary"), VMEM_BIG),
    )(naqkv, naqkv, naqkv, bias, ya, dya, lse)
    return jnp.concatenate([dq, dk, dv], axis=1), db


def rope_tables(n_lat, n_ctx):
    t = jnp.arange(n_lat)
    row = (t // GRID_W).astype(F32)
    col = (t % GRID_W).astype(F32)
    half = HEAD_DIM // 2
    inv = ROPE_THETA ** (-jnp.arange(0, half, 2, dtype=F32) / half)
    ang = jnp.concatenate([row[:, None] * inv, col[:, None] * inv], axis=-1)
    cos = jnp.repeat(jnp.cos(ang), 2, axis=-1)
    sin = jnp.repeat(jnp.sin(ang), 2, axis=-1) * jnp.tile(jnp.array([-1.0, 1.0], F32), half)
    cos = jnp.concatenate([cos, jnp.ones((n_ctx, HEAD_DIM), F32)], axis=0)
    sin = jnp.concatenate([sin, jnp.zeros((n_ctx, HEAD_DIM), F32)], axis=0)
    return jnp.tile(cos, (1, 2)), jnp.tile(sin, (1, 2))


def _row(v):
    return v.reshape(1, -1)


def device_fwd_bwd(xs, target, modv, w, small, n_lat):
    T, D = xs.shape
    n_ctx = T - n_lat
    L = modv.shape[0]
    nlt = n_lat // TM
    rows = n_lat // GRID_W
    cos_t, sin_t = rope_tables(n_lat, n_ctx)
    qg = [jnp.tile(_row(small["q_gain"][l]), (1, GQA_Q_HEADS)) for l in range(L)]
    kg = [jnp.tile(_row(small["k_gain"][l]), (1, GQA_KV_HEADS)) for l in range(L)]
    bias = [na_bias_table(small["na_rpb"][l], rows) for l in range(L)]

    saved = []
    x = xs
    h = norm_mod_call(x, _row(small["norm1"][0]), modv[0], nlt, 0)
    for l in range(L):
        naqkv, gqkv, gates = qkv_call(h, w["w_in"][l])
        ya, lse_a = na_fwd_call(naqkv, bias[l], n_lat)
        qp, kp = gqa_prep_call(gqkv, qg[l], kg[l], cos_t, sin_t)
        yb, lse_b = gqa_fwd_call(qp, kp, gqkv, n_lat, gqa_scores_small(small["q_gain"][l], small["k_gain"][l]))
        merged, pa, pb = merge_call(ya, yb, gates, w["w_pa"][l], w["w_pb"][l])
        x1, osave, h2 = wo_call(merged, w["w_o"][l], x, modv[l], _row(small["norm2"][l]), nlt)
        act, a, u = ffn_in_call(h2, w["w_ffn_in"][l])
        if l + 1 < L:
            x2, fsave, hn = ffn_out_call(act, w["w_ffn_out"][l], x1, modv[l], nlt,
                                         _row(small["norm1"][l + 1]), modv[l + 1])
        else:
            x2, fsave = ffn_out_call(act, w["w_ffn_out"][l], x1, modv[l], nlt)
            hn = None
        saved.append(dict(x=x, h=h, naqkv=naqkv, gqkv=gqkv, gates=gates, ya=ya, lse_a=lse_a, qp=qp, kp=kp,
                          yb=yb, lse_b=lse_b, merged=merged, pa=pa, pb=pb, x1=x1, osave=osave, h2=h2,
                          act=act, a=a, u=u, fsave=fsave))
        x, h = x2, hn

    dx, d_final, loss_part = loss_call(x, _row(small["final_norm"]), target, nlt)

    gw = {k: [None] * L for k in ("w_in", "w_pa", "w_pb", "w_o", "w_ffn_in", "w_ffn_out")}
    gs = {k: [None] * L for k in ("norm1", "norm2", "na_rpb", "q_gain", "k_gain")}
    dmod = [None] * L
    fold = lambda s: s.sum(axis=1)
    for l in reversed(range(L)):
        s = saved[l]
        da, du, df, s_g2 = dact_call(dx, modv[l], s["fsave"], w["w_ffn_out"][l], s["a"], s["u"], nlt)
        gw["w_ffn_out"][l] = wgrad_call(s["act"], df, "wgrad_ffn_out")
        dx1, s_sh2, s_sc2, s_n2 = dh2_call(da, du, w["w_ffn_in"][l], s["x1"], modv[l], _row(small["norm2"][l]), dx, nlt)
        gw["w_ffn_in"][l] = jnp.concatenate(
            [wgrad_call(s["h2"], da, "wgrad_ffn_in_a"), wgrad_call(s["h2"], du, "wgrad_ffn_in_u")], axis=1)
        dpa, dpb, dgates, do, s_g1 = dmerged_call(dx1, modv[l], s["osave"], w["w_o"][l], s["pa"], s["pb"], s["gates"], nlt)
        gw["w_o"][l] = wgrad_call(s["merged"], do, "wgrad_o")
        dya, dyb = dy_call(dpa, dpb, w["w_pa"][l], w["w_pb"][l])
        gw["w_pa"][l] = wgrad_call(s["ya"], dpa, "wgrad_pa")
        gw["w_pb"][l] = wgrad_call(s["yb"], dpb, "wgrad_pb")
        dna, dbias = na_bwd_call(s["naqkv"], bias[l], s["ya"], dya, s["lse_a"], n_lat)
        gs["na_rpb"][l] = na_bias_grad(dbias, rows)
        dqp, dkp, dv = gqa_bwd_call(s["qp"], s["kp"], s["gqkv"], s["yb"], dyb, s["lse_b"], n_lat)
        dg, s_qg, s_kg = gqa_prep_bwd_call(s["gqkv"], qg[l], kg[l], cos_t, sin_t, dqp, dkp, dv)
        gs["q_gain"][l] = s_qg.reshape(8 * GQA_Q_HEADS, HEAD_DIM).sum(axis=0)
        gs["k_gain"][l] = s_kg.reshape(8 * GQA_KV_HEADS, HEAD_DIM).sum(axis=0)
        dx, s_sh1, s_sc1, s_n1 = dh_call(dna, dg, dgates, w["w_in"][l], s["x"], modv[l], _row(small["norm1"][l]), dx1, nlt)
        gw["w_in"][l] = jnp.concatenate(
            [wgrad_call(s["h"], dna, "wgrad_in_na"), wgrad_call(s["h"], dg, "wgrad_in_g"),
             wgrad_call(s["h"], dgates, "wgrad_in_gate")], axis=1)
        gs["norm1"][l] = fold(s_n1).sum(axis=0)
        gs["norm2"][l] = fold(s_n2).sum(axis=0)
        dmod[l] = jnp.stack([fold(s_sh1), fold(s_sc1), fold(s_g1), fold(s_sh2), fold(s_sc2), fold(s_g2)], axis=1)

    gw = {k: jnp.stack(v) for k, v in gw.items()}
    gs = {k: jnp.stack(v) for k, v in gs.items()}
    gs["final_norm"] = d_final.sum(axis=(0, 1))
    return loss_part, dx, gw, jnp.stack(dmod), gs


N_DEV = 8
N_CHIP = 4
ANY = pl.BlockSpec(memory_space=pl.ANY)


def _place():
    x, y, c = lax.axis_index("x"), lax.axis_index("y"), lax.axis_index("c")
    chips = [(1 - x, y), (x, 1 - y), (1 - x, 1 - y)]
    return x, y, c, chips


def small_all_gather(v, name):
    m_per, n = v.shape
    assert m_per % 8 == 0

    def body(x_ref, out_ref, send_sems, recv_sems, local_sem):
        x, y, c, chips = _place()
        me, sibling = (x, y, c), (x, y, 1 - c)

        def rows(px, py, pc):
            return out_ref.at[pl.ds(pl.multiple_of((4 * px + 2 * py + pc) * m_per, 8), m_per), :]

        def copy(k, block, to, src=None):
            return pltpu.make_async_remote_copy(
                src_ref=rows(*block) if src is None else src, dst_ref=rows(*block),
                send_sem=send_sems.at[k], recv_sem=recv_sems.at[k], device_id=to, device_id_type=MESH)

        mine = pltpu.make_async_copy(x_ref, rows(*me), local_sem)
        mine.start()
        first = [copy(0, me, sibling, src=x_ref)]
        first += [copy(1 + j, me, (*chip, c), src=x_ref) for j, chip in enumerate(chips)]
        for cp in first:
            cp.start()
        passed = [copy(4 + j, (*chip, c), sibling) for j, chip in enumerate(chips)]
        for j, chip in enumerate(chips):
            copy(1 + j, (*chip, c), me).wait_recv()
            passed[j].start()
        copy(0, sibling, me).wait_recv()
        for j, chip in enumerate(chips):
            copy(4 + j, (*chip, 1 - c), me).wait_recv()
        for cp in first + passed:
            cp.wait_send()
        mine.wait()

    out = pl.pallas_call(
        body, name=name,
        out_shape=jax.ShapeDtypeStruct((N_DEV * m_per, n), v.dtype),
        in_specs=[pl.BlockSpec(memory_space=pltpu.VMEM)],
        out_specs=pl.BlockSpec(memory_space=pltpu.VMEM),
        scratch_shapes=[pltpu.SemaphoreType.DMA((7,)), pltpu.SemaphoreType.DMA((7,)), pltpu.SemaphoreType.DMA],
    )(v)
    return out.reshape(N_DEV, m_per, n)


def weights_all_gather(packed):
    _, rh, ncol = packed.shape

    def body(in_ref, out_ref, send_sems, recv_sems):
        x, y, c, chips = _place()
        sibling = (x, y, 1 - c)

        def blk(px, py, half):
            return out_ref.at[2 * px + py, half]

        def copy(k, block, to, src=None):
            return pltpu.make_async_remote_copy(
                src_ref=blk(*block) if src is None else src, dst_ref=blk(*block),
                send_sem=send_sems.at[k], recv_sem=recv_sems.at[k], device_id=to, device_id_type=MESH)

        first = [copy(j, (x, y, c), (*chip, c), src=in_ref.at[c]) for j, chip in enumerate(chips)]
        for cp in first:
            cp.start()
        passed = [copy(3 + j, (*chip, c), sibling) for j, chip in enumerate(chips)]
        for j, chip in enumerate(chips):
            copy(j, (*chip, c), (x, y, c)).wait_recv()
            passed[j].start()
        for j, chip in enumerate(chips):
            copy(3 + j, (*chip, 1 - c), (x, y, c)).wait_recv()
        for cp in first + passed:
            cp.wait_send()

    out = pl.pallas_call(
        body, name="weights_all_gather",
        out_shape=jax.ShapeDtypeStruct((N_CHIP, 2, rh, ncol), packed.dtype),
        in_specs=[ANY], out_specs=ANY,
        scratch_shapes=[pltpu.SemaphoreType.DMA((6,)), pltpu.SemaphoreType.DMA((6,))],
    )(packed)
    chip = 2 * lax.axis_index("x") + lax.axis_index("y")
    return lax.dynamic_update_slice(out, packed[None], (chip, 0, 0, 0))


def grads_swap_halves(gp):
    _, _, rh, ncol = gp.shape

    def body(gp_ref, r1_ref, send_sem, recv_sem):
        x, y, c, _ = _place()
        cp = pltpu.make_async_remote_copy(src_ref=gp_ref.at[1 - c], dst_ref=r1_ref, send_sem=send_sem,
                                          recv_sem=recv_sem, device_id=(x, y, 1 - c), device_id_type=MESH)
        cp.start()
        cp.wait()

    return pl.pallas_call(
        body, name="grads_swap_halves", out_shape=jax.ShapeDtypeStruct((N_CHIP, rh, ncol), gp.dtype),
        in_specs=[ANY], out_specs=ANY,
        scratch_shapes=[pltpu.SemaphoreType.DMA, pltpu.SemaphoreType.DMA],
    )(gp)


def grads_to_owner(p1):
    _, rh, ncol = p1.shape

    def body(p1_ref, r2_ref, send_sems, recv_sems):
        x, y, c, chips = _place()
        cps = [pltpu.make_async_remote_copy(src_ref=p1_ref.at[2 * px + py], dst_ref=r2_ref.at[j],
                                            send_sem=send_sems.at[j], recv_sem=recv_sems.at[j],
                                            device_id=(px, py, c), device_id_type=MESH)
               for j, (px, py) in enumerate(chips)]
        for cp in cps:
            cp.start()
        for cp in cps:
            cp.wait()

    return pl.pallas_call(
        body, name="grads_to_owner", out_shape=jax.ShapeDtypeStruct((3, rh, ncol), p1.dtype),
        in_specs=[ANY], out_specs=ANY,
        scratch_shapes=[pltpu.SemaphoreType.DMA((3,)), pltpu.SemaphoreType.DMA((3,))],
    )(p1)


def grads_join_halves(gh):
    rh, ncol = gh.shape

    def body(gh_ref, out_ref, send_sem, recv_sem):
        x, y, c, _ = _place()
        cp = pltpu.make_async_remote_copy(src_ref=gh_ref, dst_ref=out_ref, send_sem=send_sem,
                                          recv_sem=recv_sem, device_id=(x, y, 1 - c), device_id_type=MESH)
        cp.start()
        cp.wait()

    other = pl.pallas_call(
        body, name="grads_join_halves", out_shape=jax.ShapeDtypeStruct((rh, ncol), gh.dtype),
        in_specs=[ANY], out_specs=ANY,
        scratch_shapes=[pltpu.SemaphoreType.DMA, pltpu.SemaphoreType.DMA],
    )(gh)
    c0 = lax.axis_index("c") == 0
    return jnp.stack([jnp.where(c0, gh, other), jnp.where(c0, other, gh)])


PACK_C = 1024


def add_halves_call(gp, r1, c):
    _, n, rh, ncol = gp.shape
    tr = _pick_tile(rh, 512, 16)

    def body(c_ref, a_ref, b_ref, o_ref):
        o_ref[0] = (a_ref[0, 0].astype(F32) + b_ref[0].astype(F32)).astype(BF16)

    return pl.pallas_call(
        body, name="add_halves",
        grid_spec=pltpu.PrefetchScalarGridSpec(
            num_scalar_prefetch=1, grid=(n, rh // tr),
            in_specs=[pl.BlockSpec((1, 1, tr, ncol), lambda k, i, c_ref: (c_ref[0], k, i, 0)),
                      pl.BlockSpec((1, tr, ncol), lambda k, i, c_ref: (k, i, 0))],
            out_specs=pl.BlockSpec((1, tr, ncol), lambda k, i, c_ref: (k, i, 0))),
        out_shape=jax.ShapeDtypeStruct((n, rh, ncol), BF16),
        compiler_params=_cp(("parallel", "parallel")),
    )(jnp.reshape(c, (1,)).astype(jnp.int32), gp, r1)


def sum_owner_call(p1, r2, k):
    _, rh, ncol = p1.shape
    tr = _pick_tile(rh, 512, 16)

    def body(k_ref, a_ref, b0_ref, b1_ref, b2_ref, o_ref):
        o_ref[...] = (((a_ref[0].astype(F32) + b0_ref[0].astype(F32)) + b1_ref[0].astype(F32))
                      + b2_ref[0].astype(F32))

    other = lambda j: pl.BlockSpec((1, tr, ncol), lambda i, k_ref: (j, i, 0))
    return pl.pallas_call(
        body, name="sum_owner",
        grid_spec=pltpu.PrefetchScalarGridSpec(
            num_scalar_prefetch=1, grid=(rh // tr,),
            in_specs=[pl.BlockSpec((1, tr, ncol), lambda i, k_ref: (k_ref[0], i, 0)), other(0), other(1), other(2)],
            out_specs=pl.BlockSpec((tr, ncol), lambda i, k_ref: (i, 0))),
        out_shape=jax.ShapeDtypeStruct((rh, ncol), F32),
        compiler_params=_cp(("parallel",)),
    )(jnp.reshape(k, (1,)).astype(jnp.int32), p1, r2, r2, r2)


def _silu(x):
    return x * _sigmoid(x)


def mod_matmul_call(craw, w_mod):
    L, D, N = w_mod.shape

    def body(c_ref, w_ref, o_ref):
        o_ref[0] = _dot(_silu(c_ref[...]).astype(BF16), w_ref[0].astype(BF16))

    return pl.pallas_call(
        body, name="mod_matmul", grid=(L,),
        in_specs=[pl.BlockSpec((16, D), lambda l: (0, 0)), pl.BlockSpec((1, D, N), lambda l: (l, 0, 0))],
        out_specs=pl.BlockSpec((1, 16, N), lambda l: (l, 0, 0)),
        out_shape=jax.ShapeDtypeStruct((L, 16, N), F32),
        compiler_params=_cp(("parallel",), VMEM_BIG),
    )(craw, w_mod)


def mod_bwd_call(craw, dmod16, w_mod):
    L, D, N = w_mod.shape

    def body(c_ref, d_ref, w_ref, gw_ref, ds_ref):
        l = pl.program_id(0)
        d = d_ref[0].astype(BF16)
        gw_ref[0] = _dot_tn(_silu(c_ref[...]).astype(BF16), d)
        part = _dot_nt(d, w_ref[0].astype(BF16))

        @pl.when(l == 0)
        def _():
            ds_ref[...] = part

        @pl.when(l > 0)
        def _():
            ds_ref[...] += part

    return pl.pallas_call(
        body, name="mod_bwd", grid=(L,),
        in_specs=[pl.BlockSpec((16, D), lambda l: (0, 0)), pl.BlockSpec((1, 16, N), lambda l: (l, 0, 0)),
                  pl.BlockSpec((1, D, N), lambda l: (l, 0, 0))],
        out_specs=[pl.BlockSpec((1, D, N), lambda l: (l, 0, 0)), pl.BlockSpec((16, D), lambda l: (0, 0))],
        out_shape=[jax.ShapeDtypeStruct((L, D, N), F32), jax.ShapeDtypeStruct((16, D), F32)],
        compiler_params=_cp(("arbitrary",), VMEM_BIG),
    )(craw, dmod16, w_mod)


def _adamw(w, g, m, v):
    m2 = ADAM_B1 * m + (1.0 - ADAM_B1) * g
    v2 = ADAM_B2 * v + (1.0 - ADAM_B2) * (g * g)
    m_hat = m2 / (1.0 - ADAM_B1 ** ADAM_STEP)
    v_hat = v2 / (1.0 - ADAM_B2 ** ADAM_STEP)
    delta = -ADAM_LR * (m_hat / (jnp.sqrt(v_hat) + ADAM_EPS) + ADAM_WD * w)
    return delta, m2, v2


def adamw_call(w, g, m, v):
    shape = w.shape
    ncol = shape[-1]
    r = math.prod(shape[:-1])
    tr = _pick_tile(r, 512, 8)
    as2d = lambda t: t.reshape(r, ncol)

    def body(w_ref, g_ref, m_ref, v_ref, d_ref, m2_ref, v2_ref):
        d_ref[...], m2_ref[...], v2_ref[...] = _adamw(w_ref[...], g_ref[...], m_ref[...], v_ref[...])

    spec = pl.BlockSpec((tr, ncol), lambda i: (i, 0))
    outs = pl.pallas_call(
        body, name="adamw", grid=(r // tr,), in_specs=[spec] * 4, out_specs=[spec] * 3,
        out_shape=[jax.ShapeDtypeStruct((r, ncol), F32)] * 3,
        compiler_params=_cp(("parallel",)),
    )(as2d(w), as2d(g), as2d(m), as2d(v))
    return [o.reshape(shape) for o in outs]


def adamw_small_call(parts, w, m, v, silu_bwd=False):
    P, R, C = parts.shape

    def body(p_ref, w_ref, m_ref, v_ref, g_ref, d_ref, m2_ref, v2_ref):
        g = p_ref[0]
        for i in range(1, P):
            g = g + p_ref[i]
        wv = w_ref[...]
        if silu_bwd:
            sg = _sigmoid(wv)
            g = g * (sg * (1.0 + wv * (1.0 - sg)))
        g_ref[...] = g
        d_ref[...], m2_ref[...], v2_ref[...] = _adamw(wv, g, m_ref[...], v_ref[...])

    return pl.pallas_call(
        body, name="adamw_small", out_shape=[jax.ShapeDtypeStruct((R, C), F32)] * 4,
    )(parts, w, m, v)


def sum_parts_call(parts):
    P, R, C = parts.shape

    def body(p_ref, o_ref):
        g = p_ref[0]
        for i in range(1, P):
            g = g + p_ref[i]
        o_ref[...] = g

    return pl.pallas_call(body, name="sum_parts", out_shape=jax.ShapeDtypeStruct((R, C), F32))(parts)


BIG = (("w_in", "col"), ("w_pa", "col"), ("w_pb", "col"), ("w_o", "row"), ("w_ffn_in", "col"), ("w_ffn_out", "row"))
SMALL = ("norm1", "norm2", "final_norm", "na_rpb", "q_gain", "k_gain")


def _pack_shards(shards):
    parts = [shards[n].reshape(2, -1, PACK_C) for n, _ in BIG]
    return jnp.concatenate(parts, axis=1)


def _unpack_shards(packed, shapes):
    out, r0 = {}, 0
    lead = packed.shape[:-3]
    for n, _ in BIG:
        L, K, N = shapes[n]
        rw = (L // 2) * K * N // PACK_C
        out[n] = packed[..., r0:r0 + rw, :].reshape(lead + (L, K, N))
        r0 += rw
    return out


def _whole_from_chips(g, kind):
    _, L, K, N = g.shape
    if kind == "col":
        return g.transpose(1, 2, 0, 3).reshape(L, K, N_CHIP * N)
    return g.transpose(1, 0, 2, 3).reshape(L, N_CHIP * K, N)


def _chips_from_whole(g, kind):
    L, K, N = g.shape
    if kind == "col":
        return g.reshape(L, K, N_CHIP, N // N_CHIP).transpose(2, 0, 1, 3)
    return g.reshape(L, N_CHIP, K // N_CHIP, N).transpose(1, 0, 2, 3)


def _pad_rows(v, rows):
    return jnp.pad(v.reshape(-1), (0, rows * PACK_C - v.size)).reshape(rows, PACK_C)


def kernel(x, c, ctx, c_ctx, w_mod, b_mod, norm1, w_in, na_rpb, q_gain, k_gain, w_pa, w_pb, w_o, norm2, w_ffn_in, w_ffn_out, final_norm, loss_target, m_c_ctx, m_w_mod, m_b_mod, m_norm1, m_w_in, m_na_rpb, m_q_gain, m_k_gain, m_w_pa, m_w_pb, m_w_o, m_norm2, m_w_ffn_in, m_w_ffn_out, m_final_norm, v_c_ctx, v_w_mod, v_b_mod, v_norm1, v_w_in, v_na_rpb, v_q_gain, v_k_gain, v_w_pa, v_w_pb, v_w_o, v_norm2, v_w_ffn_in, v_w_ffn_out, v_final_norm):
    W = dict(c_ctx=c_ctx, w_mod=w_mod, b_mod=b_mod, norm1=norm1, w_in=w_in, na_rpb=na_rpb, q_gain=q_gain, k_gain=k_gain,
             w_pa=w_pa, w_pb=w_pb, w_o=w_o, norm2=norm2, w_ffn_in=w_ffn_in, w_ffn_out=w_ffn_out, final_norm=final_norm)
    M = dict(c_ctx=m_c_ctx, w_mod=m_w_mod, b_mod=m_b_mod, norm1=m_norm1, w_in=m_w_in, na_rpb=m_na_rpb, q_gain=m_q_gain,
             k_gain=m_k_gain, w_pa=m_w_pa, w_pb=m_w_pb, w_o=m_w_o, norm2=m_norm2, w_ffn_in=m_w_ffn_in,
             w_ffn_out=m_w_ffn_out, final_norm=m_final_norm)
    V = dict(c_ctx=v_c_ctx, w_mod=v_w_mod, b_mod=v_b_mod, norm1=v_norm1, w_in=v_w_in, na_rpb=v_na_rpb, q_gain=v_q_gain,
             k_gain=v_k_gain, w_pa=v_w_pa, w_pb=v_w_pb, w_o=v_w_o, norm2=v_norm2, w_ffn_in=v_w_ffn_in,
             w_ffn_out=v_w_ffn_out, final_norm=v_final_norm)
    order = ["c_ctx", "w_mod", "b_mod", "norm1", "w_in", "na_rpb", "q_gain", "k_gain", "w_pa", "w_pb", "w_o", "norm2",
             "w_ffn_in", "w_ffn_out", "final_norm"]
    L, D = norm1.shape
    n_lat = x.shape[1]
    ax, ay, ac = lax.axis_index("x"), lax.axis_index("y"), lax.axis_index("c")
    chip = 2 * ax + ay
    dev = 2 * chip + ac

    shard_shapes = {n: W[n].shape for n, _ in BIG}
    packed = _pack_shards({n: W[n].astype(BF16) for n, _ in BIG})
    gathered = _unpack_shards(weights_all_gather(packed), shard_shapes)
    wfull = {n: _whole_from_chips(gathered[n], kind) for n, kind in BIG}

    c_all = small_all_gather(jnp.pad(c, ((0, 7), (0, 0))), "gather_c")[:, 0, :]
    craw = jnp.pad(c_all, ((0, 8), (0, 0))) + jnp.pad(c_ctx[None, :], ((8, 7), (0, 0)))
    ncol_mod = w_mod.shape[2]
    mod_loc = mod_matmul_call(craw, w_mod)
    mod_all = small_all_gather(mod_loc.reshape(L * 16, ncol_mod), "gather_mod")
    mod_all = mod_all[0::2].reshape(N_CHIP, L, 16, ncol_mod).transpose(1, 2, 0, 3).reshape(L, 16, 6 * D)
    mod_all = mod_all + b_mod[:, None, :]
    mod_mine = lax.dynamic_index_in_dim(mod_all, dev, axis=1, keepdims=False)
    modv = jnp.stack([mod_mine, mod_all[:, 8]], axis=1).reshape(L, 2, 6, D)

    xs = jnp.concatenate([x[0], ctx[0]], axis=0)
    small = dict(norm1=norm1, norm2=norm2, na_rpb=na_rpb, q_gain=q_gain, k_gain=k_gain, final_norm=final_norm)
    loss_part, dx, gw, dmod, gs = device_fwd_bwd(xs, loss_target[0], modv, wfull, small, n_lat)
    loss = lax.psum(jnp.sum(loss_part), ("x", "y", "c"))
    grad_x = dx[:n_lat][None]

    gp = jnp.concatenate(
        [_chips_from_whole(gw[n], kind).reshape(N_CHIP, 2, -1, PACK_C).transpose(1, 0, 2, 3) for n, kind in BIG], axis=2)
    r1 = grads_swap_halves(gp)
    p1 = add_halves_call(gp, r1, ac)
    r2 = grads_to_owner(p1)
    gh = sum_owner_call(p1, r2, chip)
    gbig = _unpack_shards(grads_join_halves(gh), shard_shapes)

    n_mod_rows = L * 6 * D // PACK_C
    small_flat = jnp.concatenate([gs[n].reshape(-1) for n in SMALL])
    n_small_rows = -(-small_flat.size // (8 * PACK_C)) * 8
    g1 = jnp.concatenate([dmod[:, 0].reshape(n_mod_rows, PACK_C), dmod[:, 1].reshape(n_mod_rows, PACK_C),
                          _pad_rows(small_flat, n_small_rows)], axis=0)
    g1 = small_all_gather(g1, "gather_small_grads")
    dm_lat = g1[:, :n_mod_rows].reshape(N_DEV, L, 6 * D)
    dm_ctx_parts = g1[:, n_mod_rows:2 * n_mod_rows]
    dm_ctx = sum_parts_call(dm_ctx_parts).reshape(L, 1, 6 * D)
    dmod16 = (jnp.pad(dm_lat.transpose(1, 0, 2), ((0, 0), (0, 8), (0, 0)))
              + jnp.pad(dm_ctx, ((0, 0), (8, 7), (0, 0))))
    dmod16_loc = lax.dynamic_slice_in_dim(dmod16, chip * ncol_mod, ncol_mod, axis=2)
    g_wmod, dsilu_part = mod_bwd_call(craw, dmod16_loc, w_mod)
    ds_all = small_all_gather(dsilu_part[8:16], "gather_dsilu")
    ds_parts = ds_all[0::2, 0:1, :]
    ds_parts = jnp.pad(ds_parts, ((0, 0), (0, 7), (0, 0)))

    out_g, out_d, out_m, out_v = {}, {}, {}, {}
    for n, _ in BIG:
        out_g[n] = gbig[n]
        out_d[n], out_m[n], out_v[n] = adamw_call(W[n], gbig[n], M[n], V[n])
    out_g["w_mod"] = g_wmod
    out_d["w_mod"], out_m["w_mod"], out_v["w_mod"] = adamw_call(w_mod, g_wmod, m_w_mod, v_w_mod)

    pack_small = lambda t: _pad_rows(jnp.concatenate([t[n].reshape(-1) for n in SMALL]), n_small_rows)
    res = adamw_small_call(g1[:, 2 * n_mod_rows:], pack_small(W), pack_small(M), pack_small(V))
    off = 0
    for n in SMALL:
        sz = W[n].size
        for dst, r in zip((out_g, out_d, out_m, out_v), res):
            dst[n] = r.reshape(-1)[off:off + sz].reshape(W[n].shape)
        off += sz
    bparts = jnp.concatenate([g1[:, :n_mod_rows], dm_ctx_parts], axis=0)
    as_rows = lambda t: t.reshape(n_mod_rows, PACK_C)
    res = adamw_small_call(bparts, as_rows(b_mod), as_rows(m_b_mod), as_rows(v_b_mod))
    for dst, r in zip((out_g, out_d, out_m, out_v), res):
        dst["b_mod"] = r.reshape(b_mod.shape)
    row8 = lambda t: jnp.pad(t[None, :], ((0, 7), (0, 0)))
    res = adamw_small_call(ds_parts, row8(c_ctx), row8(m_c_ctx), row8(v_c_ctx), silu_bwd=True)
    for dst, r in zip((out_g, out_d, out_m, out_v), res):
        dst["c_ctx"] = r[0]

    return (loss, grad_x, *[out_g[n] for n in order], *[out_d[n] for n in order],
            *[out_m[n] for n in order], *[out_v[n] for n in order])
```

```python
import functools
import math

import numpy as np
import jax
import jax.numpy as jnp
from jax import lax
from jax.experimental import pallas as pl
from jax.experimental.pallas import tpu as pltpu

F32 = jnp.float32
BF16 = jnp.bfloat16
MESH = pl.DeviceIdType.MESH

HEAD_DIM = 64
NA_HEADS = 8
GQA_Q_HEADS = 8
GQA_KV_HEADS = 2
GRID_W = 64
NA_WIN_H = 8
NA_WIN_W = 16
ROPE_THETA = 10000.0
EPS = 1e-6
SCALE = HEAD_DIM ** -0.5
NEG = -1e30

ADAM_LR = 0.001
ADAM_B1 = 0.9
ADAM_B2 = 0.999
ADAM_EPS = 1e-08
ADAM_WD = 0.01
ADAM_STEP = 10

TM = 256
NA_ROWS = 4
NA_KROWS = 12
LANES = 128
VMEM_BIG = 56 * 1024 * 1024


def _cp(sem, vmem=None):
    return pltpu.CompilerParams(dimension_semantics=sem, vmem_limit_bytes=vmem)


def _dot(a, b):
    return jnp.dot(a, b, preferred_element_type=F32)


def _dot_nt(a, b):
    return lax.dot_general(a, b, (((1,), (1,)), ((), ())), preferred_element_type=F32)


def _dot_tn(a, b):
    return lax.dot_general(a, b, (((0,), (0,)), ((), ())), preferred_element_type=F32)


def _colsum8(v):
    tm, d = v.shape
    return v.reshape(tm // 8, 8, d).sum(axis=0)


def _acc(ref, val, first):
    @pl.when(first)
    def _():
        ref[0] = val

    @pl.when(jnp.logical_not(first))
    def _():
        ref[0] += val


def _rms_r(x):
    return lax.rsqrt(jnp.mean(x * x, axis=-1, keepdims=True) + EPS)


def _norm_mod(x, nw, sh, sc):
    return (x * _rms_r(x) * nw) * (1.0 + sc) + sh


def _norm_mod_bwd(dh, x, nw, sc):
    r = _rms_r(x)
    xn = x * r
    dxn = dh * (nw * (1.0 + sc))
    dx = r * (dxn - xn * jnp.mean(dxn * xn, axis=-1, keepdims=True))
    return dx, dh, dh * (xn * nw), dh * ((1.0 + sc) * xn)


def _sigmoid(x):
    return 1.0 / (1.0 + jnp.exp(-x))


def _group_of(i, n_lat_tiles):
    return jnp.where(i >= n_lat_tiles, 1, 0)


def _lane_half(shape):
    return (lax.broadcasted_iota(jnp.int32, shape, len(shape) - 1) % LANES) // HEAD_DIM


def _lane_lt64(shape):
    return _lane_half(shape) == 0


def _roll64(x):
    return pltpu.roll(x, HEAD_DIM, x.ndim - 1)


def norm_mod_call(x, nw, modv, n_lat_tiles, which):
    T, D = x.shape

    def body(x_ref, nw_ref, mod_ref, h_ref):
        sh = mod_ref[0, 3 * which:3 * which + 1, :]
        sc = mod_ref[0, 3 * which + 1:3 * which + 2, :]
        h_ref[...] = _norm_mod(x_ref[...], nw_ref[...], sh, sc).astype(BF16)

    return pl.pallas_call(
        body, name="norm_mod", grid=(T // TM,),
        in_specs=[pl.BlockSpec((TM, D), lambda i: (i, 0)),
                  pl.BlockSpec((1, D), lambda i: (0, 0)),
                  pl.BlockSpec((1, 6, D), lambda i: (_group_of(i, n_lat_tiles), 0, 0))],
        out_specs=pl.BlockSpec((TM, D), lambda i: (i, 0)),
        out_shape=jax.ShapeDtypeStruct((T, D), BF16),
        compiler_params=_cp(("parallel",)),
    )(x, nw, modv)


def qkv_call(h, w_in):
    T, D = h.shape
    N = w_in.shape[1]
    n_na = 3 * NA_HEADS * HEAD_DIM
    n_g = (GQA_Q_HEADS + 2 * GQA_KV_HEADS) * HEAD_DIM
    n_gate = N - n_na - n_g

    def body(h_ref, w_ref, na_ref, g_ref, gate_ref):
        acc = _dot(h_ref[...], w_ref[...])
        na_ref[...] = acc[:, :n_na].astype(BF16)
        g_ref[...] = acc[:, n_na:n_na + n_g].astype(BF16)
        gate_ref[...] = acc[:, n_na + n_g:].astype(BF16)

    return pl.pallas_call(
        body, name="qkv", grid=(T // TM,),
        in_specs=[pl.BlockSpec((TM, D), lambda i: (i, 0)),
                  pl.BlockSpec((D, N), lambda i: (0, 0))],
        out_specs=[pl.BlockSpec((TM, n_na), lambda i: (i, 0)),
                   pl.BlockSpec((TM, n_g), lambda i: (i, 0)),
                   pl.BlockSpec((TM, n_gate), lambda i: (i, 0))],
        out_shape=[jax.ShapeDtypeStruct((T, n_na), BF16),
                   jax.ShapeDtypeStruct((T, n_g), BF16),
                   jax.ShapeDtypeStruct((T, n_gate), BF16)],
        compiler_params=_cp(("parallel",), VMEM_BIG),
    )(h, w_in)


def merge_call(ya, yb, gates, w_pa, w_pb):
    T, Ka = ya.shape
    D = w_pa.shape[1]

    def body(ya_ref, yb_ref, ga_ref, gb_ref, wa_ref, wb_ref, m_ref, pa_ref, pb_ref):
        pa = _dot(ya_ref[...], wa_ref[...])
        pb = _dot(yb_ref[...], wb_ref[...])
        m = _sigmoid(ga_ref[...].astype(F32)) * pa + _sigmoid(gb_ref[...].astype(F32)) * pb
        m_ref[...] = m.astype(BF16)
        pa_ref[...] = pa.astype(BF16)
        pb_ref[...] = pb.astype(BF16)

    row = lambda w: pl.BlockSpec((TM, w), lambda i: (i, 0))
    return pl.pallas_call(
        body, name="merge", grid=(T // TM,),
        in_specs=[row(Ka), row(Ka),
                  pl.BlockSpec((TM, D), lambda i: (i, 0)), pl.BlockSpec((TM, D), lambda i: (i, 1)),
                  pl.BlockSpec((Ka, D), lambda i: (0, 0)), pl.BlockSpec((Ka, D), lambda i: (0, 0))],
        out_specs=[row(D), row(D), row(D)],
        out_shape=[jax.ShapeDtypeStruct((T, D), BF16)] * 3,
        compiler_params=_cp(("parallel",)),
    )(ya, yb, gates, gates, w_pa, w_pb)


def wo_call(merged, w_o, x, modv, nw2, n_lat_tiles):
    T, D = x.shape

    def body(m_ref, w_ref, x_ref, mod_ref, nw_ref, x1_ref, o_ref, h2_ref):
        o = _dot(m_ref[...], w_ref[...])
        x1 = x_ref[...] + mod_ref[0, 2:3, :] * o
        x1_ref[...] = x1
        o_ref[...] = o.astype(BF16)
        h2_ref[...] = _norm_mod(x1, nw_ref[...], mod_ref[0, 3:4, :], mod_ref[0, 4:5, :]).astype(BF16)

    row = pl.BlockSpec((TM, D), lambda i: (i, 0))
    return pl.pallas_call(
        body, name="wo", grid=(T // TM,),
        in_specs=[row, pl.BlockSpec((D, D), lambda i: (0, 0)), row,
                  pl.BlockSpec((1, 6, D), lambda i: (_group_of(i, n_lat_tiles), 0, 0)),
                  pl.BlockSpec((1, D), lambda i: (0, 0))],
        out_specs=[row, row, row],
        out_shape=[jax.ShapeDtypeStruct((T, D), F32), jax.ShapeDtypeStruct((T, D), BF16),
                   jax.ShapeDtypeStruct((T, D), BF16)],
        compiler_params=_cp(("parallel",)),
    )(merged, w_o, x, modv, nw2)


def ffn_in_call(h2, w_ffn_in, n_col_tiles=2):
    T, D = h2.shape
    F = w_ffn_in.shape[1] // 2
    tn = F // n_col_tiles

    def body(h_ref, wa_ref, wu_ref, act_ref, a_ref, u_ref):
        h = h_ref[...]
        a = _dot(h, wa_ref[...])
        u = _dot(h, wu_ref[...])
        act_ref[...] = (a * _sigmoid(a) * u).astype(BF16)
        a_ref[...] = a.astype(BF16)
        u_ref[...] = u.astype(BF16)

    blk = pl.BlockSpec((TM, tn), lambda j, i: (i, j))
    return pl.pallas_call(
        body, name="ffn_in", grid=(n_col_tiles, T // TM),
        in_specs=[pl.BlockSpec((TM, D), lambda j, i: (i, 0)),
                  pl.BlockSpec((D, tn), lambda j, i: (0, j)),
                  pl.BlockSpec((D, tn), lambda j, i: (0, n_col_tiles + j))],
        out_specs=[blk, blk, blk],
        out_shape=[jax.ShapeDtypeStruct((T, F), BF16)] * 3,
        compiler_params=_cp(("parallel", "parallel"), VMEM_BIG),
    )(h2, w_ffn_in, w_ffn_in)


def ffn_out_call(act, w_ffn_out, x1, modv, n_lat_tiles, next_nw=None, next_modv=None):
    T, D = x1.shape
    F = act.shape[1]
    with_next = next_nw is not None

    def body(*refs):
        if with_next:
            a_ref, w_ref, x_ref, mod_ref, nw_ref, nmod_ref, x2_ref, f_ref, hn_ref = refs
        else:
            a_ref, w_ref, x_ref, mod_ref, x2_ref, f_ref = refs
        f = _dot(a_ref[...], w_ref[...])
        x2 = x_ref[...] + mod_ref[0, 5:6, :] * f
        x2_ref[...] = x2
        f_ref[...] = f.astype(BF16)
        if with_next:
            hn_ref[...] = _norm_mod(x2, nw_ref[...], nmod_ref[0, 0:1, :], nmod_ref[0, 1:2, :]).astype(BF16)

    row = pl.BlockSpec((TM, D), lambda i: (i, 0))
    modspec = pl.BlockSpec((1, 6, D), lambda i: (_group_of(i, n_lat_tiles), 0, 0))
    in_specs = [pl.BlockSpec((TM, F), lambda i: (i, 0)), pl.BlockSpec((F, D), lambda i: (0, 0)), row, modspec]
    args = [act, w_ffn_out, x1, modv]
    out_specs = [row, row]
    out_shape = [jax.ShapeDtypeStruct((T, D), F32), jax.ShapeDtypeStruct((T, D), BF16)]
    if with_next:
        in_specs += [pl.BlockSpec((1, D), lambda i: (0, 0)), modspec]
        args += [next_nw, next_modv]
        out_specs.append(row)
        out_shape.append(jax.ShapeDtypeStruct((T, D), BF16))
    return pl.pallas_call(
        body, name="ffn_out", grid=(T // TM,), in_specs=in_specs, out_specs=out_specs, out_shape=out_shape,
        compiler_params=_cp(("parallel",), VMEM_BIG),
    )(*args)


def loss_call(x, nw, target, n_lat_tiles):
    T, D = x.shape

    def body(x_ref, nw_ref, t_ref, dx_ref, dw_ref, ls_ref):
        i = pl.program_id(0)

        @pl.when(i < n_lat_tiles)
        def _():
            xv = x_ref[...]
            nw_v = nw_ref[...]
            r = _rms_r(xv)
            xn = xv * r
            e = xn * nw_v - t_ref[...]
            dy = e * (1.0 / D)
            dxn = dy * nw_v
            dx_ref[...] = r * (dxn - xn * jnp.mean(dxn * xn, axis=-1, keepdims=True))
            _acc(dw_ref, _colsum8(dy * xn), i == 0)
            _acc(ls_ref, _colsum8(e * e * (0.5 / D)), i == 0)

        @pl.when(i >= n_lat_tiles)
        def _():
            dx_ref[...] = jnp.zeros((TM, D), F32)

    stat = pl.BlockSpec((1, 8, D), lambda i: (0, 0, 0))
    return pl.pallas_call(
        body, name="loss", grid=(T // TM,),
        in_specs=[pl.BlockSpec((TM, D), lambda i: (i, 0)), pl.BlockSpec((1, D), lambda i: (0, 0)),
                  pl.BlockSpec((TM, D), lambda i: (jnp.minimum(i, n_lat_tiles - 1), 0))],
        out_specs=[pl.BlockSpec((TM, D), lambda i: (i, 0)), stat, stat],
        out_shape=[jax.ShapeDtypeStruct((T, D), F32), jax.ShapeDtypeStruct((1, 8, D), F32),
                   jax.ShapeDtypeStruct((1, 8, D), F32)],
        compiler_params=_cp(("arbitrary",)),
    )(x, nw, target)


def dact_call(dx2, modv, fsave, w_ffn_out, a, u, n_lat_tiles):
    T, D = dx2.shape
    F = a.shape[1]

    def body(dx_ref, mod_ref, f_ref, w_ref, a_ref, u_ref, da_ref, du_ref, df_ref, dg_ref):
        i = pl.program_id(0)
        dx = dx_ref[...]
        df = (dx * mod_ref[0, 5:6, :]).astype(BF16)
        df_ref[...] = df
        dact = _dot_nt(df, w_ref[...])
        av = a_ref[...].astype(F32)
        uv = u_ref[...].astype(F32)
        sg = _sigmoid(av)
        da_ref[...] = (dact * uv * (sg * (1.0 + av * (1.0 - sg)))).astype(BF16)
        du_ref[...] = (dact * (av * sg)).astype(BF16)
        _acc(dg_ref, _colsum8(dx * f_ref[...].astype(F32)), (i == 0) | (i == n_lat_tiles))

    row = pl.BlockSpec((TM, D), lambda i: (i, 0))
    wide = pl.BlockSpec((TM, F), lambda i: (i, 0))
    grp = lambda i: (_group_of(i, n_lat_tiles), 0, 0)
    return pl.pallas_call(
        body, name="dact", grid=(T // TM,),
        in_specs=[row, pl.BlockSpec((1, 6, D), grp), row, pl.BlockSpec((F, D), lambda i: (0, 0)), wide, wide],
        out_specs=[wide, wide, row, pl.BlockSpec((1, 8, D), grp)],
        out_shape=[jax.ShapeDtypeStruct((T, F), BF16), jax.ShapeDtypeStruct((T, F), BF16),
                   jax.ShapeDtypeStruct((T, D), BF16), jax.ShapeDtypeStruct((2, 8, D), F32)],
        compiler_params=_cp(("arbitrary",), VMEM_BIG),
    )(dx2, modv, fsave, w_ffn_out, a, u)


def _norm_bwd_tail(dh, x_ref, mod_ref, nw_ref, dres_ref, dx_ref, s_sh, s_sc, s_w, which, first):
    sc = mod_ref[0, 3 * which + 1:3 * which + 2, :]
    dxn, t_sh, t_sc, t_w = _norm_mod_bwd(dh, x_ref[...], nw_ref[...], sc)
    dx_ref[...] = dres_ref[...] + dxn
    _acc(s_sh, _colsum8(t_sh), first)
    _acc(s_sc, _colsum8(t_sc), first)
    _acc(s_w, _colsum8(t_w), first)


def dh2_call(da, du, w_ffn_in, x1, modv, nw2, dx2, n_lat_tiles):
    T, D = x1.shape
    F = da.shape[1]

    def body(da_ref, du_ref, wa_ref, wu_ref, x_ref, mod_ref, nw_ref, dres_ref, dx_ref, s_sh, s_sc, s_w):
        i = pl.program_id(0)
        dh = _dot_nt(da_ref[...], wa_ref[...]) + _dot_nt(du_ref[...], wu_ref[...])
        _norm_bwd_tail(dh, x_ref, mod_ref, nw_ref, dres_ref, dx_ref, s_sh, s_sc, s_w, 1,
                       (i == 0) | (i == n_lat_tiles))

    row = pl.BlockSpec((TM, D), lambda i: (i, 0))
    wide = pl.BlockSpec((TM, F), lambda i: (i, 0))
    grp = lambda i: (_group_of(i, n_lat_tiles), 0, 0)
    stat = pl.BlockSpec((1, 8, D), grp)
    return pl.pallas_call(
        body, name="dh2", grid=(T // TM,),
        in_specs=[wide, wide, pl.BlockSpec((D, F), lambda i: (0, 0)), pl.BlockSpec((D, F), lambda i: (0, 1)),
                  row, pl.BlockSpec((1, 6, D), grp), pl.BlockSpec((1, D), lambda i: (0, 0)), row],
        out_specs=[row, stat, stat, stat],
        out_shape=[jax.ShapeDtypeStruct((T, D), F32)] + [jax.ShapeDtypeStruct((2, 8, D), F32)] * 3,
        compiler_params=_cp(("arbitrary",), VMEM_BIG),
    )(da, du, w_ffn_in, w_ffn_in, x1, modv, nw2, dx2)


def dmerged_call(dx1, modv, osave, w_o, pa, pb, gates, n_lat_tiles):
    T, D = dx1.shape

    def body(dx_ref, mod_ref, o_ref, w_ref, pa_ref, pb_ref, ga_ref, gb_ref,
             dpa_ref, dpb_ref, dgate_ref, do_ref, dg_ref):
        i = pl.program_id(0)
        dx = dx_ref[...]
        do = (dx * mod_ref[0, 2:3, :]).astype(BF16)
        do_ref[...] = do
        dm = _dot_nt(do, w_ref[...])
        sa = _sigmoid(ga_ref[...].astype(F32))
        sb = _sigmoid(gb_ref[...].astype(F32))
        dpa_ref[...] = (dm * sa).astype(BF16)
        dpb_ref[...] = (dm * sb).astype(BF16)
        dgate_ref[:, :D] = (dm * pa_ref[...].astype(F32) * (sa * (1.0 - sa))).astype(BF16)
        dgate_ref[:, D:] = (dm * pb_ref[...].astype(F32) * (sb * (1.0 - sb))).astype(BF16)
        _acc(dg_ref, _colsum8(dx * o_ref[...].astype(F32)), (i == 0) | (i == n_lat_tiles))

    row = pl.BlockSpec((TM, D), lambda i: (i, 0))
    grp = lambda i: (_group_of(i, n_lat_tiles), 0, 0)
    return pl.pallas_call(
        body, name="dmerged", grid=(T // TM,),
        in_specs=[row, pl.BlockSpec((1, 6, D), grp), row, pl.BlockSpec((D, D), lambda i: (0, 0)), row, row,
                  pl.BlockSpec((TM, D), lambda i: (i, 0)), pl.BlockSpec((TM, D), lambda i: (i, 1))],
        out_specs=[row, row, pl.BlockSpec((TM, 2 * D), lambda i: (i, 0)), row, pl.BlockSpec((1, 8, D), grp)],
        out_shape=[jax.ShapeDtypeStruct((T, D), BF16), jax.ShapeDtypeStruct((T, D), BF16),
                   jax.ShapeDtypeStruct((T, 2 * D), BF16), jax.ShapeDtypeStruct((T, D), BF16),
                   jax.ShapeDtypeStruct((2, 8, D), F32)],
        compiler_params=_cp(("arbitrary",)),
    )(dx1, modv, osave, w_o, pa, pb, gates, gates)


def dy_call(dpa, dpb, w_pa, w_pb):
    T, D = dpa.shape
    K = w_pa.shape[0]

    def body(a_ref, b_ref, wa_ref, wb_ref, ya_ref, yb_ref):
        ya_ref[...] = _dot_nt(a_ref[...], wa_ref[...]).astype(BF16)
        yb_ref[...] = _dot_nt(b_ref[...], wb_ref[...]).astype(BF16)

    row = pl.BlockSpec((TM, D), lambda i: (i, 0))
    w = pl.BlockSpec((K, D), lambda i: (0, 0))
    out = pl.BlockSpec((TM, K), lambda i: (i, 0))
    return pl.pallas_call(
        body, name="dy", grid=(T // TM,), in_specs=[row, row, w, w], out_specs=[out, out],
        out_shape=[jax.ShapeDtypeStruct((T, K), BF16)] * 2,
        compiler_params=_cp(("parallel",)),
    )(dpa, dpb, w_pa, w_pb)


def dh_call(dna, dg, dgates, w_in, x, modv, nw1, dx1, n_lat_tiles):
    T, D = x.shape
    n1, n2, n3 = dna.shape[1], dg.shape[1], dgates.shape[1]
    N = n1 + n2 + n3

    def body(a_ref, b_ref, c_ref, w_ref, x_ref, mod_ref, nw_ref, dres_ref, dx_ref, s_sh, s_sc, s_w):
        i = pl.program_id(0)
        dh = (_dot_nt(a_ref[...], w_ref[:, :n1]) + _dot_nt(b_ref[...], w_ref[:, n1:n1 + n2])
              + _dot_nt(c_ref[...], w_ref[:, n1 + n2:]))
        _norm_bwd_tail(dh, x_ref, mod_ref, nw_ref, dres_ref, dx_ref, s_sh, s_sc, s_w, 0,
                       (i == 0) | (i == n_lat_tiles))

    row = pl.BlockSpec((TM, D), lambda i: (i, 0))
    grp = lambda i: (_group_of(i, n_lat_tiles), 0, 0)
    stat = pl.BlockSpec((1, 8, D), grp)
    return pl.pallas_call(
        body, name="dh", grid=(T // TM,),
        in_specs=[pl.BlockSpec((TM, n1), lambda i: (i, 0)), pl.BlockSpec((TM, n2), lambda i: (i, 0)),
                  pl.BlockSpec((TM, n3), lambda i: (i, 0)), pl.BlockSpec((D, N), lambda i: (0, 0)),
                  row, pl.BlockSpec((1, 6, D), grp), pl.BlockSpec((1, D), lambda i: (0, 0)), row],
        out_specs=[row, stat, stat, stat],
        out_shape=[jax.ShapeDtypeStruct((T, D), F32)] + [jax.ShapeDtypeStruct((2, 8, D), F32)] * 3,
        compiler_params=_cp(("arbitrary",), VMEM_BIG),
    )(dna, dg, dgates, w_in, x, modv, nw1, dx1)


def _pick_tile(n, cap, mult):
    best = None
    for d in range(mult, min(n, cap) + 1, mult):
        if n % d == 0:
            best = d
    assert best is not None, (n, cap, mult)
    return best


def wgrad_call(a, b, name):
    T, K = a.shape
    N = b.shape[1]
    tt = _pick_tile(T, 768, 128)
    tk = K if K <= 1024 else _pick_tile(K, 1408, 128)
    tn = N if N <= 2304 else _pick_tile(N, 2304, 128)
    nt = T // tt

    def body(a_ref, b_ref, o_ref, acc_ref):
        t = pl.program_id(2)
        p = _dot_tn(a_ref[...], b_ref[...])

        @pl.when(t == 0)
        def _():
            acc_ref[...] = p

        @pl.when(t > 0)
        def _():
            acc_ref[...] += p

        @pl.when(t == nt - 1)
        def _():
            o_ref[...] = acc_ref[...].astype(BF16)

    return pl.pallas_call(
        body, name=name, grid=(K // tk, N // tn, nt),
        in_specs=[pl.BlockSpec((tt, tk), lambda i, j, t: (t, i)), pl.BlockSpec((tt, tn), lambda i, j, t: (t, j))],
        out_specs=pl.BlockSpec((tk, tn), lambda i, j, t: (i, j)),
        out_shape=jax.ShapeDtypeStruct((K, N), BF16),
        scratch_shapes=[pltpu.VMEM((tk, tn), F32)],
        compiler_params=_cp(("parallel", "parallel", "arbitrary"), VMEM_BIG),
    )(a, b)


def _head_mean(v):
    n = v.shape[-1]
    r = lax.broadcasted_iota(jnp.int32, (n, n), 0) // HEAD_DIM
    c = lax.broadcasted_iota(jnp.int32, (n, n), 1) // HEAD_DIM
    bd = jnp.where(r == c, 1.0 / HEAD_DIM, 0.0).astype(BF16)
    hi = v.astype(BF16)
    lo = (v - hi.astype(F32)).astype(BF16)
    return _dot(hi, bd) + _dot(lo, bd)


def _swap_pairs(x):
    n = x.shape[-1]
    even = lax.broadcasted_iota(jnp.int32, x.shape, x.ndim - 1) % 2 == 0
    return jnp.where(even, pltpu.roll(x, n - 1, x.ndim - 1), pltpu.roll(x, 1, x.ndim - 1))


def _tile_lanes(t, n):
    return jnp.tile(t, (1, n // t.shape[-1]))


def gqa_prep_call(gqkv, qgain, kgain, cos_t, sin_t):
    T = gqkv.shape[0]
    nq = GQA_Q_HEADS * HEAD_DIM
    nk = GQA_KV_HEADS * HEAD_DIM
    rep = GQA_Q_HEADS // GQA_KV_HEADS

    def body(g_ref, qg_ref, kg_ref, c_ref, s_ref, qp_ref, kp_ref):
        cos_v, sin_v = c_ref[...], s_ref[...]
        xq = g_ref[:, :nq].astype(F32)
        yq = xq * lax.rsqrt(_head_mean(xq * xq) + EPS) * qg_ref[...]
        yq = (yq * _tile_lanes(cos_v, nq) + _swap_pairs(yq) * _tile_lanes(sin_v, nq)) * SCALE
        lo = _lane_lt64((TM, LANES))
        for p in range(GQA_Q_HEADS // 2):
            g = (2 * p) // rep
            chunk = yq[:, LANES * p:LANES * (p + 1)]
            rolled = _roll64(chunk)
            if g == 0:
                first, second = jnp.where(lo, chunk, 0.0), jnp.where(lo, rolled, 0.0)
            else:
                first, second = jnp.where(lo, 0.0, rolled), jnp.where(lo, 0.0, chunk)
            qp_ref[:, 2 * LANES * p:2 * LANES * p + LANES] = first.astype(BF16)
            qp_ref[:, 2 * LANES * p + LANES:2 * LANES * (p + 1)] = second.astype(BF16)
        xk = g_ref[:, nq:nq + nk].astype(F32)
        yk = xk * lax.rsqrt(_head_mean(xk * xk) + EPS) * kg_ref[...]
        kp_ref[...] = (yk * cos_v + _swap_pairs(yk) * sin_v).astype(BF16)

    return pl.pallas_call(
        body, name="gqa_prep", grid=(T // TM,),
        in_specs=[pl.BlockSpec((TM, gqkv.shape[1]), lambda i: (i, 0)),
                  pl.BlockSpec((1, nq), lambda i: (0, 0)), pl.BlockSpec((1, nk), lambda i: (0, 0)),
                  pl.BlockSpec((TM, nk), lambda i: (i, 0)), pl.BlockSpec((TM, nk), lambda i: (i, 0))],
        out_specs=[pl.BlockSpec((TM, GQA_Q_HEADS * LANES), lambda i: (i, 0)), pl.BlockSpec((TM, nk), lambda i: (i, 0))],
        out_shape=[jax.ShapeDtypeStruct((T, GQA_Q_HEADS * LANES), BF16), jax.ShapeDtypeStruct((T, nk), BF16)],
        compiler_params=_cp(("parallel",)),
    )(gqkv, qgain, kgain, cos_t, sin_t)


def gqa_prep_bwd_call(gqkv, qgain, kgain, cos_t, sin_t, dqp, dkp, dv):
    T = gqkv.shape[0]
    nq = GQA_Q_HEADS * HEAD_DIM
    nk = GQA_KV_HEADS * HEAD_DIM
    rep = GQA_Q_HEADS // GQA_KV_HEADS

    def norm_rope_bwd(x, gain, dy, cos_v, sin_v):
        dyn = dy * cos_v - _swap_pairs(dy) * sin_v
        r = lax.rsqrt(_head_mean(x * x) + EPS)
        xn = x * r
        dxn = dyn * gain
        dx = r * (dxn - xn * _head_mean(dxn * xn))
        return dx, dyn * xn

    def body(g_ref, qg_ref, kg_ref, c_ref, s_ref, dqp_ref, dkp_ref, dv_ref, dg_ref, sq_ref, sk_ref):
        i = pl.program_id(0)
        cos_v, sin_v = c_ref[...], s_ref[...]
        lo = _lane_lt64((TM, LANES))
        chunks = []
        for p in range(GQA_Q_HEADS // 2):
            g = (2 * p) // rep
            da = dqp_ref[:, 2 * LANES * p:2 * LANES * p + LANES]
            db = dqp_ref[:, 2 * LANES * p + LANES:2 * LANES * (p + 1)]
            if g == 0:
                chunks.append(jnp.where(lo, da, _roll64(db)))
            else:
                chunks.append(jnp.where(lo, _roll64(da), db))
        dyq = jnp.concatenate(chunks, axis=1) * SCALE
        dxq, gq_term = norm_rope_bwd(g_ref[:, :nq].astype(F32), qg_ref[...], dyq,
                                     _tile_lanes(cos_v, nq), _tile_lanes(sin_v, nq))
        dxk, gk_term = norm_rope_bwd(g_ref[:, nq:nq + nk].astype(F32), kg_ref[...], dkp_ref[...], cos_v, sin_v)
        dg_ref[:, :nq] = dxq.astype(BF16)
        dg_ref[:, nq:nq + nk] = dxk.astype(BF16)
        dg_ref[:, nq + nk:] = dv_ref[...].astype(BF16)
        _acc(sq_ref, _colsum8(gq_term), i == 0)
        _acc(sk_ref, _colsum8(gk_term), i == 0)

    W = gqkv.shape[1]
    return pl.pallas_call(
        body, name="gqa_prep_bwd", grid=(T // TM,),
        in_specs=[pl.BlockSpec((TM, W), lambda i: (i, 0)),
                  pl.BlockSpec((1, nq), lambda i: (0, 0)), pl.BlockSpec((1, nk), lambda i: (0, 0)),
                  pl.BlockSpec((TM, nk), lambda i: (i, 0)), pl.BlockSpec((TM, nk), lambda i: (i, 0)),
                  pl.BlockSpec((TM, GQA_Q_HEADS * LANES), lambda i: (i, 0)),
                  pl.BlockSpec((TM, nk), lambda i: (i, 0)), pl.BlockSpec((TM, nk), lambda i: (i, 0))],
        out_specs=[pl.BlockSpec((TM, W), lambda i: (i, 0)),
                   pl.BlockSpec((1, 8, nq), lambda i: (0, 0, 0)), pl.BlockSpec((1, 8, nk), lambda i: (0, 0, 0))],
        out_shape=[jax.ShapeDtypeStruct((T, W), BF16), jax.ShapeDtypeStruct((1, 8, nq), F32),
                   jax.ShapeDtypeStruct((1, 8, nk), F32)],
        compiler_params=_cp(("arbitrary",)),
    )(gqkv, qgain, kgain, cos_t, sin_t, dqp, dkp, dv)


GQ_TQ = 256
GQ_TK = 256


def _chunk_offset(j):
    return j * GQ_TK if isinstance(j, int) else pl.multiple_of(j * GQ_TK, GQ_TK)


GQ_CHUNKS_PER_TRIP = 8


def _sweep_keys(chunk, is_latent_query, n_lat_k, nk):
    per_trip = math.gcd(n_lat_k, GQ_CHUNKS_PER_TRIP)

    @pl.when(is_latent_query)
    def _():
        def trip(jj, carry):
            for t in range(per_trip):
                chunk(per_trip * jj + t)
            return carry

        lax.fori_loop(0, n_lat_k // per_trip, trip, 0)

    for j in range(n_lat_k, nk):
        chunk(j)


def _gqa_place(chunk, half, g):
    gv = jnp.broadcast_to(g, chunk.shape)
    placed = jnp.where(gv == half, chunk, _roll64(chunk))
    return jnp.where(_lane_half(chunk.shape) == gv, placed, jnp.zeros_like(placed))


GQA_SMALL_SCORE = 20.0


def gqa_scores_small(q_gain, k_gain):
    bound = SCALE * HEAD_DIM * jnp.max(jnp.abs(q_gain)) * jnp.max(jnp.abs(k_gain))
    return (bound <= GQA_SMALL_SCORE).astype(jnp.int32).reshape(1)


def gqa_fwd_call(qp, kp, gqkv, n_lat, scores_small, gather=None):
    T = qp.shape[0]
    rep = GQA_Q_HEADS // GQA_KV_HEADS
    nq = T // GQ_TQ
    nk = T // GQ_TK
    n_lat_q = n_lat // GQ_TQ
    n_lat_k = n_lat // GQ_TK
    v_blk = (GQA_Q_HEADS + GQA_KV_HEADS) * HEAD_DIM // LANES
    R = rep * GQ_TQ

    def body(*refs):
        if gather is None:
            small_ref, q_ref, k_ref, v_ref, y_ref, lse_ref, m_sc, acc_sc = refs
        else:
            small_ref, q_ref, k_ref, v_ref, pk_ref, y_ref, lse_ref, ga_ref, m_sc, acc_sc, ssem, rsem = refs
        g = pl.program_id(0)
        qb = pl.program_id(1)
        if gather is not None:
            @pl.when((g == 0) & (qb == 0))
            def _():
                _ag_start(pk_ref, ga_ref, ssem, rsem)
        qs = jnp.concatenate([q_ref[:, LANES * i:LANES * (i + 1)] for i in range(rep)], axis=0)
        is_lat = qb < n_lat_q
        nsub = GQ_TK // LANES
        scores_small = small_ref[0] == 1

        @pl.when(scores_small)
        def _():
            m_sc[...] = jnp.zeros((R, LANES), F32)

        @pl.when(jnp.logical_not(scores_small))
        def _():
            m_sc[...] = jnp.full((R, LANES), NEG, F32)

            def chunk_max(j):
                s = _dot_nt(qs, k_ref[pl.ds(_chunk_offset(j), GQ_TK), :])
                mm = s[:, :LANES]
                for t in range(1, nsub):
                    mm = jnp.maximum(mm, s[:, LANES * t:LANES * (t + 1)])
                m_sc[...] = jnp.maximum(m_sc[...], mm)

            _sweep_keys(chunk_max, is_lat, n_lat_k, nk)
            m_sc[...] = jnp.broadcast_to(jnp.max(m_sc[...], axis=-1, keepdims=True), (R, LANES))

        acc_sc[...] = jnp.zeros((R, LANES), F32)
        own_v = _lane_half((GQ_TK, LANES)) == jnp.broadcast_to(g, (GQ_TK, LANES))

        def chunk_acc(j):
            off = _chunk_offset(j)
            vc = v_ref[pl.ds(off, GQ_TK), :]
            v1 = jnp.where(own_v, vc, jnp.ones_like(vc))
            s = _dot_nt(qs, k_ref[pl.ds(off, GQ_TK), :])
            mb = m_sc[...]
            p = jnp.concatenate([jnp.exp(s[:, LANES * t:LANES * (t + 1)] - mb).astype(BF16) for t in range(nsub)],
                                axis=1)
            acc_sc[...] += _dot(p, v1)

        _sweep_keys(chunk_acc, is_lat, n_lat_k, nk)
        acc = acc_sc[...]
        own = _lane_half((R, LANES)) == jnp.broadcast_to(g, (R, LANES))
        l = jnp.where(own, _roll64(acc), acc)
        o = acc / l
        lse = m_sc[...] + jnp.log(l)
        lo = _lane_lt64((GQ_TQ, LANES))
        g0 = jnp.broadcast_to(g, (GQ_TQ, LANES)) == 0
        for c2 in range(rep // 2):
            oa = o[(2 * c2) * GQ_TQ:(2 * c2 + 1) * GQ_TQ]
            ob = o[(2 * c2 + 1) * GQ_TQ:(2 * c2 + 2) * GQ_TQ]
            left = jnp.where(g0, oa, _roll64(oa))
            right = jnp.where(g0, _roll64(ob), ob)
            y_ref[:, LANES * c2:LANES * (c2 + 1)] = jnp.where(lo, left, right).astype(BF16)
        for i in range(rep):
            lse_ref[i] = lse[i * GQ_TQ:(i + 1) * GQ_TQ]
        if gather is not None:
            @pl.when((g == GQA_KV_HEADS - 1) & (qb == nq - 1))
            def _():
                _ag_finish(pk_ref, ga_ref, ssem, rsem)

    in_specs = [pl.BlockSpec((GQ_TQ, rep * LANES), lambda g, i, s: (i, g)),
                pl.BlockSpec((T, LANES), lambda g, i, s: (0, 0)),
                pl.BlockSpec((T, LANES), lambda g, i, s: (0, v_blk))]
    out_specs = [pl.BlockSpec((GQ_TQ, rep * HEAD_DIM), lambda g, i, s: (i, g)),
                 pl.BlockSpec((rep, GQ_TQ, LANES), lambda g, i, s: (g, i, 0))]
    out_shape = [jax.ShapeDtypeStruct((T, GQA_Q_HEADS * HEAD_DIM), BF16),
                 jax.ShapeDtypeStruct((GQA_Q_HEADS, T, LANES), F32)]
    scratch = [pltpu.VMEM((R, LANES), F32), pltpu.VMEM((R, LANES), F32)]
    args = [scores_small, qp, kp, gqkv]
    if gather is not None:
        in_specs.append(ANY)
        out_specs.append(ANY)
        out_shape.append(jax.ShapeDtypeStruct((N_CHIP,) + gather.shape, gather.dtype))
        scratch += [pltpu.SemaphoreType.DMA((6,)), pltpu.SemaphoreType.DMA((6,))]
        args.append(gather)
    return pl.pallas_call(
        body, name="gqa_fwd" if gather is None else "gqa_fwd_gather",
        grid_spec=pltpu.PrefetchScalarGridSpec(
            num_scalar_prefetch=1, grid=(GQA_KV_HEADS, nq), in_specs=in_specs, out_specs=out_specs,
            scratch_shapes=scratch),
        out_shape=out_shape,
        compiler_params=_cp(("arbitrary", "arbitrary") if gather is not None else ("parallel", "parallel"), VMEM_BIG),
    )(*args)


def gqa_bwd_call(qp, kp, gqkv, yb, dyb, lse, n_lat, exchange=None):
    T = qp.shape[0]
    rep = GQA_Q_HEADS // GQA_KV_HEADS
    nq = T // GQ_TQ
    nk = T // GQ_TK
    n_lat_q = n_lat // GQ_TQ
    n_lat_k = n_lat // GQ_TK
    v_blk = (GQA_Q_HEADS + GQA_KV_HEADS) * HEAD_DIM // LANES
    R = rep * GQ_TQ

    def body(*refs):
        if exchange is None:
            q_ref, k_ref, v_ref, y_ref, dy_ref, lse_ref, dq_ref, dk_ref, dv_ref, dq_sc = refs
        else:
            (q_ref, k_ref, v_ref, y_ref, dy_ref, lse_ref, gp_ref, dq_ref, dk_ref, dv_ref, rx_ref, dq_sc,
             ssem, rsem) = refs
        g = pl.program_id(0)
        qb = pl.program_id(1)
        if exchange is not None:
            @pl.when((g == 0) & (qb == 0))
            def _():
                _rs_start(gp_ref, rx_ref, ssem, rsem)

        @pl.when((g == 0) & (qb == 0))
        def _():
            dk_ref[...] = jnp.zeros((T, LANES), F32)
            dv_ref[...] = jnp.zeros((T, LANES), F32)

        qs = jnp.concatenate([q_ref[:, LANES * i:LANES * (i + 1)] for i in range(rep)], axis=0)
        dos, deltas, lses = [], [], []
        for i in range(rep):
            c2, half = i // 2, i % 2
            dch = _gqa_place(dy_ref[:, LANES * c2:LANES * (c2 + 1)].astype(F32), half, g)
            ych = _gqa_place(y_ref[:, LANES * c2:LANES * (c2 + 1)].astype(F32), half, g)
            dos.append(dch.astype(BF16))
            deltas.append(jnp.sum(dch * ych, axis=-1, keepdims=True))
            lses.append(lse_ref[i])
        do_s = jnp.concatenate(dos, axis=0)
        delta = jnp.broadcast_to(jnp.concatenate(deltas, axis=0), (R, LANES))
        lse_v = jnp.concatenate(lses, axis=0)
        dq_sc[...] = jnp.zeros((R, LANES), F32)
        nsub = GQ_TK // LANES

        def chunk(j):
            off = _chunk_offset(j)
            kc = k_ref[pl.ds(off, GQ_TK), :]
            vc = v_ref[pl.ds(off, GQ_TK), :]
            s = _dot_nt(qs, kc)
            dp = _dot_nt(do_s, vc)
            ps, dss = [], []
            for t in range(nsub):
                pt = jnp.exp(s[:, LANES * t:LANES * (t + 1)] - lse_v)
                ps.append(pt.astype(BF16))
                dss.append((pt * (dp[:, LANES * t:LANES * (t + 1)] - delta)).astype(BF16))
            p = jnp.concatenate(ps, axis=1)
            ds = jnp.concatenate(dss, axis=1)
            dv_ref[pl.ds(off, GQ_TK), :] += _dot_tn(p, do_s)
            dk_ref[pl.ds(off, GQ_TK), :] += _dot_tn(ds, qs)
            dq_sc[...] += _dot(ds, kc)

        _sweep_keys(chunk, qb < n_lat_q, n_lat_k, nk)
        for i in range(rep):
            dq_ref[:, LANES * i:LANES * (i + 1)] = dq_sc[i * GQ_TQ:(i + 1) * GQ_TQ, :]
        if exchange is not None:
            @pl.when((g == GQA_KV_HEADS - 1) & (qb == nq - 1))
            def _():
                _rs_finish(gp_ref, rx_ref, ssem, rsem)

    whole = pl.BlockSpec((T, LANES), lambda g, i: (0, 0))
    in_specs = [pl.BlockSpec((GQ_TQ, rep * LANES), lambda g, i: (i, g)),
                whole,
                pl.BlockSpec((T, LANES), lambda g, i: (0, v_blk)),
                pl.BlockSpec((GQ_TQ, rep * HEAD_DIM), lambda g, i: (i, g)),
                pl.BlockSpec((GQ_TQ, rep * HEAD_DIM), lambda g, i: (i, g)),
                pl.BlockSpec((rep, GQ_TQ, LANES), lambda g, i: (g, i, 0))]
    out_specs = [pl.BlockSpec((GQ_TQ, rep * LANES), lambda g, i: (i, g)), whole, whole]
    out_shape = [jax.ShapeDtypeStruct((T, GQA_Q_HEADS * LANES), F32),
                 jax.ShapeDtypeStruct((T, LANES), F32), jax.ShapeDtypeStruct((T, LANES), F32)]
    scratch = [pltpu.VMEM((R, LANES), F32)]
    args = [qp, kp, gqkv, yb, dyb, lse]
    if exchange is not None:
        in_specs.append(ANY)
        out_specs.append(ANY)
        out_shape.append(jax.ShapeDtypeStruct((N_DEV - 1,) + exchange.shape[2:], exchange.dtype))
        scratch += [pltpu.SemaphoreType.DMA((N_DEV - 1,)), pltpu.SemaphoreType.DMA((N_DEV - 1,))]
        args.append(exchange)
    return pl.pallas_call(
        body, name="gqa_bwd" if exchange is None else "gqa_bwd_exchange", grid=(GQA_KV_HEADS, nq),
        in_specs=in_specs, out_specs=out_specs, out_shape=out_shape, scratch_shapes=scratch,
        compiler_params=_cp(("arbitrary", "arbitrary"), VMEM_BIG),
    )(*args)


def _na_tables(rows):
    kh = min(NA_WIN_H, rows)
    assert kh == NA_WIN_H and rows >= NA_KROWS and rows % NA_ROWS == 0
    a = np.zeros((3, NA_ROWS, NA_KROWS, 2 * NA_WIN_H - 1), np.float32)
    for v, r0 in enumerate((0, NA_ROWS, rows - NA_ROWS)):
        ks = min(max(r0 - NA_WIN_H // 2, 0), rows - NA_KROWS)
        for rq in range(NA_ROWS):
            r = r0 + rq
            rs = min(max(r - kh // 2, 0), rows - kh)
            for rk in range(NA_KROWS):
                kr = ks + rk
                if rs <= kr < rs + kh:
                    a[v, rq, rk, kr - r + NA_WIN_H - 1] = 1.0
    c = np.zeros((GRID_W, GRID_W, 2 * NA_WIN_W - 1), np.float32)
    for wq in range(GRID_W):
        cs = min(max(wq - NA_WIN_W // 2, 0), GRID_W - NA_WIN_W)
        for wk in range(cs, cs + NA_WIN_W):
            c[wq, wk, min(max(wk - wq, -(NA_WIN_W - 1)), NA_WIN_W - 1) + NA_WIN_W - 1] = 1.0
    return a, c


NA_DR = 2 * NA_WIN_H - 1
NA_TK = 256


def _na_tile_index(rows):
    a, _ = _na_tables(rows)
    idx = np.where(a.sum(-1) > 0, a.argmax(-1), NA_DR).astype(np.int32)
    return jnp.asarray(idx.reshape(-1))


def na_bias_table(rpb, rows):
    _, c = _na_tables(rows)
    H = rpb.shape[0]
    KW = NA_KROWS * GRID_W
    t1 = jnp.einsum("hde,wue->hdwu", rpb, jnp.asarray(c), precision=lax.Precision.HIGHEST)
    t1 = jnp.where(jnp.asarray(c.sum(-1) > 0)[None, None], t1, NEG)
    t1 = jnp.concatenate([t1, jnp.full((H, 1, GRID_W, GRID_W), NEG, F32)], axis=1)
    t1 = jnp.tile(t1, (1, 1, 1, 2))

    def body(idx_ref, t_ref, o_ref):
        v = pl.program_id(1)
        lo = _lane_lt64((GRID_W, LANES))
        for q in range(NA_ROWS):
            for kp in range(NA_KROWS // 2):
                base = (v * NA_ROWS + q) * NA_KROWS + 2 * kp
                tile = jnp.where(lo, t_ref[0, idx_ref[base]], t_ref[0, idx_ref[base + 1]])
                o_ref[0, 0, GRID_W * q:GRID_W * (q + 1), LANES * kp:LANES * (kp + 1)] = tile

    return pl.pallas_call(
        body, name="na_bias_table",
        grid_spec=pltpu.PrefetchScalarGridSpec(
            num_scalar_prefetch=1, grid=(H, 3),
            in_specs=[pl.BlockSpec((1, NA_DR + 1, GRID_W, LANES), lambda h, v, i: (h, 0, 0, 0))],
            out_specs=pl.BlockSpec((1, 1, TM, KW), lambda h, v, i: (h, v, 0, 0))),
        out_shape=jax.ShapeDtypeStruct((H, 3, TM, KW), F32),
        compiler_params=_cp(("parallel", "parallel")),
    )(_na_tile_index(rows), t1)


def na_bias_grad(dbias, rows):
    _, c = _na_tables(rows)
    H = dbias.shape[0]
    KW = NA_KROWS * GRID_W

    def body(idx_ref, d_ref, o_ref):
        v = pl.program_id(1)
        lo = _lane_lt64((GRID_W, LANES))

        @pl.when(v == 0)
        def _():
            o_ref[...] = jnp.zeros((1, NA_DR + 1, GRID_W, LANES), F32)

        for q in range(NA_ROWS):
            for kp in range(NA_KROWS // 2):
                base = (v * NA_ROWS + q) * NA_KROWS + 2 * kp
                tile = d_ref[0, 0, GRID_W * q:GRID_W * (q + 1), LANES * kp:LANES * (kp + 1)]
                o_ref[0, idx_ref[base]] += jnp.where(lo, tile, 0.0)
                o_ref[0, idx_ref[base + 1]] += jnp.where(lo, 0.0, tile)

    dt = pl.pallas_call(
        body, name="na_bias_grad",
        grid_spec=pltpu.PrefetchScalarGridSpec(
            num_scalar_prefetch=1, grid=(H, 3),
            in_specs=[pl.BlockSpec((1, 1, TM, KW), lambda h, v, i: (h, v, 0, 0))],
            out_specs=pl.BlockSpec((1, NA_DR + 1, GRID_W, LANES), lambda h, v, i: (h, 0, 0, 0))),
        out_shape=jax.ShapeDtypeStruct((H, NA_DR + 1, GRID_W, LANES), F32),
        compiler_params=_cp(("parallel", "arbitrary")),
    )(_na_tile_index(rows), dbias)
    dt = dt[:, :NA_DR, :, :GRID_W] + dt[:, :NA_DR, :, GRID_W:]
    return jnp.einsum("hdwu,wue->hde", dt, jnp.asarray(c), precision=lax.Precision.HIGHEST)


def _na_variant(blk, n_lat_blk):
    return jnp.where(blk == 0, 0, jnp.where(blk >= n_lat_blk - 1, 2, 1))


def _na_kstart(blk, rows):
    r0 = blk * NA_ROWS
    ks = jnp.clip(r0 - NA_WIN_H // 2, 0, rows - NA_KROWS)
    return pl.multiple_of(ks * GRID_W, GRID_W)


def na_fwd_call(naqkv, bias, n_lat):
    T = naqkv.shape[0]
    nb = T // TM
    n_lat_blk = n_lat // TM
    rows = n_lat // GRID_W
    n_ctx = T - n_lat
    KW = NA_KROWS * GRID_W
    npair = NA_HEADS // 2

    def body(q_ref, k_ref, v_ref, b_ref, y_ref, lse_ref):
        blk = pl.program_id(1)
        half = _lane_half((TM, LANES))
        lo = half == 0
        q = q_ref[...].astype(F32) * SCALE
        kctx = k_ref[pl.ds(n_lat, n_ctx), :]
        vctx = v_ref[pl.ds(n_lat, n_ctx), :]

        @pl.when(blk < n_lat_blk)
        def _():
            off = _na_kstart(blk, rows)
            kwin = k_ref[pl.ds(off, KW), :]
            vwin = v_ref[pl.ds(off, KW), :]
            outs = []
            for hh in range(2):
                qm = jnp.where(half == hh, q, 0.0).astype(BF16)
                sw = _dot_nt(qm, kwin) + b_ref[hh, 0]
                sc = _dot_nt(qm, kctx)
                m = jnp.maximum(jnp.max(sw, axis=-1, keepdims=True), jnp.max(sc, axis=-1, keepdims=True))
                pw = jnp.exp(sw - m)
                pc = jnp.exp(sc - m)
                l = jnp.sum(pw, axis=-1, keepdims=True) + jnp.sum(pc, axis=-1, keepdims=True)
                outs.append((_dot(pw.astype(BF16), vwin) + _dot(pc.astype(BF16), vctx)) / l)
                lse_ref[hh] = jnp.broadcast_to(m + jnp.log(l), (TM, LANES))
            y_ref[...] = jnp.where(lo, outs[0], outs[1]).astype(BF16)

        @pl.when(blk >= n_lat_blk)
        def _():
            outs = []
            for hh in range(2):
                qm = jnp.where(half == hh, q, 0.0).astype(BF16)
                sc = _dot_nt(qm, kctx)
                m = jnp.max(sc, axis=-1, keepdims=True)
                pc = jnp.exp(sc - m)
                l = jnp.sum(pc, axis=-1, keepdims=True)
                outs.append(_dot(pc.astype(BF16), vctx) / l)
                lse_ref[hh] = jnp.broadcast_to(m + jnp.log(l), (TM, LANES))
            y_ref[...] = jnp.where(lo, outs[0], outs[1]).astype(BF16)

    return pl.pallas_call(
        body, name="na_fwd", grid=(npair, nb),
        in_specs=[pl.BlockSpec((TM, LANES), lambda p, b: (b, p)),
                  pl.BlockSpec((T, LANES), lambda p, b: (0, npair + p)),
                  pl.BlockSpec((T, LANES), lambda p, b: (0, 2 * npair + p)),
                  pl.BlockSpec((2, 1, TM, KW), lambda p, b: (p, _na_variant(b, n_lat_blk), 0, 0))],
        out_specs=[pl.BlockSpec((TM, LANES), lambda p, b: (b, p)),
                   pl.BlockSpec((2, TM, LANES), lambda p, b: (p, b, 0))],
        out_shape=[jax.ShapeDtypeStruct((T, NA_HEADS * HEAD_DIM), BF16),
                   jax.ShapeDtypeStruct((NA_HEADS, T, LANES), F32)],
        compiler_params=_cp(("parallel", "parallel"), VMEM_BIG),
    )(naqkv, naqkv, naqkv, bias)


def na_bwd_call(naqkv, bias, ya, dya, lse, n_lat):
    T = naqkv.shape[0]
    nb = T // TM
    n_lat_blk = n_lat // TM
    rows = n_lat // GRID_W
    n_ctx = T - n_lat
    KW = NA_KROWS * GRID_W
    npair = NA_HEADS // 2

    def body(q_ref, k_ref, v_ref, b_ref, y_ref, dy_ref, lse_ref, dq_ref, dk_ref, dv_ref, db_ref, dk_sc, dv_sc):
        blk = pl.program_id(1)
        half = _lane_half((TM, LANES))
        lo = half == 0
        q = q_ref[...].astype(F32) * SCALE
        kctx = k_ref[pl.ds(n_lat, n_ctx), :]
        vctx = v_ref[pl.ds(n_lat, n_ctx), :]
        dyv = dy_ref[...].astype(F32)
        yv = y_ref[...].astype(F32)

        @pl.when(blk == 0)
        def _():
            dk_sc[...] = jnp.zeros((T, LANES), F32)
            dv_sc[...] = jnp.zeros((T, LANES), F32)

        @pl.when(blk < n_lat_blk)
        def _():
            off = _na_kstart(blk, rows)
            kwin = k_ref[pl.ds(off, KW), :]
            vwin = v_ref[pl.ds(off, KW), :]
            first = (blk == 0) | (blk == 1) | (blk == n_lat_blk - 1)
            dqs = []
            for hh in range(2):
                sel = half == hh
                qm = jnp.where(sel, q, 0.0).astype(BF16)
                dom = jnp.where(sel, dyv, 0.0)
                delta = jnp.sum(dom * yv, axis=-1, keepdims=True)
                dom = dom.astype(BF16)
                lse_v = lse_ref[hh][:, 0:1]
                pw = jnp.exp(_dot_nt(qm, kwin) + b_ref[hh, 0] - lse_v)
                pc = jnp.exp(_dot_nt(qm, kctx) - lse_v)
                dsw = pw * (_dot_nt(dom, vwin) - delta)
                dsc = pc * (_dot_nt(dom, vctx) - delta)

                @pl.when(first)
                def _():
                    db_ref[hh, 0] = dsw

                @pl.when(jnp.logical_not(first))
                def _():
                    db_ref[hh, 0] += dsw

                dsw = dsw.astype(BF16)
                dsc = dsc.astype(BF16)
                dv_sc[pl.ds(off, KW), :] += _dot_tn(pw.astype(BF16), dom)
                dv_sc[pl.ds(n_lat, n_ctx), :] += _dot_tn(pc.astype(BF16), dom)
                dk_sc[pl.ds(off, KW), :] += _dot_tn(dsw, qm)
                dk_sc[pl.ds(n_lat, n_ctx), :] += _dot_tn(dsc, qm)
                dqs.append((_dot(dsw, kwin) + _dot(dsc, kctx)) * SCALE)
            dq_ref[...] = jnp.where(lo, dqs[0], dqs[1]).astype(BF16)

        @pl.when(blk >= n_lat_blk)
        def _():
            dqs = []
            for hh in range(2):
                sel = half == hh
                qm = jnp.where(sel, q, 0.0).astype(BF16)
                dom = jnp.where(sel, dyv, 0.0)
                delta = jnp.sum(dom * yv, axis=-1, keepdims=True)
                dom = dom.astype(BF16)
                pc = jnp.exp(_dot_nt(qm, kctx) - lse_ref[hh][:, 0:1])
                dsc = (pc * (_dot_nt(dom, vctx) - delta)).astype(BF16)
                dv_sc[pl.ds(n_lat, n_ctx), :] += _dot_tn(pc.astype(BF16), dom)
                dk_sc[pl.ds(n_lat, n_ctx), :] += _dot_tn(dsc, qm)
                dqs.append(_dot(dsc, kctx) * SCALE)
            dq_ref[...] = jnp.where(lo, dqs[0], dqs[1]).astype(BF16)

        @pl.when(blk == nb - 1)
        def _():
            dk_ref[...] = dk_sc[...].astype(BF16)
            dv_ref[...] = dv_sc[...].astype(BF16)

    blkspec = pl.BlockSpec((TM, LANES), lambda p, b: (b, p))
    col = pl.BlockSpec((T, LANES), lambda p, b: (0, p))
    bspec = pl.BlockSpec((2, 1, TM, KW), lambda p, b: (p, _na_variant(b, n_lat_blk), 0, 0))
    dq, dk, dv, db = pl.pallas_call(
        body, name="na_bwd", grid=(npair, nb),
        in_specs=[blkspec,
                  pl.BlockSpec((T, LANES), lambda p, b: (0, npair + p)),
                  pl.BlockSpec((T, LANES), lambda p, b: (0, 2 * npair + p)),
                  bspec, blkspec, blkspec,
                  pl.BlockSpec((2, TM, LANES), lambda p, b: (p, b, 0))],
        out_specs=[blkspec, col, col, bspec],
        out_shape=[jax.ShapeDtypeStruct((T, NA_HEADS * HEAD_DIM), BF16)] * 3
        + [jax.ShapeDtypeStruct((NA_HEADS, 3, TM, KW), F32)],
        scratch_shapes=[pltpu.VMEM((T, LANES), F32), pltpu.VMEM((T, LANES), F32)],
        compiler_params=_cp(("arbitrary", "arbitrary"), VMEM_BIG),
    )(naqkv, naqkv, naqkv, bias, ya, dya, lse)
    return jnp.concatenate([dq, dk, dv], axis=1), db


def rope_tables(n_lat, n_ctx):
    t = jnp.arange(n_lat)
    row = (t // GRID_W).astype(F32)
    col = (t % GRID_W).astype(F32)
    half = HEAD_DIM // 2
    inv = ROPE_THETA ** (-jnp.arange(0, half, 2, dtype=F32) / half)
    ang = jnp.concatenate([row[:, None] * inv, col[:, None] * inv], axis=-1)
    cos = jnp.repeat(jnp.cos(ang), 2, axis=-1)
    sin = jnp.repeat(jnp.sin(ang), 2, axis=-1) * jnp.tile(jnp.array([-1.0, 1.0], F32), half)
    cos = jnp.concatenate([cos, jnp.ones((n_ctx, HEAD_DIM), F32)], axis=0)
    sin = jnp.concatenate([sin, jnp.zeros((n_ctx, HEAD_DIM), F32)], axis=0)
    return jnp.tile(cos, (1, 2)), jnp.tile(sin, (1, 2))


def _row(v):
    return v.reshape(1, -1)


def device_fwd_bwd(xs, target, modv, w, small, n_lat, ex=None):
    T, D = xs.shape
    n_ctx = T - n_lat
    L = modv.shape[0]
    nlt = n_lat // TM
    rows = n_lat // GRID_W
    cos_t, sin_t = rope_tables(n_lat, n_ctx)
    qg = [jnp.tile(_row(small["q_gain"][l]), (1, GQA_Q_HEADS)) for l in range(L)]
    kg = [jnp.tile(_row(small["k_gain"][l]), (1, GQA_KV_HEADS)) for l in range(L)]
    bias = [na_bias_table(small["na_rpb"][l], rows) for l in range(L)]

    later = {}

    def wt(name, l):
        return w[name][l] if (ex is None or l == 0) else later[name][l - 1]

    saved = []
    x = xs
    h = norm_mod_call(x, _row(small["norm1"][0]), modv[0], nlt, 0)
    for l in range(L):
        naqkv, gqkv, gates = qkv_call(h, wt("w_in", l))
        ya, lse_a = na_fwd_call(naqkv, bias[l], n_lat)
        qp, kp = gqa_prep_call(gqkv, qg[l], kg[l], cos_t, sin_t)
        small_scores = gqa_scores_small(small["q_gain"][l], small["k_gain"][l])
        if ex is not None and l == 0 and L > 1:
            yb, lse_b, gathered = gqa_fwd_call(qp, kp, gqkv, n_lat, small_scores, gather=ex.rest())
            later.update(ex.rest_layers(gathered))
        else:
            yb, lse_b = gqa_fwd_call(qp, kp, gqkv, n_lat, small_scores)
        merged, pa, pb = merge_call(ya, yb, gates, wt("w_pa", l), wt("w_pb", l))
        x1, osave, h2 = wo_call(merged, wt("w_o", l), x, modv[l], _row(small["norm2"][l]), nlt)
        act, a, u = ffn_in_call(h2, wt("w_ffn_in", l))
        if l + 1 < L:
            x2, fsave, hn = ffn_out_call(act, wt("w_ffn_out", l), x1, modv[l], nlt,
                                         _row(small["norm1"][l + 1]), modv[l + 1])
        else:
            x2, fsave = ffn_out_call(act, wt("w_ffn_out", l), x1, modv[l], nlt)
            hn = None
        saved.append(dict(x=x, h=h, naqkv=naqkv, gqkv=gqkv, gates=gates, ya=ya, lse_a=lse_a, qp=qp, kp=kp,
                          yb=yb, lse_b=lse_b, merged=merged, pa=pa, pb=pb, x1=x1, osave=osave, h2=h2,
                          act=act, a=a, u=u, fsave=fsave))
        x, h = x2, hn

    dx, d_final, loss_part = loss_call(x, _row(small["final_norm"]), target, nlt)

    gw = {k: [None] * L for k in ("w_in", "w_pa", "w_pb", "w_o", "w_ffn_in", "w_ffn_out")}
    gs = {k: [None] * L for k in ("norm1", "norm2", "na_rpb", "q_gain", "k_gain")}
    dmod = [None] * L
    fold = lambda s: s.sum(axis=1)
    for l in reversed(range(L)):
        s = saved[l]
        da, du, df, s_g2 = dact_call(dx, modv[l], s["fsave"], wt("w_ffn_out", l), s["a"], s["u"], nlt)
        gw["w_ffn_out"][l] = wgrad_call(s["act"], df, "wgrad_ffn_out")
        dx1, s_sh2, s_sc2, s_n2 = dh2_call(da, du, wt("w_ffn_in", l), s["x1"], modv[l], _row(small["norm2"][l]), dx, nlt)
        gw["w_ffn_in"][l] = jnp.concatenate(
            [wgrad_call(s["h2"], da, "wgrad_ffn_in_a"), wgrad_call(s["h2"], du, "wgrad_ffn_in_u")], axis=1)
        dpa, dpb, dgates, do, s_g1 = dmerged_call(dx1, modv[l], s["osave"], wt("w_o", l), s["pa"], s["pb"], s["gates"], nlt)
        gw["w_o"][l] = wgrad_call(s["merged"], do, "wgrad_o")
        dya, dyb = dy_call(dpa, dpb, wt("w_pa", l), wt("w_pb", l))
        gw["w_pa"][l] = wgrad_call(s["ya"], dpa, "wgrad_pa")
        gw["w_pb"][l] = wgrad_call(s["yb"], dpb, "wgrad_pb")
        dna, dbias = na_bwd_call(s["naqkv"], bias[l], s["ya"], dya, s["lse_a"], n_lat)
        gs["na_rpb"][l] = na_bias_grad(dbias, rows)
        if ex is not None and l + 1 < L:
            dqp, dkp, dv, ex.received[l + 1] = gqa_bwd_call(s["qp"], s["kp"], s["gqkv"], s["yb"], dyb, s["lse_b"], n_lat,
                                                           exchange=ex.partials[l + 1])
        else:
            dqp, dkp, dv = gqa_bwd_call(s["qp"], s["kp"], s["gqkv"], s["yb"], dyb, s["lse_b"], n_lat)
        dg, s_qg, s_kg = gqa_prep_bwd_call(s["gqkv"], qg[l], kg[l], cos_t, sin_t, dqp, dkp, dv)
        gs["q_gain"][l] = s_qg.reshape(8 * GQA_Q_HEADS, HEAD_DIM).sum(axis=0)
        gs["k_gain"][l] = s_kg.reshape(8 * GQA_KV_HEADS, HEAD_DIM).sum(axis=0)
        dx, s_sh1, s_sc1, s_n1 = dh_call(dna, dg, dgates, wt("w_in", l), s["x"], modv[l], _row(small["norm1"][l]), dx1, nlt)
        gw["w_in"][l] = jnp.concatenate(
            [wgrad_call(s["h"], dna, "wgrad_in_na"), wgrad_call(s["h"], dg, "wgrad_in_g"),
             wgrad_call(s["h"], dgates, "wgrad_in_gate")], axis=1)
        gs["norm1"][l] = fold(s_n1).sum(axis=0)
        gs["norm2"][l] = fold(s_n2).sum(axis=0)
        dmod[l] = jnp.stack([fold(s_sh1), fold(s_sc1), fold(s_g1), fold(s_sh2), fold(s_sc2), fold(s_g2)], axis=1)
        if ex is not None:
            ex.pack_grads(l, {k: v[l] for k, v in gw.items()})

    if ex is not None:
        ex.received[0] = grads_exchange(ex.partials[0])
        gw = None
    else:
        gw = {k: jnp.stack(v) for k, v in gw.items()}
    gs = {k: jnp.stack(v) for k, v in gs.items()}
    gs["final_norm"] = d_final.sum(axis=(0, 1))
    return loss_part, dx, gw, jnp.stack(dmod), gs


N_DEV = 8
N_CHIP = 4
ANY = pl.BlockSpec(memory_space=pl.ANY)


def _place():
    x, y, c = lax.axis_index("x"), lax.axis_index("y"), lax.axis_index("c")
    chips = [(1 - x, y), (x, 1 - y), (1 - x, 1 - y)]
    return x, y, c, chips


def small_all_gather(v, name):
    m_per, n = v.shape
    assert m_per % 8 == 0

    def body(x_ref, out_ref, send_sems, recv_sems, local_sem):
        x, y, c, chips = _place()
        me, sibling = (x, y, c), (x, y, 1 - c)

        def rows(px, py, pc):
            return out_ref.at[pl.ds(pl.multiple_of((4 * px + 2 * py + pc) * m_per, 8), m_per), :]

        def copy(k, block, to, src=None):
            return pltpu.make_async_remote_copy(
                src_ref=rows(*block) if src is None else src, dst_ref=rows(*block),
                send_sem=send_sems.at[k], recv_sem=recv_sems.at[k], device_id=to, device_id_type=MESH)

        mine = pltpu.make_async_copy(x_ref, rows(*me), local_sem)
        mine.start()
        first = [copy(0, me, sibling, src=x_ref)]
        first += [copy(1 + j, me, (*chip, c), src=x_ref) for j, chip in enumerate(chips)]
        for cp in first:
            cp.start()
        passed = [copy(4 + j, (*chip, c), sibling) for j, chip in enumerate(chips)]
        for j, chip in enumerate(chips):
            copy(1 + j, (*chip, c), me).wait_recv()
            passed[j].start()
        copy(0, sibling, me).wait_recv()
        for j, chip in enumerate(chips):
            copy(4 + j, (*chip, 1 - c), me).wait_recv()
        for cp in first + passed:
            cp.wait_send()
        mine.wait()

    out = pl.pallas_call(
        body, name=name,
        out_shape=jax.ShapeDtypeStruct((N_DEV * m_per, n), v.dtype),
        in_specs=[pl.BlockSpec(memory_space=pltpu.VMEM)],
        out_specs=pl.BlockSpec(memory_space=pltpu.VMEM),
        scratch_shapes=[pltpu.SemaphoreType.DMA((7,)), pltpu.SemaphoreType.DMA((7,)), pltpu.SemaphoreType.DMA],
    )(v)
    return out.reshape(N_DEV, m_per, n)


def _ag_copies(in_ref, out_ref, send_sems, recv_sems):
    x, y, c, chips = _place()
    me, sibling = (x, y, c), (x, y, 1 - c)

    def blk(px, py, half):
        return out_ref.at[2 * px + py, half]

    def copy(k, block, to, src=None):
        return pltpu.make_async_remote_copy(
            src_ref=blk(*block) if src is None else src, dst_ref=blk(*block),
            send_sem=send_sems.at[k], recv_sem=recv_sems.at[k], device_id=to, device_id_type=MESH)

    first = [copy(j, (x, y, c), (*chip, c), src=in_ref.at[c]) for j, chip in enumerate(chips)]
    passed = [copy(3 + j, (*chip, c), sibling) for j, chip in enumerate(chips)]
    arrivals = [copy(j, (*chip, c), me) for j, chip in enumerate(chips)]
    forwarded = [copy(3 + j, (*chip, 1 - c), me) for j, chip in enumerate(chips)]
    return first, passed, arrivals, forwarded


def _ag_start(in_ref, out_ref, send_sems, recv_sems):
    for cp in _ag_copies(in_ref, out_ref, send_sems, recv_sems)[0]:
        cp.start()


def _ag_finish(in_ref, out_ref, send_sems, recv_sems):
    first, passed, arrivals, forwarded = _ag_copies(in_ref, out_ref, send_sems, recv_sems)
    for arrived, onward in zip(arrivals, passed):
        arrived.wait_recv()
        onward.start()
    for cp in forwarded:
        cp.wait_recv()
    for cp in first + passed:
        cp.wait_send()


def place_own_block(gathered, packed):
    chip = 2 * lax.axis_index("x") + lax.axis_index("y")
    return lax.dynamic_update_slice(gathered, packed[None], (chip,) + (0,) * packed.ndim)


def weights_all_gather(packed):
    def body(in_ref, out_ref, send_sems, recv_sems):
        _ag_start(in_ref, out_ref, send_sems, recv_sems)
        _ag_finish(in_ref, out_ref, send_sems, recv_sems)

    out = pl.pallas_call(
        body, name="weights_all_gather",
        out_shape=jax.ShapeDtypeStruct((N_CHIP,) + packed.shape, packed.dtype),
        in_specs=[ANY], out_specs=ANY,
        scratch_shapes=[pltpu.SemaphoreType.DMA((6,)), pltpu.SemaphoreType.DMA((6,))],
    )(packed)
    return place_own_block(out, packed)


def _rs_copies(gp_ref, rx_ref, send_sems, recv_sems):
    x, y, c, chips = _place()
    sends = [pltpu.make_async_remote_copy(src_ref=gp_ref.at[2 * x + y, 1 - c], dst_ref=rx_ref.at[0],
                                          send_sem=send_sems.at[0], recv_sem=recv_sems.at[0],
                                          device_id=(x, y, 1 - c), device_id_type=MESH)]
    for r, (px, py) in enumerate(chips):
        for h in range(2):
            sends.append(pltpu.make_async_remote_copy(
                src_ref=gp_ref.at[2 * px + py, h], dst_ref=rx_ref.at[1 + 2 * r + c],
                send_sem=send_sems.at[1 + 2 * r + h], recv_sem=recv_sems.at[1 + 2 * r + c],
                device_id=(px, py, h), device_id_type=MESH))
    arrivals = [pltpu.make_async_remote_copy(src_ref=gp_ref.at[0, 0], dst_ref=rx_ref.at[s],
                                             send_sem=send_sems.at[s], recv_sem=recv_sems.at[s],
                                             device_id=(x, y, c), device_id_type=MESH) for s in range(N_DEV - 1)]
    return sends, arrivals


def _rs_start(gp_ref, rx_ref, send_sems, recv_sems):
    for cp in _rs_copies(gp_ref, rx_ref, send_sems, recv_sems)[0]:
        cp.start()


def _rs_finish(gp_ref, rx_ref, send_sems, recv_sems):
    sends, arrivals = _rs_copies(gp_ref, rx_ref, send_sems, recv_sems)
    for cp in arrivals:
        cp.wait_recv()
    for cp in sends:
        cp.wait_send()


def grads_exchange(gp):
    def body(gp_ref, rx_ref, send_sems, recv_sems):
        _rs_start(gp_ref, rx_ref, send_sems, recv_sems)
        _rs_finish(gp_ref, rx_ref, send_sems, recv_sems)

    return pl.pallas_call(
        body, name="grads_exchange", out_shape=jax.ShapeDtypeStruct((N_DEV - 1,) + gp.shape[2:], gp.dtype),
        in_specs=[ANY], out_specs=ANY,
        scratch_shapes=[pltpu.SemaphoreType.DMA((N_DEV - 1,)), pltpu.SemaphoreType.DMA((N_DEV - 1,))],
    )(gp)


def grads_join_halves(gh):
    def body(gh_ref, out_ref, send_sem, recv_sem):
        x, y, c, _ = _place()
        cp = pltpu.make_async_remote_copy(src_ref=gh_ref, dst_ref=out_ref, send_sem=send_sem,
                                          recv_sem=recv_sem, device_id=(x, y, 1 - c), device_id_type=MESH)
        cp.start()
        cp.wait()

    other = pl.pallas_call(
        body, name="grads_join_halves", out_shape=jax.ShapeDtypeStruct(gh.shape, gh.dtype),
        in_specs=[ANY], out_specs=ANY,
        scratch_shapes=[pltpu.SemaphoreType.DMA, pltpu.SemaphoreType.DMA],
    )(gh)
    c0 = lax.axis_index("c") == 0
    return jnp.stack([jnp.where(c0, gh, other), jnp.where(c0, other, gh)], axis=1)


PACK_C = 1024


def sum_partials_call(gp, rx, chip, core):
    _, _, rh, ncol = gp.shape
    n = rx.shape[0]
    tr = _pick_tile(rh, 512, 16)

    def body(where_ref, a_ref, *rest):
        o_ref = rest[-1]
        g = a_ref[0, 0].astype(F32)
        for b_ref in rest[:-1]:
            g = g + b_ref[0].astype(F32)
        o_ref[...] = g

    other = lambda j: pl.BlockSpec((1, tr, ncol), lambda i, w: (j, i, 0))
    return pl.pallas_call(
        body, name="sum_partials",
        grid_spec=pltpu.PrefetchScalarGridSpec(
            num_scalar_prefetch=1, grid=(rh // tr,),
            in_specs=[pl.BlockSpec((1, 1, tr, ncol), lambda i, w: (w[0], w[1], i, 0))] + [other(j) for j in range(n)],
            out_specs=pl.BlockSpec((tr, ncol), lambda i, w: (i, 0))),
        out_shape=jax.ShapeDtypeStruct((rh, ncol), F32),
        compiler_params=_cp(("parallel",)),
    )(jnp.stack([chip, core]).astype(jnp.int32), gp, *([rx] * n))


def _silu(x):
    return x * _sigmoid(x)


def mod_matmul_call(craw, w_mod):
    L, D, N = w_mod.shape

    def body(c_ref, w_ref, o_ref):
        o_ref[0] = _dot(_silu(c_ref[...]).astype(BF16), w_ref[0].astype(BF16))

    return pl.pallas_call(
        body, name="mod_matmul", grid=(L,),
        in_specs=[pl.BlockSpec((16, D), lambda l: (0, 0)), pl.BlockSpec((1, D, N), lambda l: (l, 0, 0))],
        out_specs=pl.BlockSpec((1, 16, N), lambda l: (l, 0, 0)),
        out_shape=jax.ShapeDtypeStruct((L, 16, N), F32),
        compiler_params=_cp(("parallel",), VMEM_BIG),
    )(craw, w_mod)


def mod_bwd_call(craw, dmod16, w_mod):
    L, D, N = w_mod.shape

    def body(c_ref, d_ref, w_ref, gw_ref, ds_ref):
        l = pl.program_id(0)
        d = d_ref[0].astype(BF16)
        gw_ref[0] = _dot_tn(_silu(c_ref[...]).astype(BF16), d)
        part = _dot_nt(d, w_ref[0].astype(BF16))

        @pl.when(l == 0)
        def _():
            ds_ref[...] = part

        @pl.when(l > 0)
        def _():
            ds_ref[...] += part

    return pl.pallas_call(
        body, name="mod_bwd", grid=(L,),
        in_specs=[pl.BlockSpec((16, D), lambda l: (0, 0)), pl.BlockSpec((1, 16, N), lambda l: (l, 0, 0)),
                  pl.BlockSpec((1, D, N), lambda l: (l, 0, 0))],
        out_specs=[pl.BlockSpec((1, D, N), lambda l: (l, 0, 0)), pl.BlockSpec((16, D), lambda l: (0, 0))],
        out_shape=[jax.ShapeDtypeStruct((L, D, N), F32), jax.ShapeDtypeStruct((16, D), F32)],
        compiler_params=_cp(("arbitrary",), VMEM_BIG),
    )(craw, dmod16, w_mod)


def _adamw(w, g, m, v):
    m2 = ADAM_B1 * m + (1.0 - ADAM_B1) * g
    v2 = ADAM_B2 * v + (1.0 - ADAM_B2) * (g * g)
    m_hat = m2 / (1.0 - ADAM_B1 ** ADAM_STEP)
    v_hat = v2 / (1.0 - ADAM_B2 ** ADAM_STEP)
    delta = -ADAM_LR * (m_hat / (jnp.sqrt(v_hat) + ADAM_EPS) + ADAM_WD * w)
    return delta, m2, v2


def adamw_call(w, g, m, v):
    shape = w.shape
    ncol = shape[-1]
    r = math.prod(shape[:-1])
    tr = _pick_tile(r, 512, 8)
    as2d = lambda t: t.reshape(r, ncol)

    def body(w_ref, g_ref, m_ref, v_ref, d_ref, m2_ref, v2_ref):
        d_ref[...], m2_ref[...], v2_ref[...] = _adamw(w_ref[...], g_ref[...], m_ref[...], v_ref[...])

    spec = pl.BlockSpec((tr, ncol), lambda i: (i, 0))
    outs = pl.pallas_call(
        body, name="adamw", grid=(r // tr,), in_specs=[spec] * 4, out_specs=[spec] * 3,
        out_shape=[jax.ShapeDtypeStruct((r, ncol), F32)] * 3,
        compiler_params=_cp(("parallel",)),
    )(as2d(w), as2d(g), as2d(m), as2d(v))
    return [o.reshape(shape) for o in outs]


def adamw_small_call(parts, w, m, v, silu_bwd=False):
    P, R, C = parts.shape

    def body(p_ref, w_ref, m_ref, v_ref, g_ref, d_ref, m2_ref, v2_ref):
        g = p_ref[0]
        for i in range(1, P):
            g = g + p_ref[i]
        wv = w_ref[...]
        if silu_bwd:
            sg = _sigmoid(wv)
            g = g * (sg * (1.0 + wv * (1.0 - sg)))
        g_ref[...] = g
        d_ref[...], m2_ref[...], v2_ref[...] = _adamw(wv, g, m_ref[...], v_ref[...])

    return pl.pallas_call(
        body, name="adamw_small", out_shape=[jax.ShapeDtypeStruct((R, C), F32)] * 4,
    )(parts, w, m, v)


def sum_parts_call(parts):
    P, R, C = parts.shape

    def body(p_ref, o_ref):
        g = p_ref[0]
        for i in range(1, P):
            g = g + p_ref[i]
        o_ref[...] = g

    return pl.pallas_call(body, name="sum_parts", out_shape=jax.ShapeDtypeStruct((R, C), F32))(parts)


BIG = (("w_in", "col"), ("w_pa", "col"), ("w_pb", "col"), ("w_o", "row"), ("w_ffn_in", "col"), ("w_ffn_out", "row"))
SMALL = ("norm1", "norm2", "final_norm", "na_rpb", "q_gain", "k_gain")


def _pack_shards(shards):
    lead = shards[BIG[0][0]].shape[:-2]
    p = jnp.concatenate([shards[n].reshape(lead + (-1, PACK_C)) for n, _ in BIG], axis=-2)
    return p.reshape(lead + (2, p.shape[-2] // 2, PACK_C))


def _unpack_shards(packed, shapes):
    lead = packed.shape[:-3]
    flat = packed.reshape(lead + (-1, PACK_C))
    out, r0 = {}, 0
    for n, _ in BIG:
        K, N = shapes[n]
        rw = K * N // PACK_C
        out[n] = flat[..., r0:r0 + rw, :].reshape(lead + (K, N))
        r0 += rw
    return out


def _whole_from_chips(g, kind):
    _, L, K, N = g.shape
    if kind == "col":
        return g.transpose(1, 2, 0, 3).reshape(L, K, N_CHIP * N)
    return g.transpose(1, 0, 2, 3).reshape(L, N_CHIP * K, N)


def _chips_from_whole(g, kind):
    K, N = g.shape
    if kind == "col":
        return g.reshape(K, N_CHIP, N // N_CHIP).transpose(1, 0, 2)
    return g.reshape(N_CHIP, K // N_CHIP, N)


class Exchange:
    def __init__(self, packed, shard_shapes):
        self.shapes = shard_shapes
        self.packed = packed
        self.received = {}
        self.partials = {}

    def whole(self, gathered):
        g = _unpack_shards(gathered.transpose(0, 2, 1, 3, 4), self.shapes)
        return {n: _whole_from_chips(g[n], kind) for n, kind in BIG}

    def first_layer(self):
        return self.whole(weights_all_gather(self.packed[:, :1]))

    def rest(self):
        return self.packed[:, 1:]

    def rest_layers(self, gathered):
        return self.whole(place_own_block(gathered, self.rest()))

    def pack_grads(self, layer, gw):
        self.partials[layer] = _pack_shards({n: _chips_from_whole(gw[n], kind) for n, kind in BIG})
        return self.partials[layer]


def _pad_rows(v, rows):
    return jnp.pad(v.reshape(-1), (0, rows * PACK_C - v.size)).reshape(rows, PACK_C)


def kernel(x, c, ctx, c_ctx, w_mod, b_mod, norm1, w_in, na_rpb, q_gain, k_gain, w_pa, w_pb, w_o, norm2, w_ffn_in, w_ffn_out, final_norm, loss_target, m_c_ctx, m_w_mod, m_b_mod, m_norm1, m_w_in, m_na_rpb, m_q_gain, m_k_gain, m_w_pa, m_w_pb, m_w_o, m_norm2, m_w_ffn_in, m_w_ffn_out, m_final_norm, v_c_ctx, v_w_mod, v_b_mod, v_norm1, v_w_in, v_na_rpb, v_q_gain, v_k_gain, v_w_pa, v_w_pb, v_w_o, v_norm2, v_w_ffn_in, v_w_ffn_out, v_final_norm):
    W = dict(c_ctx=c_ctx, w_mod=w_mod, b_mod=b_mod, norm1=norm1, w_in=w_in, na_rpb=na_rpb, q_gain=q_gain, k_gain=k_gain,
             w_pa=w_pa, w_pb=w_pb, w_o=w_o, norm2=norm2, w_ffn_in=w_ffn_in, w_ffn_out=w_ffn_out, final_norm=final_norm)
    M = dict(c_ctx=m_c_ctx, w_mod=m_w_mod, b_mod=m_b_mod, norm1=m_norm1, w_in=m_w_in, na_rpb=m_na_rpb, q_gain=m_q_gain,
             k_gain=m_k_gain, w_pa=m_w_pa, w_pb=m_w_pb, w_o=m_w_o, norm2=m_norm2, w_ffn_in=m_w_ffn_in,
             w_ffn_out=m_w_ffn_out, final_norm=m_final_norm)
    V = dict(c_ctx=v_c_ctx, w_mod=v_w_mod, b_mod=v_b_mod, norm1=v_norm1, w_in=v_w_in, na_rpb=v_na_rpb, q_gain=v_q_gain,
             k_gain=v_k_gain, w_pa=v_w_pa, w_pb=v_w_pb, w_o=v_w_o, norm2=v_norm2, w_ffn_in=v_w_ffn_in,
             w_ffn_out=v_w_ffn_out, final_norm=v_final_norm)
    order = ["c_ctx", "w_mod", "b_mod", "norm1", "w_in", "na_rpb", "q_gain", "k_gain", "w_pa", "w_pb", "w_o", "norm2",
             "w_ffn_in", "w_ffn_out", "final_norm"]
    L, D = norm1.shape
    n_lat = x.shape[1]
    ax, ay, ac = lax.axis_index("x"), lax.axis_index("y"), lax.axis_index("c")
    chip = 2 * ax + ay
    dev = 2 * chip + ac

    shard_shapes = {n: W[n].shape[1:] for n, _ in BIG}
    ex = Exchange(_pack_shards({n: W[n].astype(BF16) for n, _ in BIG}).transpose(1, 0, 2, 3), shard_shapes)
    w_first = ex.first_layer()

    c_all = small_all_gather(jnp.pad(c, ((0, 7), (0, 0))), "gather_c")[:, 0, :]
    craw = jnp.pad(c_all, ((0, 8), (0, 0))) + jnp.pad(c_ctx[None, :], ((8, 7), (0, 0)))
    ncol_mod = w_mod.shape[2]
    mod_loc = mod_matmul_call(craw, w_mod)
    mod_all = small_all_gather(mod_loc.reshape(L * 16, ncol_mod), "gather_mod")
    mod_all = mod_all[0::2].reshape(N_CHIP, L, 16, ncol_mod).transpose(1, 2, 0, 3).reshape(L, 16, 6 * D)
    mod_all = mod_all + b_mod[:, None, :]
    mod_mine = lax.dynamic_index_in_dim(mod_all, dev, axis=1, keepdims=False)
    modv = jnp.stack([mod_mine, mod_all[:, 8]], axis=1).reshape(L, 2, 6, D)

    xs = jnp.concatenate([x[0], ctx[0]], axis=0)
    small = dict(norm1=norm1, norm2=norm2, na_rpb=na_rpb, q_gain=q_gain, k_gain=k_gain, final_norm=final_norm)
    loss_part, dx, _, dmod, gs = device_fwd_bwd(xs, loss_target[0], modv, w_first, small, n_lat, ex)
    loss = lax.psum(jnp.sum(loss_part), ("x", "y", "c"))
    grad_x = dx[:n_lat][None]

    gh = jnp.stack([sum_partials_call(ex.partials[l], ex.received[l], chip, ac) for l in range(L)])
    gbig = _unpack_shards(grads_join_halves(gh), shard_shapes)

    n_mod_rows = L * 6 * D // PACK_C
    small_flat = jnp.concatenate([gs[n].reshape(-1) for n in SMALL])
    n_small_rows = -(-small_flat.size // (8 * PACK_C)) * 8
    g1 = jnp.concatenate([dmod[:, 0].reshape(n_mod_rows, PACK_C), dmod[:, 1].reshape(n_mod_rows, PACK_C),
                          _pad_rows(small_flat, n_small_rows)], axis=0)
    g1 = small_all_gather(g1, "gather_small_grads")
    dm_lat = g1[:, :n_mod_rows].reshape(N_DEV, L, 6 * D)
    dm_ctx_parts = g1[:, n_mod_rows:2 * n_mod_rows]
    dm_ctx = sum_parts_call(dm_ctx_parts).reshape(L, 1, 6 * D)
    dmod16 = (jnp.pad(dm_lat.transpose(1, 0, 2), ((0, 0), (0, 8), (0, 0)))
              + jnp.pad(dm_ctx, ((0, 0), (8, 7), (0, 0))))
    dmod16_loc = lax.dynamic_slice_in_dim(dmod16, chip * ncol_mod, ncol_mod, axis=2)
    g_wmod, dsilu_part = mod_bwd_call(craw, dmod16_loc, w_mod)
    ds_all = small_all_gather(dsilu_part[8:16], "gather_dsilu")
    ds_parts = ds_all[0::2, 0:1, :]
    ds_parts = jnp.pad(ds_parts, ((0, 0), (0, 7), (0, 0)))

    out_g, out_d, out_m, out_v = {}, {}, {}, {}
    for n, _ in BIG:
        out_g[n] = gbig[n]
        out_d[n], out_m[n], out_v[n] = adamw_call(W[n], gbig[n], M[n], V[n])
    out_g["w_mod"] = g_wmod
    out_d["w_mod"], out_m["w_mod"], out_v["w_mod"] = adamw_call(w_mod, g_wmod, m_w_mod, v_w_mod)

    pack_small = lambda t: _pad_rows(jnp.concatenate([t[n].reshape(-1) for n in SMALL]), n_small_rows)
    res = adamw_small_call(g1[:, 2 * n_mod_rows:], pack_small(W), pack_small(M), pack_small(V))
    off = 0
    for n in SMALL:
        sz = W[n].size
        for dst, r in zip((out_g, out_d, out_m, out_v), res):
            dst[n] = r.reshape(-1)[off:off + sz].reshape(W[n].shape)
        off += sz
    bparts = jnp.concatenate([g1[:, :n_mod_rows], dm_ctx_parts], axis=0)
    as_rows = lambda t: t.reshape(n_mod_rows, PACK_C)
    res = adamw_small_call(bparts, as_rows(b_mod), as_rows(m_b_mod), as_rows(v_b_mod))
    for dst, r in zip((out_g, out_d, out_m, out_v), res):
        dst["b_mod"] = r.reshape(b_mod.shape)
    row8 = lambda t: jnp.pad(t[None, :], ((0, 7), (0, 0)))
    res = adamw_small_call(ds_parts, row8(c_ctx), row8(m_c_ctx), row8(v_c_ctx), silu_bwd=True)
    for dst, r in zip((out_g, out_d, out_m, out_v), res):
        dst["c_ctx"] = r[0]

    return (loss, grad_x, *[out_g[n] for n in order], *[out_d[n] for n in order],
            *[out_m[n] for n in order], *[out_v[n] for n in order])
```

```python
import functools
import math

import numpy as np
import jax
import jax.numpy as jnp
from jax import lax
from jax.experimental import pallas as pl
from jax.experimental.pallas import tpu as pltpu

F32 = jnp.float32
BF16 = jnp.bfloat16
MESH = pl.DeviceIdType.MESH

HEAD_DIM = 64
NA_HEADS = 8
GQA_Q_HEADS = 8
GQA_KV_HEADS = 2
GRID_W = 64
NA_WIN_H = 8
NA_WIN_W = 16
ROPE_THETA = 10000.0
EPS = 1e-6
SCALE = HEAD_DIM ** -0.5
NEG = -1e30

ADAM_LR = 0.001
ADAM_B1 = 0.9
ADAM_B2 = 0.999
ADAM_EPS = 1e-08
ADAM_WD = 0.01
ADAM_STEP = 10

TM = 256
NA_ROWS = 4
NA_KROWS = 12
LANES = 128
VMEM_BIG = 56 * 1024 * 1024


def _cp(sem, vmem=None):
    return pltpu.CompilerParams(dimension_semantics=sem, vmem_limit_bytes=vmem)


def _dot(a, b):
    return jnp.dot(a, b, preferred_element_type=F32)


def _dot_nt(a, b):
    return lax.dot_general(a, b, (((1,), (1,)), ((), ())), preferred_element_type=F32)


def _dot_tn(a, b):
    return lax.dot_general(a, b, (((0,), (0,)), ((), ())), preferred_element_type=F32)


def _colsum8(v):
    tm, d = v.shape
    return v.reshape(tm // 8, 8, d).sum(axis=0)


def _acc(ref, val, first):
    @pl.when(first)
    def _():
        ref[0] = val

    @pl.when(jnp.logical_not(first))
    def _():
        ref[0] += val


def _rms_r(x):
    return lax.rsqrt(jnp.mean(x * x, axis=-1, keepdims=True) + EPS)


def _norm_mod(x, nw, sh, sc):
    return (x * _rms_r(x) * nw) * (1.0 + sc) + sh


def _norm_mod_bwd(dh, x, nw, sc):
    r = _rms_r(x)
    xn = x * r
    dxn = dh * (nw * (1.0 + sc))
    dx = r * (dxn - xn * jnp.mean(dxn * xn, axis=-1, keepdims=True))
    return dx, dh, dh * (xn * nw), dh * ((1.0 + sc) * xn)


def _sigmoid(x):
    return 1.0 / (1.0 + jnp.exp(-x))


def _group_of(i, n_lat_tiles):
    return jnp.where(i >= n_lat_tiles, 1, 0)


def _lane_half(shape):
    return (lax.broadcasted_iota(jnp.int32, shape, len(shape) - 1) % LANES) // HEAD_DIM


def _lane_lt64(shape):
    return _lane_half(shape) == 0


def _roll64(x):
    return pltpu.roll(x, HEAD_DIM, x.ndim - 1)


def norm_mod_call(x, nw, modv, n_lat_tiles, which):
    T, D = x.shape

    def body(x_ref, nw_ref, mod_ref, h_ref):
        sh = mod_ref[0, 3 * which:3 * which + 1, :]
        sc = mod_ref[0, 3 * which + 1:3 * which + 2, :]
        h_ref[...] = _norm_mod(x_ref[...], nw_ref[...], sh, sc).astype(BF16)

    return pl.pallas_call(
        body, name="norm_mod", grid=(T // TM,),
        in_specs=[pl.BlockSpec((TM, D), lambda i: (i, 0)),
                  pl.BlockSpec((1, D), lambda i: (0, 0)),
                  pl.BlockSpec((1, 6, D), lambda i: (_group_of(i, n_lat_tiles), 0, 0))],
        out_specs=pl.BlockSpec((TM, D), lambda i: (i, 0)),
        out_shape=jax.ShapeDtypeStruct((T, D), BF16),
        compiler_params=_cp(("parallel",)),
    )(x, nw, modv)


def qkv_call(h, w_in):
    T, D = h.shape
    N = w_in.shape[1]
    n_na = 3 * NA_HEADS * HEAD_DIM
    n_g = (GQA_Q_HEADS + 2 * GQA_KV_HEADS) * HEAD_DIM
    n_gate = N - n_na - n_g

    def body(h_ref, w_ref, na_ref, g_ref, gate_ref):
        acc = _dot(h_ref[...], w_ref[...])
        na_ref[...] = acc[:, :n_na].astype(BF16)
        g_ref[...] = acc[:, n_na:n_na + n_g].astype(BF16)
        gate_ref[...] = acc[:, n_na + n_g:].astype(BF16)

    return pl.pallas_call(
        body, name="qkv", grid=(T // TM,),
        in_specs=[pl.BlockSpec((TM, D), lambda i: (i, 0)),
                  pl.BlockSpec((D, N), lambda i: (0, 0))],
        out_specs=[pl.BlockSpec((TM, n_na), lambda i: (i, 0)),
                   pl.BlockSpec((TM, n_g), lambda i: (i, 0)),
                   pl.BlockSpec((TM, n_gate), lambda i: (i, 0))],
        out_shape=[jax.ShapeDtypeStruct((T, n_na), BF16),
                   jax.ShapeDtypeStruct((T, n_g), BF16),
                   jax.ShapeDtypeStruct((T, n_gate), BF16)],
        compiler_params=_cp(("parallel",), VMEM_BIG),
    )(h, w_in)


def merge_call(ya, yb, gates, w_pa, w_pb):
    T, Ka = ya.shape
    D = w_pa.shape[1]

    def body(ya_ref, yb_ref, ga_ref, gb_ref, wa_ref, wb_ref, m_ref, pa_ref, pb_ref):
        pa = _dot(ya_ref[...], wa_ref[...])
        pb = _dot(yb_ref[...], wb_ref[...])
        m = _sigmoid(ga_ref[...].astype(F32)) * pa + _sigmoid(gb_ref[...].astype(F32)) * pb
        m_ref[...] = m.astype(BF16)
        pa_ref[...] = pa.astype(BF16)
        pb_ref[...] = pb.astype(BF16)

    row = lambda w: pl.BlockSpec((TM, w), lambda i: (i, 0))
    return pl.pallas_call(
        body, name="merge", grid=(T // TM,),
        in_specs=[row(Ka), row(Ka),
                  pl.BlockSpec((TM, D), lambda i: (i, 0)), pl.BlockSpec((TM, D), lambda i: (i, 1)),
                  pl.BlockSpec((Ka, D), lambda i: (0, 0)), pl.BlockSpec((Ka, D), lambda i: (0, 0))],
        out_specs=[row(D), row(D), row(D)],
        out_shape=[jax.ShapeDtypeStruct((T, D), BF16)] * 3,
        compiler_params=_cp(("parallel",)),
    )(ya, yb, gates, gates, w_pa, w_pb)


def wo_call(merged, w_o, x, modv, nw2, n_lat_tiles):
    T, D = x.shape

    def body(m_ref, w_ref, x_ref, mod_ref, nw_ref, x1_ref, o_ref, h2_ref):
        o = _dot(m_ref[...], w_ref[...])
        x1 = x_ref[...] + mod_ref[0, 2:3, :] * o
        x1_ref[...] = x1
        o_ref[...] = o.astype(BF16)
        h2_ref[...] = _norm_mod(x1, nw_ref[...], mod_ref[0, 3:4, :], mod_ref[0, 4:5, :]).astype(BF16)

    row = pl.BlockSpec((TM, D), lambda i: (i, 0))
    return pl.pallas_call(
        body, name="wo", grid=(T // TM,),
        in_specs=[row, pl.BlockSpec((D, D), lambda i: (0, 0)), row,
                  pl.BlockSpec((1, 6, D), lambda i: (_group_of(i, n_lat_tiles), 0, 0)),
                  pl.BlockSpec((1, D), lambda i: (0, 0))],
        out_specs=[row, row, row],
        out_shape=[jax.ShapeDtypeStruct((T, D), F32), jax.ShapeDtypeStruct((T, D), BF16),
                   jax.ShapeDtypeStruct((T, D), BF16)],
        compiler_params=_cp(("parallel",)),
    )(merged, w_o, x, modv, nw2)


def ffn_in_call(h2, w_ffn_in, n_col_tiles=2):
    T, D = h2.shape
    F = w_ffn_in.shape[1] // 2
    tn = F // n_col_tiles

    def body(h_ref, wa_ref, wu_ref, act_ref, a_ref, u_ref):
        h = h_ref[...]
        a = _dot(h, wa_ref[...])
        u = _dot(h, wu_ref[...])
        act_ref[...] = (a * _sigmoid(a) * u).astype(BF16)
        a_ref[...] = a.astype(BF16)
        u_ref[...] = u.astype(BF16)

    blk = pl.BlockSpec((TM, tn), lambda j, i: (i, j))
    return pl.pallas_call(
        body, name="ffn_in", grid=(n_col_tiles, T // TM),
        in_specs=[pl.BlockSpec((TM, D), lambda j, i: (i, 0)),
                  pl.BlockSpec((D, tn), lambda j, i: (0, j)),
                  pl.BlockSpec((D, tn), lambda j, i: (0, n_col_tiles + j))],
        out_specs=[blk, blk, blk],
        out_shape=[jax.ShapeDtypeStruct((T, F), BF16)] * 3,
        compiler_params=_cp(("parallel", "parallel"), VMEM_BIG),
    )(h2, w_ffn_in, w_ffn_in)


def ffn_out_call(act, w_ffn_out, x1, modv, n_lat_tiles, next_nw=None, next_modv=None):
    T, D = x1.shape
    F = act.shape[1]
    with_next = next_nw is not None

    def body(*refs):
        if with_next:
            a_ref, w_ref, x_ref, mod_ref, nw_ref, nmod_ref, x2_ref, f_ref, hn_ref = refs
        else:
            a_ref, w_ref, x_ref, mod_ref, x2_ref, f_ref = refs
        f = _dot(a_ref[...], w_ref[...])
        x2 = x_ref[...] + mod_ref[0, 5:6, :] * f
        x2_ref[...] = x2
        f_ref[...] = f.astype(BF16)
        if with_next:
            hn_ref[...] = _norm_mod(x2, nw_ref[...], nmod_ref[0, 0:1, :], nmod_ref[0, 1:2, :]).astype(BF16)

    row = pl.BlockSpec((TM, D), lambda i: (i, 0))
    modspec = pl.BlockSpec((1, 6, D), lambda i: (_group_of(i, n_lat_tiles), 0, 0))
    in_specs = [pl.BlockSpec((TM, F), lambda i: (i, 0)), pl.BlockSpec((F, D), lambda i: (0, 0)), row, modspec]
    args = [act, w_ffn_out, x1, modv]
    out_specs = [row, row]
    out_shape = [jax.ShapeDtypeStruct((T, D), F32), jax.ShapeDtypeStruct((T, D), BF16)]
    if with_next:
        in_specs += [pl.BlockSpec((1, D), lambda i: (0, 0)), modspec]
        args += [next_nw, next_modv]
        out_specs.append(row)
        out_shape.append(jax.ShapeDtypeStruct((T, D), BF16))
    return pl.pallas_call(
        body, name="ffn_out", grid=(T // TM,), in_specs=in_specs, out_specs=out_specs, out_shape=out_shape,
        compiler_params=_cp(("parallel",), VMEM_BIG),
    )(*args)


def loss_call(x, nw, target, n_lat_tiles):
    T, D = x.shape

    def body(x_ref, nw_ref, t_ref, dx_ref, dw_ref, ls_ref):
        i = pl.program_id(0)

        @pl.when(i < n_lat_tiles)
        def _():
            xv = x_ref[...]
            nw_v = nw_ref[...]
            r = _rms_r(xv)
            xn = xv * r
            e = xn * nw_v - t_ref[...]
            dy = e * (1.0 / D)
            dxn = dy * nw_v
            dx_ref[...] = r * (dxn - xn * jnp.mean(dxn * xn, axis=-1, keepdims=True))
            _acc(dw_ref, _colsum8(dy * xn), i == 0)
            _acc(ls_ref, _colsum8(e * e * (0.5 / D)), i == 0)

        @pl.when(i >= n_lat_tiles)
        def _():
            dx_ref[...] = jnp.zeros((TM, D), F32)

    stat = pl.BlockSpec((1, 8, D), lambda i: (0, 0, 0))
    return pl.pallas_call(
        body, name="loss", grid=(T // TM,),
        in_specs=[pl.BlockSpec((TM, D), lambda i: (i, 0)), pl.BlockSpec((1, D), lambda i: (0, 0)),
                  pl.BlockSpec((TM, D), lambda i: (jnp.minimum(i, n_lat_tiles - 1), 0))],
        out_specs=[pl.BlockSpec((TM, D), lambda i: (i, 0)), stat, stat],
        out_shape=[jax.ShapeDtypeStruct((T, D), F32), jax.ShapeDtypeStruct((1, 8, D), F32),
                   jax.ShapeDtypeStruct((1, 8, D), F32)],
        compiler_params=_cp(("arbitrary",)),
    )(x, nw, target)


def dact_call(dx2, modv, fsave, w_ffn_out, a, u, n_lat_tiles):
    T, D = dx2.shape
    F = a.shape[1]

    def body(dx_ref, mod_ref, f_ref, w_ref, a_ref, u_ref, da_ref, du_ref, df_ref, dg_ref):
        i = pl.program_id(0)
        dx = dx_ref[...]
        df = (dx * mod_ref[0, 5:6, :]).astype(BF16)
        df_ref[...] = df
        dact = _dot_nt(df, w_ref[...])
        av = a_ref[...].astype(F32)
        uv = u_ref[...].astype(F32)
        sg = _sigmoid(av)
        da_ref[...] = (dact * uv * (sg * (1.0 + av * (1.0 - sg)))).astype(BF16)
        du_ref[...] = (dact * (av * sg)).astype(BF16)
        _acc(dg_ref, _colsum8(dx * f_ref[...].astype(F32)), (i == 0) | (i == n_lat_tiles))

    row = pl.BlockSpec((TM, D), lambda i: (i, 0))
    wide = pl.BlockSpec((TM, F), lambda i: (i, 0))
    grp = lambda i: (_group_of(i, n_lat_tiles), 0, 0)
    return pl.pallas_call(
        body, name="dact", grid=(T // TM,),
        in_specs=[row, pl.BlockSpec((1, 6, D), grp), row, pl.BlockSpec((F, D), lambda i: (0, 0)), wide, wide],
        out_specs=[wide, wide, row, pl.BlockSpec((1, 8, D), grp)],
        out_shape=[jax.ShapeDtypeStruct((T, F), BF16), jax.ShapeDtypeStruct((T, F), BF16),
                   jax.ShapeDtypeStruct((T, D), BF16), jax.ShapeDtypeStruct((2, 8, D), F32)],
        compiler_params=_cp(("arbitrary",), VMEM_BIG),
    )(dx2, modv, fsave, w_ffn_out, a, u)


def _norm_bwd_tail(dh, x_ref, mod_ref, nw_ref, dres_ref, dx_ref, s_sh, s_sc, s_w, which, first):
    sc = mod_ref[0, 3 * which + 1:3 * which + 2, :]
    dxn, t_sh, t_sc, t_w = _norm_mod_bwd(dh, x_ref[...], nw_ref[...], sc)
    dx_ref[...] = dres_ref[...] + dxn
    _acc(s_sh, _colsum8(t_sh), first)
    _acc(s_sc, _colsum8(t_sc), first)
    _acc(s_w, _colsum8(t_w), first)


def dh2_call(da, du, w_ffn_in, x1, modv, nw2, dx2, n_lat_tiles):
    T, D = x1.shape
    F = da.shape[1]

    def body(da_ref, du_ref, wa_ref, wu_ref, x_ref, mod_ref, nw_ref, dres_ref, dx_ref, s_sh, s_sc, s_w):
        i = pl.program_id(0)
        dh = _dot_nt(da_ref[...], wa_ref[...]) + _dot_nt(du_ref[...], wu_ref[...])
        _norm_bwd_tail(dh, x_ref, mod_ref, nw_ref, dres_ref, dx_ref, s_sh, s_sc, s_w, 1,
                       (i == 0) | (i == n_lat_tiles))

    row = pl.BlockSpec((TM, D), lambda i: (i, 0))
    wide = pl.BlockSpec((TM, F), lambda i: (i, 0))
    grp = lambda i: (_group_of(i, n_lat_tiles), 0, 0)
    stat = pl.BlockSpec((1, 8, D), grp)
    return pl.pallas_call(
        body, name="dh2", grid=(T // TM,),
        in_specs=[wide, wide, pl.BlockSpec((D, F), lambda i: (0, 0)), pl.BlockSpec((D, F), lambda i: (0, 1)),
                  row, pl.BlockSpec((1, 6, D), grp), pl.BlockSpec((1, D), lambda i: (0, 0)), row],
        out_specs=[row, stat, stat, stat],
        out_shape=[jax.ShapeDtypeStruct((T, D), F32)] + [jax.ShapeDtypeStruct((2, 8, D), F32)] * 3,
        compiler_params=_cp(("arbitrary",), VMEM_BIG),
    )(da, du, w_ffn_in, w_ffn_in, x1, modv, nw2, dx2)


def dmerged_call(dx1, modv, osave, w_o, pa, pb, gates, n_lat_tiles):
    T, D = dx1.shape

    def body(dx_ref, mod_ref, o_ref, w_ref, pa_ref, pb_ref, ga_ref, gb_ref,
             dpa_ref, dpb_ref, dgate_ref, do_ref, dg_ref):
        i = pl.program_id(0)
        dx = dx_ref[...]
        do = (dx * mod_ref[0, 2:3, :]).astype(BF16)
        do_ref[...] = do
        dm = _dot_nt(do, w_ref[...])
        sa = _sigmoid(ga_ref[...].astype(F32))
        sb = _sigmoid(gb_ref[...].astype(F32))
        dpa_ref[...] = (dm * sa).astype(BF16)
        dpb_ref[...] = (dm * sb).astype(BF16)
        dgate_ref[:, :D] = (dm * pa_ref[...].astype(F32) * (sa * (1.0 - sa))).astype(BF16)
        dgate_ref[:, D:] = (dm * pb_ref[...].astype(F32) * (sb * (1.0 - sb))).astype(BF16)
        _acc(dg_ref, _colsum8(dx * o_ref[...].astype(F32)), (i == 0) | (i == n_lat_tiles))

    row = pl.BlockSpec((TM, D), lambda i: (i, 0))
    grp = lambda i: (_group_of(i, n_lat_tiles), 0, 0)
    return pl.pallas_call(
        body, name="dmerged", grid=(T // TM,),
        in_specs=[row, pl.BlockSpec((1, 6, D), grp), row, pl.BlockSpec((D, D), lambda i: (0, 0)), row, row,
                  pl.BlockSpec((TM, D), lambda i: (i, 0)), pl.BlockSpec((TM, D), lambda i: (i, 1))],
        out_specs=[row, row, pl.BlockSpec((TM, 2 * D), lambda i: (i, 0)), row, pl.BlockSpec((1, 8, D), grp)],
        out_shape=[jax.ShapeDtypeStruct((T, D), BF16), jax.ShapeDtypeStruct((T, D), BF16),
                   jax.ShapeDtypeStruct((T, 2 * D), BF16), jax.ShapeDtypeStruct((T, D), BF16),
                   jax.ShapeDtypeStruct((2, 8, D), F32)],
        compiler_params=_cp(("arbitrary",)),
    )(dx1, modv, osave, w_o, pa, pb, gates, gates)


def dy_call(dpa, dpb, w_pa, w_pb):
    T, D = dpa.shape
    K = w_pa.shape[0]

    def body(a_ref, b_ref, wa_ref, wb_ref, ya_ref, yb_ref):
        ya_ref[...] = _dot_nt(a_ref[...], wa_ref[...]).astype(BF16)
        yb_ref[...] = _dot_nt(b_ref[...], wb_ref[...]).astype(BF16)

    row = pl.BlockSpec((TM, D), lambda i: (i, 0))
    w = pl.BlockSpec((K, D), lambda i: (0, 0))
    out = pl.BlockSpec((TM, K), lambda i: (i, 0))
    return pl.pallas_call(
        body, name="dy", grid=(T // TM,), in_specs=[row, row, w, w], out_specs=[out, out],
        out_shape=[jax.ShapeDtypeStruct((T, K), BF16)] * 2,
        compiler_params=_cp(("parallel",)),
    )(dpa, dpb, w_pa, w_pb)


def dh_call(dna, dg, dgates, w_in, x, modv, nw1, dx1, n_lat_tiles):
    T, D = x.shape
    n1, n2, n3 = dna.shape[1], dg.shape[1], dgates.shape[1]
    N = n1 + n2 + n3

    def body(a_ref, b_ref, c_ref, w_ref, x_ref, mod_ref, nw_ref, dres_ref, dx_ref, s_sh, s_sc, s_w):
        i = pl.program_id(0)
        dh = (_dot_nt(a_ref[...], w_ref[:, :n1]) + _dot_nt(b_ref[...], w_ref[:, n1:n1 + n2])
              + _dot_nt(c_ref[...], w_ref[:, n1 + n2:]))
        _norm_bwd_tail(dh, x_ref, mod_ref, nw_ref, dres_ref, dx_ref, s_sh, s_sc, s_w, 0,
                       (i == 0) | (i == n_lat_tiles))

    row = pl.BlockSpec((TM, D), lambda i: (i, 0))
    grp = lambda i: (_group_of(i, n_lat_tiles), 0, 0)
    stat = pl.BlockSpec((1, 8, D), grp)
    return pl.pallas_call(
        body, name="dh", grid=(T // TM,),
        in_specs=[pl.BlockSpec((TM, n1), lambda i: (i, 0)), pl.BlockSpec((TM, n2), lambda i: (i, 0)),
                  pl.BlockSpec((TM, n3), lambda i: (i, 0)), pl.BlockSpec((D, N), lambda i: (0, 0)),
                  row, pl.BlockSpec((1, 6, D), grp), pl.BlockSpec((1, D), lambda i: (0, 0)), row],
        out_specs=[row, stat, stat, stat],
        out_shape=[jax.ShapeDtypeStruct((T, D), F32)] + [jax.ShapeDtypeStruct((2, 8, D), F32)] * 3,
        compiler_params=_cp(("arbitrary",), VMEM_BIG),
    )(dna, dg, dgates, w_in, x, modv, nw1, dx1)


def _pick_tile(n, cap, mult):
    best = None
    for d in range(mult, min(n, cap) + 1, mult):
        if n % d == 0:
            best = d
    assert best is not None, (n, cap, mult)
    return best


def wgrad_call(a, b, name):
    T, K = a.shape
    N = b.shape[1]
    tt = _pick_tile(T, 768, 128)
    tk = K if K <= 1024 else _pick_tile(K, 1408, 128)
    tn = N if N <= 2304 else _pick_tile(N, 2304, 128)
    nt = T // tt

    def body(a_ref, b_ref, o_ref, acc_ref):
        t = pl.program_id(2)
        p = _dot_tn(a_ref[...], b_ref[...])

        @pl.when(t == 0)
        def _():
            acc_ref[...] = p

        @pl.when(t > 0)
        def _():
            acc_ref[...] += p

        @pl.when(t == nt - 1)
        def _():
            o_ref[...] = acc_ref[...].astype(BF16)

    return pl.pallas_call(
        body, name=name, grid=(K // tk, N // tn, nt),
        in_specs=[pl.BlockSpec((tt, tk), lambda i, j, t: (t, i)), pl.BlockSpec((tt, tn), lambda i, j, t: (t, j))],
        out_specs=pl.BlockSpec((tk, tn), lambda i, j, t: (i, j)),
        out_shape=jax.ShapeDtypeStruct((K, N), BF16),
        scratch_shapes=[pltpu.VMEM((tk, tn), F32)],
        compiler_params=_cp(("parallel", "parallel", "arbitrary"), VMEM_BIG),
    )(a, b)


def _head_mean(v):
    n = v.shape[-1]
    r = lax.broadcasted_iota(jnp.int32, (n, n), 0) // HEAD_DIM
    c = lax.broadcasted_iota(jnp.int32, (n, n), 1) // HEAD_DIM
    bd = jnp.where(r == c, 1.0 / HEAD_DIM, 0.0).astype(BF16)
    hi = v.astype(BF16)
    lo = (v - hi.astype(F32)).astype(BF16)
    return _dot(hi, bd) + _dot(lo, bd)


def _swap_pairs(x):
    n = x.shape[-1]
    even = lax.broadcasted_iota(jnp.int32, x.shape, x.ndim - 1) % 2 == 0
    return jnp.where(even, pltpu.roll(x, n - 1, x.ndim - 1), pltpu.roll(x, 1, x.ndim - 1))


def _tile_lanes(t, n):
    return jnp.tile(t, (1, n // t.shape[-1]))


def gqa_prep_call(gqkv, qgain, kgain, cos_t, sin_t):
    T = gqkv.shape[0]
    nq = GQA_Q_HEADS * HEAD_DIM
    nk = GQA_KV_HEADS * HEAD_DIM
    rep = GQA_Q_HEADS // GQA_KV_HEADS

    def body(g_ref, qg_ref, kg_ref, c_ref, s_ref, qp_ref, kp_ref):
        cos_v, sin_v = c_ref[...], s_ref[...]
        xq = g_ref[:, :nq].astype(F32)
        yq = xq * lax.rsqrt(_head_mean(xq * xq) + EPS) * qg_ref[...]
        yq = (yq * _tile_lanes(cos_v, nq) + _swap_pairs(yq) * _tile_lanes(sin_v, nq)) * SCALE
        lo = _lane_lt64((TM, LANES))
        for p in range(GQA_Q_HEADS // 2):
            g = (2 * p) // rep
            chunk = yq[:, LANES * p:LANES * (p + 1)]
            rolled = _roll64(chunk)
            if g == 0:
                first, second = jnp.where(lo, chunk, 0.0), jnp.where(lo, rolled, 0.0)
            else:
                first, second = jnp.where(lo, 0.0, rolled), jnp.where(lo, 0.0, chunk)
            qp_ref[:, 2 * LANES * p:2 * LANES * p + LANES] = first.astype(BF16)
            qp_ref[:, 2 * LANES * p + LANES:2 * LANES * (p + 1)] = second.astype(BF16)
        xk = g_ref[:, nq:nq + nk].astype(F32)
        yk = xk * lax.rsqrt(_head_mean(xk * xk) + EPS) * kg_ref[...]
        kp_ref[...] = (yk * cos_v + _swap_pairs(yk) * sin_v).astype(BF16)

    return pl.pallas_call(
        body, name="gqa_prep", grid=(T // TM,),
        in_specs=[pl.BlockSpec((TM, gqkv.shape[1]), lambda i: (i, 0)),
                  pl.BlockSpec((1, nq), lambda i: (0, 0)), pl.BlockSpec((1, nk), lambda i: (0, 0)),
                  pl.BlockSpec((TM, nk), lambda i: (i, 0)), pl.BlockSpec((TM, nk), lambda i: (i, 0))],
        out_specs=[pl.BlockSpec((TM, GQA_Q_HEADS * LANES), lambda i: (i, 0)), pl.BlockSpec((TM, nk), lambda i: (i, 0))],
        out_shape=[jax.ShapeDtypeStruct((T, GQA_Q_HEADS * LANES), BF16), jax.ShapeDtypeStruct((T, nk), BF16)],
        compiler_params=_cp(("parallel",)),
    )(gqkv, qgain, kgain, cos_t, sin_t)


def gqa_prep_bwd_call(gqkv, qgain, kgain, cos_t, sin_t, dqp, dkp, dv):
    T = gqkv.shape[0]
    nq = GQA_Q_HEADS * HEAD_DIM
    nk = GQA_KV_HEADS * HEAD_DIM
    rep = GQA_Q_HEADS // GQA_KV_HEADS

    def norm_rope_bwd(x, gain, dy, cos_v, sin_v):
        dyn = dy * cos_v - _swap_pairs(dy) * sin_v
        r = lax.rsqrt(_head_mean(x * x) + EPS)
        xn = x * r
        dxn = dyn * gain
        dx = r * (dxn - xn * _head_mean(dxn * xn))
        return dx, dyn * xn

    def body(g_ref, qg_ref, kg_ref, c_ref, s_ref, dqp_ref, dkp_ref, dv_ref, dg_ref, sq_ref, sk_ref):
        i = pl.program_id(0)
        cos_v, sin_v = c_ref[...], s_ref[...]
        lo = _lane_lt64((TM, LANES))
        chunks = []
        for p in range(GQA_Q_HEADS // 2):
            g = (2 * p) // rep
            da = dqp_ref[:, 2 * LANES * p:2 * LANES * p + LANES]
            db = dqp_ref[:, 2 * LANES * p + LANES:2 * LANES * (p + 1)]
            if g == 0:
                chunks.append(jnp.where(lo, da, _roll64(db)))
            else:
                chunks.append(jnp.where(lo, _roll64(da), db))
        dyq = jnp.concatenate(chunks, axis=1) * SCALE
        dxq, gq_term = norm_rope_bwd(g_ref[:, :nq].astype(F32), qg_ref[...], dyq,
                                     _tile_lanes(cos_v, nq), _tile_lanes(sin_v, nq))
        dxk, gk_term = norm_rope_bwd(g_ref[:, nq:nq + nk].astype(F32), kg_ref[...], dkp_ref[...], cos_v, sin_v)
        dg_ref[:, :nq] = dxq.astype(BF16)
        dg_ref[:, nq:nq + nk] = dxk.astype(BF16)
        dg_ref[:, nq + nk:] = dv_ref[...].astype(BF16)
        _acc(sq_ref, _colsum8(gq_term), i == 0)
        _acc(sk_ref, _colsum8(gk_term), i == 0)

    W = gqkv.shape[1]
    return pl.pallas_call(
        body, name="gqa_prep_bwd", grid=(T // TM,),
        in_specs=[pl.BlockSpec((TM, W), lambda i: (i, 0)),
                  pl.BlockSpec((1, nq), lambda i: (0, 0)), pl.BlockSpec((1, nk), lambda i: (0, 0)),
                  pl.BlockSpec((TM, nk), lambda i: (i, 0)), pl.BlockSpec((TM, nk), lambda i: (i, 0)),
                  pl.BlockSpec((TM, GQA_Q_HEADS * LANES), lambda i: (i, 0)),
                  pl.BlockSpec((TM, nk), lambda i: (i, 0)), pl.BlockSpec((TM, nk), lambda i: (i, 0))],
        out_specs=[pl.BlockSpec((TM, W), lambda i: (i, 0)),
                   pl.BlockSpec((1, 8, nq), lambda i: (0, 0, 0)), pl.BlockSpec((1, 8, nk), lambda i: (0, 0, 0))],
        out_shape=[jax.ShapeDtypeStruct((T, W), BF16), jax.ShapeDtypeStruct((1, 8, nq), F32),
                   jax.ShapeDtypeStruct((1, 8, nk), F32)],
        compiler_params=_cp(("arbitrary",)),
    )(gqkv, qgain, kgain, cos_t, sin_t, dqp, dkp, dv)


GQ_TQ_LATENT = 512
GQ_TK = 256
GQ_CHUNKS_PER_TRIP = 8


def _chunk_offset(j):
    return j * GQ_TK if isinstance(j, int) else pl.multiple_of(j * GQ_TK, GQ_TK)


def _gqa_blocks(n_lat, T, latent):
    if latent:
        tq = GQ_TQ_LATENT if n_lat % GQ_TQ_LATENT == 0 else GQ_TK
        return tq, 0, n_lat // tq
    return GQ_TK, n_lat // GQ_TK, (T - n_lat) // GQ_TK


def _sweep_keys(chunk, latent, n_lat_k, nk):
    if latent:
        per_trip = math.gcd(n_lat_k, GQ_CHUNKS_PER_TRIP)

        def trip(jj, carry):
            for t in range(per_trip):
                chunk(per_trip * jj + t)
            return carry

        lax.fori_loop(0, n_lat_k // per_trip, trip, 0)
    for j in range(n_lat_k, nk):
        chunk(j)


def _gqa_place(chunk, half, g):
    gv = jnp.broadcast_to(g, chunk.shape)
    placed = jnp.where(gv == half, chunk, _roll64(chunk))
    return jnp.where(_lane_half(chunk.shape) == gv, placed, jnp.zeros_like(placed))


GQA_SMALL_SCORE = 20.0


def gqa_scores_small(q_gain, k_gain):
    bound = SCALE * HEAD_DIM * jnp.max(jnp.abs(q_gain)) * jnp.max(jnp.abs(k_gain))
    return (bound <= GQA_SMALL_SCORE).astype(jnp.int32).reshape(1)


def gqa_fwd_call(qp, kp, gqkv, n_lat, scores_small, gather=None):
    first = _gqa_fwd_part(qp, kp, gqkv, n_lat, scores_small, True, None, gather)
    y, lse = _gqa_fwd_part(qp, kp, gqkv, n_lat, scores_small, False, (first[0], first[1]), None)
    return (y, lse) + tuple(first[2:])


def _gqa_fwd_part(qp, kp, gqkv, n_lat, scores_small, latent, prev, gather):
    T = qp.shape[0]
    rep = GQA_Q_HEADS // GQA_KV_HEADS
    GQ_TQ, q0, nq = _gqa_blocks(n_lat, T, latent)
    nk = T // GQ_TK
    n_lat_k = n_lat // GQ_TK
    v_blk = (GQA_Q_HEADS + GQA_KV_HEADS) * HEAD_DIM // LANES
    R = rep * GQ_TQ

    def body(*refs):
        if prev is not None:
            small_ref, q_ref, k_ref, v_ref, _, _, y_ref, lse_ref, m_sc, acc_sc = refs
        elif gather is None:
            small_ref, q_ref, k_ref, v_ref, y_ref, lse_ref, m_sc, acc_sc = refs
        else:
            small_ref, q_ref, k_ref, v_ref, pk_ref, y_ref, lse_ref, ga_ref, m_sc, acc_sc, ssem, rsem = refs
        g = pl.program_id(0)
        qb = pl.program_id(1)
        if gather is not None:
            @pl.when((g == 0) & (qb == 0))
            def _():
                _ag_start(pk_ref, ga_ref, ssem, rsem)
        qs = jnp.concatenate([q_ref[:, LANES * i:LANES * (i + 1)] for i in range(rep)], axis=0)
        is_lat = latent
        nsub = GQ_TK // LANES
        scores_small = small_ref[0] == 1

        @pl.when(scores_small)
        def _():
            m_sc[...] = jnp.zeros((R, LANES), F32)

        @pl.when(jnp.logical_not(scores_small))
        def _():
            m_sc[...] = jnp.full((R, LANES), NEG, F32)

            def chunk_max(j):
                s = _dot_nt(qs, k_ref[pl.ds(_chunk_offset(j), GQ_TK), :])
                mm = s[:, :LANES]
                for t in range(1, nsub):
                    mm = jnp.maximum(mm, s[:, LANES * t:LANES * (t + 1)])
                m_sc[...] = jnp.maximum(m_sc[...], mm)

            _sweep_keys(chunk_max, is_lat, n_lat_k, nk)
            m_sc[...] = jnp.broadcast_to(jnp.max(m_sc[...], axis=-1, keepdims=True), (R, LANES))

        acc_sc[...] = jnp.zeros((R, LANES), F32)
        own_v = _lane_half((GQ_TK, LANES)) == jnp.broadcast_to(g, (GQ_TK, LANES))

        def chunk_acc(j):
            off = _chunk_offset(j)
            vc = v_ref[pl.ds(off, GQ_TK), :]
            v1 = jnp.where(own_v, vc, jnp.ones_like(vc))
            s = _dot_nt(qs, k_ref[pl.ds(off, GQ_TK), :])
            mb = m_sc[...]
            p = jnp.concatenate([jnp.exp(s[:, LANES * t:LANES * (t + 1)] - mb).astype(BF16) for t in range(nsub)],
                                axis=1)
            acc_sc[...] += _dot(p, v1)

        _sweep_keys(chunk_acc, is_lat, n_lat_k, nk)
        acc = acc_sc[...]
        own = _lane_half((R, LANES)) == jnp.broadcast_to(g, (R, LANES))
        l = jnp.where(own, _roll64(acc), acc)
        o = acc / l
        lse = m_sc[...] + jnp.log(l)
        lo = _lane_lt64((GQ_TQ, LANES))
        g0 = jnp.broadcast_to(g, (GQ_TQ, LANES)) == 0
        for c2 in range(rep // 2):
            oa = o[(2 * c2) * GQ_TQ:(2 * c2 + 1) * GQ_TQ]
            ob = o[(2 * c2 + 1) * GQ_TQ:(2 * c2 + 2) * GQ_TQ]
            left = jnp.where(g0, oa, _roll64(oa))
            right = jnp.where(g0, _roll64(ob), ob)
            y_ref[:, LANES * c2:LANES * (c2 + 1)] = jnp.where(lo, left, right).astype(BF16)
        for i in range(rep):
            lse_ref[i] = lse[i * GQ_TQ:(i + 1) * GQ_TQ]
        if gather is not None:
            @pl.when((g == GQA_KV_HEADS - 1) & (qb == nq - 1))
            def _():
                _ag_finish(pk_ref, ga_ref, ssem, rsem)

    in_specs = [pl.BlockSpec((GQ_TQ, rep * LANES), lambda g, i, s: (q0 + i, g)),
                pl.BlockSpec((T, LANES), lambda g, i, s: (0, 0)),
                pl.BlockSpec((T, LANES), lambda g, i, s: (0, v_blk))]
    out_specs = [pl.BlockSpec((GQ_TQ, rep * HEAD_DIM), lambda g, i, s: (q0 + i, g)),
                 pl.BlockSpec((rep, GQ_TQ, LANES), lambda g, i, s: (g, q0 + i, 0))]
    out_shape = [jax.ShapeDtypeStruct((T, GQA_Q_HEADS * HEAD_DIM), BF16),
                 jax.ShapeDtypeStruct((GQA_Q_HEADS, T, LANES), F32)]
    scratch = [pltpu.VMEM((R, LANES), F32), pltpu.VMEM((R, LANES), F32)]
    args = [scores_small, qp, kp, gqkv]
    aliases = {}
    name = "gqa_fwd" if latent else "gqa_fwd_ctx"
    if prev is not None:
        in_specs += [ANY, ANY]
        args += list(prev)
        aliases = {4: 0, 5: 1}
    if gather is not None:
        in_specs.append(ANY)
        out_specs.append(ANY)
        out_shape.append(jax.ShapeDtypeStruct((N_CHIP,) + gather.shape, gather.dtype))
        scratch += [pltpu.SemaphoreType.DMA((6,)), pltpu.SemaphoreType.DMA((6,))]
        args.append(gather)
        name = "gqa_fwd_gather"
    return pl.pallas_call(
        body, name=name,
        grid_spec=pltpu.PrefetchScalarGridSpec(
            num_scalar_prefetch=1, grid=(GQA_KV_HEADS, nq), in_specs=in_specs, out_specs=out_specs,
            scratch_shapes=scratch),
        out_shape=out_shape, input_output_aliases=aliases,
        compiler_params=_cp(("arbitrary", "arbitrary") if gather is not None else ("parallel", "parallel"), VMEM_BIG),
    )(*args)


def gqa_bwd_call(qp, kp, gqkv, yb, dyb, lse, n_lat, exchange=None):
    first = _gqa_bwd_part(qp, kp, gqkv, yb, dyb, lse, n_lat, True, None, exchange)
    dq, dk_c, dv_c = _gqa_bwd_part(qp, kp, gqkv, yb, dyb, lse, n_lat, False, first[0], None)
    return (dq, first[1] + dk_c, first[2] + dv_c) + tuple(first[3:])


def _gqa_bwd_part(qp, kp, gqkv, yb, dyb, lse, n_lat, latent, prev_dq, exchange):
    T = qp.shape[0]
    rep = GQA_Q_HEADS // GQA_KV_HEADS
    GQ_TQ, q0, nq = _gqa_blocks(n_lat, T, latent)
    nk = T // GQ_TK
    n_lat_k = n_lat // GQ_TK
    v_blk = (GQA_Q_HEADS + GQA_KV_HEADS) * HEAD_DIM // LANES
    R = rep * GQ_TQ

    def body(*refs):
        if prev_dq is not None:
            q_ref, k_ref, v_ref, y_ref, dy_ref, lse_ref, _, dq_ref, dk_ref, dv_ref, dq_sc = refs
        elif exchange is None:
            q_ref, k_ref, v_ref, y_ref, dy_ref, lse_ref, dq_ref, dk_ref, dv_ref, dq_sc = refs
        else:
            (q_ref, k_ref, v_ref, y_ref, dy_ref, lse_ref, gp_ref, dq_ref, dk_ref, dv_ref, rx_ref, dq_sc,
             ssem, rsem) = refs
        g = pl.program_id(0)
        qb = pl.program_id(1)
        if exchange is not None:
            @pl.when((g == 0) & (qb == 0))
            def _():
                _rs_start(gp_ref, rx_ref, ssem, rsem)

        @pl.when((g == 0) & (qb == 0))
        def _():
            dk_ref[...] = jnp.zeros((T, LANES), F32)
            dv_ref[...] = jnp.zeros((T, LANES), F32)

        qs = jnp.concatenate([q_ref[:, LANES * i:LANES * (i + 1)] for i in range(rep)], axis=0)
        dos, deltas, lses = [], [], []
        for i in range(rep):
            c2, half = i // 2, i % 2
            dch = _gqa_place(dy_ref[:, LANES * c2:LANES * (c2 + 1)].astype(F32), half, g)
            ych = _gqa_place(y_ref[:, LANES * c2:LANES * (c2 + 1)].astype(F32), half, g)
            dos.append(dch.astype(BF16))
            deltas.append(jnp.sum(dch * ych, axis=-1, keepdims=True))
            lses.append(lse_ref[i])
        do_s = jnp.concatenate(dos, axis=0)
        delta = jnp.broadcast_to(jnp.concatenate(deltas, axis=0), (R, LANES))
        lse_v = jnp.concatenate(lses, axis=0)
        dq_sc[...] = jnp.zeros((R, LANES), F32)
        nsub = GQ_TK // LANES

        def chunk(j):
            off = _chunk_offset(j)
            kc = k_ref[pl.ds(off, GQ_TK), :]
            vc = v_ref[pl.ds(off, GQ_TK), :]
            s = _dot_nt(qs, kc)
            dp = _dot_nt(do_s, vc)
            ps, dss = [], []
            for t in range(nsub):
                pt = jnp.exp(s[:, LANES * t:LANES * (t + 1)] - lse_v)
                ps.append(pt.astype(BF16))
                dss.append((pt * (dp[:, LANES * t:LANES * (t + 1)] - delta)).astype(BF16))
            p = jnp.concatenate(ps, axis=1)
            ds = jnp.concatenate(dss, axis=1)
            dv_ref[pl.ds(off, GQ_TK), :] += _dot_tn(p, do_s)
            dk_ref[pl.ds(off, GQ_TK), :] += _dot_tn(ds, qs)
            dq_sc[...] += _dot(ds, kc)

        _sweep_keys(chunk, latent, n_lat_k, nk)
        for i in range(rep):
            dq_ref[:, LANES * i:LANES * (i + 1)] = dq_sc[i * GQ_TQ:(i + 1) * GQ_TQ, :]
        if exchange is not None:
            @pl.when((g == GQA_KV_HEADS - 1) & (qb == nq - 1))
            def _():
                _rs_finish(gp_ref, rx_ref, ssem, rsem)

    whole = pl.BlockSpec((T, LANES), lambda g, i: (0, 0))
    in_specs = [pl.BlockSpec((GQ_TQ, rep * LANES), lambda g, i: (q0 + i, g)),
                whole,
                pl.BlockSpec((T, LANES), lambda g, i: (0, v_blk)),
                pl.BlockSpec((GQ_TQ, rep * HEAD_DIM), lambda g, i: (q0 + i, g)),
                pl.BlockSpec((GQ_TQ, rep * HEAD_DIM), lambda g, i: (q0 + i, g)),
                pl.BlockSpec((rep, GQ_TQ, LANES), lambda g, i: (g, q0 + i, 0))]
    out_specs = [pl.BlockSpec((GQ_TQ, rep * LANES), lambda g, i: (q0 + i, g)), whole, whole]
    out_shape = [jax.ShapeDtypeStruct((T, GQA_Q_HEADS * LANES), F32),
                 jax.ShapeDtypeStruct((T, LANES), F32), jax.ShapeDtypeStruct((T, LANES), F32)]
    scratch = [pltpu.VMEM((R, LANES), F32)]
    args = [qp, kp, gqkv, yb, dyb, lse]
    aliases = {}
    name = "gqa_bwd" if latent else "gqa_bwd_ctx"
    if prev_dq is not None:
        in_specs.append(ANY)
        args.append(prev_dq)
        aliases = {6: 0}
    if exchange is not None:
        in_specs.append(ANY)
        out_specs.append(ANY)
        out_shape.append(jax.ShapeDtypeStruct((N_DEV - 1,) + exchange.shape[2:], exchange.dtype))
        scratch += [pltpu.SemaphoreType.DMA((N_DEV - 1,)), pltpu.SemaphoreType.DMA((N_DEV - 1,))]
        args.append(exchange)
        name = "gqa_bwd_exchange"
    return pl.pallas_call(
        body, name=name, grid=(GQA_KV_HEADS, nq),
        in_specs=in_specs, out_specs=out_specs, out_shape=out_shape, scratch_shapes=scratch,
        input_output_aliases=aliases,
        compiler_params=_cp(("arbitrary", "arbitrary"), VMEM_BIG),
    )(*args)


def _na_tables(rows):
    kh = min(NA_WIN_H, rows)
    assert kh == NA_WIN_H and rows >= NA_KROWS and rows % NA_ROWS == 0
    a = np.zeros((3, NA_ROWS, NA_KROWS, 2 * NA_WIN_H - 1), np.float32)
    for v, r0 in enumerate((0, NA_ROWS, rows - NA_ROWS)):
        ks = min(max(r0 - NA_WIN_H // 2, 0), rows - NA_KROWS)
        for rq in range(NA_ROWS):
            r = r0 + rq
            rs = min(max(r - kh // 2, 0), rows - kh)
            for rk in range(NA_KROWS):
                kr = ks + rk
                if rs <= kr < rs + kh:
                    a[v, rq, rk, kr - r + NA_WIN_H - 1] = 1.0
    c = np.zeros((GRID_W, GRID_W, 2 * NA_WIN_W - 1), np.float32)
    for wq in range(GRID_W):
        cs = min(max(wq - NA_WIN_W // 2, 0), GRID_W - NA_WIN_W)
        for wk in range(cs, cs + NA_WIN_W):
            c[wq, wk, min(max(wk - wq, -(NA_WIN_W - 1)), NA_WIN_W - 1) + NA_WIN_W - 1] = 1.0
    return a, c


NA_DR = 2 * NA_WIN_H - 1
NA_TK = 256


def _na_tile_index(rows):
    a, _ = _na_tables(rows)
    idx = np.where(a.sum(-1) > 0, a.argmax(-1), NA_DR).astype(np.int32)
    return jnp.asarray(idx.reshape(-1))


def na_bias_table(rpb, rows):
    _, c = _na_tables(rows)
    H = rpb.shape[0]
    KW = NA_KROWS * GRID_W
    t1 = jnp.einsum("hde,wue->hdwu", rpb, jnp.asarray(c), precision=lax.Precision.HIGHEST)
    t1 = jnp.where(jnp.asarray(c.sum(-1) > 0)[None, None], t1, NEG)
    t1 = jnp.concatenate([t1, jnp.full((H, 1, GRID_W, GRID_W), NEG, F32)], axis=1)
    t1 = jnp.tile(t1, (1, 1, 1, 2))

    def body(idx_ref, t_ref, o_ref):
        v = pl.program_id(1)
        lo = _lane_lt64((GRID_W, LANES))
        for q in range(NA_ROWS):
            for kp in range(NA_KROWS // 2):
                base = (v * NA_ROWS + q) * NA_KROWS + 2 * kp
                tile = jnp.where(lo, t_ref[0, idx_ref[base]], t_ref[0, idx_ref[base + 1]])
                o_ref[0, 0, GRID_W * q:GRID_W * (q + 1), LANES * kp:LANES * (kp + 1)] = tile

    return pl.pallas_call(
        body, name="na_bias_table",
        grid_spec=pltpu.PrefetchScalarGridSpec(
            num_scalar_prefetch=1, grid=(H, 3),
            in_specs=[pl.BlockSpec((1, NA_DR + 1, GRID_W, LANES), lambda h, v, i: (h, 0, 0, 0))],
            out_specs=pl.BlockSpec((1, 1, TM, KW), lambda h, v, i: (h, v, 0, 0))),
        out_shape=jax.ShapeDtypeStruct((H, 3, TM, KW), F32),
        compiler_params=_cp(("parallel", "parallel")),
    )(_na_tile_index(rows), t1)


def na_bias_grad(dbias, rows):
    _, c = _na_tables(rows)
    H = dbias.shape[0]
    KW = NA_KROWS * GRID_W

    def body(idx_ref, d_ref, o_ref):
        v = pl.program_id(1)
        lo = _lane_lt64((GRID_W, LANES))

        @pl.when(v == 0)
        def _():
            o_ref[...] = jnp.zeros((1, NA_DR + 1, GRID_W, LANES), F32)

        for q in range(NA_ROWS):
            for kp in range(NA_KROWS // 2):
                base = (v * NA_ROWS + q) * NA_KROWS + 2 * kp
                tile = d_ref[0, 0, GRID_W * q:GRID_W * (q + 1), LANES * kp:LANES * (kp + 1)]
                o_ref[0, idx_ref[base]] += jnp.where(lo, tile, 0.0)
                o_ref[0, idx_ref[base + 1]] += jnp.where(lo, 0.0, tile)

    dt = pl.pallas_call(
        body, name="na_bias_grad",
        grid_spec=pltpu.PrefetchScalarGridSpec(
            num_scalar_prefetch=1, grid=(H, 3),
            in_specs=[pl.BlockSpec((1, 1, TM, KW), lambda h, v, i: (h, v, 0, 0))],
            out_specs=pl.BlockSpec((1, NA_DR + 1, GRID_W, LANES), lambda h, v, i: (h, 0, 0, 0))),
        out_shape=jax.ShapeDtypeStruct((H, NA_DR + 1, GRID_W, LANES), F32),
        compiler_params=_cp(("parallel", "arbitrary")),
    )(_na_tile_index(rows), dbias)
    dt = dt[:, :NA_DR, :, :GRID_W] + dt[:, :NA_DR, :, GRID_W:]
    return jnp.einsum("hdwu,wue->hde", dt, jnp.asarray(c), precision=lax.Precision.HIGHEST)


def _na_variant(blk, n_lat_blk):
    return jnp.where(blk == 0, 0, jnp.where(blk >= n_lat_blk - 1, 2, 1))


def _na_kstart(blk, rows):
    r0 = blk * NA_ROWS
    ks = jnp.clip(r0 - NA_WIN_H // 2, 0, rows - NA_KROWS)
    return pl.multiple_of(ks * GRID_W, GRID_W)


def na_fwd_call(naqkv, bias, n_lat):
    T = naqkv.shape[0]
    nb = T // TM
    n_lat_blk = n_lat // TM
    rows = n_lat // GRID_W
    n_ctx = T - n_lat
    KW = NA_KROWS * GRID_W
    npair = NA_HEADS // 2

    def body(q_ref, k_ref, v_ref, b_ref, y_ref, lse_ref):
        blk = pl.program_id(1)
        half = _lane_half((TM, LANES))
        lo = half == 0
        q = q_ref[...].astype(F32) * SCALE
        kctx = k_ref[pl.ds(n_lat, n_ctx), :]
        vctx = v_ref[pl.ds(n_lat, n_ctx), :]

        @pl.when(blk < n_lat_blk)
        def _():
            off = _na_kstart(blk, rows)
            kwin = k_ref[pl.ds(off, KW), :]
            vwin = v_ref[pl.ds(off, KW), :]
            outs = []
            for hh in range(2):
                qm = jnp.where(half == hh, q, 0.0).astype(BF16)
                sw = _dot_nt(qm, kwin) + b_ref[hh, 0]
                sc = _dot_nt(qm, kctx)
                m = jnp.maximum(jnp.max(sw, axis=-1, keepdims=True), jnp.max(sc, axis=-1, keepdims=True))
                pw = jnp.exp(sw - m)
                pc = jnp.exp(sc - m)
                l = jnp.sum(pw, axis=-1, keepdims=True) + jnp.sum(pc, axis=-1, keepdims=True)
                outs.append((_dot(pw.astype(BF16), vwin) + _dot(pc.astype(BF16), vctx)) / l)
                lse_ref[hh] = jnp.broadcast_to(m + jnp.log(l), (TM, LANES))
            y_ref[...] = jnp.where(lo, outs[0], outs[1]).astype(BF16)

        @pl.when(blk >= n_lat_blk)
        def _():
            outs = []
            for hh in range(2):
                qm = jnp.where(half == hh, q, 0.0).astype(BF16)
                sc = _dot_nt(qm, kctx)
                m = jnp.max(sc, axis=-1, keepdims=True)
                pc = jnp.exp(sc - m)
                l = jnp.sum(pc, axis=-1, keepdims=True)
                outs.append(_dot(pc.astype(BF16), vctx) / l)
                lse_ref[hh] = jnp.broadcast_to(m + jnp.log(l), (TM, LANES))
            y_ref[...] = jnp.where(lo, outs[0], outs[1]).astype(BF16)

    return pl.pallas_call(
        body, name="na_fwd", grid=(npair, nb),
        in_specs=[pl.BlockSpec((TM, LANES), lambda p, b: (b, p)),
                  pl.BlockSpec((T, LANES), lambda p, b: (0, npair + p)),
                  pl.BlockSpec((T, LANES), lambda p, b: (0, 2 * npair + p)),
                  pl.BlockSpec((2, 1, TM, KW), lambda p, b: (p, _na_variant(b, n_lat_blk), 0, 0))],
        out_specs=[pl.BlockSpec((TM, LANES), lambda p, b: (b, p)),
                   pl.BlockSpec((2, TM, LANES), lambda p, b: (p, b, 0))],
        out_shape=[jax.ShapeDtypeStruct((T, NA_HEADS * HEAD_DIM), BF16),
                   jax.ShapeDtypeStruct((NA_HEADS, T, LANES), F32)],
        compiler_params=_cp(("parallel", "parallel"), VMEM_BIG),
    )(naqkv, naqkv, naqkv, bias)


def na_bwd_call(naqkv, bias, ya, dya, lse, n_lat):
    T = naqkv.shape[0]
    nb = T // TM
    n_lat_blk = n_lat // TM
    rows = n_lat // GRID_W
    n_ctx = T - n_lat
    KW = NA_KROWS * GRID_W
    npair = NA_HEADS // 2

    def body(q_ref, k_ref, v_ref, b_ref, y_ref, dy_ref, lse_ref, dq_ref, dk_ref, dv_ref, db_ref, dk_sc, dv_sc):
        blk = pl.program_id(1)
        half = _lane_half((TM, LANES))
        lo = half == 0
        q = q_ref[...].astype(F32) * SCALE
        kctx = k_ref[pl.ds(n_lat, n_ctx), :]
        vctx = v_ref[pl.ds(n_lat, n_ctx), :]
        dyv = dy_ref[...].astype(F32)
        yv = y_ref[...].astype(F32)

        @pl.when(blk == 0)
        def _():
            dk_sc[...] = jnp.zeros((T, LANES), F32)
            dv_sc[...] = jnp.zeros((T, LANES), F32)

        @pl.when(blk < n_lat_blk)
        def _():
            off = _na_kstart(blk, rows)
            kwin = k_ref[pl.ds(off, KW), :]
            vwin = v_ref[pl.ds(off, KW), :]
            first = (blk == 0) | (blk == 1) | (blk == n_lat_blk - 1)
            dqs = []
            for hh in range(2):
                sel = half == hh
                qm = jnp.where(sel, q, 0.0).astype(BF16)
                dom = jnp.where(sel, dyv, 0.0)
                delta = jnp.sum(dom * yv, axis=-1, keepdims=True)
                dom = dom.astype(BF16)
                lse_v = lse_ref[hh][:, 0:1]
                pw = jnp.exp(_dot_nt(qm, kwin) + b_ref[hh, 0] - lse_v)
                pc = jnp.exp(_dot_nt(qm, kctx) - lse_v)
                dsw = pw * (_dot_nt(dom, vwin) - delta)
                dsc = pc * (_dot_nt(dom, vctx) - delta)

                @pl.when(first)
                def _():
                    db_ref[hh, 0] = dsw

                @pl.when(jnp.logical_not(first))
                def _():
                    db_ref[hh, 0] += dsw

                dsw = dsw.astype(BF16)
                dsc = dsc.astype(BF16)
                dv_sc[pl.ds(off, KW), :] += _dot_tn(pw.astype(BF16), dom)
                dv_sc[pl.ds(n_lat, n_ctx), :] += _dot_tn(pc.astype(BF16), dom)
                dk_sc[pl.ds(off, KW), :] += _dot_tn(dsw, qm)
                dk_sc[pl.ds(n_lat, n_ctx), :] += _dot_tn(dsc, qm)
                dqs.append((_dot(dsw, kwin) + _dot(dsc, kctx)) * SCALE)
            dq_ref[...] = jnp.where(lo, dqs[0], dqs[1]).astype(BF16)

        @pl.when(blk >= n_lat_blk)
        def _():
            dqs = []
            for hh in range(2):
                sel = half == hh
                qm = jnp.where(sel, q, 0.0).astype(BF16)
                dom = jnp.where(sel, dyv, 0.0)
                delta = jnp.sum(dom * yv, axis=-1, keepdims=True)
                dom = dom.astype(BF16)
                pc = jnp.exp(_dot_nt(qm, kctx) - lse_ref[hh][:, 0:1])
                dsc = (pc * (_dot_nt(dom, vctx) - delta)).astype(BF16)
                dv_sc[pl.ds(n_lat, n_ctx), :] += _dot_tn(pc.astype(BF16), dom)
                dk_sc[pl.ds(n_lat, n_ctx), :] += _dot_tn(dsc, qm)
                dqs.append(_dot(dsc, kctx) * SCALE)
            dq_ref[...] = jnp.where(lo, dqs[0], dqs[1]).astype(BF16)

        @pl.when(blk == nb - 1)
        def _():
            dk_ref[...] = dk_sc[...].astype(BF16)
            dv_ref[...] = dv_sc[...].astype(BF16)

    blkspec = pl.BlockSpec((TM, LANES), lambda p, b: (b, p))
    col = pl.BlockSpec((T, LANES), lambda p, b: (0, p))
    bspec = pl.BlockSpec((2, 1, TM, KW), lambda p, b: (p, _na_variant(b, n_lat_blk), 0, 0))
    dq, dk, dv, db = pl.pallas_call(
        body, name="na_bwd", grid=(npair, nb),
        in_specs=[blkspec,
                  pl.BlockSpec((T, LANES), lambda p, b: (0, npair + p)),
                  pl.BlockSpec((T, LANES), lambda p, b: (0, 2 * npair + p)),
                  bspec, blkspec, blkspec,
                  pl.BlockSpec((2, TM, LANES), lambda p, b: (p, b, 0))],
        out_specs=[blkspec, col, col, bspec],
        out_shape=[jax.ShapeDtypeStruct((T, NA_HEADS * HEAD_DIM), BF16)] * 3
        + [jax.ShapeDtypeStruct((NA_HEADS, 3, TM, KW), F32)],
        scratch_shapes=[pltpu.VMEM((T, LANES), F32), pltpu.VMEM((T, LANES), F32)],
        compiler_params=_cp(("arbitrary", "arbitrary"), VMEM_BIG),
    )(naqkv, naqkv, naqkv, bias, ya, dya, lse)
    return jnp.concatenate([dq, dk, dv], axis=1), db


def rope_tables(n_lat, n_ctx):
    t = jnp.arange(n_lat)
    row = (t // GRID_W).astype(F32)
    col = (t % GRID_W).astype(F32)
    half = HEAD_DIM // 2
    inv = ROPE_THETA ** (-jnp.arange(0, half, 2, dtype=F32) / half)
    ang = jnp.concatenate([row[:, None] * inv, col[:, None] * inv], axis=-1)
    cos = jnp.repeat(jnp.cos(ang), 2, axis=-1)
    sin = jnp.repeat(jnp.sin(ang), 2, axis=-1) * jnp.tile(jnp.array([-1.0, 1.0], F32), half)
    cos = jnp.concatenate([cos, jnp.ones((n_ctx, HEAD_DIM), F32)], axis=0)
    sin = jnp.concatenate([sin, jnp.zeros((n_ctx, HEAD_DIM), F32)], axis=0)
    return jnp.tile(cos, (1, 2)), jnp.tile(sin, (1, 2))


def _row(v):
    return v.reshape(1, -1)


def device_fwd_bwd(xs, target, modv, w, small, n_lat, ex=None):
    T, D = xs.shape
    n_ctx = T - n_lat
    L = modv.shape[0]
    nlt = n_lat // TM
    rows = n_lat // GRID_W
    cos_t, sin_t = rope_tables(n_lat, n_ctx)
    qg = [jnp.tile(_row(small["q_gain"][l]), (1, GQA_Q_HEADS)) for l in range(L)]
    kg = [jnp.tile(_row(small["k_gain"][l]), (1, GQA_KV_HEADS)) for l in range(L)]
    bias = [na_bias_table(small["na_rpb"][l], rows) for l in range(L)]

    later = {}

    def wt(name, l):
        return w[name][l] if (ex is None or l == 0) else later[name][l - 1]

    saved = []
    x = xs
    h = norm_mod_call(x, _row(small["norm1"][0]), modv[0], nlt, 0)
    for l in range(L):
        naqkv, gqkv, gates = qkv_call(h, wt("w_in", l))
        ya, lse_a = na_fwd_call(naqkv, bias[l], n_lat)
        qp, kp = gqa_prep_call(gqkv, qg[l], kg[l], cos_t, sin_t)
        small_scores = gqa_scores_small(small["q_gain"][l], small["k_gain"][l])
        if ex is not None and l == 0 and L > 1:
            yb, lse_b, gathered = gqa_fwd_call(qp, kp, gqkv, n_lat, small_scores, gather=ex.rest())
            later.update(ex.rest_layers(gathered))
        else:
            yb, lse_b = gqa_fwd_call(qp, kp, gqkv, n_lat, small_scores)
        merged, pa, pb = merge_call(ya, yb, gates, wt("w_pa", l), wt("w_pb", l))
        x1, osave, h2 = wo_call(merged, wt("w_o", l), x, modv[l], _row(small["norm2"][l]), nlt)
        act, a, u = ffn_in_call(h2, wt("w_ffn_in", l))
        if l + 1 < L:
            x2, fsave, hn = ffn_out_call(act, wt("w_ffn_out", l), x1, modv[l], nlt,
                                         _row(small["norm1"][l + 1]), modv[l + 1])
        else:
            x2, fsave = ffn_out_call(act, wt("w_ffn_out", l), x1, modv[l], nlt)
            hn = None
        saved.append(dict(x=x, h=h, naqkv=naqkv, gqkv=gqkv, gates=gates, ya=ya, lse_a=lse_a, qp=qp, kp=kp,
                          yb=yb, lse_b=lse_b, merged=merged, pa=pa, pb=pb, x1=x1, osave=osave, h2=h2,
                          act=act, a=a, u=u, fsave=fsave))
        x, h = x2, hn

    dx, d_final, loss_part = loss_call(x, _row(small["final_norm"]), target, nlt)

    gw = {k: [None] * L for k in ("w_in", "w_pa", "w_pb", "w_o", "w_ffn_in", "w_ffn_out")}
    gs = {k: [None] * L for k in ("norm1", "norm2", "na_rpb", "q_gain", "k_gain")}
    dmod = [None] * L
    fold = lambda s: s.sum(axis=1)
    for l in reversed(range(L)):
        s = saved[l]
        da, du, df, s_g2 = dact_call(dx, modv[l], s["fsave"], wt("w_ffn_out", l), s["a"], s["u"], nlt)
        gw["w_ffn_out"][l] = wgrad_call(s["act"], df, "wgrad_ffn_out")
        dx1, s_sh2, s_sc2, s_n2 = dh2_call(da, du, wt("w_ffn_in", l), s["x1"], modv[l], _row(small["norm2"][l]), dx, nlt)
        gw["w_ffn_in"][l] = jnp.concatenate(
            [wgrad_call(s["h2"], da, "wgrad_ffn_in_a"), wgrad_call(s["h2"], du, "wgrad_ffn_in_u")], axis=1)
        dpa, dpb, dgates, do, s_g1 = dmerged_call(dx1, modv[l], s["osave"], wt("w_o", l), s["pa"], s["pb"], s["gates"], nlt)
        gw["w_o"][l] = wgrad_call(s["merged"], do, "wgrad_o")
        dya, dyb = dy_call(dpa, dpb, wt("w_pa", l), wt("w_pb", l))
        gw["w_pa"][l] = wgrad_call(s["ya"], dpa, "wgrad_pa")
        gw["w_pb"][l] = wgrad_call(s["yb"], dpb, "wgrad_pb")
        dna, dbias = na_bwd_call(s["naqkv"], bias[l], s["ya"], dya, s["lse_a"], n_lat)
        gs["na_rpb"][l] = na_bias_grad(dbias, rows)
        if ex is not None and l + 1 < L:
            dqp, dkp, dv, ex.received[l + 1] = gqa_bwd_call(s["qp"], s["kp"], s["gqkv"], s["yb"], dyb, s["lse_b"], n_lat,
                                                           exchange=ex.partials[l + 1])
        else:
            dqp, dkp, dv = gqa_bwd_call(s["qp"], s["kp"], s["gqkv"], s["yb"], dyb, s["lse_b"], n_lat)
        dg, s_qg, s_kg = gqa_prep_bwd_call(s["gqkv"], qg[l], kg[l], cos_t, sin_t, dqp, dkp, dv)
        gs["q_gain"][l] = s_qg.reshape(8 * GQA_Q_HEADS, HEAD_DIM).sum(axis=0)
        gs["k_gain"][l] = s_kg.reshape(8 * GQA_KV_HEADS, HEAD_DIM).sum(axis=0)
        dx, s_sh1, s_sc1, s_n1 = dh_call(dna, dg, dgates, wt("w_in", l), s["x"], modv[l], _row(small["norm1"][l]), dx1, nlt)
        gw["w_in"][l] = jnp.concatenate(
            [wgrad_call(s["h"], dna, "wgrad_in_na"), wgrad_call(s["h"], dg, "wgrad_in_g"),
             wgrad_call(s["h"], dgates, "wgrad_in_gate")], axis=1)
        gs["norm1"][l] = fold(s_n1).sum(axis=0)
        gs["norm2"][l] = fold(s_n2).sum(axis=0)
        dmod[l] = jnp.stack([fold(s_sh1), fold(s_sc1), fold(s_g1), fold(s_sh2), fold(s_sc2), fold(s_g2)], axis=1)
        if ex is not None:
            ex.pack_grads(l, {k: v[l] for k, v in gw.items()})

    if ex is not None:
        ex.received[0] = grads_exchange(ex.partials[0])
        gw = None
    else:
        gw = {k: jnp.stack(v) for k, v in gw.items()}
    gs = {k: jnp.stack(v) for k, v in gs.items()}
    gs["final_norm"] = d_final.sum(axis=(0, 1))
    return loss_part, dx, gw, jnp.stack(dmod), gs


N_DEV = 8
N_CHIP = 4
ANY = pl.BlockSpec(memory_space=pl.ANY)


def _place():
    x, y, c = lax.axis_index("x"), lax.axis_index("y"), lax.axis_index("c")
    chips = [(1 - x, y), (x, 1 - y), (1 - x, 1 - y)]
    return x, y, c, chips


def small_all_gather(v, name):
    m_per, n = v.shape
    assert m_per % 8 == 0

    def body(x_ref, out_ref, send_sems, recv_sems, local_sem):
        x, y, c, chips = _place()
        me, sibling = (x, y, c), (x, y, 1 - c)

        def rows(px, py, pc):
            return out_ref.at[pl.ds(pl.multiple_of((4 * px + 2 * py + pc) * m_per, 8), m_per), :]

        def copy(k, block, to, src=None):
            return pltpu.make_async_remote_copy(
                src_ref=rows(*block) if src is None else src, dst_ref=rows(*block),
                send_sem=send_sems.at[k], recv_sem=recv_sems.at[k], device_id=to, device_id_type=MESH)

        mine = pltpu.make_async_copy(x_ref, rows(*me), local_sem)
        mine.start()
        first = [copy(0, me, sibling, src=x_ref)]
        first += [copy(1 + j, me, (*chip, c), src=x_ref) for j, chip in enumerate(chips)]
        for cp in first:
            cp.start()
        passed = [copy(4 + j, (*chip, c), sibling) for j, chip in enumerate(chips)]
        for j, chip in enumerate(chips):
            copy(1 + j, (*chip, c), me).wait_recv()
            passed[j].start()
        copy(0, sibling, me).wait_recv()
        for j, chip in enumerate(chips):
            copy(4 + j, (*chip, 1 - c), me).wait_recv()
        for cp in first + passed:
            cp.wait_send()
        mine.wait()

    out = pl.pallas_call(
        body, name=name,
        out_shape=jax.ShapeDtypeStruct((N_DEV * m_per, n), v.dtype),
        in_specs=[pl.BlockSpec(memory_space=pltpu.VMEM)],
        out_specs=pl.BlockSpec(memory_space=pltpu.VMEM),
        scratch_shapes=[pltpu.SemaphoreType.DMA((7,)), pltpu.SemaphoreType.DMA((7,)), pltpu.SemaphoreType.DMA],
    )(v)
    return out.reshape(N_DEV, m_per, n)


def _ag_copies(in_ref, out_ref, send_sems, recv_sems):
    x, y, c, chips = _place()
    me, sibling = (x, y, c), (x, y, 1 - c)

    def blk(px, py, half):
        return out_ref.at[2 * px + py, half]

    def copy(k, block, to, src=None):
        return pltpu.make_async_remote_copy(
            src_ref=blk(*block) if src is None else src, dst_ref=blk(*block),
            send_sem=send_sems.at[k], recv_sem=recv_sems.at[k], device_id=to, device_id_type=MESH)

    first = [copy(j, (x, y, c), (*chip, c), src=in_ref.at[c]) for j, chip in enumerate(chips)]
    passed = [copy(3 + j, (*chip, c), sibling) for j, chip in enumerate(chips)]
    arrivals = [copy(j, (*chip, c), me) for j, chip in enumerate(chips)]
    forwarded = [copy(3 + j, (*chip, 1 - c), me) for j, chip in enumerate(chips)]
    return first, passed, arrivals, forwarded


def _ag_start(in_ref, out_ref, send_sems, recv_sems):
    for cp in _ag_copies(in_ref, out_ref, send_sems, recv_sems)[0]:
        cp.start()


def _ag_finish(in_ref, out_ref, send_sems, recv_sems):
    first, passed, arrivals, forwarded = _ag_copies(in_ref, out_ref, send_sems, recv_sems)
    for arrived, onward in zip(arrivals, passed):
        arrived.wait_recv()
        onward.start()
    for cp in forwarded:
        cp.wait_recv()
    for cp in first + passed:
        cp.wait_send()


def place_own_block(gathered, packed):
    chip = 2 * lax.axis_index("x") + lax.axis_index("y")
    return lax.dynamic_update_slice(gathered, packed[None], (chip,) + (0,) * packed.ndim)


def weights_all_gather(packed):
    def body(in_ref, out_ref, send_sems, recv_sems):
        _ag_start(in_ref, out_ref, send_sems, recv_sems)
        _ag_finish(in_ref, out_ref, send_sems, recv_sems)

    out = pl.pallas_call(
        body, name="weights_all_gather",
        out_shape=jax.ShapeDtypeStruct((N_CHIP,) + packed.shape, packed.dtype),
        in_specs=[ANY], out_specs=ANY,
        scratch_shapes=[pltpu.SemaphoreType.DMA((6,)), pltpu.SemaphoreType.DMA((6,))],
    )(packed)
    return place_own_block(out, packed)


def _rs_copies(gp_ref, rx_ref, send_sems, recv_sems):
    x, y, c, chips = _place()
    sends = [pltpu.make_async_remote_copy(src_ref=gp_ref.at[2 * x + y, 1 - c], dst_ref=rx_ref.at[0],
                                          send_sem=send_sems.at[0], recv_sem=recv_sems.at[0],
                                          device_id=(x, y, 1 - c), device_id_type=MESH)]
    for r, (px, py) in enumerate(chips):
        for h in range(2):
            sends.append(pltpu.make_async_remote_copy(
                src_ref=gp_ref.at[2 * px + py, h], dst_ref=rx_ref.at[1 + 2 * r + c],
                send_sem=send_sems.at[1 + 2 * r + h], recv_sem=recv_sems.at[1 + 2 * r + c],
                device_id=(px, py, h), device_id_type=MESH))
    arrivals = [pltpu.make_async_remote_copy(src_ref=gp_ref.at[0, 0], dst_ref=rx_ref.at[s],
                                             send_sem=send_sems.at[s], recv_sem=recv_sems.at[s],
                                             device_id=(x, y, c), device_id_type=MESH) for s in range(N_DEV - 1)]
    return sends, arrivals


def _rs_start(gp_ref, rx_ref, send_sems, recv_sems):
    for cp in _rs_copies(gp_ref, rx_ref, send_sems, recv_sems)[0]:
        cp.start()


def _rs_finish(gp_ref, rx_ref, send_sems, recv_sems):
    sends, arrivals = _rs_copies(gp_ref, rx_ref, send_sems, recv_sems)
    for cp in arrivals:
        cp.wait_recv()
    for cp in sends:
        cp.wait_send()


def grads_exchange(gp):
    def body(gp_ref, rx_ref, send_sems, recv_sems):
        _rs_start(gp_ref, rx_ref, send_sems, recv_sems)
        _rs_finish(gp_ref, rx_ref, send_sems, recv_sems)

    return pl.pallas_call(
        body, name="grads_exchange", out_shape=jax.ShapeDtypeStruct((N_DEV - 1,) + gp.shape[2:], gp.dtype),
        in_specs=[ANY], out_specs=ANY,
        scratch_shapes=[pltpu.SemaphoreType.DMA((N_DEV - 1,)), pltpu.SemaphoreType.DMA((N_DEV - 1,))],
    )(gp)


def grads_join_halves(gh):
    def body(gh_ref, out_ref, send_sem, recv_sem):
        x, y, c, _ = _place()
        cp = pltpu.make_async_remote_copy(src_ref=gh_ref, dst_ref=out_ref, send_sem=send_sem,
                                          recv_sem=recv_sem, device_id=(x, y, 1 - c), device_id_type=MESH)
        cp.start()
        cp.wait()

    other = pl.pallas_call(
        body, name="grads_join_halves", out_shape=jax.ShapeDtypeStruct(gh.shape, gh.dtype),
        in_specs=[ANY], out_specs=ANY,
        scratch_shapes=[pltpu.SemaphoreType.DMA, pltpu.SemaphoreType.DMA],
    )(gh)
    c0 = lax.axis_index("c") == 0
    return jnp.stack([jnp.where(c0, gh, other), jnp.where(c0, other, gh)], axis=1)


PACK_C = 1024


def sum_partials_call(gp, rx, chip, core):
    _, _, rh, ncol = gp.shape
    n = rx.shape[0]
    tr = _pick_tile(rh, 512, 16)

    def body(where_ref, a_ref, *rest):
        o_ref = rest[-1]
        g = a_ref[0, 0].astype(F32)
        for b_ref in rest[:-1]:
            g = g + b_ref[0].astype(F32)
        o_ref[...] = g

    other = lambda j: pl.BlockSpec((1, tr, ncol), lambda i, w: (j, i, 0))
    return pl.pallas_call(
        body, name="sum_partials",
        grid_spec=pltpu.PrefetchScalarGridSpec(
            num_scalar_prefetch=1, grid=(rh // tr,),
            in_specs=[pl.BlockSpec((1, 1, tr, ncol), lambda i, w: (w[0], w[1], i, 0))] + [other(j) for j in range(n)],
            out_specs=pl.BlockSpec((tr, ncol), lambda i, w: (i, 0))),
        out_shape=jax.ShapeDtypeStruct((rh, ncol), F32),
        compiler_params=_cp(("parallel",)),
    )(jnp.stack([chip, core]).astype(jnp.int32), gp, *([rx] * n))


def _silu(x):
    return x * _sigmoid(x)


def mod_matmul_call(craw, w_mod):
    L, D, N = w_mod.shape

    def body(c_ref, w_ref, o_ref):
        o_ref[0] = _dot(_silu(c_ref[...]).astype(BF16), w_ref[0].astype(BF16))

    return pl.pallas_call(
        body, name="mod_matmul", grid=(L,),
        in_specs=[pl.BlockSpec((16, D), lambda l: (0, 0)), pl.BlockSpec((1, D, N), lambda l: (l, 0, 0))],
        out_specs=pl.BlockSpec((1, 16, N), lambda l: (l, 0, 0)),
        out_shape=jax.ShapeDtypeStruct((L, 16, N), F32),
        compiler_params=_cp(("parallel",), VMEM_BIG),
    )(craw, w_mod)


def mod_bwd_call(craw, dmod16, w_mod):
    L, D, N = w_mod.shape

    def body(c_ref, d_ref, w_ref, gw_ref, ds_ref):
        l = pl.program_id(0)
        d = d_ref[0].astype(BF16)
        gw_ref[0] = _dot_tn(_silu(c_ref[...]).astype(BF16), d)
        part = _dot_nt(d, w_ref[0].astype(BF16))

        @pl.when(l == 0)
        def _():
            ds_ref[...] = part

        @pl.when(l > 0)
        def _():
            ds_ref[...] += part

    return pl.pallas_call(
        body, name="mod_bwd", grid=(L,),
        in_specs=[pl.BlockSpec((16, D), lambda l: (0, 0)), pl.BlockSpec((1, 16, N), lambda l: (l, 0, 0)),
                  pl.BlockSpec((1, D, N), lambda l: (l, 0, 0))],
        out_specs=[pl.BlockSpec((1, D, N), lambda l: (l, 0, 0)), pl.BlockSpec((16, D), lambda l: (0, 0))],
        out_shape=[jax.ShapeDtypeStruct((L, D, N), F32), jax.ShapeDtypeStruct((16, D), F32)],
        compiler_params=_cp(("arbitrary",), VMEM_BIG),
    )(craw, dmod16, w_mod)


def _adamw(w, g, m, v):
    m2 = ADAM_B1 * m + (1.0 - ADAM_B1) * g
    v2 = ADAM_B2 * v + (1.0 - ADAM_B2) * (g * g)
    m_hat = m2 / (1.0 - ADAM_B1 ** ADAM_STEP)
    v_hat = v2 / (1.0 - ADAM_B2 ** ADAM_STEP)
    delta = -ADAM_LR * (m_hat / (jnp.sqrt(v_hat) + ADAM_EPS) + ADAM_WD * w)
    return delta, m2, v2


def adamw_call(w, g, m, v):
    shape = w.shape
    ncol = shape[-1]
    r = math.prod(shape[:-1])
    tr = _pick_tile(r, 512, 8)
    as2d = lambda t: t.reshape(r, ncol)

    def body(w_ref, g_ref, m_ref, v_ref, d_ref, m2_ref, v2_ref):
        d_ref[...], m2_ref[...], v2_ref[...] = _adamw(w_ref[...], g_ref[...], m_ref[...], v_ref[...])

    spec = pl.BlockSpec((tr, ncol), lambda i: (i, 0))
    outs = pl.pallas_call(
        body, name="adamw", grid=(r // tr,), in_specs=[spec] * 4, out_specs=[spec] * 3,
        out_shape=[jax.ShapeDtypeStruct((r, ncol), F32)] * 3,
        compiler_params=_cp(("parallel",)),
    )(as2d(w), as2d(g), as2d(m), as2d(v))
    return [o.reshape(shape) for o in outs]


def adamw_small_call(parts, w, m, v, silu_bwd=False):
    P, R, C = parts.shape

    def body(p_ref, w_ref, m_ref, v_ref, g_ref, d_ref, m2_ref, v2_ref):
        g = p_ref[0]
        for i in range(1, P):
            g = g + p_ref[i]
        wv = w_ref[...]
        if silu_bwd:
            sg = _sigmoid(wv)
            g = g * (sg * (1.0 + wv * (1.0 - sg)))
        g_ref[...] = g
        d_ref[...], m2_ref[...], v2_ref[...] = _adamw(wv, g, m_ref[...], v_ref[...])

    return pl.pallas_call(
        body, name="adamw_small", out_shape=[jax.ShapeDtypeStruct((R, C), F32)] * 4,
    )(parts, w, m, v)


def sum_parts_call(parts):
    P, R, C = parts.shape

    def body(p_ref, o_ref):
        g = p_ref[0]
        for i in range(1, P):
            g = g + p_ref[i]
        o_ref[...] = g

    return pl.pallas_call(body, name="sum_parts", out_shape=jax.ShapeDtypeStruct((R, C), F32))(parts)


BIG = (("w_in", "col"), ("w_pa", "col"), ("w_pb", "col"), ("w_o", "row"), ("w_ffn_in", "col"), ("w_ffn_out", "row"))
SMALL = ("norm1", "norm2", "final_norm", "na_rpb", "q_gain", "k_gain")


def _pack_shards(shards):
    lead = shards[BIG[0][0]].shape[:-2]
    p = jnp.concatenate([shards[n].reshape(lead + (-1, PACK_C)) for n, _ in BIG], axis=-2)
    return p.reshape(lead + (2, p.shape[-2] // 2, PACK_C))


def _unpack_shards(packed, shapes):
    lead = packed.shape[:-3]
    flat = packed.reshape(lead + (-1, PACK_C))
    out, r0 = {}, 0
    for n, _ in BIG:
        K, N = shapes[n]
        rw = K * N // PACK_C
        out[n] = flat[..., r0:r0 + rw, :].reshape(lead + (K, N))
        r0 += rw
    return out


def _whole_from_chips(g, kind):
    _, L, K, N = g.shape
    if kind == "col":
        return g.transpose(1, 2, 0, 3).reshape(L, K, N_CHIP * N)
    return g.transpose(1, 0, 2, 3).reshape(L, N_CHIP * K, N)


def _chips_from_whole(g, kind):
    K, N = g.shape
    if kind == "col":
        return g.reshape(K, N_CHIP, N // N_CHIP).transpose(1, 0, 2)
    return g.reshape(N_CHIP, K // N_CHIP, N)


class Exchange:
    def __init__(self, packed, shard_shapes):
        self.shapes = shard_shapes
        self.packed = packed
        self.received = {}
        self.partials = {}

    def whole(self, gathered):
        g = _unpack_shards(gathered.transpose(0, 2, 1, 3, 4), self.shapes)
        return {n: _whole_from_chips(g[n], kind) for n, kind in BIG}

    def first_layer(self):
        return self.whole(weights_all_gather(self.packed[:, :1]))

    def rest(self):
        return self.packed[:, 1:]

    def rest_layers(self, gathered):
        return self.whole(place_own_block(gathered, self.rest()))

    def pack_grads(self, layer, gw):
        self.partials[layer] = _pack_shards({n: _chips_from_whole(gw[n], kind) for n, kind in BIG})
        return self.partials[layer]


def _pad_rows(v, rows):
    return jnp.pad(v.reshape(-1), (0, rows * PACK_C - v.size)).reshape(rows, PACK_C)


def kernel(x, c, ctx, c_ctx, w_mod, b_mod, norm1, w_in, na_rpb, q_gain, k_gain, w_pa, w_pb, w_o, norm2, w_ffn_in, w_ffn_out, final_norm, loss_target, m_c_ctx, m_w_mod, m_b_mod, m_norm1, m_w_in, m_na_rpb, m_q_gain, m_k_gain, m_w_pa, m_w_pb, m_w_o, m_norm2, m_w_ffn_in, m_w_ffn_out, m_final_norm, v_c_ctx, v_w_mod, v_b_mod, v_norm1, v_w_in, v_na_rpb, v_q_gain, v_k_gain, v_w_pa, v_w_pb, v_w_o, v_norm2, v_w_ffn_in, v_w_ffn_out, v_final_norm):
    W = dict(c_ctx=c_ctx, w_mod=w_mod, b_mod=b_mod, norm1=norm1, w_in=w_in, na_rpb=na_rpb, q_gain=q_gain, k_gain=k_gain,
             w_pa=w_pa, w_pb=w_pb, w_o=w_o, norm2=norm2, w_ffn_in=w_ffn_in, w_ffn_out=w_ffn_out, final_norm=final_norm)
    M = dict(c_ctx=m_c_ctx, w_mod=m_w_mod, b_mod=m_b_mod, norm1=m_norm1, w_in=m_w_in, na_rpb=m_na_rpb, q_gain=m_q_gain,
             k_gain=m_k_gain, w_pa=m_w_pa, w_pb=m_w_pb, w_o=m_w_o, norm2=m_norm2, w_ffn_in=m_w_ffn_in,
             w_ffn_out=m_w_ffn_out, final_norm=m_final_norm)
    V = dict(c_ctx=v_c_ctx, w_mod=v_w_mod, b_mod=v_b_mod, norm1=v_norm1, w_in=v_w_in, na_rpb=v_na_rpb, q_gain=v_q_gain,
             k_gain=v_k_gain, w_pa=v_w_pa, w_pb=v_w_pb, w_o=v_w_o, norm2=v_norm2, w_ffn_in=v_w_ffn_in,
             w_ffn_out=v_w_ffn_out, final_norm=v_final_norm)
    order = ["c_ctx", "w_mod", "b_mod", "norm1", "w_in", "na_rpb", "q_gain", "k_gain", "w_pa", "w_pb", "w_o", "norm2",
             "w_ffn_in", "w_ffn_out", "final_norm"]
    L, D = norm1.shape
    n_lat = x.shape[1]
    ax, ay, ac = lax.axis_index("x"), lax.axis_index("y"), lax.axis_index("c")
    chip = 2 * ax + ay
    dev = 2 * chip + ac

    shard_shapes = {n: W[n].shape[1:] for n, _ in BIG}
    ex = Exchange(_pack_shards({n: W[n].astype(BF16) for n, _ in BIG}).transpose(1, 0, 2, 3), shard_shapes)
    w_first = ex.first_layer()

    c_all = small_all_gather(jnp.pad(c, ((0, 7), (0, 0))), "gather_c")[:, 0, :]
    craw = jnp.pad(c_all, ((0, 8), (0, 0))) + jnp.pad(c_ctx[None, :], ((8, 7), (0, 0)))
    ncol_mod = w_mod.shape[2]
    mod_loc = mod_matmul_call(craw, w_mod)
    mod_all = small_all_gather(mod_loc.reshape(L * 16, ncol_mod), "gather_mod")
    mod_all = mod_all[0::2].reshape(N_CHIP, L, 16, ncol_mod).transpose(1, 2, 0, 3).reshape(L, 16, 6 * D)
    mod_all = mod_all + b_mod[:, None, :]
    mod_mine = lax.dynamic_index_in_dim(mod_all, dev, axis=1, keepdims=False)
    modv = jnp.stack([mod_mine, mod_all[:, 8]], axis=1).reshape(L, 2, 6, D)

    xs = jnp.concatenate([x[0], ctx[0]], axis=0)
    small = dict(norm1=norm1, norm2=norm2, na_rpb=na_rpb, q_gain=q_gain, k_gain=k_gain, final_norm=final_norm)
    loss_part, dx, _, dmod, gs = device_fwd_bwd(xs, loss_target[0], modv, w_first, small, n_lat, ex)
    loss = lax.psum(jnp.sum(loss_part), ("x", "y", "c"))
    grad_x = dx[:n_lat][None]

    gh = jnp.stack([sum_partials_call(ex.partials[l], ex.received[l], chip, ac) for l in range(L)])
    gbig = _unpack_shards(grads_join_halves(gh), shard_shapes)

    n_mod_rows = L * 6 * D // PACK_C
    small_flat = jnp.concatenate([gs[n].reshape(-1) for n in SMALL])
    n_small_rows = -(-small_flat.size // (8 * PACK_C)) * 8
    g1 = jnp.concatenate([dmod[:, 0].reshape(n_mod_rows, PACK_C), dmod[:, 1].reshape(n_mod_rows, PACK_C),
                          _pad_rows(small_flat, n_small_rows)], axis=0)
    g1 = small_all_gather(g1, "gather_small_grads")
    dm_lat = g1[:, :n_mod_rows].reshape(N_DEV, L, 6 * D)
    dm_ctx_parts = g1[:, n_mod_rows:2 * n_mod_rows]
    dm_ctx = sum_parts_call(dm_ctx_parts).reshape(L, 1, 6 * D)
    dmod16 = (jnp.pad(dm_lat.transpose(1, 0, 2), ((0, 0), (0, 8), (0, 0)))
              + jnp.pad(dm_ctx, ((0, 0), (8, 7), (0, 0))))
    dmod16_loc = lax.dynamic_slice_in_dim(dmod16, chip * ncol_mod, ncol_mod, axis=2)
    g_wmod, dsilu_part = mod_bwd_call(craw, dmod16_loc, w_mod)
    ds_all = small_all_gather(dsilu_part[8:16], "gather_dsilu")
    ds_parts = ds_all[0::2, 0:1, :]
    ds_parts = jnp.pad(ds_parts, ((0, 0), (0, 7), (0, 0)))

    out_g, out_d, out_m, out_v = {}, {}, {}, {}
    for n, _ in BIG:
        out_g[n] = gbig[n]
        out_d[n], out_m[n], out_v[n] = adamw_call(W[n], gbig[n], M[n], V[n])
    out_g["w_mod"] = g_wmod
    out_d["w_mod"], out_m["w_mod"], out_v["w_mod"] = adamw_call(w_mod, g_wmod, m_w_mod, v_w_mod)

    pack_small = lambda t: _pad_rows(jnp.concatenate([t[n].reshape(-1) for n in SMALL]), n_small_rows)
    res = adamw_small_call(g1[:, 2 * n_mod_rows:], pack_small(W), pack_small(M), pack_small(V))
    off = 0
    for n in SMALL:
        sz = W[n].size
        for dst, r in zip((out_g, out_d, out_m, out_v), res):
            dst[n] = r.reshape(-1)[off:off + sz].reshape(W[n].shape)
        off += sz
    bparts = jnp.concatenate([g1[:, :n_mod_rows], dm_ctx_parts], axis=0)
    as_rows = lambda t: t.reshape(n_mod_rows, PACK_C)
    res = adamw_small_call(bparts, as_rows(b_mod), as_rows(m_b_mod), as_rows(v_b_mod))
    for dst, r in zip((out_g, out_d, out_m, out_v), res):
        dst["b_mod"] = r.reshape(b_mod.shape)
    row8 = lambda t: jnp.pad(t[None, :], ((0, 7), (0, 0)))
    res = adamw_small_call(ds_parts, row8(c_ctx), row8(m_c_ctx), row8(v_c_ctx), silu_bwd=True)
    for dst, r in zip((out_g, out_d, out_m, out_v), res):
        dst["c_ctx"] = r[0]

    return (loss, grad_x, *[out_g[n] for n in order], *[out_d[n] for n in order],
            *[out_m[n] for n in order], *[out_v[n] for n in order])
```

```python
import functools
import math

import numpy as np
import jax
import jax.numpy as jnp
from jax import lax
from jax.experimental import pallas as pl
from jax.experimental.pallas import tpu as pltpu

F32 = jnp.float32
BF16 = jnp.bfloat16
MESH = pl.DeviceIdType.MESH

HEAD_DIM = 64
NA_HEADS = 8
GQA_Q_HEADS = 8
GQA_KV_HEADS = 2
GRID_W = 64
NA_WIN_H = 8
NA_WIN_W = 16
ROPE_THETA = 10000.0
EPS = 1e-6
SCALE = HEAD_DIM ** -0.5
NEG = -1e30

ADAM_LR = 0.001
ADAM_B1 = 0.9
ADAM_B2 = 0.999
ADAM_EPS = 1e-08
ADAM_WD = 0.01
ADAM_STEP = 10

TM = 256
NA_ROWS = 4
NA_KROWS = 12
LANES = 128
VMEM_BIG = 56 * 1024 * 1024


def _cp(sem, vmem=None):
    return pltpu.CompilerParams(dimension_semantics=sem, vmem_limit_bytes=vmem)


def _dot(a, b):
    return jnp.dot(a, b, preferred_element_type=F32)


def _dot_nt(a, b):
    return lax.dot_general(a, b, (((1,), (1,)), ((), ())), preferred_element_type=F32)


def _dot_tn(a, b):
    return lax.dot_general(a, b, (((0,), (0,)), ((), ())), preferred_element_type=F32)


def _colsum8(v):
    tm, d = v.shape
    return v.reshape(tm // 8, 8, d).sum(axis=0)


def _acc(ref, val, first):
    @pl.when(first)
    def _():
        ref[0] = val

    @pl.when(jnp.logical_not(first))
    def _():
        ref[0] += val


def _rms_r(x):
    return lax.rsqrt(jnp.mean(x * x, axis=-1, keepdims=True) + EPS)


def _norm_mod(x, nw, sh, sc):
    return (x * _rms_r(x) * nw) * (1.0 + sc) + sh


def _norm_mod_bwd(dh, x, nw, sc):
    r = _rms_r(x)
    xn = x * r
    dxn = dh * (nw * (1.0 + sc))
    dx = r * (dxn - xn * jnp.mean(dxn * xn, axis=-1, keepdims=True))
    return dx, dh, dh * (xn * nw), dh * ((1.0 + sc) * xn)


def _sigmoid(x):
    return 1.0 / (1.0 + jnp.exp(-x))


def _group_of(i, n_lat_tiles):
    return jnp.where(i >= n_lat_tiles, 1, 0)


def _lane_half(shape):
    return (lax.broadcasted_iota(jnp.int32, shape, len(shape) - 1) % LANES) // HEAD_DIM


def _lane_lt64(shape):
    return _lane_half(shape) == 0


def _roll64(x):
    return pltpu.roll(x, HEAD_DIM, x.ndim - 1)


def norm_mod_call(x, nw, modv, n_lat_tiles, which):
    T, D = x.shape

    def body(x_ref, nw_ref, mod_ref, h_ref):
        sh = mod_ref[0, 3 * which:3 * which + 1, :]
        sc = mod_ref[0, 3 * which + 1:3 * which + 2, :]
        h_ref[...] = _norm_mod(x_ref[...], nw_ref[...], sh, sc).astype(BF16)

    return pl.pallas_call(
        body, name="norm_mod", grid=(T // TM,),
        in_specs=[pl.BlockSpec((TM, D), lambda i: (i, 0)),
                  pl.BlockSpec((1, D), lambda i: (0, 0)),
                  pl.BlockSpec((1, 6, D), lambda i: (_group_of(i, n_lat_tiles), 0, 0))],
        out_specs=pl.BlockSpec((TM, D), lambda i: (i, 0)),
        out_shape=jax.ShapeDtypeStruct((T, D), BF16),
        compiler_params=_cp(("parallel",)),
    )(x, nw, modv)


def qkv_call(h, w_in):
    T, D = h.shape
    N = w_in.shape[1]
    n_na = 3 * NA_HEADS * HEAD_DIM
    n_g = (GQA_Q_HEADS + 2 * GQA_KV_HEADS) * HEAD_DIM
    n_gate = N - n_na - n_g

    def body(h_ref, w_ref, na_ref, g_ref, gate_ref):
        acc = _dot(h_ref[...], w_ref[...])
        na_ref[...] = acc[:, :n_na].astype(BF16)
        g_ref[...] = acc[:, n_na:n_na + n_g].astype(BF16)
        gate_ref[...] = acc[:, n_na + n_g:].astype(BF16)

    return pl.pallas_call(
        body, name="qkv", grid=(T // TM,),
        in_specs=[pl.BlockSpec((TM, D), lambda i: (i, 0)),
                  pl.BlockSpec((D, N), lambda i: (0, 0))],
        out_specs=[pl.BlockSpec((TM, n_na), lambda i: (i, 0)),
                   pl.BlockSpec((TM, n_g), lambda i: (i, 0)),
                   pl.BlockSpec((TM, n_gate), lambda i: (i, 0))],
        out_shape=[jax.ShapeDtypeStruct((T, n_na), BF16),
                   jax.ShapeDtypeStruct((T, n_g), BF16),
                   jax.ShapeDtypeStruct((T, n_gate), BF16)],
        compiler_params=_cp(("parallel",), VMEM_BIG),
    )(h, w_in)


def merge_call(ya, yb, gates, w_pa, w_pb):
    T, Ka = ya.shape
    D = w_pa.shape[1]

    def body(ya_ref, yb_ref, ga_ref, gb_ref, wa_ref, wb_ref, m_ref, pa_ref, pb_ref):
        pa = _dot(ya_ref[...], wa_ref[...])
        pb = _dot(yb_ref[...], wb_ref[...])
        m = _sigmoid(ga_ref[...].astype(F32)) * pa + _sigmoid(gb_ref[...].astype(F32)) * pb
        m_ref[...] = m.astype(BF16)
        pa_ref[...] = pa.astype(BF16)
        pb_ref[...] = pb.astype(BF16)

    row = lambda w: pl.BlockSpec((TM, w), lambda i: (i, 0))
    return pl.pallas_call(
        body, name="merge", grid=(T // TM,),
        in_specs=[row(Ka), row(Ka),
                  pl.BlockSpec((TM, D), lambda i: (i, 0)), pl.BlockSpec((TM, D), lambda i: (i, 1)),
                  pl.BlockSpec((Ka, D), lambda i: (0, 0)), pl.BlockSpec((Ka, D), lambda i: (0, 0))],
        out_specs=[row(D), row(D), row(D)],
        out_shape=[jax.ShapeDtypeStruct((T, D), BF16)] * 3,
        compiler_params=_cp(("parallel",)),
    )(ya, yb, gates, gates, w_pa, w_pb)


def wo_call(merged, w_o, x, modv, nw2, n_lat_tiles):
    T, D = x.shape

    def body(m_ref, w_ref, x_ref, mod_ref, nw_ref, x1_ref, o_ref, h2_ref):
        o = _dot(m_ref[...], w_ref[...])
        x1 = x_ref[...] + mod_ref[0, 2:3, :] * o
        x1_ref[...] = x1
        o_ref[...] = o.astype(BF16)
        h2_ref[...] = _norm_mod(x1, nw_ref[...], mod_ref[0, 3:4, :], mod_ref[0, 4:5, :]).astype(BF16)

    row = pl.BlockSpec((TM, D), lambda i: (i, 0))
    return pl.pallas_call(
        body, name="wo", grid=(T // TM,),
        in_specs=[row, pl.BlockSpec((D, D), lambda i: (0, 0)), row,
                  pl.BlockSpec((1, 6, D), lambda i: (_group_of(i, n_lat_tiles), 0, 0)),
                  pl.BlockSpec((1, D), lambda i: (0, 0))],
        out_specs=[row, row, row],
        out_shape=[jax.ShapeDtypeStruct((T, D), F32), jax.ShapeDtypeStruct((T, D), BF16),
                   jax.ShapeDtypeStruct((T, D), BF16)],
        compiler_params=_cp(("parallel",)),
    )(merged, w_o, x, modv, nw2)


def ffn_in_call(h2, w_ffn_in, n_col_tiles=2):
    T, D = h2.shape
    F = w_ffn_in.shape[1] // 2
    tn = F // n_col_tiles

    def body(h_ref, wa_ref, wu_ref, act_ref, a_ref, u_ref):
        h = h_ref[...]
        a = _dot(h, wa_ref[...])
        u = _dot(h, wu_ref[...])
        act_ref[...] = (a * _sigmoid(a) * u).astype(BF16)
        a_ref[...] = a.astype(BF16)
        u_ref[...] = u.astype(BF16)

    blk = pl.BlockSpec((TM, tn), lambda j, i: (i, j))
    return pl.pallas_call(
        body, name="ffn_in", grid=(n_col_tiles, T // TM),
        in_specs=[pl.BlockSpec((TM, D), lambda j, i: (i, 0)),
                  pl.BlockSpec((D, tn), lambda j, i: (0, j)),
                  pl.BlockSpec((D, tn), lambda j, i: (0, n_col_tiles + j))],
        out_specs=[blk, blk, blk],
        out_shape=[jax.ShapeDtypeStruct((T, F), BF16)] * 3,
        compiler_params=_cp(("parallel", "parallel"), VMEM_BIG),
    )(h2, w_ffn_in, w_ffn_in)


def ffn_out_call(act, w_ffn_out, x1, modv, n_lat_tiles, next_nw=None, next_modv=None):
    T, D = x1.shape
    F = act.shape[1]
    with_next = next_nw is not None

    def body(*refs):
        if with_next:
            a_ref, w_ref, x_ref, mod_ref, nw_ref, nmod_ref, x2_ref, f_ref, hn_ref = refs
        else:
            a_ref, w_ref, x_ref, mod_ref, x2_ref, f_ref = refs
        f = _dot(a_ref[...], w_ref[...])
        x2 = x_ref[...] + mod_ref[0, 5:6, :] * f
        x2_ref[...] = x2
        f_ref[...] = f.astype(BF16)
        if with_next:
            hn_ref[...] = _norm_mod(x2, nw_ref[...], nmod_ref[0, 0:1, :], nmod_ref[0, 1:2, :]).astype(BF16)

    row = pl.BlockSpec((TM, D), lambda i: (i, 0))
    modspec = pl.BlockSpec((1, 6, D), lambda i: (_group_of(i, n_lat_tiles), 0, 0))
    in_specs = [pl.BlockSpec((TM, F), lambda i: (i, 0)), pl.BlockSpec((F, D), lambda i: (0, 0)), row, modspec]
    args = [act, w_ffn_out, x1, modv]
    out_specs = [row, row]
    out_shape = [jax.ShapeDtypeStruct((T, D), F32), jax.ShapeDtypeStruct((T, D), BF16)]
    if with_next:
        in_specs += [pl.BlockSpec((1, D), lambda i: (0, 0)), modspec]
        args += [next_nw, next_modv]
        out_specs.append(row)
        out_shape.append(jax.ShapeDtypeStruct((T, D), BF16))
    return pl.pallas_call(
        body, name="ffn_out", grid=(T // TM,), in_specs=in_specs, out_specs=out_specs, out_shape=out_shape,
        compiler_params=_cp(("parallel",), VMEM_BIG),
    )(*args)


def loss_call(x, nw, target, n_lat_tiles):
    T, D = x.shape

    def body(x_ref, nw_ref, t_ref, dx_ref, dw_ref, ls_ref):
        i = pl.program_id(0)

        @pl.when(i < n_lat_tiles)
        def _():
            xv = x_ref[...]
            nw_v = nw_ref[...]
            r = _rms_r(xv)
            xn = xv * r
            e = xn * nw_v - t_ref[...]
            dy = e * (1.0 / D)
            dxn = dy * nw_v
            dx_ref[...] = r * (dxn - xn * jnp.mean(dxn * xn, axis=-1, keepdims=True))
            _acc(dw_ref, _colsum8(dy * xn), i == 0)
            _acc(ls_ref, _colsum8(e * e * (0.5 / D)), i == 0)

        @pl.when(i >= n_lat_tiles)
        def _():
            dx_ref[...] = jnp.zeros((TM, D), F32)

    stat = pl.BlockSpec((1, 8, D), lambda i: (0, 0, 0))
    return pl.pallas_call(
        body, name="loss", grid=(T // TM,),
        in_specs=[pl.BlockSpec((TM, D), lambda i: (i, 0)), pl.BlockSpec((1, D), lambda i: (0, 0)),
                  pl.BlockSpec((TM, D), lambda i: (jnp.minimum(i, n_lat_tiles - 1), 0))],
        out_specs=[pl.BlockSpec((TM, D), lambda i: (i, 0)), stat, stat],
        out_shape=[jax.ShapeDtypeStruct((T, D), F32), jax.ShapeDtypeStruct((1, 8, D), F32),
                   jax.ShapeDtypeStruct((1, 8, D), F32)],
        compiler_params=_cp(("arbitrary",)),
    )(x, nw, target)


def dact_call(dx2, modv, fsave, w_ffn_out, a, u, n_lat_tiles):
    T, D = dx2.shape
    F = a.shape[1]

    def body(dx_ref, mod_ref, f_ref, w_ref, a_ref, u_ref, da_ref, du_ref, df_ref, dg_ref):
        i = pl.program_id(0)
        dx = dx_ref[...]
        df = (dx * mod_ref[0, 5:6, :]).astype(BF16)
        df_ref[...] = df
        dact = _dot_nt(df, w_ref[...])
        av = a_ref[...].astype(F32)
        uv = u_ref[...].astype(F32)
        sg = _sigmoid(av)
        da_ref[...] = (dact * uv * (sg * (1.0 + av * (1.0 - sg)))).astype(BF16)
        du_ref[...] = (dact * (av * sg)).astype(BF16)
        _acc(dg_ref, _colsum8(dx * f_ref[...].astype(F32)), (i == 0) | (i == n_lat_tiles))

    row = pl.BlockSpec((TM, D), lambda i: (i, 0))
    wide = pl.BlockSpec((TM, F), lambda i: (i, 0))
    grp = lambda i: (_group_of(i, n_lat_tiles), 0, 0)
    return pl.pallas_call(
        body, name="dact", grid=(T // TM,),
        in_specs=[row, pl.BlockSpec((1, 6, D), grp), row, pl.BlockSpec((F, D), lambda i: (0, 0)), wide, wide],
        out_specs=[wide, wide, row, pl.BlockSpec((1, 8, D), grp)],
        out_shape=[jax.ShapeDtypeStruct((T, F), BF16), jax.ShapeDtypeStruct((T, F), BF16),
                   jax.ShapeDtypeStruct((T, D), BF16), jax.ShapeDtypeStruct((2, 8, D), F32)],
        compiler_params=_cp(("arbitrary",), VMEM_BIG),
    )(dx2, modv, fsave, w_ffn_out, a, u)


def _norm_bwd_tail(dh, x_ref, mod_ref, nw_ref, dres_ref, dx_ref, s_sh, s_sc, s_w, which, first):
    sc = mod_ref[0, 3 * which + 1:3 * which + 2, :]
    dxn, t_sh, t_sc, t_w = _norm_mod_bwd(dh, x_ref[...], nw_ref[...], sc)
    dx_ref[...] = dres_ref[...] + dxn
    _acc(s_sh, _colsum8(t_sh), first)
    _acc(s_sc, _colsum8(t_sc), first)
    _acc(s_w, _colsum8(t_w), first)


def dh2_call(da, du, w_ffn_in, x1, modv, nw2, dx2, n_lat_tiles):
    T, D = x1.shape
    F = da.shape[1]

    def body(da_ref, du_ref, wa_ref, wu_ref, x_ref, mod_ref, nw_ref, dres_ref, dx_ref, s_sh, s_sc, s_w):
        i = pl.program_id(0)
        dh = _dot_nt(da_ref[...], wa_ref[...]) + _dot_nt(du_ref[...], wu_ref[...])
        _norm_bwd_tail(dh, x_ref, mod_ref, nw_ref, dres_ref, dx_ref, s_sh, s_sc, s_w, 1,
                       (i == 0) | (i == n_lat_tiles))

    row = pl.BlockSpec((TM, D), lambda i: (i, 0))
    wide = pl.BlockSpec((TM, F), lambda i: (i, 0))
    grp = lambda i: (_group_of(i, n_lat_tiles), 0, 0)
    stat = pl.BlockSpec((1, 8, D), grp)
    return pl.pallas_call(
        body, name="dh2", grid=(T // TM,),
        in_specs=[wide, wide, pl.BlockSpec((D, F), lambda i: (0, 0)), pl.BlockSpec((D, F), lambda i: (0, 1)),
                  row, pl.BlockSpec((1, 6, D), grp), pl.BlockSpec((1, D), lambda i: (0, 0)), row],
        out_specs=[row, stat, stat, stat],
        out_shape=[jax.ShapeDtypeStruct((T, D), F32)] + [jax.ShapeDtypeStruct((2, 8, D), F32)] * 3,
        compiler_params=_cp(("arbitrary",), VMEM_BIG),
    )(da, du, w_ffn_in, w_ffn_in, x1, modv, nw2, dx2)


def dmerged_call(dx1, modv, osave, w_o, pa, pb, gates, n_lat_tiles):
    T, D = dx1.shape

    def body(dx_ref, mod_ref, o_ref, w_ref, pa_ref, pb_ref, ga_ref, gb_ref,
             dpa_ref, dpb_ref, dgate_ref, do_ref, dg_ref):
        i = pl.program_id(0)
        dx = dx_ref[...]
        do = (dx * mod_ref[0, 2:3, :]).astype(BF16)
        do_ref[...] = do
        dm = _dot_nt(do, w_ref[...])
        sa = _sigmoid(ga_ref[...].astype(F32))
        sb = _sigmoid(gb_ref[...].astype(F32))
        dpa_ref[...] = (dm * sa).astype(BF16)
        dpb_ref[...] = (dm * sb).astype(BF16)
        dgate_ref[:, :D] = (dm * pa_ref[...].astype(F32) * (sa * (1.0 - sa))).astype(BF16)
        dgate_ref[:, D:] = (dm * pb_ref[...].astype(F32) * (sb * (1.0 - sb))).astype(BF16)
        _acc(dg_ref, _colsum8(dx * o_ref[...].astype(F32)), (i == 0) | (i == n_lat_tiles))

    row = pl.BlockSpec((TM, D), lambda i: (i, 0))
    grp = lambda i: (_group_of(i, n_lat_tiles), 0, 0)
    return pl.pallas_call(
        body, name="dmerged", grid=(T // TM,),
        in_specs=[row, pl.BlockSpec((1, 6, D), grp), row, pl.BlockSpec((D, D), lambda i: (0, 0)), row, row,
                  pl.BlockSpec((TM, D), lambda i: (i, 0)), pl.BlockSpec((TM, D), lambda i: (i, 1))],
        out_specs=[row, row, pl.BlockSpec((TM, 2 * D), lambda i: (i, 0)), row, pl.BlockSpec((1, 8, D), grp)],
        out_shape=[jax.ShapeDtypeStruct((T, D), BF16), jax.ShapeDtypeStruct((T, D), BF16),
                   jax.ShapeDtypeStruct((T, 2 * D), BF16), jax.ShapeDtypeStruct((T, D), BF16),
                   jax.ShapeDtypeStruct((2, 8, D), F32)],
        compiler_params=_cp(("arbitrary",)),
    )(dx1, modv, osave, w_o, pa, pb, gates, gates)


def dy_call(dpa, dpb, w_pa, w_pb):
    T, D = dpa.shape
    K = w_pa.shape[0]

    def body(a_ref, b_ref, wa_ref, wb_ref, ya_ref, yb_ref):
        ya_ref[...] = _dot_nt(a_ref[...], wa_ref[...]).astype(BF16)
        yb_ref[...] = _dot_nt(b_ref[...], wb_ref[...]).astype(BF16)

    row = pl.BlockSpec((TM, D), lambda i: (i, 0))
    w = pl.BlockSpec((K, D), lambda i: (0, 0))
    out = pl.BlockSpec((TM, K), lambda i: (i, 0))
    return pl.pallas_call(
        body, name="dy", grid=(T // TM,), in_specs=[row, row, w, w], out_specs=[out, out],
        out_shape=[jax.ShapeDtypeStruct((T, K), BF16)] * 2,
        compiler_params=_cp(("parallel",)),
    )(dpa, dpb, w_pa, w_pb)


def dh_call(dna, dg, dgates, w_in, x, modv, nw1, dx1, n_lat_tiles):
    T, D = x.shape
    n1, n2, n3 = dna.shape[1], dg.shape[1], dgates.shape[1]
    N = n1 + n2 + n3

    def body(a_ref, b_ref, c_ref, w_ref, x_ref, mod_ref, nw_ref, dres_ref, dx_ref, s_sh, s_sc, s_w):
        i = pl.program_id(0)
        dh = (_dot_nt(a_ref[...], w_ref[:, :n1]) + _dot_nt(b_ref[...], w_ref[:, n1:n1 + n2])
              + _dot_nt(c_ref[...], w_ref[:, n1 + n2:]))
        _norm_bwd_tail(dh, x_ref, mod_ref, nw_ref, dres_ref, dx_ref, s_sh, s_sc, s_w, 0,
                       (i == 0) | (i == n_lat_tiles))

    row = pl.BlockSpec((TM, D), lambda i: (i, 0))
    grp = lambda i: (_group_of(i, n_lat_tiles), 0, 0)
    stat = pl.BlockSpec((1, 8, D), grp)
    return pl.pallas_call(
        body, name="dh", grid=(T // TM,),
        in_specs=[pl.BlockSpec((TM, n1), lambda i: (i, 0)), pl.BlockSpec((TM, n2), lambda i: (i, 0)),
                  pl.BlockSpec((TM, n3), lambda i: (i, 0)), pl.BlockSpec((D, N), lambda i: (0, 0)),
                  row, pl.BlockSpec((1, 6, D), grp), pl.BlockSpec((1, D), lambda i: (0, 0)), row],
        out_specs=[row, stat, stat, stat],
        out_shape=[jax.ShapeDtypeStruct((T, D), F32)] + [jax.ShapeDtypeStruct((2, 8, D), F32)] * 3,
        compiler_params=_cp(("arbitrary",), VMEM_BIG),
    )(dna, dg, dgates, w_in, x, modv, nw1, dx1)


def _pick_tile(n, cap, mult):
    best = None
    for d in range(mult, min(n, cap) + 1, mult):
        if n % d == 0:
            best = d
    assert best is not None, (n, cap, mult)
    return best


def wgrad_call(a, b, name):
    T, K = a.shape
    N = b.shape[1]
    tt = _pick_tile(T, 768, 128)
    tk = K if K <= 1024 else _pick_tile(K, 1408, 128)
    tn = N if N <= 2304 else _pick_tile(N, 2304, 128)
    nt = T // tt

    def body(a_ref, b_ref, o_ref, acc_ref):
        t = pl.program_id(2)
        p = _dot_tn(a_ref[...], b_ref[...])

        @pl.when(t == 0)
        def _():
            acc_ref[...] = p

        @pl.when(t > 0)
        def _():
            acc_ref[...] += p

        @pl.when(t == nt - 1)
        def _():
            o_ref[...] = acc_ref[...].astype(BF16)

    return pl.pallas_call(
        body, name=name, grid=(K // tk, N // tn, nt),
        in_specs=[pl.BlockSpec((tt, tk), lambda i, j, t: (t, i)), pl.BlockSpec((tt, tn), lambda i, j, t: (t, j))],
        out_specs=pl.BlockSpec((tk, tn), lambda i, j, t: (i, j)),
        out_shape=jax.ShapeDtypeStruct((K, N), BF16),
        scratch_shapes=[pltpu.VMEM((tk, tn), F32)],
        compiler_params=_cp(("parallel", "parallel", "arbitrary"), VMEM_BIG),
    )(a, b)


def _head_mean(v):
    n = v.shape[-1]
    r = lax.broadcasted_iota(jnp.int32, (n, n), 0) // HEAD_DIM
    c = lax.broadcasted_iota(jnp.int32, (n, n), 1) // HEAD_DIM
    bd = jnp.where(r == c, 1.0 / HEAD_DIM, 0.0).astype(BF16)
    hi = v.astype(BF16)
    lo = (v - hi.astype(F32)).astype(BF16)
    return _dot(hi, bd) + _dot(lo, bd)


def _swap_pairs(x):
    n = x.shape[-1]
    even = lax.broadcasted_iota(jnp.int32, x.shape, x.ndim - 1) % 2 == 0
    return jnp.where(even, pltpu.roll(x, n - 1, x.ndim - 1), pltpu.roll(x, 1, x.ndim - 1))


def _tile_lanes(t, n):
    return jnp.tile(t, (1, n // t.shape[-1]))


def gqa_prep_call(gqkv, qgain, kgain, cos_t, sin_t):
    T = gqkv.shape[0]
    nq = GQA_Q_HEADS * HEAD_DIM
    nk = GQA_KV_HEADS * HEAD_DIM
    rep = GQA_Q_HEADS // GQA_KV_HEADS

    def body(g_ref, qg_ref, kg_ref, c_ref, s_ref, qp_ref, kp_ref):
        cos_v, sin_v = c_ref[...], s_ref[...]
        xq = g_ref[:, :nq].astype(F32)
        yq = xq * lax.rsqrt(_head_mean(xq * xq) + EPS) * qg_ref[...]
        yq = (yq * _tile_lanes(cos_v, nq) + _swap_pairs(yq) * _tile_lanes(sin_v, nq)) * SCALE
        lo = _lane_lt64((TM, LANES))
        for p in range(GQA_Q_HEADS // 2):
            g = (2 * p) // rep
            chunk = yq[:, LANES * p:LANES * (p + 1)]
            rolled = _roll64(chunk)
            if g == 0:
                first, second = jnp.where(lo, chunk, 0.0), jnp.where(lo, rolled, 0.0)
            else:
                first, second = jnp.where(lo, 0.0, rolled), jnp.where(lo, 0.0, chunk)
            qp_ref[:, 2 * LANES * p:2 * LANES * p + LANES] = first.astype(BF16)
            qp_ref[:, 2 * LANES * p + LANES:2 * LANES * (p + 1)] = second.astype(BF16)
        xk = g_ref[:, nq:nq + nk].astype(F32)
        yk = xk * lax.rsqrt(_head_mean(xk * xk) + EPS) * kg_ref[...]
        kp_ref[...] = (yk * cos_v + _swap_pairs(yk) * sin_v).astype(BF16)

    return pl.pallas_call(
        body, name="gqa_prep", grid=(T // TM,),
        in_specs=[pl.BlockSpec((TM, gqkv.shape[1]), lambda i: (i, 0)),
                  pl.BlockSpec((1, nq), lambda i: (0, 0)), pl.BlockSpec((1, nk), lambda i: (0, 0)),
                  pl.BlockSpec((TM, nk), lambda i: (i, 0)), pl.BlockSpec((TM, nk), lambda i: (i, 0))],
        out_specs=[pl.BlockSpec((TM, GQA_Q_HEADS * LANES), lambda i: (i, 0)), pl.BlockSpec((TM, nk), lambda i: (i, 0))],
        out_shape=[jax.ShapeDtypeStruct((T, GQA_Q_HEADS * LANES), BF16), jax.ShapeDtypeStruct((T, nk), BF16)],
        compiler_params=_cp(("parallel",)),
    )(gqkv, qgain, kgain, cos_t, sin_t)


def gqa_prep_bwd_call(gqkv, qgain, kgain, cos_t, sin_t, dqp, dkp, dv):
    T = gqkv.shape[0]
    nq = GQA_Q_HEADS * HEAD_DIM
    nk = GQA_KV_HEADS * HEAD_DIM
    rep = GQA_Q_HEADS // GQA_KV_HEADS

    def norm_rope_bwd(x, gain, dy, cos_v, sin_v):
        dyn = dy * cos_v - _swap_pairs(dy) * sin_v
        r = lax.rsqrt(_head_mean(x * x) + EPS)
        xn = x * r
        dxn = dyn * gain
        dx = r * (dxn - xn * _head_mean(dxn * xn))
        return dx, dyn * xn

    def body(g_ref, qg_ref, kg_ref, c_ref, s_ref, dqp_ref, dkp_ref, dv_ref, dg_ref, sq_ref, sk_ref):
        i = pl.program_id(0)
        cos_v, sin_v = c_ref[...], s_ref[...]
        lo = _lane_lt64((TM, LANES))
        chunks = []
        for p in range(GQA_Q_HEADS // 2):
            g = (2 * p) // rep
            da = dqp_ref[:, 2 * LANES * p:2 * LANES * p + LANES]
            db = dqp_ref[:, 2 * LANES * p + LANES:2 * LANES * (p + 1)]
            if g == 0:
                chunks.append(jnp.where(lo, da, _roll64(db)))
            else:
                chunks.append(jnp.where(lo, _roll64(da), db))
        dyq = jnp.concatenate(chunks, axis=1) * SCALE
        dxq, gq_term = norm_rope_bwd(g_ref[:, :nq].astype(F32), qg_ref[...], dyq,
                                     _tile_lanes(cos_v, nq), _tile_lanes(sin_v, nq))
        dxk, gk_term = norm_rope_bwd(g_ref[:, nq:nq + nk].astype(F32), kg_ref[...], dkp_ref[...], cos_v, sin_v)
        dg_ref[:, :nq] = dxq.astype(BF16)
        dg_ref[:, nq:nq + nk] = dxk.astype(BF16)
        dg_ref[:, nq + nk:] = dv_ref[...].astype(BF16)
        _acc(sq_ref, _colsum8(gq_term), i == 0)
        _acc(sk_ref, _colsum8(gk_term), i == 0)

    W = gqkv.shape[1]
    return pl.pallas_call(
        body, name="gqa_prep_bwd", grid=(T // TM,),
        in_specs=[pl.BlockSpec((TM, W), lambda i: (i, 0)),
                  pl.BlockSpec((1, nq), lambda i: (0, 0)), pl.BlockSpec((1, nk), lambda i: (0, 0)),
                  pl.BlockSpec((TM, nk), lambda i: (i, 0)), pl.BlockSpec((TM, nk), lambda i: (i, 0)),
                  pl.BlockSpec((TM, GQA_Q_HEADS * LANES), lambda i: (i, 0)),
                  pl.BlockSpec((TM, nk), lambda i: (i, 0)), pl.BlockSpec((TM, nk), lambda i: (i, 0))],
        out_specs=[pl.BlockSpec((TM, W), lambda i: (i, 0)),
                   pl.BlockSpec((1, 8, nq), lambda i: (0, 0, 0)), pl.BlockSpec((1, 8, nk), lambda i: (0, 0, 0))],
        out_shape=[jax.ShapeDtypeStruct((T, W), BF16), jax.ShapeDtypeStruct((1, 8, nq), F32),
                   jax.ShapeDtypeStruct((1, 8, nk), F32)],
        compiler_params=_cp(("arbitrary",)),
    )(gqkv, qgain, kgain, cos_t, sin_t, dqp, dkp, dv)


GQ_TQ_LATENT = 512
GQ_TK = 256
GQ_CHUNKS_PER_TRIP = 8


def _chunk_offset(j):
    return j * GQ_TK if isinstance(j, int) else pl.multiple_of(j * GQ_TK, GQ_TK)


def _gqa_blocks(n_lat, T, latent):
    if latent:
        tq = GQ_TQ_LATENT if n_lat % GQ_TQ_LATENT == 0 else GQ_TK
        return tq, 0, n_lat // tq
    return GQ_TK, n_lat // GQ_TK, (T - n_lat) // GQ_TK


def _sweep_keys(chunk, latent, n_lat_k, nk):
    if latent:
        per_trip = math.gcd(n_lat_k, GQ_CHUNKS_PER_TRIP)

        def trip(jj, carry):
            for t in range(per_trip):
                chunk(per_trip * jj + t)
            return carry

        lax.fori_loop(0, n_lat_k // per_trip, trip, 0)
    for j in range(n_lat_k, nk):
        chunk(j)


def _gqa_place(chunk, half, g):
    gv = jnp.broadcast_to(g, chunk.shape)
    placed = jnp.where(gv == half, chunk, _roll64(chunk))
    return jnp.where(_lane_half(chunk.shape) == gv, placed, jnp.zeros_like(placed))


GQA_SMALL_SCORE = 20.0


def gqa_scores_small(q_gain, k_gain):
    bound = SCALE * HEAD_DIM * jnp.max(jnp.abs(q_gain)) * jnp.max(jnp.abs(k_gain))
    return (bound <= GQA_SMALL_SCORE).astype(jnp.int32).reshape(1)


def gqa_fwd_call(qp, kp, gqkv, n_lat, scores_small, gather=None):
    first = _gqa_fwd_part(qp, kp, gqkv, n_lat, scores_small, True, None, gather)
    y, lse = _gqa_fwd_part(qp, kp, gqkv, n_lat, scores_small, False, (first[0], first[1]), None)
    return (y, lse) + tuple(first[2:])


def _gqa_fwd_part(qp, kp, gqkv, n_lat, scores_small, latent, prev, gather):
    T = qp.shape[0]
    rep = GQA_Q_HEADS // GQA_KV_HEADS
    GQ_TQ, q0, nq = _gqa_blocks(n_lat, T, latent)
    nk = T // GQ_TK
    n_lat_k = n_lat // GQ_TK
    v_blk = (GQA_Q_HEADS + GQA_KV_HEADS) * HEAD_DIM // LANES
    R = rep * GQ_TQ

    def body(*refs):
        if prev is not None:
            small_ref, q_ref, k_ref, v_ref, _, _, y_ref, lse_ref, m_sc, acc_sc = refs
        elif gather is None:
            small_ref, q_ref, k_ref, v_ref, y_ref, lse_ref, m_sc, acc_sc = refs
        else:
            nb = len(gather)
            small_ref, q_ref, k_ref, v_ref = refs[:4]
            pk_refs = refs[4:4 + nb]
            y_ref, lse_ref = refs[4 + nb:6 + nb]
            ga_refs = refs[6 + nb:6 + 2 * nb]
            m_sc, acc_sc, ssem, rsem = refs[6 + 2 * nb:]
        g = pl.program_id(0)
        qb = pl.program_id(1)
        if gather is not None:
            @pl.when((g == 0) & (qb == 0))
            def _():
                _ag_start(pk_refs, ga_refs, ssem, rsem)
        qs = jnp.concatenate([q_ref[:, LANES * i:LANES * (i + 1)] for i in range(rep)], axis=0)
        is_lat = latent
        nsub = GQ_TK // LANES
        scores_small = small_ref[0] == 1

        @pl.when(scores_small)
        def _():
            m_sc[...] = jnp.zeros((R, LANES), F32)

        @pl.when(jnp.logical_not(scores_small))
        def _():
            m_sc[...] = jnp.full((R, LANES), NEG, F32)

            def chunk_max(j):
                s = _dot_nt(qs, k_ref[pl.ds(_chunk_offset(j), GQ_TK), :])
                mm = s[:, :LANES]
                for t in range(1, nsub):
                    mm = jnp.maximum(mm, s[:, LANES * t:LANES * (t + 1)])
                m_sc[...] = jnp.maximum(m_sc[...], mm)

            _sweep_keys(chunk_max, is_lat, n_lat_k, nk)
            m_sc[...] = jnp.broadcast_to(jnp.max(m_sc[...], axis=-1, keepdims=True), (R, LANES))

        acc_sc[...] = jnp.zeros((R, LANES), F32)
        own_v = _lane_half((GQ_TK, LANES)) == jnp.broadcast_to(g, (GQ_TK, LANES))

        def chunk_acc(j):
            off = _chunk_offset(j)
            vc = v_ref[pl.ds(off, GQ_TK), :]
            v1 = jnp.where(own_v, vc, jnp.ones_like(vc))
            s = _dot_nt(qs, k_ref[pl.ds(off, GQ_TK), :])
            mb = m_sc[...]
            p = jnp.concatenate([jnp.exp(s[:, LANES * t:LANES * (t + 1)] - mb).astype(BF16) for t in range(nsub)],
                                axis=1)
            acc_sc[...] += _dot(p, v1)

        _sweep_keys(chunk_acc, is_lat, n_lat_k, nk)
        acc = acc_sc[...]
        own = _lane_half((R, LANES)) == jnp.broadcast_to(g, (R, LANES))
        l = jnp.where(own, _roll64(acc), acc)
        o = acc / l
        lse = m_sc[...] + jnp.log(l)
        lo = _lane_lt64((GQ_TQ, LANES))
        g0 = jnp.broadcast_to(g, (GQ_TQ, LANES)) == 0
        for c2 in range(rep // 2):
            oa = o[(2 * c2) * GQ_TQ:(2 * c2 + 1) * GQ_TQ]
            ob = o[(2 * c2 + 1) * GQ_TQ:(2 * c2 + 2) * GQ_TQ]
            left = jnp.where(g0, oa, _roll64(oa))
            right = jnp.where(g0, _roll64(ob), ob)
            y_ref[:, LANES * c2:LANES * (c2 + 1)] = jnp.where(lo, left, right).astype(BF16)
        for i in range(rep):
            lse_ref[i] = lse[i * GQ_TQ:(i + 1) * GQ_TQ]
        if gather is not None:
            @pl.when((g == GQA_KV_HEADS - 1) & (qb == nq - 1))
            def _():
                _ag_finish(pk_refs, ga_refs, ssem, rsem)

    in_specs = [pl.BlockSpec((GQ_TQ, rep * LANES), lambda g, i, s: (q0 + i, g)),
                pl.BlockSpec((T, LANES), lambda g, i, s: (0, 0)),
                pl.BlockSpec((T, LANES), lambda g, i, s: (0, v_blk))]
    out_specs = [pl.BlockSpec((GQ_TQ, rep * HEAD_DIM), lambda g, i, s: (q0 + i, g)),
                 pl.BlockSpec((rep, GQ_TQ, LANES), lambda g, i, s: (g, q0 + i, 0))]
    out_shape = [jax.ShapeDtypeStruct((T, GQA_Q_HEADS * HEAD_DIM), BF16),
                 jax.ShapeDtypeStruct((GQA_Q_HEADS, T, LANES), F32)]
    scratch = [pltpu.VMEM((R, LANES), F32), pltpu.VMEM((R, LANES), F32)]
    args = [scores_small, qp, kp, gqkv]
    aliases = {}
    name = "gqa_fwd" if latent else "gqa_fwd_ctx"
    if prev is not None:
        in_specs += [ANY, ANY]
        args += list(prev)
        aliases = {4: 0, 5: 1}
    if gather is not None:
        in_specs += [ANY] * len(gather)
        out_specs += [ANY] * len(gather)
        out_shape += [jax.ShapeDtypeStruct((N_CHIP,) + p.shape, p.dtype) for p in gather]
        scratch += [pltpu.SemaphoreType.DMA((AG_SEMS * len(gather),)), pltpu.SemaphoreType.DMA((AG_SEMS * len(gather),))]
        args += list(gather)
        name = "gqa_fwd_gather"
    return pl.pallas_call(
        body, name=name,
        grid_spec=pltpu.PrefetchScalarGridSpec(
            num_scalar_prefetch=1, grid=(GQA_KV_HEADS, nq), in_specs=in_specs, out_specs=out_specs,
            scratch_shapes=scratch),
        out_shape=out_shape, input_output_aliases=aliases,
        compiler_params=_cp(("arbitrary", "arbitrary") if gather is not None else ("parallel", "parallel"), VMEM_BIG),
    )(*args)


def gqa_bwd_call(qp, kp, gqkv, yb, dyb, lse, n_lat, exchange=None):
    first = _gqa_bwd_part(qp, kp, gqkv, yb, dyb, lse, n_lat, True, None, exchange)
    dq, dk_c, dv_c = _gqa_bwd_part(qp, kp, gqkv, yb, dyb, lse, n_lat, False, first[0], None)
    return (dq, first[1] + dk_c, first[2] + dv_c) + tuple(first[3:])


def _gqa_bwd_part(qp, kp, gqkv, yb, dyb, lse, n_lat, latent, prev_dq, exchange):
    T = qp.shape[0]
    rep = GQA_Q_HEADS // GQA_KV_HEADS
    GQ_TQ, q0, nq = _gqa_blocks(n_lat, T, latent)
    nk = T // GQ_TK
    n_lat_k = n_lat // GQ_TK
    v_blk = (GQA_Q_HEADS + GQA_KV_HEADS) * HEAD_DIM // LANES
    R = rep * GQ_TQ

    def body(*refs):
        if prev_dq is not None:
            q_ref, k_ref, v_ref, y_ref, dy_ref, lse_ref, _, dq_ref, dk_ref, dv_ref, dq_sc = refs
        elif exchange is None:
            q_ref, k_ref, v_ref, y_ref, dy_ref, lse_ref, dq_ref, dk_ref, dv_ref, dq_sc = refs
        else:
            nb = len(exchange)
            q_ref, k_ref, v_ref, y_ref, dy_ref, lse_ref = refs[:6]
            gp_refs = refs[6:6 + nb]
            dq_ref, dk_ref, dv_ref = refs[6 + nb:9 + nb]
            rx_refs = refs[9 + nb:9 + 2 * nb]
            dq_sc, ssem, rsem = refs[9 + 2 * nb:]
        g = pl.program_id(0)
        qb = pl.program_id(1)
        if exchange is not None:
            @pl.when((g == 0) & (qb == 0))
            def _():
                _rs_start(gp_refs, rx_refs, ssem, rsem)

        @pl.when((g == 0) & (qb == 0))
        def _():
            dk_ref[...] = jnp.zeros((T, LANES), F32)
            dv_ref[...] = jnp.zeros((T, LANES), F32)

        qs = jnp.concatenate([q_ref[:, LANES * i:LANES * (i + 1)] for i in range(rep)], axis=0)
        dos, deltas, lses = [], [], []
        for i in range(rep):
            c2, half = i // 2, i % 2
            dch = _gqa_place(dy_ref[:, LANES * c2:LANES * (c2 + 1)].astype(F32), half, g)
            ych = _gqa_place(y_ref[:, LANES * c2:LANES * (c2 + 1)].astype(F32), half, g)
            dos.append(dch.astype(BF16))
            deltas.append(jnp.sum(dch * ych, axis=-1, keepdims=True))
            lses.append(lse_ref[i])
        do_s = jnp.concatenate(dos, axis=0)
        delta = jnp.broadcast_to(jnp.concatenate(deltas, axis=0), (R, LANES))
        lse_v = jnp.concatenate(lses, axis=0)
        dq_sc[...] = jnp.zeros((R, LANES), F32)
        nsub = GQ_TK // LANES

        def chunk(j):
            off = _chunk_offset(j)
            kc = k_ref[pl.ds(off, GQ_TK), :]
            vc = v_ref[pl.ds(off, GQ_TK), :]
            s = _dot_nt(qs, kc)
            dp = _dot_nt(do_s, vc)
            ps, dss = [], []
            for t in range(nsub):
                pt = jnp.exp(s[:, LANES * t:LANES * (t + 1)] - lse_v)
                ps.append(pt.astype(BF16))
                dss.append((pt * (dp[:, LANES * t:LANES * (t + 1)] - delta)).astype(BF16))
            p = jnp.concatenate(ps, axis=1)
            ds = jnp.concatenate(dss, axis=1)
            dv_ref[pl.ds(off, GQ_TK), :] += _dot_tn(p, do_s)
            dk_ref[pl.ds(off, GQ_TK), :] += _dot_tn(ds, qs)
            dq_sc[...] += _dot(ds, kc)

        _sweep_keys(chunk, latent, n_lat_k, nk)
        for i in range(rep):
            dq_ref[:, LANES * i:LANES * (i + 1)] = dq_sc[i * GQ_TQ:(i + 1) * GQ_TQ, :]
        if exchange is not None:
            @pl.when((g == GQA_KV_HEADS - 1) & (qb == nq - 1))
            def _():
                _rs_finish(gp_refs, rx_refs, ssem, rsem)

    whole = pl.BlockSpec((T, LANES), lambda g, i: (0, 0))
    in_specs = [pl.BlockSpec((GQ_TQ, rep * LANES), lambda g, i: (q0 + i, g)),
                whole,
                pl.BlockSpec((T, LANES), lambda g, i: (0, v_blk)),
                pl.BlockSpec((GQ_TQ, rep * HEAD_DIM), lambda g, i: (q0 + i, g)),
                pl.BlockSpec((GQ_TQ, rep * HEAD_DIM), lambda g, i: (q0 + i, g)),
                pl.BlockSpec((rep, GQ_TQ, LANES), lambda g, i: (g, q0 + i, 0))]
    out_specs = [pl.BlockSpec((GQ_TQ, rep * LANES), lambda g, i: (q0 + i, g)), whole, whole]
    out_shape = [jax.ShapeDtypeStruct((T, GQA_Q_HEADS * LANES), F32),
                 jax.ShapeDtypeStruct((T, LANES), F32), jax.ShapeDtypeStruct((T, LANES), F32)]
    scratch = [pltpu.VMEM((R, LANES), F32)]
    args = [qp, kp, gqkv, yb, dyb, lse]
    aliases = {}
    name = "gqa_bwd" if latent else "gqa_bwd_ctx"
    if prev_dq is not None:
        in_specs.append(ANY)
        args.append(prev_dq)
        aliases = {6: 0}
    if exchange is not None:
        in_specs += [ANY] * len(exchange)
        out_specs += [ANY] * len(exchange)
        out_shape += [_rs_out_shape(g) for g in exchange]
        scratch += [pltpu.SemaphoreType.DMA((RS_SEMS * len(exchange),)), pltpu.SemaphoreType.DMA((RS_SEMS * len(exchange),))]
        args += list(exchange)
        name = "gqa_bwd_exchange"
    return pl.pallas_call(
        body, name=name, grid=(GQA_KV_HEADS, nq),
        in_specs=in_specs, out_specs=out_specs, out_shape=out_shape, scratch_shapes=scratch,
        input_output_aliases=aliases,
        compiler_params=_cp(("arbitrary", "arbitrary"), VMEM_BIG),
    )(*args)


def _na_tables(rows):
    kh = min(NA_WIN_H, rows)
    assert kh == NA_WIN_H and rows >= NA_KROWS and rows % NA_ROWS == 0
    a = np.zeros((3, NA_ROWS, NA_KROWS, 2 * NA_WIN_H - 1), np.float32)
    for v, r0 in enumerate((0, NA_ROWS, rows - NA_ROWS)):
        ks = min(max(r0 - NA_WIN_H // 2, 0), rows - NA_KROWS)
        for rq in range(NA_ROWS):
            r = r0 + rq
            rs = min(max(r - kh // 2, 0), rows - kh)
            for rk in range(NA_KROWS):
                kr = ks + rk
                if rs <= kr < rs + kh:
                    a[v, rq, rk, kr - r + NA_WIN_H - 1] = 1.0
    c = np.zeros((GRID_W, GRID_W, 2 * NA_WIN_W - 1), np.float32)
    for wq in range(GRID_W):
        cs = min(max(wq - NA_WIN_W // 2, 0), GRID_W - NA_WIN_W)
        for wk in range(cs, cs + NA_WIN_W):
            c[wq, wk, min(max(wk - wq, -(NA_WIN_W - 1)), NA_WIN_W - 1) + NA_WIN_W - 1] = 1.0
    return a, c


NA_DR = 2 * NA_WIN_H - 1
NA_TK = 256


def _na_tile_index(rows):
    a, _ = _na_tables(rows)
    idx = np.where(a.sum(-1) > 0, a.argmax(-1), NA_DR).astype(np.int32)
    return jnp.asarray(idx.reshape(-1))


def na_bias_table(rpb, rows):
    _, c = _na_tables(rows)
    H = rpb.shape[0]
    KW = NA_KROWS * GRID_W
    c2 = np.concatenate([c, c], axis=1)
    t1 = jnp.einsum("hde,wue->hdwu", rpb, jnp.asarray(c2), precision=lax.Precision.HIGHEST)
    t1 = jnp.where(jnp.asarray(c2.sum(-1) > 0)[None, None], t1, NEG)
    t1 = jnp.concatenate([t1, jnp.full((H, 1, GRID_W, LANES), NEG, F32)], axis=1)

    def body(idx_ref, t_ref, o_ref):
        v = pl.program_id(1)
        lo = _lane_lt64((GRID_W, LANES))
        for q in range(NA_ROWS):
            for kp in range(NA_KROWS // 2):
                base = (v * NA_ROWS + q) * NA_KROWS + 2 * kp
                tile = jnp.where(lo, t_ref[0, idx_ref[base]], t_ref[0, idx_ref[base + 1]])
                o_ref[0, 0, GRID_W * q:GRID_W * (q + 1), LANES * kp:LANES * (kp + 1)] = tile

    return pl.pallas_call(
        body, name="na_bias_table",
        grid_spec=pltpu.PrefetchScalarGridSpec(
            num_scalar_prefetch=1, grid=(H, 3),
            in_specs=[pl.BlockSpec((1, NA_DR + 1, GRID_W, LANES), lambda h, v, i: (h, 0, 0, 0))],
            out_specs=pl.BlockSpec((1, 1, TM, KW), lambda h, v, i: (h, v, 0, 0))),
        out_shape=jax.ShapeDtypeStruct((H, 3, TM, KW), F32),
        compiler_params=_cp(("parallel", "parallel")),
    )(_na_tile_index(rows), t1)


def na_bias_grad(dbias, rows):
    _, c = _na_tables(rows)
    H = dbias.shape[0]
    KW = NA_KROWS * GRID_W

    def body(idx_ref, d_ref, o_ref):
        v = pl.program_id(1)
        lo = _lane_lt64((GRID_W, LANES))

        @pl.when(v == 0)
        def _():
            o_ref[...] = jnp.zeros((1, NA_DR + 1, GRID_W, LANES), F32)

        for q in range(NA_ROWS):
            for kp in range(NA_KROWS // 2):
                base = (v * NA_ROWS + q) * NA_KROWS + 2 * kp
                tile = d_ref[0, 0, GRID_W * q:GRID_W * (q + 1), LANES * kp:LANES * (kp + 1)]
                o_ref[0, idx_ref[base]] += jnp.where(lo, tile, 0.0)
                o_ref[0, idx_ref[base + 1]] += jnp.where(lo, 0.0, tile)

    dt = pl.pallas_call(
        body, name="na_bias_grad",
        grid_spec=pltpu.PrefetchScalarGridSpec(
            num_scalar_prefetch=1, grid=(H, 3),
            in_specs=[pl.BlockSpec((1, 1, TM, KW), lambda h, v, i: (h, v, 0, 0))],
            out_specs=pl.BlockSpec((1, NA_DR + 1, GRID_W, LANES), lambda h, v, i: (h, 0, 0, 0))),
        out_shape=jax.ShapeDtypeStruct((H, NA_DR + 1, GRID_W, LANES), F32),
        compiler_params=_cp(("parallel", "arbitrary")),
    )(_na_tile_index(rows), dbias)
    dt = dt[:, :NA_DR, :, :GRID_W] + dt[:, :NA_DR, :, GRID_W:]
    return jnp.einsum("hdwu,wue->hde", dt, jnp.asarray(c), precision=lax.Precision.HIGHEST)


def _na_variant(blk, n_lat_blk):
    return jnp.where(blk == 0, 0, jnp.where(blk >= n_lat_blk - 1, 2, 1))


def _na_kstart(blk, rows):
    r0 = blk * NA_ROWS
    ks = jnp.clip(r0 - NA_WIN_H // 2, 0, rows - NA_KROWS)
    return pl.multiple_of(ks * GRID_W, GRID_W)


def na_fwd_call(naqkv, bias, n_lat):
    T = naqkv.shape[0]
    nb = T // TM
    n_lat_blk = n_lat // TM
    rows = n_lat // GRID_W
    n_ctx = T - n_lat
    KW = NA_KROWS * GRID_W
    npair = NA_HEADS // 2

    def body(q_ref, k_ref, v_ref, b_ref, y_ref, lse_ref):
        blk = pl.program_id(1)
        half = _lane_half((TM, LANES))
        lo = half == 0
        q = q_ref[...].astype(F32) * SCALE
        kctx = k_ref[pl.ds(n_lat, n_ctx), :]
        vctx = v_ref[pl.ds(n_lat, n_ctx), :]

        @pl.when(blk < n_lat_blk)
        def _():
            off = _na_kstart(blk, rows)
            kwin = k_ref[pl.ds(off, KW), :]
            vwin = v_ref[pl.ds(off, KW), :]
            outs = []
            for hh in range(2):
                qm = jnp.where(half == hh, q, 0.0).astype(BF16)
                sw = _dot_nt(qm, kwin) + b_ref[hh, 0]
                sc = _dot_nt(qm, kctx)
                m = jnp.maximum(jnp.max(sw, axis=-1, keepdims=True), jnp.max(sc, axis=-1, keepdims=True))
                pw = jnp.exp(sw - m)
                pc = jnp.exp(sc - m)
                l = jnp.sum(pw, axis=-1, keepdims=True) + jnp.sum(pc, axis=-1, keepdims=True)
                outs.append((_dot(pw.astype(BF16), vwin) + _dot(pc.astype(BF16), vctx)) / l)
                lse_ref[hh] = jnp.broadcast_to(m + jnp.log(l), (TM, LANES))
            y_ref[...] = jnp.where(lo, outs[0], outs[1]).astype(BF16)

        @pl.when(blk >= n_lat_blk)
        def _():
            outs = []
            for hh in range(2):
                qm = jnp.where(half == hh, q, 0.0).astype(BF16)
                sc = _dot_nt(qm, kctx)
                m = jnp.max(sc, axis=-1, keepdims=True)
                pc = jnp.exp(sc - m)
                l = jnp.sum(pc, axis=-1, keepdims=True)
                outs.append(_dot(pc.astype(BF16), vctx) / l)
                lse_ref[hh] = jnp.broadcast_to(m + jnp.log(l), (TM, LANES))
            y_ref[...] = jnp.where(lo, outs[0], outs[1]).astype(BF16)

    return pl.pallas_call(
        body, name="na_fwd", grid=(npair, nb),
        in_specs=[pl.BlockSpec((TM, LANES), lambda p, b: (b, p)),
                  pl.BlockSpec((T, LANES), lambda p, b: (0, npair + p)),
                  pl.BlockSpec((T, LANES), lambda p, b: (0, 2 * npair + p)),
                  pl.BlockSpec((2, 1, TM, KW), lambda p, b: (p, _na_variant(b, n_lat_blk), 0, 0))],
        out_specs=[pl.BlockSpec((TM, LANES), lambda p, b: (b, p)),
                   pl.BlockSpec((2, TM, LANES), lambda p, b: (p, b, 0))],
        out_shape=[jax.ShapeDtypeStruct((T, NA_HEADS * HEAD_DIM), BF16),
                   jax.ShapeDtypeStruct((NA_HEADS, T, LANES), F32)],
        compiler_params=_cp(("parallel", "parallel"), VMEM_BIG),
    )(naqkv, naqkv, naqkv, bias)


def na_bwd_call(naqkv, bias, ya, dya, lse, n_lat):
    T = naqkv.shape[0]
    nb = T // TM
    n_lat_blk = n_lat // TM
    rows = n_lat // GRID_W
    n_ctx = T - n_lat
    KW = NA_KROWS * GRID_W
    npair = NA_HEADS // 2

    def body(q_ref, k_ref, v_ref, b_ref, y_ref, dy_ref, lse_ref, dq_ref, dk_ref, dv_ref, db_ref, dk_sc, dv_sc):
        blk = pl.program_id(1)
        half = _lane_half((TM, LANES))
        lo = half == 0
        q = q_ref[...].astype(F32) * SCALE
        kctx = k_ref[pl.ds(n_lat, n_ctx), :]
        vctx = v_ref[pl.ds(n_lat, n_ctx), :]
        dyv = dy_ref[...].astype(F32)
        yv = y_ref[...].astype(F32)

        @pl.when(blk == 0)
        def _():
            dk_sc[...] = jnp.zeros((T, LANES), F32)
            dv_sc[...] = jnp.zeros((T, LANES), F32)

        @pl.when(blk < n_lat_blk)
        def _():
            off = _na_kstart(blk, rows)
            kwin = k_ref[pl.ds(off, KW), :]
            vwin = v_ref[pl.ds(off, KW), :]
            first = (blk == 0) | (blk == 1) | (blk == n_lat_blk - 1)
            dqs = []
            for hh in range(2):
                sel = half == hh
                qm = jnp.where(sel, q, 0.0).astype(BF16)
                dom = jnp.where(sel, dyv, 0.0)
                delta = jnp.sum(dom * yv, axis=-1, keepdims=True)
                dom = dom.astype(BF16)
                lse_v = lse_ref[hh][:, 0:1]
                pw = jnp.exp(_dot_nt(qm, kwin) + b_ref[hh, 0] - lse_v)
                pc = jnp.exp(_dot_nt(qm, kctx) - lse_v)
                dsw = pw * (_dot_nt(dom, vwin) - delta)
                dsc = pc * (_dot_nt(dom, vctx) - delta)

                @pl.when(first)
                def _():
                    db_ref[hh, 0] = dsw

                @pl.when(jnp.logical_not(first))
                def _():
                    db_ref[hh, 0] += dsw

                dsw = dsw.astype(BF16)
                dsc = dsc.astype(BF16)
                dv_sc[pl.ds(off, KW), :] += _dot_tn(pw.astype(BF16), dom)
                dv_sc[pl.ds(n_lat, n_ctx), :] += _dot_tn(pc.astype(BF16), dom)
                dk_sc[pl.ds(off, KW), :] += _dot_tn(dsw, qm)
                dk_sc[pl.ds(n_lat, n_ctx), :] += _dot_tn(dsc, qm)
                dqs.append((_dot(dsw, kwin) + _dot(dsc, kctx)) * SCALE)
            dq_ref[...] = jnp.where(lo, dqs[0], dqs[1]).astype(BF16)

        @pl.when(blk >= n_lat_blk)
        def _():
            dqs = []
            for hh in range(2):
                sel = half == hh
                qm = jnp.where(sel, q, 0.0).astype(BF16)
                dom = jnp.where(sel, dyv, 0.0)
                delta = jnp.sum(dom * yv, axis=-1, keepdims=True)
                dom = dom.astype(BF16)
                pc = jnp.exp(_dot_nt(qm, kctx) - lse_ref[hh][:, 0:1])
                dsc = (pc * (_dot_nt(dom, vctx) - delta)).astype(BF16)
                dv_sc[pl.ds(n_lat, n_ctx), :] += _dot_tn(pc.astype(BF16), dom)
                dk_sc[pl.ds(n_lat, n_ctx), :] += _dot_tn(dsc, qm)
                dqs.append(_dot(dsc, kctx) * SCALE)
            dq_ref[...] = jnp.where(lo, dqs[0], dqs[1]).astype(BF16)

        @pl.when(blk == nb - 1)
        def _():
            dk_ref[...] = dk_sc[...].astype(BF16)
            dv_ref[...] = dv_sc[...].astype(BF16)

    blkspec = pl.BlockSpec((TM, LANES), lambda p, b: (b, p))
    col = pl.BlockSpec((T, LANES), lambda p, b: (0, p))
    bspec = pl.BlockSpec((2, 1, TM, KW), lambda p, b: (p, _na_variant(b, n_lat_blk), 0, 0))
    dq, dk, dv, db = pl.pallas_call(
        body, name="na_bwd", grid=(npair, nb),
        in_specs=[blkspec,
                  pl.BlockSpec((T, LANES), lambda p, b: (0, npair + p)),
                  pl.BlockSpec((T, LANES), lambda p, b: (0, 2 * npair + p)),
                  bspec, blkspec, blkspec,
                  pl.BlockSpec((2, TM, LANES), lambda p, b: (p, b, 0))],
        out_specs=[blkspec, col, col, bspec],
        out_shape=[jax.ShapeDtypeStruct((T, NA_HEADS * HEAD_DIM), BF16)] * 3
        + [jax.ShapeDtypeStruct((NA_HEADS, 3, TM, KW), F32)],
        scratch_shapes=[pltpu.VMEM((T, LANES), F32), pltpu.VMEM((T, LANES), F32)],
        compiler_params=_cp(("arbitrary", "arbitrary"), VMEM_BIG),
    )(naqkv, naqkv, naqkv, bias, ya, dya, lse)
    return jnp.concatenate([dq, dk, dv], axis=1), db


def rope_tables(n_lat, n_ctx):
    t = jnp.arange(n_lat)
    row = (t // GRID_W).astype(F32)
    col = (t % GRID_W).astype(F32)
    half = HEAD_DIM // 2
    inv = ROPE_THETA ** (-jnp.arange(0, half, 2, dtype=F32) / half)
    ang = jnp.concatenate([row[:, None] * inv, col[:, None] * inv], axis=-1)
    cos = jnp.repeat(jnp.cos(ang), 2, axis=-1)
    sin = jnp.repeat(jnp.sin(ang), 2, axis=-1) * jnp.tile(jnp.array([-1.0, 1.0], F32), half)
    cos = jnp.concatenate([cos, jnp.ones((n_ctx, HEAD_DIM), F32)], axis=0)
    sin = jnp.concatenate([sin, jnp.zeros((n_ctx, HEAD_DIM), F32)], axis=0)
    return jnp.tile(cos, (1, 2)), jnp.tile(sin, (1, 2))


def _row(v):
    return v.reshape(1, -1)


def device_fwd_bwd(xs, target, modv, w, small, n_lat, ex=None):
    T, D = xs.shape
    n_ctx = T - n_lat
    L = modv.shape[0]
    nlt = n_lat // TM
    rows = n_lat // GRID_W
    cos_t, sin_t = rope_tables(n_lat, n_ctx)
    qg = [jnp.tile(_row(small["q_gain"][l]), (1, GQA_Q_HEADS)) for l in range(L)]
    kg = [jnp.tile(_row(small["k_gain"][l]), (1, GQA_KV_HEADS)) for l in range(L)]
    bias = [na_bias_table(small["na_rpb"][l], rows) for l in range(L)]

    later = {}

    def wt(name, l):
        if ex is None or (name == "w_in" and l == 0):
            return w[name][l]
        return later[name][l - 1] if name == "w_in" else later[name][l]

    saved = []
    x = xs
    h = norm_mod_call(x, _row(small["norm1"][0]), modv[0], nlt, 0)
    for l in range(L):
        naqkv, gqkv, gates = qkv_call(h, wt("w_in", l))
        ya, lse_a = na_fwd_call(naqkv, bias[l], n_lat)
        qp, kp = gqa_prep_call(gqkv, qg[l], kg[l], cos_t, sin_t)
        small_scores = gqa_scores_small(small["q_gain"][l], small["k_gain"][l])
        if ex is not None and l == 0:
            yb, lse_b, *gathered = gqa_fwd_call(qp, kp, gqkv, n_lat, small_scores, gather=ex.rest())
            w_in_later, others = ex.rest_whole(gathered)
            later.update(w_in_later)
            later.update(others)
        else:
            yb, lse_b = gqa_fwd_call(qp, kp, gqkv, n_lat, small_scores)
        merged, pa, pb = merge_call(ya, yb, gates, wt("w_pa", l), wt("w_pb", l))
        x1, osave, h2 = wo_call(merged, wt("w_o", l), x, modv[l], _row(small["norm2"][l]), nlt)
        act, a, u = ffn_in_call(h2, wt("w_ffn_in", l))
        if l + 1 < L:
            x2, fsave, hn = ffn_out_call(act, wt("w_ffn_out", l), x1, modv[l], nlt,
                                         _row(small["norm1"][l + 1]), modv[l + 1])
        else:
            x2, fsave = ffn_out_call(act, wt("w_ffn_out", l), x1, modv[l], nlt)
            hn = None
        saved.append(dict(x=x, h=h, naqkv=naqkv, gqkv=gqkv, gates=gates, ya=ya, lse_a=lse_a, qp=qp, kp=kp,
                          yb=yb, lse_b=lse_b, merged=merged, pa=pa, pb=pb, x1=x1, osave=osave, h2=h2,
                          act=act, a=a, u=u, fsave=fsave))
        x, h = x2, hn

    dx, d_final, loss_part = loss_call(x, _row(small["final_norm"]), target, nlt)

    gw = {k: [None] * L for k in ("w_in", "w_pa", "w_pb", "w_o", "w_ffn_in", "w_ffn_out")}
    gs = {k: [None] * L for k in ("norm1", "norm2", "na_rpb", "q_gain", "k_gain")}
    dmod = [None] * L
    fold = lambda s: s.sum(axis=1)
    for l in reversed(range(L)):
        s = saved[l]
        da, du, df, s_g2 = dact_call(dx, modv[l], s["fsave"], wt("w_ffn_out", l), s["a"], s["u"], nlt)
        gw["w_ffn_out"][l] = wgrad_call(s["act"], df, "wgrad_ffn_out")
        dx1, s_sh2, s_sc2, s_n2 = dh2_call(da, du, wt("w_ffn_in", l), s["x1"], modv[l], _row(small["norm2"][l]), dx, nlt)
        gw["w_ffn_in"][l] = jnp.concatenate(
            [wgrad_call(s["h2"], da, "wgrad_ffn_in_a"), wgrad_call(s["h2"], du, "wgrad_ffn_in_u")], axis=1)
        dpa, dpb, dgates, do, s_g1 = dmerged_call(dx1, modv[l], s["osave"], wt("w_o", l), s["pa"], s["pb"], s["gates"], nlt)
        gw["w_o"][l] = wgrad_call(s["merged"], do, "wgrad_o")
        dya, dyb = dy_call(dpa, dpb, wt("w_pa", l), wt("w_pb", l))
        gw["w_pa"][l] = wgrad_call(s["ya"], dpa, "wgrad_pa")
        gw["w_pb"][l] = wgrad_call(s["yb"], dpb, "wgrad_pb")
        dna, dbias = na_bwd_call(s["naqkv"], bias[l], s["ya"], dya, s["lse_a"], n_lat)
        gs["na_rpb"][l] = na_bias_grad(dbias, rows)
        if ex is not None:
            ex.pack_grads(l, {k: v[l] for k, v in gw.items() if k != "w_in"}, [gi for gi in range(len(GROUPS)) if gi != G_IN])
            keys = [(l, gi) for gi in range(len(GROUPS)) if gi != G_IN] + ([(l + 1, G_IN)] if l + 1 < L else [])
            dqp, dkp, dv, *got = gqa_bwd_call(s["qp"], s["kp"], s["gqkv"], s["yb"], dyb, s["lse_b"], n_lat,
                                              exchange=[ex.partials[k] for k in keys])
            ex.received.update(zip(keys, got))
        else:
            dqp, dkp, dv = gqa_bwd_call(s["qp"], s["kp"], s["gqkv"], s["yb"], dyb, s["lse_b"], n_lat)
        dg, s_qg, s_kg = gqa_prep_bwd_call(s["gqkv"], qg[l], kg[l], cos_t, sin_t, dqp, dkp, dv)
        gs["q_gain"][l] = s_qg.reshape(8 * GQA_Q_HEADS, HEAD_DIM).sum(axis=0)
        gs["k_gain"][l] = s_kg.reshape(8 * GQA_KV_HEADS, HEAD_DIM).sum(axis=0)
        dx, s_sh1, s_sc1, s_n1 = dh_call(dna, dg, dgates, wt("w_in", l), s["x"], modv[l], _row(small["norm1"][l]), dx1, nlt)
        gw["w_in"][l] = jnp.concatenate(
            [wgrad_call(s["h"], dna, "wgrad_in_na"), wgrad_call(s["h"], dg, "wgrad_in_g"),
             wgrad_call(s["h"], dgates, "wgrad_in_gate")], axis=1)
        gs["norm1"][l] = fold(s_n1).sum(axis=0)
        gs["norm2"][l] = fold(s_n2).sum(axis=0)
        dmod[l] = jnp.stack([fold(s_sh1), fold(s_sc1), fold(s_g1), fold(s_sh2), fold(s_sc2), fold(s_g2)], axis=1)
        if ex is not None:
            ex.pack_grads(l, {"w_in": gw["w_in"][l]}, [G_IN])

    if ex is not None:
        ex.received[(0, G_IN)] = grads_exchange([ex.partials[(0, G_IN)]])[0]
        gw = None
    else:
        gw = {k: jnp.stack(v) for k, v in gw.items()}
    gs = {k: jnp.stack(v) for k, v in gs.items()}
    gs["final_norm"] = d_final.sum(axis=(0, 1))
    return loss_part, dx, gw, jnp.stack(dmod), gs


N_DEV = 8
N_CHIP = 4
ANY = pl.BlockSpec(memory_space=pl.ANY)


def _place():
    x, y, c = lax.axis_index("x"), lax.axis_index("y"), lax.axis_index("c")
    chips = [(1 - x, y), (x, 1 - y), (1 - x, 1 - y)]
    return x, y, c, chips


def small_all_gather(v, name):
    m_per, n = v.shape
    assert m_per % 8 == 0

    def body(x_ref, out_ref, send_sems, recv_sems, local_sem):
        x, y, c, chips = _place()
        me, sibling = (x, y, c), (x, y, 1 - c)

        def rows(px, py, pc):
            return out_ref.at[pl.ds(pl.multiple_of((4 * px + 2 * py + pc) * m_per, 8), m_per), :]

        def copy(k, block, to, src=None):
            return pltpu.make_async_remote_copy(
                src_ref=rows(*block) if src is None else src, dst_ref=rows(*block),
                send_sem=send_sems.at[k], recv_sem=recv_sems.at[k], device_id=to, device_id_type=MESH)

        mine = pltpu.make_async_copy(x_ref, rows(*me), local_sem)
        mine.start()
        first = [copy(0, me, sibling, src=x_ref)]
        first += [copy(1 + j, me, (*chip, c), src=x_ref) for j, chip in enumerate(chips)]
        for cp in first:
            cp.start()
        passed = [copy(4 + j, (*chip, c), sibling) for j, chip in enumerate(chips)]
        for j, chip in enumerate(chips):
            copy(1 + j, (*chip, c), me).wait_recv()
            passed[j].start()
        copy(0, sibling, me).wait_recv()
        for j, chip in enumerate(chips):
            copy(4 + j, (*chip, 1 - c), me).wait_recv()
        for cp in first + passed:
            cp.wait_send()
        mine.wait()

    out = pl.pallas_call(
        body, name=name,
        out_shape=jax.ShapeDtypeStruct((N_DEV * m_per, n), v.dtype),
        in_specs=[pl.BlockSpec(memory_space=pltpu.VMEM)],
        out_specs=pl.BlockSpec(memory_space=pltpu.VMEM),
        scratch_shapes=[pltpu.SemaphoreType.DMA((7,)), pltpu.SemaphoreType.DMA((7,)), pltpu.SemaphoreType.DMA],
    )(v)
    return out.reshape(N_DEV, m_per, n)


def _ag_copies(in_ref, out_ref, send_sems, recv_sems, base=0):
    x, y, c, chips = _place()
    me, sibling = (x, y, c), (x, y, 1 - c)

    def blk(px, py, half):
        return out_ref.at[2 * px + py, half]

    def copy(k, block, to, src=None):
        return pltpu.make_async_remote_copy(
            src_ref=blk(*block) if src is None else src, dst_ref=blk(*block),
            send_sem=send_sems.at[base + k], recv_sem=recv_sems.at[base + k], device_id=to, device_id_type=MESH)

    first = [copy(j, (x, y, c), (*chip, c), src=in_ref.at[c]) for j, chip in enumerate(chips)]
    passed = [copy(3 + j, (*chip, c), sibling) for j, chip in enumerate(chips)]
    arrivals = [copy(j, (*chip, c), me) for j, chip in enumerate(chips)]
    forwarded = [copy(3 + j, (*chip, 1 - c), me) for j, chip in enumerate(chips)]
    return first, passed, arrivals, forwarded


AG_SEMS = 6
RS_SEMS = 7


def _ag_start(in_refs, out_refs, send_sems, recv_sems):
    for b, (i, o) in enumerate(zip(in_refs, out_refs)):
        for cp in _ag_copies(i, o, send_sems, recv_sems, AG_SEMS * b)[0]:
            cp.start()


def _ag_finish(in_refs, out_refs, send_sems, recv_sems):
    copies = [_ag_copies(i, o, send_sems, recv_sems, AG_SEMS * b) for b, (i, o) in enumerate(zip(in_refs, out_refs))]
    for first, passed, arrivals, forwarded in copies:
        for arrived, onward in zip(arrivals, passed):
            arrived.wait_recv()
            onward.start()
    for first, passed, arrivals, forwarded in copies:
        for cp in forwarded:
            cp.wait_recv()
        for cp in first + passed:
            cp.wait_send()


def place_own_block(gathered, packed):
    chip = 2 * lax.axis_index("x") + lax.axis_index("y")
    return lax.dynamic_update_slice(gathered, packed[None], (chip,) + (0,) * packed.ndim)


def weights_all_gather(packed):
    n = len(packed)

    def body(*refs):
        _ag_start(refs[:n], refs[n:2 * n], refs[2 * n], refs[2 * n + 1])
        _ag_finish(refs[:n], refs[n:2 * n], refs[2 * n], refs[2 * n + 1])

    outs = pl.pallas_call(
        body, name="weights_all_gather",
        out_shape=[jax.ShapeDtypeStruct((N_CHIP,) + p.shape, p.dtype) for p in packed],
        in_specs=[ANY] * n, out_specs=[ANY] * n,
        scratch_shapes=[pltpu.SemaphoreType.DMA((AG_SEMS * n,)), pltpu.SemaphoreType.DMA((AG_SEMS * n,))],
    )(*packed)
    return [place_own_block(o, p) for o, p in zip(outs, packed)]


def _rs_copies(gp_ref, rx_ref, send_sems, recv_sems, base=0):
    x, y, c, chips = _place()
    sends = [pltpu.make_async_remote_copy(src_ref=gp_ref.at[2 * x + y, 1 - c], dst_ref=rx_ref.at[0],
                                          send_sem=send_sems.at[base], recv_sem=recv_sems.at[base],
                                          device_id=(x, y, 1 - c), device_id_type=MESH)]
    for r, (px, py) in enumerate(chips):
        for h in range(2):
            sends.append(pltpu.make_async_remote_copy(
                src_ref=gp_ref.at[2 * px + py, h], dst_ref=rx_ref.at[1 + 2 * r + c],
                send_sem=send_sems.at[base + 1 + 2 * r + h], recv_sem=recv_sems.at[base + 1 + 2 * r + c],
                device_id=(px, py, h), device_id_type=MESH))
    arrivals = [pltpu.make_async_remote_copy(src_ref=gp_ref.at[0, 0], dst_ref=rx_ref.at[s],
                                             send_sem=send_sems.at[base + s], recv_sem=recv_sems.at[base + s],
                                             device_id=(x, y, c), device_id_type=MESH) for s in range(RS_SEMS)]
    return sends, arrivals


def _rs_start(gp_refs, rx_refs, send_sems, recv_sems):
    for b, (gp, rx) in enumerate(zip(gp_refs, rx_refs)):
        for cp in _rs_copies(gp, rx, send_sems, recv_sems, RS_SEMS * b)[0]:
            cp.start()


def _rs_finish(gp_refs, rx_refs, send_sems, recv_sems):
    for b, (gp, rx) in enumerate(zip(gp_refs, rx_refs)):
        sends, arrivals = _rs_copies(gp, rx, send_sems, recv_sems, RS_SEMS * b)
        for cp in arrivals:
            cp.wait_recv()
        for cp in sends:
            cp.wait_send()


def _rs_out_shape(gp):
    return jax.ShapeDtypeStruct((RS_SEMS,) + gp.shape[2:], gp.dtype)


def grads_exchange(gps):
    n = len(gps)

    def body(*refs):
        _rs_start(refs[:n], refs[n:2 * n], refs[2 * n], refs[2 * n + 1])
        _rs_finish(refs[:n], refs[n:2 * n], refs[2 * n], refs[2 * n + 1])

    return pl.pallas_call(
        body, name="grads_exchange", out_shape=[_rs_out_shape(g) for g in gps],
        in_specs=[ANY] * n, out_specs=[ANY] * n,
        scratch_shapes=[pltpu.SemaphoreType.DMA((RS_SEMS * n,)), pltpu.SemaphoreType.DMA((RS_SEMS * n,))],
    )(*gps)


def grads_join_halves(ghs):
    n = len(ghs)

    def body(*refs):
        x, y, c, _ = _place()
        cps = [pltpu.make_async_remote_copy(src_ref=refs[b], dst_ref=refs[n + b], send_sem=refs[2 * n].at[b],
                                            recv_sem=refs[2 * n + 1].at[b], device_id=(x, y, 1 - c),
                                            device_id_type=MESH) for b in range(n)]
        for cp in cps:
            cp.start()
        for cp in cps:
            cp.wait()

    others = pl.pallas_call(
        body, name="grads_join_halves", out_shape=[jax.ShapeDtypeStruct(g.shape, g.dtype) for g in ghs],
        in_specs=[ANY] * n, out_specs=[ANY] * n,
        scratch_shapes=[pltpu.SemaphoreType.DMA((n,)), pltpu.SemaphoreType.DMA((n,))],
    )(*ghs)
    c0 = lax.axis_index("c") == 0
    return [jnp.stack([jnp.where(c0, g, o), jnp.where(c0, o, g)], axis=1) for g, o in zip(ghs, others)]


PACK_C = 1024


def sum_partials_call(gp, rx, chip, core):
    _, _, rh, ncol = gp.shape
    n = rx.shape[0]
    tr = _pick_tile(rh, 512, 16)

    def body(where_ref, a_ref, *rest):
        o_ref = rest[-1]
        g = a_ref[0, 0].astype(F32)
        for b_ref in rest[:-1]:
            g = g + b_ref[0].astype(F32)
        o_ref[...] = g

    other = lambda j: pl.BlockSpec((1, tr, ncol), lambda i, w: (j, i, 0))
    return pl.pallas_call(
        body, name="sum_partials",
        grid_spec=pltpu.PrefetchScalarGridSpec(
            num_scalar_prefetch=1, grid=(rh // tr,),
            in_specs=[pl.BlockSpec((1, 1, tr, ncol), lambda i, w: (w[0], w[1], i, 0))] + [other(j) for j in range(n)],
            out_specs=pl.BlockSpec((tr, ncol), lambda i, w: (i, 0))),
        out_shape=jax.ShapeDtypeStruct((rh, ncol), F32),
        compiler_params=_cp(("parallel",)),
    )(jnp.stack([chip, core]).astype(jnp.int32), gp, *([rx] * n))


def _silu(x):
    return x * _sigmoid(x)


def mod_matmul_call(craw, w_mod):
    L, D, N = w_mod.shape

    def body(c_ref, w_ref, o_ref):
        o_ref[0] = _dot(_silu(c_ref[...]).astype(BF16), w_ref[0].astype(BF16))

    return pl.pallas_call(
        body, name="mod_matmul", grid=(L,),
        in_specs=[pl.BlockSpec((16, D), lambda l: (0, 0)), pl.BlockSpec((1, D, N), lambda l: (l, 0, 0))],
        out_specs=pl.BlockSpec((1, 16, N), lambda l: (l, 0, 0)),
        out_shape=jax.ShapeDtypeStruct((L, 16, N), F32),
        compiler_params=_cp(("parallel",), VMEM_BIG),
    )(craw, w_mod)


def mod_bwd_call(craw, dmod16, w_mod):
    L, D, N = w_mod.shape

    def body(c_ref, d_ref, w_ref, gw_ref, ds_ref):
        l = pl.program_id(0)
        d = d_ref[0].astype(BF16)
        gw_ref[0] = _dot_tn(_silu(c_ref[...]).astype(BF16), d)
        part = _dot_nt(d, w_ref[0].astype(BF16))

        @pl.when(l == 0)
        def _():
            ds_ref[...] = part

        @pl.when(l > 0)
        def _():
            ds_ref[...] += part

    return pl.pallas_call(
        body, name="mod_bwd", grid=(L,),
        in_specs=[pl.BlockSpec((16, D), lambda l: (0, 0)), pl.BlockSpec((1, 16, N), lambda l: (l, 0, 0)),
                  pl.BlockSpec((1, D, N), lambda l: (l, 0, 0))],
        out_specs=[pl.BlockSpec((1, D, N), lambda l: (l, 0, 0)), pl.BlockSpec((16, D), lambda l: (0, 0))],
        out_shape=[jax.ShapeDtypeStruct((L, D, N), F32), jax.ShapeDtypeStruct((16, D), F32)],
        compiler_params=_cp(("arbitrary",), VMEM_BIG),
    )(craw, dmod16, w_mod)


def _adamw(w, g, m, v):
    m2 = ADAM_B1 * m + (1.0 - ADAM_B1) * g
    v2 = ADAM_B2 * v + (1.0 - ADAM_B2) * (g * g)
    m_hat = m2 / (1.0 - ADAM_B1 ** ADAM_STEP)
    v_hat = v2 / (1.0 - ADAM_B2 ** ADAM_STEP)
    delta = -ADAM_LR * (m_hat / (jnp.sqrt(v_hat) + ADAM_EPS) + ADAM_WD * w)
    return delta, m2, v2


def adamw_call(w, g, m, v):
    shape = w.shape
    ncol = shape[-1]
    r = math.prod(shape[:-1])
    tr = _pick_tile(r, 512, 8)
    as2d = lambda t: t.reshape(r, ncol)

    def body(w_ref, g_ref, m_ref, v_ref, d_ref, m2_ref, v2_ref):
        d_ref[...], m2_ref[...], v2_ref[...] = _adamw(w_ref[...], g_ref[...], m_ref[...], v_ref[...])

    spec = pl.BlockSpec((tr, ncol), lambda i: (i, 0))
    outs = pl.pallas_call(
        body, name="adamw", grid=(r // tr,), in_specs=[spec] * 4, out_specs=[spec] * 3,
        out_shape=[jax.ShapeDtypeStruct((r, ncol), F32)] * 3,
        compiler_params=_cp(("parallel",)),
    )(as2d(w), as2d(g), as2d(m), as2d(v))
    return [o.reshape(shape) for o in outs]


def adamw_small_call(parts, w, m, v, silu_bwd=False):
    P, R, C = parts.shape

    def body(p_ref, w_ref, m_ref, v_ref, g_ref, d_ref, m2_ref, v2_ref):
        g = p_ref[0]
        for i in range(1, P):
            g = g + p_ref[i]
        wv = w_ref[...]
        if silu_bwd:
            sg = _sigmoid(wv)
            g = g * (sg * (1.0 + wv * (1.0 - sg)))
        g_ref[...] = g
        d_ref[...], m2_ref[...], v2_ref[...] = _adamw(wv, g, m_ref[...], v_ref[...])

    return pl.pallas_call(
        body, name="adamw_small", out_shape=[jax.ShapeDtypeStruct((R, C), F32)] * 4,
    )(parts, w, m, v)


def sum_parts_call(parts):
    P, R, C = parts.shape

    def body(p_ref, o_ref):
        g = p_ref[0]
        for i in range(1, P):
            g = g + p_ref[i]
        o_ref[...] = g

    return pl.pallas_call(body, name="sum_parts", out_shape=jax.ShapeDtypeStruct((R, C), F32))(parts)


BIG = (("w_in", "col"), ("w_pa", "col"), ("w_pb", "col"), ("w_o", "row"), ("w_ffn_in", "col"), ("w_ffn_out", "row"))
KIND = dict(BIG)
GROUPS = (("w_in",), ("w_ffn_in",), ("w_pa", "w_pb", "w_o", "w_ffn_out"))
G_IN = 0
SMALL = ("norm1", "norm2", "final_norm", "na_rpb", "q_gain", "k_gain")


def _group_width(group, shapes):
    return shapes[group[0]][1] if len(group) == 1 else PACK_C


def _pack_shards(shards, group, shapes):
    width = _group_width(group, shapes)
    lead = shards[group[0]].shape[:-2]
    p = jnp.concatenate([shards[n].reshape(lead + (-1, width)) for n in group], axis=-2)
    return p.reshape(lead + (2, p.shape[-2] // 2, width))


def _unpack_shards(packed, group, shapes):
    width = _group_width(group, shapes)
    lead = packed.shape[:-3]
    flat = packed.reshape(lead + (-1, width))
    out, r0 = {}, 0
    for n in group:
        K, N = shapes[n]
        rw = K * N // width
        out[n] = flat[..., r0:r0 + rw, :].reshape(lead + (K, N))
        r0 += rw
    return out


def _whole_from_chips(g, kind):
    _, L, K, N = g.shape
    if kind == "col":
        return g.transpose(1, 2, 0, 3).reshape(L, K, N_CHIP * N)
    return g.transpose(1, 0, 2, 3).reshape(L, N_CHIP * K, N)


def _chips_from_whole(g, kind):
    K, N = g.shape
    if kind == "col":
        return g.reshape(K, N_CHIP, N // N_CHIP).transpose(1, 0, 2)
    return g.reshape(N_CHIP, K // N_CHIP, N)


class Exchange:
    def __init__(self, local_bf16, shard_shapes):
        self.shapes = shard_shapes
        self.packed = [_pack_shards(local_bf16, g, shard_shapes).transpose(1, 0, 2, 3) for g in GROUPS]
        self.received = {}
        self.partials = {}

    def whole(self, gathered, group):
        g = _unpack_shards(gathered.transpose(0, 2, 1, 3, 4), group, self.shapes)
        return {n: _whole_from_chips(g[n], KIND[n]) for n in group}

    def first_needed(self):
        return self.whole(weights_all_gather([self.packed[G_IN][:, :1]])[0], GROUPS[G_IN])

    def rest(self):
        return [p[:, 1:] if gi == G_IN else p for gi, p in enumerate(self.packed)]

    def rest_whole(self, gathered):
        out = [self.whole(place_own_block(g, p), grp) for g, p, grp in zip(gathered, self.rest(), GROUPS)]
        others = {}
        for gi, d in enumerate(out):
            if gi != G_IN:
                others.update(d)
        return out[G_IN], others

    def pack_grads(self, layer, gw, groups):
        for gi in groups:
            shards = {n: _chips_from_whole(gw[n], KIND[n]) for n in GROUPS[gi]}
            self.partials[(layer, gi)] = _pack_shards(shards, GROUPS[gi], self.shapes)


def _pad_rows(v, rows):
    return jnp.pad(v.reshape(-1), (0, rows * PACK_C - v.size)).reshape(rows, PACK_C)


def kernel(x, c, ctx, c_ctx, w_mod, b_mod, norm1, w_in, na_rpb, q_gain, k_gain, w_pa, w_pb, w_o, norm2, w_ffn_in, w_ffn_out, final_norm, loss_target, m_c_ctx, m_w_mod, m_b_mod, m_norm1, m_w_in, m_na_rpb, m_q_gain, m_k_gain, m_w_pa, m_w_pb, m_w_o, m_norm2, m_w_ffn_in, m_w_ffn_out, m_final_norm, v_c_ctx, v_w_mod, v_b_mod, v_norm1, v_w_in, v_na_rpb, v_q_gain, v_k_gain, v_w_pa, v_w_pb, v_w_o, v_norm2, v_w_ffn_in, v_w_ffn_out, v_final_norm):
    W = dict(c_ctx=c_ctx, w_mod=w_mod, b_mod=b_mod, norm1=norm1, w_in=w_in, na_rpb=na_rpb, q_gain=q_gain, k_gain=k_gain,
             w_pa=w_pa, w_pb=w_pb, w_o=w_o, norm2=norm2, w_ffn_in=w_ffn_in, w_ffn_out=w_ffn_out, final_norm=final_norm)
    M = dict(c_ctx=m_c_ctx, w_mod=m_w_mod, b_mod=m_b_mod, norm1=m_norm1, w_in=m_w_in, na_rpb=m_na_rpb, q_gain=m_q_gain,
             k_gain=m_k_gain, w_pa=m_w_pa, w_pb=m_w_pb, w_o=m_w_o, norm2=m_norm2, w_ffn_in=m_w_ffn_in,
             w_ffn_out=m_w_ffn_out, final_norm=m_final_norm)
    V = dict(c_ctx=v_c_ctx, w_mod=v_w_mod, b_mod=v_b_mod, norm1=v_norm1, w_in=v_w_in, na_rpb=v_na_rpb, q_gain=v_q_gain,
             k_gain=v_k_gain, w_pa=v_w_pa, w_pb=v_w_pb, w_o=v_w_o, norm2=v_norm2, w_ffn_in=v_w_ffn_in,
             w_ffn_out=v_w_ffn_out, final_norm=v_final_norm)
    order = ["c_ctx", "w_mod", "b_mod", "norm1", "w_in", "na_rpb", "q_gain", "k_gain", "w_pa", "w_pb", "w_o", "norm2",
             "w_ffn_in", "w_ffn_out", "final_norm"]
    L, D = norm1.shape
    n_lat = x.shape[1]
    ax, ay, ac = lax.axis_index("x"), lax.axis_index("y"), lax.axis_index("c")
    chip = 2 * ax + ay
    dev = 2 * chip + ac

    shard_shapes = {n: W[n].shape[1:] for n, _ in BIG}
    ex = Exchange({n: W[n].astype(BF16) for n, _ in BIG}, shard_shapes)
    w_first = ex.first_needed()

    c_all = small_all_gather(jnp.pad(c, ((0, 7), (0, 0))), "gather_c")[:, 0, :]
    craw = jnp.pad(c_all, ((0, 8), (0, 0))) + jnp.pad(c_ctx[None, :], ((8, 7), (0, 0)))
    ncol_mod = w_mod.shape[2]
    mod_loc = mod_matmul_call(craw, w_mod)
    mod_all = small_all_gather(mod_loc.reshape(L * 16, ncol_mod), "gather_mod")
    mod_all = mod_all[0::2].reshape(N_CHIP, L, 16, ncol_mod).transpose(1, 2, 0, 3).reshape(L, 16, 6 * D)
    mod_all = mod_all + b_mod[:, None, :]
    mod_mine = lax.dynamic_index_in_dim(mod_all, dev, axis=1, keepdims=False)
    modv = jnp.stack([mod_mine, mod_all[:, 8]], axis=1).reshape(L, 2, 6, D)

    xs = jnp.concatenate([x[0], ctx[0]], axis=0)
    small = dict(norm1=norm1, norm2=norm2, na_rpb=na_rpb, q_gain=q_gain, k_gain=k_gain, final_norm=final_norm)
    loss_part, dx, _, dmod, gs = device_fwd_bwd(xs, loss_target[0], modv, w_first, small, n_lat, ex)
    loss = lax.psum(jnp.sum(loss_part), ("x", "y", "c"))
    grad_x = dx[:n_lat][None]

    gh = [jnp.stack([sum_partials_call(ex.partials[(l, gi)], ex.received[(l, gi)], chip, ac) for l in range(L)])
          for gi in range(len(GROUPS))]
    gbig = {}
    for grp, joined in zip(GROUPS, grads_join_halves(gh)):
        gbig.update(_unpack_shards(joined, grp, shard_shapes))

    n_mod_rows = L * 6 * D // PACK_C
    small_flat = jnp.concatenate([gs[n].reshape(-1) for n in SMALL])
    n_small_rows = -(-small_flat.size // (8 * PACK_C)) * 8
    g1 = jnp.concatenate([dmod[:, 0].reshape(n_mod_rows, PACK_C), dmod[:, 1].reshape(n_mod_rows, PACK_C),
                          _pad_rows(small_flat, n_small_rows)], axis=0)
    g1 = small_all_gather(g1, "gather_small_grads")
    dm_lat = g1[:, :n_mod_rows].reshape(N_DEV, L, 6 * D)
    dm_ctx_parts = g1[:, n_mod_rows:2 * n_mod_rows]
    dm_ctx = sum_parts_call(dm_ctx_parts).reshape(L, 1, 6 * D)
    dmod16 = (jnp.pad(dm_lat.transpose(1, 0, 2), ((0, 0), (0, 8), (0, 0)))
              + jnp.pad(dm_ctx, ((0, 0), (8, 7), (0, 0))))
    dmod16_loc = lax.dynamic_slice_in_dim(dmod16, chip * ncol_mod, ncol_mod, axis=2)
    g_wmod, dsilu_part = mod_bwd_call(craw, dmod16_loc, w_mod)
    ds_all = small_all_gather(dsilu_part[8:16], "gather_dsilu")
    ds_parts = ds_all[0::2, 0:1, :]
    ds_parts = jnp.pad(ds_parts, ((0, 0), (0, 7), (0, 0)))

    out_g, out_d, out_m, out_v = {}, {}, {}, {}
    for n, _ in BIG:
        out_g[n] = gbig[n]
        out_d[n], out_m[n], out_v[n] = adamw_call(W[n], gbig[n], M[n], V[n])
    out_g["w_mod"] = g_wmod
    out_d["w_mod"], out_m["w_mod"], out_v["w_mod"] = adamw_call(w_mod, g_wmod, m_w_mod, v_w_mod)

    pack_small = lambda t: _pad_rows(jnp.concatenate([t[n].reshape(-1) for n in SMALL]), n_small_rows)
    res = adamw_small_call(g1[:, 2 * n_mod_rows:], pack_small(W), pack_small(M), pack_small(V))
    off = 0
    for n in SMALL:
        sz = W[n].size
        for dst, r in zip((out_g, out_d, out_m, out_v), res):
            dst[n] = r.reshape(-1)[off:off + sz].reshape(W[n].shape)
        off += sz
    bparts = jnp.concatenate([g1[:, :n_mod_rows], dm_ctx_parts], axis=0)
    as_rows = lambda t: t.reshape(n_mod_rows, PACK_C)
    res = adamw_small_call(bparts, as_rows(b_mod), as_rows(m_b_mod), as_rows(v_b_mod))
    for dst, r in zip((out_g, out_d, out_m, out_v), res):
        dst["b_mod"] = r.reshape(b_mod.shape)
    row8 = lambda t: jnp.pad(t[None, :], ((0, 7), (0, 0)))
    res = adamw_small_call(ds_parts, row8(c_ctx), row8(m_c_ctx), row8(v_c_ctx), silu_bwd=True)
    for dst, r in zip((out_g, out_d, out_m, out_v), res):
        dst["c_ctx"] = r[0]

    return (loss, grad_x, *[out_g[n] for n in order], *[out_d[n] for n in order],
            *[out_m[n] for n in order], *[out_v[n] for n in order])
```

```python
import math

import numpy as np
import jax
import jax.numpy as jnp
from jax import lax
from jax.experimental import pallas as pl
from jax.experimental.pallas import tpu as pltpu

F32 = jnp.float32
BF16 = jnp.bfloat16
MESH = pl.DeviceIdType.MESH

HEAD_DIM = 64
NA_HEADS = 8
GQA_Q_HEADS = 8
GQA_KV_HEADS = 2
GRID_W = 64
NA_WIN_H = 8
NA_WIN_W = 16
ROPE_THETA = 10000.0
EPS = 1e-6
SCALE = HEAD_DIM ** -0.5
NEG = -1e30

ADAM_LR = 0.001
ADAM_B1 = 0.9
ADAM_B2 = 0.999
ADAM_EPS = 1e-08
ADAM_WD = 0.01
ADAM_STEP = 10

TM = 256
NA_ROWS = 4
NA_KROWS = 12
LANES = 128
VMEM_BIG = 56 * 1024 * 1024


def _cp(sem, vmem=None):
    return pltpu.CompilerParams(dimension_semantics=sem, vmem_limit_bytes=vmem)


def _dot(a, b):
    return jnp.dot(a, b, preferred_element_type=F32)


def _dot_nt(a, b):
    return lax.dot_general(a, b, (((1,), (1,)), ((), ())), preferred_element_type=F32)


def _dot_tn(a, b):
    return lax.dot_general(a, b, (((0,), (0,)), ((), ())), preferred_element_type=F32)


def _colsum8(v):
    tm, d = v.shape
    return v.reshape(tm // 8, 8, d).sum(axis=0)


def _acc(ref, val, first):
    @pl.when(first)
    def _():
        ref[0] = val

    @pl.when(jnp.logical_not(first))
    def _():
        ref[0] += val


def _rms_r(x):
    return lax.rsqrt(jnp.mean(x * x, axis=-1, keepdims=True) + EPS)


def _norm_mod(x, nw, sh, sc):
    return (x * _rms_r(x) * nw) * (1.0 + sc) + sh


def _norm_mod_bwd(dh, x, nw, sc):
    r = _rms_r(x)
    xn = x * r
    dxn = dh * (nw * (1.0 + sc))
    dx = r * (dxn - xn * jnp.mean(dxn * xn, axis=-1, keepdims=True))
    return dx, dh, dh * (xn * nw), dh * ((1.0 + sc) * xn)


def _sigmoid(x):
    return 1.0 / (1.0 + jnp.exp(-x))


def _group_of(i, n_lat_tiles):
    return jnp.where(i >= n_lat_tiles, 1, 0)


def _lane_half(shape):
    return (lax.broadcasted_iota(jnp.int32, shape, len(shape) - 1) % LANES) // HEAD_DIM


def _lane_lt64(shape):
    return _lane_half(shape) == 0


def _roll64(x):
    return pltpu.roll(x, HEAD_DIM, x.ndim - 1)


def norm_mod_call(x, nw, modv, n_lat_tiles, which):
    T, D = x.shape

    def body(x_ref, nw_ref, mod_ref, h_ref):
        sh = mod_ref[0, 3 * which:3 * which + 1, :]
        sc = mod_ref[0, 3 * which + 1:3 * which + 2, :]
        h_ref[...] = _norm_mod(x_ref[...], nw_ref[...], sh, sc).astype(BF16)

    return pl.pallas_call(
        body, name="norm_mod", grid=(T // TM,),
        in_specs=[pl.BlockSpec((TM, D), lambda i: (i, 0)),
                  pl.BlockSpec((1, D), lambda i: (0, 0)),
                  pl.BlockSpec((1, 6, D), lambda i: (_group_of(i, n_lat_tiles), 0, 0))],
        out_specs=pl.BlockSpec((TM, D), lambda i: (i, 0)),
        out_shape=jax.ShapeDtypeStruct((T, D), BF16),
        compiler_params=_cp(("parallel",)),
    )(x, nw, modv)


def qkv_call(h, w_in):
    T, D = h.shape
    N = w_in.shape[1]
    n_na = 3 * NA_HEADS * HEAD_DIM
    n_g = (GQA_Q_HEADS + 2 * GQA_KV_HEADS) * HEAD_DIM
    n_gate = N - n_na - n_g

    def body(h_ref, w_ref, na_ref, g_ref, gate_ref):
        acc = _dot(h_ref[...], w_ref[...])
        na_ref[...] = acc[:, :n_na].astype(BF16)
        g_ref[...] = acc[:, n_na:n_na + n_g].astype(BF16)
        gate_ref[...] = acc[:, n_na + n_g:].astype(BF16)

    return pl.pallas_call(
        body, name="qkv", grid=(T // TM,),
        in_specs=[pl.BlockSpec((TM, D), lambda i: (i, 0)),
                  pl.BlockSpec((D, N), lambda i: (0, 0))],
        out_specs=[pl.BlockSpec((TM, n_na), lambda i: (i, 0)),
                   pl.BlockSpec((TM, n_g), lambda i: (i, 0)),
                   pl.BlockSpec((TM, n_gate), lambda i: (i, 0))],
        out_shape=[jax.ShapeDtypeStruct((T, n_na), BF16),
                   jax.ShapeDtypeStruct((T, n_g), BF16),
                   jax.ShapeDtypeStruct((T, n_gate), BF16)],
        compiler_params=_cp(("parallel",), VMEM_BIG),
    )(h, w_in)


def merge_call(ya, yb, gates, w_pa, w_pb):
    T, Ka = ya.shape
    D = w_pa.shape[1]

    def body(ya_ref, yb_ref, ga_ref, gb_ref, wa_ref, wb_ref, m_ref, pa_ref, pb_ref):
        pa = _dot(ya_ref[...], wa_ref[...])
        pb = _dot(yb_ref[...], wb_ref[...])
        m = _sigmoid(ga_ref[...].astype(F32)) * pa + _sigmoid(gb_ref[...].astype(F32)) * pb
        m_ref[...] = m.astype(BF16)
        pa_ref[...] = pa.astype(BF16)
        pb_ref[...] = pb.astype(BF16)

    row = lambda w: pl.BlockSpec((TM, w), lambda i: (i, 0))
    return pl.pallas_call(
        body, name="merge", grid=(T // TM,),
        in_specs=[row(Ka), row(Ka),
                  pl.BlockSpec((TM, D), lambda i: (i, 0)), pl.BlockSpec((TM, D), lambda i: (i, 1)),
                  pl.BlockSpec((Ka, D), lambda i: (0, 0)), pl.BlockSpec((Ka, D), lambda i: (0, 0))],
        out_specs=[row(D), row(D), row(D)],
        out_shape=[jax.ShapeDtypeStruct((T, D), BF16)] * 3,
        compiler_params=_cp(("parallel",)),
    )(ya, yb, gates, gates, w_pa, w_pb)


def wo_call(merged, w_o, x, modv, nw2, n_lat_tiles):
    T, D = x.shape

    def body(m_ref, w_ref, x_ref, mod_ref, nw_ref, x1_ref, o_ref, h2_ref):
        o = _dot(m_ref[...], w_ref[...])
        x1 = x_ref[...] + mod_ref[0, 2:3, :] * o
        x1_ref[...] = x1
        o_ref[...] = o.astype(BF16)
        h2_ref[...] = _norm_mod(x1, nw_ref[...], mod_ref[0, 3:4, :], mod_ref[0, 4:5, :]).astype(BF16)

    row = pl.BlockSpec((TM, D), lambda i: (i, 0))
    return pl.pallas_call(
        body, name="wo", grid=(T // TM,),
        in_specs=[row, pl.BlockSpec((D, D), lambda i: (0, 0)), row,
                  pl.BlockSpec((1, 6, D), lambda i: (_group_of(i, n_lat_tiles), 0, 0)),
                  pl.BlockSpec((1, D), lambda i: (0, 0))],
        out_specs=[row, row, row],
        out_shape=[jax.ShapeDtypeStruct((T, D), F32), jax.ShapeDtypeStruct((T, D), BF16),
                   jax.ShapeDtypeStruct((T, D), BF16)],
        compiler_params=_cp(("parallel",)),
    )(merged, w_o, x, modv, nw2)


def ffn_in_call(h2, w_ffn_in, n_col_tiles=2):
    T, D = h2.shape
    F = w_ffn_in.shape[1] // 2
    tn = F // n_col_tiles

    def body(h_ref, wa_ref, wu_ref, act_ref, a_ref, u_ref):
        h = h_ref[...]
        a = _dot(h, wa_ref[...])
        u = _dot(h, wu_ref[...])
        act_ref[...] = (a * _sigmoid(a) * u).astype(BF16)
        a_ref[...] = a.astype(BF16)
        u_ref[...] = u.astype(BF16)

    blk = pl.BlockSpec((TM, tn), lambda j, i: (i, j))
    return pl.pallas_call(
        body, name="ffn_in", grid=(n_col_tiles, T // TM),
        in_specs=[pl.BlockSpec((TM, D), lambda j, i: (i, 0)),
                  pl.BlockSpec((D, tn), lambda j, i: (0, j)),
                  pl.BlockSpec((D, tn), lambda j, i: (0, n_col_tiles + j))],
        out_specs=[blk, blk, blk],
        out_shape=[jax.ShapeDtypeStruct((T, F), BF16)] * 3,
        compiler_params=_cp(("parallel", "parallel"), VMEM_BIG),
    )(h2, w_ffn_in, w_ffn_in)


def ffn_out_call(act, w_ffn_out, x1, modv, n_lat_tiles, next_nw=None, next_modv=None):
    T, D = x1.shape
    F = act.shape[1]
    with_next = next_nw is not None

    def body(*refs):
        if with_next:
            a_ref, w_ref, x_ref, mod_ref, nw_ref, nmod_ref, x2_ref, f_ref, hn_ref = refs
        else:
            a_ref, w_ref, x_ref, mod_ref, x2_ref, f_ref = refs
        f = _dot(a_ref[...], w_ref[...])
        x2 = x_ref[...] + mod_ref[0, 5:6, :] * f
        x2_ref[...] = x2
        f_ref[...] = f.astype(BF16)
        if with_next:
            hn_ref[...] = _norm_mod(x2, nw_ref[...], nmod_ref[0, 0:1, :], nmod_ref[0, 1:2, :]).astype(BF16)

    row = pl.BlockSpec((TM, D), lambda i: (i, 0))
    modspec = pl.BlockSpec((1, 6, D), lambda i: (_group_of(i, n_lat_tiles), 0, 0))
    in_specs = [pl.BlockSpec((TM, F), lambda i: (i, 0)), pl.BlockSpec((F, D), lambda i: (0, 0)), row, modspec]
    args = [act, w_ffn_out, x1, modv]
    out_specs = [row, row]
    out_shape = [jax.ShapeDtypeStruct((T, D), F32), jax.ShapeDtypeStruct((T, D), BF16)]
    if with_next:
        in_specs += [pl.BlockSpec((1, D), lambda i: (0, 0)), modspec]
        args += [next_nw, next_modv]
        out_specs.append(row)
        out_shape.append(jax.ShapeDtypeStruct((T, D), BF16))
    return pl.pallas_call(
        body, name="ffn_out", grid=(T // TM,), in_specs=in_specs, out_specs=out_specs, out_shape=out_shape,
        compiler_params=_cp(("parallel",), VMEM_BIG),
    )(*args)


def loss_call(x, nw, target, n_lat_tiles):
    T, D = x.shape

    def body(x_ref, nw_ref, t_ref, dx_ref, dw_ref, ls_ref):
        i = pl.program_id(0)

        @pl.when(i < n_lat_tiles)
        def _():
            xv = x_ref[...]
            nw_v = nw_ref[...]
            r = _rms_r(xv)
            xn = xv * r
            e = xn * nw_v - t_ref[...]
            dy = e * (1.0 / D)
            dxn = dy * nw_v
            dx_ref[...] = r * (dxn - xn * jnp.mean(dxn * xn, axis=-1, keepdims=True))
            _acc(dw_ref, _colsum8(dy * xn), i == 0)
            _acc(ls_ref, _colsum8(e * e * (0.5 / D)), i == 0)

        @pl.when(i >= n_lat_tiles)
        def _():
            dx_ref[...] = jnp.zeros((TM, D), F32)

    stat = pl.BlockSpec((1, 8, D), lambda i: (0, 0, 0))
    return pl.pallas_call(
        body, name="loss", grid=(T // TM,),
        in_specs=[pl.BlockSpec((TM, D), lambda i: (i, 0)), pl.BlockSpec((1, D), lambda i: (0, 0)),
                  pl.BlockSpec((TM, D), lambda i: (jnp.minimum(i, n_lat_tiles - 1), 0))],
        out_specs=[pl.BlockSpec((TM, D), lambda i: (i, 0)), stat, stat],
        out_shape=[jax.ShapeDtypeStruct((T, D), F32), jax.ShapeDtypeStruct((1, 8, D), F32),
                   jax.ShapeDtypeStruct((1, 8, D), F32)],
        compiler_params=_cp(("arbitrary",)),
    )(x, nw, target)


def dact_call(dx2, modv, fsave, w_ffn_out, a, u, n_lat_tiles):
    T, D = dx2.shape
    F = a.shape[1]

    def body(dx_ref, mod_ref, f_ref, w_ref, a_ref, u_ref, da_ref, du_ref, df_ref, dg_ref):
        i = pl.program_id(0)
        dx = dx_ref[...]
        df = (dx * mod_ref[0, 5:6, :]).astype(BF16)
        df_ref[...] = df
        dact = _dot_nt(df, w_ref[...])
        av = a_ref[...].astype(F32)
        uv = u_ref[...].astype(F32)
        sg = _sigmoid(av)
        da_ref[...] = (dact * uv * (sg * (1.0 + av * (1.0 - sg)))).astype(BF16)
        du_ref[...] = (dact * (av * sg)).astype(BF16)
        _acc(dg_ref, _colsum8(dx * f_ref[...].astype(F32)), (i == 0) | (i == n_lat_tiles))

    row = pl.BlockSpec((TM, D), lambda i: (i, 0))
    wide = pl.BlockSpec((TM, F), lambda i: (i, 0))
    grp = lambda i: (_group_of(i, n_lat_tiles), 0, 0)
    return pl.pallas_call(
        body, name="dact", grid=(T // TM,),
        in_specs=[row, pl.BlockSpec((1, 6, D), grp), row, pl.BlockSpec((F, D), lambda i: (0, 0)), wide, wide],
        out_specs=[wide, wide, row, pl.BlockSpec((1, 8, D), grp)],
        out_shape=[jax.ShapeDtypeStruct((T, F), BF16), jax.ShapeDtypeStruct((T, F), BF16),
                   jax.ShapeDtypeStruct((T, D), BF16), jax.ShapeDtypeStruct((2, 8, D), F32)],
        compiler_params=_cp(("arbitrary",), VMEM_BIG),
    )(dx2, modv, fsave, w_ffn_out, a, u)


def _norm_bwd_tail(dh, x_ref, mod_ref, nw_ref, dres_ref, dx_ref, s_sh, s_sc, s_w, which, first):
    sc = mod_ref[0, 3 * which + 1:3 * which + 2, :]
    dxn, t_sh, t_sc, t_w = _norm_mod_bwd(dh, x_ref[...], nw_ref[...], sc)
    dx_ref[...] = dres_ref[...] + dxn
    _acc(s_sh, _colsum8(t_sh), first)
    _acc(s_sc, _colsum8(t_sc), first)
    _acc(s_w, _colsum8(t_w), first)


def dh2_call(da, du, w_ffn_in, x1, modv, nw2, dx2, n_lat_tiles):
    T, D = x1.shape
    F = da.shape[1]

    def body(da_ref, du_ref, wa_ref, wu_ref, x_ref, mod_ref, nw_ref, dres_ref, dx_ref, s_sh, s_sc, s_w):
        i = pl.program_id(0)
        dh = _dot_nt(da_ref[...], wa_ref[...]) + _dot_nt(du_ref[...], wu_ref[...])
        _norm_bwd_tail(dh, x_ref, mod_ref, nw_ref, dres_ref, dx_ref, s_sh, s_sc, s_w, 1,
                       (i == 0) | (i == n_lat_tiles))

    row = pl.BlockSpec((TM, D), lambda i: (i, 0))
    wide = pl.BlockSpec((TM, F), lambda i: (i, 0))
    grp = lambda i: (_group_of(i, n_lat_tiles), 0, 0)
    stat = pl.BlockSpec((1, 8, D), grp)
    return pl.pallas_call(
        body, name="dh2", grid=(T // TM,),
        in_specs=[wide, wide, pl.BlockSpec((D, F), lambda i: (0, 0)), pl.BlockSpec((D, F), lambda i: (0, 1)),
                  row, pl.BlockSpec((1, 6, D), grp), pl.BlockSpec((1, D), lambda i: (0, 0)), row],
        out_specs=[row, stat, stat, stat],
        out_shape=[jax.ShapeDtypeStruct((T, D), F32)] + [jax.ShapeDtypeStruct((2, 8, D), F32)] * 3,
        compiler_params=_cp(("arbitrary",), VMEM_BIG),
    )(da, du, w_ffn_in, w_ffn_in, x1, modv, nw2, dx2)


def dmerged_call(dx1, modv, osave, w_o, pa, pb, gates, n_lat_tiles):
    T, D = dx1.shape

    def body(dx_ref, mod_ref, o_ref, w_ref, pa_ref, pb_ref, ga_ref, gb_ref,
             dpa_ref, dpb_ref, dgate_ref, do_ref, dg_ref):
        i = pl.program_id(0)
        dx = dx_ref[...]
        do = (dx * mod_ref[0, 2:3, :]).astype(BF16)
        do_ref[...] = do
        dm = _dot_nt(do, w_ref[...])
        sa = _sigmoid(ga_ref[...].astype(F32))
        sb = _sigmoid(gb_ref[...].astype(F32))
        dpa_ref[...] = (dm * sa).astype(BF16)
        dpb_ref[...] = (dm * sb).astype(BF16)
        dgate_ref[:, :D] = (dm * pa_ref[...].astype(F32) * (sa * (1.0 - sa))).astype(BF16)
        dgate_ref[:, D:] = (dm * pb_ref[...].astype(F32) * (sb * (1.0 - sb))).astype(BF16)
        _acc(dg_ref, _colsum8(dx * o_ref[...].astype(F32)), (i == 0) | (i == n_lat_tiles))

    row = pl.BlockSpec((TM, D), lambda i: (i, 0))
    grp = lambda i: (_group_of(i, n_lat_tiles), 0, 0)
    return pl.pallas_call(
        body, name="dmerged", grid=(T // TM,),
        in_specs=[row, pl.BlockSpec((1, 6, D), grp), row, pl.BlockSpec((D, D), lambda i: (0, 0)), row, row,
                  pl.BlockSpec((TM, D), lambda i: (i, 0)), pl.BlockSpec((TM, D), lambda i: (i, 1))],
        out_specs=[row, row, pl.BlockSpec((TM, 2 * D), lambda i: (i, 0)), row, pl.BlockSpec((1, 8, D), grp)],
        out_shape=[jax.ShapeDtypeStruct((T, D), BF16), jax.ShapeDtypeStruct((T, D), BF16),
                   jax.ShapeDtypeStruct((T, 2 * D), BF16), jax.ShapeDtypeStruct((T, D), BF16),
                   jax.ShapeDtypeStruct((2, 8, D), F32)],
        compiler_params=_cp(("arbitrary",)),
    )(dx1, modv, osave, w_o, pa, pb, gates, gates)


def dy_call(dpa, dpb, w_pa, w_pb):
    T, D = dpa.shape
    K = w_pa.shape[0]

    def body(a_ref, b_ref, wa_ref, wb_ref, ya_ref, yb_ref):
        ya_ref[...] = _dot_nt(a_ref[...], wa_ref[...]).astype(BF16)
        yb_ref[...] = _dot_nt(b_ref[...], wb_ref[...]).astype(BF16)

    row = pl.BlockSpec((TM, D), lambda i: (i, 0))
    w = pl.BlockSpec((K, D), lambda i: (0, 0))
    out = pl.BlockSpec((TM, K), lambda i: (i, 0))
    return pl.pallas_call(
        body, name="dy", grid=(T // TM,), in_specs=[row, row, w, w], out_specs=[out, out],
        out_shape=[jax.ShapeDtypeStruct((T, K), BF16)] * 2,
        compiler_params=_cp(("parallel",)),
    )(dpa, dpb, w_pa, w_pb)


def dh_call(dna, dg, dgates, w_in, x, modv, nw1, dx1, n_lat_tiles):
    T, D = x.shape
    n1, n2, n3 = dna.shape[1], dg.shape[1], dgates.shape[1]
    N = n1 + n2 + n3

    def body(a_ref, b_ref, c_ref, w_ref, x_ref, mod_ref, nw_ref, dres_ref, dx_ref, s_sh, s_sc, s_w):
        i = pl.program_id(0)
        dh = (_dot_nt(a_ref[...], w_ref[:, :n1]) + _dot_nt(b_ref[...], w_ref[:, n1:n1 + n2])
              + _dot_nt(c_ref[...], w_ref[:, n1 + n2:]))
        _norm_bwd_tail(dh, x_ref, mod_ref, nw_ref, dres_ref, dx_ref, s_sh, s_sc, s_w, 0,
                       (i == 0) | (i == n_lat_tiles))

    row = pl.BlockSpec((TM, D), lambda i: (i, 0))
    grp = lambda i: (_group_of(i, n_lat_tiles), 0, 0)
    stat = pl.BlockSpec((1, 8, D), grp)
    return pl.pallas_call(
        body, name="dh", grid=(T // TM,),
        in_specs=[pl.BlockSpec((TM, n1), lambda i: (i, 0)), pl.BlockSpec((TM, n2), lambda i: (i, 0)),
                  pl.BlockSpec((TM, n3), lambda i: (i, 0)), pl.BlockSpec((D, N), lambda i: (0, 0)),
                  row, pl.BlockSpec((1, 6, D), grp), pl.BlockSpec((1, D), lambda i: (0, 0)), row],
        out_specs=[row, stat, stat, stat],
        out_shape=[jax.ShapeDtypeStruct((T, D), F32)] + [jax.ShapeDtypeStruct((2, 8, D), F32)] * 3,
        compiler_params=_cp(("arbitrary",), VMEM_BIG),
    )(dna, dg, dgates, w_in, x, modv, nw1, dx1)


def _pick_tile(n, cap, mult):
    best = None
    for d in range(mult, min(n, cap) + 1, mult):
        if n % d == 0:
            best = d
    assert best is not None, (n, cap, mult)
    return best


def wgrad_call(a, b, name):
    T, K = a.shape
    N = b.shape[1]
    tt = _pick_tile(T, 768, 128)
    tk = K if K <= 1024 else _pick_tile(K, 1408, 128)
    tn = N if N <= 2304 else _pick_tile(N, 2304, 128)
    nt = T // tt

    def body(a_ref, b_ref, o_ref, acc_ref):
        t = pl.program_id(2)
        p = _dot_tn(a_ref[...], b_ref[...])

        @pl.when(t == 0)
        def _():
            acc_ref[...] = p

        @pl.when(t > 0)
        def _():
            acc_ref[...] += p

        @pl.when(t == nt - 1)
        def _():
            o_ref[...] = acc_ref[...].astype(BF16)

    return pl.pallas_call(
        body, name=name, grid=(K // tk, N // tn, nt),
        in_specs=[pl.BlockSpec((tt, tk), lambda i, j, t: (t, i)), pl.BlockSpec((tt, tn), lambda i, j, t: (t, j))],
        out_specs=pl.BlockSpec((tk, tn), lambda i, j, t: (i, j)),
        out_shape=jax.ShapeDtypeStruct((K, N), BF16),
        scratch_shapes=[pltpu.VMEM((tk, tn), F32)],
        compiler_params=_cp(("parallel", "parallel", "arbitrary"), VMEM_BIG),
    )(a, b)


def _head_mean(v):
    n = v.shape[-1]
    r = lax.broadcasted_iota(jnp.int32, (n, n), 0) // HEAD_DIM
    c = lax.broadcasted_iota(jnp.int32, (n, n), 1) // HEAD_DIM
    bd = jnp.where(r == c, 1.0 / HEAD_DIM, 0.0).astype(BF16)
    hi = v.astype(BF16)
    lo = (v - hi.astype(F32)).astype(BF16)
    return _dot(hi, bd) + _dot(lo, bd)


def _swap_pairs(x):
    n = x.shape[-1]
    even = lax.broadcasted_iota(jnp.int32, x.shape, x.ndim - 1) % 2 == 0
    return jnp.where(even, pltpu.roll(x, n - 1, x.ndim - 1), pltpu.roll(x, 1, x.ndim - 1))


def _tile_lanes(t, n):
    return jnp.tile(t, (1, n // t.shape[-1]))


def gqa_prep_call(gqkv, qgain, kgain, cos_t, sin_t):
    T = gqkv.shape[0]
    nq = GQA_Q_HEADS * HEAD_DIM
    nk = GQA_KV_HEADS * HEAD_DIM
    rep = GQA_Q_HEADS // GQA_KV_HEADS

    def body(g_ref, qg_ref, kg_ref, c_ref, s_ref, qp_ref, kp_ref):
        cos_v, sin_v = c_ref[...], s_ref[...]
        xq = g_ref[:, :nq].astype(F32)
        yq = xq * lax.rsqrt(_head_mean(xq * xq) + EPS) * qg_ref[...]
        yq = (yq * _tile_lanes(cos_v, nq) + _swap_pairs(yq) * _tile_lanes(sin_v, nq)) * SCALE
        lo = _lane_lt64((TM, LANES))
        for p in range(GQA_Q_HEADS // 2):
            g = (2 * p) // rep
            chunk = yq[:, LANES * p:LANES * (p + 1)]
            rolled = _roll64(chunk)
            if g == 0:
                first, second = jnp.where(lo, chunk, 0.0), jnp.where(lo, rolled, 0.0)
            else:
                first, second = jnp.where(lo, 0.0, rolled), jnp.where(lo, 0.0, chunk)
            qp_ref[:, 2 * LANES * p:2 * LANES * p + LANES] = first.astype(BF16)
            qp_ref[:, 2 * LANES * p + LANES:2 * LANES * (p + 1)] = second.astype(BF16)
        xk = g_ref[:, nq:nq + nk].astype(F32)
        yk = xk * lax.rsqrt(_head_mean(xk * xk) + EPS) * kg_ref[...]
        kp_ref[...] = (yk * cos_v + _swap_pairs(yk) * sin_v).astype(BF16)

    return pl.pallas_call(
        body, name="gqa_prep", grid=(T // TM,),
        in_specs=[pl.BlockSpec((TM, gqkv.shape[1]), lambda i: (i, 0)),
                  pl.BlockSpec((1, nq), lambda i: (0, 0)), pl.BlockSpec((1, nk), lambda i: (0, 0)),
                  pl.BlockSpec((TM, nk), lambda i: (i, 0)), pl.BlockSpec((TM, nk), lambda i: (i, 0))],
        out_specs=[pl.BlockSpec((TM, GQA_Q_HEADS * LANES), lambda i: (i, 0)), pl.BlockSpec((TM, nk), lambda i: (i, 0))],
        out_shape=[jax.ShapeDtypeStruct((T, GQA_Q_HEADS * LANES), BF16), jax.ShapeDtypeStruct((T, nk), BF16)],
        compiler_params=_cp(("parallel",)),
    )(gqkv, qgain, kgain, cos_t, sin_t)


def gqa_prep_bwd_call(gqkv, qgain, kgain, cos_t, sin_t, dqp, dkp, dv):
    T = gqkv.shape[0]
    nq = GQA_Q_HEADS * HEAD_DIM
    nk = GQA_KV_HEADS * HEAD_DIM
    rep = GQA_Q_HEADS // GQA_KV_HEADS

    def norm_rope_bwd(x, gain, dy, cos_v, sin_v):
        dyn = dy * cos_v - _swap_pairs(dy) * sin_v
        r = lax.rsqrt(_head_mean(x * x) + EPS)
        xn = x * r
        dxn = dyn * gain
        dx = r * (dxn - xn * _head_mean(dxn * xn))
        return dx, dyn * xn

    def body(g_ref, qg_ref, kg_ref, c_ref, s_ref, dqp_ref, dkp_ref, dv_ref, dg_ref, sq_ref, sk_ref):
        i = pl.program_id(0)
        cos_v, sin_v = c_ref[...], s_ref[...]
        lo = _lane_lt64((TM, LANES))
        chunks = []
        for p in range(GQA_Q_HEADS // 2):
            g = (2 * p) // rep
            da = dqp_ref[:, 2 * LANES * p:2 * LANES * p + LANES]
            db = dqp_ref[:, 2 * LANES * p + LANES:2 * LANES * (p + 1)]
            if g == 0:
                chunks.append(jnp.where(lo, da, _roll64(db)))
            else:
                chunks.append(jnp.where(lo, _roll64(da), db))
        dyq = jnp.concatenate(chunks, axis=1) * SCALE
        dxq, gq_term = norm_rope_bwd(g_ref[:, :nq].astype(F32), qg_ref[...], dyq,
                                     _tile_lanes(cos_v, nq), _tile_lanes(sin_v, nq))
        dxk, gk_term = norm_rope_bwd(g_ref[:, nq:nq + nk].astype(F32), kg_ref[...], dkp_ref[...], cos_v, sin_v)
        dg_ref[:, :nq] = dxq.astype(BF16)
        dg_ref[:, nq:nq + nk] = dxk.astype(BF16)
        dg_ref[:, nq + nk:] = dv_ref[...].astype(BF16)
        _acc(sq_ref, _colsum8(gq_term), i == 0)
        _acc(sk_ref, _colsum8(gk_term), i == 0)

    W = gqkv.shape[1]
    return pl.pallas_call(
        body, name="gqa_prep_bwd", grid=(T // TM,),
        in_specs=[pl.BlockSpec((TM, W), lambda i: (i, 0)),
                  pl.BlockSpec((1, nq), lambda i: (0, 0)), pl.BlockSpec((1, nk), lambda i: (0, 0)),
                  pl.BlockSpec((TM, nk), lambda i: (i, 0)), pl.BlockSpec((TM, nk), lambda i: (i, 0)),
                  pl.BlockSpec((TM, GQA_Q_HEADS * LANES), lambda i: (i, 0)),
                  pl.BlockSpec((TM, nk), lambda i: (i, 0)), pl.BlockSpec((TM, nk), lambda i: (i, 0))],
        out_specs=[pl.BlockSpec((TM, W), lambda i: (i, 0)),
                   pl.BlockSpec((1, 8, nq), lambda i: (0, 0, 0)), pl.BlockSpec((1, 8, nk), lambda i: (0, 0, 0))],
        out_shape=[jax.ShapeDtypeStruct((T, W), BF16), jax.ShapeDtypeStruct((1, 8, nq), F32),
                   jax.ShapeDtypeStruct((1, 8, nk), F32)],
        compiler_params=_cp(("arbitrary",)),
    )(gqkv, qgain, kgain, cos_t, sin_t, dqp, dkp, dv)


GQ_TQ_LATENT = 512
GQ_TK = 256
GQ_CHUNKS_PER_TRIP = 8
GQ_FORWARD_STEPS = 4


def _chunk_offset(j):
    return j * GQ_TK if isinstance(j, int) else pl.multiple_of(j * GQ_TK, GQ_TK)


def _gqa_blocks(n_lat, T, latent):
    if latent:
        tq = GQ_TQ_LATENT if n_lat % GQ_TQ_LATENT == 0 else GQ_TK
        return tq, 0, n_lat // tq
    return GQ_TK, n_lat // GQ_TK, (T - n_lat) // GQ_TK


def _sweep_keys(chunk, latent, n_lat_k, nk):
    if latent:
        per_trip = math.gcd(n_lat_k, GQ_CHUNKS_PER_TRIP)

        def trip(jj, carry):
            for t in range(per_trip):
                chunk(per_trip * jj + t)
            return carry

        lax.fori_loop(0, n_lat_k // per_trip, trip, 0)
    for j in range(n_lat_k, nk):
        chunk(j)


def _gqa_place(chunk, half, g):
    gv = jnp.broadcast_to(g, chunk.shape)
    placed = jnp.where(gv == half, chunk, _roll64(chunk))
    return jnp.where(_lane_half(chunk.shape) == gv, placed, jnp.zeros_like(placed))


GQA_SMALL_SCORE = 20.0


def gqa_scores_small(q_gain, k_gain):
    bound = SCALE * HEAD_DIM * jnp.max(jnp.abs(q_gain)) * jnp.max(jnp.abs(k_gain))
    return (bound <= GQA_SMALL_SCORE).astype(jnp.int32).reshape(1)


def gqa_fwd_call(qp, kp, gqkv, n_lat, scores_small, gather=None):
    first = _gqa_fwd_part(qp, kp, gqkv, n_lat, scores_small, True, None, gather)
    y, lse = _gqa_fwd_part(qp, kp, gqkv, n_lat, scores_small, False, (first[0], first[1]), None)
    return (y, lse) + tuple(first[2:])


def _gqa_fwd_part(qp, kp, gqkv, n_lat, scores_small, latent, prev, gather):
    T = qp.shape[0]
    rep = GQA_Q_HEADS // GQA_KV_HEADS
    GQ_TQ, q0, nq = _gqa_blocks(n_lat, T, latent)
    nk = T // GQ_TK
    n_lat_k = n_lat // GQ_TK
    v_blk = (GQA_Q_HEADS + GQA_KV_HEADS) * HEAD_DIM // LANES
    R = rep * GQ_TQ

    def body(*refs):
        if prev is not None:
            small_ref, q_ref, k_ref, v_ref, _, _, y_ref, lse_ref, m_sc, acc_sc = refs
        elif gather is None:
            small_ref, q_ref, k_ref, v_ref, y_ref, lse_ref, m_sc, acc_sc = refs
        else:
            nb = len(gather)
            small_ref, q_ref, k_ref, v_ref = refs[:4]
            pk_refs = refs[4:4 + nb]
            y_ref, lse_ref = refs[4 + nb:6 + nb]
            ga_refs = refs[6 + nb:6 + 2 * nb]
            m_sc, acc_sc, ssem, rsem = refs[6 + 2 * nb:]
        g = pl.program_id(0)
        qb = pl.program_id(1)
        if gather is not None:
            @pl.when((g == 0) & (qb == 0))
            def _():
                _ag_start(pk_refs, ga_refs, ssem, rsem)
        qs = jnp.concatenate([q_ref[:, LANES * i:LANES * (i + 1)] for i in range(rep)], axis=0)
        is_lat = latent
        nsub = GQ_TK // LANES
        scores_small = small_ref[0] == 1

        @pl.when(scores_small)
        def _():
            m_sc[...] = jnp.zeros((R, LANES), F32)

        @pl.when(jnp.logical_not(scores_small))
        def _():
            m_sc[...] = jnp.full((R, LANES), NEG, F32)

            def chunk_max(j):
                s = _dot_nt(qs, k_ref[pl.ds(_chunk_offset(j), GQ_TK), :])
                mm = s[:, :LANES]
                for t in range(1, nsub):
                    mm = jnp.maximum(mm, s[:, LANES * t:LANES * (t + 1)])
                m_sc[...] = jnp.maximum(m_sc[...], mm)

            _sweep_keys(chunk_max, is_lat, n_lat_k, nk)
            m_sc[...] = jnp.broadcast_to(jnp.max(m_sc[...], axis=-1, keepdims=True), (R, LANES))

        acc_sc[...] = jnp.zeros((R, LANES), F32)
        own_v = _lane_half((GQ_TK, LANES)) == jnp.broadcast_to(g, (GQ_TK, LANES))

        def chunk_acc(j):
            off = _chunk_offset(j)
            vc = v_ref[pl.ds(off, GQ_TK), :]
            v1 = jnp.where(own_v, vc, jnp.ones_like(vc))
            s = _dot_nt(qs, k_ref[pl.ds(off, GQ_TK), :])
            mb = m_sc[...]
            p = jnp.concatenate([jnp.exp(s[:, LANES * t:LANES * (t + 1)] - mb).astype(BF16) for t in range(nsub)],
                                axis=1)
            acc_sc[...] += _dot(p, v1)

        _sweep_keys(chunk_acc, is_lat, n_lat_k, nk)
        acc = acc_sc[...]
        own = _lane_half((R, LANES)) == jnp.broadcast_to(g, (R, LANES))
        l = jnp.where(own, _roll64(acc), acc)
        o = acc / l
        lse = m_sc[...] + jnp.log(l)
        lo = _lane_lt64((GQ_TQ, LANES))
        g0 = jnp.broadcast_to(g, (GQ_TQ, LANES)) == 0
        for c2 in range(rep // 2):
            oa = o[(2 * c2) * GQ_TQ:(2 * c2 + 1) * GQ_TQ]
            ob = o[(2 * c2 + 1) * GQ_TQ:(2 * c2 + 2) * GQ_TQ]
            left = jnp.where(g0, oa, _roll64(oa))
            right = jnp.where(g0, _roll64(ob), ob)
            y_ref[:, LANES * c2:LANES * (c2 + 1)] = jnp.where(lo, left, right).astype(BF16)
        for i in range(rep):
            lse_ref[i] = lse[i * GQ_TQ:(i + 1) * GQ_TQ]
        if gather is not None:
            @pl.when((g == GQA_KV_HEADS - 1) & (qb == max(nq - 1 - GQ_FORWARD_STEPS, 0)))
            def _():
                _ag_forward(pk_refs, ga_refs, ssem, rsem)

            @pl.when((g == GQA_KV_HEADS - 1) & (qb == nq - 1))
            def _():
                _ag_finish(pk_refs, ga_refs, ssem, rsem)

    in_specs = [pl.BlockSpec((GQ_TQ, rep * LANES), lambda g, i, s: (q0 + i, g)),
                pl.BlockSpec((T, LANES), lambda g, i, s: (0, 0)),
                pl.BlockSpec((T, LANES), lambda g, i, s: (0, v_blk))]
    out_specs = [pl.BlockSpec((GQ_TQ, rep * HEAD_DIM), lambda g, i, s: (q0 + i, g)),
                 pl.BlockSpec((rep, GQ_TQ, LANES), lambda g, i, s: (g, q0 + i, 0))]
    out_shape = [jax.ShapeDtypeStruct((T, GQA_Q_HEADS * HEAD_DIM), BF16),
                 jax.ShapeDtypeStruct((GQA_Q_HEADS, T, LANES), F32)]
    scratch = [pltpu.VMEM((R, LANES), F32), pltpu.VMEM((R, LANES), F32)]
    args = [scores_small, qp, kp, gqkv]
    aliases = {}
    name = "gqa_fwd" if latent else "gqa_fwd_ctx"
    if prev is not None:
        in_specs += [ANY, ANY]
        args += list(prev)
        aliases = {4: 0, 5: 1}
    if gather is not None:
        in_specs += [ANY] * len(gather)
        out_specs += [ANY] * len(gather)
        out_shape += [jax.ShapeDtypeStruct((N_CHIP,) + p.shape, p.dtype) for p in gather]
        scratch += [pltpu.SemaphoreType.DMA((AG_SEMS * len(gather),)), pltpu.SemaphoreType.DMA((AG_SEMS * len(gather),))]
        args += list(gather)
        name = "gqa_fwd_gather"
    return pl.pallas_call(
        body, name=name,
        grid_spec=pltpu.PrefetchScalarGridSpec(
            num_scalar_prefetch=1, grid=(GQA_KV_HEADS, nq), in_specs=in_specs, out_specs=out_specs,
            scratch_shapes=scratch),
        out_shape=out_shape, input_output_aliases=aliases,
        compiler_params=_cp(("arbitrary", "arbitrary") if gather is not None else ("parallel", "parallel"), VMEM_BIG),
    )(*args)


def gqa_bwd_call(qp, kp, gqkv, yb, dyb, lse, n_lat, exchange=None):
    first = _gqa_bwd_part(qp, kp, gqkv, yb, dyb, lse, n_lat, True, None, exchange)
    dq, dk_c, dv_c = _gqa_bwd_part(qp, kp, gqkv, yb, dyb, lse, n_lat, False, first[0], None)
    return (dq, first[1] + dk_c, first[2] + dv_c) + tuple(first[3:])


def _gqa_bwd_part(qp, kp, gqkv, yb, dyb, lse, n_lat, latent, prev_dq, exchange):
    T = qp.shape[0]
    rep = GQA_Q_HEADS // GQA_KV_HEADS
    GQ_TQ, q0, nq = _gqa_blocks(n_lat, T, latent)
    nk = T // GQ_TK
    n_lat_k = n_lat // GQ_TK
    v_blk = (GQA_Q_HEADS + GQA_KV_HEADS) * HEAD_DIM // LANES
    R = rep * GQ_TQ

    def body(*refs):
        if prev_dq is not None:
            q_ref, k_ref, v_ref, y_ref, dy_ref, lse_ref, _, dq_ref, dk_ref, dv_ref, dq_sc = refs
        elif exchange is None:
            q_ref, k_ref, v_ref, y_ref, dy_ref, lse_ref, dq_ref, dk_ref, dv_ref, dq_sc = refs
        else:
            nb = len(exchange)
            q_ref, k_ref, v_ref, y_ref, dy_ref, lse_ref = refs[:6]
            gp_refs = refs[6:6 + nb]
            dq_ref, dk_ref, dv_ref = refs[6 + nb:9 + nb]
            rx_refs = refs[9 + nb:9 + 2 * nb]
            dq_sc, ssem, rsem = refs[9 + 2 * nb:]
        g = pl.program_id(0)
        qb = pl.program_id(1)
        if exchange is not None:
            @pl.when((g == 0) & (qb == 0))
            def _():
                _rs_start(gp_refs, rx_refs, ssem, rsem)

        @pl.when((g == 0) & (qb == 0))
        def _():
            dk_ref[...] = jnp.zeros((T, LANES), F32)
            dv_ref[...] = jnp.zeros((T, LANES), F32)

        qs = jnp.concatenate([q_ref[:, LANES * i:LANES * (i + 1)] for i in range(rep)], axis=0)
        dos, deltas, lses = [], [], []
        for i in range(rep):
            c2, half = i // 2, i % 2
            dch = _gqa_place(dy_ref[:, LANES * c2:LANES * (c2 + 1)].astype(F32), half, g)
            ych = _gqa_place(y_ref[:, LANES * c2:LANES * (c2 + 1)].astype(F32), half, g)
            dos.append(dch.astype(BF16))
            deltas.append(jnp.sum(dch * ych, axis=-1, keepdims=True))
            lses.append(lse_ref[i])
        do_s = jnp.concatenate(dos, axis=0)
        delta = jnp.broadcast_to(jnp.concatenate(deltas, axis=0), (R, LANES))
        lse_v = jnp.concatenate(lses, axis=0)
        dq_sc[...] = jnp.zeros((R, LANES), F32)
        nsub = GQ_TK // LANES

        def chunk(j):
            off = _chunk_offset(j)
            kc = k_ref[pl.ds(off, GQ_TK), :]
            vc = v_ref[pl.ds(off, GQ_TK), :]
            s = _dot_nt(qs, kc)
            dp = _dot_nt(do_s, vc)
            ps, dss = [], []
            for t in range(nsub):
                pt = jnp.exp(s[:, LANES * t:LANES * (t + 1)] - lse_v)
                ps.append(pt.astype(BF16))
                dss.append((pt * (dp[:, LANES * t:LANES * (t + 1)] - delta)).astype(BF16))
            p = jnp.concatenate(ps, axis=1)
            ds = jnp.concatenate(dss, axis=1)
            dv_ref[pl.ds(off, GQ_TK), :] += _dot_tn(p, do_s)
            dk_ref[pl.ds(off, GQ_TK), :] += _dot_tn(ds, qs)
            dq_sc[...] += _dot(ds, kc)

        _sweep_keys(chunk, latent, n_lat_k, nk)
        for i in range(rep):
            dq_ref[:, LANES * i:LANES * (i + 1)] = dq_sc[i * GQ_TQ:(i + 1) * GQ_TQ, :]
        if exchange is not None:
            @pl.when((g == GQA_KV_HEADS - 1) & (qb == nq - 1))
            def _():
                _rs_finish(gp_refs, rx_refs, ssem, rsem)

    whole = pl.BlockSpec((T, LANES), lambda g, i: (0, 0))
    in_specs = [pl.BlockSpec((GQ_TQ, rep * LANES), lambda g, i: (q0 + i, g)),
                whole,
                pl.BlockSpec((T, LANES), lambda g, i: (0, v_blk)),
                pl.BlockSpec((GQ_TQ, rep * HEAD_DIM), lambda g, i: (q0 + i, g)),
                pl.BlockSpec((GQ_TQ, rep * HEAD_DIM), lambda g, i: (q0 + i, g)),
                pl.BlockSpec((rep, GQ_TQ, LANES), lambda g, i: (g, q0 + i, 0))]
    out_specs = [pl.BlockSpec((GQ_TQ, rep * LANES), lambda g, i: (q0 + i, g)), whole, whole]
    out_shape = [jax.ShapeDtypeStruct((T, GQA_Q_HEADS * LANES), F32),
                 jax.ShapeDtypeStruct((T, LANES), F32), jax.ShapeDtypeStruct((T, LANES), F32)]
    scratch = [pltpu.VMEM((R, LANES), F32)]
    args = [qp, kp, gqkv, yb, dyb, lse]
    aliases = {}
    name = "gqa_bwd" if latent else "gqa_bwd_ctx"
    if prev_dq is not None:
        in_specs.append(ANY)
        args.append(prev_dq)
        aliases = {6: 0}
    if exchange is not None:
        in_specs += [ANY] * len(exchange)
        out_specs += [ANY] * len(exchange)
        out_shape += [_rs_out_shape(g) for g in exchange]
        scratch += [pltpu.SemaphoreType.DMA((RS_SEMS * len(exchange),)), pltpu.SemaphoreType.DMA((RS_SEMS * len(exchange),))]
        args += list(exchange)
        name = "gqa_bwd_exchange"
    return pl.pallas_call(
        body, name=name, grid=(GQA_KV_HEADS, nq),
        in_specs=in_specs, out_specs=out_specs, out_shape=out_shape, scratch_shapes=scratch,
        input_output_aliases=aliases,
        compiler_params=_cp(("arbitrary", "arbitrary"), VMEM_BIG),
    )(*args)


def _na_tables(rows):
    kh = min(NA_WIN_H, rows)
    assert kh == NA_WIN_H and rows >= NA_KROWS and rows % NA_ROWS == 0
    a = np.zeros((3, NA_ROWS, NA_KROWS, 2 * NA_WIN_H - 1), np.float32)
    for v, r0 in enumerate((0, NA_ROWS, rows - NA_ROWS)):
        ks = min(max(r0 - NA_WIN_H // 2, 0), rows - NA_KROWS)
        for rq in range(NA_ROWS):
            r = r0 + rq
            rs = min(max(r - kh // 2, 0), rows - kh)
            for rk in range(NA_KROWS):
                kr = ks + rk
                if rs <= kr < rs + kh:
                    a[v, rq, rk, kr - r + NA_WIN_H - 1] = 1.0
    c = np.zeros((GRID_W, GRID_W, 2 * NA_WIN_W - 1), np.float32)
    for wq in range(GRID_W):
        cs = min(max(wq - NA_WIN_W // 2, 0), GRID_W - NA_WIN_W)
        for wk in range(cs, cs + NA_WIN_W):
            c[wq, wk, min(max(wk - wq, -(NA_WIN_W - 1)), NA_WIN_W - 1) + NA_WIN_W - 1] = 1.0
    return a, c


NA_DR = 2 * NA_WIN_H - 1


def _na_tile_index(rows):
    a, _ = _na_tables(rows)
    idx = np.where(a.sum(-1) > 0, a.argmax(-1), NA_DR).astype(np.int32)
    return jnp.asarray(idx.reshape(-1))


def na_bias_table(rpb, rows):
    _, c = _na_tables(rows)
    H = rpb.shape[0]
    KW = NA_KROWS * GRID_W
    c2 = np.concatenate([c, c], axis=1)
    t1 = jnp.einsum("hde,wue->hdwu", rpb, jnp.asarray(c2), precision=lax.Precision.HIGHEST)
    t1 = jnp.where(jnp.asarray(c2.sum(-1) > 0)[None, None], t1, NEG)
    t1 = jnp.concatenate([t1, jnp.full((H, 1, GRID_W, LANES), NEG, F32)], axis=1)

    def body(idx_ref, t_ref, o_ref):
        v = pl.program_id(1)
        lo = _lane_lt64((GRID_W, LANES))
        for q in range(NA_ROWS):
            for kp in range(NA_KROWS // 2):
                base = (v * NA_ROWS + q) * NA_KROWS + 2 * kp
                tile = jnp.where(lo, t_ref[0, idx_ref[base]], t_ref[0, idx_ref[base + 1]])
                o_ref[0, 0, GRID_W * q:GRID_W * (q + 1), LANES * kp:LANES * (kp + 1)] = tile

    return pl.pallas_call(
        body, name="na_bias_table",
        grid_spec=pltpu.PrefetchScalarGridSpec(
            num_scalar_prefetch=1, grid=(H, 3),
            in_specs=[pl.BlockSpec((1, NA_DR + 1, GRID_W, LANES), lambda h, v, i: (h, 0, 0, 0))],
            out_specs=pl.BlockSpec((1, 1, TM, KW), lambda h, v, i: (h, v, 0, 0))),
        out_shape=jax.ShapeDtypeStruct((H, 3, TM, KW), F32),
        compiler_params=_cp(("parallel", "parallel")),
    )(_na_tile_index(rows), t1)


def na_bias_grad(dbias, rows):
    _, c = _na_tables(rows)
    H = dbias.shape[0]
    KW = NA_KROWS * GRID_W

    def body(idx_ref, d_ref, o_ref):
        v = pl.program_id(1)
        lo = _lane_lt64((GRID_W, LANES))

        @pl.when(v == 0)
        def _():
            o_ref[...] = jnp.zeros((1, NA_DR + 1, GRID_W, LANES), F32)

        for q in range(NA_ROWS):
            for kp in range(NA_KROWS // 2):
                base = (v * NA_ROWS + q) * NA_KROWS + 2 * kp
                tile = d_ref[0, 0, GRID_W * q:GRID_W * (q + 1), LANES * kp:LANES * (kp + 1)]
                o_ref[0, idx_ref[base]] += jnp.where(lo, tile, 0.0)
                o_ref[0, idx_ref[base + 1]] += jnp.where(lo, 0.0, tile)

    dt = pl.pallas_call(
        body, name="na_bias_grad",
        grid_spec=pltpu.PrefetchScalarGridSpec(
            num_scalar_prefetch=1, grid=(H, 3),
            in_specs=[pl.BlockSpec((1, 1, TM, KW), lambda h, v, i: (h, v, 0, 0))],
            out_specs=pl.BlockSpec((1, NA_DR + 1, GRID_W, LANES), lambda h, v, i: (h, 0, 0, 0))),
        out_shape=jax.ShapeDtypeStruct((H, NA_DR + 1, GRID_W, LANES), F32),
        compiler_params=_cp(("parallel", "arbitrary")),
    )(_na_tile_index(rows), dbias)
    dt = dt[:, :NA_DR, :, :GRID_W] + dt[:, :NA_DR, :, GRID_W:]
    return jnp.einsum("hdwu,wue->hde", dt, jnp.asarray(c), precision=lax.Precision.HIGHEST)


def _na_variant(blk, n_lat_blk):
    return jnp.where(blk == 0, 0, jnp.where(blk >= n_lat_blk - 1, 2, 1))


def _na_kstart(blk, rows):
    r0 = blk * NA_ROWS
    ks = jnp.clip(r0 - NA_WIN_H // 2, 0, rows - NA_KROWS)
    return pl.multiple_of(ks * GRID_W, GRID_W)


def na_fwd_call(naqkv, bias, n_lat):
    T = naqkv.shape[0]
    nb = T // TM
    n_lat_blk = n_lat // TM
    rows = n_lat // GRID_W
    n_ctx = T - n_lat
    KW = NA_KROWS * GRID_W
    npair = NA_HEADS // 2

    def body(q_ref, k_ref, v_ref, b_ref, y_ref, lse_ref):
        blk = pl.program_id(1)
        half = _lane_half((TM, LANES))
        lo = half == 0
        q = q_ref[...].astype(F32) * SCALE
        kctx = k_ref[pl.ds(n_lat, n_ctx), :]
        vctx = v_ref[pl.ds(n_lat, n_ctx), :]

        @pl.when(blk < n_lat_blk)
        def _():
            off = _na_kstart(blk, rows)
            kwin = k_ref[pl.ds(off, KW), :]
            vwin = v_ref[pl.ds(off, KW), :]
            outs = []
            for hh in range(2):
                qm = jnp.where(half == hh, q, 0.0).astype(BF16)
                sw = _dot_nt(qm, kwin) + b_ref[hh, 0]
                sc = _dot_nt(qm, kctx)
                m = jnp.maximum(jnp.max(sw, axis=-1, keepdims=True), jnp.max(sc, axis=-1, keepdims=True))
                pw = jnp.exp(sw - m)
                pc = jnp.exp(sc - m)
                l = jnp.sum(pw, axis=-1, keepdims=True) + jnp.sum(pc, axis=-1, keepdims=True)
                outs.append((_dot(pw.astype(BF16), vwin) + _dot(pc.astype(BF16), vctx)) / l)
                lse_ref[hh] = jnp.broadcast_to(m + jnp.log(l), (TM, LANES))
            y_ref[...] = jnp.where(lo, outs[0], outs[1]).astype(BF16)

        @pl.when(blk >= n_lat_blk)
        def _():
            outs = []
            for hh in range(2):
                qm = jnp.where(half == hh, q, 0.0).astype(BF16)
                sc = _dot_nt(qm, kctx)
                m = jnp.max(sc, axis=-1, keepdims=True)
                pc = jnp.exp(sc - m)
                l = jnp.sum(pc, axis=-1, keepdims=True)
                outs.append(_dot(pc.astype(BF16), vctx) / l)
                lse_ref[hh] = jnp.broadcast_to(m + jnp.log(l), (TM, LANES))
            y_ref[...] = jnp.where(lo, outs[0], outs[1]).astype(BF16)

    return pl.pallas_call(
        body, name="na_fwd", grid=(npair, nb),
        in_specs=[pl.BlockSpec((TM, LANES), lambda p, b: (b, p)),
                  pl.BlockSpec((T, LANES), lambda p, b: (0, npair + p)),
                  pl.BlockSpec((T, LANES), lambda p, b: (0, 2 * npair + p)),
                  pl.BlockSpec((2, 1, TM, KW), lambda p, b: (p, _na_variant(b, n_lat_blk), 0, 0))],
        out_specs=[pl.BlockSpec((TM, LANES), lambda p, b: (b, p)),
                   pl.BlockSpec((2, TM, LANES), lambda p, b: (p, b, 0))],
        out_shape=[jax.ShapeDtypeStruct((T, NA_HEADS * HEAD_DIM), BF16),
                   jax.ShapeDtypeStruct((NA_HEADS, T, LANES), F32)],
        compiler_params=_cp(("parallel", "parallel"), VMEM_BIG),
    )(naqkv, naqkv, naqkv, bias)


def na_bwd_call(naqkv, bias, ya, dya, lse, n_lat):
    T = naqkv.shape[0]
    nb = T // TM
    n_lat_blk = n_lat // TM
    rows = n_lat // GRID_W
    n_ctx = T - n_lat
    KW = NA_KROWS * GRID_W
    npair = NA_HEADS // 2

    def body(q_ref, k_ref, v_ref, b_ref, y_ref, dy_ref, lse_ref, dq_ref, dk_ref, dv_ref, db_ref, dk_sc, dv_sc):
        blk = pl.program_id(1)
        half = _lane_half((TM, LANES))
        lo = half == 0
        q = q_ref[...].astype(F32) * SCALE
        kctx = k_ref[pl.ds(n_lat, n_ctx), :]
        vctx = v_ref[pl.ds(n_lat, n_ctx), :]
        dyv = dy_ref[...].astype(F32)
        yv = y_ref[...].astype(F32)

        @pl.when(blk == 0)
        def _():
            dk_sc[...] = jnp.zeros((T, LANES), F32)
            dv_sc[...] = jnp.zeros((T, LANES), F32)

        @pl.when(blk < n_lat_blk)
        def _():
            off = _na_kstart(blk, rows)
            kwin = k_ref[pl.ds(off, KW), :]
            vwin = v_ref[pl.ds(off, KW), :]
            first = (blk == 0) | (blk == 1) | (blk == n_lat_blk - 1)
            dqs = []
            for hh in range(2):
                sel = half == hh
                qm = jnp.where(sel, q, 0.0).astype(BF16)
                dom = jnp.where(sel, dyv, 0.0)
                delta = jnp.sum(dom * yv, axis=-1, keepdims=True)
                dom = dom.astype(BF16)
                lse_v = lse_ref[hh][:, 0:1]
                pw = jnp.exp(_dot_nt(qm, kwin) + b_ref[hh, 0] - lse_v)
                pc = jnp.exp(_dot_nt(qm, kctx) - lse_v)
                dsw = pw * (_dot_nt(dom, vwin) - delta)
                dsc = pc * (_dot_nt(dom, vctx) - delta)

                @pl.when(first)
                def _():
                    db_ref[hh, 0] = dsw

                @pl.when(jnp.logical_not(first))
                def _():
                    db_ref[hh, 0] += dsw

                dsw = dsw.astype(BF16)
                dsc = dsc.astype(BF16)
                dv_sc[pl.ds(off, KW), :] += _dot_tn(pw.astype(BF16), dom)
                dv_sc[pl.ds(n_lat, n_ctx), :] += _dot_tn(pc.astype(BF16), dom)
                dk_sc[pl.ds(off, KW), :] += _dot_tn(dsw, qm)
                dk_sc[pl.ds(n_lat, n_ctx), :] += _dot_tn(dsc, qm)
                dqs.append((_dot(dsw, kwin) + _dot(dsc, kctx)) * SCALE)
            dq_ref[...] = jnp.where(lo, dqs[0], dqs[1]).astype(BF16)

        @pl.when(blk >= n_lat_blk)
        def _():
            dqs = []
            for hh in range(2):
                sel = half == hh
                qm = jnp.where(sel, q, 0.0).astype(BF16)
                dom = jnp.where(sel, dyv, 0.0)
                delta = jnp.sum(dom * yv, axis=-1, keepdims=True)
                dom = dom.astype(BF16)
                pc = jnp.exp(_dot_nt(qm, kctx) - lse_ref[hh][:, 0:1])
                dsc = (pc * (_dot_nt(dom, vctx) - delta)).astype(BF16)
                dv_sc[pl.ds(n_lat, n_ctx), :] += _dot_tn(pc.astype(BF16), dom)
                dk_sc[pl.ds(n_lat, n_ctx), :] += _dot_tn(dsc, qm)
                dqs.append(_dot(dsc, kctx) * SCALE)
            dq_ref[...] = jnp.where(lo, dqs[0], dqs[1]).astype(BF16)

        @pl.when(blk == nb - 1)
        def _():
            dk_ref[...] = dk_sc[...].astype(BF16)
            dv_ref[...] = dv_sc[...].astype(BF16)

    blkspec = pl.BlockSpec((TM, LANES), lambda p, b: (b, p))
    col = pl.BlockSpec((T, LANES), lambda p, b: (0, p))
    bspec = pl.BlockSpec((2, 1, TM, KW), lambda p, b: (p, _na_variant(b, n_lat_blk), 0, 0))
    dq, dk, dv, db = pl.pallas_call(
        body, name="na_bwd", grid=(npair, nb),
        in_specs=[blkspec,
                  pl.BlockSpec((T, LANES), lambda p, b: (0, npair + p)),
                  pl.BlockSpec((T, LANES), lambda p, b: (0, 2 * npair + p)),
                  bspec, blkspec, blkspec,
                  pl.BlockSpec((2, TM, LANES), lambda p, b: (p, b, 0))],
        out_specs=[blkspec, col, col, bspec],
        out_shape=[jax.ShapeDtypeStruct((T, NA_HEADS * HEAD_DIM), BF16)] * 3
        + [jax.ShapeDtypeStruct((NA_HEADS, 3, TM, KW), F32)],
        scratch_shapes=[pltpu.VMEM((T, LANES), F32), pltpu.VMEM((T, LANES), F32)],
        compiler_params=_cp(("arbitrary", "arbitrary"), VMEM_BIG),
    )(naqkv, naqkv, naqkv, bias, ya, dya, lse)
    return jnp.concatenate([dq, dk, dv], axis=1), db


def rope_tables(n_lat, n_ctx):
    t = jnp.arange(n_lat)
    row = (t // GRID_W).astype(F32)
    col = (t % GRID_W).astype(F32)
    half = HEAD_DIM // 2
    inv = ROPE_THETA ** (-jnp.arange(0, half, 2, dtype=F32) / half)
    ang = jnp.concatenate([row[:, None] * inv, col[:, None] * inv], axis=-1)
    cos = jnp.repeat(jnp.cos(ang), 2, axis=-1)
    sin = jnp.repeat(jnp.sin(ang), 2, axis=-1) * jnp.tile(jnp.array([-1.0, 1.0], F32), half)
    cos = jnp.concatenate([cos, jnp.ones((n_ctx, HEAD_DIM), F32)], axis=0)
    sin = jnp.concatenate([sin, jnp.zeros((n_ctx, HEAD_DIM), F32)], axis=0)
    return jnp.tile(cos, (1, 2)), jnp.tile(sin, (1, 2))


def _row(v):
    return v.reshape(1, -1)


def device_fwd_bwd(xs, target, modv, w, small, n_lat, ex=None):
    T, D = xs.shape
    n_ctx = T - n_lat
    L = modv.shape[0]
    nlt = n_lat // TM
    rows = n_lat // GRID_W
    cos_t, sin_t = rope_tables(n_lat, n_ctx)
    qg = [jnp.tile(_row(small["q_gain"][l]), (1, GQA_Q_HEADS)) for l in range(L)]
    kg = [jnp.tile(_row(small["k_gain"][l]), (1, GQA_KV_HEADS)) for l in range(L)]
    bias = [na_bias_table(small["na_rpb"][l], rows) for l in range(L)]

    later = {}

    def wt(name, l):
        if ex is None or (name == "w_in" and l == 0):
            return w[name][l]
        return later[name][l - 1] if name == "w_in" else later[name][l]

    saved = []
    x = xs
    h = norm_mod_call(x, _row(small["norm1"][0]), modv[0], nlt, 0)
    for l in range(L):
        naqkv, gqkv, gates = qkv_call(h, wt("w_in", l))
        ya, lse_a = na_fwd_call(naqkv, bias[l], n_lat)
        qp, kp = gqa_prep_call(gqkv, qg[l], kg[l], cos_t, sin_t)
        small_scores = gqa_scores_small(small["q_gain"][l], small["k_gain"][l])
        if ex is not None and l == 0:
            yb, lse_b, *gathered = gqa_fwd_call(qp, kp, gqkv, n_lat, small_scores, gather=ex.rest())
            w_in_later, others = ex.rest_whole(gathered)
            later.update(w_in_later)
            later.update(others)
        else:
            yb, lse_b = gqa_fwd_call(qp, kp, gqkv, n_lat, small_scores)
        merged, pa, pb = merge_call(ya, yb, gates, wt("w_pa", l), wt("w_pb", l))
        x1, osave, h2 = wo_call(merged, wt("w_o", l), x, modv[l], _row(small["norm2"][l]), nlt)
        act, a, u = ffn_in_call(h2, wt("w_ffn_in", l))
        if l + 1 < L:
            x2, fsave, hn = ffn_out_call(act, wt("w_ffn_out", l), x1, modv[l], nlt,
                                         _row(small["norm1"][l + 1]), modv[l + 1])
        else:
            x2, fsave = ffn_out_call(act, wt("w_ffn_out", l), x1, modv[l], nlt)
            hn = None
        saved.append(dict(x=x, h=h, naqkv=naqkv, gqkv=gqkv, gates=gates, ya=ya, lse_a=lse_a, qp=qp, kp=kp,
                          yb=yb, lse_b=lse_b, merged=merged, pa=pa, pb=pb, x1=x1, osave=osave, h2=h2,
                          act=act, a=a, u=u, fsave=fsave))
        x, h = x2, hn

    dx, d_final, loss_part = loss_call(x, _row(small["final_norm"]), target, nlt)

    gw = {k: [None] * L for k in ("w_in", "w_pa", "w_pb", "w_o", "w_ffn_in", "w_ffn_out")}
    gs = {k: [None] * L for k in ("norm1", "norm2", "na_rpb", "q_gain", "k_gain")}
    dmod = [None] * L
    fold = lambda s: s.sum(axis=1)
    for l in reversed(range(L)):
        s = saved[l]
        da, du, df, s_g2 = dact_call(dx, modv[l], s["fsave"], wt("w_ffn_out", l), s["a"], s["u"], nlt)
        gw["w_ffn_out"][l] = wgrad_call(s["act"], df, "wgrad_ffn_out")
        dx1, s_sh2, s_sc2, s_n2 = dh2_call(da, du, wt("w_ffn_in", l), s["x1"], modv[l], _row(small["norm2"][l]), dx, nlt)
        gw["w_ffn_in"][l] = jnp.concatenate(
            [wgrad_call(s["h2"], da, "wgrad_ffn_in_a"), wgrad_call(s["h2"], du, "wgrad_ffn_in_u")], axis=1)
        dpa, dpb, dgates, do, s_g1 = dmerged_call(dx1, modv[l], s["osave"], wt("w_o", l), s["pa"], s["pb"], s["gates"], nlt)
        gw["w_o"][l] = wgrad_call(s["merged"], do, "wgrad_o")
        dya, dyb = dy_call(dpa, dpb, wt("w_pa", l), wt("w_pb", l))
        gw["w_pa"][l] = wgrad_call(s["ya"], dpa, "wgrad_pa")
        gw["w_pb"][l] = wgrad_call(s["yb"], dpb, "wgrad_pb")
        dna, dbias = na_bwd_call(s["naqkv"], bias[l], s["ya"], dya, s["lse_a"], n_lat)
        gs["na_rpb"][l] = na_bias_grad(dbias, rows)
        if ex is not None:
            ex.pack_grads(l, {k: v[l] for k, v in gw.items() if k != "w_in"}, [gi for gi in range(len(GROUPS)) if gi != G_IN])
            keys = [(l, gi) for gi in range(len(GROUPS)) if gi != G_IN] + ([(l + 1, G_IN)] if l + 1 < L else [])
            dqp, dkp, dv, *got = gqa_bwd_call(s["qp"], s["kp"], s["gqkv"], s["yb"], dyb, s["lse_b"], n_lat,
                                              exchange=[ex.partials[k] for k in keys])
            ex.received.update(zip(keys, got))
        else:
            dqp, dkp, dv = gqa_bwd_call(s["qp"], s["kp"], s["gqkv"], s["yb"], dyb, s["lse_b"], n_lat)
        dg, s_qg, s_kg = gqa_prep_bwd_call(s["gqkv"], qg[l], kg[l], cos_t, sin_t, dqp, dkp, dv)
        gs["q_gain"][l] = s_qg.reshape(8 * GQA_Q_HEADS, HEAD_DIM).sum(axis=0)
        gs["k_gain"][l] = s_kg.reshape(8 * GQA_KV_HEADS, HEAD_DIM).sum(axis=0)
        dx, s_sh1, s_sc1, s_n1 = dh_call(dna, dg, dgates, wt("w_in", l), s["x"], modv[l], _row(small["norm1"][l]), dx1, nlt)
        gw["w_in"][l] = jnp.concatenate(
            [wgrad_call(s["h"], dna, "wgrad_in_na"), wgrad_call(s["h"], dg, "wgrad_in_g"),
             wgrad_call(s["h"], dgates, "wgrad_in_gate")], axis=1)
        gs["norm1"][l] = fold(s_n1).sum(axis=0)
        gs["norm2"][l] = fold(s_n2).sum(axis=0)
        dmod[l] = jnp.stack([s_sh1, s_sc1, s_g1, s_sh2, s_sc2, s_g2], axis=1).sum(axis=2)
        if ex is not None:
            ex.pack_grads(l, {"w_in": gw["w_in"][l]}, [G_IN])

    if ex is not None:
        ex.received[(0, G_IN)] = grads_exchange([ex.partials[(0, G_IN)]])[0]
        gw = None
    else:
        gw = {k: jnp.stack(v) for k, v in gw.items()}
    gs = {k: jnp.stack(v) for k, v in gs.items()}
    gs["final_norm"] = d_final.sum(axis=(0, 1))
    return loss_part, dx, gw, jnp.stack(dmod), gs


N_DEV = 8
N_CHIP = 4
ANY = pl.BlockSpec(memory_space=pl.ANY)


def _place():
    x, y, c = lax.axis_index("x"), lax.axis_index("y"), lax.axis_index("c")
    chips = [(1 - x, y), (x, 1 - y), (1 - x, 1 - y)]
    return x, y, c, chips


def small_all_gather(v, name):
    m_per, n = v.shape
    assert m_per % 8 == 0

    def body(x_ref, out_ref, send_sems, recv_sems, local_sem):
        x, y, c, chips = _place()
        me, sibling = (x, y, c), (x, y, 1 - c)

        def rows(px, py, pc):
            return out_ref.at[pl.ds(pl.multiple_of((4 * px + 2 * py + pc) * m_per, 8), m_per), :]

        def copy(k, block, to, src=None):
            return pltpu.make_async_remote_copy(
                src_ref=rows(*block) if src is None else src, dst_ref=rows(*block),
                send_sem=send_sems.at[k], recv_sem=recv_sems.at[k], device_id=to, device_id_type=MESH)

        mine = pltpu.make_async_copy(x_ref, rows(*me), local_sem)
        mine.start()
        first = [copy(0, me, sibling, src=x_ref)]
        first += [copy(1 + j, me, (*chip, c), src=x_ref) for j, chip in enumerate(chips)]
        for cp in first:
            cp.start()
        passed = [copy(4 + j, (*chip, c), sibling) for j, chip in enumerate(chips)]
        for j, chip in enumerate(chips):
            copy(1 + j, (*chip, c), me).wait_recv()
            passed[j].start()
        copy(0, sibling, me).wait_recv()
        for j, chip in enumerate(chips):
            copy(4 + j, (*chip, 1 - c), me).wait_recv()
        for cp in first + passed:
            cp.wait_send()
        mine.wait()

    out = pl.pallas_call(
        body, name=name,
        out_shape=jax.ShapeDtypeStruct((N_DEV * m_per, n), v.dtype),
        in_specs=[pl.BlockSpec(memory_space=pltpu.VMEM)],
        out_specs=pl.BlockSpec(memory_space=pltpu.VMEM),
        scratch_shapes=[pltpu.SemaphoreType.DMA((7,)), pltpu.SemaphoreType.DMA((7,)), pltpu.SemaphoreType.DMA],
    )(v)
    return out.reshape(N_DEV, m_per, n)


def _ag_copies(in_ref, out_ref, send_sems, recv_sems, base=0):
    x, y, c, chips = _place()
    me, sibling = (x, y, c), (x, y, 1 - c)

    def blk(px, py, half):
        return out_ref.at[2 * px + py, half]

    def copy(k, block, to, src=None):
        return pltpu.make_async_remote_copy(
            src_ref=blk(*block) if src is None else src, dst_ref=blk(*block),
            send_sem=send_sems.at[base + k], recv_sem=recv_sems.at[base + k], device_id=to, device_id_type=MESH)

    first = [copy(j, (x, y, c), (*chip, c), src=in_ref.at[c]) for j, chip in enumerate(chips)]
    passed = [copy(3 + j, (*chip, c), sibling) for j, chip in enumerate(chips)]
    arrivals = [copy(j, (*chip, c), me) for j, chip in enumerate(chips)]
    forwarded = [copy(3 + j, (*chip, 1 - c), me) for j, chip in enumerate(chips)]
    return first, passed, arrivals, forwarded


AG_SEMS = 6
RS_SEMS = 7


def _ag_start(in_refs, out_refs, send_sems, recv_sems):
    for b, (i, o) in enumerate(zip(in_refs, out_refs)):
        for cp in _ag_copies(i, o, send_sems, recv_sems, AG_SEMS * b)[0]:
            cp.start()


def _ag_forward(in_refs, out_refs, send_sems, recv_sems):
    for b, (i, o) in enumerate(zip(in_refs, out_refs)):
        _, passed, arrivals, _ = _ag_copies(i, o, send_sems, recv_sems, AG_SEMS * b)
        for arrived, onward in zip(arrivals, passed):
            arrived.wait_recv()
            onward.start()


def _ag_finish(in_refs, out_refs, send_sems, recv_sems):
    for b, (i, o) in enumerate(zip(in_refs, out_refs)):
        first, passed, _, forwarded = _ag_copies(i, o, send_sems, recv_sems, AG_SEMS * b)
        for cp in forwarded:
            cp.wait_recv()
        for cp in first + passed:
            cp.wait_send()


def place_own_block(gathered, packed):
    chip = 2 * lax.axis_index("x") + lax.axis_index("y")
    return lax.dynamic_update_slice(gathered, packed[None], (chip,) + (0,) * packed.ndim)


def weights_all_gather(packed):
    n = len(packed)

    def body(*refs):
        _ag_start(refs[:n], refs[n:2 * n], refs[2 * n], refs[2 * n + 1])
        _ag_forward(refs[:n], refs[n:2 * n], refs[2 * n], refs[2 * n + 1])
        _ag_finish(refs[:n], refs[n:2 * n], refs[2 * n], refs[2 * n + 1])

    outs = pl.pallas_call(
        body, name="weights_all_gather",
        out_shape=[jax.ShapeDtypeStruct((N_CHIP,) + p.shape, p.dtype) for p in packed],
        in_specs=[ANY] * n, out_specs=[ANY] * n,
        scratch_shapes=[pltpu.SemaphoreType.DMA((AG_SEMS * n,)), pltpu.SemaphoreType.DMA((AG_SEMS * n,))],
    )(*packed)
    return [place_own_block(o, p) for o, p in zip(outs, packed)]


def _rs_copies(gp_ref, rx_ref, send_sems, recv_sems, base=0):
    x, y, c, chips = _place()
    sends = [pltpu.make_async_remote_copy(src_ref=gp_ref.at[2 * x + y, 1 - c], dst_ref=rx_ref.at[0],
                                          send_sem=send_sems.at[base], recv_sem=recv_sems.at[base],
                                          device_id=(x, y, 1 - c), device_id_type=MESH)]
    for r, (px, py) in enumerate(chips):
        for h in range(2):
            sends.append(pltpu.make_async_remote_copy(
                src_ref=gp_ref.at[2 * px + py, h], dst_ref=rx_ref.at[1 + 2 * r + c],
                send_sem=send_sems.at[base + 1 + 2 * r + h], recv_sem=recv_sems.at[base + 1 + 2 * r + c],
                device_id=(px, py, h), device_id_type=MESH))
    arrivals = [pltpu.make_async_remote_copy(src_ref=gp_ref.at[0, 0], dst_ref=rx_ref.at[s],
                                             send_sem=send_sems.at[base + s], recv_sem=recv_sems.at[base + s],
                                             device_id=(x, y, c), device_id_type=MESH) for s in range(RS_SEMS)]
    return sends, arrivals


def _rs_start(gp_refs, rx_refs, send_sems, recv_sems):
    for b, (gp, rx) in enumerate(zip(gp_refs, rx_refs)):
        for cp in _rs_copies(gp, rx, send_sems, recv_sems, RS_SEMS * b)[0]:
            cp.start()


def _rs_finish(gp_refs, rx_refs, send_sems, recv_sems):
    for b, (gp, rx) in enumerate(zip(gp_refs, rx_refs)):
        sends, arrivals = _rs_copies(gp, rx, send_sems, recv_sems, RS_SEMS * b)
        for cp in arrivals:
            cp.wait_recv()
        for cp in sends:
            cp.wait_send()


def _rs_out_shape(gp):
    return jax.ShapeDtypeStruct((RS_SEMS,) + gp.shape[2:], gp.dtype)


def grads_exchange(gps):
    n = len(gps)

    def body(*refs):
        _rs_start(refs[:n], refs[n:2 * n], refs[2 * n], refs[2 * n + 1])
        _rs_finish(refs[:n], refs[n:2 * n], refs[2 * n], refs[2 * n + 1])

    return pl.pallas_call(
        body, name="grads_exchange", out_shape=[_rs_out_shape(g) for g in gps],
        in_specs=[ANY] * n, out_specs=[ANY] * n,
        scratch_shapes=[pltpu.SemaphoreType.DMA((RS_SEMS * n,)), pltpu.SemaphoreType.DMA((RS_SEMS * n,))],
    )(*gps)


def grads_join_halves(ghs):
    n = len(ghs)

    def body(*refs):
        x, y, c, _ = _place()
        cps = [pltpu.make_async_remote_copy(src_ref=refs[b], dst_ref=refs[n + b], send_sem=refs[2 * n].at[b],
                                            recv_sem=refs[2 * n + 1].at[b], device_id=(x, y, 1 - c),
                                            device_id_type=MESH) for b in range(n)]
        for cp in cps:
            cp.start()
        for cp in cps:
            cp.wait()

    others = pl.pallas_call(
        body, name="grads_join_halves", out_shape=[jax.ShapeDtypeStruct(g.shape, g.dtype) for g in ghs],
        in_specs=[ANY] * n, out_specs=[ANY] * n,
        scratch_shapes=[pltpu.SemaphoreType.DMA((n,)), pltpu.SemaphoreType.DMA((n,))],
    )(*ghs)
    c0 = lax.axis_index("c") == 0
    return [jnp.stack([jnp.where(c0, g, o), jnp.where(c0, o, g)], axis=1) for g, o in zip(ghs, others)]


PACK_C = 1024


def sum_partials_call(gp, rx, chip, core):
    _, _, rh, ncol = gp.shape
    n = rx.shape[0]
    tr = _pick_tile(rh, 512, 16)

    def body(where_ref, a_ref, *rest):
        o_ref = rest[-1]
        g = a_ref[0, 0].astype(F32)
        for b_ref in rest[:-1]:
            g = g + b_ref[0].astype(F32)
        o_ref[...] = g

    other = lambda j: pl.BlockSpec((1, tr, ncol), lambda i, w: (j, i, 0))
    return pl.pallas_call(
        body, name="sum_partials",
        grid_spec=pltpu.PrefetchScalarGridSpec(
            num_scalar_prefetch=1, grid=(rh // tr,),
            in_specs=[pl.BlockSpec((1, 1, tr, ncol), lambda i, w: (w[0], w[1], i, 0))] + [other(j) for j in range(n)],
            out_specs=pl.BlockSpec((tr, ncol), lambda i, w: (i, 0))),
        out_shape=jax.ShapeDtypeStruct((rh, ncol), F32),
        compiler_params=_cp(("parallel",)),
    )(jnp.stack([chip, core]).astype(jnp.int32), gp, *([rx] * n))


def _silu(x):
    return x * _sigmoid(x)


def mod_matmul_call(craw, w_mod):
    L, D, N = w_mod.shape

    def body(c_ref, w_ref, o_ref):
        o_ref[0] = _dot(_silu(c_ref[...]).astype(BF16), w_ref[0].astype(BF16))

    return pl.pallas_call(
        body, name="mod_matmul", grid=(L,),
        in_specs=[pl.BlockSpec((16, D), lambda l: (0, 0)), pl.BlockSpec((1, D, N), lambda l: (l, 0, 0))],
        out_specs=pl.BlockSpec((1, 16, N), lambda l: (l, 0, 0)),
        out_shape=jax.ShapeDtypeStruct((L, 16, N), F32),
        compiler_params=_cp(("parallel",), VMEM_BIG),
    )(craw, w_mod)


def mod_bwd_call(craw, dmod16, w_mod):
    L, D, N = w_mod.shape

    def body(c_ref, d_ref, w_ref, gw_ref, ds_ref):
        l = pl.program_id(0)
        d = d_ref[0].astype(BF16)
        gw_ref[0] = _dot_tn(_silu(c_ref[...]).astype(BF16), d)
        part = _dot_nt(d, w_ref[0].astype(BF16))

        @pl.when(l == 0)
        def _():
            ds_ref[...] = part

        @pl.when(l > 0)
        def _():
            ds_ref[...] += part

    return pl.pallas_call(
        body, name="mod_bwd", grid=(L,),
        in_specs=[pl.BlockSpec((16, D), lambda l: (0, 0)), pl.BlockSpec((1, 16, N), lambda l: (l, 0, 0)),
                  pl.BlockSpec((1, D, N), lambda l: (l, 0, 0))],
        out_specs=[pl.BlockSpec((1, D, N), lambda l: (l, 0, 0)), pl.BlockSpec((16, D), lambda l: (0, 0))],
        out_shape=[jax.ShapeDtypeStruct((L, D, N), F32), jax.ShapeDtypeStruct((16, D), F32)],
        compiler_params=_cp(("arbitrary",), VMEM_BIG),
    )(craw, dmod16, w_mod)


def _adamw(w, g, m, v):
    m2 = ADAM_B1 * m + (1.0 - ADAM_B1) * g
    v2 = ADAM_B2 * v + (1.0 - ADAM_B2) * (g * g)
    m_hat = m2 / (1.0 - ADAM_B1 ** ADAM_STEP)
    v_hat = v2 / (1.0 - ADAM_B2 ** ADAM_STEP)
    delta = -ADAM_LR * (m_hat / (jnp.sqrt(v_hat) + ADAM_EPS) + ADAM_WD * w)
    return delta, m2, v2


def adamw_call(w, g, m, v):
    shape = w.shape
    ncol = shape[-1]
    r = math.prod(shape[:-1])
    tr = _pick_tile(r, 512, 8)
    as2d = lambda t: t.reshape(r, ncol)

    def body(w_ref, g_ref, m_ref, v_ref, d_ref, m2_ref, v2_ref):
        d_ref[...], m2_ref[...], v2_ref[...] = _adamw(w_ref[...], g_ref[...], m_ref[...], v_ref[...])

    spec = pl.BlockSpec((tr, ncol), lambda i: (i, 0))
    outs = pl.pallas_call(
        body, name="adamw", grid=(r // tr,), in_specs=[spec] * 4, out_specs=[spec] * 3,
        out_shape=[jax.ShapeDtypeStruct((r, ncol), F32)] * 3,
        compiler_params=_cp(("parallel",)),
    )(as2d(w), as2d(g), as2d(m), as2d(v))
    return [o.reshape(shape) for o in outs]


def adamw_small_call(parts, w, m, v, silu_bwd=False):
    P, R, C = parts.shape

    def body(p_ref, w_ref, m_ref, v_ref, g_ref, d_ref, m2_ref, v2_ref):
        g = p_ref[0]
        for i in range(1, P):
            g = g + p_ref[i]
        wv = w_ref[...]
        if silu_bwd:
            sg = _sigmoid(wv)
            g = g * (sg * (1.0 + wv * (1.0 - sg)))
        g_ref[...] = g
        d_ref[...], m2_ref[...], v2_ref[...] = _adamw(wv, g, m_ref[...], v_ref[...])

    return pl.pallas_call(
        body, name="adamw_small", out_shape=[jax.ShapeDtypeStruct((R, C), F32)] * 4,
    )(parts, w, m, v)


def sum_parts_call(parts):
    P, R, C = parts.shape

    def body(p_ref, o_ref):
        g = p_ref[0]
        for i in range(1, P):
            g = g + p_ref[i]
        o_ref[...] = g

    return pl.pallas_call(body, name="sum_parts", out_shape=jax.ShapeDtypeStruct((R, C), F32))(parts)


BIG = (("w_in", "col"), ("w_pa", "col"), ("w_pb", "col"), ("w_o", "row"), ("w_ffn_in", "col"), ("w_ffn_out", "row"))
KIND = dict(BIG)
GROUPS = (("w_in",), ("w_ffn_in",), ("w_pa", "w_pb", "w_o", "w_ffn_out"))
G_IN = 0
SMALL = ("norm1", "norm2", "final_norm", "na_rpb", "q_gain", "k_gain")


def _group_width(group, shapes):
    return shapes[group[0]][1] if len(group) == 1 else PACK_C


def _pack_shards(shards, group, shapes):
    width = _group_width(group, shapes)
    lead = shards[group[0]].shape[:-2]
    p = jnp.concatenate([shards[n].reshape(lead + (-1, width)) for n in group], axis=-2)
    return p.reshape(lead + (2, p.shape[-2] // 2, width))


def _unpack_shards(packed, group, shapes):
    width = _group_width(group, shapes)
    lead = packed.shape[:-3]
    flat = packed.reshape(lead + (-1, width))
    out, r0 = {}, 0
    for n in group:
        K, N = shapes[n]
        rw = K * N // width
        out[n] = flat[..., r0:r0 + rw, :].reshape(lead + (K, N))
        r0 += rw
    return out


def _whole_from_chips(g, kind):
    _, L, K, N = g.shape
    if kind == "col":
        return g.transpose(1, 2, 0, 3).reshape(L, K, N_CHIP * N)
    return g.transpose(1, 0, 2, 3).reshape(L, N_CHIP * K, N)


def _chips_from_whole(g, kind):
    K, N = g.shape
    if kind == "col":
        return g.reshape(K, N_CHIP, N // N_CHIP).transpose(1, 0, 2)
    return g.reshape(N_CHIP, K // N_CHIP, N)


class Exchange:
    def __init__(self, local_bf16, shard_shapes):
        self.shapes = shard_shapes
        self.packed = [_pack_shards(local_bf16, g, shard_shapes).transpose(1, 0, 2, 3) for g in GROUPS]
        self.received = {}
        self.partials = {}

    def whole(self, gathered, group):
        g = _unpack_shards(gathered.transpose(0, 2, 1, 3, 4), group, self.shapes)
        return {n: _whole_from_chips(g[n], KIND[n]) for n in group}

    def first_needed(self):
        return self.whole(weights_all_gather([self.packed[G_IN][:, :1]])[0], GROUPS[G_IN])

    def rest(self):
        return [p[:, 1:] if gi == G_IN else p for gi, p in enumerate(self.packed)]

    def rest_whole(self, gathered):
        out = [self.whole(place_own_block(g, p), grp) for g, p, grp in zip(gathered, self.rest(), GROUPS)]
        others = {}
        for gi, d in enumerate(out):
            if gi != G_IN:
                others.update(d)
        return out[G_IN], others

    def pack_grads(self, layer, gw, groups):
        for gi in groups:
            shards = {n: _chips_from_whole(gw[n], KIND[n]) for n in GROUPS[gi]}
            self.partials[(layer, gi)] = _pack_shards(shards, GROUPS[gi], self.shapes)


def _pad_rows(v, rows):
    return jnp.pad(v.reshape(-1), (0, rows * PACK_C - v.size)).reshape(rows, PACK_C)


def kernel(x, c, ctx, c_ctx, w_mod, b_mod, norm1, w_in, na_rpb, q_gain, k_gain, w_pa, w_pb, w_o, norm2, w_ffn_in, w_ffn_out, final_norm, loss_target, m_c_ctx, m_w_mod, m_b_mod, m_norm1, m_w_in, m_na_rpb, m_q_gain, m_k_gain, m_w_pa, m_w_pb, m_w_o, m_norm2, m_w_ffn_in, m_w_ffn_out, m_final_norm, v_c_ctx, v_w_mod, v_b_mod, v_norm1, v_w_in, v_na_rpb, v_q_gain, v_k_gain, v_w_pa, v_w_pb, v_w_o, v_norm2, v_w_ffn_in, v_w_ffn_out, v_final_norm):
    W = dict(c_ctx=c_ctx, w_mod=w_mod, b_mod=b_mod, norm1=norm1, w_in=w_in, na_rpb=na_rpb, q_gain=q_gain, k_gain=k_gain,
             w_pa=w_pa, w_pb=w_pb, w_o=w_o, norm2=norm2, w_ffn_in=w_ffn_in, w_ffn_out=w_ffn_out, final_norm=final_norm)
    M = dict(c_ctx=m_c_ctx, w_mod=m_w_mod, b_mod=m_b_mod, norm1=m_norm1, w_in=m_w_in, na_rpb=m_na_rpb, q_gain=m_q_gain,
             k_gain=m_k_gain, w_pa=m_w_pa, w_pb=m_w_pb, w_o=m_w_o, norm2=m_norm2, w_ffn_in=m_w_ffn_in,
             w_ffn_out=m_w_ffn_out, final_norm=m_final_norm)
    V = dict(c_ctx=v_c_ctx, w_mod=v_w_mod, b_mod=v_b_mod, norm1=v_norm1, w_in=v_w_in, na_rpb=v_na_rpb, q_gain=v_q_gain,
             k_gain=v_k_gain, w_pa=v_w_pa, w_pb=v_w_pb, w_o=v_w_o, norm2=v_norm2, w_ffn_in=v_w_ffn_in,
             w_ffn_out=v_w_ffn_out, final_norm=v_final_norm)
    order = ["c_ctx", "w_mod", "b_mod", "norm1", "w_in", "na_rpb", "q_gain", "k_gain", "w_pa", "w_pb", "w_o", "norm2",
             "w_ffn_in", "w_ffn_out", "final_norm"]
    L, D = norm1.shape
    n_lat = x.shape[1]
    ax, ay, ac = lax.axis_index("x"), lax.axis_index("y"), lax.axis_index("c")
    chip = 2 * ax + ay
    dev = 2 * chip + ac

    shard_shapes = {n: W[n].shape[1:] for n, _ in BIG}
    ex = Exchange({n: W[n].astype(BF16) for n, _ in BIG}, shard_shapes)
    w_first = ex.first_needed()

    c_all = small_all_gather(jnp.pad(c, ((0, 7), (0, 0))), "gather_c")[:, 0, :]
    craw = jnp.pad(c_all, ((0, 8), (0, 0))) + jnp.pad(c_ctx[None, :], ((8, 7), (0, 0)))
    ncol_mod = w_mod.shape[2]
    mod_loc = mod_matmul_call(craw, w_mod)
    mod_all = small_all_gather(mod_loc.reshape(L * 16, ncol_mod), "gather_mod")
    mod_all = mod_all[0::2].reshape(N_CHIP, L, 16, ncol_mod).transpose(1, 2, 0, 3).reshape(L, 16, 6 * D)
    mod_all = mod_all + b_mod[:, None, :]
    mod_mine = lax.dynamic_index_in_dim(mod_all, dev, axis=1, keepdims=False)
    modv = jnp.stack([mod_mine, mod_all[:, 8]], axis=1).reshape(L, 2, 6, D)

    xs = jnp.concatenate([x[0], ctx[0]], axis=0)
    small = dict(norm1=norm1, norm2=norm2, na_rpb=na_rpb, q_gain=q_gain, k_gain=k_gain, final_norm=final_norm)
    loss_part, dx, _, dmod, gs = device_fwd_bwd(xs, loss_target[0], modv, w_first, small, n_lat, ex)
    loss = lax.psum(jnp.sum(loss_part), ("x", "y", "c"))
    grad_x = dx[:n_lat][None]

    gh = [jnp.stack([sum_partials_call(ex.partials[(l, gi)], ex.received[(l, gi)], chip, ac) for l in range(L)])
          for gi in range(len(GROUPS))]
    gbig = {}
    for grp, joined in zip(GROUPS, grads_join_halves(gh)):
        gbig.update(_unpack_shards(joined, grp, shard_shapes))

    n_mod_rows = L * 6 * D // PACK_C
    small_flat = jnp.concatenate([gs[n].reshape(-1) for n in SMALL])
    n_small_rows = -(-small_flat.size // (8 * PACK_C)) * 8
    g1 = jnp.concatenate([dmod[:, 0].reshape(n_mod_rows, PACK_C), dmod[:, 1].reshape(n_mod_rows, PACK_C),
                          _pad_rows(small_flat, n_small_rows)], axis=0)
    g1 = small_all_gather(g1, "gather_small_grads")
    dm_lat = g1[:, :n_mod_rows].reshape(N_DEV, L, 6 * D)
    dm_ctx_parts = g1[:, n_mod_rows:2 * n_mod_rows]
    dm_ctx = sum_parts_call(dm_ctx_parts).reshape(L, 1, 6 * D)
    dmod16 = (jnp.pad(dm_lat.transpose(1, 0, 2), ((0, 0), (0, 8), (0, 0)))
              + jnp.pad(dm_ctx, ((0, 0), (8, 7), (0, 0))))
    dmod16_loc = lax.dynamic_slice_in_dim(dmod16, chip * ncol_mod, ncol_mod, axis=2)
    g_wmod, dsilu_part = mod_bwd_call(craw, dmod16_loc, w_mod)
    ds_all = small_all_gather(dsilu_part[8:16], "gather_dsilu")
    ds_parts = ds_all[0::2, 0:1, :]
    ds_parts = jnp.pad(ds_parts, ((0, 0), (0, 7), (0, 0)))

    out_g, out_d, out_m, out_v = {}, {}, {}, {}
    for n, _ in BIG:
        out_g[n] = gbig[n]
        out_d[n], out_m[n], out_v[n] = adamw_call(W[n], gbig[n], M[n], V[n])
    out_g["w_mod"] = g_wmod
    out_d["w_mod"], out_m["w_mod"], out_v["w_mod"] = adamw_call(w_mod, g_wmod, m_w_mod, v_w_mod)

    pack_small = lambda t: _pad_rows(jnp.concatenate([t[n].reshape(-1) for n in SMALL]), n_small_rows)
    res = adamw_small_call(g1[:, 2 * n_mod_rows:], pack_small(W), pack_small(M), pack_small(V))
    off = 0
    for n in SMALL:
        sz = W[n].size
        for dst, r in zip((out_g, out_d, out_m, out_v), res):
            dst[n] = r.reshape(-1)[off:off + sz].reshape(W[n].shape)
        off += sz
    bparts = jnp.concatenate([g1[:, :n_mod_rows], dm_ctx_parts], axis=0)
    as_rows = lambda t: t.reshape(n_mod_rows, PACK_C)
    res = adamw_small_call(bparts, as_rows(b_mod), as_rows(m_b_mod), as_rows(v_b_mod))
    for dst, r in zip((out_g, out_d, out_m, out_v), res):
        dst["b_mod"] = r.reshape(b_mod.shape)
    row8 = lambda t: jnp.pad(t[None, :], ((0, 7), (0, 0)))
    res = adamw_small_call(ds_parts, row8(c_ctx), row8(m_c_ctx), row8(v_c_ctx), silu_bwd=True)
    for dst, r in zip((out_g, out_d, out_m, out_v), res):
        dst["c_ctx"] = r[0]

    return (loss, grad_x, *[out_g[n] for n in order], *[out_d[n] for n in order],
            *[out_m[n] for n in order], *[out_v[n] for n in order])
```

```python
import math

import numpy as np
import jax
import jax.numpy as jnp
from jax import lax
from jax.experimental import pallas as pl
from jax.experimental.pallas import tpu as pltpu

F32 = jnp.float32
BF16 = jnp.bfloat16
MESH = pl.DeviceIdType.MESH

HEAD_DIM = 64
NA_HEADS = 8
GQA_Q_HEADS = 8
GQA_KV_HEADS = 2
GRID_W = 64
NA_WIN_H = 8
NA_WIN_W = 16
ROPE_THETA = 10000.0
EPS = 1e-6
SCALE = HEAD_DIM ** -0.5
NEG = -1e30

ADAM_LR = 0.001
ADAM_B1 = 0.9
ADAM_B2 = 0.999
ADAM_EPS = 1e-08
ADAM_WD = 0.01
ADAM_STEP = 10

TM = 256
NA_ROWS = 4
NA_KROWS = 12
LANES = 128
VMEM_BIG = 56 * 1024 * 1024


def _cp(sem, vmem=None):
    return pltpu.CompilerParams(dimension_semantics=sem, vmem_limit_bytes=vmem)


def _dot(a, b):
    return jnp.dot(a, b, preferred_element_type=F32)


def _dot_nt(a, b):
    return lax.dot_general(a, b, (((1,), (1,)), ((), ())), preferred_element_type=F32)


def _dot_tn(a, b):
    return lax.dot_general(a, b, (((0,), (0,)), ((), ())), preferred_element_type=F32)


def _colsum8(v):
    tm, d = v.shape
    return v.reshape(tm // 8, 8, d).sum(axis=0)


def _acc(ref, val, first):
    @pl.when(first)
    def _():
        ref[0] = val

    @pl.when(jnp.logical_not(first))
    def _():
        ref[0] += val


def _rms_r(x):
    return lax.rsqrt(jnp.mean(x * x, axis=-1, keepdims=True) + EPS)


def _norm_mod(x, nw, sh, sc):
    return (x * _rms_r(x) * nw) * (1.0 + sc) + sh


def _norm_mod_bwd(dh, x, nw, sc):
    r = _rms_r(x)
    xn = x * r
    dxn = dh * (nw * (1.0 + sc))
    dx = r * (dxn - xn * jnp.mean(dxn * xn, axis=-1, keepdims=True))
    return dx, dh, dh * (xn * nw), dh * ((1.0 + sc) * xn)


def _sigmoid(x):
    return 1.0 / (1.0 + jnp.exp(-x))


def _group_of(i, n_lat_tiles):
    return jnp.where(i >= n_lat_tiles, 1, 0)


def _lane_half(shape):
    return (lax.broadcasted_iota(jnp.int32, shape, len(shape) - 1) % LANES) // HEAD_DIM


def _lane_lt64(shape):
    return _lane_half(shape) == 0


def _roll64(x):
    return pltpu.roll(x, HEAD_DIM, x.ndim - 1)


def norm_mod_call(x, nw, modv, n_lat_tiles, which):
    T, D = x.shape

    def body(x_ref, nw_ref, mod_ref, h_ref):
        sh = mod_ref[0, 3 * which:3 * which + 1, :]
        sc = mod_ref[0, 3 * which + 1:3 * which + 2, :]
        h_ref[...] = _norm_mod(x_ref[...], nw_ref[...], sh, sc).astype(BF16)

    return pl.pallas_call(
        body, name="norm_mod", grid=(T // TM,),
        in_specs=[pl.BlockSpec((TM, D), lambda i: (i, 0)),
                  pl.BlockSpec((1, D), lambda i: (0, 0)),
                  pl.BlockSpec((1, 6, D), lambda i: (_group_of(i, n_lat_tiles), 0, 0))],
        out_specs=pl.BlockSpec((TM, D), lambda i: (i, 0)),
        out_shape=jax.ShapeDtypeStruct((T, D), BF16),
        compiler_params=_cp(("parallel",)),
    )(x, nw, modv)


def qkv_call(h, w_in):
    T, D = h.shape
    N = w_in.shape[1]
    n_na = 3 * NA_HEADS * HEAD_DIM
    n_g = (GQA_Q_HEADS + 2 * GQA_KV_HEADS) * HEAD_DIM
    n_gate = N - n_na - n_g

    def body(h_ref, w_ref, na_ref, g_ref, gate_ref):
        acc = _dot(h_ref[...], w_ref[...])
        na_ref[...] = acc[:, :n_na].astype(BF16)
        g_ref[...] = acc[:, n_na:n_na + n_g].astype(BF16)
        gate_ref[...] = acc[:, n_na + n_g:].astype(BF16)

    return pl.pallas_call(
        body, name="qkv", grid=(T // TM,),
        in_specs=[pl.BlockSpec((TM, D), lambda i: (i, 0)),
                  pl.BlockSpec((D, N), lambda i: (0, 0))],
        out_specs=[pl.BlockSpec((TM, n_na), lambda i: (i, 0)),
                   pl.BlockSpec((TM, n_g), lambda i: (i, 0)),
                   pl.BlockSpec((TM, n_gate), lambda i: (i, 0))],
        out_shape=[jax.ShapeDtypeStruct((T, n_na), BF16),
                   jax.ShapeDtypeStruct((T, n_g), BF16),
                   jax.ShapeDtypeStruct((T, n_gate), BF16)],
        compiler_params=_cp(("parallel",), VMEM_BIG),
    )(h, w_in)


def merge_call(ya, yb, gates, w_pa, w_pb):
    T, Ka = ya.shape
    D = w_pa.shape[1]

    def body(ya_ref, yb_ref, ga_ref, gb_ref, wa_ref, wb_ref, m_ref, pa_ref, pb_ref):
        pa = _dot(ya_ref[...], wa_ref[...])
        pb = _dot(yb_ref[...], wb_ref[...])
        m = _sigmoid(ga_ref[...].astype(F32)) * pa + _sigmoid(gb_ref[...].astype(F32)) * pb
        m_ref[...] = m.astype(BF16)
        pa_ref[...] = pa.astype(BF16)
        pb_ref[...] = pb.astype(BF16)

    row = lambda w: pl.BlockSpec((TM, w), lambda i: (i, 0))
    return pl.pallas_call(
        body, name="merge", grid=(T // TM,),
        in_specs=[row(Ka), row(Ka),
                  pl.BlockSpec((TM, D), lambda i: (i, 0)), pl.BlockSpec((TM, D), lambda i: (i, 1)),
                  pl.BlockSpec((Ka, D), lambda i: (0, 0)), pl.BlockSpec((Ka, D), lambda i: (0, 0))],
        out_specs=[row(D), row(D), row(D)],
        out_shape=[jax.ShapeDtypeStruct((T, D), BF16)] * 3,
        compiler_params=_cp(("parallel",)),
    )(ya, yb, gates, gates, w_pa, w_pb)


def wo_call(merged, w_o, x, modv, nw2, n_lat_tiles):
    T, D = x.shape

    def body(m_ref, w_ref, x_ref, mod_ref, nw_ref, x1_ref, o_ref, h2_ref):
        o = _dot(m_ref[...], w_ref[...])
        x1 = x_ref[...] + mod_ref[0, 2:3, :] * o
        x1_ref[...] = x1
        o_ref[...] = o.astype(BF16)
        h2_ref[...] = _norm_mod(x1, nw_ref[...], mod_ref[0, 3:4, :], mod_ref[0, 4:5, :]).astype(BF16)

    row = pl.BlockSpec((TM, D), lambda i: (i, 0))
    return pl.pallas_call(
        body, name="wo", grid=(T // TM,),
        in_specs=[row, pl.BlockSpec((D, D), lambda i: (0, 0)), row,
                  pl.BlockSpec((1, 6, D), lambda i: (_group_of(i, n_lat_tiles), 0, 0)),
                  pl.BlockSpec((1, D), lambda i: (0, 0))],
        out_specs=[row, row, row],
        out_shape=[jax.ShapeDtypeStruct((T, D), F32), jax.ShapeDtypeStruct((T, D), BF16),
                   jax.ShapeDtypeStruct((T, D), BF16)],
        compiler_params=_cp(("parallel",)),
    )(merged, w_o, x, modv, nw2)


def ffn_in_call(h2, w_ffn_in, n_col_tiles=2):
    T, D = h2.shape
    F = w_ffn_in.shape[1] // 2
    tn = F // n_col_tiles

    def body(h_ref, wa_ref, wu_ref, act_ref, a_ref, u_ref):
        h = h_ref[...]
        a = _dot(h, wa_ref[...])
        u = _dot(h, wu_ref[...])
        act_ref[...] = (a * _sigmoid(a) * u).astype(BF16)
        a_ref[...] = a.astype(BF16)
        u_ref[...] = u.astype(BF16)

    blk = pl.BlockSpec((TM, tn), lambda j, i: (i, j))
    return pl.pallas_call(
        body, name="ffn_in", grid=(n_col_tiles, T // TM),
        in_specs=[pl.BlockSpec((TM, D), lambda j, i: (i, 0)),
                  pl.BlockSpec((D, tn), lambda j, i: (0, j)),
                  pl.BlockSpec((D, tn), lambda j, i: (0, n_col_tiles + j))],
        out_specs=[blk, blk, blk],
        out_shape=[jax.ShapeDtypeStruct((T, F), BF16)] * 3,
        compiler_params=_cp(("parallel", "parallel"), VMEM_BIG),
    )(h2, w_ffn_in, w_ffn_in)


def ffn_out_call(act, w_ffn_out, x1, modv, n_lat_tiles, next_nw=None, next_modv=None):
    T, D = x1.shape
    F = act.shape[1]
    with_next = next_nw is not None

    def body(*refs):
        if with_next:
            a_ref, w_ref, x_ref, mod_ref, nw_ref, nmod_ref, x2_ref, f_ref, hn_ref = refs
        else:
            a_ref, w_ref, x_ref, mod_ref, x2_ref, f_ref = refs
        f = _dot(a_ref[...], w_ref[...])
        x2 = x_ref[...] + mod_ref[0, 5:6, :] * f
        x2_ref[...] = x2
        f_ref[...] = f.astype(BF16)
        if with_next:
            hn_ref[...] = _norm_mod(x2, nw_ref[...], nmod_ref[0, 0:1, :], nmod_ref[0, 1:2, :]).astype(BF16)

    row = pl.BlockSpec((TM, D), lambda i: (i, 0))
    modspec = pl.BlockSpec((1, 6, D), lambda i: (_group_of(i, n_lat_tiles), 0, 0))
    in_specs = [pl.BlockSpec((TM, F), lambda i: (i, 0)), pl.BlockSpec((F, D), lambda i: (0, 0)), row, modspec]
    args = [act, w_ffn_out, x1, modv]
    out_specs = [row, row]
    out_shape = [jax.ShapeDtypeStruct((T, D), F32), jax.ShapeDtypeStruct((T, D), BF16)]
    if with_next:
        in_specs += [pl.BlockSpec((1, D), lambda i: (0, 0)), modspec]
        args += [next_nw, next_modv]
        out_specs.append(row)
        out_shape.append(jax.ShapeDtypeStruct((T, D), BF16))
    return pl.pallas_call(
        body, name="ffn_out", grid=(T // TM,), in_specs=in_specs, out_specs=out_specs, out_shape=out_shape,
        compiler_params=_cp(("parallel",), VMEM_BIG),
    )(*args)


def loss_call(x, nw, target, n_lat_tiles):
    T, D = x.shape

    def body(x_ref, nw_ref, t_ref, dx_ref, dw_ref, ls_ref):
        i = pl.program_id(0)

        @pl.when(i < n_lat_tiles)
        def _():
            xv = x_ref[...]
            nw_v = nw_ref[...]
            r = _rms_r(xv)
            xn = xv * r
            e = xn * nw_v - t_ref[...]
            dy = e * (1.0 / D)
            dxn = dy * nw_v
            dx_ref[...] = r * (dxn - xn * jnp.mean(dxn * xn, axis=-1, keepdims=True))
            _acc(dw_ref, _colsum8(dy * xn), i == 0)
            _acc(ls_ref, _colsum8(e * e * (0.5 / D)), i == 0)

        @pl.when(i >= n_lat_tiles)
        def _():
            dx_ref[...] = jnp.zeros((TM, D), F32)

    stat = pl.BlockSpec((1, 8, D), lambda i: (0, 0, 0))
    return pl.pallas_call(
        body, name="loss", grid=(T // TM,),
        in_specs=[pl.BlockSpec((TM, D), lambda i: (i, 0)), pl.BlockSpec((1, D), lambda i: (0, 0)),
                  pl.BlockSpec((TM, D), lambda i: (jnp.minimum(i, n_lat_tiles - 1), 0))],
        out_specs=[pl.BlockSpec((TM, D), lambda i: (i, 0)), stat, stat],
        out_shape=[jax.ShapeDtypeStruct((T, D), F32), jax.ShapeDtypeStruct((1, 8, D), F32),
                   jax.ShapeDtypeStruct((1, 8, D), F32)],
        compiler_params=_cp(("arbitrary",)),
    )(x, nw, target)


def dact_call(dx2, modv, fsave, w_ffn_out, a, u, n_lat_tiles):
    T, D = dx2.shape
    F = a.shape[1]

    def body(dx_ref, mod_ref, f_ref, w_ref, a_ref, u_ref, da_ref, du_ref, df_ref, dg_ref):
        i = pl.program_id(0)
        dx = dx_ref[...]
        df = (dx * mod_ref[0, 5:6, :]).astype(BF16)
        df_ref[...] = df
        dact = _dot_nt(df, w_ref[...])
        av = a_ref[...].astype(F32)
        uv = u_ref[...].astype(F32)
        sg = _sigmoid(av)
        da_ref[...] = (dact * uv * (sg * (1.0 + av * (1.0 - sg)))).astype(BF16)
        du_ref[...] = (dact * (av * sg)).astype(BF16)
        _acc(dg_ref, _colsum8(dx * f_ref[...].astype(F32)), (i == 0) | (i == n_lat_tiles))

    row = pl.BlockSpec((TM, D), lambda i: (i, 0))
    wide = pl.BlockSpec((TM, F), lambda i: (i, 0))
    grp = lambda i: (_group_of(i, n_lat_tiles), 0, 0)
    return pl.pallas_call(
        body, name="dact", grid=(T // TM,),
        in_specs=[row, pl.BlockSpec((1, 6, D), grp), row, pl.BlockSpec((F, D), lambda i: (0, 0)), wide, wide],
        out_specs=[wide, wide, row, pl.BlockSpec((1, 8, D), grp)],
        out_shape=[jax.ShapeDtypeStruct((T, F), BF16), jax.ShapeDtypeStruct((T, F), BF16),
                   jax.ShapeDtypeStruct((T, D), BF16), jax.ShapeDtypeStruct((2, 8, D), F32)],
        compiler_params=_cp(("arbitrary",), VMEM_BIG),
    )(dx2, modv, fsave, w_ffn_out, a, u)


def _norm_bwd_tail(dh, x_ref, mod_ref, nw_ref, dres_ref, dx_ref, s_sh, s_sc, s_w, which, first):
    sc = mod_ref[0, 3 * which + 1:3 * which + 2, :]
    dxn, t_sh, t_sc, t_w = _norm_mod_bwd(dh, x_ref[...], nw_ref[...], sc)
    dx_ref[...] = dres_ref[...] + dxn
    _acc(s_sh, _colsum8(t_sh), first)
    _acc(s_sc, _colsum8(t_sc), first)
    _acc(s_w, _colsum8(t_w), first)


def dh2_call(da, du, w_ffn_in, x1, modv, nw2, dx2, n_lat_tiles):
    T, D = x1.shape
    F = da.shape[1]

    def body(da_ref, du_ref, wa_ref, wu_ref, x_ref, mod_ref, nw_ref, dres_ref, dx_ref, s_sh, s_sc, s_w):
        i = pl.program_id(0)
        dh = _dot_nt(da_ref[...], wa_ref[...]) + _dot_nt(du_ref[...], wu_ref[...])
        _norm_bwd_tail(dh, x_ref, mod_ref, nw_ref, dres_ref, dx_ref, s_sh, s_sc, s_w, 1,
                       (i == 0) | (i == n_lat_tiles))

    row = pl.BlockSpec((TM, D), lambda i: (i, 0))
    wide = pl.BlockSpec((TM, F), lambda i: (i, 0))
    grp = lambda i: (_group_of(i, n_lat_tiles), 0, 0)
    stat = pl.BlockSpec((1, 8, D), grp)
    return pl.pallas_call(
        body, name="dh2", grid=(T // TM,),
        in_specs=[wide, wide, pl.BlockSpec((D, F), lambda i: (0, 0)), pl.BlockSpec((D, F), lambda i: (0, 1)),
                  row, pl.BlockSpec((1, 6, D), grp), pl.BlockSpec((1, D), lambda i: (0, 0)), row],
        out_specs=[row, stat, stat, stat],
        out_shape=[jax.ShapeDtypeStruct((T, D), F32)] + [jax.ShapeDtypeStruct((2, 8, D), F32)] * 3,
        compiler_params=_cp(("arbitrary",), VMEM_BIG),
    )(da, du, w_ffn_in, w_ffn_in, x1, modv, nw2, dx2)


def dmerged_call(dx1, modv, osave, w_o, pa, pb, gates, n_lat_tiles):
    T, D = dx1.shape

    def body(dx_ref, mod_ref, o_ref, w_ref, pa_ref, pb_ref, ga_ref, gb_ref,
             dpa_ref, dpb_ref, dgate_ref, do_ref, dg_ref):
        i = pl.program_id(0)
        dx = dx_ref[...]
        do = (dx * mod_ref[0, 2:3, :]).astype(BF16)
        do_ref[...] = do
        dm = _dot_nt(do, w_ref[...])
        sa = _sigmoid(ga_ref[...].astype(F32))
        sb = _sigmoid(gb_ref[...].astype(F32))
        dpa_ref[...] = (dm * sa).astype(BF16)
        dpb_ref[...] = (dm * sb).astype(BF16)
        dgate_ref[:, :D] = (dm * pa_ref[...].astype(F32) * (sa * (1.0 - sa))).astype(BF16)
        dgate_ref[:, D:] = (dm * pb_ref[...].astype(F32) * (sb * (1.0 - sb))).astype(BF16)
        _acc(dg_ref, _colsum8(dx * o_ref[...].astype(F32)), (i == 0) | (i == n_lat_tiles))

    row = pl.BlockSpec((TM, D), lambda i: (i, 0))
    grp = lambda i: (_group_of(i, n_lat_tiles), 0, 0)
    return pl.pallas_call(
        body, name="dmerged", grid=(T // TM,),
        in_specs=[row, pl.BlockSpec((1, 6, D), grp), row, pl.BlockSpec((D, D), lambda i: (0, 0)), row, row,
                  pl.BlockSpec((TM, D), lambda i: (i, 0)), pl.BlockSpec((TM, D), lambda i: (i, 1))],
        out_specs=[row, row, pl.BlockSpec((TM, 2 * D), lambda i: (i, 0)), row, pl.BlockSpec((1, 8, D), grp)],
        out_shape=[jax.ShapeDtypeStruct((T, D), BF16), jax.ShapeDtypeStruct((T, D), BF16),
                   jax.ShapeDtypeStruct((T, 2 * D), BF16), jax.ShapeDtypeStruct((T, D), BF16),
                   jax.ShapeDtypeStruct((2, 8, D), F32)],
        compiler_params=_cp(("arbitrary",)),
    )(dx1, modv, osave, w_o, pa, pb, gates, gates)


def dy_call(dpa, dpb, w_pa, w_pb):
    T, D = dpa.shape
    K = w_pa.shape[0]

    def body(a_ref, b_ref, wa_ref, wb_ref, ya_ref, yb_ref):
        ya_ref[...] = _dot_nt(a_ref[...], wa_ref[...]).astype(BF16)
        yb_ref[...] = _dot_nt(b_ref[...], wb_ref[...]).astype(BF16)

    row = pl.BlockSpec((TM, D), lambda i: (i, 0))
    w = pl.BlockSpec((K, D), lambda i: (0, 0))
    out = pl.BlockSpec((TM, K), lambda i: (i, 0))
    return pl.pallas_call(
        body, name="dy", grid=(T // TM,), in_specs=[row, row, w, w], out_specs=[out, out],
        out_shape=[jax.ShapeDtypeStruct((T, K), BF16)] * 2,
        compiler_params=_cp(("parallel",)),
    )(dpa, dpb, w_pa, w_pb)


def dh_call(dna, dg, dgates, w_in, x, modv, nw1, dx1, n_lat_tiles):
    T, D = x.shape
    n1, n2, n3 = dna.shape[1], dg.shape[1], dgates.shape[1]
    N = n1 + n2 + n3

    def body(a_ref, b_ref, c_ref, w_ref, x_ref, mod_ref, nw_ref, dres_ref, dx_ref, s_sh, s_sc, s_w):
        i = pl.program_id(0)
        dh = (_dot_nt(a_ref[...], w_ref[:, :n1]) + _dot_nt(b_ref[...], w_ref[:, n1:n1 + n2])
              + _dot_nt(c_ref[...], w_ref[:, n1 + n2:]))
        _norm_bwd_tail(dh, x_ref, mod_ref, nw_ref, dres_ref, dx_ref, s_sh, s_sc, s_w, 0,
                       (i == 0) | (i == n_lat_tiles))

    row = pl.BlockSpec((TM, D), lambda i: (i, 0))
    grp = lambda i: (_group_of(i, n_lat_tiles), 0, 0)
    stat = pl.BlockSpec((1, 8, D), grp)
    return pl.pallas_call(
        body, name="dh", grid=(T // TM,),
        in_specs=[pl.BlockSpec((TM, n1), lambda i: (i, 0)), pl.BlockSpec((TM, n2), lambda i: (i, 0)),
                  pl.BlockSpec((TM, n3), lambda i: (i, 0)), pl.BlockSpec((D, N), lambda i: (0, 0)),
                  row, pl.BlockSpec((1, 6, D), grp), pl.BlockSpec((1, D), lambda i: (0, 0)), row],
        out_specs=[row, stat, stat, stat],
        out_shape=[jax.ShapeDtypeStruct((T, D), F32)] + [jax.ShapeDtypeStruct((2, 8, D), F32)] * 3,
        compiler_params=_cp(("arbitrary",), VMEM_BIG),
    )(dna, dg, dgates, w_in, x, modv, nw1, dx1)


def _pick_tile(n, cap, mult):
    best = None
    for d in range(mult, min(n, cap) + 1, mult):
        if n % d == 0:
            best = d
    assert best is not None, (n, cap, mult)
    return best


def wgrad_call(a, b, name):
    T, K = a.shape
    N = b.shape[1]
    tt = _pick_tile(T, 768, 128)
    tk = K if K <= 1024 else _pick_tile(K, 1408, 128)
    tn = N if N <= 2304 else _pick_tile(N, 2304, 128)
    nt = T // tt

    def body(a_ref, b_ref, o_ref, acc_ref):
        t = pl.program_id(2)
        p = _dot_tn(a_ref[...], b_ref[...])

        @pl.when(t == 0)
        def _():
            acc_ref[...] = p

        @pl.when(t > 0)
        def _():
            acc_ref[...] += p

        @pl.when(t == nt - 1)
        def _():
            o_ref[...] = acc_ref[...].astype(BF16)

    return pl.pallas_call(
        body, name=name, grid=(K // tk, N // tn, nt),
        in_specs=[pl.BlockSpec((tt, tk), lambda i, j, t: (t, i)), pl.BlockSpec((tt, tn), lambda i, j, t: (t, j))],
        out_specs=pl.BlockSpec((tk, tn), lambda i, j, t: (i, j)),
        out_shape=jax.ShapeDtypeStruct((K, N), BF16),
        scratch_shapes=[pltpu.VMEM((tk, tn), F32)],
        compiler_params=_cp(("parallel", "parallel", "arbitrary"), VMEM_BIG),
    )(a, b)


def _head_mean(v):
    n = v.shape[-1]
    r = lax.broadcasted_iota(jnp.int32, (n, n), 0) // HEAD_DIM
    c = lax.broadcasted_iota(jnp.int32, (n, n), 1) // HEAD_DIM
    bd = jnp.where(r == c, 1.0 / HEAD_DIM, 0.0).astype(BF16)
    hi = v.astype(BF16)
    lo = (v - hi.astype(F32)).astype(BF16)
    return _dot(hi, bd) + _dot(lo, bd)


def _swap_pairs(x):
    n = x.shape[-1]
    even = lax.broadcasted_iota(jnp.int32, x.shape, x.ndim - 1) % 2 == 0
    return jnp.where(even, pltpu.roll(x, n - 1, x.ndim - 1), pltpu.roll(x, 1, x.ndim - 1))


def _tile_lanes(t, n):
    return jnp.tile(t, (1, n // t.shape[-1]))


def gqa_prep_call(gqkv, qgain, kgain, cos_t, sin_t):
    T = gqkv.shape[0]
    nq = GQA_Q_HEADS * HEAD_DIM
    nk = GQA_KV_HEADS * HEAD_DIM
    rep = GQA_Q_HEADS // GQA_KV_HEADS

    def body(g_ref, qg_ref, kg_ref, c_ref, s_ref, qp_ref, kp_ref):
        cos_v, sin_v = c_ref[...], s_ref[...]
        xq = g_ref[:, :nq].astype(F32)
        yq = xq * lax.rsqrt(_head_mean(xq * xq) + EPS) * qg_ref[...]
        yq = (yq * _tile_lanes(cos_v, nq) + _swap_pairs(yq) * _tile_lanes(sin_v, nq)) * SCALE
        lo = _lane_lt64((TM, LANES))
        for p in range(GQA_Q_HEADS // 2):
            g = (2 * p) // rep
            chunk = yq[:, LANES * p:LANES * (p + 1)]
            rolled = _roll64(chunk)
            if g == 0:
                first, second = jnp.where(lo, chunk, 0.0), jnp.where(lo, rolled, 0.0)
            else:
                first, second = jnp.where(lo, 0.0, rolled), jnp.where(lo, 0.0, chunk)
            qp_ref[:, 2 * LANES * p:2 * LANES * p + LANES] = first.astype(BF16)
            qp_ref[:, 2 * LANES * p + LANES:2 * LANES * (p + 1)] = second.astype(BF16)
        xk = g_ref[:, nq:nq + nk].astype(F32)
        yk = xk * lax.rsqrt(_head_mean(xk * xk) + EPS) * kg_ref[...]
        kp_ref[...] = (yk * cos_v + _swap_pairs(yk) * sin_v).astype(BF16)

    return pl.pallas_call(
        body, name="gqa_prep", grid=(T // TM,),
        in_specs=[pl.BlockSpec((TM, gqkv.shape[1]), lambda i: (i, 0)),
                  pl.BlockSpec((1, nq), lambda i: (0, 0)), pl.BlockSpec((1, nk), lambda i: (0, 0)),
                  pl.BlockSpec((TM, nk), lambda i: (i, 0)), pl.BlockSpec((TM, nk), lambda i: (i, 0))],
        out_specs=[pl.BlockSpec((TM, GQA_Q_HEADS * LANES), lambda i: (i, 0)), pl.BlockSpec((TM, nk), lambda i: (i, 0))],
        out_shape=[jax.ShapeDtypeStruct((T, GQA_Q_HEADS * LANES), BF16), jax.ShapeDtypeStruct((T, nk), BF16)],
        compiler_params=_cp(("parallel",)),
    )(gqkv, qgain, kgain, cos_t, sin_t)


def gqa_prep_bwd_call(gqkv, qgain, kgain, cos_t, sin_t, dqp, dkp, dv):
    T = gqkv.shape[0]
    nq = GQA_Q_HEADS * HEAD_DIM
    nk = GQA_KV_HEADS * HEAD_DIM
    rep = GQA_Q_HEADS // GQA_KV_HEADS

    def norm_rope_bwd(x, gain, dy, cos_v, sin_v):
        dyn = dy * cos_v - _swap_pairs(dy) * sin_v
        r = lax.rsqrt(_head_mean(x * x) + EPS)
        xn = x * r
        dxn = dyn * gain
        dx = r * (dxn - xn * _head_mean(dxn * xn))
        return dx, dyn * xn

    def body(g_ref, qg_ref, kg_ref, c_ref, s_ref, dqp_ref, dkp_ref, dv_ref, dg_ref, sq_ref, sk_ref):
        i = pl.program_id(0)
        cos_v, sin_v = c_ref[...], s_ref[...]
        lo = _lane_lt64((TM, LANES))
        chunks = []
        for p in range(GQA_Q_HEADS // 2):
            g = (2 * p) // rep
            da = dqp_ref[:, 2 * LANES * p:2 * LANES * p + LANES]
            db = dqp_ref[:, 2 * LANES * p + LANES:2 * LANES * (p + 1)]
            if g == 0:
                chunks.append(jnp.where(lo, da, _roll64(db)))
            else:
                chunks.append(jnp.where(lo, _roll64(da), db))
        dyq = jnp.concatenate(chunks, axis=1) * SCALE
        dxq, gq_term = norm_rope_bwd(g_ref[:, :nq].astype(F32), qg_ref[...], dyq,
                                     _tile_lanes(cos_v, nq), _tile_lanes(sin_v, nq))
        dxk, gk_term = norm_rope_bwd(g_ref[:, nq:nq + nk].astype(F32), kg_ref[...], dkp_ref[...], cos_v, sin_v)
        dg_ref[:, :nq] = dxq.astype(BF16)
        dg_ref[:, nq:nq + nk] = dxk.astype(BF16)
        dg_ref[:, nq + nk:] = dv_ref[...].astype(BF16)
        _acc(sq_ref, _colsum8(gq_term), i == 0)
        _acc(sk_ref, _colsum8(gk_term), i == 0)

    W = gqkv.shape[1]
    return pl.pallas_call(
        body, name="gqa_prep_bwd", grid=(T // TM,),
        in_specs=[pl.BlockSpec((TM, W), lambda i: (i, 0)),
                  pl.BlockSpec((1, nq), lambda i: (0, 0)), pl.BlockSpec((1, nk), lambda i: (0, 0)),
                  pl.BlockSpec((TM, nk), lambda i: (i, 0)), pl.BlockSpec((TM, nk), lambda i: (i, 0)),
                  pl.BlockSpec((TM, GQA_Q_HEADS * LANES), lambda i: (i, 0)),
                  pl.BlockSpec((TM, nk), lambda i: (i, 0)), pl.BlockSpec((TM, nk), lambda i: (i, 0))],
        out_specs=[pl.BlockSpec((TM, W), lambda i: (i, 0)),
                   pl.BlockSpec((1, 8, nq), lambda i: (0, 0, 0)), pl.BlockSpec((1, 8, nk), lambda i: (0, 0, 0))],
        out_shape=[jax.ShapeDtypeStruct((T, W), BF16), jax.ShapeDtypeStruct((1, 8, nq), F32),
                   jax.ShapeDtypeStruct((1, 8, nk), F32)],
        compiler_params=_cp(("arbitrary",)),
    )(gqkv, qgain, kgain, cos_t, sin_t, dqp, dkp, dv)


GQ_TQ_LATENT = 1024
GQ_TK = 256
GQ_CHUNKS_PER_TRIP = 8
GQ_FORWARD_STEPS = 1


def _chunk_offset(j):
    return j * GQ_TK if isinstance(j, int) else pl.multiple_of(j * GQ_TK, GQ_TK)


def _gqa_blocks(n_lat, T, latent):
    if latent:
        tq = GQ_TQ_LATENT if n_lat % GQ_TQ_LATENT == 0 else GQ_TK
        return tq, 0, n_lat // tq
    return GQ_TK, n_lat // GQ_TK, (T - n_lat) // GQ_TK


def _sweep_keys(chunk, latent, n_lat_k, nk):
    if latent:
        per_trip = math.gcd(n_lat_k, GQ_CHUNKS_PER_TRIP)

        def trip(jj, carry):
            for t in range(per_trip):
                chunk(per_trip * jj + t)
            return carry

        lax.fori_loop(0, n_lat_k // per_trip, trip, 0)
    for j in range(n_lat_k, nk):
        chunk(j)


def _gqa_place(chunk, half, g):
    gv = jnp.broadcast_to(g, chunk.shape)
    placed = jnp.where(gv == half, chunk, _roll64(chunk))
    return jnp.where(_lane_half(chunk.shape) == gv, placed, jnp.zeros_like(placed))


GQA_SMALL_SCORE = 20.0


def gqa_scores_small(q_gain, k_gain):
    bound = SCALE * HEAD_DIM * jnp.max(jnp.abs(q_gain)) * jnp.max(jnp.abs(k_gain))
    return (bound <= GQA_SMALL_SCORE).astype(jnp.int32).reshape(1)


def gqa_fwd_call(qp, kp, gqkv, n_lat, scores_small, gather=None):
    first = _gqa_fwd_part(qp, kp, gqkv, n_lat, scores_small, True, None, gather)
    y, lse = _gqa_fwd_part(qp, kp, gqkv, n_lat, scores_small, False, (first[0], first[1]), None)
    return (y, lse) + tuple(first[2:])


def _gqa_fwd_part(qp, kp, gqkv, n_lat, scores_small, latent, prev, gather):
    T = qp.shape[0]
    rep = GQA_Q_HEADS // GQA_KV_HEADS
    GQ_TQ, q0, nq = _gqa_blocks(n_lat, T, latent)
    nk = T // GQ_TK
    n_lat_k = n_lat // GQ_TK
    v_blk = (GQA_Q_HEADS + GQA_KV_HEADS) * HEAD_DIM // LANES
    R = rep * GQ_TQ

    def body(*refs):
        if prev is not None:
            small_ref, q_ref, k_ref, v_ref, _, _, y_ref, lse_ref, m_sc, acc_sc = refs
        elif gather is None:
            small_ref, q_ref, k_ref, v_ref, y_ref, lse_ref, m_sc, acc_sc = refs
        else:
            nb = len(gather)
            small_ref, q_ref, k_ref, v_ref = refs[:4]
            pk_refs = refs[4:4 + nb]
            y_ref, lse_ref = refs[4 + nb:6 + nb]
            ga_refs = refs[6 + nb:6 + 2 * nb]
            m_sc, acc_sc, ssem, rsem = refs[6 + 2 * nb:]
        g = pl.program_id(0)
        qb = pl.program_id(1)
        if gather is not None:
            @pl.when((g == 0) & (qb == 0))
            def _():
                _ag_start(pk_refs, ga_refs, ssem, rsem)
        qs = jnp.concatenate([q_ref[:, LANES * i:LANES * (i + 1)] for i in range(rep)], axis=0)
        is_lat = latent
        nsub = GQ_TK // LANES
        scores_small = small_ref[0] == 1

        @pl.when(scores_small)
        def _():
            m_sc[...] = jnp.zeros((R, LANES), F32)

        @pl.when(jnp.logical_not(scores_small))
        def _():
            m_sc[...] = jnp.full((R, LANES), NEG, F32)

            def chunk_max(j):
                s = _dot_nt(qs, k_ref[pl.ds(_chunk_offset(j), GQ_TK), :])
                mm = s[:, :LANES]
                for t in range(1, nsub):
                    mm = jnp.maximum(mm, s[:, LANES * t:LANES * (t + 1)])
                m_sc[...] = jnp.maximum(m_sc[...], mm)

            _sweep_keys(chunk_max, is_lat, n_lat_k, nk)
            m_sc[...] = jnp.broadcast_to(jnp.max(m_sc[...], axis=-1, keepdims=True), (R, LANES))

        acc_sc[...] = jnp.zeros((R, LANES), F32)
        own_v = _lane_half((GQ_TK, LANES)) == jnp.broadcast_to(g, (GQ_TK, LANES))

        def chunk_acc(j):
            off = _chunk_offset(j)
            vc = v_ref[pl.ds(off, GQ_TK), :]
            v1 = jnp.where(own_v, vc, jnp.ones_like(vc))
            s = _dot_nt(qs, k_ref[pl.ds(off, GQ_TK), :])
            mb = m_sc[...]
            p = jnp.concatenate([jnp.exp(s[:, LANES * t:LANES * (t + 1)] - mb).astype(BF16) for t in range(nsub)],
                                axis=1)
            acc_sc[...] += _dot(p, v1)

        _sweep_keys(chunk_acc, is_lat, n_lat_k, nk)
        acc = acc_sc[...]
        own = _lane_half((R, LANES)) == jnp.broadcast_to(g, (R, LANES))
        l = jnp.where(own, _roll64(acc), acc)
        o = acc / l
        lse = m_sc[...] + jnp.log(l)
        lo = _lane_lt64((GQ_TQ, LANES))
        g0 = jnp.broadcast_to(g, (GQ_TQ, LANES)) == 0
        for c2 in range(rep // 2):
            oa = o[(2 * c2) * GQ_TQ:(2 * c2 + 1) * GQ_TQ]
            ob = o[(2 * c2 + 1) * GQ_TQ:(2 * c2 + 2) * GQ_TQ]
            left = jnp.where(g0, oa, _roll64(oa))
            right = jnp.where(g0, _roll64(ob), ob)
            y_ref[:, LANES * c2:LANES * (c2 + 1)] = jnp.where(lo, left, right).astype(BF16)
        for i in range(rep):
            lse_ref[i] = lse[i * GQ_TQ:(i + 1) * GQ_TQ]
        if gather is not None:
            @pl.when((g == GQA_KV_HEADS - 1) & (qb == max(nq - 1 - GQ_FORWARD_STEPS, 0)))
            def _():
                _ag_forward(pk_refs, ga_refs, ssem, rsem)

            @pl.when((g == GQA_KV_HEADS - 1) & (qb == nq - 1))
            def _():
                _ag_finish(pk_refs, ga_refs, ssem, rsem)

    in_specs = [pl.BlockSpec((GQ_TQ, rep * LANES), lambda g, i, s: (q0 + i, g)),
                pl.BlockSpec((T, LANES), lambda g, i, s: (0, 0)),
                pl.BlockSpec((T, LANES), lambda g, i, s: (0, v_blk))]
    out_specs = [pl.BlockSpec((GQ_TQ, rep * HEAD_DIM), lambda g, i, s: (q0 + i, g)),
                 pl.BlockSpec((rep, GQ_TQ, LANES), lambda g, i, s: (g, q0 + i, 0))]
    out_shape = [jax.ShapeDtypeStruct((T, GQA_Q_HEADS * HEAD_DIM), BF16),
                 jax.ShapeDtypeStruct((GQA_Q_HEADS, T, LANES), F32)]
    scratch = [pltpu.VMEM((R, LANES), F32), pltpu.VMEM((R, LANES), F32)]
    args = [scores_small, qp, kp, gqkv]
    aliases = {}
    name = "gqa_fwd" if latent else "gqa_fwd_ctx"
    if prev is not None:
        in_specs += [ANY, ANY]
        args += list(prev)
        aliases = {4: 0, 5: 1}
    if gather is not None:
        in_specs += [ANY] * len(gather)
        out_specs += [ANY] * len(gather)
        out_shape += [jax.ShapeDtypeStruct((N_CHIP,) + p.shape, p.dtype) for p in gather]
        scratch += [pltpu.SemaphoreType.DMA((AG_SEMS * len(gather),)), pltpu.SemaphoreType.DMA((AG_SEMS * len(gather),))]
        args += list(gather)
        name = "gqa_fwd_gather"
    return pl.pallas_call(
        body, name=name,
        grid_spec=pltpu.PrefetchScalarGridSpec(
            num_scalar_prefetch=1, grid=(GQA_KV_HEADS, nq), in_specs=in_specs, out_specs=out_specs,
            scratch_shapes=scratch),
        out_shape=out_shape, input_output_aliases=aliases,
        compiler_params=_cp(("arbitrary", "arbitrary") if gather is not None else ("parallel", "parallel"), VMEM_BIG),
    )(*args)


def gqa_bwd_call(qp, kp, gqkv, yb, dyb, lse, n_lat, exchange=None):
    first = _gqa_bwd_part(qp, kp, gqkv, yb, dyb, lse, n_lat, True, None, exchange)
    dq, dk_c, dv_c = _gqa_bwd_part(qp, kp, gqkv, yb, dyb, lse, n_lat, False, first[0], None)
    return (dq, first[1] + dk_c, first[2] + dv_c) + tuple(first[3:])


def _gqa_bwd_part(qp, kp, gqkv, yb, dyb, lse, n_lat, latent, prev_dq, exchange):
    T = qp.shape[0]
    rep = GQA_Q_HEADS // GQA_KV_HEADS
    GQ_TQ, q0, nq = _gqa_blocks(n_lat, T, latent)
    nk = T // GQ_TK
    n_lat_k = n_lat // GQ_TK
    v_blk = (GQA_Q_HEADS + GQA_KV_HEADS) * HEAD_DIM // LANES
    R = rep * GQ_TQ

    def body(*refs):
        if prev_dq is not None:
            q_ref, k_ref, v_ref, y_ref, dy_ref, lse_ref, _, dq_ref, dk_ref, dv_ref, dq_sc = refs
        elif exchange is None:
            q_ref, k_ref, v_ref, y_ref, dy_ref, lse_ref, dq_ref, dk_ref, dv_ref, dq_sc = refs
        else:
            nb = len(exchange)
            q_ref, k_ref, v_ref, y_ref, dy_ref, lse_ref = refs[:6]
            gp_refs = refs[6:6 + nb]
            dq_ref, dk_ref, dv_ref = refs[6 + nb:9 + nb]
            rx_refs = refs[9 + nb:9 + 2 * nb]
            dq_sc, ssem, rsem = refs[9 + 2 * nb:]
        g = pl.program_id(0)
        qb = pl.program_id(1)
        if exchange is not None:
            @pl.when((g == 0) & (qb == 0))
            def _():
                _rs_start(gp_refs, rx_refs, ssem, rsem)

        @pl.when((g == 0) & (qb == 0))
        def _():
            dk_ref[...] = jnp.zeros((T, LANES), F32)
            dv_ref[...] = jnp.zeros((T, LANES), F32)

        qs = jnp.concatenate([q_ref[:, LANES * i:LANES * (i + 1)] for i in range(rep)], axis=0)
        dos, deltas, lses = [], [], []
        for i in range(rep):
            c2, half = i // 2, i % 2
            dch = _gqa_place(dy_ref[:, LANES * c2:LANES * (c2 + 1)].astype(F32), half, g)
            ych = _gqa_place(y_ref[:, LANES * c2:LANES * (c2 + 1)].astype(F32), half, g)
            dos.append(dch.astype(BF16))
            deltas.append(jnp.sum(dch * ych, axis=-1, keepdims=True))
            lses.append(lse_ref[i])
        do_s = jnp.concatenate(dos, axis=0)
        delta = jnp.broadcast_to(jnp.concatenate(deltas, axis=0), (R, LANES))
        lse_v = jnp.concatenate(lses, axis=0)
        dq_sc[...] = jnp.zeros((R, LANES), F32)
        nsub = GQ_TK // LANES

        def chunk(j):
            off = _chunk_offset(j)
            kc = k_ref[pl.ds(off, GQ_TK), :]
            vc = v_ref[pl.ds(off, GQ_TK), :]
            s = _dot_nt(qs, kc)
            dp = _dot_nt(do_s, vc)
            ps, dss = [], []
            for t in range(nsub):
                pt = jnp.exp(s[:, LANES * t:LANES * (t + 1)] - lse_v)
                ps.append(pt.astype(BF16))
                dss.append((pt * (dp[:, LANES * t:LANES * (t + 1)] - delta)).astype(BF16))
            p = jnp.concatenate(ps, axis=1)
            ds = jnp.concatenate(dss, axis=1)
            dv_ref[pl.ds(off, GQ_TK), :] += _dot_tn(p, do_s)
            dk_ref[pl.ds(off, GQ_TK), :] += _dot_tn(ds, qs)
            dq_sc[...] += _dot(ds, kc)

        _sweep_keys(chunk, latent, n_lat_k, nk)
        for i in range(rep):
            dq_ref[:, LANES * i:LANES * (i + 1)] = dq_sc[i * GQ_TQ:(i + 1) * GQ_TQ, :]
        if exchange is not None:
            @pl.when((g == GQA_KV_HEADS - 1) & (qb == nq - 1))
            def _():
                _rs_finish(gp_refs, rx_refs, ssem, rsem)

    whole = pl.BlockSpec((T, LANES), lambda g, i: (0, 0))
    in_specs = [pl.BlockSpec((GQ_TQ, rep * LANES), lambda g, i: (q0 + i, g)),
                whole,
                pl.BlockSpec((T, LANES), lambda g, i: (0, v_blk)),
                pl.BlockSpec((GQ_TQ, rep * HEAD_DIM), lambda g, i: (q0 + i, g)),
                pl.BlockSpec((GQ_TQ, rep * HEAD_DIM), lambda g, i: (q0 + i, g)),
                pl.BlockSpec((rep, GQ_TQ, LANES), lambda g, i: (g, q0 + i, 0))]
    out_specs = [pl.BlockSpec((GQ_TQ, rep * LANES), lambda g, i: (q0 + i, g)), whole, whole]
    out_shape = [jax.ShapeDtypeStruct((T, GQA_Q_HEADS * LANES), F32),
                 jax.ShapeDtypeStruct((T, LANES), F32), jax.ShapeDtypeStruct((T, LANES), F32)]
    scratch = [pltpu.VMEM((R, LANES), F32)]
    args = [qp, kp, gqkv, yb, dyb, lse]
    aliases = {}
    name = "gqa_bwd" if latent else "gqa_bwd_ctx"
    if prev_dq is not None:
        in_specs.append(ANY)
        args.append(prev_dq)
        aliases = {6: 0}
    if exchange is not None:
        in_specs += [ANY] * len(exchange)
        out_specs += [ANY] * len(exchange)
        out_shape += [_rs_out_shape(g) for g in exchange]
        scratch += [pltpu.SemaphoreType.DMA((RS_SEMS * len(exchange),)), pltpu.SemaphoreType.DMA((RS_SEMS * len(exchange),))]
        args += list(exchange)
        name = "gqa_bwd_exchange"
    return pl.pallas_call(
        body, name=name, grid=(GQA_KV_HEADS, nq),
        in_specs=in_specs, out_specs=out_specs, out_shape=out_shape, scratch_shapes=scratch,
        input_output_aliases=aliases,
        compiler_params=_cp(("arbitrary", "arbitrary"), VMEM_BIG),
    )(*args)


def _na_tables(rows):
    kh = min(NA_WIN_H, rows)
    assert kh == NA_WIN_H and rows >= NA_KROWS and rows % NA_ROWS == 0
    a = np.zeros((3, NA_ROWS, NA_KROWS, 2 * NA_WIN_H - 1), np.float32)
    for v, r0 in enumerate((0, NA_ROWS, rows - NA_ROWS)):
        ks = min(max(r0 - NA_WIN_H // 2, 0), rows - NA_KROWS)
        for rq in range(NA_ROWS):
            r = r0 + rq
            rs = min(max(r - kh // 2, 0), rows - kh)
            for rk in range(NA_KROWS):
                kr = ks + rk
                if rs <= kr < rs + kh:
                    a[v, rq, rk, kr - r + NA_WIN_H - 1] = 1.0
    c = np.zeros((GRID_W, GRID_W, 2 * NA_WIN_W - 1), np.float32)
    for wq in range(GRID_W):
        cs = min(max(wq - NA_WIN_W // 2, 0), GRID_W - NA_WIN_W)
        for wk in range(cs, cs + NA_WIN_W):
            c[wq, wk, min(max(wk - wq, -(NA_WIN_W - 1)), NA_WIN_W - 1) + NA_WIN_W - 1] = 1.0
    return a, c


NA_DR = 2 * NA_WIN_H - 1


def _na_tile_index(rows):
    a, _ = _na_tables(rows)
    idx = np.where(a.sum(-1) > 0, a.argmax(-1), NA_DR).astype(np.int32)
    return jnp.asarray(idx.reshape(-1))


def na_bias_table(rpb, rows):
    _, c = _na_tables(rows)
    H = rpb.shape[0]
    KW = NA_KROWS * GRID_W
    c2 = np.concatenate([c, c], axis=1)
    t1 = jnp.einsum("hde,wue->hdwu", rpb, jnp.asarray(c2), precision=lax.Precision.HIGHEST)
    t1 = jnp.where(jnp.asarray(c2.sum(-1) > 0)[None, None], t1, NEG)
    t1 = jnp.concatenate([t1, jnp.full((H, 1, GRID_W, LANES), NEG, F32)], axis=1)

    def body(idx_ref, t_ref, o_ref):
        v = pl.program_id(1)
        lo = _lane_lt64((GRID_W, LANES))
        for q in range(NA_ROWS):
            for kp in range(NA_KROWS // 2):
                base = (v * NA_ROWS + q) * NA_KROWS + 2 * kp
                tile = jnp.where(lo, t_ref[0, idx_ref[base]], t_ref[0, idx_ref[base + 1]])
                o_ref[0, 0, GRID_W * q:GRID_W * (q + 1), LANES * kp:LANES * (kp + 1)] = tile

    return pl.pallas_call(
        body, name="na_bias_table",
        grid_spec=pltpu.PrefetchScalarGridSpec(
            num_scalar_prefetch=1, grid=(H, 3),
            in_specs=[pl.BlockSpec((1, NA_DR + 1, GRID_W, LANES), lambda h, v, i: (h, 0, 0, 0))],
            out_specs=pl.BlockSpec((1, 1, TM, KW), lambda h, v, i: (h, v, 0, 0))),
        out_shape=jax.ShapeDtypeStruct((H, 3, TM, KW), F32),
        compiler_params=_cp(("parallel", "parallel")),
    )(_na_tile_index(rows), t1)


def na_bias_grad(dbias, rows):
    _, c = _na_tables(rows)
    H = dbias.shape[0]
    KW = NA_KROWS * GRID_W

    def body(idx_ref, d_ref, o_ref):
        v = pl.program_id(1)
        lo = _lane_lt64((GRID_W, LANES))

        @pl.when(v == 0)
        def _():
            o_ref[...] = jnp.zeros((1, NA_DR + 1, GRID_W, LANES), F32)

        for q in range(NA_ROWS):
            for kp in range(NA_KROWS // 2):
                base = (v * NA_ROWS + q) * NA_KROWS + 2 * kp
                tile = d_ref[0, 0, GRID_W * q:GRID_W * (q + 1), LANES * kp:LANES * (kp + 1)]
                o_ref[0, idx_ref[base]] += jnp.where(lo, tile, 0.0)
                o_ref[0, idx_ref[base + 1]] += jnp.where(lo, 0.0, tile)

    dt = pl.pallas_call(
        body, name="na_bias_grad",
        grid_spec=pltpu.PrefetchScalarGridSpec(
            num_scalar_prefetch=1, grid=(H, 3),
            in_specs=[pl.BlockSpec((1, 1, TM, KW), lambda h, v, i: (h, v, 0, 0))],
            out_specs=pl.BlockSpec((1, NA_DR + 1, GRID_W, LANES), lambda h, v, i: (h, 0, 0, 0))),
        out_shape=jax.ShapeDtypeStruct((H, NA_DR + 1, GRID_W, LANES), F32),
        compiler_params=_cp(("parallel", "arbitrary")),
    )(_na_tile_index(rows), dbias)
    dt = dt[:, :NA_DR, :, :GRID_W] + dt[:, :NA_DR, :, GRID_W:]
    return jnp.einsum("hdwu,wue->hde", dt, jnp.asarray(c), precision=lax.Precision.HIGHEST)


def _na_variant(blk, n_lat_blk):
    return jnp.where(blk == 0, 0, jnp.where(blk >= n_lat_blk - 1, 2, 1))


def _na_kstart(blk, rows):
    r0 = blk * NA_ROWS
    ks = jnp.clip(r0 - NA_WIN_H // 2, 0, rows - NA_KROWS)
    return pl.multiple_of(ks * GRID_W, GRID_W)


def na_fwd_call(naqkv, bias, n_lat):
    T = naqkv.shape[0]
    nb = T // TM
    n_lat_blk = n_lat // TM
    rows = n_lat // GRID_W
    n_ctx = T - n_lat
    KW = NA_KROWS * GRID_W
    npair = NA_HEADS // 2

    def body(q_ref, k_ref, v_ref, b_ref, y_ref, lse_ref):
        blk = pl.program_id(1)
        half = _lane_half((TM, LANES))
        lo = half == 0
        q = q_ref[...].astype(F32) * SCALE
        kctx = k_ref[pl.ds(n_lat, n_ctx), :]
        vctx = v_ref[pl.ds(n_lat, n_ctx), :]

        @pl.when(blk < n_lat_blk)
        def _():
            off = _na_kstart(blk, rows)
            kwin = k_ref[pl.ds(off, KW), :]
            vwin = v_ref[pl.ds(off, KW), :]
            outs = []
            for hh in range(2):
                qm = jnp.where(half == hh, q, 0.0).astype(BF16)
                sw = _dot_nt(qm, kwin) + b_ref[hh, 0]
                sc = _dot_nt(qm, kctx)
                m = jnp.maximum(jnp.max(sw, axis=-1, keepdims=True), jnp.max(sc, axis=-1, keepdims=True))
                pw = jnp.exp(sw - m)
                pc = jnp.exp(sc - m)
                l = jnp.sum(pw, axis=-1, keepdims=True) + jnp.sum(pc, axis=-1, keepdims=True)
                outs.append((_dot(pw.astype(BF16), vwin) + _dot(pc.astype(BF16), vctx)) / l)
                lse_ref[hh] = jnp.broadcast_to(m + jnp.log(l), (TM, LANES))
            y_ref[...] = jnp.where(lo, outs[0], outs[1]).astype(BF16)

        @pl.when(blk >= n_lat_blk)
        def _():
            outs = []
            for hh in range(2):
                qm = jnp.where(half == hh, q, 0.0).astype(BF16)
                sc = _dot_nt(qm, kctx)
                m = jnp.max(sc, axis=-1, keepdims=True)
                pc = jnp.exp(sc - m)
                l = jnp.sum(pc, axis=-1, keepdims=True)
                outs.append(_dot(pc.astype(BF16), vctx) / l)
                lse_ref[hh] = jnp.broadcast_to(m + jnp.log(l), (TM, LANES))
            y_ref[...] = jnp.where(lo, outs[0], outs[1]).astype(BF16)

    return pl.pallas_call(
        body, name="na_fwd", grid=(npair, nb),
        in_specs=[pl.BlockSpec((TM, LANES), lambda p, b: (b, p)),
                  pl.BlockSpec((T, LANES), lambda p, b: (0, npair + p)),
                  pl.BlockSpec((T, LANES), lambda p, b: (0, 2 * npair + p)),
                  pl.BlockSpec((2, 1, TM, KW), lambda p, b: (p, _na_variant(b, n_lat_blk), 0, 0))],
        out_specs=[pl.BlockSpec((TM, LANES), lambda p, b: (b, p)),
                   pl.BlockSpec((2, TM, LANES), lambda p, b: (p, b, 0))],
        out_shape=[jax.ShapeDtypeStruct((T, NA_HEADS * HEAD_DIM), BF16),
                   jax.ShapeDtypeStruct((NA_HEADS, T, LANES), F32)],
        compiler_params=_cp(("parallel", "parallel"), VMEM_BIG),
    )(naqkv, naqkv, naqkv, bias)


def na_bwd_call(naqkv, bias, ya, dya, lse, n_lat):
    T = naqkv.shape[0]
    nb = T // TM
    n_lat_blk = n_lat // TM
    rows = n_lat // GRID_W
    n_ctx = T - n_lat
    KW = NA_KROWS * GRID_W
    npair = NA_HEADS // 2

    def body(q_ref, k_ref, v_ref, b_ref, y_ref, dy_ref, lse_ref, dq_ref, dk_ref, dv_ref, db_ref, dk_sc, dv_sc):
        blk = pl.program_id(1)
        half = _lane_half((TM, LANES))
        lo = half == 0
        q = q_ref[...].astype(F32) * SCALE
        kctx = k_ref[pl.ds(n_lat, n_ctx), :]
        vctx = v_ref[pl.ds(n_lat, n_ctx), :]
        dyv = dy_ref[...].astype(F32)
        yv = y_ref[...].astype(F32)

        @pl.when(blk == 0)
        def _():
            dk_sc[...] = jnp.zeros((T, LANES), F32)
            dv_sc[...] = jnp.zeros((T, LANES), F32)

        @pl.when(blk < n_lat_blk)
        def _():
            off = _na_kstart(blk, rows)
            kwin = k_ref[pl.ds(off, KW), :]
            vwin = v_ref[pl.ds(off, KW), :]
            first = (blk == 0) | (blk == 1) | (blk == n_lat_blk - 1)
            dqs = []
            for hh in range(2):
                sel = half == hh
                qm = jnp.where(sel, q, 0.0).astype(BF16)
                dom = jnp.where(sel, dyv, 0.0)
                delta = jnp.sum(dom * yv, axis=-1, keepdims=True)
                dom = dom.astype(BF16)
                lse_v = lse_ref[hh][:, 0:1]
                pw = jnp.exp(_dot_nt(qm, kwin) + b_ref[hh, 0] - lse_v)
                pc = jnp.exp(_dot_nt(qm, kctx) - lse_v)
                dsw = pw * (_dot_nt(dom, vwin) - delta)
                dsc = pc * (_dot_nt(dom, vctx) - delta)

                @pl.when(first)
                def _():
                    db_ref[hh, 0] = dsw

                @pl.when(jnp.logical_not(first))
                def _():
                    db_ref[hh, 0] += dsw

                dsw = dsw.astype(BF16)
                dsc = dsc.astype(BF16)
                dv_sc[pl.ds(off, KW), :] += _dot_tn(pw.astype(BF16), dom)
                dv_sc[pl.ds(n_lat, n_ctx), :] += _dot_tn(pc.astype(BF16), dom)
                dk_sc[pl.ds(off, KW), :] += _dot_tn(dsw, qm)
                dk_sc[pl.ds(n_lat, n_ctx), :] += _dot_tn(dsc, qm)
                dqs.append((_dot(dsw, kwin) + _dot(dsc, kctx)) * SCALE)
            dq_ref[...] = jnp.where(lo, dqs[0], dqs[1]).astype(BF16)

        @pl.when(blk >= n_lat_blk)
        def _():
            dqs = []
            for hh in range(2):
                sel = half == hh
                qm = jnp.where(sel, q, 0.0).astype(BF16)
                dom = jnp.where(sel, dyv, 0.0)
                delta = jnp.sum(dom * yv, axis=-1, keepdims=True)
                dom = dom.astype(BF16)
                pc = jnp.exp(_dot_nt(qm, kctx) - lse_ref[hh][:, 0:1])
                dsc = (pc * (_dot_nt(dom, vctx) - delta)).astype(BF16)
                dv_sc[pl.ds(n_lat, n_ctx), :] += _dot_tn(pc.astype(BF16), dom)
                dk_sc[pl.ds(n_lat, n_ctx), :] += _dot_tn(dsc, qm)
                dqs.append(_dot(dsc, kctx) * SCALE)
            dq_ref[...] = jnp.where(lo, dqs[0], dqs[1]).astype(BF16)

        @pl.when(blk == nb - 1)
        def _():
            dk_ref[...] = dk_sc[...].astype(BF16)
            dv_ref[...] = dv_sc[...].astype(BF16)

    blkspec = pl.BlockSpec((TM, LANES), lambda p, b: (b, p))
    col = pl.BlockSpec((T, LANES), lambda p, b: (0, p))
    bspec = pl.BlockSpec((2, 1, TM, KW), lambda p, b: (p, _na_variant(b, n_lat_blk), 0, 0))
    dq, dk, dv, db = pl.pallas_call(
        body, name="na_bwd", grid=(npair, nb),
        in_specs=[blkspec,
                  pl.BlockSpec((T, LANES), lambda p, b: (0, npair + p)),
                  pl.BlockSpec((T, LANES), lambda p, b: (0, 2 * npair + p)),
                  bspec, blkspec, blkspec,
                  pl.BlockSpec((2, TM, LANES), lambda p, b: (p, b, 0))],
        out_specs=[blkspec, col, col, bspec],
        out_shape=[jax.ShapeDtypeStruct((T, NA_HEADS * HEAD_DIM), BF16)] * 3
        + [jax.ShapeDtypeStruct((NA_HEADS, 3, TM, KW), F32)],
        scratch_shapes=[pltpu.VMEM((T, LANES), F32), pltpu.VMEM((T, LANES), F32)],
        compiler_params=_cp(("arbitrary", "arbitrary"), VMEM_BIG),
    )(naqkv, naqkv, naqkv, bias, ya, dya, lse)
    return jnp.concatenate([dq, dk, dv], axis=1), db


def rope_tables(n_lat, n_ctx):
    t = jnp.arange(n_lat)
    row = (t // GRID_W).astype(F32)
    col = (t % GRID_W).astype(F32)
    half = HEAD_DIM // 2
    inv = ROPE_THETA ** (-jnp.arange(0, half, 2, dtype=F32) / half)
    ang = jnp.concatenate([row[:, None] * inv, col[:, None] * inv], axis=-1)
    cos = jnp.repeat(jnp.cos(ang), 2, axis=-1)
    sin = jnp.repeat(jnp.sin(ang), 2, axis=-1) * jnp.tile(jnp.array([-1.0, 1.0], F32), half)
    cos = jnp.concatenate([cos, jnp.ones((n_ctx, HEAD_DIM), F32)], axis=0)
    sin = jnp.concatenate([sin, jnp.zeros((n_ctx, HEAD_DIM), F32)], axis=0)
    return jnp.tile(cos, (1, 2)), jnp.tile(sin, (1, 2))


def _row(v):
    return v.reshape(1, -1)


def device_fwd_bwd(xs, target, modv, w, small, n_lat, ex=None):
    T, D = xs.shape
    n_ctx = T - n_lat
    L = modv.shape[0]
    nlt = n_lat // TM
    rows = n_lat // GRID_W
    cos_t, sin_t = rope_tables(n_lat, n_ctx)
    qg = [jnp.tile(_row(small["q_gain"][l]), (1, GQA_Q_HEADS)) for l in range(L)]
    kg = [jnp.tile(_row(small["k_gain"][l]), (1, GQA_KV_HEADS)) for l in range(L)]
    bias = [na_bias_table(small["na_rpb"][l], rows) for l in range(L)]

    later = {}

    def wt(name, l):
        if ex is None or (name == "w_in" and l == 0):
            return w[name][l]
        return later[name][l - 1] if name == "w_in" else later[name][l]

    saved = []
    x = xs
    h = norm_mod_call(x, _row(small["norm1"][0]), modv[0], nlt, 0)
    for l in range(L):
        naqkv, gqkv, gates = qkv_call(h, wt("w_in", l))
        ya, lse_a = na_fwd_call(naqkv, bias[l], n_lat)
        qp, kp = gqa_prep_call(gqkv, qg[l], kg[l], cos_t, sin_t)
        small_scores = gqa_scores_small(small["q_gain"][l], small["k_gain"][l])
        if ex is not None and l == 0:
            yb, lse_b, *gathered = gqa_fwd_call(qp, kp, gqkv, n_lat, small_scores, gather=ex.rest())
            w_in_later, others = ex.rest_whole(gathered)
            later.update(w_in_later)
            later.update(others)
        else:
            yb, lse_b = gqa_fwd_call(qp, kp, gqkv, n_lat, small_scores)
        merged, pa, pb = merge_call(ya, yb, gates, wt("w_pa", l), wt("w_pb", l))
        x1, osave, h2 = wo_call(merged, wt("w_o", l), x, modv[l], _row(small["norm2"][l]), nlt)
        act, a, u = ffn_in_call(h2, wt("w_ffn_in", l))
        if l + 1 < L:
            x2, fsave, hn = ffn_out_call(act, wt("w_ffn_out", l), x1, modv[l], nlt,
                                         _row(small["norm1"][l + 1]), modv[l + 1])
        else:
            x2, fsave = ffn_out_call(act, wt("w_ffn_out", l), x1, modv[l], nlt)
            hn = None
        saved.append(dict(x=x, h=h, naqkv=naqkv, gqkv=gqkv, gates=gates, ya=ya, lse_a=lse_a, qp=qp, kp=kp,
                          yb=yb, lse_b=lse_b, merged=merged, pa=pa, pb=pb, x1=x1, osave=osave, h2=h2,
                          act=act, a=a, u=u, fsave=fsave))
        x, h = x2, hn

    dx, d_final, loss_part = loss_call(x, _row(small["final_norm"]), target, nlt)

    gw = {k: [None] * L for k in ("w_in", "w_pa", "w_pb", "w_o", "w_ffn_in", "w_ffn_out")}
    gs = {k: [None] * L for k in ("norm1", "norm2", "na_rpb", "q_gain", "k_gain")}
    dmod = [None] * L
    fold = lambda s: s.sum(axis=1)
    for l in reversed(range(L)):
        s = saved[l]
        da, du, df, s_g2 = dact_call(dx, modv[l], s["fsave"], wt("w_ffn_out", l), s["a"], s["u"], nlt)
        gw["w_ffn_out"][l] = wgrad_call(s["act"], df, "wgrad_ffn_out")
        dx1, s_sh2, s_sc2, s_n2 = dh2_call(da, du, wt("w_ffn_in", l), s["x1"], modv[l], _row(small["norm2"][l]), dx, nlt)
        gw["w_ffn_in"][l] = jnp.concatenate(
            [wgrad_call(s["h2"], da, "wgrad_ffn_in_a"), wgrad_call(s["h2"], du, "wgrad_ffn_in_u")], axis=1)
        dpa, dpb, dgates, do, s_g1 = dmerged_call(dx1, modv[l], s["osave"], wt("w_o", l), s["pa"], s["pb"], s["gates"], nlt)
        gw["w_o"][l] = wgrad_call(s["merged"], do, "wgrad_o")
        dya, dyb = dy_call(dpa, dpb, wt("w_pa", l), wt("w_pb", l))
        gw["w_pa"][l] = wgrad_call(s["ya"], dpa, "wgrad_pa")
        gw["w_pb"][l] = wgrad_call(s["yb"], dpb, "wgrad_pb")
        dna, dbias = na_bwd_call(s["naqkv"], bias[l], s["ya"], dya, s["lse_a"], n_lat)
        gs["na_rpb"][l] = na_bias_grad(dbias, rows)
        if ex is not None:
            ex.pack_grads(l, {k: v[l] for k, v in gw.items() if k != "w_in"}, [gi for gi in range(len(GROUPS)) if gi != G_IN])
            keys = [(l, gi) for gi in range(len(GROUPS)) if gi != G_IN] + ([(l + 1, G_IN)] if l + 1 < L else [])
            dqp, dkp, dv, *got = gqa_bwd_call(s["qp"], s["kp"], s["gqkv"], s["yb"], dyb, s["lse_b"], n_lat,
                                              exchange=[ex.partials[k] for k in keys])
            ex.received.update(zip(keys, got))
        else:
            dqp, dkp, dv = gqa_bwd_call(s["qp"], s["kp"], s["gqkv"], s["yb"], dyb, s["lse_b"], n_lat)
        dg, s_qg, s_kg = gqa_prep_bwd_call(s["gqkv"], qg[l], kg[l], cos_t, sin_t, dqp, dkp, dv)
        gs["q_gain"][l] = s_qg.reshape(8 * GQA_Q_HEADS, HEAD_DIM).sum(axis=0)
        gs["k_gain"][l] = s_kg.reshape(8 * GQA_KV_HEADS, HEAD_DIM).sum(axis=0)
        dx, s_sh1, s_sc1, s_n1 = dh_call(dna, dg, dgates, wt("w_in", l), s["x"], modv[l], _row(small["norm1"][l]), dx1, nlt)
        gw["w_in"][l] = jnp.concatenate(
            [wgrad_call(s["h"], dna, "wgrad_in_na"), wgrad_call(s["h"], dg, "wgrad_in_g"),
             wgrad_call(s["h"], dgates, "wgrad_in_gate")], axis=1)
        gs["norm1"][l] = fold(s_n1).sum(axis=0)
        gs["norm2"][l] = fold(s_n2).sum(axis=0)
        dmod[l] = jnp.stack([s_sh1, s_sc1, s_g1, s_sh2, s_sc2, s_g2], axis=1).sum(axis=2)
        if ex is not None:
            ex.pack_grads(l, {"w_in": gw["w_in"][l]}, [G_IN])

    if ex is not None:
        ex.received[(0, G_IN)] = grads_exchange([ex.partials[(0, G_IN)]])[0]
        gw = None
    else:
        gw = {k: jnp.stack(v) for k, v in gw.items()}
    gs = {k: jnp.stack(v) for k, v in gs.items()}
    gs["final_norm"] = d_final.sum(axis=(0, 1))
    return loss_part, dx, gw, jnp.stack(dmod), gs


N_DEV = 8
N_CHIP = 4
ANY = pl.BlockSpec(memory_space=pl.ANY)


def _place():
    x, y, c = lax.axis_index("x"), lax.axis_index("y"), lax.axis_index("c")
    chips = [(1 - x, y), (x, 1 - y), (1 - x, 1 - y)]
    return x, y, c, chips


def small_all_gather(v, name):
    m_per, n = v.shape
    assert m_per % 8 == 0

    def body(x_ref, out_ref, send_sems, recv_sems, local_sem):
        x, y, c, chips = _place()
        me, sibling = (x, y, c), (x, y, 1 - c)

        def rows(px, py, pc):
            return out_ref.at[pl.ds(pl.multiple_of((4 * px + 2 * py + pc) * m_per, 8), m_per), :]

        def copy(k, block, to, src=None):
            return pltpu.make_async_remote_copy(
                src_ref=rows(*block) if src is None else src, dst_ref=rows(*block),
                send_sem=send_sems.at[k], recv_sem=recv_sems.at[k], device_id=to, device_id_type=MESH)

        mine = pltpu.make_async_copy(x_ref, rows(*me), local_sem)
        mine.start()
        first = [copy(0, me, sibling, src=x_ref)]
        first += [copy(1 + j, me, (*chip, c), src=x_ref) for j, chip in enumerate(chips)]
        for cp in first:
            cp.start()
        passed = [copy(4 + j, (*chip, c), sibling) for j, chip in enumerate(chips)]
        for j, chip in enumerate(chips):
            copy(1 + j, (*chip, c), me).wait_recv()
            passed[j].start()
        copy(0, sibling, me).wait_recv()
        for j, chip in enumerate(chips):
            copy(4 + j, (*chip, 1 - c), me).wait_recv()
        for cp in first + passed:
            cp.wait_send()
        mine.wait()

    out = pl.pallas_call(
        body, name=name,
        out_shape=jax.ShapeDtypeStruct((N_DEV * m_per, n), v.dtype),
        in_specs=[pl.BlockSpec(memory_space=pltpu.VMEM)],
        out_specs=pl.BlockSpec(memory_space=pltpu.VMEM),
        scratch_shapes=[pltpu.SemaphoreType.DMA((7,)), pltpu.SemaphoreType.DMA((7,)), pltpu.SemaphoreType.DMA],
    )(v)
    return out.reshape(N_DEV, m_per, n)


def _ag_copies(in_ref, out_ref, send_sems, recv_sems, base=0):
    x, y, c, chips = _place()
    me, sibling = (x, y, c), (x, y, 1 - c)

    def blk(px, py, half):
        return out_ref.at[2 * px + py, half]

    def copy(k, block, to, src=None):
        return pltpu.make_async_remote_copy(
            src_ref=blk(*block) if src is None else src, dst_ref=blk(*block),
            send_sem=send_sems.at[base + k], recv_sem=recv_sems.at[base + k], device_id=to, device_id_type=MESH)

    first = [copy(j, (x, y, c), (*chip, c), src=in_ref.at[c]) for j, chip in enumerate(chips)]
    passed = [copy(3 + j, (*chip, c), sibling) for j, chip in enumerate(chips)]
    arrivals = [copy(j, (*chip, c), me) for j, chip in enumerate(chips)]
    forwarded = [copy(3 + j, (*chip, 1 - c), me) for j, chip in enumerate(chips)]
    return first, passed, arrivals, forwarded


AG_SEMS = 6
RS_SEMS = 7


def _ag_start(in_refs, out_refs, send_sems, recv_sems):
    for b, (i, o) in enumerate(zip(in_refs, out_refs)):
        for cp in _ag_copies(i, o, send_sems, recv_sems, AG_SEMS * b)[0]:
            cp.start()


def _ag_forward(in_refs, out_refs, send_sems, recv_sems):
    for b, (i, o) in enumerate(zip(in_refs, out_refs)):
        _, passed, arrivals, _ = _ag_copies(i, o, send_sems, recv_sems, AG_SEMS * b)
        for arrived, onward in zip(arrivals, passed):
            arrived.wait_recv()
            onward.start()


def _ag_finish(in_refs, out_refs, send_sems, recv_sems):
    for b, (i, o) in enumerate(zip(in_refs, out_refs)):
        first, passed, _, forwarded = _ag_copies(i, o, send_sems, recv_sems, AG_SEMS * b)
        for cp in forwarded:
            cp.wait_recv()
        for cp in first + passed:
            cp.wait_send()


def place_own_block(gathered, packed):
    chip = 2 * lax.axis_index("x") + lax.axis_index("y")
    return lax.dynamic_update_slice(gathered, packed[None], (chip,) + (0,) * packed.ndim)


def weights_all_gather(packed):
    n = len(packed)

    def body(*refs):
        _ag_start(refs[:n], refs[n:2 * n], refs[2 * n], refs[2 * n + 1])
        _ag_forward(refs[:n], refs[n:2 * n], refs[2 * n], refs[2 * n + 1])
        _ag_finish(refs[:n], refs[n:2 * n], refs[2 * n], refs[2 * n + 1])

    outs = pl.pallas_call(
        body, name="weights_all_gather",
        out_shape=[jax.ShapeDtypeStruct((N_CHIP,) + p.shape, p.dtype) for p in packed],
        in_specs=[ANY] * n, out_specs=[ANY] * n,
        scratch_shapes=[pltpu.SemaphoreType.DMA((AG_SEMS * n,)), pltpu.SemaphoreType.DMA((AG_SEMS * n,))],
    )(*packed)
    return [place_own_block(o, p) for o, p in zip(outs, packed)]


def _rs_copies(gp_ref, rx_ref, send_sems, recv_sems, base=0):
    x, y, c, chips = _place()
    sends = [pltpu.make_async_remote_copy(src_ref=gp_ref.at[2 * x + y, 1 - c], dst_ref=rx_ref.at[0],
                                          send_sem=send_sems.at[base], recv_sem=recv_sems.at[base],
                                          device_id=(x, y, 1 - c), device_id_type=MESH)]
    for r, (px, py) in enumerate(chips):
        for h in range(2):
            sends.append(pltpu.make_async_remote_copy(
                src_ref=gp_ref.at[2 * px + py, h], dst_ref=rx_ref.at[1 + 2 * r + c],
                send_sem=send_sems.at[base + 1 + 2 * r + h], recv_sem=recv_sems.at[base + 1 + 2 * r + c],
                device_id=(px, py, h), device_id_type=MESH))
    arrivals = [pltpu.make_async_remote_copy(src_ref=gp_ref.at[0, 0], dst_ref=rx_ref.at[s],
                                             send_sem=send_sems.at[base + s], recv_sem=recv_sems.at[base + s],
                                             device_id=(x, y, c), device_id_type=MESH) for s in range(RS_SEMS)]
    return sends, arrivals


def _rs_start(gp_refs, rx_refs, send_sems, recv_sems):
    for b, (gp, rx) in enumerate(zip(gp_refs, rx_refs)):
        for cp in _rs_copies(gp, rx, send_sems, recv_sems, RS_SEMS * b)[0]:
            cp.start()


def _rs_finish(gp_refs, rx_refs, send_sems, recv_sems):
    for b, (gp, rx) in enumerate(zip(gp_refs, rx_refs)):
        sends, arrivals = _rs_copies(gp, rx, send_sems, recv_sems, RS_SEMS * b)
        for cp in arrivals:
            cp.wait_recv()
        for cp in sends:
            cp.wait_send()


def _rs_out_shape(gp):
    return jax.ShapeDtypeStruct((RS_SEMS,) + gp.shape[2:], gp.dtype)


def grads_exchange(gps):
    n = len(gps)

    def body(*refs):
        _rs_start(refs[:n], refs[n:2 * n], refs[2 * n], refs[2 * n + 1])
        _rs_finish(refs[:n], refs[n:2 * n], refs[2 * n], refs[2 * n + 1])

    return pl.pallas_call(
        body, name="grads_exchange", out_shape=[_rs_out_shape(g) for g in gps],
        in_specs=[ANY] * n, out_specs=[ANY] * n,
        scratch_shapes=[pltpu.SemaphoreType.DMA((RS_SEMS * n,)), pltpu.SemaphoreType.DMA((RS_SEMS * n,))],
    )(*gps)


def grads_join_halves(ghs):
    n = len(ghs)

    def body(*refs):
        x, y, c, _ = _place()
        cps = [pltpu.make_async_remote_copy(src_ref=refs[b], dst_ref=refs[n + b], send_sem=refs[2 * n].at[b],
                                            recv_sem=refs[2 * n + 1].at[b], device_id=(x, y, 1 - c),
                                            device_id_type=MESH) for b in range(n)]
        for cp in cps:
            cp.start()
        for cp in cps:
            cp.wait()

    others = pl.pallas_call(
        body, name="grads_join_halves", out_shape=[jax.ShapeDtypeStruct(g.shape, g.dtype) for g in ghs],
        in_specs=[ANY] * n, out_specs=[ANY] * n,
        scratch_shapes=[pltpu.SemaphoreType.DMA((n,)), pltpu.SemaphoreType.DMA((n,))],
    )(*ghs)
    c0 = lax.axis_index("c") == 0
    return [jnp.stack([jnp.where(c0, g, o), jnp.where(c0, o, g)], axis=1) for g, o in zip(ghs, others)]


PACK_C = 1024


def sum_partials_call(gp, rx, chip, core):
    _, _, rh, ncol = gp.shape
    n = rx.shape[0]
    tr = _pick_tile(rh, 512, 16)

    def body(where_ref, a_ref, *rest):
        o_ref = rest[-1]
        g = a_ref[0, 0].astype(F32)
        for b_ref in rest[:-1]:
            g = g + b_ref[0].astype(F32)
        o_ref[...] = g

    other = lambda j: pl.BlockSpec((1, tr, ncol), lambda i, w: (j, i, 0))
    return pl.pallas_call(
        body, name="sum_partials",
        grid_spec=pltpu.PrefetchScalarGridSpec(
            num_scalar_prefetch=1, grid=(rh // tr,),
            in_specs=[pl.BlockSpec((1, 1, tr, ncol), lambda i, w: (w[0], w[1], i, 0))] + [other(j) for j in range(n)],
            out_specs=pl.BlockSpec((tr, ncol), lambda i, w: (i, 0))),
        out_shape=jax.ShapeDtypeStruct((rh, ncol), F32),
        compiler_params=_cp(("parallel",)),
    )(jnp.stack([chip, core]).astype(jnp.int32), gp, *([rx] * n))


def _silu(x):
    return x * _sigmoid(x)


def mod_matmul_call(craw, w_mod):
    L, D, N = w_mod.shape

    def body(c_ref, w_ref, o_ref):
        o_ref[0] = _dot(_silu(c_ref[...]).astype(BF16), w_ref[0].astype(BF16))

    return pl.pallas_call(
        body, name="mod_matmul", grid=(L,),
        in_specs=[pl.BlockSpec((16, D), lambda l: (0, 0)), pl.BlockSpec((1, D, N), lambda l: (l, 0, 0))],
        out_specs=pl.BlockSpec((1, 16, N), lambda l: (l, 0, 0)),
        out_shape=jax.ShapeDtypeStruct((L, 16, N), F32),
        compiler_params=_cp(("parallel",), VMEM_BIG),
    )(craw, w_mod)


def mod_bwd_call(craw, dmod16, w_mod):
    L, D, N = w_mod.shape

    def body(c_ref, d_ref, w_ref, gw_ref, ds_ref):
        l = pl.program_id(0)
        d = d_ref[0].astype(BF16)
        gw_ref[0] = _dot_tn(_silu(c_ref[...]).astype(BF16), d)
        part = _dot_nt(d, w_ref[0].astype(BF16))

        @pl.when(l == 0)
        def _():
            ds_ref[...] = part

        @pl.when(l > 0)
        def _():
            ds_ref[...] += part

    return pl.pallas_call(
        body, name="mod_bwd", grid=(L,),
        in_specs=[pl.BlockSpec((16, D), lambda l: (0, 0)), pl.BlockSpec((1, 16, N), lambda l: (l, 0, 0)),
                  pl.BlockSpec((1, D, N), lambda l: (l, 0, 0))],
        out_specs=[pl.BlockSpec((1, D, N), lambda l: (l, 0, 0)), pl.BlockSpec((16, D), lambda l: (0, 0))],
        out_shape=[jax.ShapeDtypeStruct((L, D, N), F32), jax.ShapeDtypeStruct((16, D), F32)],
        compiler_params=_cp(("arbitrary",), VMEM_BIG),
    )(craw, dmod16, w_mod)


def _adamw(w, g, m, v):
    m2 = ADAM_B1 * m + (1.0 - ADAM_B1) * g
    v2 = ADAM_B2 * v + (1.0 - ADAM_B2) * (g * g)
    m_hat = m2 / (1.0 - ADAM_B1 ** ADAM_STEP)
    v_hat = v2 / (1.0 - ADAM_B2 ** ADAM_STEP)
    delta = -ADAM_LR * (m_hat / (jnp.sqrt(v_hat) + ADAM_EPS) + ADAM_WD * w)
    return delta, m2, v2


def adamw_call(w, g, m, v):
    shape = w.shape
    ncol = shape[-1]
    r = math.prod(shape[:-1])
    tr = _pick_tile(r, 512, 8)
    as2d = lambda t: t.reshape(r, ncol)

    def body(w_ref, g_ref, m_ref, v_ref, d_ref, m2_ref, v2_ref):
        d_ref[...], m2_ref[...], v2_ref[...] = _adamw(w_ref[...], g_ref[...], m_ref[...], v_ref[...])

    spec = pl.BlockSpec((tr, ncol), lambda i: (i, 0))
    outs = pl.pallas_call(
        body, name="adamw", grid=(r // tr,), in_specs=[spec] * 4, out_specs=[spec] * 3,
        out_shape=[jax.ShapeDtypeStruct((r, ncol), F32)] * 3,
        compiler_params=_cp(("parallel",)),
    )(as2d(w), as2d(g), as2d(m), as2d(v))
    return [o.reshape(shape) for o in outs]


def adamw_small_call(parts, w, m, v, silu_bwd=False):
    P, R, C = parts.shape

    def body(p_ref, w_ref, m_ref, v_ref, g_ref, d_ref, m2_ref, v2_ref):
        g = p_ref[0]
        for i in range(1, P):
            g = g + p_ref[i]
        wv = w_ref[...]
        if silu_bwd:
            sg = _sigmoid(wv)
            g = g * (sg * (1.0 + wv * (1.0 - sg)))
        g_ref[...] = g
        d_ref[...], m2_ref[...], v2_ref[...] = _adamw(wv, g, m_ref[...], v_ref[...])

    return pl.pallas_call(
        body, name="adamw_small", out_shape=[jax.ShapeDtypeStruct((R, C), F32)] * 4,
    )(parts, w, m, v)


def sum_parts_call(parts):
    P, R, C = parts.shape

    def body(p_ref, o_ref):
        g = p_ref[0]
        for i in range(1, P):
            g = g + p_ref[i]
        o_ref[...] = g

    return pl.pallas_call(body, name="sum_parts", out_shape=jax.ShapeDtypeStruct((R, C), F32))(parts)


BIG = (("w_in", "col"), ("w_pa", "col"), ("w_pb", "col"), ("w_o", "row"), ("w_ffn_in", "col"), ("w_ffn_out", "row"))
KIND = dict(BIG)
GROUPS = (("w_in",), ("w_ffn_in",), ("w_pa", "w_pb", "w_o", "w_ffn_out"))
G_IN = 0
SMALL = ("norm1", "norm2", "final_norm", "na_rpb", "q_gain", "k_gain")


def _group_width(group, shapes):
    return shapes[group[0]][1] if len(group) == 1 else PACK_C


def _pack_shards(shards, group, shapes):
    width = _group_width(group, shapes)
    lead = shards[group[0]].shape[:-2]
    p = jnp.concatenate([shards[n].reshape(lead + (-1, width)) for n in group], axis=-2)
    return p.reshape(lead + (2, p.shape[-2] // 2, width))


def _unpack_shards(packed, group, shapes):
    width = _group_width(group, shapes)
    lead = packed.shape[:-3]
    flat = packed.reshape(lead + (-1, width))
    out, r0 = {}, 0
    for n in group:
        K, N = shapes[n]
        rw = K * N // width
        out[n] = flat[..., r0:r0 + rw, :].reshape(lead + (K, N))
        r0 += rw
    return out


def _whole_from_chips(g, kind):
    _, L, K, N = g.shape
    if kind == "col":
        return g.transpose(1, 2, 0, 3).reshape(L, K, N_CHIP * N)
    return g.transpose(1, 0, 2, 3).reshape(L, N_CHIP * K, N)


def _chips_from_whole(g, kind):
    K, N = g.shape
    if kind == "col":
        return g.reshape(K, N_CHIP, N // N_CHIP).transpose(1, 0, 2)
    return g.reshape(N_CHIP, K // N_CHIP, N)


class Exchange:
    def __init__(self, local_bf16, shard_shapes):
        self.shapes = shard_shapes
        self.packed = [_pack_shards(local_bf16, g, shard_shapes).transpose(1, 0, 2, 3) for g in GROUPS]
        self.received = {}
        self.partials = {}

    def whole(self, gathered, group):
        g = _unpack_shards(gathered.transpose(0, 2, 1, 3, 4), group, self.shapes)
        return {n: _whole_from_chips(g[n], KIND[n]) for n in group}

    def first_needed(self):
        return self.whole(weights_all_gather([self.packed[G_IN][:, :1]])[0], GROUPS[G_IN])

    def rest(self):
        return [p[:, 1:] if gi == G_IN else p for gi, p in enumerate(self.packed)]

    def rest_whole(self, gathered):
        out = [self.whole(place_own_block(g, p), grp) for g, p, grp in zip(gathered, self.rest(), GROUPS)]
        others = {}
        for gi, d in enumerate(out):
            if gi != G_IN:
                others.update(d)
        return out[G_IN], others

    def pack_grads(self, layer, gw, groups):
        for gi in groups:
            shards = {n: _chips_from_whole(gw[n], KIND[n]) for n in GROUPS[gi]}
            self.partials[(layer, gi)] = _pack_shards(shards, GROUPS[gi], self.shapes)


def _pad_rows(v, rows):
    return jnp.pad(v.reshape(-1), (0, rows * PACK_C - v.size)).reshape(rows, PACK_C)


def kernel(x, c, ctx, c_ctx, w_mod, b_mod, norm1, w_in, na_rpb, q_gain, k_gain, w_pa, w_pb, w_o, norm2, w_ffn_in, w_ffn_out, final_norm, loss_target, m_c_ctx, m_w_mod, m_b_mod, m_norm1, m_w_in, m_na_rpb, m_q_gain, m_k_gain, m_w_pa, m_w_pb, m_w_o, m_norm2, m_w_ffn_in, m_w_ffn_out, m_final_norm, v_c_ctx, v_w_mod, v_b_mod, v_norm1, v_w_in, v_na_rpb, v_q_gain, v_k_gain, v_w_pa, v_w_pb, v_w_o, v_norm2, v_w_ffn_in, v_w_ffn_out, v_final_norm):
    W = dict(c_ctx=c_ctx, w_mod=w_mod, b_mod=b_mod, norm1=norm1, w_in=w_in, na_rpb=na_rpb, q_gain=q_gain, k_gain=k_gain,
             w_pa=w_pa, w_pb=w_pb, w_o=w_o, norm2=norm2, w_ffn_in=w_ffn_in, w_ffn_out=w_ffn_out, final_norm=final_norm)
    M = dict(c_ctx=m_c_ctx, w_mod=m_w_mod, b_mod=m_b_mod, norm1=m_norm1, w_in=m_w_in, na_rpb=m_na_rpb, q_gain=m_q_gain,
             k_gain=m_k_gain, w_pa=m_w_pa, w_pb=m_w_pb, w_o=m_w_o, norm2=m_norm2, w_ffn_in=m_w_ffn_in,
             w_ffn_out=m_w_ffn_out, final_norm=m_final_norm)
    V = dict(c_ctx=v_c_ctx, w_mod=v_w_mod, b_mod=v_b_mod, norm1=v_norm1, w_in=v_w_in, na_rpb=v_na_rpb, q_gain=v_q_gain,
             k_gain=v_k_gain, w_pa=v_w_pa, w_pb=v_w_pb, w_o=v_w_o, norm2=v_norm2, w_ffn_in=v_w_ffn_in,
             w_ffn_out=v_w_ffn_out, final_norm=v_final_norm)
    order = ["c_ctx", "w_mod", "b_mod", "norm1", "w_in", "na_rpb", "q_gain", "k_gain", "w_pa", "w_pb", "w_o", "norm2",
             "w_ffn_in", "w_ffn_out", "final_norm"]
    L, D = norm1.shape
    n_lat = x.shape[1]
    ax, ay, ac = lax.axis_index("x"), lax.axis_index("y"), lax.axis_index("c")
    chip = 2 * ax + ay
    dev = 2 * chip + ac

    shard_shapes = {n: W[n].shape[1:] for n, _ in BIG}
    ex = Exchange({n: W[n].astype(BF16) for n, _ in BIG}, shard_shapes)
    w_first = ex.first_needed()

    c_all = small_all_gather(jnp.pad(c, ((0, 7), (0, 0))), "gather_c")[:, 0, :]
    craw = jnp.pad(c_all, ((0, 8), (0, 0))) + jnp.pad(c_ctx[None, :], ((8, 7), (0, 0)))
    ncol_mod = w_mod.shape[2]
    mod_loc = mod_matmul_call(craw, w_mod)
    mod_all = small_all_gather(mod_loc.reshape(L * 16, ncol_mod), "gather_mod")
    mod_all = mod_all[0::2].reshape(N_CHIP, L, 16, ncol_mod).transpose(1, 2, 0, 3).reshape(L, 16, 6 * D)
    mod_all = mod_all + b_mod[:, None, :]
    mod_mine = lax.dynamic_index_in_dim(mod_all, dev, axis=1, keepdims=False)
    modv = jnp.stack([mod_mine, mod_all[:, 8]], axis=1).reshape(L, 2, 6, D)

    xs = jnp.concatenate([x[0], ctx[0]], axis=0)
    small = dict(norm1=norm1, norm2=norm2, na_rpb=na_rpb, q_gain=q_gain, k_gain=k_gain, final_norm=final_norm)
    loss_part, dx, _, dmod, gs = device_fwd_bwd(xs, loss_target[0], modv, w_first, small, n_lat, ex)
    loss = lax.psum(jnp.sum(loss_part), ("x", "y", "c"))
    grad_x = dx[:n_lat][None]

    gh = [jnp.stack([sum_partials_call(ex.partials[(l, gi)], ex.received[(l, gi)], chip, ac) for l in range(L)])
          for gi in range(len(GROUPS))]
    gbig = {}
    for grp, joined in zip(GROUPS, grads_join_halves(gh)):
        gbig.update(_unpack_shards(joined, grp, shard_shapes))

    n_mod_rows = L * 6 * D // PACK_C
    small_flat = jnp.concatenate([gs[n].reshape(-1) for n in SMALL])
    n_small_rows = -(-small_flat.size // (8 * PACK_C)) * 8
    g1 = jnp.concatenate([dmod[:, 0].reshape(n_mod_rows, PACK_C), dmod[:, 1].reshape(n_mod_rows, PACK_C),
                          _pad_rows(small_flat, n_small_rows)], axis=0)
    g1 = small_all_gather(g1, "gather_small_grads")
    dm_lat = g1[:, :n_mod_rows].reshape(N_DEV, L, 6 * D)
    dm_ctx_parts = g1[:, n_mod_rows:2 * n_mod_rows]
    dm_ctx = sum_parts_call(dm_ctx_parts).reshape(L, 1, 6 * D)
    dmod16 = (jnp.pad(dm_lat.transpose(1, 0, 2), ((0, 0), (0, 8), (0, 0)))
              + jnp.pad(dm_ctx, ((0, 0), (8, 7), (0, 0))))
    dmod16_loc = lax.dynamic_slice_in_dim(dmod16, chip * ncol_mod, ncol_mod, axis=2)
    g_wmod, dsilu_part = mod_bwd_call(craw, dmod16_loc, w_mod)
    ds_all = small_all_gather(dsilu_part[8:16], "gather_dsilu")
    ds_parts = ds_all[0::2, 0:1, :]
    ds_parts = jnp.pad(ds_parts, ((0, 0), (0, 7), (0, 0)))

    out_g, out_d, out_m, out_v = {}, {}, {}, {}
    for n, _ in BIG:
        out_g[n] = gbig[n]
        out_d[n], out_m[n], out_v[n] = adamw_call(W[n], gbig[n], M[n], V[n])
    out_g["w_mod"] = g_wmod
    out_d["w_mod"], out_m["w_mod"], out_v["w_mod"] = adamw_call(w_mod, g_wmod, m_w_mod, v_w_mod)

    pack_small = lambda t: _pad_rows(jnp.concatenate([t[n].reshape(-1) for n in SMALL]), n_small_rows)
    res = adamw_small_call(g1[:, 2 * n_mod_rows:], pack_small(W), pack_small(M), pack_small(V))
    off = 0
    for n in SMALL:
        sz = W[n].size
        for dst, r in zip((out_g, out_d, out_m, out_v), res):
            dst[n] = r.reshape(-1)[off:off + sz].reshape(W[n].shape)
        off += sz
    bparts = jnp.concatenate([g1[:, :n_mod_rows], dm_ctx_parts], axis=0)
    as_rows = lambda t: t.reshape(n_mod_rows, PACK_C)
    res = adamw_small_call(bparts, as_rows(b_mod), as_rows(m_b_mod), as_rows(v_b_mod))
    for dst, r in zip((out_g, out_d, out_m, out_v), res):
        dst["b_mod"] = r.reshape(b_mod.shape)
    row8 = lambda t: jnp.pad(t[None, :], ((0, 7), (0, 0)))
    res = adamw_small_call(ds_parts, row8(c_ctx), row8(m_c_ctx), row8(v_c_ctx), silu_bwd=True)
    for dst, r in zip((out_g, out_d, out_m, out_v), res):
        dst["c_ctx"] = r[0]

    return (loss, grad_x, *[out_g[n] for n in order], *[out_d[n] for n in order],
            *[out_m[n] for n in order], *[out_v[n] for n in order])
```

```python
import math

import numpy as np
import jax
import jax.numpy as jnp
from jax import lax
from jax.experimental import pallas as pl
from jax.experimental.pallas import tpu as pltpu

F32 = jnp.float32
BF16 = jnp.bfloat16
MESH = pl.DeviceIdType.MESH

HEAD_DIM = 64
NA_HEADS = 8
GQA_Q_HEADS = 8
GQA_KV_HEADS = 2
GRID_W = 64
NA_WIN_H = 8
NA_WIN_W = 16
ROPE_THETA = 10000.0
EPS = 1e-6
SCALE = HEAD_DIM ** -0.5
NEG = -1e30

ADAM_LR = 0.001
ADAM_B1 = 0.9
ADAM_B2 = 0.999
ADAM_EPS = 1e-08
ADAM_WD = 0.01
ADAM_STEP = 10

TM = 256
NA_ROWS = 4
NA_KROWS = 12
LANES = 128
VMEM_BIG = 56 * 1024 * 1024


def _cp(sem, vmem=None):
    return pltpu.CompilerParams(dimension_semantics=sem, vmem_limit_bytes=vmem)


def _dot(a, b):
    return jnp.dot(a, b, preferred_element_type=F32)


def _dot_nt(a, b):
    return lax.dot_general(a, b, (((1,), (1,)), ((), ())), preferred_element_type=F32)


def _dot_tn(a, b):
    return lax.dot_general(a, b, (((0,), (0,)), ((), ())), preferred_element_type=F32)


def _colsum8(v):
    tm, d = v.shape
    return v.reshape(tm // 8, 8, d).sum(axis=0)


def _acc(ref, val, first):
    @pl.when(first)
    def _():
        ref[0] = val

    @pl.when(jnp.logical_not(first))
    def _():
        ref[0] += val


def _rms_r(x):
    return lax.rsqrt(jnp.mean(x * x, axis=-1, keepdims=True) + EPS)


def _norm_mod(x, nw, sh, sc):
    return (x * _rms_r(x) * nw) * (1.0 + sc) + sh


def _norm_mod_bwd(dh, x, nw, sc):
    r = _rms_r(x)
    xn = x * r
    dxn = dh * (nw * (1.0 + sc))
    dx = r * (dxn - xn * jnp.mean(dxn * xn, axis=-1, keepdims=True))
    return dx, dh, dh * (xn * nw), dh * ((1.0 + sc) * xn)


def _sigmoid(x):
    return 1.0 / (1.0 + jnp.exp(-x))


def _group_of(i, n_lat_tiles):
    return jnp.where(i >= n_lat_tiles, 1, 0)


def _lane_half(shape):
    return (lax.broadcasted_iota(jnp.int32, shape, len(shape) - 1) % LANES) // HEAD_DIM


def _lane_lt64(shape):
    return _lane_half(shape) == 0


def _roll64(x):
    return pltpu.roll(x, HEAD_DIM, x.ndim - 1)


def norm_mod_call(x, nw, modv, n_lat_tiles, which):
    T, D = x.shape

    def body(x_ref, nw_ref, mod_ref, h_ref):
        sh = mod_ref[0, 3 * which:3 * which + 1, :]
        sc = mod_ref[0, 3 * which + 1:3 * which + 2, :]
        h_ref[...] = _norm_mod(x_ref[...], nw_ref[...], sh, sc).astype(BF16)

    return pl.pallas_call(
        body, name="norm_mod", grid=(T // TM,),
        in_specs=[pl.BlockSpec((TM, D), lambda i: (i, 0)),
                  pl.BlockSpec((1, D), lambda i: (0, 0)),
                  pl.BlockSpec((1, 6, D), lambda i: (_group_of(i, n_lat_tiles), 0, 0))],
        out_specs=pl.BlockSpec((TM, D), lambda i: (i, 0)),
        out_shape=jax.ShapeDtypeStruct((T, D), BF16),
        compiler_params=_cp(("parallel",)),
    )(x, nw, modv)


def qkv_call(h, w_in):
    T, D = h.shape
    N = w_in.shape[1]
    n_na = 3 * NA_HEADS * HEAD_DIM
    n_g = (GQA_Q_HEADS + 2 * GQA_KV_HEADS) * HEAD_DIM
    n_gate = N - n_na - n_g

    def body(h_ref, w_ref, na_ref, g_ref, gate_ref):
        acc = _dot(h_ref[...], w_ref[...])
        na_ref[...] = acc[:, :n_na].astype(BF16)
        g_ref[...] = acc[:, n_na:n_na + n_g].astype(BF16)
        gate_ref[...] = acc[:, n_na + n_g:].astype(BF16)

    return pl.pallas_call(
        body, name="qkv", grid=(T // TM,),
        in_specs=[pl.BlockSpec((TM, D), lambda i: (i, 0)),
                  pl.BlockSpec((D, N), lambda i: (0, 0))],
        out_specs=[pl.BlockSpec((TM, n_na), lambda i: (i, 0)),
                   pl.BlockSpec((TM, n_g), lambda i: (i, 0)),
                   pl.BlockSpec((TM, n_gate), lambda i: (i, 0))],
        out_shape=[jax.ShapeDtypeStruct((T, n_na), BF16),
                   jax.ShapeDtypeStruct((T, n_g), BF16),
                   jax.ShapeDtypeStruct((T, n_gate), BF16)],
        compiler_params=_cp(("parallel",), VMEM_BIG),
    )(h, w_in)


def merge_call(ya, yb, gates, w_pa, w_pb):
    T, Ka = ya.shape
    D = w_pa.shape[1]

    def body(ya_ref, yb_ref, ga_ref, gb_ref, wa_ref, wb_ref, m_ref, pa_ref, pb_ref):
        pa = _dot(ya_ref[...], wa_ref[...])
        pb = _dot(yb_ref[...], wb_ref[...])
        m = _sigmoid(ga_ref[...].astype(F32)) * pa + _sigmoid(gb_ref[...].astype(F32)) * pb
        m_ref[...] = m.astype(BF16)
        pa_ref[...] = pa.astype(BF16)
        pb_ref[...] = pb.astype(BF16)

    row = lambda w: pl.BlockSpec((TM, w), lambda i: (i, 0))
    return pl.pallas_call(
        body, name="merge", grid=(T // TM,),
        in_specs=[row(Ka), row(Ka),
                  pl.BlockSpec((TM, D), lambda i: (i, 0)), pl.BlockSpec((TM, D), lambda i: (i, 1)),
                  pl.BlockSpec((Ka, D), lambda i: (0, 0)), pl.BlockSpec((Ka, D), lambda i: (0, 0))],
        out_specs=[row(D), row(D), row(D)],
        out_shape=[jax.ShapeDtypeStruct((T, D), BF16)] * 3,
        compiler_params=_cp(("parallel",)),
    )(ya, yb, gates, gates, w_pa, w_pb)


def wo_call(merged, w_o, x, modv, nw2, n_lat_tiles):
    T, D = x.shape

    def body(m_ref, w_ref, x_ref, mod_ref, nw_ref, x1_ref, o_ref, h2_ref):
        o = _dot(m_ref[...], w_ref[...])
        x1 = x_ref[...] + mod_ref[0, 2:3, :] * o
        x1_ref[...] = x1
        o_ref[...] = o.astype(BF16)
        h2_ref[...] = _norm_mod(x1, nw_ref[...], mod_ref[0, 3:4, :], mod_ref[0, 4:5, :]).astype(BF16)

    row = pl.BlockSpec((TM, D), lambda i: (i, 0))
    return pl.pallas_call(
        body, name="wo", grid=(T // TM,),
        in_specs=[row, pl.BlockSpec((D, D), lambda i: (0, 0)), row,
                  pl.BlockSpec((1, 6, D), lambda i: (_group_of(i, n_lat_tiles), 0, 0)),
                  pl.BlockSpec((1, D), lambda i: (0, 0))],
        out_specs=[row, row, row],
        out_shape=[jax.ShapeDtypeStruct((T, D), F32), jax.ShapeDtypeStruct((T, D), BF16),
                   jax.ShapeDtypeStruct((T, D), BF16)],
        compiler_params=_cp(("parallel",)),
    )(merged, w_o, x, modv, nw2)


def ffn_in_call(h2, w_ffn_in, n_col_tiles=2):
    T, D = h2.shape
    F = w_ffn_in.shape[1] // 2
    tn = F // n_col_tiles

    def body(h_ref, wa_ref, wu_ref, act_ref, a_ref, u_ref):
        h = h_ref[...]
        a = _dot(h, wa_ref[...])
        u = _dot(h, wu_ref[...])
        act_ref[...] = (a * _sigmoid(a) * u).astype(BF16)
        a_ref[...] = a.astype(BF16)
        u_ref[...] = u.astype(BF16)

    blk = pl.BlockSpec((TM, tn), lambda j, i: (i, j))
    return pl.pallas_call(
        body, name="ffn_in", grid=(n_col_tiles, T // TM),
        in_specs=[pl.BlockSpec((TM, D), lambda j, i: (i, 0)),
                  pl.BlockSpec((D, tn), lambda j, i: (0, j)),
                  pl.BlockSpec((D, tn), lambda j, i: (0, n_col_tiles + j))],
        out_specs=[blk, blk, blk],
        out_shape=[jax.ShapeDtypeStruct((T, F), BF16)] * 3,
        compiler_params=_cp(("parallel", "parallel"), VMEM_BIG),
    )(h2, w_ffn_in, w_ffn_in)


def ffn_out_call(act, w_ffn_out, x1, modv, n_lat_tiles, next_nw=None, next_modv=None):
    T, D = x1.shape
    F = act.shape[1]
    with_next = next_nw is not None

    def body(*refs):
        if with_next:
            a_ref, w_ref, x_ref, mod_ref, nw_ref, nmod_ref, x2_ref, f_ref, hn_ref = refs
        else:
            a_ref, w_ref, x_ref, mod_ref, x2_ref, f_ref = refs
        f = _dot(a_ref[...], w_ref[...])
        x2 = x_ref[...] + mod_ref[0, 5:6, :] * f
        x2_ref[...] = x2
        f_ref[...] = f.astype(BF16)
        if with_next:
            hn_ref[...] = _norm_mod(x2, nw_ref[...], nmod_ref[0, 0:1, :], nmod_ref[0, 1:2, :]).astype(BF16)

    row = pl.BlockSpec((TM, D), lambda i: (i, 0))
    modspec = pl.BlockSpec((1, 6, D), lambda i: (_group_of(i, n_lat_tiles), 0, 0))
    in_specs = [pl.BlockSpec((TM, F), lambda i: (i, 0)), pl.BlockSpec((F, D), lambda i: (0, 0)), row, modspec]
    args = [act, w_ffn_out, x1, modv]
    out_specs = [row, row]
    out_shape = [jax.ShapeDtypeStruct((T, D), F32), jax.ShapeDtypeStruct((T, D), BF16)]
    if with_next:
        in_specs += [pl.BlockSpec((1, D), lambda i: (0, 0)), modspec]
        args += [next_nw, next_modv]
        out_specs.append(row)
        out_shape.append(jax.ShapeDtypeStruct((T, D), BF16))
    return pl.pallas_call(
        body, name="ffn_out", grid=(T // TM,), in_specs=in_specs, out_specs=out_specs, out_shape=out_shape,
        compiler_params=_cp(("parallel",), VMEM_BIG),
    )(*args)


def loss_call(x, nw, target, n_lat_tiles):
    T, D = x.shape

    def body(x_ref, nw_ref, t_ref, dx_ref, dw_ref, ls_ref):
        i = pl.program_id(0)

        @pl.when(i < n_lat_tiles)
        def _():
            xv = x_ref[...]
            nw_v = nw_ref[...]
            r = _rms_r(xv)
            xn = xv * r
            e = xn * nw_v - t_ref[...]
            dy = e * (1.0 / D)
            dxn = dy * nw_v
            dx_ref[...] = r * (dxn - xn * jnp.mean(dxn * xn, axis=-1, keepdims=True))
            _acc(dw_ref, _colsum8(dy * xn), i == 0)
            _acc(ls_ref, _colsum8(e * e * (0.5 / D)), i == 0)

        @pl.when(i >= n_lat_tiles)
        def _():
            dx_ref[...] = jnp.zeros((TM, D), F32)

    stat = pl.BlockSpec((1, 8, D), lambda i: (0, 0, 0))
    return pl.pallas_call(
        body, name="loss", grid=(T // TM,),
        in_specs=[pl.BlockSpec((TM, D), lambda i: (i, 0)), pl.BlockSpec((1, D), lambda i: (0, 0)),
                  pl.BlockSpec((TM, D), lambda i: (jnp.minimum(i, n_lat_tiles - 1), 0))],
        out_specs=[pl.BlockSpec((TM, D), lambda i: (i, 0)), stat, stat],
        out_shape=[jax.ShapeDtypeStruct((T, D), F32), jax.ShapeDtypeStruct((1, 8, D), F32),
                   jax.ShapeDtypeStruct((1, 8, D), F32)],
        compiler_params=_cp(("arbitrary",)),
    )(x, nw, target)


def dact_call(dx2, modv, fsave, w_ffn_out, a, u, n_lat_tiles):
    T, D = dx2.shape
    F = a.shape[1]

    def body(dx_ref, mod_ref, f_ref, w_ref, a_ref, u_ref, da_ref, du_ref, df_ref, dg_ref):
        i = pl.program_id(0)
        dx = dx_ref[...]
        df = (dx * mod_ref[0, 5:6, :]).astype(BF16)
        df_ref[...] = df
        dact = _dot_nt(df, w_ref[...])
        av = a_ref[...].astype(F32)
        uv = u_ref[...].astype(F32)
        sg = _sigmoid(av)
        da_ref[...] = (dact * uv * (sg * (1.0 + av * (1.0 - sg)))).astype(BF16)
        du_ref[...] = (dact * (av * sg)).astype(BF16)
        _acc(dg_ref, _colsum8(dx * f_ref[...].astype(F32)), (i == 0) | (i == n_lat_tiles))

    row = pl.BlockSpec((TM, D), lambda i: (i, 0))
    wide = pl.BlockSpec((TM, F), lambda i: (i, 0))
    grp = lambda i: (_group_of(i, n_lat_tiles), 0, 0)
    return pl.pallas_call(
        body, name="dact", grid=(T // TM,),
        in_specs=[row, pl.BlockSpec((1, 6, D), grp), row, pl.BlockSpec((F, D), lambda i: (0, 0)), wide, wide],
        out_specs=[wide, wide, row, pl.BlockSpec((1, 8, D), grp)],
        out_shape=[jax.ShapeDtypeStruct((T, F), BF16), jax.ShapeDtypeStruct((T, F), BF16),
                   jax.ShapeDtypeStruct((T, D), BF16), jax.ShapeDtypeStruct((2, 8, D), F32)],
        compiler_params=_cp(("arbitrary",), VMEM_BIG),
    )(dx2, modv, fsave, w_ffn_out, a, u)


def _norm_bwd_tail(dh, x_ref, mod_ref, nw_ref, dres_ref, dx_ref, s_sh, s_sc, s_w, which, first):
    sc = mod_ref[0, 3 * which + 1:3 * which + 2, :]
    dxn, t_sh, t_sc, t_w = _norm_mod_bwd(dh, x_ref[...], nw_ref[...], sc)
    dx_ref[...] = dres_ref[...] + dxn
    _acc(s_sh, _colsum8(t_sh), first)
    _acc(s_sc, _colsum8(t_sc), first)
    _acc(s_w, _colsum8(t_w), first)


def dh2_call(da, du, w_ffn_in, x1, modv, nw2, dx2, n_lat_tiles):
    T, D = x1.shape
    F = da.shape[1]

    def body(da_ref, du_ref, wa_ref, wu_ref, x_ref, mod_ref, nw_ref, dres_ref, dx_ref, s_sh, s_sc, s_w):
        i = pl.program_id(0)
        dh = _dot_nt(da_ref[...], wa_ref[...]) + _dot_nt(du_ref[...], wu_ref[...])
        _norm_bwd_tail(dh, x_ref, mod_ref, nw_ref, dres_ref, dx_ref, s_sh, s_sc, s_w, 1,
                       (i == 0) | (i == n_lat_tiles))

    row = pl.BlockSpec((TM, D), lambda i: (i, 0))
    wide = pl.BlockSpec((TM, F), lambda i: (i, 0))
    grp = lambda i: (_group_of(i, n_lat_tiles), 0, 0)
    stat = pl.BlockSpec((1, 8, D), grp)
    return pl.pallas_call(
        body, name="dh2", grid=(T // TM,),
        in_specs=[wide, wide, pl.BlockSpec((D, F), lambda i: (0, 0)), pl.BlockSpec((D, F), lambda i: (0, 1)),
                  row, pl.BlockSpec((1, 6, D), grp), pl.BlockSpec((1, D), lambda i: (0, 0)), row],
        out_specs=[row, stat, stat, stat],
        out_shape=[jax.ShapeDtypeStruct((T, D), F32)] + [jax.ShapeDtypeStruct((2, 8, D), F32)] * 3,
        compiler_params=_cp(("arbitrary",), VMEM_BIG),
    )(da, du, w_ffn_in, w_ffn_in, x1, modv, nw2, dx2)


def dmerged_call(dx1, modv, osave, w_o, pa, pb, gates, n_lat_tiles):
    T, D = dx1.shape

    def body(dx_ref, mod_ref, o_ref, w_ref, pa_ref, pb_ref, ga_ref, gb_ref,
             dpa_ref, dpb_ref, dgate_ref, do_ref, dg_ref):
        i = pl.program_id(0)
        dx = dx_ref[...]
        do = (dx * mod_ref[0, 2:3, :]).astype(BF16)
        do_ref[...] = do
        dm = _dot_nt(do, w_ref[...])
        sa = _sigmoid(ga_ref[...].astype(F32))
        sb = _sigmoid(gb_ref[...].astype(F32))
        dpa_ref[...] = (dm * sa).astype(BF16)
        dpb_ref[...] = (dm * sb).astype(BF16)
        dgate_ref[:, :D] = (dm * pa_ref[...].astype(F32) * (sa * (1.0 - sa))).astype(BF16)
        dgate_ref[:, D:] = (dm * pb_ref[...].astype(F32) * (sb * (1.0 - sb))).astype(BF16)
        _acc(dg_ref, _colsum8(dx * o_ref[...].astype(F32)), (i == 0) | (i == n_lat_tiles))

    row = pl.BlockSpec((TM, D), lambda i: (i, 0))
    grp = lambda i: (_group_of(i, n_lat_tiles), 0, 0)
    return pl.pallas_call(
        body, name="dmerged", grid=(T // TM,),
        in_specs=[row, pl.BlockSpec((1, 6, D), grp), row, pl.BlockSpec((D, D), lambda i: (0, 0)), row, row,
                  pl.BlockSpec((TM, D), lambda i: (i, 0)), pl.BlockSpec((TM, D), lambda i: (i, 1))],
        out_specs=[row, row, pl.BlockSpec((TM, 2 * D), lambda i: (i, 0)), row, pl.BlockSpec((1, 8, D), grp)],
        out_shape=[jax.ShapeDtypeStruct((T, D), BF16), jax.ShapeDtypeStruct((T, D), BF16),
                   jax.ShapeDtypeStruct((T, 2 * D), BF16), jax.ShapeDtypeStruct((T, D), BF16),
                   jax.ShapeDtypeStruct((2, 8, D), F32)],
        compiler_params=_cp(("arbitrary",)),
    )(dx1, modv, osave, w_o, pa, pb, gates, gates)


def dy_call(dpa, dpb, w_pa, w_pb):
    T, D = dpa.shape
    K = w_pa.shape[0]

    def body(a_ref, b_ref, wa_ref, wb_ref, ya_ref, yb_ref):
        ya_ref[...] = _dot_nt(a_ref[...], wa_ref[...]).astype(BF16)
        yb_ref[...] = _dot_nt(b_ref[...], wb_ref[...]).astype(BF16)

    row = pl.BlockSpec((TM, D), lambda i: (i, 0))
    w = pl.BlockSpec((K, D), lambda i: (0, 0))
    out = pl.BlockSpec((TM, K), lambda i: (i, 0))
    return pl.pallas_call(
        body, name="dy", grid=(T // TM,), in_specs=[row, row, w, w], out_specs=[out, out],
        out_shape=[jax.ShapeDtypeStruct((T, K), BF16)] * 2,
        compiler_params=_cp(("parallel",)),
    )(dpa, dpb, w_pa, w_pb)


def dh_call(dna, dg, dgates, w_in, x, modv, nw1, dx1, n_lat_tiles):
    T, D = x.shape
    n1, n2, n3 = dna.shape[1], dg.shape[1], dgates.shape[1]
    N = n1 + n2 + n3

    def body(a_ref, b_ref, c_ref, w_ref, x_ref, mod_ref, nw_ref, dres_ref, dx_ref, s_sh, s_sc, s_w):
        i = pl.program_id(0)
        dh = (_dot_nt(a_ref[...], w_ref[:, :n1]) + _dot_nt(b_ref[...], w_ref[:, n1:n1 + n2])
              + _dot_nt(c_ref[...], w_ref[:, n1 + n2:]))
        _norm_bwd_tail(dh, x_ref, mod_ref, nw_ref, dres_ref, dx_ref, s_sh, s_sc, s_w, 0,
                       (i == 0) | (i == n_lat_tiles))

    row = pl.BlockSpec((TM, D), lambda i: (i, 0))
    grp = lambda i: (_group_of(i, n_lat_tiles), 0, 0)
    stat = pl.BlockSpec((1, 8, D), grp)
    return pl.pallas_call(
        body, name="dh", grid=(T // TM,),
        in_specs=[pl.BlockSpec((TM, n1), lambda i: (i, 0)), pl.BlockSpec((TM, n2), lambda i: (i, 0)),
                  pl.BlockSpec((TM, n3), lambda i: (i, 0)), pl.BlockSpec((D, N), lambda i: (0, 0)),
                  row, pl.BlockSpec((1, 6, D), grp), pl.BlockSpec((1, D), lambda i: (0, 0)), row],
        out_specs=[row, stat, stat, stat],
        out_shape=[jax.ShapeDtypeStruct((T, D), F32)] + [jax.ShapeDtypeStruct((2, 8, D), F32)] * 3,
        compiler_params=_cp(("arbitrary",), VMEM_BIG),
    )(dna, dg, dgates, w_in, x, modv, nw1, dx1)


def _pick_tile(n, cap, mult):
    best = None
    for d in range(mult, min(n, cap) + 1, mult):
        if n % d == 0:
            best = d
    assert best is not None, (n, cap, mult)
    return best


def wgrad_call(a, b, name):
    T, K = a.shape
    N = b.shape[1]
    tt = _pick_tile(T, 768, 128)
    tk = K if K <= 1024 else _pick_tile(K, 1408, 128)
    tn = N if N <= 2304 else _pick_tile(N, 2304, 128)
    nt = T // tt

    def body(a_ref, b_ref, o_ref, acc_ref):
        t = pl.program_id(2)
        p = _dot_tn(a_ref[...], b_ref[...])

        @pl.when(t == 0)
        def _():
            acc_ref[...] = p

        @pl.when(t > 0)
        def _():
            acc_ref[...] += p

        @pl.when(t == nt - 1)
        def _():
            o_ref[...] = acc_ref[...].astype(BF16)

    return pl.pallas_call(
        body, name=name, grid=(K // tk, N // tn, nt),
        in_specs=[pl.BlockSpec((tt, tk), lambda i, j, t: (t, i)), pl.BlockSpec((tt, tn), lambda i, j, t: (t, j))],
        out_specs=pl.BlockSpec((tk, tn), lambda i, j, t: (i, j)),
        out_shape=jax.ShapeDtypeStruct((K, N), BF16),
        scratch_shapes=[pltpu.VMEM((tk, tn), F32)],
        compiler_params=_cp(("parallel", "parallel", "arbitrary"), VMEM_BIG),
    )(a, b)


def _head_mean(v):
    n = v.shape[-1]
    r = lax.broadcasted_iota(jnp.int32, (n, n), 0) // HEAD_DIM
    c = lax.broadcasted_iota(jnp.int32, (n, n), 1) // HEAD_DIM
    bd = jnp.where(r == c, 1.0 / HEAD_DIM, 0.0).astype(BF16)
    hi = v.astype(BF16)
    lo = (v - hi.astype(F32)).astype(BF16)
    return _dot(hi, bd) + _dot(lo, bd)


def _swap_pairs(x):
    n = x.shape[-1]
    even = lax.broadcasted_iota(jnp.int32, x.shape, x.ndim - 1) % 2 == 0
    return jnp.where(even, pltpu.roll(x, n - 1, x.ndim - 1), pltpu.roll(x, 1, x.ndim - 1))


def _tile_lanes(t, n):
    return jnp.tile(t, (1, n // t.shape[-1]))


def gqa_prep_call(gqkv, qgain, kgain, cos_t, sin_t):
    T = gqkv.shape[0]
    nq = GQA_Q_HEADS * HEAD_DIM
    nk = GQA_KV_HEADS * HEAD_DIM
    rep = GQA_Q_HEADS // GQA_KV_HEADS

    def body(g_ref, qg_ref, kg_ref, c_ref, s_ref, qp_ref, kp_ref):
        cos_v, sin_v = c_ref[...], s_ref[...]
        xq = g_ref[:, :nq].astype(F32)
        yq = xq * lax.rsqrt(_head_mean(xq * xq) + EPS) * qg_ref[...]
        yq = (yq * _tile_lanes(cos_v, nq) + _swap_pairs(yq) * _tile_lanes(sin_v, nq)) * SCALE
        lo = _lane_lt64((TM, LANES))
        for p in range(GQA_Q_HEADS // 2):
            g = (2 * p) // rep
            chunk = yq[:, LANES * p:LANES * (p + 1)]
            rolled = _roll64(chunk)
            if g == 0:
                first, second = jnp.where(lo, chunk, 0.0), jnp.where(lo, rolled, 0.0)
            else:
                first, second = jnp.where(lo, 0.0, rolled), jnp.where(lo, 0.0, chunk)
            qp_ref[:, 2 * LANES * p:2 * LANES * p + LANES] = first.astype(BF16)
            qp_ref[:, 2 * LANES * p + LANES:2 * LANES * (p + 1)] = second.astype(BF16)
        xk = g_ref[:, nq:nq + nk].astype(F32)
        yk = xk * lax.rsqrt(_head_mean(xk * xk) + EPS) * kg_ref[...]
        kp_ref[...] = (yk * cos_v + _swap_pairs(yk) * sin_v).astype(BF16)

    return pl.pallas_call(
        body, name="gqa_prep", grid=(T // TM,),
        in_specs=[pl.BlockSpec((TM, gqkv.shape[1]), lambda i: (i, 0)),
                  pl.BlockSpec((1, nq), lambda i: (0, 0)), pl.BlockSpec((1, nk), lambda i: (0, 0)),
                  pl.BlockSpec((TM, nk), lambda i: (i, 0)), pl.BlockSpec((TM, nk), lambda i: (i, 0))],
        out_specs=[pl.BlockSpec((TM, GQA_Q_HEADS * LANES), lambda i: (i, 0)), pl.BlockSpec((TM, nk), lambda i: (i, 0))],
        out_shape=[jax.ShapeDtypeStruct((T, GQA_Q_HEADS * LANES), BF16), jax.ShapeDtypeStruct((T, nk), BF16)],
        compiler_params=_cp(("parallel",)),
    )(gqkv, qgain, kgain, cos_t, sin_t)


def gqa_prep_bwd_call(gqkv, qgain, kgain, cos_t, sin_t, dqp, dkp, dv):
    T = gqkv.shape[0]
    nq = GQA_Q_HEADS * HEAD_DIM
    nk = GQA_KV_HEADS * HEAD_DIM
    rep = GQA_Q_HEADS // GQA_KV_HEADS

    def norm_rope_bwd(x, gain, dy, cos_v, sin_v):
        dyn = dy * cos_v - _swap_pairs(dy) * sin_v
        r = lax.rsqrt(_head_mean(x * x) + EPS)
        xn = x * r
        dxn = dyn * gain
        dx = r * (dxn - xn * _head_mean(dxn * xn))
        return dx, dyn * xn

    def body(g_ref, qg_ref, kg_ref, c_ref, s_ref, dqp_ref, dkp_ref, dv_ref, dg_ref, sq_ref, sk_ref):
        i = pl.program_id(0)
        cos_v, sin_v = c_ref[...], s_ref[...]
        lo = _lane_lt64((TM, LANES))
        chunks = []
        for p in range(GQA_Q_HEADS // 2):
            g = (2 * p) // rep
            da = dqp_ref[:, 2 * LANES * p:2 * LANES * p + LANES]
            db = dqp_ref[:, 2 * LANES * p + LANES:2 * LANES * (p + 1)]
            if g == 0:
                chunks.append(jnp.where(lo, da, _roll64(db)))
            else:
                chunks.append(jnp.where(lo, _roll64(da), db))
        dyq = jnp.concatenate(chunks, axis=1) * SCALE
        dxq, gq_term = norm_rope_bwd(g_ref[:, :nq].astype(F32), qg_ref[...], dyq,
                                     _tile_lanes(cos_v, nq), _tile_lanes(sin_v, nq))
        dxk, gk_term = norm_rope_bwd(g_ref[:, nq:nq + nk].astype(F32), kg_ref[...], dkp_ref[...], cos_v, sin_v)
        dg_ref[:, :nq] = dxq.astype(BF16)
        dg_ref[:, nq:nq + nk] = dxk.astype(BF16)
        dg_ref[:, nq + nk:] = dv_ref[...].astype(BF16)
        _acc(sq_ref, _colsum8(gq_term), i == 0)
        _acc(sk_ref, _colsum8(gk_term), i == 0)

    W = gqkv.shape[1]
    return pl.pallas_call(
        body, name="gqa_prep_bwd", grid=(T // TM,),
        in_specs=[pl.BlockSpec((TM, W), lambda i: (i, 0)),
                  pl.BlockSpec((1, nq), lambda i: (0, 0)), pl.BlockSpec((1, nk), lambda i: (0, 0)),
                  pl.BlockSpec((TM, nk), lambda i: (i, 0)), pl.BlockSpec((TM, nk), lambda i: (i, 0)),
                  pl.BlockSpec((TM, GQA_Q_HEADS * LANES), lambda i: (i, 0)),
                  pl.BlockSpec((TM, nk), lambda i: (i, 0)), pl.BlockSpec((TM, nk), lambda i: (i, 0))],
        out_specs=[pl.BlockSpec((TM, W), lambda i: (i, 0)),
                   pl.BlockSpec((1, 8, nq), lambda i: (0, 0, 0)), pl.BlockSpec((1, 8, nk), lambda i: (0, 0, 0))],
        out_shape=[jax.ShapeDtypeStruct((T, W), BF16), jax.ShapeDtypeStruct((1, 8, nq), F32),
                   jax.ShapeDtypeStruct((1, 8, nk), F32)],
        compiler_params=_cp(("arbitrary",)),
    )(gqkv, qgain, kgain, cos_t, sin_t, dqp, dkp, dv)


GQ_TQ_FWD = 1024
GQ_TQ_BWD = 512
GQ_TK = 256
GQ_CHUNKS_PER_TRIP = 8
GQ_FORWARD_STEPS = 1


def _chunk_offset(j):
    return j * GQ_TK if isinstance(j, int) else pl.multiple_of(j * GQ_TK, GQ_TK)


def _gqa_blocks(n_lat, T, latent, tq_latent):
    if latent:
        tq = tq_latent if n_lat % tq_latent == 0 else GQ_TK
        return tq, 0, n_lat // tq
    return GQ_TK, n_lat // GQ_TK, (T - n_lat) // GQ_TK


def _sweep_keys(chunk, latent, n_lat_k, nk):
    if latent:
        per_trip = math.gcd(n_lat_k, GQ_CHUNKS_PER_TRIP)

        def trip(jj, carry):
            for t in range(per_trip):
                chunk(per_trip * jj + t)
            return carry

        lax.fori_loop(0, n_lat_k // per_trip, trip, 0)
    for j in range(n_lat_k, nk):
        chunk(j)


def _gqa_place(chunk, half, g):
    gv = jnp.broadcast_to(g, chunk.shape)
    placed = jnp.where(gv == half, chunk, _roll64(chunk))
    return jnp.where(_lane_half(chunk.shape) == gv, placed, jnp.zeros_like(placed))


GQA_SMALL_SCORE = 20.0


def gqa_scores_small(q_gain, k_gain):
    bound = SCALE * HEAD_DIM * jnp.max(jnp.abs(q_gain)) * jnp.max(jnp.abs(k_gain))
    return (bound <= GQA_SMALL_SCORE).astype(jnp.int32).reshape(1)


def gqa_fwd_call(qp, kp, gqkv, n_lat, scores_small, gather=None):
    first = _gqa_fwd_part(qp, kp, gqkv, n_lat, scores_small, True, None, gather)
    y, lse = _gqa_fwd_part(qp, kp, gqkv, n_lat, scores_small, False, (first[0], first[1]), None)
    return (y, lse) + tuple(first[2:])


def _gqa_fwd_part(qp, kp, gqkv, n_lat, scores_small, latent, prev, gather):
    T = qp.shape[0]
    rep = GQA_Q_HEADS // GQA_KV_HEADS
    GQ_TQ, q0, nq = _gqa_blocks(n_lat, T, latent, GQ_TQ_FWD)
    nk = T // GQ_TK
    n_lat_k = n_lat // GQ_TK
    v_blk = (GQA_Q_HEADS + GQA_KV_HEADS) * HEAD_DIM // LANES
    R = rep * GQ_TQ

    def body(*refs):
        if prev is not None:
            small_ref, q_ref, k_ref, v_ref, _, _, y_ref, lse_ref, m_sc, acc_sc = refs
        elif gather is None:
            small_ref, q_ref, k_ref, v_ref, y_ref, lse_ref, m_sc, acc_sc = refs
        else:
            nb = len(gather)
            small_ref, q_ref, k_ref, v_ref = refs[:4]
            pk_refs = refs[4:4 + nb]
            y_ref, lse_ref = refs[4 + nb:6 + nb]
            ga_refs = refs[6 + nb:6 + 2 * nb]
            m_sc, acc_sc, ssem, rsem = refs[6 + 2 * nb:]
        g = pl.program_id(0)
        qb = pl.program_id(1)
        if gather is not None:
            @pl.when((g == 0) & (qb == 0))
            def _():
                _ag_start(pk_refs, ga_refs, ssem, rsem)
        qs = jnp.concatenate([q_ref[:, LANES * i:LANES * (i + 1)] for i in range(rep)], axis=0)
        is_lat = latent
        nsub = GQ_TK // LANES
        scores_small = small_ref[0] == 1

        @pl.when(scores_small)
        def _():
            m_sc[...] = jnp.zeros((R, LANES), F32)

        @pl.when(jnp.logical_not(scores_small))
        def _():
            m_sc[...] = jnp.full((R, LANES), NEG, F32)

            def chunk_max(j):
                s = _dot_nt(qs, k_ref[pl.ds(_chunk_offset(j), GQ_TK), :])
                mm = s[:, :LANES]
                for t in range(1, nsub):
                    mm = jnp.maximum(mm, s[:, LANES * t:LANES * (t + 1)])
                m_sc[...] = jnp.maximum(m_sc[...], mm)

            _sweep_keys(chunk_max, is_lat, n_lat_k, nk)
            m_sc[...] = jnp.broadcast_to(jnp.max(m_sc[...], axis=-1, keepdims=True), (R, LANES))

        acc_sc[...] = jnp.zeros((R, LANES), F32)
        own_v = _lane_half((GQ_TK, LANES)) == jnp.broadcast_to(g, (GQ_TK, LANES))

        def chunk_acc(j):
            off = _chunk_offset(j)
            vc = v_ref[pl.ds(off, GQ_TK), :]
            v1 = jnp.where(own_v, vc, jnp.ones_like(vc))
            s = _dot_nt(qs, k_ref[pl.ds(off, GQ_TK), :])
            mb = m_sc[...]
            p = jnp.concatenate([jnp.exp(s[:, LANES * t:LANES * (t + 1)] - mb).astype(BF16) for t in range(nsub)],
                                axis=1)
            acc_sc[...] += _dot(p, v1)

        _sweep_keys(chunk_acc, is_lat, n_lat_k, nk)
        acc = acc_sc[...]
        own = _lane_half((R, LANES)) == jnp.broadcast_to(g, (R, LANES))
        l = jnp.where(own, _roll64(acc), acc)
        o = acc / l
        lse = m_sc[...] + jnp.log(l)
        lo = _lane_lt64((GQ_TQ, LANES))
        g0 = jnp.broadcast_to(g, (GQ_TQ, LANES)) == 0
        for c2 in range(rep // 2):
            oa = o[(2 * c2) * GQ_TQ:(2 * c2 + 1) * GQ_TQ]
            ob = o[(2 * c2 + 1) * GQ_TQ:(2 * c2 + 2) * GQ_TQ]
            left = jnp.where(g0, oa, _roll64(oa))
            right = jnp.where(g0, _roll64(ob), ob)
            y_ref[:, LANES * c2:LANES * (c2 + 1)] = jnp.where(lo, left, right).astype(BF16)
        for i in range(rep):
            lse_ref[i] = lse[i * GQ_TQ:(i + 1) * GQ_TQ]
        if gather is not None:
            @pl.when((g == GQA_KV_HEADS - 1) & (qb == max(nq - 1 - GQ_FORWARD_STEPS, 0)))
            def _():
                _ag_forward(pk_refs, ga_refs, ssem, rsem)

            @pl.when((g == GQA_KV_HEADS - 1) & (qb == nq - 1))
            def _():
                _ag_finish(pk_refs, ga_refs, ssem, rsem)

    in_specs = [pl.BlockSpec((GQ_TQ, rep * LANES), lambda g, i, s: (q0 + i, g)),
                pl.BlockSpec((T, LANES), lambda g, i, s: (0, 0)),
                pl.BlockSpec((T, LANES), lambda g, i, s: (0, v_blk))]
    out_specs = [pl.BlockSpec((GQ_TQ, rep * HEAD_DIM), lambda g, i, s: (q0 + i, g)),
                 pl.BlockSpec((rep, GQ_TQ, LANES), lambda g, i, s: (g, q0 + i, 0))]
    out_shape = [jax.ShapeDtypeStruct((T, GQA_Q_HEADS * HEAD_DIM), BF16),
                 jax.ShapeDtypeStruct((GQA_Q_HEADS, T, LANES), F32)]
    scratch = [pltpu.VMEM((R, LANES), F32), pltpu.VMEM((R, LANES), F32)]
    args = [scores_small, qp, kp, gqkv]
    aliases = {}
    name = "gqa_fwd" if latent else "gqa_fwd_ctx"
    if prev is not None:
        in_specs += [ANY, ANY]
        args += list(prev)
        aliases = {4: 0, 5: 1}
    if gather is not None:
        in_specs += [ANY] * len(gather)
        out_specs += [ANY] * len(gather)
        out_shape += [jax.ShapeDtypeStruct((N_CHIP,) + p.shape, p.dtype) for p in gather]
        scratch += [pltpu.SemaphoreType.DMA((AG_SEMS * len(gather),)), pltpu.SemaphoreType.DMA((AG_SEMS * len(gather),))]
        args += list(gather)
        name = "gqa_fwd_gather"
    return pl.pallas_call(
        body, name=name,
        grid_spec=pltpu.PrefetchScalarGridSpec(
            num_scalar_prefetch=1, grid=(GQA_KV_HEADS, nq), in_specs=in_specs, out_specs=out_specs,
            scratch_shapes=scratch),
        out_shape=out_shape, input_output_aliases=aliases,
        compiler_params=_cp(("arbitrary", "arbitrary") if gather is not None else ("parallel", "parallel"), VMEM_BIG),
    )(*args)


def gqa_bwd_call(qp, kp, gqkv, yb, dyb, lse, n_lat, exchange=None):
    first = _gqa_bwd_part(qp, kp, gqkv, yb, dyb, lse, n_lat, True, None, exchange)
    dq, dk_c, dv_c = _gqa_bwd_part(qp, kp, gqkv, yb, dyb, lse, n_lat, False, first[0], None)
    return (dq, first[1] + dk_c, first[2] + dv_c) + tuple(first[3:])


def _gqa_bwd_part(qp, kp, gqkv, yb, dyb, lse, n_lat, latent, prev_dq, exchange):
    T = qp.shape[0]
    rep = GQA_Q_HEADS // GQA_KV_HEADS
    GQ_TQ, q0, nq = _gqa_blocks(n_lat, T, latent, GQ_TQ_BWD)
    nk = T // GQ_TK
    n_lat_k = n_lat // GQ_TK
    v_blk = (GQA_Q_HEADS + GQA_KV_HEADS) * HEAD_DIM // LANES
    R = rep * GQ_TQ

    def body(*refs):
        if prev_dq is not None:
            q_ref, k_ref, v_ref, y_ref, dy_ref, lse_ref, _, dq_ref, dk_ref, dv_ref, dq_sc = refs
        elif exchange is None:
            q_ref, k_ref, v_ref, y_ref, dy_ref, lse_ref, dq_ref, dk_ref, dv_ref, dq_sc = refs
        else:
            nb = len(exchange)
            q_ref, k_ref, v_ref, y_ref, dy_ref, lse_ref = refs[:6]
            gp_refs = refs[6:6 + nb]
            dq_ref, dk_ref, dv_ref = refs[6 + nb:9 + nb]
            rx_refs = refs[9 + nb:9 + 2 * nb]
            dq_sc, ssem, rsem = refs[9 + 2 * nb:]
        g = pl.program_id(0)
        qb = pl.program_id(1)
        if exchange is not None:
            @pl.when((g == 0) & (qb == 0))
            def _():
                _rs_start(gp_refs, rx_refs, ssem, rsem)

        @pl.when((g == 0) & (qb == 0))
        def _():
            dk_ref[...] = jnp.zeros((T, LANES), F32)
            dv_ref[...] = jnp.zeros((T, LANES), F32)

        qs = jnp.concatenate([q_ref[:, LANES * i:LANES * (i + 1)] for i in range(rep)], axis=0)
        dos, deltas, lses = [], [], []
        for i in range(rep):
            c2, half = i // 2, i % 2
            dch = _gqa_place(dy_ref[:, LANES * c2:LANES * (c2 + 1)].astype(F32), half, g)
            ych = _gqa_place(y_ref[:, LANES * c2:LANES * (c2 + 1)].astype(F32), half, g)
            dos.append(dch.astype(BF16))
            deltas.append(jnp.sum(dch * ych, axis=-1, keepdims=True))
            lses.append(lse_ref[i])
        do_s = jnp.concatenate(dos, axis=0)
        delta = jnp.broadcast_to(jnp.concatenate(deltas, axis=0), (R, LANES))
        lse_v = jnp.concatenate(lses, axis=0)
        dq_sc[...] = jnp.zeros((R, LANES), F32)
        nsub = GQ_TK // LANES

        def chunk(j):
            off = _chunk_offset(j)
            kc = k_ref[pl.ds(off, GQ_TK), :]
            vc = v_ref[pl.ds(off, GQ_TK), :]
            s = _dot_nt(qs, kc)
            dp = _dot_nt(do_s, vc)
            ps, dss = [], []
            for t in range(nsub):
                pt = jnp.exp(s[:, LANES * t:LANES * (t + 1)] - lse_v)
                ps.append(pt.astype(BF16))
                dss.append((pt * (dp[:, LANES * t:LANES * (t + 1)] - delta)).astype(BF16))
            p = jnp.concatenate(ps, axis=1)
            ds = jnp.concatenate(dss, axis=1)
            dv_ref[pl.ds(off, GQ_TK), :] += _dot_tn(p, do_s)
            dk_ref[pl.ds(off, GQ_TK), :] += _dot_tn(ds, qs)
            dq_sc[...] += _dot(ds, kc)

        _sweep_keys(chunk, latent, n_lat_k, nk)
        for i in range(rep):
            dq_ref[:, LANES * i:LANES * (i + 1)] = dq_sc[i * GQ_TQ:(i + 1) * GQ_TQ, :]
        if exchange is not None:
            @pl.when((g == GQA_KV_HEADS - 1) & (qb == nq - 1))
            def _():
                _rs_finish(gp_refs, rx_refs, ssem, rsem)

    whole = pl.BlockSpec((T, LANES), lambda g, i: (0, 0))
    in_specs = [pl.BlockSpec((GQ_TQ, rep * LANES), lambda g, i: (q0 + i, g)),
                whole,
                pl.BlockSpec((T, LANES), lambda g, i: (0, v_blk)),
                pl.BlockSpec((GQ_TQ, rep * HEAD_DIM), lambda g, i: (q0 + i, g)),
                pl.BlockSpec((GQ_TQ, rep * HEAD_DIM), lambda g, i: (q0 + i, g)),
                pl.BlockSpec((rep, GQ_TQ, LANES), lambda g, i: (g, q0 + i, 0))]
    out_specs = [pl.BlockSpec((GQ_TQ, rep * LANES), lambda g, i: (q0 + i, g)), whole, whole]
    out_shape = [jax.ShapeDtypeStruct((T, GQA_Q_HEADS * LANES), F32),
                 jax.ShapeDtypeStruct((T, LANES), F32), jax.ShapeDtypeStruct((T, LANES), F32)]
    scratch = [pltpu.VMEM((R, LANES), F32)]
    args = [qp, kp, gqkv, yb, dyb, lse]
    aliases = {}
    name = "gqa_bwd" if latent else "gqa_bwd_ctx"
    if prev_dq is not None:
        in_specs.append(ANY)
        args.append(prev_dq)
        aliases = {6: 0}
    if exchange is not None:
        in_specs += [ANY] * len(exchange)
        out_specs += [ANY] * len(exchange)
        out_shape += [_rs_out_shape(g) for g in exchange]
        scratch += [pltpu.SemaphoreType.DMA((RS_SEMS * len(exchange),)), pltpu.SemaphoreType.DMA((RS_SEMS * len(exchange),))]
        args += list(exchange)
        name = "gqa_bwd_exchange"
    return pl.pallas_call(
        body, name=name, grid=(GQA_KV_HEADS, nq),
        in_specs=in_specs, out_specs=out_specs, out_shape=out_shape, scratch_shapes=scratch,
        input_output_aliases=aliases,
        compiler_params=_cp(("arbitrary", "arbitrary"), VMEM_BIG),
    )(*args)


def _na_tables(rows):
    kh = min(NA_WIN_H, rows)
    assert kh == NA_WIN_H and rows >= NA_KROWS and rows % NA_ROWS == 0
    a = np.zeros((3, NA_ROWS, NA_KROWS, 2 * NA_WIN_H - 1), np.float32)
    for v, r0 in enumerate((0, NA_ROWS, rows - NA_ROWS)):
        ks = min(max(r0 - NA_WIN_H // 2, 0), rows - NA_KROWS)
        for rq in range(NA_ROWS):
            r = r0 + rq
            rs = min(max(r - kh // 2, 0), rows - kh)
            for rk in range(NA_KROWS):
                kr = ks + rk
                if rs <= kr < rs + kh:
                    a[v, rq, rk, kr - r + NA_WIN_H - 1] = 1.0
    c = np.zeros((GRID_W, GRID_W, 2 * NA_WIN_W - 1), np.float32)
    for wq in range(GRID_W):
        cs = min(max(wq - NA_WIN_W // 2, 0), GRID_W - NA_WIN_W)
        for wk in range(cs, cs + NA_WIN_W):
            c[wq, wk, min(max(wk - wq, -(NA_WIN_W - 1)), NA_WIN_W - 1) + NA_WIN_W - 1] = 1.0
    return a, c


NA_DR = 2 * NA_WIN_H - 1


def _na_tile_index(rows):
    a, _ = _na_tables(rows)
    idx = np.where(a.sum(-1) > 0, a.argmax(-1), NA_DR).astype(np.int32)
    return jnp.asarray(idx.reshape(-1))


def na_bias_table(rpb, rows):
    _, c = _na_tables(rows)
    H = rpb.shape[0]
    KW = NA_KROWS * GRID_W
    c2 = np.concatenate([c, c], axis=1)
    t1 = jnp.einsum("hde,wue->hdwu", rpb, jnp.asarray(c2), precision=lax.Precision.HIGHEST)
    t1 = jnp.where(jnp.asarray(c2.sum(-1) > 0)[None, None], t1, NEG)
    t1 = jnp.concatenate([t1, jnp.full((H, 1, GRID_W, LANES), NEG, F32)], axis=1)

    def body(idx_ref, t_ref, o_ref):
        v = pl.program_id(1)
        lo = _lane_lt64((GRID_W, LANES))
        for q in range(NA_ROWS):
            for kp in range(NA_KROWS // 2):
                base = (v * NA_ROWS + q) * NA_KROWS + 2 * kp
                tile = jnp.where(lo, t_ref[0, idx_ref[base]], t_ref[0, idx_ref[base + 1]])
                o_ref[0, 0, GRID_W * q:GRID_W * (q + 1), LANES * kp:LANES * (kp + 1)] = tile

    return pl.pallas_call(
        body, name="na_bias_table",
        grid_spec=pltpu.PrefetchScalarGridSpec(
            num_scalar_prefetch=1, grid=(H, 3),
            in_specs=[pl.BlockSpec((1, NA_DR + 1, GRID_W, LANES), lambda h, v, i: (h, 0, 0, 0))],
            out_specs=pl.BlockSpec((1, 1, TM, KW), lambda h, v, i: (h, v, 0, 0))),
        out_shape=jax.ShapeDtypeStruct((H, 3, TM, KW), F32),
        compiler_params=_cp(("parallel", "parallel")),
    )(_na_tile_index(rows), t1)


def na_bias_grad(dbias, rows):
    _, c = _na_tables(rows)
    H = dbias.shape[0]
    KW = NA_KROWS * GRID_W

    def body(idx_ref, d_ref, o_ref):
        v = pl.program_id(1)
        lo = _lane_lt64((GRID_W, LANES))

        @pl.when(v == 0)
        def _():
            o_ref[...] = jnp.zeros((1, NA_DR + 1, GRID_W, LANES), F32)

        for q in range(NA_ROWS):
            for kp in range(NA_KROWS // 2):
                base = (v * NA_ROWS + q) * NA_KROWS + 2 * kp
                tile = d_ref[0, 0, GRID_W * q:GRID_W * (q + 1), LANES * kp:LANES * (kp + 1)]
                o_ref[0, idx_ref[base]] += jnp.where(lo, tile, 0.0)
                o_ref[0, idx_ref[base + 1]] += jnp.where(lo, 0.0, tile)

    dt = pl.pallas_call(
        body, name="na_bias_grad",
        grid_spec=pltpu.PrefetchScalarGridSpec(
            num_scalar_prefetch=1, grid=(H, 3),
            in_specs=[pl.BlockSpec((1, 1, TM, KW), lambda h, v, i: (h, v, 0, 0))],
            out_specs=pl.BlockSpec((1, NA_DR + 1, GRID_W, LANES), lambda h, v, i: (h, 0, 0, 0))),
        out_shape=jax.ShapeDtypeStruct((H, NA_DR + 1, GRID_W, LANES), F32),
        compiler_params=_cp(("parallel", "arbitrary")),
    )(_na_tile_index(rows), dbias)
    dt = dt[:, :NA_DR, :, :GRID_W] + dt[:, :NA_DR, :, GRID_W:]
    return jnp.einsum("hdwu,wue->hde", dt, jnp.asarray(c), precision=lax.Precision.HIGHEST)


def _na_variant(blk, n_lat_blk):
    return jnp.where(blk == 0, 0, jnp.where(blk >= n_lat_blk - 1, 2, 1))


def _na_kstart(blk, rows):
    r0 = blk * NA_ROWS
    ks = jnp.clip(r0 - NA_WIN_H // 2, 0, rows - NA_KROWS)
    return pl.multiple_of(ks * GRID_W, GRID_W)


def na_fwd_call(naqkv, bias, n_lat):
    T = naqkv.shape[0]
    nb = T // TM
    n_lat_blk = n_lat // TM
    rows = n_lat // GRID_W
    n_ctx = T - n_lat
    KW = NA_KROWS * GRID_W
    npair = NA_HEADS // 2

    def body(q_ref, k_ref, v_ref, b_ref, y_ref, lse_ref):
        blk = pl.program_id(1)
        half = _lane_half((TM, LANES))
        lo = half == 0
        q = q_ref[...].astype(F32) * SCALE
        kctx = k_ref[pl.ds(n_lat, n_ctx), :]
        vctx = v_ref[pl.ds(n_lat, n_ctx), :]

        @pl.when(blk < n_lat_blk)
        def _():
            off = _na_kstart(blk, rows)
            kwin = k_ref[pl.ds(off, KW), :]
            vwin = v_ref[pl.ds(off, KW), :]
            outs = []
            for hh in range(2):
                qm = jnp.where(half == hh, q, 0.0).astype(BF16)
                sw = _dot_nt(qm, kwin) + b_ref[hh, 0]
                sc = _dot_nt(qm, kctx)
                m = jnp.maximum(jnp.max(sw, axis=-1, keepdims=True), jnp.max(sc, axis=-1, keepdims=True))
                pw = jnp.exp(sw - m)
                pc = jnp.exp(sc - m)
                l = jnp.sum(pw, axis=-1, keepdims=True) + jnp.sum(pc, axis=-1, keepdims=True)
                outs.append((_dot(pw.astype(BF16), vwin) + _dot(pc.astype(BF16), vctx)) / l)
                lse_ref[hh] = jnp.broadcast_to(m + jnp.log(l), (TM, LANES))
            y_ref[...] = jnp.where(lo, outs[0], outs[1]).astype(BF16)

        @pl.when(blk >= n_lat_blk)
        def _():
            outs = []
            for hh in range(2):
                qm = jnp.where(half == hh, q, 0.0).astype(BF16)
                sc = _dot_nt(qm, kctx)
                m = jnp.max(sc, axis=-1, keepdims=True)
                pc = jnp.exp(sc - m)
                l = jnp.sum(pc, axis=-1, keepdims=True)
                outs.append(_dot(pc.astype(BF16), vctx) / l)
                lse_ref[hh] = jnp.broadcast_to(m + jnp.log(l), (TM, LANES))
            y_ref[...] = jnp.where(lo, outs[0], outs[1]).astype(BF16)

    return pl.pallas_call(
        body, name="na_fwd", grid=(npair, nb),
        in_specs=[pl.BlockSpec((TM, LANES), lambda p, b: (b, p)),
                  pl.BlockSpec((T, LANES), lambda p, b: (0, npair + p)),
                  pl.BlockSpec((T, LANES), lambda p, b: (0, 2 * npair + p)),
                  pl.BlockSpec((2, 1, TM, KW), lambda p, b: (p, _na_variant(b, n_lat_blk), 0, 0))],
        out_specs=[pl.BlockSpec((TM, LANES), lambda p, b: (b, p)),
                   pl.BlockSpec((2, TM, LANES), lambda p, b: (p, b, 0))],
        out_shape=[jax.ShapeDtypeStruct((T, NA_HEADS * HEAD_DIM), BF16),
                   jax.ShapeDtypeStruct((NA_HEADS, T, LANES), F32)],
        compiler_params=_cp(("parallel", "parallel"), VMEM_BIG),
    )(naqkv, naqkv, naqkv, bias)


def na_bwd_call(naqkv, bias, ya, dya, lse, n_lat):
    T = naqkv.shape[0]
    nb = T // TM
    n_lat_blk = n_lat // TM
    rows = n_lat // GRID_W
    n_ctx = T - n_lat
    KW = NA_KROWS * GRID_W
    npair = NA_HEADS // 2

    def body(q_ref, k_ref, v_ref, b_ref, y_ref, dy_ref, lse_ref, dq_ref, dk_ref, dv_ref, db_ref, dk_sc, dv_sc):
        blk = pl.program_id(1)
        half = _lane_half((TM, LANES))
        lo = half == 0
        q = q_ref[...].astype(F32) * SCALE
        kctx = k_ref[pl.ds(n_lat, n_ctx), :]
        vctx = v_ref[pl.ds(n_lat, n_ctx), :]
        dyv = dy_ref[...].astype(F32)
        yv = y_ref[...].astype(F32)

        @pl.when(blk == 0)
        def _():
            dk_sc[...] = jnp.zeros((T, LANES), F32)
            dv_sc[...] = jnp.zeros((T, LANES), F32)

        @pl.when(blk < n_lat_blk)
        def _():
            off = _na_kstart(blk, rows)
            kwin = k_ref[pl.ds(off, KW), :]
            vwin = v_ref[pl.ds(off, KW), :]
            first = (blk == 0) | (blk == 1) | (blk == n_lat_blk - 1)
            dqs = []
            for hh in range(2):
                sel = half == hh
                qm = jnp.where(sel, q, 0.0).astype(BF16)
                dom = jnp.where(sel, dyv, 0.0)
                delta = jnp.sum(dom * yv, axis=-1, keepdims=True)
                dom = dom.astype(BF16)
                lse_v = lse_ref[hh][:, 0:1]
                pw = jnp.exp(_dot_nt(qm, kwin) + b_ref[hh, 0] - lse_v)
                pc = jnp.exp(_dot_nt(qm, kctx) - lse_v)
                dsw = pw * (_dot_nt(dom, vwin) - delta)
                dsc = pc * (_dot_nt(dom, vctx) - delta)

                @pl.when(first)
                def _():
                    db_ref[hh, 0] = dsw

                @pl.when(jnp.logical_not(first))
                def _():
                    db_ref[hh, 0] += dsw

                dsw = dsw.astype(BF16)
                dsc = dsc.astype(BF16)
                dv_sc[pl.ds(off, KW), :] += _dot_tn(pw.astype(BF16), dom)
                dv_sc[pl.ds(n_lat, n_ctx), :] += _dot_tn(pc.astype(BF16), dom)
                dk_sc[pl.ds(off, KW), :] += _dot_tn(dsw, qm)
                dk_sc[pl.ds(n_lat, n_ctx), :] += _dot_tn(dsc, qm)
                dqs.append((_dot(dsw, kwin) + _dot(dsc, kctx)) * SCALE)
            dq_ref[...] = jnp.where(lo, dqs[0], dqs[1]).astype(BF16)

        @pl.when(blk >= n_lat_blk)
        def _():
            dqs = []
            for hh in range(2):
                sel = half == hh
                qm = jnp.where(sel, q, 0.0).astype(BF16)
                dom = jnp.where(sel, dyv, 0.0)
                delta = jnp.sum(dom * yv, axis=-1, keepdims=True)
                dom = dom.astype(BF16)
                pc = jnp.exp(_dot_nt(qm, kctx) - lse_ref[hh][:, 0:1])
                dsc = (pc * (_dot_nt(dom, vctx) - delta)).astype(BF16)
                dv_sc[pl.ds(n_lat, n_ctx), :] += _dot_tn(pc.astype(BF16), dom)
                dk_sc[pl.ds(n_lat, n_ctx), :] += _dot_tn(dsc, qm)
                dqs.append(_dot(dsc, kctx) * SCALE)
            dq_ref[...] = jnp.where(lo, dqs[0], dqs[1]).astype(BF16)

        @pl.when(blk == nb - 1)
        def _():
            dk_ref[...] = dk_sc[...].astype(BF16)
            dv_ref[...] = dv_sc[...].astype(BF16)

    blkspec = pl.BlockSpec((TM, LANES), lambda p, b: (b, p))
    col = pl.BlockSpec((T, LANES), lambda p, b: (0, p))
    bspec = pl.BlockSpec((2, 1, TM, KW), lambda p, b: (p, _na_variant(b, n_lat_blk), 0, 0))
    dq, dk, dv, db = pl.pallas_call(
        body, name="na_bwd", grid=(npair, nb),
        in_specs=[blkspec,
                  pl.BlockSpec((T, LANES), lambda p, b: (0, npair + p)),
                  pl.BlockSpec((T, LANES), lambda p, b: (0, 2 * npair + p)),
                  bspec, blkspec, blkspec,
                  pl.BlockSpec((2, TM, LANES), lambda p, b: (p, b, 0))],
        out_specs=[blkspec, col, col, bspec],
        out_shape=[jax.ShapeDtypeStruct((T, NA_HEADS * HEAD_DIM), BF16)] * 3
        + [jax.ShapeDtypeStruct((NA_HEADS, 3, TM, KW), F32)],
        scratch_shapes=[pltpu.VMEM((T, LANES), F32), pltpu.VMEM((T, LANES), F32)],
        compiler_params=_cp(("arbitrary", "arbitrary"), VMEM_BIG),
    )(naqkv, naqkv, naqkv, bias, ya, dya, lse)
    return jnp.concatenate([dq, dk, dv], axis=1), db


def rope_tables(n_lat, n_ctx):
    t = jnp.arange(n_lat)
    row = (t // GRID_W).astype(F32)
    col = (t % GRID_W).astype(F32)
    half = HEAD_DIM // 2
    inv = ROPE_THETA ** (-jnp.arange(0, half, 2, dtype=F32) / half)
    ang = jnp.concatenate([row[:, None] * inv, col[:, None] * inv], axis=-1)
    cos = jnp.repeat(jnp.cos(ang), 2, axis=-1)
    sin = jnp.repeat(jnp.sin(ang), 2, axis=-1) * jnp.tile(jnp.array([-1.0, 1.0], F32), half)
    cos = jnp.concatenate([cos, jnp.ones((n_ctx, HEAD_DIM), F32)], axis=0)
    sin = jnp.concatenate([sin, jnp.zeros((n_ctx, HEAD_DIM), F32)], axis=0)
    return jnp.tile(cos, (1, 2)), jnp.tile(sin, (1, 2))


def _row(v):
    return v.reshape(1, -1)


def device_fwd_bwd(xs, target, modv, w, small, n_lat, ex=None):
    T, D = xs.shape
    n_ctx = T - n_lat
    L = modv.shape[0]
    nlt = n_lat // TM
    rows = n_lat // GRID_W
    cos_t, sin_t = rope_tables(n_lat, n_ctx)
    qg = [jnp.tile(_row(small["q_gain"][l]), (1, GQA_Q_HEADS)) for l in range(L)]
    kg = [jnp.tile(_row(small["k_gain"][l]), (1, GQA_KV_HEADS)) for l in range(L)]
    bias = [na_bias_table(small["na_rpb"][l], rows) for l in range(L)]

    later = {}

    def wt(name, l):
        if ex is None or (name == "w_in" and l == 0):
            return w[name][l]
        return later[name][l - 1] if name == "w_in" else later[name][l]

    saved = []
    x = xs
    h = norm_mod_call(x, _row(small["norm1"][0]), modv[0], nlt, 0)
    for l in range(L):
        naqkv, gqkv, gates = qkv_call(h, wt("w_in", l))
        ya, lse_a = na_fwd_call(naqkv, bias[l], n_lat)
        qp, kp = gqa_prep_call(gqkv, qg[l], kg[l], cos_t, sin_t)
        small_scores = gqa_scores_small(small["q_gain"][l], small["k_gain"][l])
        if ex is not None and l == 0:
            yb, lse_b, *gathered = gqa_fwd_call(qp, kp, gqkv, n_lat, small_scores, gather=ex.rest())
            w_in_later, others = ex.rest_whole(gathered)
            later.update(w_in_later)
            later.update(others)
        else:
            yb, lse_b = gqa_fwd_call(qp, kp, gqkv, n_lat, small_scores)
        merged, pa, pb = merge_call(ya, yb, gates, wt("w_pa", l), wt("w_pb", l))
        x1, osave, h2 = wo_call(merged, wt("w_o", l), x, modv[l], _row(small["norm2"][l]), nlt)
        act, a, u = ffn_in_call(h2, wt("w_ffn_in", l))
        if l + 1 < L:
            x2, fsave, hn = ffn_out_call(act, wt("w_ffn_out", l), x1, modv[l], nlt,
                                         _row(small["norm1"][l + 1]), modv[l + 1])
        else:
            x2, fsave = ffn_out_call(act, wt("w_ffn_out", l), x1, modv[l], nlt)
            hn = None
        saved.append(dict(x=x, h=h, naqkv=naqkv, gqkv=gqkv, gates=gates, ya=ya, lse_a=lse_a, qp=qp, kp=kp,
                          yb=yb, lse_b=lse_b, merged=merged, pa=pa, pb=pb, x1=x1, osave=osave, h2=h2,
                          act=act, a=a, u=u, fsave=fsave))
        x, h = x2, hn

    dx, d_final, loss_part = loss_call(x, _row(small["final_norm"]), target, nlt)

    gw = {k: [None] * L for k in ("w_in", "w_pa", "w_pb", "w_o", "w_ffn_in", "w_ffn_out")}
    gs = {k: [None] * L for k in ("norm1", "norm2", "na_rpb", "q_gain", "k_gain")}
    dmod = [None] * L
    fold = lambda s: s.sum(axis=1)
    for l in reversed(range(L)):
        s = saved[l]
        da, du, df, s_g2 = dact_call(dx, modv[l], s["fsave"], wt("w_ffn_out", l), s["a"], s["u"], nlt)
        gw["w_ffn_out"][l] = wgrad_call(s["act"], df, "wgrad_ffn_out")
        dx1, s_sh2, s_sc2, s_n2 = dh2_call(da, du, wt("w_ffn_in", l), s["x1"], modv[l], _row(small["norm2"][l]), dx, nlt)
        gw["w_ffn_in"][l] = jnp.concatenate(
            [wgrad_call(s["h2"], da, "wgrad_ffn_in_a"), wgrad_call(s["h2"], du, "wgrad_ffn_in_u")], axis=1)
        dpa, dpb, dgates, do, s_g1 = dmerged_call(dx1, modv[l], s["osave"], wt("w_o", l), s["pa"], s["pb"], s["gates"], nlt)
        gw["w_o"][l] = wgrad_call(s["merged"], do, "wgrad_o")
        dya, dyb = dy_call(dpa, dpb, wt("w_pa", l), wt("w_pb", l))
        gw["w_pa"][l] = wgrad_call(s["ya"], dpa, "wgrad_pa")
        gw["w_pb"][l] = wgrad_call(s["yb"], dpb, "wgrad_pb")
        dna, dbias = na_bwd_call(s["naqkv"], bias[l], s["ya"], dya, s["lse_a"], n_lat)
        gs["na_rpb"][l] = na_bias_grad(dbias, rows)
        if ex is not None:
            ex.pack_grads(l, {k: v[l] for k, v in gw.items() if k != "w_in"}, [gi for gi in range(len(GROUPS)) if gi != G_IN])
            keys = [(l, gi) for gi in range(len(GROUPS)) if gi != G_IN] + ([(l + 1, G_IN)] if l + 1 < L else [])
            dqp, dkp, dv, *got = gqa_bwd_call(s["qp"], s["kp"], s["gqkv"], s["yb"], dyb, s["lse_b"], n_lat,
                                              exchange=[ex.partials[k] for k in keys])
            ex.received.update(zip(keys, got))
        else:
            dqp, dkp, dv = gqa_bwd_call(s["qp"], s["kp"], s["gqkv"], s["yb"], dyb, s["lse_b"], n_lat)
        dg, s_qg, s_kg = gqa_prep_bwd_call(s["gqkv"], qg[l], kg[l], cos_t, sin_t, dqp, dkp, dv)
        gs["q_gain"][l] = s_qg.reshape(8 * GQA_Q_HEADS, HEAD_DIM).sum(axis=0)
        gs["k_gain"][l] = s_kg.reshape(8 * GQA_KV_HEADS, HEAD_DIM).sum(axis=0)
        dx, s_sh1, s_sc1, s_n1 = dh_call(dna, dg, dgates, wt("w_in", l), s["x"], modv[l], _row(small["norm1"][l]), dx1, nlt)
        gw["w_in"][l] = jnp.concatenate(
            [wgrad_call(s["h"], dna, "wgrad_in_na"), wgrad_call(s["h"], dg, "wgrad_in_g"),
             wgrad_call(s["h"], dgates, "wgrad_in_gate")], axis=1)
        gs["norm1"][l] = fold(s_n1).sum(axis=0)
        gs["norm2"][l] = fold(s_n2).sum(axis=0)
        dmod[l] = jnp.stack([s_sh1, s_sc1, s_g1, s_sh2, s_sc2, s_g2], axis=1).sum(axis=2)
        if ex is not None:
            ex.pack_grads(l, {"w_in": gw["w_in"][l]}, [G_IN])

    if ex is not None:
        ex.received[(0, G_IN)] = grads_exchange([ex.partials[(0, G_IN)]])[0]
        gw = None
    else:
        gw = {k: jnp.stack(v) for k, v in gw.items()}
    gs = {k: jnp.stack(v) for k, v in gs.items()}
    gs["final_norm"] = d_final.sum(axis=(0, 1))
    return loss_part, dx, gw, jnp.stack(dmod), gs


N_DEV = 8
N_CHIP = 4
ANY = pl.BlockSpec(memory_space=pl.ANY)


def _place():
    x, y, c = lax.axis_index("x"), lax.axis_index("y"), lax.axis_index("c")
    chips = [(1 - x, y), (x, 1 - y), (1 - x, 1 - y)]
    return x, y, c, chips


def small_all_gather(v, name):
    m_per, n = v.shape
    assert m_per % 8 == 0

    def body(x_ref, out_ref, send_sems, recv_sems, local_sem):
        x, y, c, chips = _place()
        me, sibling = (x, y, c), (x, y, 1 - c)

        def rows(px, py, pc):
            return out_ref.at[pl.ds(pl.multiple_of((4 * px + 2 * py + pc) * m_per, 8), m_per), :]

        def copy(k, block, to, src=None):
            return pltpu.make_async_remote_copy(
                src_ref=rows(*block) if src is None else src, dst_ref=rows(*block),
                send_sem=send_sems.at[k], recv_sem=recv_sems.at[k], device_id=to, device_id_type=MESH)

        mine = pltpu.make_async_copy(x_ref, rows(*me), local_sem)
        mine.start()
        first = [copy(0, me, sibling, src=x_ref)]
        first += [copy(1 + j, me, (*chip, c), src=x_ref) for j, chip in enumerate(chips)]
        for cp in first:
            cp.start()
        passed = [copy(4 + j, (*chip, c), sibling) for j, chip in enumerate(chips)]
        for j, chip in enumerate(chips):
            copy(1 + j, (*chip, c), me).wait_recv()
            passed[j].start()
        copy(0, sibling, me).wait_recv()
        for j, chip in enumerate(chips):
            copy(4 + j, (*chip, 1 - c), me).wait_recv()
        for cp in first + passed:
            cp.wait_send()
        mine.wait()

    out = pl.pallas_call(
        body, name=name,
        out_shape=jax.ShapeDtypeStruct((N_DEV * m_per, n), v.dtype),
        in_specs=[pl.BlockSpec(memory_space=pltpu.VMEM)],
        out_specs=pl.BlockSpec(memory_space=pltpu.VMEM),
        scratch_shapes=[pltpu.SemaphoreType.DMA((7,)), pltpu.SemaphoreType.DMA((7,)), pltpu.SemaphoreType.DMA],
    )(v)
    return out.reshape(N_DEV, m_per, n)


def _ag_copies(in_ref, out_ref, send_sems, recv_sems, base=0):
    x, y, c, chips = _place()
    me, sibling = (x, y, c), (x, y, 1 - c)

    def blk(px, py, half):
        return out_ref.at[2 * px + py, half]

    def copy(k, block, to, src=None):
        return pltpu.make_async_remote_copy(
            src_ref=blk(*block) if src is None else src, dst_ref=blk(*block),
            send_sem=send_sems.at[base + k], recv_sem=recv_sems.at[base + k], device_id=to, device_id_type=MESH)

    first = [copy(j, (x, y, c), (*chip, c), src=in_ref.at[c]) for j, chip in enumerate(chips)]
    passed = [copy(3 + j, (*chip, c), sibling) for j, chip in enumerate(chips)]
    arrivals = [copy(j, (*chip, c), me) for j, chip in enumerate(chips)]
    forwarded = [copy(3 + j, (*chip, 1 - c), me) for j, chip in enumerate(chips)]
    return first, passed, arrivals, forwarded


AG_SEMS = 6
RS_SEMS = 7


def _ag_start(in_refs, out_refs, send_sems, recv_sems):
    for b, (i, o) in enumerate(zip(in_refs, out_refs)):
        for cp in _ag_copies(i, o, send_sems, recv_sems, AG_SEMS * b)[0]:
            cp.start()


def _ag_forward(in_refs, out_refs, send_sems, recv_sems):
    for b, (i, o) in enumerate(zip(in_refs, out_refs)):
        _, passed, arrivals, _ = _ag_copies(i, o, send_sems, recv_sems, AG_SEMS * b)
        for arrived, onward in zip(arrivals, passed):
            arrived.wait_recv()
            onward.start()


def _ag_finish(in_refs, out_refs, send_sems, recv_sems):
    for b, (i, o) in enumerate(zip(in_refs, out_refs)):
        first, passed, _, forwarded = _ag_copies(i, o, send_sems, recv_sems, AG_SEMS * b)
        for cp in forwarded:
            cp.wait_recv()
        for cp in first + passed:
            cp.wait_send()


def place_own_block(gathered, packed):
    chip = 2 * lax.axis_index("x") + lax.axis_index("y")
    return lax.dynamic_update_slice(gathered, packed[None], (chip,) + (0,) * packed.ndim)


def weights_all_gather(packed):
    n = len(packed)

    def body(*refs):
        _ag_start(refs[:n], refs[n:2 * n], refs[2 * n], refs[2 * n + 1])
        _ag_forward(refs[:n], refs[n:2 * n], refs[2 * n], refs[2 * n + 1])
        _ag_finish(refs[:n], refs[n:2 * n], refs[2 * n], refs[2 * n + 1])

    outs = pl.pallas_call(
        body, name="weights_all_gather",
        out_shape=[jax.ShapeDtypeStruct((N_CHIP,) + p.shape, p.dtype) for p in packed],
        in_specs=[ANY] * n, out_specs=[ANY] * n,
        scratch_shapes=[pltpu.SemaphoreType.DMA((AG_SEMS * n,)), pltpu.SemaphoreType.DMA((AG_SEMS * n,))],
    )(*packed)
    return [place_own_block(o, p) for o, p in zip(outs, packed)]


def _rs_copies(gp_ref, rx_ref, send_sems, recv_sems, base=0):
    x, y, c, chips = _place()
    sends = [pltpu.make_async_remote_copy(src_ref=gp_ref.at[2 * x + y, 1 - c], dst_ref=rx_ref.at[0],
                                          send_sem=send_sems.at[base], recv_sem=recv_sems.at[base],
                                          device_id=(x, y, 1 - c), device_id_type=MESH)]
    for r, (px, py) in enumerate(chips):
        for h in range(2):
            sends.append(pltpu.make_async_remote_copy(
                src_ref=gp_ref.at[2 * px + py, h], dst_ref=rx_ref.at[1 + 2 * r + c],
                send_sem=send_sems.at[base + 1 + 2 * r + h], recv_sem=recv_sems.at[base + 1 + 2 * r + c],
                device_id=(px, py, h), device_id_type=MESH))
    arrivals = [pltpu.make_async_remote_copy(src_ref=gp_ref.at[0, 0], dst_ref=rx_ref.at[s],
                                             send_sem=send_sems.at[base + s], recv_sem=recv_sems.at[base + s],
                                             device_id=(x, y, c), device_id_type=MESH) for s in range(RS_SEMS)]
    return sends, arrivals


def _rs_start(gp_refs, rx_refs, send_sems, recv_sems):
    for b, (gp, rx) in enumerate(zip(gp_refs, rx_refs)):
        for cp in _rs_copies(gp, rx, send_sems, recv_sems, RS_SEMS * b)[0]:
            cp.start()


def _rs_finish(gp_refs, rx_refs, send_sems, recv_sems):
    for b, (gp, rx) in enumerate(zip(gp_refs, rx_refs)):
        sends, arrivals = _rs_copies(gp, rx, send_sems, recv_sems, RS_SEMS * b)
        for cp in arrivals:
            cp.wait_recv()
        for cp in sends:
            cp.wait_send()


def _rs_out_shape(gp):
    return jax.ShapeDtypeStruct((RS_SEMS,) + gp.shape[2:], gp.dtype)


def grads_exchange(gps):
    n = len(gps)

    def body(*refs):
        _rs_start(refs[:n], refs[n:2 * n], refs[2 * n], refs[2 * n + 1])
        _rs_finish(refs[:n], refs[n:2 * n], refs[2 * n], refs[2 * n + 1])

    return pl.pallas_call(
        body, name="grads_exchange", out_shape=[_rs_out_shape(g) for g in gps],
        in_specs=[ANY] * n, out_specs=[ANY] * n,
        scratch_shapes=[pltpu.SemaphoreType.DMA((RS_SEMS * n,)), pltpu.SemaphoreType.DMA((RS_SEMS * n,))],
    )(*gps)


def grads_join_halves(ghs):
    n = len(ghs)

    def body(*refs):
        x, y, c, _ = _place()
        cps = [pltpu.make_async_remote_copy(src_ref=refs[b], dst_ref=refs[n + b], send_sem=refs[2 * n].at[b],
                                            recv_sem=refs[2 * n + 1].at[b], device_id=(x, y, 1 - c),
                                            device_id_type=MESH) for b in range(n)]
        for cp in cps:
            cp.start()
        for cp in cps:
            cp.wait()

    others = pl.pallas_call(
        body, name="grads_join_halves", out_shape=[jax.ShapeDtypeStruct(g.shape, g.dtype) for g in ghs],
        in_specs=[ANY] * n, out_specs=[ANY] * n,
        scratch_shapes=[pltpu.SemaphoreType.DMA((n,)), pltpu.SemaphoreType.DMA((n,))],
    )(*ghs)
    c0 = lax.axis_index("c") == 0
    return [jnp.stack([jnp.where(c0, g, o), jnp.where(c0, o, g)], axis=1) for g, o in zip(ghs, others)]


PACK_C = 1024


def sum_partials_call(gp, rx, chip, core):
    _, _, rh, ncol = gp.shape
    n = rx.shape[0]
    tr = _pick_tile(rh, 512, 16)

    def body(where_ref, a_ref, *rest):
        o_ref = rest[-1]
        g = a_ref[0, 0].astype(F32)
        for b_ref in rest[:-1]:
            g = g + b_ref[0].astype(F32)
        o_ref[...] = g

    other = lambda j: pl.BlockSpec((1, tr, ncol), lambda i, w: (j, i, 0))
    return pl.pallas_call(
        body, name="sum_partials",
        grid_spec=pltpu.PrefetchScalarGridSpec(
            num_scalar_prefetch=1, grid=(rh // tr,),
            in_specs=[pl.BlockSpec((1, 1, tr, ncol), lambda i, w: (w[0], w[1], i, 0))] + [other(j) for j in range(n)],
            out_specs=pl.BlockSpec((tr, ncol), lambda i, w: (i, 0))),
        out_shape=jax.ShapeDtypeStruct((rh, ncol), F32),
        compiler_params=_cp(("parallel",)),
    )(jnp.stack([chip, core]).astype(jnp.int32), gp, *([rx] * n))


def _silu(x):
    return x * _sigmoid(x)


def mod_matmul_call(craw, w_mod):
    L, D, N = w_mod.shape

    def body(c_ref, w_ref, o_ref):
        o_ref[0] = _dot(_silu(c_ref[...]).astype(BF16), w_ref[0].astype(BF16))

    return pl.pallas_call(
        body, name="mod_matmul", grid=(L,),
        in_specs=[pl.BlockSpec((16, D), lambda l: (0, 0)), pl.BlockSpec((1, D, N), lambda l: (l, 0, 0))],
        out_specs=pl.BlockSpec((1, 16, N), lambda l: (l, 0, 0)),
        out_shape=jax.ShapeDtypeStruct((L, 16, N), F32),
        compiler_params=_cp(("parallel",), VMEM_BIG),
    )(craw, w_mod)


def mod_bwd_call(craw, dmod16, w_mod):
    L, D, N = w_mod.shape

    def body(c_ref, d_ref, w_ref, gw_ref, ds_ref):
        l = pl.program_id(0)
        d = d_ref[0].astype(BF16)
        gw_ref[0] = _dot_tn(_silu(c_ref[...]).astype(BF16), d)
        part = _dot_nt(d, w_ref[0].astype(BF16))

        @pl.when(l == 0)
        def _():
            ds_ref[...] = part

        @pl.when(l > 0)
        def _():
            ds_ref[...] += part

    return pl.pallas_call(
        body, name="mod_bwd", grid=(L,),
        in_specs=[pl.BlockSpec((16, D), lambda l: (0, 0)), pl.BlockSpec((1, 16, N), lambda l: (l, 0, 0)),
                  pl.BlockSpec((1, D, N), lambda l: (l, 0, 0))],
        out_specs=[pl.BlockSpec((1, D, N), lambda l: (l, 0, 0)), pl.BlockSpec((16, D), lambda l: (0, 0))],
        out_shape=[jax.ShapeDtypeStruct((L, D, N), F32), jax.ShapeDtypeStruct((16, D), F32)],
        compiler_params=_cp(("arbitrary",), VMEM_BIG),
    )(craw, dmod16, w_mod)


def _adamw(w, g, m, v):
    m2 = ADAM_B1 * m + (1.0 - ADAM_B1) * g
    v2 = ADAM_B2 * v + (1.0 - ADAM_B2) * (g * g)
    m_hat = m2 / (1.0 - ADAM_B1 ** ADAM_STEP)
    v_hat = v2 / (1.0 - ADAM_B2 ** ADAM_STEP)
    delta = -ADAM_LR * (m_hat / (jnp.sqrt(v_hat) + ADAM_EPS) + ADAM_WD * w)
    return delta, m2, v2


def adamw_call(w, g, m, v):
    shape = w.shape
    ncol = shape[-1]
    r = math.prod(shape[:-1])
    tr = _pick_tile(r, 512, 8)
    as2d = lambda t: t.reshape(r, ncol)

    def body(w_ref, g_ref, m_ref, v_ref, d_ref, m2_ref, v2_ref):
        d_ref[...], m2_ref[...], v2_ref[...] = _adamw(w_ref[...], g_ref[...], m_ref[...], v_ref[...])

    spec = pl.BlockSpec((tr, ncol), lambda i: (i, 0))
    outs = pl.pallas_call(
        body, name="adamw", grid=(r // tr,), in_specs=[spec] * 4, out_specs=[spec] * 3,
        out_shape=[jax.ShapeDtypeStruct((r, ncol), F32)] * 3,
        compiler_params=_cp(("parallel",)),
    )(as2d(w), as2d(g), as2d(m), as2d(v))
    return [o.reshape(shape) for o in outs]


def adamw_small_call(parts, w, m, v, silu_bwd=False):
    P, R, C = parts.shape

    def body(p_ref, w_ref, m_ref, v_ref, g_ref, d_ref, m2_ref, v2_ref):
        g = p_ref[0]
        for i in range(1, P):
            g = g + p_ref[i]
        wv = w_ref[...]
        if silu_bwd:
            sg = _sigmoid(wv)
            g = g * (sg * (1.0 + wv * (1.0 - sg)))
        g_ref[...] = g
        d_ref[...], m2_ref[...], v2_ref[...] = _adamw(wv, g, m_ref[...], v_ref[...])

    return pl.pallas_call(
        body, name="adamw_small", out_shape=[jax.ShapeDtypeStruct((R, C), F32)] * 4,
    )(parts, w, m, v)


def sum_parts_call(parts):
    P, R, C = parts.shape

    def body(p_ref, o_ref):
        g = p_ref[0]
        for i in range(1, P):
            g = g + p_ref[i]
        o_ref[...] = g

    return pl.pallas_call(body, name="sum_parts", out_shape=jax.ShapeDtypeStruct((R, C), F32))(parts)


BIG = (("w_in", "col"), ("w_pa", "col"), ("w_pb", "col"), ("w_o", "row"), ("w_ffn_in", "col"), ("w_ffn_out", "row"))
KIND = dict(BIG)
GROUPS = (("w_in",), ("w_ffn_in",), ("w_pa", "w_pb", "w_o", "w_ffn_out"))
G_IN = 0
SMALL = ("norm1", "norm2", "final_norm", "na_rpb", "q_gain", "k_gain")


def _group_width(group, shapes):
    return shapes[group[0]][1] if len(group) == 1 else PACK_C


def _pack_shards(shards, group, shapes):
    width = _group_width(group, shapes)
    lead = shards[group[0]].shape[:-2]
    p = jnp.concatenate([shards[n].reshape(lead + (-1, width)) for n in group], axis=-2)
    return p.reshape(lead + (2, p.shape[-2] // 2, width))


def _unpack_shards(packed, group, shapes):
    width = _group_width(group, shapes)
    lead = packed.shape[:-3]
    flat = packed.reshape(lead + (-1, width))
    out, r0 = {}, 0
    for n in group:
        K, N = shapes[n]
        rw = K * N // width
        out[n] = flat[..., r0:r0 + rw, :].reshape(lead + (K, N))
        r0 += rw
    return out


def _whole_from_chips(g, kind):
    _, L, K, N = g.shape
    if kind == "col":
        return g.transpose(1, 2, 0, 3).reshape(L, K, N_CHIP * N)
    return g.transpose(1, 0, 2, 3).reshape(L, N_CHIP * K, N)


def _chips_from_whole(g, kind):
    K, N = g.shape
    if kind == "col":
        return g.reshape(K, N_CHIP, N // N_CHIP).transpose(1, 0, 2)
    return g.reshape(N_CHIP, K // N_CHIP, N)


class Exchange:
    def __init__(self, local_bf16, shard_shapes):
        self.shapes = shard_shapes
        self.packed = [_pack_shards(local_bf16, g, shard_shapes).transpose(1, 0, 2, 3) for g in GROUPS]
        self.received = {}
        self.partials = {}

    def whole(self, gathered, group):
        g = _unpack_shards(gathered.transpose(0, 2, 1, 3, 4), group, self.shapes)
        return {n: _whole_from_chips(g[n], KIND[n]) for n in group}

    def first_needed(self):
        return self.whole(weights_all_gather([self.packed[G_IN][:, :1]])[0], GROUPS[G_IN])

    def rest(self):
        return [p[:, 1:] if gi == G_IN else p for gi, p in enumerate(self.packed)]

    def rest_whole(self, gathered):
        out = [self.whole(place_own_block(g, p), grp) for g, p, grp in zip(gathered, self.rest(), GROUPS)]
        others = {}
        for gi, d in enumerate(out):
            if gi != G_IN:
                others.update(d)
        return out[G_IN], others

    def pack_grads(self, layer, gw, groups):
        for gi in groups:
            shards = {n: _chips_from_whole(gw[n], KIND[n]) for n in GROUPS[gi]}
            self.partials[(layer, gi)] = _pack_shards(shards, GROUPS[gi], self.shapes)


def _pad_rows(v, rows):
    return jnp.pad(v.reshape(-1), (0, rows * PACK_C - v.size)).reshape(rows, PACK_C)


def kernel(x, c, ctx, c_ctx, w_mod, b_mod, norm1, w_in, na_rpb, q_gain, k_gain, w_pa, w_pb, w_o, norm2, w_ffn_in, w_ffn_out, final_norm, loss_target, m_c_ctx, m_w_mod, m_b_mod, m_norm1, m_w_in, m_na_rpb, m_q_gain, m_k_gain, m_w_pa, m_w_pb, m_w_o, m_norm2, m_w_ffn_in, m_w_ffn_out, m_final_norm, v_c_ctx, v_w_mod, v_b_mod, v_norm1, v_w_in, v_na_rpb, v_q_gain, v_k_gain, v_w_pa, v_w_pb, v_w_o, v_norm2, v_w_ffn_in, v_w_ffn_out, v_final_norm):
    W = dict(c_ctx=c_ctx, w_mod=w_mod, b_mod=b_mod, norm1=norm1, w_in=w_in, na_rpb=na_rpb, q_gain=q_gain, k_gain=k_gain,
             w_pa=w_pa, w_pb=w_pb, w_o=w_o, norm2=norm2, w_ffn_in=w_ffn_in, w_ffn_out=w_ffn_out, final_norm=final_norm)
    M = dict(c_ctx=m_c_ctx, w_mod=m_w_mod, b_mod=m_b_mod, norm1=m_norm1, w_in=m_w_in, na_rpb=m_na_rpb, q_gain=m_q_gain,
             k_gain=m_k_gain, w_pa=m_w_pa, w_pb=m_w_pb, w_o=m_w_o, norm2=m_norm2, w_ffn_in=m_w_ffn_in,
             w_ffn_out=m_w_ffn_out, final_norm=m_final_norm)
    V = dict(c_ctx=v_c_ctx, w_mod=v_w_mod, b_mod=v_b_mod, norm1=v_norm1, w_in=v_w_in, na_rpb=v_na_rpb, q_gain=v_q_gain,
             k_gain=v_k_gain, w_pa=v_w_pa, w_pb=v_w_pb, w_o=v_w_o, norm2=v_norm2, w_ffn_in=v_w_ffn_in,
             w_ffn_out=v_w_ffn_out, final_norm=v_final_norm)
    order = ["c_ctx", "w_mod", "b_mod", "norm1", "w_in", "na_rpb", "q_gain", "k_gain", "w_pa", "w_pb", "w_o", "norm2",
             "w_ffn_in", "w_ffn_out", "final_norm"]
    L, D = norm1.shape
    n_lat = x.shape[1]
    ax, ay, ac = lax.axis_index("x"), lax.axis_index("y"), lax.axis_index("c")
    chip = 2 * ax + ay
    dev = 2 * chip + ac

    shard_shapes = {n: W[n].shape[1:] for n, _ in BIG}
    ex = Exchange({n: W[n].astype(BF16) for n, _ in BIG}, shard_shapes)
    w_first = ex.first_needed()

    c_all = small_all_gather(jnp.pad(c, ((0, 7), (0, 0))), "gather_c")[:, 0, :]
    craw = jnp.pad(c_all, ((0, 8), (0, 0))) + jnp.pad(c_ctx[None, :], ((8, 7), (0, 0)))
    ncol_mod = w_mod.shape[2]
    mod_loc = mod_matmul_call(craw, w_mod)
    mod_all = small_all_gather(mod_loc.reshape(L * 16, ncol_mod), "gather_mod")
    mod_all = mod_all[0::2].reshape(N_CHIP, L, 16, ncol_mod).transpose(1, 2, 0, 3).reshape(L, 16, 6 * D)
    mod_all = mod_all + b_mod[:, None, :]
    mod_mine = lax.dynamic_index_in_dim(mod_all, dev, axis=1, keepdims=False)
    modv = jnp.stack([mod_mine, mod_all[:, 8]], axis=1).reshape(L, 2, 6, D)

    xs = jnp.concatenate([x[0], ctx[0]], axis=0)
    small = dict(norm1=norm1, norm2=norm2, na_rpb=na_rpb, q_gain=q_gain, k_gain=k_gain, final_norm=final_norm)
    loss_part, dx, _, dmod, gs = device_fwd_bwd(xs, loss_target[0], modv, w_first, small, n_lat, ex)
    loss = lax.psum(jnp.sum(loss_part), ("x", "y", "c"))
    grad_x = dx[:n_lat][None]

    gh = [jnp.stack([sum_partials_call(ex.partials[(l, gi)], ex.received[(l, gi)], chip, ac) for l in range(L)])
          for gi in range(len(GROUPS))]
    gbig = {}
    for grp, joined in zip(GROUPS, grads_join_halves(gh)):
        gbig.update(_unpack_shards(joined, grp, shard_shapes))

    n_mod_rows = L * 6 * D // PACK_C
    small_flat = jnp.concatenate([gs[n].reshape(-1) for n in SMALL])
    n_small_rows = -(-small_flat.size // (8 * PACK_C)) * 8
    g1 = jnp.concatenate([dmod[:, 0].reshape(n_mod_rows, PACK_C), dmod[:, 1].reshape(n_mod_rows, PACK_C),
                          _pad_rows(small_flat, n_small_rows)], axis=0)
    g1 = small_all_gather(g1, "gather_small_grads")
    dm_lat = g1[:, :n_mod_rows].reshape(N_DEV, L, 6 * D)
    dm_ctx_parts = g1[:, n_mod_rows:2 * n_mod_rows]
    dm_ctx = sum_parts_call(dm_ctx_parts).reshape(L, 1, 6 * D)
    dmod16 = (jnp.pad(dm_lat.transpose(1, 0, 2), ((0, 0), (0, 8), (0, 0)))
              + jnp.pad(dm_ctx, ((0, 0), (8, 7), (0, 0))))
    dmod16_loc = lax.dynamic_slice_in_dim(dmod16, chip * ncol_mod, ncol_mod, axis=2)
    g_wmod, dsilu_part = mod_bwd_call(craw, dmod16_loc, w_mod)
    ds_all = small_all_gather(dsilu_part[8:16], "gather_dsilu")
    ds_parts = ds_all[0::2, 0:1, :]
    ds_parts = jnp.pad(ds_parts, ((0, 0), (0, 7), (0, 0)))

    out_g, out_d, out_m, out_v = {}, {}, {}, {}
    for n, _ in BIG:
        out_g[n] = gbig[n]
        out_d[n], out_m[n], out_v[n] = adamw_call(W[n], gbig[n], M[n], V[n])
    out_g["w_mod"] = g_wmod
    out_d["w_mod"], out_m["w_mod"], out_v["w_mod"] = adamw_call(w_mod, g_wmod, m_w_mod, v_w_mod)

    pack_small = lambda t: _pad_rows(jnp.concatenate([t[n].reshape(-1) for n in SMALL]), n_small_rows)
    res = adamw_small_call(g1[:, 2 * n_mod_rows:], pack_small(W), pack_small(M), pack_small(V))
    off = 0
    for n in SMALL:
        sz = W[n].size
        for dst, r in zip((out_g, out_d, out_m, out_v), res):
            dst[n] = r.reshape(-1)[off:off + sz].reshape(W[n].shape)
        off += sz
    bparts = jnp.concatenate([g1[:, :n_mod_rows], dm_ctx_parts], axis=0)
    as_rows = lambda t: t.reshape(n_mod_rows, PACK_C)
    res = adamw_small_call(bparts, as_rows(b_mod), as_rows(m_b_mod), as_rows(v_b_mod))
    for dst, r in zip((out_g, out_d, out_m, out_v), res):
        dst["b_mod"] = r.reshape(b_mod.shape)
    row8 = lambda t: jnp.pad(t[None, :], ((0, 7), (0, 0)))
    res = adamw_small_call(ds_parts, row8(c_ctx), row8(m_c_ctx), row8(v_c_ctx), silu_bwd=True)
    for dst, r in zip((out_g, out_d, out_m, out_v), res):
        dst["c_ctx"] = r[0]

    return (loss, grad_x, *[out_g[n] for n in order], *[out_d[n] for n in order],
            *[out_m[n] for n in order], *[out_v[n] for n in order])
```

```python
import math

import numpy as np
import jax
import jax.numpy as jnp
from jax import lax
from jax.experimental import pallas as pl
from jax.experimental.pallas import tpu as pltpu

F32 = jnp.float32
BF16 = jnp.bfloat16
MESH = pl.DeviceIdType.MESH

HEAD_DIM = 64
NA_HEADS = 8
GQA_Q_HEADS = 8
GQA_KV_HEADS = 2
GRID_W = 64
NA_WIN_H = 8
NA_WIN_W = 16
ROPE_THETA = 10000.0
EPS = 1e-6
SCALE = HEAD_DIM ** -0.5
NEG = -1e30

ADAM_LR = 0.001
ADAM_B1 = 0.9
ADAM_B2 = 0.999
ADAM_EPS = 1e-08
ADAM_WD = 0.01
ADAM_STEP = 10

TM = 256
NA_ROWS = 4
NA_KROWS = 12
LANES = 128
VMEM_BIG = 56 * 1024 * 1024


def _cp(sem, vmem=None):
    return pltpu.CompilerParams(dimension_semantics=sem, vmem_limit_bytes=vmem)


def _dot(a, b):
    return jnp.dot(a, b, preferred_element_type=F32)


def _dot_nt(a, b):
    return lax.dot_general(a, b, (((1,), (1,)), ((), ())), preferred_element_type=F32)


def _dot_tn(a, b):
    return lax.dot_general(a, b, (((0,), (0,)), ((), ())), preferred_element_type=F32)


def _colsum8(v):
    tm, d = v.shape
    return v.reshape(tm // 8, 8, d).sum(axis=0)


def _acc(ref, val, first):
    @pl.when(first)
    def _():
        ref[0] = val

    @pl.when(jnp.logical_not(first))
    def _():
        ref[0] += val


def _rms_r(x):
    return lax.rsqrt(jnp.mean(x * x, axis=-1, keepdims=True) + EPS)


def _norm_mod(x, nw, sh, sc):
    return (x * _rms_r(x) * nw) * (1.0 + sc) + sh


def _norm_mod_bwd(dh, x, nw, sc):
    r = _rms_r(x)
    xn = x * r
    dxn = dh * (nw * (1.0 + sc))
    dx = r * (dxn - xn * jnp.mean(dxn * xn, axis=-1, keepdims=True))
    return dx, dh, dh * (xn * nw), dh * ((1.0 + sc) * xn)


def _sigmoid(x):
    return 1.0 / (1.0 + jnp.exp(-x))


def _group_of(i, n_lat_tiles):
    return jnp.where(i >= n_lat_tiles, 1, 0)


def _lane_half(shape):
    return (lax.broadcasted_iota(jnp.int32, shape, len(shape) - 1) % LANES) // HEAD_DIM


def _lane_lt64(shape):
    return _lane_half(shape) == 0


def _roll64(x):
    return pltpu.roll(x, HEAD_DIM, x.ndim - 1)


def norm_mod_call(x, nw, modv, n_lat_tiles, which):
    T, D = x.shape

    def body(x_ref, nw_ref, mod_ref, h_ref):
        sh = mod_ref[0, 3 * which:3 * which + 1, :]
        sc = mod_ref[0, 3 * which + 1:3 * which + 2, :]
        h_ref[...] = _norm_mod(x_ref[...], nw_ref[...], sh, sc).astype(BF16)

    return pl.pallas_call(
        body, name="norm_mod", grid=(T // TM,),
        in_specs=[pl.BlockSpec((TM, D), lambda i: (i, 0)),
                  pl.BlockSpec((1, D), lambda i: (0, 0)),
                  pl.BlockSpec((1, 6, D), lambda i: (_group_of(i, n_lat_tiles), 0, 0))],
        out_specs=pl.BlockSpec((TM, D), lambda i: (i, 0)),
        out_shape=jax.ShapeDtypeStruct((T, D), BF16),
        compiler_params=_cp(("parallel",)),
    )(x, nw, modv)


def qkv_call(h, w_in):
    T, D = h.shape
    N = w_in.shape[1]
    n_na = 3 * NA_HEADS * HEAD_DIM
    n_g = (GQA_Q_HEADS + 2 * GQA_KV_HEADS) * HEAD_DIM
    n_gate = N - n_na - n_g

    def body(h_ref, w_ref, na_ref, g_ref, gate_ref):
        acc = _dot(h_ref[...], w_ref[...])
        na_ref[...] = acc[:, :n_na].astype(BF16)
        g_ref[...] = acc[:, n_na:n_na + n_g].astype(BF16)
        gate_ref[...] = acc[:, n_na + n_g:].astype(BF16)

    return pl.pallas_call(
        body, name="qkv", grid=(T // TM,),
        in_specs=[pl.BlockSpec((TM, D), lambda i: (i, 0)),
                  pl.BlockSpec((D, N), lambda i: (0, 0))],
        out_specs=[pl.BlockSpec((TM, n_na), lambda i: (i, 0)),
                   pl.BlockSpec((TM, n_g), lambda i: (i, 0)),
                   pl.BlockSpec((TM, n_gate), lambda i: (i, 0))],
        out_shape=[jax.ShapeDtypeStruct((T, n_na), BF16),
                   jax.ShapeDtypeStruct((T, n_g), BF16),
                   jax.ShapeDtypeStruct((T, n_gate), BF16)],
        compiler_params=_cp(("parallel",), VMEM_BIG),
    )(h, w_in)


def merge_call(ya, yb, gates, w_pa, w_pb):
    T, Ka = ya.shape
    D = w_pa.shape[1]

    def body(ya_ref, yb_ref, ga_ref, gb_ref, wa_ref, wb_ref, m_ref, pa_ref, pb_ref):
        pa = _dot(ya_ref[...], wa_ref[...])
        pb = _dot(yb_ref[...], wb_ref[...])
        m = _sigmoid(ga_ref[...].astype(F32)) * pa + _sigmoid(gb_ref[...].astype(F32)) * pb
        m_ref[...] = m.astype(BF16)
        pa_ref[...] = pa.astype(BF16)
        pb_ref[...] = pb.astype(BF16)

    row = lambda w: pl.BlockSpec((TM, w), lambda i: (i, 0))
    return pl.pallas_call(
        body, name="merge", grid=(T // TM,),
        in_specs=[row(Ka), row(Ka),
                  pl.BlockSpec((TM, D), lambda i: (i, 0)), pl.BlockSpec((TM, D), lambda i: (i, 1)),
                  pl.BlockSpec((Ka, D), lambda i: (0, 0)), pl.BlockSpec((Ka, D), lambda i: (0, 0))],
        out_specs=[row(D), row(D), row(D)],
        out_shape=[jax.ShapeDtypeStruct((T, D), BF16)] * 3,
        compiler_params=_cp(("parallel",)),
    )(ya, yb, gates, gates, w_pa, w_pb)


def wo_call(merged, w_o, x, modv, nw2, n_lat_tiles):
    T, D = x.shape

    def body(m_ref, w_ref, x_ref, mod_ref, nw_ref, x1_ref, o_ref, h2_ref):
        o = _dot(m_ref[...], w_ref[...])
        x1 = x_ref[...] + mod_ref[0, 2:3, :] * o
        x1_ref[...] = x1
        o_ref[...] = o.astype(BF16)
        h2_ref[...] = _norm_mod(x1, nw_ref[...], mod_ref[0, 3:4, :], mod_ref[0, 4:5, :]).astype(BF16)

    row = pl.BlockSpec((TM, D), lambda i: (i, 0))
    return pl.pallas_call(
        body, name="wo", grid=(T // TM,),
        in_specs=[row, pl.BlockSpec((D, D), lambda i: (0, 0)), row,
                  pl.BlockSpec((1, 6, D), lambda i: (_group_of(i, n_lat_tiles), 0, 0)),
                  pl.BlockSpec((1, D), lambda i: (0, 0))],
        out_specs=[row, row, row],
        out_shape=[jax.ShapeDtypeStruct((T, D), F32), jax.ShapeDtypeStruct((T, D), BF16),
                   jax.ShapeDtypeStruct((T, D), BF16)],
        compiler_params=_cp(("parallel",)),
    )(merged, w_o, x, modv, nw2)


def ffn_in_call(h2, w_ffn_in, n_col_tiles=2):
    T, D = h2.shape
    F = w_ffn_in.shape[1] // 2
    tn = F // n_col_tiles

    def body(h_ref, wa_ref, wu_ref, act_ref, a_ref, u_ref):
        h = h_ref[...]
        a = _dot(h, wa_ref[...])
        u = _dot(h, wu_ref[...])
        act_ref[...] = (a * _sigmoid(a) * u).astype(BF16)
        a_ref[...] = a.astype(BF16)
        u_ref[...] = u.astype(BF16)

    blk = pl.BlockSpec((TM, tn), lambda j, i: (i, j))
    return pl.pallas_call(
        body, name="ffn_in", grid=(n_col_tiles, T // TM),
        in_specs=[pl.BlockSpec((TM, D), lambda j, i: (i, 0)),
                  pl.BlockSpec((D, tn), lambda j, i: (0, j)),
                  pl.BlockSpec((D, tn), lambda j, i: (0, n_col_tiles + j))],
        out_specs=[blk, blk, blk],
        out_shape=[jax.ShapeDtypeStruct((T, F), BF16)] * 3,
        compiler_params=_cp(("parallel", "parallel"), VMEM_BIG),
    )(h2, w_ffn_in, w_ffn_in)


def ffn_out_call(act, w_ffn_out, x1, modv, n_lat_tiles, next_nw=None, next_modv=None):
    T, D = x1.shape
    F = act.shape[1]
    with_next = next_nw is not None

    def body(*refs):
        if with_next:
            a_ref, w_ref, x_ref, mod_ref, nw_ref, nmod_ref, x2_ref, f_ref, hn_ref = refs
        else:
            a_ref, w_ref, x_ref, mod_ref, x2_ref, f_ref = refs
        f = _dot(a_ref[...], w_ref[...])
        x2 = x_ref[...] + mod_ref[0, 5:6, :] * f
        x2_ref[...] = x2
        f_ref[...] = f.astype(BF16)
        if with_next:
            hn_ref[...] = _norm_mod(x2, nw_ref[...], nmod_ref[0, 0:1, :], nmod_ref[0, 1:2, :]).astype(BF16)

    row = pl.BlockSpec((TM, D), lambda i: (i, 0))
    modspec = pl.BlockSpec((1, 6, D), lambda i: (_group_of(i, n_lat_tiles), 0, 0))
    in_specs = [pl.BlockSpec((TM, F), lambda i: (i, 0)), pl.BlockSpec((F, D), lambda i: (0, 0)), row, modspec]
    args = [act, w_ffn_out, x1, modv]
    out_specs = [row, row]
    out_shape = [jax.ShapeDtypeStruct((T, D), F32), jax.ShapeDtypeStruct((T, D), BF16)]
    if with_next:
        in_specs += [pl.BlockSpec((1, D), lambda i: (0, 0)), modspec]
        args += [next_nw, next_modv]
        out_specs.append(row)
        out_shape.append(jax.ShapeDtypeStruct((T, D), BF16))
    return pl.pallas_call(
        body, name="ffn_out", grid=(T // TM,), in_specs=in_specs, out_specs=out_specs, out_shape=out_shape,
        compiler_params=_cp(("parallel",), VMEM_BIG),
    )(*args)


def loss_call(x, nw, target, n_lat_tiles):
    T, D = x.shape

    def body(x_ref, nw_ref, t_ref, dx_ref, dw_ref, ls_ref):
        i = pl.program_id(0)

        @pl.when(i < n_lat_tiles)
        def _():
            xv = x_ref[...]
            nw_v = nw_ref[...]
            r = _rms_r(xv)
            xn = xv * r
            e = xn * nw_v - t_ref[...]
            dy = e * (1.0 / D)
            dxn = dy * nw_v
            dx_ref[...] = r * (dxn - xn * jnp.mean(dxn * xn, axis=-1, keepdims=True))
            _acc(dw_ref, _colsum8(dy * xn), i == 0)
            _acc(ls_ref, _colsum8(e * e * (0.5 / D)), i == 0)

        @pl.when(i >= n_lat_tiles)
        def _():
            dx_ref[...] = jnp.zeros((TM, D), F32)

    stat = pl.BlockSpec((1, 8, D), lambda i: (0, 0, 0))
    return pl.pallas_call(
        body, name="loss", grid=(T // TM,),
        in_specs=[pl.BlockSpec((TM, D), lambda i: (i, 0)), pl.BlockSpec((1, D), lambda i: (0, 0)),
                  pl.BlockSpec((TM, D), lambda i: (jnp.minimum(i, n_lat_tiles - 1), 0))],
        out_specs=[pl.BlockSpec((TM, D), lambda i: (i, 0)), stat, stat],
        out_shape=[jax.ShapeDtypeStruct((T, D), F32), jax.ShapeDtypeStruct((1, 8, D), F32),
                   jax.ShapeDtypeStruct((1, 8, D), F32)],
        compiler_params=_cp(("arbitrary",)),
    )(x, nw, target)


def dact_call(dx2, modv, fsave, w_ffn_out, a, u, n_lat_tiles):
    T, D = dx2.shape
    F = a.shape[1]

    def body(dx_ref, mod_ref, f_ref, w_ref, a_ref, u_ref, da_ref, du_ref, df_ref, dg_ref):
        i = pl.program_id(0)
        dx = dx_ref[...]
        df = (dx * mod_ref[0, 5:6, :]).astype(BF16)
        df_ref[...] = df
        dact = _dot_nt(df, w_ref[...])
        av = a_ref[...].astype(F32)
        uv = u_ref[...].astype(F32)
        sg = _sigmoid(av)
        da_ref[...] = (dact * uv * (sg * (1.0 + av * (1.0 - sg)))).astype(BF16)
        du_ref[...] = (dact * (av * sg)).astype(BF16)
        _acc(dg_ref, _colsum8(dx * f_ref[...].astype(F32)), (i == 0) | (i == n_lat_tiles))

    row = pl.BlockSpec((TM, D), lambda i: (i, 0))
    wide = pl.BlockSpec((TM, F), lambda i: (i, 0))
    grp = lambda i: (_group_of(i, n_lat_tiles), 0, 0)
    return pl.pallas_call(
        body, name="dact", grid=(T // TM,),
        in_specs=[row, pl.BlockSpec((1, 6, D), grp), row, pl.BlockSpec((F, D), lambda i: (0, 0)), wide, wide],
        out_specs=[wide, wide, row, pl.BlockSpec((1, 8, D), grp)],
        out_shape=[jax.ShapeDtypeStruct((T, F), BF16), jax.ShapeDtypeStruct((T, F), BF16),
                   jax.ShapeDtypeStruct((T, D), BF16), jax.ShapeDtypeStruct((2, 8, D), F32)],
        compiler_params=_cp(("arbitrary",), VMEM_BIG),
    )(dx2, modv, fsave, w_ffn_out, a, u)


def _norm_bwd_tail(dh, x_ref, mod_ref, nw_ref, dres_ref, dx_ref, s_sh, s_sc, s_w, which, first):
    sc = mod_ref[0, 3 * which + 1:3 * which + 2, :]
    dxn, t_sh, t_sc, t_w = _norm_mod_bwd(dh, x_ref[...], nw_ref[...], sc)
    dx_ref[...] = dres_ref[...] + dxn
    _acc(s_sh, _colsum8(t_sh), first)
    _acc(s_sc, _colsum8(t_sc), first)
    _acc(s_w, _colsum8(t_w), first)


def dh2_call(da, du, w_ffn_in, x1, modv, nw2, dx2, n_lat_tiles):
    T, D = x1.shape
    F = da.shape[1]

    def body(da_ref, du_ref, wa_ref, wu_ref, x_ref, mod_ref, nw_ref, dres_ref, dx_ref, s_sh, s_sc, s_w):
        i = pl.program_id(0)
        dh = _dot_nt(da_ref[...], wa_ref[...]) + _dot_nt(du_ref[...], wu_ref[...])
        _norm_bwd_tail(dh, x_ref, mod_ref, nw_ref, dres_ref, dx_ref, s_sh, s_sc, s_w, 1,
                       (i == 0) | (i == n_lat_tiles))

    row = pl.BlockSpec((TM, D), lambda i: (i, 0))
    wide = pl.BlockSpec((TM, F), lambda i: (i, 0))
    grp = lambda i: (_group_of(i, n_lat_tiles), 0, 0)
    stat = pl.BlockSpec((1, 8, D), grp)
    return pl.pallas_call(
        body, name="dh2", grid=(T // TM,),
        in_specs=[wide, wide, pl.BlockSpec((D, F), lambda i: (0, 0)), pl.BlockSpec((D, F), lambda i: (0, 1)),
                  row, pl.BlockSpec((1, 6, D), grp), pl.BlockSpec((1, D), lambda i: (0, 0)), row],
        out_specs=[row, stat, stat, stat],
        out_shape=[jax.ShapeDtypeStruct((T, D), F32)] + [jax.ShapeDtypeStruct((2, 8, D), F32)] * 3,
        compiler_params=_cp(("arbitrary",), VMEM_BIG),
    )(da, du, w_ffn_in, w_ffn_in, x1, modv, nw2, dx2)


def dmerged_call(dx1, modv, osave, w_o, pa, pb, gates, n_lat_tiles):
    T, D = dx1.shape

    def body(dx_ref, mod_ref, o_ref, w_ref, pa_ref, pb_ref, ga_ref, gb_ref,
             dpa_ref, dpb_ref, dgate_ref, do_ref, dg_ref):
        i = pl.program_id(0)
        dx = dx_ref[...]
        do = (dx * mod_ref[0, 2:3, :]).astype(BF16)
        do_ref[...] = do
        dm = _dot_nt(do, w_ref[...])
        sa = _sigmoid(ga_ref[...].astype(F32))
        sb = _sigmoid(gb_ref[...].astype(F32))
        dpa_ref[...] = (dm * sa).astype(BF16)
        dpb_ref[...] = (dm * sb).astype(BF16)
        dgate_ref[:, :D] = (dm * pa_ref[...].astype(F32) * (sa * (1.0 - sa))).astype(BF16)
        dgate_ref[:, D:] = (dm * pb_ref[...].astype(F32) * (sb * (1.0 - sb))).astype(BF16)
        _acc(dg_ref, _colsum8(dx * o_ref[...].astype(F32)), (i == 0) | (i == n_lat_tiles))

    row = pl.BlockSpec((TM, D), lambda i: (i, 0))
    grp = lambda i: (_group_of(i, n_lat_tiles), 0, 0)
    return pl.pallas_call(
        body, name="dmerged", grid=(T // TM,),
        in_specs=[row, pl.BlockSpec((1, 6, D), grp), row, pl.BlockSpec((D, D), lambda i: (0, 0)), row, row,
                  pl.BlockSpec((TM, D), lambda i: (i, 0)), pl.BlockSpec((TM, D), lambda i: (i, 1))],
        out_specs=[row, row, pl.BlockSpec((TM, 2 * D), lambda i: (i, 0)), row, pl.BlockSpec((1, 8, D), grp)],
        out_shape=[jax.ShapeDtypeStruct((T, D), BF16), jax.ShapeDtypeStruct((T, D), BF16),
                   jax.ShapeDtypeStruct((T, 2 * D), BF16), jax.ShapeDtypeStruct((T, D), BF16),
                   jax.ShapeDtypeStruct((2, 8, D), F32)],
        compiler_params=_cp(("arbitrary",)),
    )(dx1, modv, osave, w_o, pa, pb, gates, gates)


def dy_call(dpa, dpb, w_pa, w_pb):
    T, D = dpa.shape
    K = w_pa.shape[0]

    def body(a_ref, b_ref, wa_ref, wb_ref, ya_ref, yb_ref):
        ya_ref[...] = _dot_nt(a_ref[...], wa_ref[...]).astype(BF16)
        yb_ref[...] = _dot_nt(b_ref[...], wb_ref[...]).astype(BF16)

    row = pl.BlockSpec((TM, D), lambda i: (i, 0))
    w = pl.BlockSpec((K, D), lambda i: (0, 0))
    out = pl.BlockSpec((TM, K), lambda i: (i, 0))
    return pl.pallas_call(
        body, name="dy", grid=(T // TM,), in_specs=[row, row, w, w], out_specs=[out, out],
        out_shape=[jax.ShapeDtypeStruct((T, K), BF16)] * 2,
        compiler_params=_cp(("parallel",)),
    )(dpa, dpb, w_pa, w_pb)


def dh_call(dna, dg, dgates, w_in, x, modv, nw1, dx1, n_lat_tiles):
    T, D = x.shape
    n1, n2, n3 = dna.shape[1], dg.shape[1], dgates.shape[1]
    N = n1 + n2 + n3

    def body(a_ref, b_ref, c_ref, w_ref, x_ref, mod_ref, nw_ref, dres_ref, dx_ref, s_sh, s_sc, s_w):
        i = pl.program_id(0)
        dh = (_dot_nt(a_ref[...], w_ref[:, :n1]) + _dot_nt(b_ref[...], w_ref[:, n1:n1 + n2])
              + _dot_nt(c_ref[...], w_ref[:, n1 + n2:]))
        _norm_bwd_tail(dh, x_ref, mod_ref, nw_ref, dres_ref, dx_ref, s_sh, s_sc, s_w, 0,
                       (i == 0) | (i == n_lat_tiles))

    row = pl.BlockSpec((TM, D), lambda i: (i, 0))
    grp = lambda i: (_group_of(i, n_lat_tiles), 0, 0)
    stat = pl.BlockSpec((1, 8, D), grp)
    return pl.pallas_call(
        body, name="dh", grid=(T // TM,),
        in_specs=[pl.BlockSpec((TM, n1), lambda i: (i, 0)), pl.BlockSpec((TM, n2), lambda i: (i, 0)),
                  pl.BlockSpec((TM, n3), lambda i: (i, 0)), pl.BlockSpec((D, N), lambda i: (0, 0)),
                  row, pl.BlockSpec((1, 6, D), grp), pl.BlockSpec((1, D), lambda i: (0, 0)), row],
        out_specs=[row, stat, stat, stat],
        out_shape=[jax.ShapeDtypeStruct((T, D), F32)] + [jax.ShapeDtypeStruct((2, 8, D), F32)] * 3,
        compiler_params=_cp(("arbitrary",), VMEM_BIG),
    )(dna, dg, dgates, w_in, x, modv, nw1, dx1)


def _pick_tile(n, cap, mult):
    best = None
    for d in range(mult, min(n, cap) + 1, mult):
        if n % d == 0:
            best = d
    assert best is not None, (n, cap, mult)
    return best


def wgrad_call(a, b, name):
    T, K = a.shape
    N = b.shape[1]
    tt = _pick_tile(T, 1408, 128)
    tk = K if K <= 1024 else _pick_tile(K, 1408, 128)
    tn = N if N <= 2304 else _pick_tile(N, 2304, 128)
    nt = T // tt

    def body(a_ref, b_ref, o_ref, acc_ref):
        t = pl.program_id(2)
        p = _dot_tn(a_ref[...], b_ref[...])

        @pl.when(t == 0)
        def _():
            acc_ref[...] = p

        @pl.when(t > 0)
        def _():
            acc_ref[...] += p

        @pl.when(t == nt - 1)
        def _():
            o_ref[...] = acc_ref[...].astype(BF16)

    return pl.pallas_call(
        body, name=name, grid=(K // tk, N // tn, nt),
        in_specs=[pl.BlockSpec((tt, tk), lambda i, j, t: (t, i)), pl.BlockSpec((tt, tn), lambda i, j, t: (t, j))],
        out_specs=pl.BlockSpec((tk, tn), lambda i, j, t: (i, j)),
        out_shape=jax.ShapeDtypeStruct((K, N), BF16),
        scratch_shapes=[pltpu.VMEM((tk, tn), F32)],
        compiler_params=_cp(("parallel", "parallel", "arbitrary"), VMEM_BIG),
    )(a, b)


def _head_mean(v):
    n = v.shape[-1]
    r = lax.broadcasted_iota(jnp.int32, (n, n), 0) // HEAD_DIM
    c = lax.broadcasted_iota(jnp.int32, (n, n), 1) // HEAD_DIM
    bd = jnp.where(r == c, 1.0 / HEAD_DIM, 0.0).astype(BF16)
    hi = v.astype(BF16)
    lo = (v - hi.astype(F32)).astype(BF16)
    return _dot(hi, bd) + _dot(lo, bd)


def _swap_pairs(x):
    n = x.shape[-1]
    even = lax.broadcasted_iota(jnp.int32, x.shape, x.ndim - 1) % 2 == 0
    return jnp.where(even, pltpu.roll(x, n - 1, x.ndim - 1), pltpu.roll(x, 1, x.ndim - 1))


def _tile_lanes(t, n):
    return jnp.tile(t, (1, n // t.shape[-1]))


def gqa_prep_call(gqkv, qgain, kgain, cos_t, sin_t):
    T = gqkv.shape[0]
    nq = GQA_Q_HEADS * HEAD_DIM
    nk = GQA_KV_HEADS * HEAD_DIM
    rep = GQA_Q_HEADS // GQA_KV_HEADS

    def body(g_ref, qg_ref, kg_ref, c_ref, s_ref, qp_ref, kp_ref):
        cos_v, sin_v = c_ref[...], s_ref[...]
        xq = g_ref[:, :nq].astype(F32)
        yq = xq * lax.rsqrt(_head_mean(xq * xq) + EPS) * qg_ref[...]
        yq = (yq * _tile_lanes(cos_v, nq) + _swap_pairs(yq) * _tile_lanes(sin_v, nq)) * SCALE
        lo = _lane_lt64((TM, LANES))
        for p in range(GQA_Q_HEADS // 2):
            g = (2 * p) // rep
            chunk = yq[:, LANES * p:LANES * (p + 1)]
            rolled = _roll64(chunk)
            if g == 0:
                first, second = jnp.where(lo, chunk, 0.0), jnp.where(lo, rolled, 0.0)
            else:
                first, second = jnp.where(lo, 0.0, rolled), jnp.where(lo, 0.0, chunk)
            qp_ref[:, 2 * LANES * p:2 * LANES * p + LANES] = first.astype(BF16)
            qp_ref[:, 2 * LANES * p + LANES:2 * LANES * (p + 1)] = second.astype(BF16)
        xk = g_ref[:, nq:nq + nk].astype(F32)
        yk = xk * lax.rsqrt(_head_mean(xk * xk) + EPS) * kg_ref[...]
        kp_ref[...] = (yk * cos_v + _swap_pairs(yk) * sin_v).astype(BF16)

    return pl.pallas_call(
        body, name="gqa_prep", grid=(T // TM,),
        in_specs=[pl.BlockSpec((TM, gqkv.shape[1]), lambda i: (i, 0)),
                  pl.BlockSpec((1, nq), lambda i: (0, 0)), pl.BlockSpec((1, nk), lambda i: (0, 0)),
                  pl.BlockSpec((TM, nk), lambda i: (i, 0)), pl.BlockSpec((TM, nk), lambda i: (i, 0))],
        out_specs=[pl.BlockSpec((TM, GQA_Q_HEADS * LANES), lambda i: (i, 0)), pl.BlockSpec((TM, nk), lambda i: (i, 0))],
        out_shape=[jax.ShapeDtypeStruct((T, GQA_Q_HEADS * LANES), BF16), jax.ShapeDtypeStruct((T, nk), BF16)],
        compiler_params=_cp(("parallel",)),
    )(gqkv, qgain, kgain, cos_t, sin_t)


def gqa_prep_bwd_call(gqkv, qgain, kgain, cos_t, sin_t, dqp, dkp, dv):
    T = gqkv.shape[0]
    nq = GQA_Q_HEADS * HEAD_DIM
    nk = GQA_KV_HEADS * HEAD_DIM
    rep = GQA_Q_HEADS // GQA_KV_HEADS

    def norm_rope_bwd(x, gain, dy, cos_v, sin_v):
        dyn = dy * cos_v - _swap_pairs(dy) * sin_v
        r = lax.rsqrt(_head_mean(x * x) + EPS)
        xn = x * r
        dxn = dyn * gain
        dx = r * (dxn - xn * _head_mean(dxn * xn))
        return dx, dyn * xn

    def body(g_ref, qg_ref, kg_ref, c_ref, s_ref, dqp_ref, dkp_ref, dv_ref, dg_ref, sq_ref, sk_ref):
        i = pl.program_id(0)
        cos_v, sin_v = c_ref[...], s_ref[...]
        lo = _lane_lt64((TM, LANES))
        chunks = []
        for p in range(GQA_Q_HEADS // 2):
            g = (2 * p) // rep
            da = dqp_ref[:, 2 * LANES * p:2 * LANES * p + LANES]
            db = dqp_ref[:, 2 * LANES * p + LANES:2 * LANES * (p + 1)]
            if g == 0:
                chunks.append(jnp.where(lo, da, _roll64(db)))
            else:
                chunks.append(jnp.where(lo, _roll64(da), db))
        dyq = jnp.concatenate(chunks, axis=1) * SCALE
        dxq, gq_term = norm_rope_bwd(g_ref[:, :nq].astype(F32), qg_ref[...], dyq,
                                     _tile_lanes(cos_v, nq), _tile_lanes(sin_v, nq))
        dxk, gk_term = norm_rope_bwd(g_ref[:, nq:nq + nk].astype(F32), kg_ref[...], dkp_ref[...], cos_v, sin_v)
        dg_ref[:, :nq] = dxq.astype(BF16)
        dg_ref[:, nq:nq + nk] = dxk.astype(BF16)
        dg_ref[:, nq + nk:] = dv_ref[...].astype(BF16)
        _acc(sq_ref, _colsum8(gq_term), i == 0)
        _acc(sk_ref, _colsum8(gk_term), i == 0)

    W = gqkv.shape[1]
    return pl.pallas_call(
        body, name="gqa_prep_bwd", grid=(T // TM,),
        in_specs=[pl.BlockSpec((TM, W), lambda i: (i, 0)),
                  pl.BlockSpec((1, nq), lambda i: (0, 0)), pl.BlockSpec((1, nk), lambda i: (0, 0)),
                  pl.BlockSpec((TM, nk), lambda i: (i, 0)), pl.BlockSpec((TM, nk), lambda i: (i, 0)),
                  pl.BlockSpec((TM, GQA_Q_HEADS * LANES), lambda i: (i, 0)),
                  pl.BlockSpec((TM, nk), lambda i: (i, 0)), pl.BlockSpec((TM, nk), lambda i: (i, 0))],
        out_specs=[pl.BlockSpec((TM, W), lambda i: (i, 0)),
                   pl.BlockSpec((1, 8, nq), lambda i: (0, 0, 0)), pl.BlockSpec((1, 8, nk), lambda i: (0, 0, 0))],
        out_shape=[jax.ShapeDtypeStruct((T, W), BF16), jax.ShapeDtypeStruct((1, 8, nq), F32),
                   jax.ShapeDtypeStruct((1, 8, nk), F32)],
        compiler_params=_cp(("arbitrary",)),
    )(gqkv, qgain, kgain, cos_t, sin_t, dqp, dkp, dv)


GQ_TQ_FWD = 1024
GQ_TQ_BWD = 512
GQ_TK = 256
GQ_CHUNKS_PER_TRIP = 8
GQ_FORWARD_STEPS = 1


def _chunk_offset(j):
    return j * GQ_TK if isinstance(j, int) else pl.multiple_of(j * GQ_TK, GQ_TK)


def _gqa_blocks(n_lat, T, latent, tq_latent):
    if latent:
        tq = tq_latent if n_lat % tq_latent == 0 else GQ_TK
        return tq, 0, n_lat // tq
    return GQ_TK, n_lat // GQ_TK, (T - n_lat) // GQ_TK


def _sweep_keys(chunk, latent, n_lat_k, nk):
    if latent:
        per_trip = math.gcd(n_lat_k, GQ_CHUNKS_PER_TRIP)

        def trip(jj, carry):
            for t in range(per_trip):
                chunk(per_trip * jj + t)
            return carry

        lax.fori_loop(0, n_lat_k // per_trip, trip, 0)
    for j in range(n_lat_k, nk):
        chunk(j)


def _gqa_place(chunk, half, g):
    gv = jnp.broadcast_to(g, chunk.shape)
    placed = jnp.where(gv == half, chunk, _roll64(chunk))
    return jnp.where(_lane_half(chunk.shape) == gv, placed, jnp.zeros_like(placed))


GQA_SMALL_SCORE = 20.0


def gqa_scores_small(q_gain, k_gain):
    bound = SCALE * HEAD_DIM * jnp.max(jnp.abs(q_gain)) * jnp.max(jnp.abs(k_gain))
    return (bound <= GQA_SMALL_SCORE).astype(jnp.int32).reshape(1)


def gqa_fwd_call(qp, kp, gqkv, n_lat, scores_small, gather=None):
    first = _gqa_fwd_part(qp, kp, gqkv, n_lat, scores_small, True, None, gather)
    y, lse = _gqa_fwd_part(qp, kp, gqkv, n_lat, scores_small, False, (first[0], first[1]), None)
    return (y, lse) + tuple(first[2:])


def _gqa_fwd_part(qp, kp, gqkv, n_lat, scores_small, latent, prev, gather):
    T = qp.shape[0]
    rep = GQA_Q_HEADS // GQA_KV_HEADS
    GQ_TQ, q0, nq = _gqa_blocks(n_lat, T, latent, GQ_TQ_FWD)
    nk = T // GQ_TK
    n_lat_k = n_lat // GQ_TK
    v_blk = (GQA_Q_HEADS + GQA_KV_HEADS) * HEAD_DIM // LANES
    R = rep * GQ_TQ

    def body(*refs):
        if prev is not None:
            small_ref, q_ref, k_ref, v_ref, _, _, y_ref, lse_ref, m_sc, acc_sc = refs
        elif gather is None:
            small_ref, q_ref, k_ref, v_ref, y_ref, lse_ref, m_sc, acc_sc = refs
        else:
            nb = len(gather)
            small_ref, q_ref, k_ref, v_ref = refs[:4]
            pk_refs = refs[4:4 + nb]
            y_ref, lse_ref = refs[4 + nb:6 + nb]
            ga_refs = refs[6 + nb:6 + 2 * nb]
            m_sc, acc_sc, ssem, rsem = refs[6 + 2 * nb:]
        g = pl.program_id(0)
        qb = pl.program_id(1)
        if gather is not None:
            @pl.when((g == 0) & (qb == 0))
            def _():
                _ag_start(pk_refs, ga_refs, ssem, rsem)
        qs = jnp.concatenate([q_ref[:, LANES * i:LANES * (i + 1)] for i in range(rep)], axis=0)
        is_lat = latent
        nsub = GQ_TK // LANES
        scores_small = small_ref[0] == 1

        @pl.when(scores_small)
        def _():
            m_sc[...] = jnp.zeros((R, LANES), F32)

        @pl.when(jnp.logical_not(scores_small))
        def _():
            m_sc[...] = jnp.full((R, LANES), NEG, F32)

            def chunk_max(j):
                s = _dot_nt(qs, k_ref[pl.ds(_chunk_offset(j), GQ_TK), :])
                mm = s[:, :LANES]
                for t in range(1, nsub):
                    mm = jnp.maximum(mm, s[:, LANES * t:LANES * (t + 1)])
                m_sc[...] = jnp.maximum(m_sc[...], mm)

            _sweep_keys(chunk_max, is_lat, n_lat_k, nk)
            m_sc[...] = jnp.broadcast_to(jnp.max(m_sc[...], axis=-1, keepdims=True), (R, LANES))

        acc_sc[...] = jnp.zeros((R, LANES), F32)
        own_v = _lane_half((GQ_TK, LANES)) == jnp.broadcast_to(g, (GQ_TK, LANES))

        def chunk_acc(j):
            off = _chunk_offset(j)
            vc = v_ref[pl.ds(off, GQ_TK), :]
            v1 = jnp.where(own_v, vc, jnp.ones_like(vc))
            s = _dot_nt(qs, k_ref[pl.ds(off, GQ_TK), :])
            mb = m_sc[...]
            p = jnp.concatenate([jnp.exp(s[:, LANES * t:LANES * (t + 1)] - mb).astype(BF16) for t in range(nsub)],
                                axis=1)
            acc_sc[...] += _dot(p, v1)

        _sweep_keys(chunk_acc, is_lat, n_lat_k, nk)
        acc = acc_sc[...]
        own = _lane_half((R, LANES)) == jnp.broadcast_to(g, (R, LANES))
        l = jnp.where(own, _roll64(acc), acc)
        o = acc / l
        lse = m_sc[...] + jnp.log(l)
        lo = _lane_lt64((GQ_TQ, LANES))
        g0 = jnp.broadcast_to(g, (GQ_TQ, LANES)) == 0
        for c2 in range(rep // 2):
            oa = o[(2 * c2) * GQ_TQ:(2 * c2 + 1) * GQ_TQ]
            ob = o[(2 * c2 + 1) * GQ_TQ:(2 * c2 + 2) * GQ_TQ]
            left = jnp.where(g0, oa, _roll64(oa))
            right = jnp.where(g0, _roll64(ob), ob)
            y_ref[:, LANES * c2:LANES * (c2 + 1)] = jnp.where(lo, left, right).astype(BF16)
        for i in range(rep):
            lse_ref[i] = lse[i * GQ_TQ:(i + 1) * GQ_TQ]
        if gather is not None:
            @pl.when((g == GQA_KV_HEADS - 1) & (qb == max(nq - 1 - GQ_FORWARD_STEPS, 0)))
            def _():
                _ag_forward(pk_refs, ga_refs, ssem, rsem)

            @pl.when((g == GQA_KV_HEADS - 1) & (qb == nq - 1))
            def _():
                _ag_finish(pk_refs, ga_refs, ssem, rsem)

    in_specs = [pl.BlockSpec((GQ_TQ, rep * LANES), lambda g, i, s: (q0 + i, g)),
                pl.BlockSpec((T, LANES), lambda g, i, s: (0, 0)),
                pl.BlockSpec((T, LANES), lambda g, i, s: (0, v_blk))]
    out_specs = [pl.BlockSpec((GQ_TQ, rep * HEAD_DIM), lambda g, i, s: (q0 + i, g)),
                 pl.BlockSpec((rep, GQ_TQ, LANES), lambda g, i, s: (g, q0 + i, 0))]
    out_shape = [jax.ShapeDtypeStruct((T, GQA_Q_HEADS * HEAD_DIM), BF16),
                 jax.ShapeDtypeStruct((GQA_Q_HEADS, T, LANES), F32)]
    scratch = [pltpu.VMEM((R, LANES), F32), pltpu.VMEM((R, LANES), F32)]
    args = [scores_small, qp, kp, gqkv]
    aliases = {}
    name = "gqa_fwd" if latent else "gqa_fwd_ctx"
    if prev is not None:
        in_specs += [ANY, ANY]
        args += list(prev)
        aliases = {4: 0, 5: 1}
    if gather is not None:
        in_specs += [ANY] * len(gather)
        out_specs += [ANY] * len(gather)
        out_shape += [jax.ShapeDtypeStruct((N_CHIP,) + p.shape, p.dtype) for p in gather]
        scratch += [pltpu.SemaphoreType.DMA((AG_SEMS * len(gather),)), pltpu.SemaphoreType.DMA((AG_SEMS * len(gather),))]
        args += list(gather)
        name = "gqa_fwd_gather"
    return pl.pallas_call(
        body, name=name,
        grid_spec=pltpu.PrefetchScalarGridSpec(
            num_scalar_prefetch=1, grid=(GQA_KV_HEADS, nq), in_specs=in_specs, out_specs=out_specs,
            scratch_shapes=scratch),
        out_shape=out_shape, input_output_aliases=aliases,
        compiler_params=_cp(("arbitrary", "arbitrary") if gather is not None else ("parallel", "parallel"), VMEM_BIG),
    )(*args)


def gqa_bwd_call(qp, kp, gqkv, yb, dyb, lse, n_lat, exchange=None):
    first = _gqa_bwd_part(qp, kp, gqkv, yb, dyb, lse, n_lat, True, None, exchange)
    dq, dk_c, dv_c = _gqa_bwd_part(qp, kp, gqkv, yb, dyb, lse, n_lat, False, first[0], None)
    return (dq, first[1] + dk_c, first[2] + dv_c) + tuple(first[3:])


def _gqa_bwd_part(qp, kp, gqkv, yb, dyb, lse, n_lat, latent, prev_dq, exchange):
    T = qp.shape[0]
    rep = GQA_Q_HEADS // GQA_KV_HEADS
    GQ_TQ, q0, nq = _gqa_blocks(n_lat, T, latent, GQ_TQ_BWD)
    nk = T // GQ_TK
    n_lat_k = n_lat // GQ_TK
    v_blk = (GQA_Q_HEADS + GQA_KV_HEADS) * HEAD_DIM // LANES
    R = rep * GQ_TQ

    def body(*refs):
        if prev_dq is not None:
            q_ref, k_ref, v_ref, y_ref, dy_ref, lse_ref, _, dq_ref, dk_ref, dv_ref, dq_sc = refs
        elif exchange is None:
            q_ref, k_ref, v_ref, y_ref, dy_ref, lse_ref, dq_ref, dk_ref, dv_ref, dq_sc = refs
        else:
            nb = len(exchange)
            q_ref, k_ref, v_ref, y_ref, dy_ref, lse_ref = refs[:6]
            gp_refs = refs[6:6 + nb]
            dq_ref, dk_ref, dv_ref = refs[6 + nb:9 + nb]
            rx_refs = refs[9 + nb:9 + 2 * nb]
            dq_sc, ssem, rsem = refs[9 + 2 * nb:]
        g = pl.program_id(0)
        qb = pl.program_id(1)
        if exchange is not None:
            @pl.when((g == 0) & (qb == 0))
            def _():
                _rs_start(gp_refs, rx_refs, ssem, rsem)

        @pl.when((g == 0) & (qb == 0))
        def _():
            dk_ref[...] = jnp.zeros((T, LANES), F32)
            dv_ref[...] = jnp.zeros((T, LANES), F32)

        qs = jnp.concatenate([q_ref[:, LANES * i:LANES * (i + 1)] for i in range(rep)], axis=0)
        dos, deltas, lses = [], [], []
        for i in range(rep):
            c2, half = i // 2, i % 2
            dch = _gqa_place(dy_ref[:, LANES * c2:LANES * (c2 + 1)].astype(F32), half, g)
            ych = _gqa_place(y_ref[:, LANES * c2:LANES * (c2 + 1)].astype(F32), half, g)
            dos.append(dch.astype(BF16))
            deltas.append(jnp.sum(dch * ych, axis=-1, keepdims=True))
            lses.append(lse_ref[i])
        do_s = jnp.concatenate(dos, axis=0)
        delta = jnp.broadcast_to(jnp.concatenate(deltas, axis=0), (R, LANES))
        lse_v = jnp.concatenate(lses, axis=0)
        dq_sc[...] = jnp.zeros((R, LANES), F32)
        nsub = GQ_TK // LANES

        def chunk(j):
            off = _chunk_offset(j)
            kc = k_ref[pl.ds(off, GQ_TK), :]
            vc = v_ref[pl.ds(off, GQ_TK), :]
            s = _dot_nt(qs, kc)
            dp = _dot_nt(do_s, vc)
            ps, dss = [], []
            for t in range(nsub):
                pt = jnp.exp(s[:, LANES * t:LANES * (t + 1)] - lse_v)
                ps.append(pt.astype(BF16))
                dss.append((pt * (dp[:, LANES * t:LANES * (t + 1)] - delta)).astype(BF16))
            p = jnp.concatenate(ps, axis=1)
            ds = jnp.concatenate(dss, axis=1)
            dv_ref[pl.ds(off, GQ_TK), :] += _dot_tn(p, do_s)
            dk_ref[pl.ds(off, GQ_TK), :] += _dot_tn(ds, qs)
            dq_sc[...] += _dot(ds, kc)

        _sweep_keys(chunk, latent, n_lat_k, nk)
        for i in range(rep):
            dq_ref[:, LANES * i:LANES * (i + 1)] = dq_sc[i * GQ_TQ:(i + 1) * GQ_TQ, :]
        if exchange is not None:
            @pl.when((g == GQA_KV_HEADS - 1) & (qb == nq - 1))
            def _():
                _rs_finish(gp_refs, rx_refs, ssem, rsem)

    whole = pl.BlockSpec((T, LANES), lambda g, i: (0, 0))
    in_specs = [pl.BlockSpec((GQ_TQ, rep * LANES), lambda g, i: (q0 + i, g)),
                whole,
                pl.BlockSpec((T, LANES), lambda g, i: (0, v_blk)),
                pl.BlockSpec((GQ_TQ, rep * HEAD_DIM), lambda g, i: (q0 + i, g)),
                pl.BlockSpec((GQ_TQ, rep * HEAD_DIM), lambda g, i: (q0 + i, g)),
                pl.BlockSpec((rep, GQ_TQ, LANES), lambda g, i: (g, q0 + i, 0))]
    out_specs = [pl.BlockSpec((GQ_TQ, rep * LANES), lambda g, i: (q0 + i, g)), whole, whole]
    out_shape = [jax.ShapeDtypeStruct((T, GQA_Q_HEADS * LANES), F32),
                 jax.ShapeDtypeStruct((T, LANES), F32), jax.ShapeDtypeStruct((T, LANES), F32)]
    scratch = [pltpu.VMEM((R, LANES), F32)]
    args = [qp, kp, gqkv, yb, dyb, lse]
    aliases = {}
    name = "gqa_bwd" if latent else "gqa_bwd_ctx"
    if prev_dq is not None:
        in_specs.append(ANY)
        args.append(prev_dq)
        aliases = {6: 0}
    if exchange is not None:
        in_specs += [ANY] * len(exchange)
        out_specs += [ANY] * len(exchange)
        out_shape += [_rs_out_shape(g) for g in exchange]
        scratch += [pltpu.SemaphoreType.DMA((RS_SEMS * len(exchange),)), pltpu.SemaphoreType.DMA((RS_SEMS * len(exchange),))]
        args += list(exchange)
        name = "gqa_bwd_exchange"
    return pl.pallas_call(
        body, name=name, grid=(GQA_KV_HEADS, nq),
        in_specs=in_specs, out_specs=out_specs, out_shape=out_shape, scratch_shapes=scratch,
        input_output_aliases=aliases,
        compiler_params=_cp(("arbitrary", "arbitrary"), VMEM_BIG),
    )(*args)


def _na_tables(rows):
    kh = min(NA_WIN_H, rows)
    assert kh == NA_WIN_H and rows >= NA_KROWS and rows % NA_ROWS == 0
    a = np.zeros((3, NA_ROWS, NA_KROWS, 2 * NA_WIN_H - 1), np.float32)
    for v, r0 in enumerate((0, NA_ROWS, rows - NA_ROWS)):
        ks = min(max(r0 - NA_WIN_H // 2, 0), rows - NA_KROWS)
        for rq in range(NA_ROWS):
            r = r0 + rq
            rs = min(max(r - kh // 2, 0), rows - kh)
            for rk in range(NA_KROWS):
                kr = ks + rk
                if rs <= kr < rs + kh:
                    a[v, rq, rk, kr - r + NA_WIN_H - 1] = 1.0
    c = np.zeros((GRID_W, GRID_W, 2 * NA_WIN_W - 1), np.float32)
    for wq in range(GRID_W):
        cs = min(max(wq - NA_WIN_W // 2, 0), GRID_W - NA_WIN_W)
        for wk in range(cs, cs + NA_WIN_W):
            c[wq, wk, min(max(wk - wq, -(NA_WIN_W - 1)), NA_WIN_W - 1) + NA_WIN_W - 1] = 1.0
    return a, c


NA_DR = 2 * NA_WIN_H - 1


def _na_tile_index(rows):
    a, _ = _na_tables(rows)
    idx = np.where(a.sum(-1) > 0, a.argmax(-1), NA_DR).astype(np.int32)
    return jnp.asarray(idx.reshape(-1))


def na_bias_table(rpb, rows):
    _, c = _na_tables(rows)
    H = rpb.shape[0]
    KW = NA_KROWS * GRID_W
    c2 = np.concatenate([c, c], axis=1)
    t1 = jnp.einsum("hde,wue->hdwu", rpb, jnp.asarray(c2), precision=lax.Precision.HIGHEST)
    t1 = jnp.where(jnp.asarray(c2.sum(-1) > 0)[None, None], t1, NEG)
    t1 = jnp.concatenate([t1, jnp.full((H, 1, GRID_W, LANES), NEG, F32)], axis=1)

    def body(idx_ref, t_ref, o_ref):
        v = pl.program_id(1)
        lo = _lane_lt64((GRID_W, LANES))
        for q in range(NA_ROWS):
            for kp in range(NA_KROWS // 2):
                base = (v * NA_ROWS + q) * NA_KROWS + 2 * kp
                tile = jnp.where(lo, t_ref[0, idx_ref[base]], t_ref[0, idx_ref[base + 1]])
                o_ref[0, 0, GRID_W * q:GRID_W * (q + 1), LANES * kp:LANES * (kp + 1)] = tile

    return pl.pallas_call(
        body, name="na_bias_table",
        grid_spec=pltpu.PrefetchScalarGridSpec(
            num_scalar_prefetch=1, grid=(H, 3),
            in_specs=[pl.BlockSpec((1, NA_DR + 1, GRID_W, LANES), lambda h, v, i: (h, 0, 0, 0))],
            out_specs=pl.BlockSpec((1, 1, TM, KW), lambda h, v, i: (h, v, 0, 0))),
        out_shape=jax.ShapeDtypeStruct((H, 3, TM, KW), F32),
        compiler_params=_cp(("parallel", "parallel")),
    )(_na_tile_index(rows), t1)


def na_bias_grad(dbias, rows):
    _, c = _na_tables(rows)
    H = dbias.shape[0]
    KW = NA_KROWS * GRID_W

    def body(idx_ref, d_ref, o_ref):
        v = pl.program_id(1)
        lo = _lane_lt64((GRID_W, LANES))

        @pl.when(v == 0)
        def _():
            o_ref[...] = jnp.zeros((1, NA_DR + 1, GRID_W, LANES), F32)

        for q in range(NA_ROWS):
            for kp in range(NA_KROWS // 2):
                base = (v * NA_ROWS + q) * NA_KROWS + 2 * kp
                tile = d_ref[0, 0, GRID_W * q:GRID_W * (q + 1), LANES * kp:LANES * (kp + 1)]
                o_ref[0, idx_ref[base]] += jnp.where(lo, tile, 0.0)
                o_ref[0, idx_ref[base + 1]] += jnp.where(lo, 0.0, tile)

    dt = pl.pallas_call(
        body, name="na_bias_grad",
        grid_spec=pltpu.PrefetchScalarGridSpec(
            num_scalar_prefetch=1, grid=(H, 3),
            in_specs=[pl.BlockSpec((1, 1, TM, KW), lambda h, v, i: (h, v, 0, 0))],
            out_specs=pl.BlockSpec((1, NA_DR + 1, GRID_W, LANES), lambda h, v, i: (h, 0, 0, 0))),
        out_shape=jax.ShapeDtypeStruct((H, NA_DR + 1, GRID_W, LANES), F32),
        compiler_params=_cp(("parallel", "arbitrary")),
    )(_na_tile_index(rows), dbias)
    dt = dt[:, :NA_DR, :, :GRID_W] + dt[:, :NA_DR, :, GRID_W:]
    return jnp.einsum("hdwu,wue->hde", dt, jnp.asarray(c), precision=lax.Precision.HIGHEST)


def _na_variant(blk, n_lat_blk):
    return jnp.where(blk == 0, 0, jnp.where(blk >= n_lat_blk - 1, 2, 1))


def _na_kstart(blk, rows):
    r0 = blk * NA_ROWS
    ks = jnp.clip(r0 - NA_WIN_H // 2, 0, rows - NA_KROWS)
    return pl.multiple_of(ks * GRID_W, GRID_W)


def na_fwd_call(naqkv, bias, n_lat):
    T = naqkv.shape[0]
    nb = T // TM
    n_lat_blk = n_lat // TM
    rows = n_lat // GRID_W
    n_ctx = T - n_lat
    KW = NA_KROWS * GRID_W
    npair = NA_HEADS // 2

    def body(q_ref, k_ref, v_ref, b_ref, y_ref, lse_ref):
        blk = pl.program_id(1)
        half = _lane_half((TM, LANES))
        lo = half == 0
        q = q_ref[...].astype(F32) * SCALE
        kctx = k_ref[pl.ds(n_lat, n_ctx), :]
        vctx = v_ref[pl.ds(n_lat, n_ctx), :]

        @pl.when(blk < n_lat_blk)
        def _():
            off = _na_kstart(blk, rows)
            kwin = k_ref[pl.ds(off, KW), :]
            vwin = v_ref[pl.ds(off, KW), :]
            outs = []
            for hh in range(2):
                qm = jnp.where(half == hh, q, 0.0).astype(BF16)
                sw = _dot_nt(qm, kwin) + b_ref[hh, 0]
                sc = _dot_nt(qm, kctx)
                m = jnp.maximum(jnp.max(sw, axis=-1, keepdims=True), jnp.max(sc, axis=-1, keepdims=True))
                pw = jnp.exp(sw - m)
                pc = jnp.exp(sc - m)
                l = jnp.sum(pw, axis=-1, keepdims=True) + jnp.sum(pc, axis=-1, keepdims=True)
                outs.append((_dot(pw.astype(BF16), vwin) + _dot(pc.astype(BF16), vctx)) / l)
                lse_ref[hh] = jnp.broadcast_to(m + jnp.log(l), (TM, LANES))
            y_ref[...] = jnp.where(lo, outs[0], outs[1]).astype(BF16)

        @pl.when(blk >= n_lat_blk)
        def _():
            outs = []
            for hh in range(2):
                qm = jnp.where(half == hh, q, 0.0).astype(BF16)
                sc = _dot_nt(qm, kctx)
                m = jnp.max(sc, axis=-1, keepdims=True)
                pc = jnp.exp(sc - m)
                l = jnp.sum(pc, axis=-1, keepdims=True)
                outs.append(_dot(pc.astype(BF16), vctx) / l)
                lse_ref[hh] = jnp.broadcast_to(m + jnp.log(l), (TM, LANES))
            y_ref[...] = jnp.where(lo, outs[0], outs[1]).astype(BF16)

    return pl.pallas_call(
        body, name="na_fwd", grid=(npair, nb),
        in_specs=[pl.BlockSpec((TM, LANES), lambda p, b: (b, p)),
                  pl.BlockSpec((T, LANES), lambda p, b: (0, npair + p)),
                  pl.BlockSpec((T, LANES), lambda p, b: (0, 2 * npair + p)),
                  pl.BlockSpec((2, 1, TM, KW), lambda p, b: (p, _na_variant(b, n_lat_blk), 0, 0))],
        out_specs=[pl.BlockSpec((TM, LANES), lambda p, b: (b, p)),
                   pl.BlockSpec((2, TM, LANES), lambda p, b: (p, b, 0))],
        out_shape=[jax.ShapeDtypeStruct((T, NA_HEADS * HEAD_DIM), BF16),
                   jax.ShapeDtypeStruct((NA_HEADS, T, LANES), F32)],
        compiler_params=_cp(("parallel", "parallel"), VMEM_BIG),
    )(naqkv, naqkv, naqkv, bias)


def na_bwd_call(naqkv, bias, ya, dya, lse, n_lat):
    T = naqkv.shape[0]
    nb = T // TM
    n_lat_blk = n_lat // TM
    rows = n_lat // GRID_W
    n_ctx = T - n_lat
    KW = NA_KROWS * GRID_W
    npair = NA_HEADS // 2

    def body(q_ref, k_ref, v_ref, b_ref, y_ref, dy_ref, lse_ref, dq_ref, dk_ref, dv_ref, db_ref, dk_sc, dv_sc):
        blk = pl.program_id(1)
        half = _lane_half((TM, LANES))
        lo = half == 0
        q = q_ref[...].astype(F32) * SCALE
        kctx = k_ref[pl.ds(n_lat, n_ctx), :]
        vctx = v_ref[pl.ds(n_lat, n_ctx), :]
        dyv = dy_ref[...].astype(F32)
        yv = y_ref[...].astype(F32)

        @pl.when(blk == 0)
        def _():
            dk_sc[...] = jnp.zeros((T, LANES), F32)
            dv_sc[...] = jnp.zeros((T, LANES), F32)

        @pl.when(blk < n_lat_blk)
        def _():
            off = _na_kstart(blk, rows)
            kwin = k_ref[pl.ds(off, KW), :]
            vwin = v_ref[pl.ds(off, KW), :]
            first = (blk == 0) | (blk == 1) | (blk == n_lat_blk - 1)
            dqs = []
            for hh in range(2):
                sel = half == hh
                qm = jnp.where(sel, q, 0.0).astype(BF16)
                dom = jnp.where(sel, dyv, 0.0)
                delta = jnp.sum(dom * yv, axis=-1, keepdims=True)
                dom = dom.astype(BF16)
                lse_v = lse_ref[hh][:, 0:1]
                pw = jnp.exp(_dot_nt(qm, kwin) + b_ref[hh, 0] - lse_v)
                pc = jnp.exp(_dot_nt(qm, kctx) - lse_v)
                dsw = pw * (_dot_nt(dom, vwin) - delta)
                dsc = pc * (_dot_nt(dom, vctx) - delta)

                @pl.when(first)
                def _():
                    db_ref[hh, 0] = dsw

                @pl.when(jnp.logical_not(first))
                def _():
                    db_ref[hh, 0] += dsw

                dsw = dsw.astype(BF16)
                dsc = dsc.astype(BF16)
                dv_sc[pl.ds(off, KW), :] += _dot_tn(pw.astype(BF16), dom)
                dv_sc[pl.ds(n_lat, n_ctx), :] += _dot_tn(pc.astype(BF16), dom)
                dk_sc[pl.ds(off, KW), :] += _dot_tn(dsw, qm)
                dk_sc[pl.ds(n_lat, n_ctx), :] += _dot_tn(dsc, qm)
                dqs.append((_dot(dsw, kwin) + _dot(dsc, kctx)) * SCALE)
            dq_ref[...] = jnp.where(lo, dqs[0], dqs[1]).astype(BF16)

        @pl.when(blk >= n_lat_blk)
        def _():
            dqs = []
            for hh in range(2):
                sel = half == hh
                qm = jnp.where(sel, q, 0.0).astype(BF16)
                dom = jnp.where(sel, dyv, 0.0)
                delta = jnp.sum(dom * yv, axis=-1, keepdims=True)
                dom = dom.astype(BF16)
                pc = jnp.exp(_dot_nt(qm, kctx) - lse_ref[hh][:, 0:1])
                dsc = (pc * (_dot_nt(dom, vctx) - delta)).astype(BF16)
                dv_sc[pl.ds(n_lat, n_ctx), :] += _dot_tn(pc.astype(BF16), dom)
                dk_sc[pl.ds(n_lat, n_ctx), :] += _dot_tn(dsc, qm)
                dqs.append(_dot(dsc, kctx) * SCALE)
            dq_ref[...] = jnp.where(lo, dqs[0], dqs[1]).astype(BF16)

        @pl.when(blk == nb - 1)
        def _():
            dk_ref[...] = dk_sc[...].astype(BF16)
            dv_ref[...] = dv_sc[...].astype(BF16)

    blkspec = pl.BlockSpec((TM, LANES), lambda p, b: (b, p))
    col = pl.BlockSpec((T, LANES), lambda p, b: (0, p))
    bspec = pl.BlockSpec((2, 1, TM, KW), lambda p, b: (p, _na_variant(b, n_lat_blk), 0, 0))
    dq, dk, dv, db = pl.pallas_call(
        body, name="na_bwd", grid=(npair, nb),
        in_specs=[blkspec,
                  pl.BlockSpec((T, LANES), lambda p, b: (0, npair + p)),
                  pl.BlockSpec((T, LANES), lambda p, b: (0, 2 * npair + p)),
                  bspec, blkspec, blkspec,
                  pl.BlockSpec((2, TM, LANES), lambda p, b: (p, b, 0))],
        out_specs=[blkspec, col, col, bspec],
        out_shape=[jax.ShapeDtypeStruct((T, NA_HEADS * HEAD_DIM), BF16)] * 3
        + [jax.ShapeDtypeStruct((NA_HEADS, 3, TM, KW), F32)],
        scratch_shapes=[pltpu.VMEM((T, LANES), F32), pltpu.VMEM((T, LANES), F32)],
        compiler_params=_cp(("arbitrary", "arbitrary"), VMEM_BIG),
    )(naqkv, naqkv, naqkv, bias, ya, dya, lse)
    return jnp.concatenate([dq, dk, dv], axis=1), db


def rope_tables(n_lat, n_ctx):
    t = jnp.arange(n_lat)
    row = (t // GRID_W).astype(F32)
    col = (t % GRID_W).astype(F32)
    half = HEAD_DIM // 2
    inv = ROPE_THETA ** (-jnp.arange(0, half, 2, dtype=F32) / half)
    ang = jnp.concatenate([row[:, None] * inv, col[:, None] * inv], axis=-1)
    cos = jnp.repeat(jnp.cos(ang), 2, axis=-1)
    sin = jnp.repeat(jnp.sin(ang), 2, axis=-1) * jnp.tile(jnp.array([-1.0, 1.0], F32), half)
    cos = jnp.concatenate([cos, jnp.ones((n_ctx, HEAD_DIM), F32)], axis=0)
    sin = jnp.concatenate([sin, jnp.zeros((n_ctx, HEAD_DIM), F32)], axis=0)
    return jnp.tile(cos, (1, 2)), jnp.tile(sin, (1, 2))


def _row(v):
    return v.reshape(1, -1)


def device_fwd_bwd(xs, target, modv, w, small, n_lat, ex=None):
    T, D = xs.shape
    n_ctx = T - n_lat
    L = modv.shape[0]
    nlt = n_lat // TM
    rows = n_lat // GRID_W
    cos_t, sin_t = rope_tables(n_lat, n_ctx)
    qg = [jnp.tile(_row(small["q_gain"][l]), (1, GQA_Q_HEADS)) for l in range(L)]
    kg = [jnp.tile(_row(small["k_gain"][l]), (1, GQA_KV_HEADS)) for l in range(L)]
    bias = [na_bias_table(small["na_rpb"][l], rows) for l in range(L)]

    later = {}

    def wt(name, l):
        if ex is None or (name == "w_in" and l == 0):
            return w[name][l]
        return later[name][l - 1] if name == "w_in" else later[name][l]

    saved = []
    x = xs
    h = norm_mod_call(x, _row(small["norm1"][0]), modv[0], nlt, 0)
    for l in range(L):
        naqkv, gqkv, gates = qkv_call(h, wt("w_in", l))
        ya, lse_a = na_fwd_call(naqkv, bias[l], n_lat)
        qp, kp = gqa_prep_call(gqkv, qg[l], kg[l], cos_t, sin_t)
        small_scores = gqa_scores_small(small["q_gain"][l], small["k_gain"][l])
        if ex is not None and l == 0:
            yb, lse_b, *gathered = gqa_fwd_call(qp, kp, gqkv, n_lat, small_scores, gather=ex.rest())
            w_in_later, others = ex.rest_whole(gathered)
            later.update(w_in_later)
            later.update(others)
        else:
            yb, lse_b = gqa_fwd_call(qp, kp, gqkv, n_lat, small_scores)
        merged, pa, pb = merge_call(ya, yb, gates, wt("w_pa", l), wt("w_pb", l))
        x1, osave, h2 = wo_call(merged, wt("w_o", l), x, modv[l], _row(small["norm2"][l]), nlt)
        act, a, u = ffn_in_call(h2, wt("w_ffn_in", l))
        if l + 1 < L:
            x2, fsave, hn = ffn_out_call(act, wt("w_ffn_out", l), x1, modv[l], nlt,
                                         _row(small["norm1"][l + 1]), modv[l + 1])
        else:
            x2, fsave = ffn_out_call(act, wt("w_ffn_out", l), x1, modv[l], nlt)
            hn = None
        saved.append(dict(x=x, h=h, naqkv=naqkv, gqkv=gqkv, gates=gates, ya=ya, lse_a=lse_a, qp=qp, kp=kp,
                          yb=yb, lse_b=lse_b, merged=merged, pa=pa, pb=pb, x1=x1, osave=osave, h2=h2,
                          act=act, a=a, u=u, fsave=fsave))
        x, h = x2, hn

    dx, d_final, loss_part = loss_call(x, _row(small["final_norm"]), target, nlt)

    gw = {k: [None] * L for k in ("w_in", "w_pa", "w_pb", "w_o", "w_ffn_in", "w_ffn_out")}
    gs = {k: [None] * L for k in ("norm1", "norm2", "na_rpb", "q_gain", "k_gain")}
    dmod = [None] * L
    fold = lambda s: s.sum(axis=1)
    for l in reversed(range(L)):
        s = saved[l]
        da, du, df, s_g2 = dact_call(dx, modv[l], s["fsave"], wt("w_ffn_out", l), s["a"], s["u"], nlt)
        gw["w_ffn_out"][l] = wgrad_call(s["act"], df, "wgrad_ffn_out")
        dx1, s_sh2, s_sc2, s_n2 = dh2_call(da, du, wt("w_ffn_in", l), s["x1"], modv[l], _row(small["norm2"][l]), dx, nlt)
        gw["w_ffn_in"][l] = jnp.concatenate(
            [wgrad_call(s["h2"], da, "wgrad_ffn_in_a"), wgrad_call(s["h2"], du, "wgrad_ffn_in_u")], axis=1)
        dpa, dpb, dgates, do, s_g1 = dmerged_call(dx1, modv[l], s["osave"], wt("w_o", l), s["pa"], s["pb"], s["gates"], nlt)
        gw["w_o"][l] = wgrad_call(s["merged"], do, "wgrad_o")
        dya, dyb = dy_call(dpa, dpb, wt("w_pa", l), wt("w_pb", l))
        gw["w_pa"][l] = wgrad_call(s["ya"], dpa, "wgrad_pa")
        gw["w_pb"][l] = wgrad_call(s["yb"], dpb, "wgrad_pb")
        dna, dbias = na_bwd_call(s["naqkv"], bias[l], s["ya"], dya, s["lse_a"], n_lat)
        gs["na_rpb"][l] = na_bias_grad(dbias, rows)
        if ex is not None:
            ex.pack_grads(l, {k: v[l] for k, v in gw.items() if k != "w_in"}, [gi for gi in range(len(GROUPS)) if gi != G_IN])
            keys = [(l, gi) for gi in range(len(GROUPS)) if gi != G_IN] + ([(l + 1, G_IN)] if l + 1 < L else [])
            dqp, dkp, dv, *got = gqa_bwd_call(s["qp"], s["kp"], s["gqkv"], s["yb"], dyb, s["lse_b"], n_lat,
                                              exchange=[ex.partials[k] for k in keys])
            ex.received.update(zip(keys, got))
        else:
            dqp, dkp, dv = gqa_bwd_call(s["qp"], s["kp"], s["gqkv"], s["yb"], dyb, s["lse_b"], n_lat)
        dg, s_qg, s_kg = gqa_prep_bwd_call(s["gqkv"], qg[l], kg[l], cos_t, sin_t, dqp, dkp, dv)
        gs["q_gain"][l] = s_qg.reshape(8 * GQA_Q_HEADS, HEAD_DIM).sum(axis=0)
        gs["k_gain"][l] = s_kg.reshape(8 * GQA_KV_HEADS, HEAD_DIM).sum(axis=0)
        dx, s_sh1, s_sc1, s_n1 = dh_call(dna, dg, dgates, wt("w_in", l), s["x"], modv[l], _row(small["norm1"][l]), dx1, nlt)
        gw["w_in"][l] = jnp.concatenate(
            [wgrad_call(s["h"], dna, "wgrad_in_na"), wgrad_call(s["h"], dg, "wgrad_in_g"),
             wgrad_call(s["h"], dgates, "wgrad_in_gate")], axis=1)
        gs["norm1"][l] = fold(s_n1).sum(axis=0)
        gs["norm2"][l] = fold(s_n2).sum(axis=0)
        dmod[l] = jnp.stack([s_sh1, s_sc1, s_g1, s_sh2, s_sc2, s_g2], axis=1).sum(axis=2)
        if ex is not None:
            ex.pack_grads(l, {"w_in": gw["w_in"][l]}, [G_IN])

    if ex is not None:
        ex.received[(0, G_IN)] = grads_exchange([ex.partials[(0, G_IN)]])[0]
        gw = None
    else:
        gw = {k: jnp.stack(v) for k, v in gw.items()}
    gs = {k: jnp.stack(v) for k, v in gs.items()}
    gs["final_norm"] = d_final.sum(axis=(0, 1))
    return loss_part, dx, gw, jnp.stack(dmod), gs


N_DEV = 8
N_CHIP = 4
ANY = pl.BlockSpec(memory_space=pl.ANY)


def _place():
    x, y, c = lax.axis_index("x"), lax.axis_index("y"), lax.axis_index("c")
    chips = [(1 - x, y), (x, 1 - y), (1 - x, 1 - y)]
    return x, y, c, chips


def small_all_gather(v, name):
    m_per, n = v.shape
    assert m_per % 8 == 0

    def body(x_ref, out_ref, send_sems, recv_sems, local_sem):
        x, y, c, chips = _place()
        me, sibling = (x, y, c), (x, y, 1 - c)

        def rows(px, py, pc):
            return out_ref.at[pl.ds(pl.multiple_of((4 * px + 2 * py + pc) * m_per, 8), m_per), :]

        def copy(k, block, to, src=None):
            return pltpu.make_async_remote_copy(
                src_ref=rows(*block) if src is None else src, dst_ref=rows(*block),
                send_sem=send_sems.at[k], recv_sem=recv_sems.at[k], device_id=to, device_id_type=MESH)

        mine = pltpu.make_async_copy(x_ref, rows(*me), local_sem)
        mine.start()
        first = [copy(0, me, sibling, src=x_ref)]
        first += [copy(1 + j, me, (*chip, c), src=x_ref) for j, chip in enumerate(chips)]
        for cp in first:
            cp.start()
        passed = [copy(4 + j, (*chip, c), sibling) for j, chip in enumerate(chips)]
        for j, chip in enumerate(chips):
            copy(1 + j, (*chip, c), me).wait_recv()
            passed[j].start()
        copy(0, sibling, me).wait_recv()
        for j, chip in enumerate(chips):
            copy(4 + j, (*chip, 1 - c), me).wait_recv()
        for cp in first + passed:
            cp.wait_send()
        mine.wait()

    out = pl.pallas_call(
        body, name=name,
        out_shape=jax.ShapeDtypeStruct((N_DEV * m_per, n), v.dtype),
        in_specs=[pl.BlockSpec(memory_space=pltpu.VMEM)],
        out_specs=pl.BlockSpec(memory_space=pltpu.VMEM),
        scratch_shapes=[pltpu.SemaphoreType.DMA((7,)), pltpu.SemaphoreType.DMA((7,)), pltpu.SemaphoreType.DMA],
    )(v)
    return out.reshape(N_DEV, m_per, n)


def _ag_copies(in_ref, out_ref, send_sems, recv_sems, base=0):
    x, y, c, chips = _place()
    me, sibling = (x, y, c), (x, y, 1 - c)

    def blk(px, py, half):
        return out_ref.at[2 * px + py, half]

    def copy(k, block, to, src=None):
        return pltpu.make_async_remote_copy(
            src_ref=blk(*block) if src is None else src, dst_ref=blk(*block),
            send_sem=send_sems.at[base + k], recv_sem=recv_sems.at[base + k], device_id=to, device_id_type=MESH)

    first = [copy(j, (x, y, c), (*chip, c), src=in_ref.at[c]) for j, chip in enumerate(chips)]
    passed = [copy(3 + j, (*chip, c), sibling) for j, chip in enumerate(chips)]
    arrivals = [copy(j, (*chip, c), me) for j, chip in enumerate(chips)]
    forwarded = [copy(3 + j, (*chip, 1 - c), me) for j, chip in enumerate(chips)]
    return first, passed, arrivals, forwarded


AG_SEMS = 6
RS_SEMS = 7


def _ag_start(in_refs, out_refs, send_sems, recv_sems):
    for b, (i, o) in enumerate(zip(in_refs, out_refs)):
        for cp in _ag_copies(i, o, send_sems, recv_sems, AG_SEMS * b)[0]:
            cp.start()


def _ag_forward(in_refs, out_refs, send_sems, recv_sems):
    for b, (i, o) in enumerate(zip(in_refs, out_refs)):
        _, passed, arrivals, _ = _ag_copies(i, o, send_sems, recv_sems, AG_SEMS * b)
        for arrived, onward in zip(arrivals, passed):
            arrived.wait_recv()
            onward.start()


def _ag_finish(in_refs, out_refs, send_sems, recv_sems):
    for b, (i, o) in enumerate(zip(in_refs, out_refs)):
        first, passed, _, forwarded = _ag_copies(i, o, send_sems, recv_sems, AG_SEMS * b)
        for cp in forwarded:
            cp.wait_recv()
        for cp in first + passed:
            cp.wait_send()


def place_own_block(gathered, packed):
    chip = 2 * lax.axis_index("x") + lax.axis_index("y")
    return lax.dynamic_update_slice(gathered, packed[None], (chip,) + (0,) * packed.ndim)


def weights_all_gather(packed):
    n = len(packed)

    def body(*refs):
        _ag_start(refs[:n], refs[n:2 * n], refs[2 * n], refs[2 * n + 1])
        _ag_forward(refs[:n], refs[n:2 * n], refs[2 * n], refs[2 * n + 1])
        _ag_finish(refs[:n], refs[n:2 * n], refs[2 * n], refs[2 * n + 1])

    outs = pl.pallas_call(
        body, name="weights_all_gather",
        out_shape=[jax.ShapeDtypeStruct((N_CHIP,) + p.shape, p.dtype) for p in packed],
        in_specs=[ANY] * n, out_specs=[ANY] * n,
        scratch_shapes=[pltpu.SemaphoreType.DMA((AG_SEMS * n,)), pltpu.SemaphoreType.DMA((AG_SEMS * n,))],
    )(*packed)
    return [place_own_block(o, p) for o, p in zip(outs, packed)]


def _rs_copies(gp_ref, rx_ref, send_sems, recv_sems, base=0):
    x, y, c, chips = _place()
    sends = [pltpu.make_async_remote_copy(src_ref=gp_ref.at[2 * x + y, 1 - c], dst_ref=rx_ref.at[0],
                                          send_sem=send_sems.at[base], recv_sem=recv_sems.at[base],
                                          device_id=(x, y, 1 - c), device_id_type=MESH)]
    for r, (px, py) in enumerate(chips):
        for h in range(2):
            sends.append(pltpu.make_async_remote_copy(
                src_ref=gp_ref.at[2 * px + py, h], dst_ref=rx_ref.at[1 + 2 * r + c],
                send_sem=send_sems.at[base + 1 + 2 * r + h], recv_sem=recv_sems.at[base + 1 + 2 * r + c],
                device_id=(px, py, h), device_id_type=MESH))
    arrivals = [pltpu.make_async_remote_copy(src_ref=gp_ref.at[0, 0], dst_ref=rx_ref.at[s],
                                             send_sem=send_sems.at[base + s], recv_sem=recv_sems.at[base + s],
                                             device_id=(x, y, c), device_id_type=MESH) for s in range(RS_SEMS)]
    return sends, arrivals


def _rs_start(gp_refs, rx_refs, send_sems, recv_sems):
    for b, (gp, rx) in enumerate(zip(gp_refs, rx_refs)):
        for cp in _rs_copies(gp, rx, send_sems, recv_sems, RS_SEMS * b)[0]:
            cp.start()


def _rs_finish(gp_refs, rx_refs, send_sems, recv_sems):
    for b, (gp, rx) in enumerate(zip(gp_refs, rx_refs)):
        sends, arrivals = _rs_copies(gp, rx, send_sems, recv_sems, RS_SEMS * b)
        for cp in arrivals:
            cp.wait_recv()
        for cp in sends:
            cp.wait_send()


def _rs_out_shape(gp):
    return jax.ShapeDtypeStruct((RS_SEMS,) + gp.shape[2:], gp.dtype)


def grads_exchange(gps):
    n = len(gps)

    def body(*refs):
        _rs_start(refs[:n], refs[n:2 * n], refs[2 * n], refs[2 * n + 1])
        _rs_finish(refs[:n], refs[n:2 * n], refs[2 * n], refs[2 * n + 1])

    return pl.pallas_call(
        body, name="grads_exchange", out_shape=[_rs_out_shape(g) for g in gps],
        in_specs=[ANY] * n, out_specs=[ANY] * n,
        scratch_shapes=[pltpu.SemaphoreType.DMA((RS_SEMS * n,)), pltpu.SemaphoreType.DMA((RS_SEMS * n,))],
    )(*gps)


def grads_join_halves(ghs):
    n = len(ghs)

    def body(*refs):
        x, y, c, _ = _place()
        cps = [pltpu.make_async_remote_copy(src_ref=refs[b], dst_ref=refs[n + b], send_sem=refs[2 * n].at[b],
                                            recv_sem=refs[2 * n + 1].at[b], device_id=(x, y, 1 - c),
                                            device_id_type=MESH) for b in range(n)]
        for cp in cps:
            cp.start()
        for cp in cps:
            cp.wait()

    others = pl.pallas_call(
        body, name="grads_join_halves", out_shape=[jax.ShapeDtypeStruct(g.shape, g.dtype) for g in ghs],
        in_specs=[ANY] * n, out_specs=[ANY] * n,
        scratch_shapes=[pltpu.SemaphoreType.DMA((n,)), pltpu.SemaphoreType.DMA((n,))],
    )(*ghs)
    c0 = lax.axis_index("c") == 0
    return [jnp.stack([jnp.where(c0, g, o), jnp.where(c0, o, g)], axis=1) for g, o in zip(ghs, others)]


PACK_C = 1024


def sum_partials_call(gp, rx, chip, core):
    _, _, rh, ncol = gp.shape
    n = rx.shape[0]
    tr = _pick_tile(rh, 512, 16)

    def body(where_ref, a_ref, *rest):
        o_ref = rest[-1]
        g = a_ref[0, 0].astype(F32)
        for b_ref in rest[:-1]:
            g = g + b_ref[0].astype(F32)
        o_ref[...] = g

    other = lambda j: pl.BlockSpec((1, tr, ncol), lambda i, w: (j, i, 0))
    return pl.pallas_call(
        body, name="sum_partials",
        grid_spec=pltpu.PrefetchScalarGridSpec(
            num_scalar_prefetch=1, grid=(rh // tr,),
            in_specs=[pl.BlockSpec((1, 1, tr, ncol), lambda i, w: (w[0], w[1], i, 0))] + [other(j) for j in range(n)],
            out_specs=pl.BlockSpec((tr, ncol), lambda i, w: (i, 0))),
        out_shape=jax.ShapeDtypeStruct((rh, ncol), F32),
        compiler_params=_cp(("parallel",)),
    )(jnp.stack([chip, core]).astype(jnp.int32), gp, *([rx] * n))


def _silu(x):
    return x * _sigmoid(x)


def mod_matmul_call(craw, w_mod):
    L, D, N = w_mod.shape

    def body(c_ref, w_ref, o_ref):
        o_ref[0] = _dot(_silu(c_ref[...]).astype(BF16), w_ref[0].astype(BF16))

    return pl.pallas_call(
        body, name="mod_matmul", grid=(L,),
        in_specs=[pl.BlockSpec((16, D), lambda l: (0, 0)), pl.BlockSpec((1, D, N), lambda l: (l, 0, 0))],
        out_specs=pl.BlockSpec((1, 16, N), lambda l: (l, 0, 0)),
        out_shape=jax.ShapeDtypeStruct((L, 16, N), F32),
        compiler_params=_cp(("parallel",), VMEM_BIG),
    )(craw, w_mod)


def mod_bwd_call(craw, dmod16, w_mod):
    L, D, N = w_mod.shape

    def body(c_ref, d_ref, w_ref, gw_ref, ds_ref):
        l = pl.program_id(0)
        d = d_ref[0].astype(BF16)
        gw_ref[0] = _dot_tn(_silu(c_ref[...]).astype(BF16), d)
        part = _dot_nt(d, w_ref[0].astype(BF16))

        @pl.when(l == 0)
        def _():
            ds_ref[...] = part

        @pl.when(l > 0)
        def _():
            ds_ref[...] += part

    return pl.pallas_call(
        body, name="mod_bwd", grid=(L,),
        in_specs=[pl.BlockSpec((16, D), lambda l: (0, 0)), pl.BlockSpec((1, 16, N), lambda l: (l, 0, 0)),
                  pl.BlockSpec((1, D, N), lambda l: (l, 0, 0))],
        out_specs=[pl.BlockSpec((1, D, N), lambda l: (l, 0, 0)), pl.BlockSpec((16, D), lambda l: (0, 0))],
        out_shape=[jax.ShapeDtypeStruct((L, D, N), F32), jax.ShapeDtypeStruct((16, D), F32)],
        compiler_params=_cp(("arbitrary",), VMEM_BIG),
    )(craw, dmod16, w_mod)


def _adamw(w, g, m, v):
    m2 = ADAM_B1 * m + (1.0 - ADAM_B1) * g
    v2 = ADAM_B2 * v + (1.0 - ADAM_B2) * (g * g)
    m_hat = m2 / (1.0 - ADAM_B1 ** ADAM_STEP)
    v_hat = v2 / (1.0 - ADAM_B2 ** ADAM_STEP)
    delta = -ADAM_LR * (m_hat / (jnp.sqrt(v_hat) + ADAM_EPS) + ADAM_WD * w)
    return delta, m2, v2


def adamw_call(w, g, m, v):
    shape = w.shape
    ncol = shape[-1]
    r = math.prod(shape[:-1])
    tr = _pick_tile(r, 512, 8)
    as2d = lambda t: t.reshape(r, ncol)

    def body(w_ref, g_ref, m_ref, v_ref, d_ref, m2_ref, v2_ref):
        d_ref[...], m2_ref[...], v2_ref[...] = _adamw(w_ref[...], g_ref[...], m_ref[...], v_ref[...])

    spec = pl.BlockSpec((tr, ncol), lambda i: (i, 0))
    outs = pl.pallas_call(
        body, name="adamw", grid=(r // tr,), in_specs=[spec] * 4, out_specs=[spec] * 3,
        out_shape=[jax.ShapeDtypeStruct((r, ncol), F32)] * 3,
        compiler_params=_cp(("parallel",)),
    )(as2d(w), as2d(g), as2d(m), as2d(v))
    return [o.reshape(shape) for o in outs]


def adamw_small_call(parts, w, m, v, silu_bwd=False):
    P, R, C = parts.shape

    def body(p_ref, w_ref, m_ref, v_ref, g_ref, d_ref, m2_ref, v2_ref):
        g = p_ref[0]
        for i in range(1, P):
            g = g + p_ref[i]
        wv = w_ref[...]
        if silu_bwd:
            sg = _sigmoid(wv)
            g = g * (sg * (1.0 + wv * (1.0 - sg)))
        g_ref[...] = g
        d_ref[...], m2_ref[...], v2_ref[...] = _adamw(wv, g, m_ref[...], v_ref[...])

    return pl.pallas_call(
        body, name="adamw_small", out_shape=[jax.ShapeDtypeStruct((R, C), F32)] * 4,
    )(parts, w, m, v)


def sum_parts_call(parts):
    P, R, C = parts.shape

    def body(p_ref, o_ref):
        g = p_ref[0]
        for i in range(1, P):
            g = g + p_ref[i]
        o_ref[...] = g

    return pl.pallas_call(body, name="sum_parts", out_shape=jax.ShapeDtypeStruct((R, C), F32))(parts)


BIG = (("w_in", "col"), ("w_pa", "col"), ("w_pb", "col"), ("w_o", "row"), ("w_ffn_in", "col"), ("w_ffn_out", "row"))
KIND = dict(BIG)
GROUPS = (("w_in",), ("w_ffn_in",), ("w_pa", "w_pb", "w_o", "w_ffn_out"))
G_IN = 0
SMALL = ("norm1", "norm2", "final_norm", "na_rpb", "q_gain", "k_gain")


def _group_width(group, shapes):
    return shapes[group[0]][1] if len(group) == 1 else PACK_C


def _pack_shards(shards, group, shapes):
    width = _group_width(group, shapes)
    lead = shards[group[0]].shape[:-2]
    p = jnp.concatenate([shards[n].reshape(lead + (-1, width)) for n in group], axis=-2)
    return p.reshape(lead + (2, p.shape[-2] // 2, width))


def _unpack_shards(packed, group, shapes):
    width = _group_width(group, shapes)
    lead = packed.shape[:-3]
    flat = packed.reshape(lead + (-1, width))
    out, r0 = {}, 0
    for n in group:
        K, N = shapes[n]
        rw = K * N // width
        out[n] = flat[..., r0:r0 + rw, :].reshape(lead + (K, N))
        r0 += rw
    return out


def _whole_from_chips(g, kind):
    _, L, K, N = g.shape
    if kind == "col":
        return g.transpose(1, 2, 0, 3).reshape(L, K, N_CHIP * N)
    return g.transpose(1, 0, 2, 3).reshape(L, N_CHIP * K, N)


def _chips_from_whole(g, kind):
    K, N = g.shape
    if kind == "col":
        return g.reshape(K, N_CHIP, N // N_CHIP).transpose(1, 0, 2)
    return g.reshape(N_CHIP, K // N_CHIP, N)


class Exchange:
    def __init__(self, local_bf16, shard_shapes):
        self.shapes = shard_shapes
        self.packed = [_pack_shards(local_bf16, g, shard_shapes).transpose(1, 0, 2, 3) for g in GROUPS]
        self.received = {}
        self.partials = {}

    def whole(self, gathered, group):
        g = _unpack_shards(gathered.transpose(0, 2, 1, 3, 4), group, self.shapes)
        return {n: _whole_from_chips(g[n], KIND[n]) for n in group}

    def first_needed(self):
        return self.whole(weights_all_gather([self.packed[G_IN][:, :1]])[0], GROUPS[G_IN])

    def rest(self):
        return [p[:, 1:] if gi == G_IN else p for gi, p in enumerate(self.packed)]

    def rest_whole(self, gathered):
        out = [self.whole(place_own_block(g, p), grp) for g, p, grp in zip(gathered, self.rest(), GROUPS)]
        others = {}
        for gi, d in enumerate(out):
            if gi != G_IN:
                others.update(d)
        return out[G_IN], others

    def pack_grads(self, layer, gw, groups):
        for gi in groups:
            shards = {n: _chips_from_whole(gw[n], KIND[n]) for n in GROUPS[gi]}
            self.partials[(layer, gi)] = _pack_shards(shards, GROUPS[gi], self.shapes)


def _pad_rows(v, rows):
    return jnp.pad(v.reshape(-1), (0, rows * PACK_C - v.size)).reshape(rows, PACK_C)


def kernel(x, c, ctx, c_ctx, w_mod, b_mod, norm1, w_in, na_rpb, q_gain, k_gain, w_pa, w_pb, w_o, norm2, w_ffn_in, w_ffn_out, final_norm, loss_target, m_c_ctx, m_w_mod, m_b_mod, m_norm1, m_w_in, m_na_rpb, m_q_gain, m_k_gain, m_w_pa, m_w_pb, m_w_o, m_norm2, m_w_ffn_in, m_w_ffn_out, m_final_norm, v_c_ctx, v_w_mod, v_b_mod, v_norm1, v_w_in, v_na_rpb, v_q_gain, v_k_gain, v_w_pa, v_w_pb, v_w_o, v_norm2, v_w_ffn_in, v_w_ffn_out, v_final_norm):
    W = dict(c_ctx=c_ctx, w_mod=w_mod, b_mod=b_mod, norm1=norm1, w_in=w_in, na_rpb=na_rpb, q_gain=q_gain, k_gain=k_gain,
             w_pa=w_pa, w_pb=w_pb, w_o=w_o, norm2=norm2, w_ffn_in=w_ffn_in, w_ffn_out=w_ffn_out, final_norm=final_norm)
    M = dict(c_ctx=m_c_ctx, w_mod=m_w_mod, b_mod=m_b_mod, norm1=m_norm1, w_in=m_w_in, na_rpb=m_na_rpb, q_gain=m_q_gain,
             k_gain=m_k_gain, w_pa=m_w_pa, w_pb=m_w_pb, w_o=m_w_o, norm2=m_norm2, w_ffn_in=m_w_ffn_in,
             w_ffn_out=m_w_ffn_out, final_norm=m_final_norm)
    V = dict(c_ctx=v_c_ctx, w_mod=v_w_mod, b_mod=v_b_mod, norm1=v_norm1, w_in=v_w_in, na_rpb=v_na_rpb, q_gain=v_q_gain,
             k_gain=v_k_gain, w_pa=v_w_pa, w_pb=v_w_pb, w_o=v_w_o, norm2=v_norm2, w_ffn_in=v_w_ffn_in,
             w_ffn_out=v_w_ffn_out, final_norm=v_final_norm)
    order = ["c_ctx", "w_mod", "b_mod", "norm1", "w_in", "na_rpb", "q_gain", "k_gain", "w_pa", "w_pb", "w_o", "norm2",
             "w_ffn_in", "w_ffn_out", "final_norm"]
    L, D = norm1.shape
    n_lat = x.shape[1]
    ax, ay, ac = lax.axis_index("x"), lax.axis_index("y"), lax.axis_index("c")
    chip = 2 * ax + ay
    dev = 2 * chip + ac

    shard_shapes = {n: W[n].shape[1:] for n, _ in BIG}
    ex = Exchange({n: W[n].astype(BF16) for n, _ in BIG}, shard_shapes)
    w_first = ex.first_needed()

    c_all = small_all_gather(jnp.pad(c, ((0, 7), (0, 0))), "gather_c")[:, 0, :]
    craw = jnp.pad(c_all, ((0, 8), (0, 0))) + jnp.pad(c_ctx[None, :], ((8, 7), (0, 0)))
    ncol_mod = w_mod.shape[2]
    mod_loc = mod_matmul_call(craw, w_mod)
    mod_all = small_all_gather(mod_loc.reshape(L * 16, ncol_mod), "gather_mod")
    mod_all = mod_all[0::2].reshape(N_CHIP, L, 16, ncol_mod).transpose(1, 2, 0, 3).reshape(L, 16, 6 * D)
    mod_all = mod_all + b_mod[:, None, :]
    mod_mine = lax.dynamic_index_in_dim(mod_all, dev, axis=1, keepdims=False)
    modv = jnp.stack([mod_mine, mod_all[:, 8]], axis=1).reshape(L, 2, 6, D)

    xs = jnp.concatenate([x[0], ctx[0]], axis=0)
    small = dict(norm1=norm1, norm2=norm2, na_rpb=na_rpb, q_gain=q_gain, k_gain=k_gain, final_norm=final_norm)
    loss_part, dx, _, dmod, gs = device_fwd_bwd(xs, loss_target[0], modv, w_first, small, n_lat, ex)
    loss = lax.psum(jnp.sum(loss_part), ("x", "y", "c"))
    grad_x = dx[:n_lat][None]

    gh = [jnp.stack([sum_partials_call(ex.partials[(l, gi)], ex.received[(l, gi)], chip, ac) for l in range(L)])
          for gi in range(len(GROUPS))]
    gbig = {}
    for grp, joined in zip(GROUPS, grads_join_halves(gh)):
        gbig.update(_unpack_shards(joined, grp, shard_shapes))

    n_mod_rows = L * 6 * D // PACK_C
    small_flat = jnp.concatenate([gs[n].reshape(-1) for n in SMALL])
    n_small_rows = -(-small_flat.size // (8 * PACK_C)) * 8
    g1 = jnp.concatenate([dmod[:, 0].reshape(n_mod_rows, PACK_C), dmod[:, 1].reshape(n_mod_rows, PACK_C),
                          _pad_rows(small_flat, n_small_rows)], axis=0)
    g1 = small_all_gather(g1, "gather_small_grads")
    dm_lat = g1[:, :n_mod_rows].reshape(N_DEV, L, 6 * D)
    dm_ctx_parts = g1[:, n_mod_rows:2 * n_mod_rows]
    dm_ctx = sum_parts_call(dm_ctx_parts).reshape(L, 1, 6 * D)
    dmod16 = (jnp.pad(dm_lat.transpose(1, 0, 2), ((0, 0), (0, 8), (0, 0)))
              + jnp.pad(dm_ctx, ((0, 0), (8, 7), (0, 0))))
    dmod16_loc = lax.dynamic_slice_in_dim(dmod16, chip * ncol_mod, ncol_mod, axis=2)
    g_wmod, dsilu_part = mod_bwd_call(craw, dmod16_loc, w_mod)
    ds_all = small_all_gather(dsilu_part[8:16], "gather_dsilu")
    ds_parts = ds_all[0::2, 0:1, :]
    ds_parts = jnp.pad(ds_parts, ((0, 0), (0, 7), (0, 0)))

    out_g, out_d, out_m, out_v = {}, {}, {}, {}
    for n, _ in BIG:
        out_g[n] = gbig[n]
        out_d[n], out_m[n], out_v[n] = adamw_call(W[n], gbig[n], M[n], V[n])
    out_g["w_mod"] = g_wmod
    out_d["w_mod"], out_m["w_mod"], out_v["w_mod"] = adamw_call(w_mod, g_wmod, m_w_mod, v_w_mod)

    pack_small = lambda t: _pad_rows(jnp.concatenate([t[n].reshape(-1) for n in SMALL]), n_small_rows)
    res = adamw_small_call(g1[:, 2 * n_mod_rows:], pack_small(W), pack_small(M), pack_small(V))
    off = 0
    for n in SMALL:
        sz = W[n].size
        for dst, r in zip((out_g, out_d, out_m, out_v), res):
            dst[n] = r.reshape(-1)[off:off + sz].reshape(W[n].shape)
        off += sz
    bparts = jnp.concatenate([g1[:, :n_mod_rows], dm_ctx_parts], axis=0)
    as_rows = lambda t: t.reshape(n_mod_rows, PACK_C)
    res = adamw_small_call(bparts, as_rows(b_mod), as_rows(m_b_mod), as_rows(v_b_mod))
    for dst, r in zip((out_g, out_d, out_m, out_v), res):
        dst["b_mod"] = r.reshape(b_mod.shape)
    row8 = lambda t: jnp.pad(t[None, :], ((0, 7), (0, 0)))
    res = adamw_small_call(ds_parts, row8(c_ctx), row8(m_c_ctx), row8(v_c_ctx), silu_bwd=True)
    for dst, r in zip((out_g, out_d, out_m, out_v), res):
        dst["c_ctx"] = r[0]

    return (loss, grad_x, *[out_g[n] for n in order], *[out_d[n] for n in order],
            *[out_m[n] for n in order], *[out_v[n] for n in order])
```
